```python
import jax, jax.numpy as jnp
from jax import lax
import numpy as np

D_MODEL = 1024
BATCH = 32
SEQ = 2048
DEPTH = 2

MEM_LEN = 256
HEAD_DIM = 64
N_SB_HEADS = 12
N_MEM_HEADS = 4
DIL_GROUPS = ((128, 1), (512, 4), (2048, 16))
HEADS_PER_GROUP = 4
N_DIL_HEADS = HEADS_PER_GROUP * len(DIL_GROUPS)
SB_WIDTH = N_SB_HEADS * HEAD_DIM
MEM_WIDTH = N_MEM_HEADS * HEAD_DIM
DIL_WIDTH = N_DIL_HEADS * HEAD_DIM
D_FF = 2816
CONV_WIDTH = 3
Q_BLOCK = 128
N_A_LAYERS = DEPTH // 2
N_B_LAYERS = DEPTH - N_A_LAYERS
EPS = 1e-6
ALIBI_MAX_BIAS = 8.0

kernel_name = "yoco_stickbreaking_dilated_hybrid"


def rmsnorm(x, g):
    xf = x.astype(jnp.float32)
    y = xf * lax.rsqrt(jnp.mean(xf * xf, axis=-1, keepdims=True) + EPS)
    return (y * g.astype(jnp.float32)).astype(x.dtype)


def alibi_slopes(n):
    return 2.0 ** (-ALIBI_MAX_BIAS * jnp.arange(1, n + 1, dtype=jnp.float32) / n)


def _heads(t, n_heads):
    return t.reshape(t.shape[0], t.shape[1], n_heads, HEAD_DIM)


def stick_breaking_attention(q, k, v):
    s_len = q.shape[1]
    scale = HEAD_DIM ** -0.5
    outs = []
    for blk in range(s_len // Q_BLOCK):
        q0 = blk * Q_BLOCK
        k_end = q0 + Q_BLOCK
        z = jnp.einsum('bqhd,bkhd->bhqk', q[:, q0:k_end], k[:, :k_end]).astype(jnp.float32) * scale
        t_pos = q0 + jnp.arange(Q_BLOCK)[:, None]
        s_pos = jnp.arange(k_end)[None, :]
        causal = s_pos < t_pos
        log_stay = jnp.where(causal, -jax.nn.softplus(z), 0.0)
        later = lax.cumsum(log_stay, axis=3, reverse=True) - log_stay
        w = jnp.where(causal, jnp.exp(jax.nn.log_sigmoid(z) + later), 0.0)
        outs.append(jnp.einsum('bhqk,bkhd->bqhd', w.astype(v.dtype), v[:, :k_end]))
    return jnp.concatenate(outs, axis=1)


def dilated_window_attention(q, k, v, slopes, window, dilation):
    b, s_len, n_h, dh = q.shape
    w_sub = window // dilation
    blk = w_sub
    L = s_len // dilation
    nb = -(-L // blk)
    Lp = nb * blk

    def by_residue(t):
        return t.reshape(b, L, dilation, n_h, dh).transpose(0, 2, 1, 3, 4)

    qs, ks, vs = by_residue(q), by_residue(k), by_residue(v)
    qb = jnp.pad(qs, ((0, 0), (0, 0), (0, Lp - L), (0, 0), (0, 0))).reshape(b, dilation, nb, blk, n_h, dh)

    def key_blocks(t):
        tp = jnp.pad(t, ((0, 0), (0, 0), (blk, Lp - L), (0, 0), (0, 0)))
        prev = tp[:, :, :Lp].reshape(b, dilation, nb, blk, n_h, dh)
        cur = tp[:, :, blk:].reshape(b, dilation, nb, blk, n_h, dh)
        return jnp.concatenate([prev, cur], axis=3)

    kb, vb = key_blocks(ks), key_blocks(vs)
    sc = jnp.einsum('brnqhd,brnkhd->brnhqk', qb, kb).astype(jnp.float32) * dh ** -0.5
    n_idx = jnp.arange(nb)[:, None, None]
    i_idx = jnp.arange(blk)[None, :, None]
    j_idx = jnp.arange(2 * blk)[None, None, :]
    delta = i_idx + blk - j_idx
    valid = (delta >= 0) & (delta <= w_sub) & (n_idx * blk - blk + j_idx >= 0)
    bias = -slopes[None, :, None, None] * (delta * dilation).astype(jnp.float32)[:, None]
    sc = jnp.where(valid[:, None], sc + bias, -jnp.inf)
    m = jnp.max(sc, axis=-1, keepdims=True)
    p = jnp.exp(sc - m)
    denom = jnp.sum(p, axis=-1)
    o = jnp.einsum('brnhqk,brnkhd->brnqhd', p.astype(v.dtype), vb).astype(jnp.float32)
    o = o / jnp.moveaxis(denom, 3, 4)[..., None]
    lse = jnp.moveaxis(m[..., 0] + jnp.log(denom), 3, 4)

    def back(t):
        t = t.reshape((b, dilation, Lp) + t.shape[4:])[:, :, :L]
        t = jnp.swapaxes(t, 1, 2)
        return t.reshape((b, s_len) + t.shape[3:])

    return back(o).astype(q.dtype), back(lse)


def memory_branch(q_mem, mem, norm_mem, w_mem_kv):
    k_m, v_m = jnp.split(rmsnorm(mem, norm_mem) @ w_mem_kv, 2, axis=-1)
    q, k, v = _heads(q_mem, N_MEM_HEADS), _heads(k_m, N_MEM_HEADS), _heads(v_m, N_MEM_HEADS)
    sc = jnp.einsum('bqhd,bkhd->bhqk', q, k).astype(jnp.float32) * HEAD_DIM ** -0.5
    p = jax.nn.softmax(sc, axis=-1)
    o = jnp.einsum('bhqk,bkhd->bqhd', p.astype(v.dtype), v)
    return o.reshape(q_mem.shape)


def conv_ffn(x, w_up, w_conv, w_down):
    s_len = x.shape[1]
    u = x @ w_up
    up = jnp.pad(u, ((0, 0), (CONV_WIDTH - 1, 0), (0, 0)))
    c = w_conv[0] * up[:, 0:s_len]
    for j in range(1, CONV_WIDTH):
        c = c + w_conv[j] * up[:, j:j + s_len]
    a, g = jnp.split(c, 2, axis=-1)
    return (jax.nn.silu(g) * a) @ w_down


def self_decoder_layer(x, mem, norm_attn, w_in, w_out, norm_mem, w_mem_kv, norm_ffn, ffn_up, ffn_conv, ffn_down):
    b, s_len, _ = x.shape
    proj = rmsnorm(x, norm_attn) @ w_in
    q_sb, k_sb, v_sb, q_mem = jnp.split(proj, [SB_WIDTH, 2 * SB_WIDTH, 3 * SB_WIDTH], axis=-1)
    o_sb = stick_breaking_attention(_heads(q_sb, N_SB_HEADS), _heads(k_sb, N_SB_HEADS), _heads(v_sb, N_SB_HEADS))
    o_mem = memory_branch(q_mem, mem, norm_mem, w_mem_kv)
    x = x + jnp.concatenate([o_sb.reshape(b, s_len, SB_WIDTH), o_mem], axis=-1) @ w_out
    return x + conv_ffn(rmsnorm(x, norm_ffn), ffn_up, ffn_conv, ffn_down)


def cross_decoder_layer(x, k_sh, v_sh, mem, norm_attn, w_in, w_out, norm_mem, w_mem_kv, norm_ffn, ffn_up, ffn_conv, ffn_down):
    b, s_len, _ = x.shape
    proj = rmsnorm(x, norm_attn) @ w_in
    q_dil, q_mem = jnp.split(proj, [DIL_WIDTH], axis=-1)
    q_dil = _heads(q_dil, N_DIL_HEADS)
    slopes = alibi_slopes(N_DIL_HEADS)
    outs, lses = [], []
    for g, (window, dilation) in enumerate(DIL_GROUPS):
        hs = slice(g * HEADS_PER_GROUP, (g + 1) * HEADS_PER_GROUP)
        o_g, lse_g = dilated_window_attention(q_dil[:, :, hs], k_sh[:, :, hs], v_sh[:, :, hs], slopes[hs], window, dilation)
        outs.append(o_g)
        lses.append(lse_g)
    alpha = jax.nn.softmax(jnp.stack(lses, axis=0), axis=0)
    o_dil = jnp.concatenate([o * alpha[g][..., None].astype(o.dtype) for g, o in enumerate(outs)], axis=2)
    o_mem = memory_branch(q_mem, mem, norm_mem, w_mem_kv)
    x = x + jnp.concatenate([o_dil.reshape(b, s_len, DIL_WIDTH), o_mem], axis=-1) @ w_out
    return x + conv_ffn(rmsnorm(x, norm_ffn), ffn_up, ffn_conv, ffn_down)


def _fwd_setup_inputs(seed: int = 0) -> dict:
    key = jax.random.key(seed)
    ks = jax.random.split(key, 24)

    def w(k, shape, fan_in):
        return jax.random.normal(k, shape, jnp.float32) * fan_in ** -0.5

    def gain(k, shape):
        return 1.0 + 0.02 * jax.random.normal(k, shape, jnp.float32)

    na, nb = N_A_LAYERS, N_B_LAYERS
    return {
        'x': jax.random.normal(ks[0], (BATCH, SEQ, D_MODEL), jnp.float32),
        'mem': jax.random.normal(ks[1], (BATCH, MEM_LEN, D_MODEL), jnp.float32),
        'a_norm_attn': gain(ks[2], (na, D_MODEL)),
        'a_w_in': w(ks[3], (na, D_MODEL, 3 * SB_WIDTH + MEM_WIDTH), D_MODEL),
        'a_w_out': w(ks[4], (na, SB_WIDTH + MEM_WIDTH, D_MODEL), SB_WIDTH + MEM_WIDTH),
        'a_norm_mem': gain(ks[5], (na, D_MODEL)),
        'a_w_mem_kv': w(ks[6], (na, D_MODEL, 2 * MEM_WIDTH), D_MODEL),
        'a_norm_ffn': gain(ks[7], (na, D_MODEL)),
        'a_ffn_up': w(ks[8], (na, D_MODEL, 2 * D_FF), D_MODEL),
        'a_ffn_conv': w(ks[9], (na, CONV_WIDTH, 2 * D_FF), CONV_WIDTH),
        'a_ffn_down': w(ks[10], (na, D_FF, D_MODEL), D_FF),
        'kv_norm': gain(ks[11], (D_MODEL,)),
        'w_kv_shared': w(ks[12], (D_MODEL, 2 * DIL_WIDTH), D_MODEL),
        'b_norm_attn': gain(ks[13], (nb, D_MODEL)),
        'b_w_in': w(ks[14], (nb, D_MODEL, DIL_WIDTH + MEM_WIDTH), D_MODEL),
        'b_w_out': w(ks[15], (nb, DIL_WIDTH + MEM_WIDTH, D_MODEL), DIL_WIDTH + MEM_WIDTH),
        'b_norm_mem': gain(ks[16], (nb, D_MODEL)),
        'b_w_mem_kv': w(ks[17], (nb, D_MODEL, 2 * MEM_WIDTH), D_MODEL),
        'b_norm_ffn': gain(ks[18], (nb, D_MODEL)),
        'b_ffn_up': w(ks[19], (nb, D_MODEL, 2 * D_FF), D_MODEL),
        'b_ffn_conv': w(ks[20], (nb, CONV_WIDTH, 2 * D_FF), CONV_WIDTH),
        'b_ffn_down': w(ks[21], (nb, D_FF, D_MODEL), D_FF),
        'final_norm': gain(ks[22], (D_MODEL,)),
    }


def _fwd_reference(x, mem, a_norm_attn, a_w_in, a_w_out, a_norm_mem, a_w_mem_kv, a_norm_ffn, a_ffn_up, a_ffn_conv, a_ffn_down,
              kv_norm, w_kv_shared, b_norm_attn, b_w_in, b_w_out, b_norm_mem, b_w_mem_kv, b_norm_ffn, b_ffn_up, b_ffn_conv,
              b_ffn_down, final_norm):
    b, s_len, _ = x.shape
    h = x
    k_sh = None
    v_sh = None
    for layer in range(DEPTH):
        if layer < N_A_LAYERS:
            i = layer
            h = self_decoder_layer(h, mem, a_norm_attn[i], a_w_in[i], a_w_out[i], a_norm_mem[i], a_w_mem_kv[i],
                                   a_norm_ffn[i], a_ffn_up[i], a_ffn_conv[i], a_ffn_down[i])
            if layer == N_A_LAYERS - 1:
                k_flat, v_flat = jnp.split(rmsnorm(h, kv_norm) @ w_kv_shared, 2, axis=-1)
                k_sh = _heads(k_flat, N_DIL_HEADS)
                v_sh = _heads(v_flat, N_DIL_HEADS)
        else:
            j = layer - N_A_LAYERS
            h = cross_decoder_layer(h, k_sh, v_sh, mem, b_norm_attn[j], b_w_in[j], b_w_out[j], b_norm_mem[j],
                                    b_w_mem_kv[j], b_norm_ffn[j], b_ffn_up[j], b_ffn_conv[j], b_ffn_down[j])
    return rmsnorm(h, final_norm)


import jax as _jax
import jax.numpy as _jnp

TWIN_FORMAT = 'train_step'
FWD_PARAMS = ['x', 'mem', 'a_norm_attn', 'a_w_in', 'a_w_out', 'a_norm_mem', 'a_w_mem_kv', 'a_norm_ffn', 'a_ffn_up', 'a_ffn_conv', 'a_ffn_down', 'kv_norm', 'w_kv_shared', 'b_norm_attn', 'b_w_in', 'b_w_out', 'b_norm_mem', 'b_w_mem_kv', 'b_norm_ffn', 'b_ffn_up', 'b_ffn_conv', 'b_ffn_down', 'final_norm']
TWIN_WEIGHTS = ['a_norm_attn', 'a_w_in', 'a_w_out', 'a_norm_mem', 'a_w_mem_kv', 'a_norm_ffn', 'a_ffn_up', 'a_ffn_conv', 'a_ffn_down', 'kv_norm', 'w_kv_shared', 'b_norm_attn', 'b_w_in', 'b_w_out', 'b_norm_mem', 'b_w_mem_kv', 'b_norm_ffn', 'b_ffn_up', 'b_ffn_conv', 'b_ffn_down', 'final_norm']
TWIN_DIFF_INPUT = 'x'
TWIN_INPUTS = ['x', 'mem', 'a_norm_attn', 'a_w_in', 'a_w_out', 'a_norm_mem', 'a_w_mem_kv', 'a_norm_ffn', 'a_ffn_up', 'a_ffn_conv', 'a_ffn_down', 'kv_norm', 'w_kv_shared', 'b_norm_attn', 'b_w_in', 'b_w_out', 'b_norm_mem', 'b_w_mem_kv', 'b_norm_ffn', 'b_ffn_up', 'b_ffn_conv', 'b_ffn_down', 'final_norm', 'loss_target', 'm_a_norm_attn', 'm_a_w_in', 'm_a_w_out', 'm_a_norm_mem', 'm_a_w_mem_kv', 'm_a_norm_ffn', 'm_a_ffn_up', 'm_a_ffn_conv', 'm_a_ffn_down', 'm_kv_norm', 'm_w_kv_shared', 'm_b_norm_attn', 'm_b_w_in', 'm_b_w_out', 'm_b_norm_mem', 'm_b_w_mem_kv', 'm_b_norm_ffn', 'm_b_ffn_up', 'm_b_ffn_conv', 'm_b_ffn_down', 'm_final_norm', 'v_a_norm_attn', 'v_a_w_in', 'v_a_w_out', 'v_a_norm_mem', 'v_a_w_mem_kv', 'v_a_norm_ffn', 'v_a_ffn_up', 'v_a_ffn_conv', 'v_a_ffn_down', 'v_kv_norm', 'v_w_kv_shared', 'v_b_norm_attn', 'v_b_w_in', 'v_b_w_out', 'v_b_norm_mem', 'v_b_w_mem_kv', 'v_b_norm_ffn', 'v_b_ffn_up', 'v_b_ffn_conv', 'v_b_ffn_down', 'v_final_norm']
TWIN_OUTPUTS = ['loss', 'grad_x', 'grad_a_norm_attn', 'grad_a_w_in', 'grad_a_w_out', 'grad_a_norm_mem', 'grad_a_w_mem_kv', 'grad_a_norm_ffn', 'grad_a_ffn_up', 'grad_a_ffn_conv', 'grad_a_ffn_down', 'grad_kv_norm', 'grad_w_kv_shared', 'grad_b_norm_attn', 'grad_b_w_in', 'grad_b_w_out', 'grad_b_norm_mem', 'grad_b_w_mem_kv', 'grad_b_norm_ffn', 'grad_b_ffn_up', 'grad_b_ffn_conv', 'grad_b_ffn_down', 'grad_final_norm', 'delta_a_norm_attn', 'delta_a_w_in', 'delta_a_w_out', 'delta_a_norm_mem', 'delta_a_w_mem_kv', 'delta_a_norm_ffn', 'delta_a_ffn_up', 'delta_a_ffn_conv', 'delta_a_ffn_down', 'delta_kv_norm', 'delta_w_kv_shared', 'delta_b_norm_attn', 'delta_b_w_in', 'delta_b_w_out', 'delta_b_norm_mem', 'delta_b_w_mem_kv', 'delta_b_norm_ffn', 'delta_b_ffn_up', 'delta_b_ffn_conv', 'delta_b_ffn_down', 'delta_final_norm', 'new_m_a_norm_attn', 'new_m_a_w_in', 'new_m_a_w_out', 'new_m_a_norm_mem', 'new_m_a_w_mem_kv', 'new_m_a_norm_ffn', 'new_m_a_ffn_up', 'new_m_a_ffn_conv', 'new_m_a_ffn_down', 'new_m_kv_norm', 'new_m_w_kv_shared', 'new_m_b_norm_attn', 'new_m_b_w_in', 'new_m_b_w_out', 'new_m_b_norm_mem', 'new_m_b_w_mem_kv', 'new_m_b_norm_ffn', 'new_m_b_ffn_up', 'new_m_b_ffn_conv', 'new_m_b_ffn_down', 'new_m_final_norm', 'new_v_a_norm_attn', 'new_v_a_w_in', 'new_v_a_w_out', 'new_v_a_norm_mem', 'new_v_a_w_mem_kv', 'new_v_a_norm_ffn', 'new_v_a_ffn_up', 'new_v_a_ffn_conv', 'new_v_a_ffn_down', 'new_v_kv_norm', 'new_v_w_kv_shared', 'new_v_b_norm_attn', 'new_v_b_w_in', 'new_v_b_w_out', 'new_v_b_norm_mem', 'new_v_b_w_mem_kv', 'new_v_b_norm_ffn', 'new_v_b_ffn_up', 'new_v_b_ffn_conv', 'new_v_b_ffn_down', 'new_v_final_norm']
TWIN_LEAF_KINDS = {'loss': 'loss', 'grad_x': 'grad_x', 'grad_a_norm_attn': 'grad_w', 'grad_a_w_in': 'grad_w', 'grad_a_w_out': 'grad_w', 'grad_a_norm_mem': 'grad_w', 'grad_a_w_mem_kv': 'grad_w', 'grad_a_norm_ffn': 'grad_w', 'grad_a_ffn_up': 'grad_w', 'grad_a_ffn_conv': 'grad_w', 'grad_a_ffn_down': 'grad_w', 'grad_kv_norm': 'grad_w', 'grad_w_kv_shared': 'grad_w', 'grad_b_norm_attn': 'grad_w', 'grad_b_w_in': 'grad_w', 'grad_b_w_out': 'grad_w', 'grad_b_norm_mem': 'grad_w', 'grad_b_w_mem_kv': 'grad_w', 'grad_b_norm_ffn': 'grad_w', 'grad_b_ffn_up': 'grad_w', 'grad_b_ffn_conv': 'grad_w', 'grad_b_ffn_down': 'grad_w', 'grad_final_norm': 'grad_w', 'delta_a_norm_attn': 'delta_w', 'delta_a_w_in': 'delta_w', 'delta_a_w_out': 'delta_w', 'delta_a_norm_mem': 'delta_w', 'delta_a_w_mem_kv': 'delta_w', 'delta_a_norm_ffn': 'delta_w', 'delta_a_ffn_up': 'delta_w', 'delta_a_ffn_conv': 'delta_w', 'delta_a_ffn_down': 'delta_w', 'delta_kv_norm': 'delta_w', 'delta_w_kv_shared': 'delta_w', 'delta_b_norm_attn': 'delta_w', 'delta_b_w_in': 'delta_w', 'delta_b_w_out': 'delta_w', 'delta_b_norm_mem': 'delta_w', 'delta_b_w_mem_kv': 'delta_w', 'delta_b_norm_ffn': 'delta_w', 'delta_b_ffn_up': 'delta_w', 'delta_b_ffn_conv': 'delta_w', 'delta_b_ffn_down': 'delta_w', 'delta_final_norm': 'delta_w', 'new_m_a_norm_attn': 'new_m', 'new_m_a_w_in': 'new_m', 'new_m_a_w_out': 'new_m', 'new_m_a_norm_mem': 'new_m', 'new_m_a_w_mem_kv': 'new_m', 'new_m_a_norm_ffn': 'new_m', 'new_m_a_ffn_up': 'new_m', 'new_m_a_ffn_conv': 'new_m', 'new_m_a_ffn_down': 'new_m', 'new_m_kv_norm': 'new_m', 'new_m_w_kv_shared': 'new_m', 'new_m_b_norm_attn': 'new_m', 'new_m_b_w_in': 'new_m', 'new_m_b_w_out': 'new_m', 'new_m_b_norm_mem': 'new_m', 'new_m_b_w_mem_kv': 'new_m', 'new_m_b_norm_ffn': 'new_m', 'new_m_b_ffn_up': 'new_m', 'new_m_b_ffn_conv': 'new_m', 'new_m_b_ffn_down': 'new_m', 'new_m_final_norm': 'new_m', 'new_v_a_norm_attn': 'new_v', 'new_v_a_w_in': 'new_v', 'new_v_a_w_out': 'new_v', 'new_v_a_norm_mem': 'new_v', 'new_v_a_w_mem_kv': 'new_v', 'new_v_a_norm_ffn': 'new_v', 'new_v_a_ffn_up': 'new_v', 'new_v_a_ffn_conv': 'new_v', 'new_v_a_ffn_down': 'new_v', 'new_v_kv_norm': 'new_v', 'new_v_w_kv_shared': 'new_v', 'new_v_b_norm_attn': 'new_v', 'new_v_b_w_in': 'new_v', 'new_v_b_w_out': 'new_v', 'new_v_b_norm_mem': 'new_v', 'new_v_b_w_mem_kv': 'new_v', 'new_v_b_norm_ffn': 'new_v', 'new_v_b_ffn_up': 'new_v', 'new_v_b_ffn_conv': 'new_v', 'new_v_b_ffn_down': 'new_v', 'new_v_final_norm': 'new_v'}


def _forward(args):
    return _fwd_reference(*[args[k] for k in FWD_PARAMS])


def _output_shape():
    out = _jax.eval_shape(lambda: _forward(_fwd_setup_inputs(0)))
    return out.shape, out.dtype

N_MICROBATCH = 1
ADAM_LR = 0.001
ADAM_B1 = 0.9
ADAM_B2 = 0.999
ADAM_EPS = 1e-08
ADAM_WD = 0.01
ADAM_STEP = 10
PER_EXAMPLE_BATCH_AXIS = {'x': 0, 'mem': 0, 'loss_target': 0}
SHARED_INPUTS = []
_WEIGHT_DTYPES = {'a_norm_attn': _jnp.float32, 'a_w_in': _jnp.float32, 'a_w_out': _jnp.float32, 'a_norm_mem': _jnp.float32, 'a_w_mem_kv': _jnp.float32, 'a_norm_ffn': _jnp.float32, 'a_ffn_up': _jnp.float32, 'a_ffn_conv': _jnp.float32, 'a_ffn_down': _jnp.float32, 'kv_norm': _jnp.float32, 'w_kv_shared': _jnp.float32, 'b_norm_attn': _jnp.float32, 'b_w_in': _jnp.float32, 'b_w_out': _jnp.float32, 'b_norm_mem': _jnp.float32, 'b_w_mem_kv': _jnp.float32, 'b_norm_ffn': _jnp.float32, 'b_ffn_up': _jnp.float32, 'b_ffn_conv': _jnp.float32, 'b_ffn_down': _jnp.float32, 'final_norm': _jnp.float32}
MOMENT_SCALE = {'a_norm_attn': 1.928501e-01, 'a_w_in': 1.131350e-01, 'a_w_out': 1.498486e-01, 'a_norm_mem': 2.115805e-02, 'a_w_mem_kv': 2.833549e-02, 'a_norm_ffn': 1.907648e-01, 'a_ffn_up': 7.916228e-02, 'a_ffn_conv': 8.348345e-02, 'a_ffn_down': 1.292847e-01, 'kv_norm': 4.664560e-02, 'w_kv_shared': 3.677912e-02, 'b_norm_attn': 3.031442e-02, 'b_w_in': 3.060545e-02, 'b_w_out': 3.668341e-02, 'b_norm_mem': 1.672364e-02, 'b_w_mem_kv': 2.229250e-02, 'b_norm_ffn': 1.517375e-01, 'b_ffn_up': 6.482535e-02, 'b_ffn_conv': 6.441490e-02, 'b_ffn_down': 1.065162e-01, 'final_norm': 6.385089e+01}


def _to_microbatches(a, axis):
    t = _jnp.moveaxis(a, axis, 0)
    t = t.reshape((N_MICROBATCH, t.shape[0] // N_MICROBATCH) + t.shape[1:])
    return _jnp.moveaxis(t, 1, axis + 1)


def setup_inputs(seed: int = 0) -> dict:
    inp = _fwd_setup_inputs(seed)
    key = _jax.random.fold_in(_jax.random.key(seed), 7919)
    shape, _ = _output_shape()
    out = dict(inp)
    out["loss_target"] = _jax.random.normal(_jax.random.fold_in(key, 0), shape, _jnp.float32)
    for i, name in enumerate(TWIN_WEIGHTS):
        w = inp[name].astype(_jnp.float32)
        if MOMENT_SCALE is None:
            s = _jnp.sqrt(_jnp.mean(_jnp.square(w)) + 1e-30)
        else:
            s = MOMENT_SCALE[name]
        km, kv = _jax.random.split(_jax.random.fold_in(key, i + 1))
        out[name] = w
        out["m_" + name] = s * _jax.random.normal(km, w.shape, _jnp.float32)
        out["v_" + name] = (s * s) * _jax.random.uniform(kv, w.shape, _jnp.float32, 0.5, 1.5)
    if N_MICROBATCH > 1:
        for name, axis in PER_EXAMPLE_BATCH_AXIS.items():
            out[name] = _to_microbatches(out[name], axis)
    return {'x': out['x'], 'mem': out['mem'], 'a_norm_attn': out['a_norm_attn'], 'a_w_in': out['a_w_in'], 'a_w_out': out['a_w_out'], 'a_norm_mem': out['a_norm_mem'], 'a_w_mem_kv': out['a_w_mem_kv'], 'a_norm_ffn': out['a_norm_ffn'], 'a_ffn_up': out['a_ffn_up'], 'a_ffn_conv': out['a_ffn_conv'], 'a_ffn_down': out['a_ffn_down'], 'kv_norm': out['kv_norm'], 'w_kv_shared': out['w_kv_shared'], 'b_norm_attn': out['b_norm_attn'], 'b_w_in': out['b_w_in'], 'b_w_out': out['b_w_out'], 'b_norm_mem': out['b_norm_mem'], 'b_w_mem_kv': out['b_w_mem_kv'], 'b_norm_ffn': out['b_norm_ffn'], 'b_ffn_up': out['b_ffn_up'], 'b_ffn_conv': out['b_ffn_conv'], 'b_ffn_down': out['b_ffn_down'], 'final_norm': out['final_norm'], 'loss_target': out['loss_target'], 'm_a_norm_attn': out['m_a_norm_attn'], 'm_a_w_in': out['m_a_w_in'], 'm_a_w_out': out['m_a_w_out'], 'm_a_norm_mem': out['m_a_norm_mem'], 'm_a_w_mem_kv': out['m_a_w_mem_kv'], 'm_a_norm_ffn': out['m_a_norm_ffn'], 'm_a_ffn_up': out['m_a_ffn_up'], 'm_a_ffn_conv': out['m_a_ffn_conv'], 'm_a_ffn_down': out['m_a_ffn_down'], 'm_kv_norm': out['m_kv_norm'], 'm_w_kv_shared': out['m_w_kv_shared'], 'm_b_norm_attn': out['m_b_norm_attn'], 'm_b_w_in': out['m_b_w_in'], 'm_b_w_out': out['m_b_w_out'], 'm_b_norm_mem': out['m_b_norm_mem'], 'm_b_w_mem_kv': out['m_b_w_mem_kv'], 'm_b_norm_ffn': out['m_b_norm_ffn'], 'm_b_ffn_up': out['m_b_ffn_up'], 'm_b_ffn_conv': out['m_b_ffn_conv'], 'm_b_ffn_down': out['m_b_ffn_down'], 'm_final_norm': out['m_final_norm'], 'v_a_norm_attn': out['v_a_norm_attn'], 'v_a_w_in': out['v_a_w_in'], 'v_a_w_out': out['v_a_w_out'], 'v_a_norm_mem': out['v_a_norm_mem'], 'v_a_w_mem_kv': out['v_a_w_mem_kv'], 'v_a_norm_ffn': out['v_a_norm_ffn'], 'v_a_ffn_up': out['v_a_ffn_up'], 'v_a_ffn_conv': out['v_a_ffn_conv'], 'v_a_ffn_down': out['v_a_ffn_down'], 'v_kv_norm': out['v_kv_norm'], 'v_w_kv_shared': out['v_w_kv_shared'], 'v_b_norm_attn': out['v_b_norm_attn'], 'v_b_w_in': out['v_b_w_in'], 'v_b_w_out': out['v_b_w_out'], 'v_b_norm_mem': out['v_b_norm_mem'], 'v_b_w_mem_kv': out['v_b_w_mem_kv'], 'v_b_norm_ffn': out['v_b_norm_ffn'], 'v_b_ffn_up': out['v_b_ffn_up'], 'v_b_ffn_conv': out['v_b_ffn_conv'], 'v_b_ffn_down': out['v_b_ffn_down'], 'v_final_norm': out['v_final_norm']}


def _loss(weights, diff, rest, loss_target):
    with _jax.named_scope("forward"):
        args = {**rest, TWIN_DIFF_INPUT: diff, **{k: w.astype(_WEIGHT_DTYPES[k]) for k, w in weights.items()}}
        y = _forward(args)
    with _jax.named_scope("loss_head"):
        err = _jnp.square(y.astype(_jnp.float32) - loss_target)
        return 0.5 * _jnp.sum(_jnp.mean(err, axis=-1)) if err.ndim else 0.5 * err


def _adamw(w, g, m, v):
    m = ADAM_B1 * m + (1.0 - ADAM_B1) * g
    v = ADAM_B2 * v + (1.0 - ADAM_B2) * _jnp.square(g)
    m_hat = m / (1.0 - ADAM_B1 ** ADAM_STEP)
    v_hat = v / (1.0 - ADAM_B2 ** ADAM_STEP)
    delta = -ADAM_LR * (m_hat / (_jnp.sqrt(v_hat) + ADAM_EPS) + ADAM_WD * w)
    return delta, m, v


def reference(x, mem, a_norm_attn, a_w_in, a_w_out, a_norm_mem, a_w_mem_kv, a_norm_ffn, a_ffn_up, a_ffn_conv, a_ffn_down, kv_norm, w_kv_shared, b_norm_attn, b_w_in, b_w_out, b_norm_mem, b_w_mem_kv, b_norm_ffn, b_ffn_up, b_ffn_conv, b_ffn_down, final_norm, loss_target, m_a_norm_attn, m_a_w_in, m_a_w_out, m_a_norm_mem, m_a_w_mem_kv, m_a_norm_ffn, m_a_ffn_up, m_a_ffn_conv, m_a_ffn_down, m_kv_norm, m_w_kv_shared, m_b_norm_attn, m_b_w_in, m_b_w_out, m_b_norm_mem, m_b_w_mem_kv, m_b_norm_ffn, m_b_ffn_up, m_b_ffn_conv, m_b_ffn_down, m_final_norm, v_a_norm_attn, v_a_w_in, v_a_w_out, v_a_norm_mem, v_a_w_mem_kv, v_a_norm_ffn, v_a_ffn_up, v_a_ffn_conv, v_a_ffn_down, v_kv_norm, v_w_kv_shared, v_b_norm_attn, v_b_w_in, v_b_w_out, v_b_norm_mem, v_b_w_mem_kv, v_b_norm_ffn, v_b_ffn_up, v_b_ffn_conv, v_b_ffn_down, v_final_norm):
    given = dict(x=x, mem=mem, a_norm_attn=a_norm_attn, a_w_in=a_w_in, a_w_out=a_w_out, a_norm_mem=a_norm_mem, a_w_mem_kv=a_w_mem_kv, a_norm_ffn=a_norm_ffn, a_ffn_up=a_ffn_up, a_ffn_conv=a_ffn_conv, a_ffn_down=a_ffn_down, kv_norm=kv_norm, w_kv_shared=w_kv_shared, b_norm_attn=b_norm_attn, b_w_in=b_w_in, b_w_out=b_w_out, b_norm_mem=b_norm_mem, b_w_mem_kv=b_w_mem_kv, b_norm_ffn=b_norm_ffn, b_ffn_up=b_ffn_up, b_ffn_conv=b_ffn_conv, b_ffn_down=b_ffn_down, final_norm=final_norm, loss_target=loss_target, m_a_norm_attn=m_a_norm_attn, m_a_w_in=m_a_w_in, m_a_w_out=m_a_w_out, m_a_norm_mem=m_a_norm_mem, m_a_w_mem_kv=m_a_w_mem_kv, m_a_norm_ffn=m_a_norm_ffn, m_a_ffn_up=m_a_ffn_up, m_a_ffn_conv=m_a_ffn_conv, m_a_ffn_down=m_a_ffn_down, m_kv_norm=m_kv_norm, m_w_kv_shared=m_w_kv_shared, m_b_norm_attn=m_b_norm_attn, m_b_w_in=m_b_w_in, m_b_w_out=m_b_w_out, m_b_norm_mem=m_b_norm_mem, m_b_w_mem_kv=m_b_w_mem_kv, m_b_norm_ffn=m_b_norm_ffn, m_b_ffn_up=m_b_ffn_up, m_b_ffn_conv=m_b_ffn_conv, m_b_ffn_down=m_b_ffn_down, m_final_norm=m_final_norm, v_a_norm_attn=v_a_norm_attn, v_a_w_in=v_a_w_in, v_a_w_out=v_a_w_out, v_a_norm_mem=v_a_norm_mem, v_a_w_mem_kv=v_a_w_mem_kv, v_a_norm_ffn=v_a_norm_ffn, v_a_ffn_up=v_a_ffn_up, v_a_ffn_conv=v_a_ffn_conv, v_a_ffn_down=v_a_ffn_down, v_kv_norm=v_kv_norm, v_w_kv_shared=v_w_kv_shared, v_b_norm_attn=v_b_norm_attn, v_b_w_in=v_b_w_in, v_b_w_out=v_b_w_out, v_b_norm_mem=v_b_norm_mem, v_b_w_mem_kv=v_b_w_mem_kv, v_b_norm_ffn=v_b_norm_ffn, v_b_ffn_up=v_b_ffn_up, v_b_ffn_conv=v_b_ffn_conv, v_b_ffn_down=v_b_ffn_down, v_final_norm=v_final_norm)
    weights = {n: given[n] for n in TWIN_WEIGHTS}
    shared = {n: given[n] for n in SHARED_INPUTS}
    per_example = {n: given[n] for n in ['x', 'mem']}
    grad_fn = _jax.value_and_grad(_loss, argnums=(0, 1))

    def one_microbatch(ex, loss_target):
        ex = dict(ex)
        diff = ex.pop(TWIN_DIFF_INPUT)
        return grad_fn(weights, diff, {**shared, **ex}, loss_target)

    if N_MICROBATCH == 1:
        loss, (grad_w, grad_x) = one_microbatch(per_example, given["loss_target"])
    else:
        def body(carry, xs):
            loss_sum, grad_sum = carry
            l_k, (gw_k, gx_k) = one_microbatch(xs[0], xs[1])
            with _jax.named_scope("update"):
                return (loss_sum + l_k, _jax.tree.map(_jnp.add, grad_sum, gw_k)), gx_k

        init = (_jnp.zeros((), _jnp.float32), _jax.tree.map(_jnp.zeros_like, weights))
        (loss, grad_w), grad_x = _jax.lax.scan(body, init, (per_example, given["loss_target"]))
    with _jax.named_scope("update"):
        delta_w, new_m, new_v = {}, {}, {}
        for n in TWIN_WEIGHTS:
            delta_w[n], new_m[n], new_v[n] = _adamw(weights[n], grad_w[n], given["m_" + n], given["v_" + n])
    return (loss, grad_x, *[grad_w[n] for n in TWIN_WEIGHTS], *[delta_w[n] for n in TWIN_WEIGHTS],
            *[new_m[n] for n in TWIN_WEIGHTS], *[new_v[n] for n in TWIN_WEIGHTS])
```

```python
import functools
import math

import jax
import jax.numpy as jnp
from jax import lax
from jax.experimental import pallas as pl
from jax.experimental.pallas import tpu as pltpu

F32 = jnp.float32
BF = jnp.bfloat16
MESH = pl.DeviceIdType.MESH

HEAD_DIM = 64
LANES = 128
SB_WIDTH = 12 * HEAD_DIM
MEM_WIDTH = 4 * HEAD_DIM
DIL_WIDTH = 12 * HEAD_DIM
MEM_LEN = 256
DIL_GROUPS = ((128, 1), (512, 4), (2048, 16))
QB = 128
EPS = 1e-6
SCALE = HEAD_DIM ** -0.5
NEG = -1e30
ALIBI = tuple(2.0 ** (-8.0 * i / 12) for i in range(1, 13))
N_CHIPS = 4
N_DEV = 8

ADAM_LR, ADAM_B1, ADAM_B2, ADAM_EPS, ADAM_WD, ADAM_STEP = 0.001, 0.9, 0.999, 1e-08, 0.01, 10

VMEM_LIMIT = 48 * 1024 * 1024


def _mo(v, m):
    return v if isinstance(v, int) else pl.multiple_of(v, m)


def _pick(n, prefs):
    for t in prefs:
        if n % t == 0:
            return t
    return n


def _params(*sem):
    return pltpu.CompilerParams(dimension_semantics=sem, vmem_limit_bytes=VMEM_LIMIT)


def matmul(a, b, *, ta=False, tb=False, out_dtype=BF, add=None, shard_cols=0, name):
    m, k = (a.shape[1], a.shape[0]) if ta else a.shape
    n = b.shape[0] if tb else b.shape[1]
    tm = _pick(m, (512, 1408, 256, 128))
    tk = _pick(k, (512, 1408, 256, 128))
    tn = shard_cols if shard_cols else _pick(n, (1024, 1408, 512, 256, 128))
    nk = k // tk
    dims = (((0,) if ta else (1,), (1,) if tb else (0,)), ((), ()))

    def body(*refs):
        if add is None:
            a_ref, b_ref, o_ref, acc_ref = refs
        else:
            a_ref, b_ref, add_ref, o_ref, acc_ref = refs
        kk = pl.program_id(2)

        @pl.when(kk == 0)
        def _():
            acc_ref[...] = jnp.zeros_like(acc_ref)

        acc_ref[...] += lax.dot_general(a_ref[...].astype(BF), b_ref[...].astype(BF), dims,
                                        preferred_element_type=F32)

        @pl.when(kk == nk - 1)
        def _():
            r = acc_ref[...]
            if add is not None:
                r = r + add_ref[...]
            o_ref[...] = r.astype(o_ref.dtype)

    a_spec = pl.BlockSpec((tk, tm), lambda i, j, q: (q, i)) if ta else pl.BlockSpec((tm, tk), lambda i, j, q: (i, q))
    b_spec = pl.BlockSpec((tn, tk), lambda i, j, q: (j, q)) if tb else pl.BlockSpec((tk, tn), lambda i, j, q: (q, j))
    in_specs = [a_spec, b_spec]
    args = [a, b]
    if add is not None:
        in_specs.append(pl.BlockSpec((tm, tn), lambda i, j, q: (i, j)))
        args.append(add)
    if shard_cols:
        out_shape = jax.ShapeDtypeStruct((N_CHIPS, m, shard_cols), out_dtype)
        out_spec = pl.BlockSpec((None, tm, tn), lambda i, j, q: (j, i, 0))
    else:
        out_shape = jax.ShapeDtypeStruct((m, n), out_dtype)
        out_spec = pl.BlockSpec((tm, tn), lambda i, j, q: (i, j))
    return pl.pallas_call(
        body, name=name, grid=(m // tm, n // tn, nk), in_specs=in_specs, out_specs=out_spec, out_shape=out_shape,
        scratch_shapes=[pltpu.VMEM((tm, tn), F32)],
        compiler_params=_params("parallel", "parallel", "arbitrary"),
    )(*args)


def rmsnorm_fwd(x, g, *, name):
    t, d = x.shape
    tr = _pick(t, (512, 256))

    def body(x_ref, g_ref, o_ref):
        xv = x_ref[...]
        r = lax.rsqrt(jnp.mean(xv * xv, axis=-1, keepdims=True) + EPS)
        o_ref[...] = ((xv * r) * g_ref[...]).astype(o_ref.dtype)

    return pl.pallas_call(
        body, name=name, grid=(t // tr,),
        in_specs=[pl.BlockSpec((tr, d), lambda i: (i, 0)), pl.BlockSpec((1, d), lambda i: (0, 0))],
        out_specs=pl.BlockSpec((tr, d), lambda i: (i, 0)), out_shape=jax.ShapeDtypeStruct((t, d), BF),
        compiler_params=_params("parallel"),
    )(x, g)


def rmsnorm_bwd(x, g, dn, dres, *, name):
    t, d = x.shape
    tr = _pick(t, (512, 256))
    want_dx = dres is not None

    def body(*refs):
        if want_dx:
            x_ref, g_ref, dn_ref, dres_ref, dx_ref, dg_ref = refs
        else:
            x_ref, g_ref, dn_ref, dg_ref = refs

        @pl.when(pl.program_id(0) == 0)
        def _():
            dg_ref[...] = jnp.zeros_like(dg_ref)

        xv = x_ref[...]
        r = lax.rsqrt(jnp.mean(xv * xv, axis=-1, keepdims=True) + EPS)
        xn = xv * r
        dnv = dn_ref[...].astype(F32)
        dg_ref[...] += jnp.sum(dnv * xn, axis=0, keepdims=True)
        if want_dx:
            dyg = dnv * g_ref[...]
            cm = jnp.mean(dyg * xn, axis=-1, keepdims=True)
            dx_ref[...] = dres_ref[...] + r * (dyg - xn * cm)

    row = pl.BlockSpec((tr, d), lambda i: (i, 0))
    vec = pl.BlockSpec((1, d), lambda i: (0, 0))
    if want_dx:
        return pl.pallas_call(
            body, name=name, grid=(t // tr,), in_specs=[row, vec, row, row], out_specs=[row, vec],
            out_shape=[jax.ShapeDtypeStruct((t, d), F32), jax.ShapeDtypeStruct((1, d), F32)],
            compiler_params=_params("arbitrary"),
        )(x, g, dn, dres)
    return None, pl.pallas_call(
        body, name=name, grid=(t // tr,), in_specs=[row, vec, row], out_specs=vec,
        out_shape=jax.ShapeDtypeStruct((1, d), F32), compiler_params=_params("arbitrary"),
    )(x, g, dn)


def final_loss(x, g, target, *, name):
    t, d = x.shape
    tr = _pick(t, (512, 256))

    def body(x_ref, g_ref, t_ref, dx_ref, dg_ref, lv_ref):
        @pl.when(pl.program_id(0) == 0)
        def _():
            dg_ref[...] = jnp.zeros_like(dg_ref)
            lv_ref[...] = jnp.zeros_like(lv_ref)

        xv = x_ref[...]
        r = lax.rsqrt(jnp.mean(xv * xv, axis=-1, keepdims=True) + EPS)
        xn = xv * r
        err = xn * g_ref[...] - t_ref[...]
        lv_ref[...] += jnp.sum(err * err, axis=0, keepdims=True)
        dy = err * (1.0 / d)
        dg_ref[...] += jnp.sum(dy * xn, axis=0, keepdims=True)
        dyg = dy * g_ref[...]
        cm = jnp.mean(dyg * xn, axis=-1, keepdims=True)
        dx_ref[...] = r * (dyg - xn * cm)

    row = pl.BlockSpec((tr, d), lambda i: (i, 0))
    vec = pl.BlockSpec((1, d), lambda i: (0, 0))
    return pl.pallas_call(
        body, name=name, grid=(t // tr,), in_specs=[row, vec, row], out_specs=[row, vec, vec],
        out_shape=[jax.ShapeDtypeStruct((t, d), F32), jax.ShapeDtypeStruct((1, d), F32), jax.ShapeDtypeStruct((1, d), F32)],
        compiler_params=_params("arbitrary"),
    )(x, g, target)


CONV_ROWS = 256
HALO = 16


def _conv_taps(ext, w, rows):
    s0 = ext[HALO:HALO + rows]
    s1 = pltpu.roll(ext, 1, 0)[HALO:HALO + rows]
    s2 = pltpu.roll(ext, 2, 0)[HALO:HALO + rows]
    return (w[0:1] * s2 + w[1:2] * s1) + w[2:3] * s0, s0, s1, s2


def conv_silu_fwd(u, w, bl, *, name):
    t, f2 = u.shape
    f = f2 // 2
    s = t // bl
    tc = _pick(f, (256, 128))
    nf = f // tc
    nch = s // CONV_ROWS

    def body(ua_ref, ug_ref, wa_ref, wg_ref, h_ref):
        wa, wg = wa_ref[...], wg_ref[...]

        def chunk(ci, carry):
            r0 = pl.multiple_of(ci * CONV_ROWS, CONV_ROWS)
            ps = pl.multiple_of(jnp.maximum(r0 - HALO, 0), HALO)
            keep = (ci > 0).astype(F32)

            def conv(ref, wv):
                ext = jnp.concatenate([ref[0, pl.ds(ps, HALO), :] * keep, ref[0, pl.ds(r0, CONV_ROWS), :]], axis=0)
                return _conv_taps(ext, wv, CONV_ROWS)[0]

            ca, cg = conv(ua_ref, wa), conv(ug_ref, wg)
            sg = 1.0 / (1.0 + jnp.exp(-cg))
            h_ref[0, pl.ds(r0, CONV_ROWS), :] = ((cg * sg) * ca).astype(h_ref.dtype)
            return carry

        lax.fori_loop(0, nch, chunk, 0)

    u3 = u.reshape(bl, s, f2)
    blk = lambda off: pl.BlockSpec((1, s, tc), lambda j, b: (b, 0, j + off))
    wblk = lambda off: pl.BlockSpec((3, tc), lambda j, b: (0, j + off))
    h = pl.pallas_call(
        body, name=name, grid=(nf, bl), in_specs=[blk(0), blk(nf), wblk(0), wblk(nf)],
        out_specs=pl.BlockSpec((1, s, tc), lambda j, b: (b, 0, j)), out_shape=jax.ShapeDtypeStruct((bl, s, f), BF),
        compiler_params=_params("parallel", "parallel"),
    )(u3, u3, w, w)
    return h.reshape(t, f)


def conv_silu_bwd(u, w, dh, bl, *, name):
    t, f2 = u.shape
    f = f2 // 2
    s = t // bl
    tc = _pick(f, (256, 128))
    nf = f // tc
    nch = s // CONV_ROWS
    ext_rows = CONV_ROWS + HALO

    def body(ua_ref, ug_ref, wa_ref, wg_ref, dh_ref, dua_ref, dug_ref, dwa_ref, dwg_ref):
        wa, wg = wa_ref[...], wg_ref[...]

        @pl.when(pl.program_id(1) == 0)
        def _():
            dwa_ref[...] = jnp.zeros_like(dwa_ref)
            dwg_ref[...] = jnp.zeros_like(dwg_ref)

        def chunk(ci, carry):
            r0 = pl.multiple_of(ci * CONV_ROWS, CONV_ROWS)
            ps = pl.multiple_of(jnp.maximum(r0 - HALO, 0), HALO)
            ns = pl.multiple_of(jnp.minimum(r0 + CONV_ROWS, s - HALO), HALO)
            keep_p = (ci > 0).astype(F32)
            keep_n = (ci < nch - 1).astype(F32)

            def ext_of(ref):
                return jnp.concatenate([ref[0, pl.ds(ps, HALO), :] * keep_p, ref[0, pl.ds(r0, CONV_ROWS), :],
                                        ref[0, pl.ds(ns, HALO), :] * keep_n], axis=0)

            ca, a0, a1, a2 = _conv_taps(ext_of(ua_ref), wa, ext_rows)
            cg, g0, g1, g2 = _conv_taps(ext_of(ug_ref), wg, ext_rows)
            dhe = jnp.concatenate([dh_ref[0, pl.ds(r0, CONV_ROWS), :].astype(F32),
                                   dh_ref[0, pl.ds(ns, HALO), :].astype(F32) * keep_n], axis=0)
            sg = 1.0 / (1.0 + jnp.exp(-cg))
            dca = dhe * (cg * sg)
            dcg = dhe * ca * (sg * (1.0 + cg * (1.0 - sg)))

            def back(dc, wv):
                n1 = pltpu.roll(dc, ext_rows - 1, 0)
                n2 = pltpu.roll(dc, ext_rows - 2, 0)
                return ((wv[2:3] * dc + wv[1:2] * n1) + wv[0:1] * n2)[:CONV_ROWS]

            dua_ref[0, pl.ds(r0, CONV_ROWS), :] = back(dca, wa).astype(dua_ref.dtype)
            dug_ref[0, pl.ds(r0, CONV_ROWS), :] = back(dcg, wg).astype(dug_ref.dtype)

            def wsum(dc, taps):
                d0 = dc[:CONV_ROWS]
                return [jnp.sum(d0 * tp[:CONV_ROWS], axis=0, keepdims=True) for tp in taps]

            sa = wsum(dca, (a2, a1, a0))
            sgs = wsum(dcg, (g2, g1, g0))
            for j in range(3):
                dwa_ref[j:j + 1, :] += sa[j]
                dwg_ref[j:j + 1, :] += sgs[j]
            return carry

        lax.fori_loop(0, nch, chunk, 0)

    u3 = u.reshape(bl, s, f2)
    dh3 = dh.reshape(bl, s, f)
    blk = lambda off: pl.BlockSpec((1, s, tc), lambda j, b: (b, 0, j + off))
    wblk = lambda off: pl.BlockSpec((3, tc), lambda j, b: (0, j + off))
    dua, dug, dwa, dwg = pl.pallas_call(
        body, name=name, grid=(nf, bl), in_specs=[blk(0), blk(nf), wblk(0), wblk(nf), blk(0)],
        out_specs=[blk(0), blk(0), wblk(0), wblk(0)],
        out_shape=[jax.ShapeDtypeStruct((bl, s, f), BF), jax.ShapeDtypeStruct((bl, s, f), BF),
                   jax.ShapeDtypeStruct((3, f), F32), jax.ShapeDtypeStruct((3, f), F32)],
        compiler_params=_params("parallel", "arbitrary"),
    )(u3, u3, w, w, dh3)
    du = jnp.concatenate([dua.reshape(t, f), dug.reshape(t, f)], axis=1)
    return du, jnp.concatenate([dwa, dwg], axis=1)


def _lane_masks(rows):
    lane = lax.broadcasted_iota(jnp.int32, (rows, LANES), 1)
    return lane < HEAD_DIM, lane >= HEAD_DIM


def _nt(a, b):
    return lax.dot_general(a, b, (((1,), (1,)), ((), ())), preferred_element_type=F32)


def _tn(a, b):
    return lax.dot_general(a, b, (((0,), (0,)), ((), ())), preferred_element_type=F32)


def _nn(a, b):
    return jnp.dot(a, b, preferred_element_type=F32)


def _suffix_ones():
    j = lax.broadcasted_iota(jnp.int32, (2 * QB, QB), 0) % QB
    s = lax.broadcasted_iota(jnp.int32, (2 * QB, QB), 1)
    return (j >= s).astype(BF)


def _suffix_sum(v, uu):
    hi = v.astype(BF)
    lo = (v - hi.astype(F32)).astype(BF)
    return _nn(jnp.concatenate([hi, lo], axis=1), uu)


def _softplus(z):
    return jnp.maximum(z, 0.0) + jnp.log(1.0 + jnp.exp(-jnp.abs(z)))


def sb_fwd(proj, bl, *, name):
    t, width = proj.shape
    s = t // bl
    npair = SB_WIDTH // LANES
    nq = s // QB

    def body(q_ref, k_ref, v_ref, o_ref, of_ref):
        i = pl.program_id(2)
        uu = _suffix_ones()
        ma, mb = _lane_masks(QB)
        q = q_ref[0]
        row = lax.broadcasted_iota(jnp.int32, (QB, QB), 0)
        col = lax.broadcasted_iota(jnp.int32, (QB, QB), 1)
        causal = col < row

        def tile(j, qh, carry, acc, diag):
            start = pl.multiple_of(j * QB, QB)
            kb = k_ref[0, pl.ds(start, QB), :]
            vb = v_ref[0, pl.ds(start, QB), :]
            z = _nt(qh, kb) * SCALE
            ls = -_softplus(z)
            if diag:
                ls = jnp.where(causal, ls, 0.0)
            cin = _suffix_sum(ls, uu)
            w = jnp.exp(z + cin + carry)
            if diag:
                w = jnp.where(causal, w, 0.0)
            return carry + cin[:, 0:1], acc + _nn(w.astype(BF), vb)

        outs = []
        for m in (ma, mb):
            qh = jnp.where(m, q, jnp.zeros_like(q))
            carry, acc = tile(i, qh, jnp.zeros((QB, 1), F32), jnp.zeros((QB, LANES), F32), True)

            def step(n, c, qh=qh):
                return tile(i - 1 - n, qh, c[0], c[1], False)

            carry, acc = lax.fori_loop(0, i, step, (carry, acc))
            outs.append(acc)
        both = jnp.where(ma, outs[0], outs[1])
        o_ref[0] = both.astype(o_ref.dtype)
        of_ref[0] = both

    p3 = proj.reshape(bl, s, width)
    qblk = pl.BlockSpec((1, QB, LANES), lambda b, p, i: (b, i, p))
    o, of = pl.pallas_call(
        body, name=name, grid=(bl, npair, nq),
        in_specs=[qblk, pl.BlockSpec((1, s, LANES), lambda b, p, i: (b, 0, npair + p)),
                  pl.BlockSpec((1, s, LANES), lambda b, p, i: (b, 0, 2 * npair + p))],
        out_specs=[qblk, qblk],
        out_shape=[jax.ShapeDtypeStruct((bl, s, SB_WIDTH), BF), jax.ShapeDtypeStruct((bl, s, SB_WIDTH), F32)],
        compiler_params=_params("parallel", "parallel", "parallel"),
    )(p3, p3, p3)
    return o.reshape(t, SB_WIDTH), of.reshape(t, SB_WIDTH)


def sb_bwd(proj, o, dcat, bl, *, name):
    t, width = proj.shape
    s = t // bl
    npair = SB_WIDTH // LANES
    nq = s // QB

    def body(q_ref, k_ref, v_ref, o_ref, do_ref, dq_ref, dk_ref, dv_ref):
        i = pl.program_id(2)

        @pl.when(i == 0)
        def _():
            dk_ref[...] = jnp.zeros_like(dk_ref)
            dv_ref[...] = jnp.zeros_like(dv_ref)

        uu = _suffix_ones()
        ma, mb = _lane_masks(QB)
        q = q_ref[0]
        do = do_ref[0]
        prod = do.astype(F32) * o_ref[0].astype(F32)
        row = lax.broadcasted_iota(jnp.int32, (QB, QB), 0)
        col = lax.broadcasted_iota(jnp.int32, (QB, QB), 1)
        causal = col < row

        def tile(j, qh, doh, dsum, c, diag):
            carry_c, carry_s, dq = c
            start = pl.multiple_of(j * QB, QB)
            kb = k_ref[0, pl.ds(start, QB), :]
            vb = v_ref[0, pl.ds(start, QB), :]
            z = _nt(qh, kb) * SCALE
            ls = -_softplus(z)
            if diag:
                ls = jnp.where(causal, ls, 0.0)
            cin = _suffix_sum(ls, uu)
            w = jnp.exp(z + cin + carry_c)
            if diag:
                w = jnp.where(causal, w, 0.0)
            wb = w.astype(BF)
            da = wb.astype(F32) * _nt(doh, vb)
            sin = _suffix_sum(da, uu)
            pre = dsum - (sin - da + carry_s)
            dz = da - jnp.exp(z + ls) * pre
            if diag:
                dz = jnp.where(causal, dz, 0.0)
            dzs = (dz * SCALE).astype(BF)
            dk_ref[0, pl.ds(start, QB), :] += _tn(dzs, qh)
            dv_ref[0, pl.ds(start, QB), :] += _tn(wb, doh)
            return carry_c + cin[:, 0:1], carry_s + sin[:, 0:1], dq + _nn(dzs, kb)

        dqs = []
        for m in (ma, mb):
            qh = jnp.where(m, q, jnp.zeros_like(q))
            doh = jnp.where(m, do, jnp.zeros_like(do))
            dsum = jnp.sum(jnp.where(m, prod, 0.0), axis=1, keepdims=True)
            zero = jnp.zeros((QB, 1), F32)
            c = tile(i, qh, doh, dsum, (zero, zero, jnp.zeros((QB, LANES), F32)), True)

            def step(n, c, qh=qh, doh=doh, dsum=dsum):
                return tile(i - 1 - n, qh, doh, dsum, c, False)

            c = lax.fori_loop(0, i, step, c)
            dqs.append(c[2])
        dq_ref[0] = jnp.where(ma, dqs[0], dqs[1]).astype(dq_ref.dtype)

    p3 = proj.reshape(bl, s, width)
    o3 = o.reshape(bl, s, SB_WIDTH)
    d3 = dcat.reshape(bl, s, dcat.shape[1])
    qblk = pl.BlockSpec((1, QB, LANES), lambda b, p, i: (b, i, p))
    full = pl.BlockSpec((1, s, LANES), lambda b, p, i: (b, 0, p))
    dq, dk, dv = pl.pallas_call(
        body, name=name, grid=(bl, npair, nq),
        in_specs=[qblk, pl.BlockSpec((1, s, LANES), lambda b, p, i: (b, 0, npair + p)),
                  pl.BlockSpec((1, s, LANES), lambda b, p, i: (b, 0, 2 * npair + p)), qblk, qblk],
        out_specs=[qblk, full, full],
        out_shape=[jax.ShapeDtypeStruct((bl, s, SB_WIDTH), BF), jax.ShapeDtypeStruct((bl, s, SB_WIDTH), F32),
                   jax.ShapeDtypeStruct((bl, s, SB_WIDTH), F32)],
        compiler_params=_params("parallel", "parallel", "arbitrary"),
    )(p3, p3, p3, o3, d3)
    return dq.reshape(t, SB_WIDTH), dk.reshape(t, SB_WIDTH), dv.reshape(t, SB_WIDTH)


MEM_TQ = 512


def mem_fwd(proj, kvm, bl, *, name):
    t, width = proj.shape
    s = t // bl
    qoff = (width - MEM_WIDTH) // LANES
    npair = MEM_WIDTH // LANES

    def body(q_ref, k_ref, v_ref, o_ref):
        ma, mb = _lane_masks(MEM_TQ)
        q = q_ref[...]
        outs = []
        for m in (ma, mb):
            qh = jnp.where(m, q, jnp.zeros_like(q))
            sc = _nt(qh, k_ref[...]) * SCALE
            p = jnp.exp(sc - jnp.max(sc, axis=-1, keepdims=True))
            p = p / jnp.sum(p, axis=-1, keepdims=True)
            outs.append(_nn(p.astype(BF), v_ref[...]))
        o_ref[...] = jnp.where(ma, outs[0], outs[1]).astype(o_ref.dtype)

    nt = s // MEM_TQ
    return pl.pallas_call(
        body, name=name, grid=(bl, npair, nt),
        in_specs=[pl.BlockSpec((MEM_TQ, LANES), lambda b, p, i: (b * nt + i, qoff + p)),
                  pl.BlockSpec((MEM_LEN, LANES), lambda b, p, i: (b, p)),
                  pl.BlockSpec((MEM_LEN, LANES), lambda b, p, i: (b, npair + p))],
        out_specs=pl.BlockSpec((MEM_TQ, LANES), lambda b, p, i: (b * nt + i, p)),
        out_shape=jax.ShapeDtypeStruct((t, MEM_WIDTH), BF),
        compiler_params=_params("parallel", "parallel", "parallel"),
    )(proj, kvm, kvm)


def mem_bwd(proj, kvm, dcat, bl, *, name):
    t, width = proj.shape
    s = t // bl
    qoff = (width - MEM_WIDTH) // LANES
    doff = (dcat.shape[1] - MEM_WIDTH) // LANES
    npair = MEM_WIDTH // LANES
    nt = s // MEM_TQ

    def body(q_ref, k_ref, v_ref, do_ref, dq_ref, dk_ref, dv_ref):
        @pl.when(pl.program_id(2) == 0)
        def _():
            dk_ref[...] = jnp.zeros_like(dk_ref)
            dv_ref[...] = jnp.zeros_like(dv_ref)

        ma, mb = _lane_masks(MEM_TQ)
        q = q_ref[...]
        do = do_ref[...]
        kb, vb = k_ref[...], v_ref[...]
        dqs = []
        for m in (ma, mb):
            qh = jnp.where(m, q, jnp.zeros_like(q))
            doh = jnp.where(m, do, jnp.zeros_like(do))
            sc = _nt(qh, kb) * SCALE
            p = jnp.exp(sc - jnp.max(sc, axis=-1, keepdims=True))
            p = p / jnp.sum(p, axis=-1, keepdims=True)
            dp = _nt(doh, vb)
            ds = p * (dp - jnp.sum(p * dp, axis=-1, keepdims=True))
            dss = (ds * SCALE).astype(BF)
            dk_ref[...] += _tn(dss, qh)
            dv_ref[...] += _tn(p.astype(BF), doh)
            dqs.append(_nn(dss, kb))
        dq_ref[...] = jnp.where(ma, dqs[0], dqs[1]).astype(dq_ref.dtype)

    kblk = pl.BlockSpec((MEM_LEN, LANES), lambda b, p, i: (b, p))
    dq, dk, dv = pl.pallas_call(
        body, name=name, grid=(bl, npair, nt),
        in_specs=[pl.BlockSpec((MEM_TQ, LANES), lambda b, p, i: (b * nt + i, qoff + p)), kblk,
                  pl.BlockSpec((MEM_LEN, LANES), lambda b, p, i: (b, npair + p)),
                  pl.BlockSpec((MEM_TQ, LANES), lambda b, p, i: (b * nt + i, doff + p))],
        out_specs=[pl.BlockSpec((MEM_TQ, LANES), lambda b, p, i: (b * nt + i, p)), kblk, kblk],
        out_shape=[jax.ShapeDtypeStruct((t, MEM_WIDTH), BF), jax.ShapeDtypeStruct((bl * MEM_LEN, MEM_WIDTH), F32),
                   jax.ShapeDtypeStruct((bl * MEM_LEN, MEM_WIDTH), F32)],
        compiler_params=_params("parallel", "parallel", "arbitrary"),
    )(proj, kvm, kvm, dcat)
    return dq, jnp.concatenate([dk, dv], axis=1).astype(BF)


def _dil_scores(qh, kb, slope_d, n_keys):
    i = lax.broadcasted_iota(jnp.int32, (QB, n_keys), 0)
    j = lax.broadcasted_iota(jnp.int32, (QB, n_keys), 1)
    delta = i + (n_keys - QB) - j
    valid = (delta >= 0) & (delta <= QB)
    sc = _nt(qh, kb) * SCALE - slope_d * delta.astype(F32)
    return jnp.where(valid, sc, NEG)


def _dil_slopes(g, dil):
    p = pl.program_id(2)
    sa = jnp.where(p == 0, ALIBI[4 * g] * dil, ALIBI[4 * g + 2] * dil).astype(F32)
    sb = jnp.where(p == 0, ALIBI[4 * g + 1] * dil, ALIBI[4 * g + 3] * dil).astype(F32)
    return sa, sb


def dil_fwd(projb, kv, g, bl, *, name):
    _, dil = DIL_GROUPS[g]
    t, wq = projb.shape
    wk = kv.shape[1]
    s = t // bl
    ln = s // dil
    nb = ln // QB
    gw = 4 * HEAD_DIM
    cq, ck, co = wq // LANES, wk // LANES, gw // LANES

    def body(q_ref, k_ref, v_ref, o_ref, lse_ref):
        sa, sb = _dil_slopes(g, dil)
        ma, mb = _lane_masks(QB)

        def block(n, n_keys):
            q0 = _mo(n * QB, QB)
            k0 = _mo(q0 - (n_keys - QB), QB)
            q = q_ref[0, pl.ds(q0, QB), :]
            kb = k_ref[0, pl.ds(k0, n_keys), :]
            vb = v_ref[0, pl.ds(k0, n_keys), :]
            outs, lses = [], []
            for m, sl in ((ma, sa), (mb, sb)):
                qh = jnp.where(m, q, jnp.zeros_like(q))
                sc = _dil_scores(qh, kb, sl, n_keys)
                mx = jnp.max(sc, axis=-1, keepdims=True)
                p = jnp.exp(sc - mx)
                den = jnp.sum(p, axis=-1, keepdims=True)
                outs.append(_nn(p.astype(BF), vb) / den)
                lses.append(mx + jnp.log(den))
            o_ref[0, pl.ds(q0, QB), :] = jnp.where(ma, outs[0], outs[1])
            lse_ref[0, pl.ds(q0, QB), :] = jnp.where(ma, lses[0], lses[1])

        block(0, QB)
        if nb > 1:
            def step(n, c):
                block(n, 2 * QB)
                return c
            lax.fori_loop(1, nb, step, 0)

    q3 = projb.reshape(bl, ln, dil * wq)
    k3 = kv.reshape(bl, ln, dil * wk)
    oblk = pl.BlockSpec((1, ln, LANES), lambda b, r, p: (b, 0, r * co + p))
    o, lse = pl.pallas_call(
        body, name=name, grid=(bl, dil, co),
        in_specs=[pl.BlockSpec((1, ln, LANES), lambda b, r, p: (b, 0, r * cq + g * co + p)),
                  pl.BlockSpec((1, ln, LANES), lambda b, r, p: (b, 0, r * ck + g * co + p)),
                  pl.BlockSpec((1, ln, LANES), lambda b, r, p: (b, 0, r * ck + ck // 2 + g * co + p))],
        out_specs=[oblk, oblk],
        out_shape=[jax.ShapeDtypeStruct((bl, ln, dil * gw), F32), jax.ShapeDtypeStruct((bl, ln, dil * gw), F32)],
        compiler_params=_params("parallel", "parallel", "parallel"),
    )(q3, k3, k3)
    return o.reshape(t, gw), lse.reshape(t, gw)


def dil_bwd(projb, kv, lse, dog, dshift, g, bl, *, name):
    _, dil = DIL_GROUPS[g]
    t, wq = projb.shape
    wk = kv.shape[1]
    s = t // bl
    ln = s // dil
    nb = ln // QB
    gw = 4 * HEAD_DIM
    cq, ck, co, cd = wq // LANES, wk // LANES, gw // LANES, DIL_WIDTH // LANES

    def body(q_ref, k_ref, v_ref, lse_ref, do_ref, sh_ref, dq_ref, dk_ref, dv_ref):
        sa, sb = _dil_slopes(g, dil)
        ma, mb = _lane_masks(QB)
        dk_ref[...] = jnp.zeros_like(dk_ref)
        dv_ref[...] = jnp.zeros_like(dv_ref)

        def block(n, n_keys):
            q0 = _mo(n * QB, QB)
            k0 = _mo(q0 - (n_keys - QB), QB)
            q = q_ref[0, pl.ds(q0, QB), :]
            do = do_ref[0, pl.ds(q0, QB), :]
            lse_b = lse_ref[0, pl.ds(q0, QB), :]
            sh_b = sh_ref[0, pl.ds(q0, QB), :]
            kb = k_ref[0, pl.ds(k0, n_keys), :]
            vb = v_ref[0, pl.ds(k0, n_keys), :]
            dqs = []
            for m, sl, c0 in ((ma, sa, 0), (mb, sb, HEAD_DIM)):
                qh = jnp.where(m, q, jnp.zeros_like(q))
                doh = jnp.where(m, do, jnp.zeros_like(do))
                sc = _dil_scores(qh, kb, sl, n_keys)
                p = jnp.exp(sc - lse_b[:, c0:c0 + 1])
                ds = p * (_nt(doh, vb) - sh_b[:, c0:c0 + 1])
                dss = (ds * SCALE).astype(BF)
                dk_ref[0, pl.ds(k0, n_keys), :] += _tn(dss, qh)
                dv_ref[0, pl.ds(k0, n_keys), :] += _tn(p.astype(BF), doh)
                dqs.append(_nn(dss, kb))
            dq_ref[0, pl.ds(q0, QB), :] = jnp.where(ma, dqs[0], dqs[1]).astype(dq_ref.dtype)

        block(0, QB)
        if nb > 1:
            def step(n, c):
                block(n, 2 * QB)
                return c
            lax.fori_loop(1, nb, step, 0)

    q3 = projb.reshape(bl, ln, dil * wq)
    k3 = kv.reshape(bl, ln, dil * wk)
    l3 = lse.reshape(bl, ln, dil * gw)
    d3 = dog.reshape(bl, ln, dil * DIL_WIDTH)
    s3 = dshift.reshape(bl, ln, dil * DIL_WIDTH)
    oblk = pl.BlockSpec((1, ln, LANES), lambda b, r, p: (b, 0, r * co + p))
    dblk = pl.BlockSpec((1, ln, LANES), lambda b, r, p: (b, 0, r * cd + g * co + p))
    dq, dk, dv = pl.pallas_call(
        body, name=name, grid=(bl, dil, co),
        in_specs=[pl.BlockSpec((1, ln, LANES), lambda b, r, p: (b, 0, r * cq + g * co + p)),
                  pl.BlockSpec((1, ln, LANES), lambda b, r, p: (b, 0, r * ck + g * co + p)),
                  pl.BlockSpec((1, ln, LANES), lambda b, r, p: (b, 0, r * ck + ck // 2 + g * co + p)),
                  oblk, dblk, dblk],
        out_specs=[oblk, oblk, oblk],
        out_shape=[jax.ShapeDtypeStruct((bl, ln, dil * gw), BF), jax.ShapeDtypeStruct((bl, ln, dil * gw), F32),
                   jax.ShapeDtypeStruct((bl, ln, dil * gw), F32)],
        compiler_params=_params("parallel", "parallel", "parallel"),
    )(q3, k3, k3, l3, d3, s3)
    return dq.reshape(t, gw), dk.reshape(t, gw), dv.reshape(t, gw)


def _group_weights(lses):
    mx = jnp.maximum(jnp.maximum(lses[0], lses[1]), lses[2])
    es = [jnp.exp(l - mx) for l in lses]
    tot = es[0] + es[1] + es[2]
    return [e / tot for e in es]


def dil_combine_fwd(os_, lses, *, name):
    t, gw = os_[0].shape
    tr = _pick(t, (512, 256))

    def body(o0, o1, o2, l0, l1, l2, out_ref):
        al = _group_weights([l0[...], l1[...], l2[...]])
        for g, o_ref in enumerate((o0, o1, o2)):
            out_ref[:, g * gw:(g + 1) * gw] = (o_ref[...] * al[g]).astype(out_ref.dtype)

    blk = pl.BlockSpec((tr, gw), lambda i: (i, 0))
    return pl.pallas_call(
        body, name=name, grid=(t // tr,), in_specs=[blk] * 6,
        out_specs=pl.BlockSpec((tr, 3 * gw), lambda i: (i, 0)), out_shape=jax.ShapeDtypeStruct((t, 3 * gw), BF),
        compiler_params=_params("parallel"),
    )(*os_, *lses)


def dil_combine_bwd(os_, lses, dcat, *, name):
    t, gw = os_[0].shape
    tr = _pick(t, (512, 256))

    def head_sum(v):
        parts = []
        for c in range(gw // LANES):
            blk = v[:, c * LANES:(c + 1) * LANES]
            ma, _ = _lane_masks(tr)
            sa = jnp.sum(jnp.where(ma, blk, 0.0), axis=1, keepdims=True)
            sb = jnp.sum(blk, axis=1, keepdims=True) - sa
            parts.append(jnp.where(ma, sa, sb))
        return jnp.concatenate(parts, axis=1)

    def body(o0, o1, o2, l0, l1, l2, d_ref, dog_ref, sh_ref):
        al = _group_weights([l0[...], l1[...], l2[...]])
        dos = [d_ref[:, g * gw:(g + 1) * gw].astype(F32) for g in range(3)]
        dal = [head_sum(dos[g] * o_ref[...]) for g, o_ref in enumerate((o0, o1, o2))]
        mix = al[0] * dal[0] + al[1] * dal[1] + al[2] * dal[2]
        for g in range(3):
            dog_ref[:, g * gw:(g + 1) * gw] = (al[g] * dos[g]).astype(dog_ref.dtype)
            sh_ref[:, g * gw:(g + 1) * gw] = al[g] * mix

    blk = pl.BlockSpec((tr, gw), lambda i: (i, 0))
    wide = pl.BlockSpec((tr, 3 * gw), lambda i: (i, 0))
    return pl.pallas_call(
        body, name=name, grid=(t // tr,), in_specs=[blk] * 6 + [wide], out_specs=[wide, wide],
        out_shape=[jax.ShapeDtypeStruct((t, 3 * gw), BF), jax.ShapeDtypeStruct((t, 3 * gw), F32)],
        compiler_params=_params("parallel"),
    )(*os_, *lses, dcat)


def adamw(w, g1, g2, m, v, *, name):
    r, c = w.shape
    tr = r
    for cand in (256, 128, 64, 32, 16, 8):
        if r % cand == 0 and cand * c * 4 <= (1 << 20):
            tr = cand
            break
    two = g2 is not None

    def body(*refs):
        if two:
            w_ref, g1_ref, g2_ref, m_ref, v_ref, g_ref, d_ref, nm_ref, nv_ref = refs
            g = g1_ref[...] + g2_ref[...]
        else:
            w_ref, g1_ref, m_ref, v_ref, g_ref, d_ref, nm_ref, nv_ref = refs
            g = g1_ref[...]
        nm = ADAM_B1 * m_ref[...] + (1.0 - ADAM_B1) * g
        nv = ADAM_B2 * v_ref[...] + (1.0 - ADAM_B2) * (g * g)
        m_hat = nm / (1.0 - ADAM_B1 ** ADAM_STEP)
        v_hat = nv / (1.0 - ADAM_B2 ** ADAM_STEP)
        g_ref[...] = g
        d_ref[...] = -ADAM_LR * (m_hat / (jnp.sqrt(v_hat) + ADAM_EPS) + ADAM_WD * w_ref[...])
        nm_ref[...] = nm
        nv_ref[...] = nv

    blk = pl.BlockSpec((tr, c), lambda i: (i, 0))
    args = [w, g1] + ([g2] if two else []) + [m, v]
    return pl.pallas_call(
        body, name=name, grid=(r // tr,), in_specs=[blk] * len(args), out_specs=[blk] * 4,
        out_shape=[jax.ShapeDtypeStruct((r, c), F32)] * 4, compiler_params=_params("parallel"),
    )(*args)


def sum4(own, land, *, name):
    r, c = own.shape
    tr = _pick(r, (256, 128, 64))

    def body(o_ref, l_ref, s_ref):
        s_ref[...] = ((o_ref[...].astype(F32) + l_ref[0].astype(F32)) + l_ref[1].astype(F32)) + l_ref[2].astype(F32)

    return pl.pallas_call(
        body, name=name, grid=(r // tr,),
        in_specs=[pl.BlockSpec((tr, c), lambda i: (i, 0)), pl.BlockSpec((3, tr, c), lambda i: (0, i, 0))],
        out_specs=pl.BlockSpec((tr, c), lambda i: (i, 0)), out_shape=jax.ShapeDtypeStruct((r, c), F32),
        compiler_params=_params("parallel"),
    )(own, land)


ANY = pl.BlockSpec(memory_space=pl.ANY)


def _place():
    x, y, c = lax.axis_index("x"), lax.axis_index("y"), lax.axis_index("c")
    chips = [(1 - x, y), (x, 1 - y), (1 - x, 1 - y)]
    return x, y, c, chips


def gather_shards(shards, axes, *, name):
    n = len(shards)

    def body(*refs):
        ins, outs = refs[:n], refs[n:2 * n]
        send_sems, recv_sems, local_sems = refs[2 * n:]
        x, y, c, chips = _place()

        def slot(a, q):
            size = shards[a].shape[axes[a]]
            start = pl.multiple_of(q * size, size)
            return outs[a].at[pl.ds(start, size), :] if axes[a] == 0 else outs[a].at[:, pl.ds(start, size)]

        me = 2 * x + y
        copies = []
        for a in range(n):
            loc = pltpu.make_async_copy(ins[a], slot(a, me), local_sems.at[a])
            loc.start()
            copies.append(loc)
        sends = []
        for a in range(n):
            for k, (px, py) in enumerate(chips):
                cp = pltpu.make_async_remote_copy(src_ref=ins[a], dst_ref=slot(a, me), send_sem=send_sems.at[3 * a + k],
                                                  recv_sem=recv_sems.at[3 * a + k], device_id=(px, py, c), device_id_type=MESH)
                cp.start()
                sends.append(cp)
        for a in range(n):
            for k, (px, py) in enumerate(chips):
                pltpu.make_async_remote_copy(src_ref=ins[a], dst_ref=slot(a, 2 * px + py), send_sem=send_sems.at[3 * a + k],
                                             recv_sem=recv_sems.at[3 * a + k], device_id=(px, py, c),
                                             device_id_type=MESH).wait_recv()
        for cp in sends:
            cp.wait_send()
        for loc in copies:
            loc.wait()

    out_shape = []
    for a, sh in enumerate(shards):
        full = list(sh.shape)
        full[axes[a]] *= N_CHIPS
        out_shape.append(jax.ShapeDtypeStruct(tuple(full), sh.dtype))
    return pl.pallas_call(
        body, name=name, in_specs=[ANY] * n, out_specs=[ANY] * n, out_shape=out_shape,
        scratch_shapes=[pltpu.SemaphoreType.DMA((3 * n,)), pltpu.SemaphoreType.DMA((3 * n,)), pltpu.SemaphoreType.DMA((n,))],
    )(*shards)


def scatter_partials(grads, *, name):
    n = len(grads)

    def body(*refs):
        ins, outs = refs[:n], refs[n:2 * n]
        send_sems, recv_sems = refs[2 * n:]
        x, y, c, chips = _place()
        sends = []
        for a in range(n):
            for k, (px, py) in enumerate(chips):
                cp = pltpu.make_async_remote_copy(src_ref=ins[a].at[2 * px + py], dst_ref=outs[a].at[k],
                                                  send_sem=send_sems.at[3 * a + k], recv_sem=recv_sems.at[3 * a + k],
                                                  device_id=(px, py, c), device_id_type=MESH)
                cp.start()
                sends.append(cp)
        for cp in sends:
            cp.wait_recv()
        for cp in sends:
            cp.wait_send()

    return pl.pallas_call(
        body, name=name, in_specs=[ANY] * n, out_specs=[ANY] * n,
        out_shape=[jax.ShapeDtypeStruct((3,) + g.shape[1:], g.dtype) for g in grads],
        scratch_shapes=[pltpu.SemaphoreType.DMA((3 * n,)), pltpu.SemaphoreType.DMA((3 * n,))],
    )(*grads)


def swap_with_sibling(arrs, *, name):
    n = len(arrs)

    def body(*refs):
        ins, outs = refs[:n], refs[n:2 * n]
        send_sems, recv_sems = refs[2 * n:]
        x, y, c, _ = _place()
        sends = []
        for a in range(n):
            cp = pltpu.make_async_remote_copy(src_ref=ins[a], dst_ref=outs[a], send_sem=send_sems.at[a],
                                              recv_sem=recv_sems.at[a], device_id=(x, y, 1 - c), device_id_type=MESH)
            cp.start()
            sends.append(cp)
        for cp in sends:
            cp.wait_recv()
        for cp in sends:
            cp.wait_send()

    return pl.pallas_call(
        body, name=name, in_specs=[ANY] * n, out_specs=[ANY] * n,
        out_shape=[jax.ShapeDtypeStruct(a.shape, a.dtype) for a in arrs],
        scratch_shapes=[pltpu.SemaphoreType.DMA((n,)), pltpu.SemaphoreType.DMA((n,))],
    )(*arrs)


def all_reduce_small(v, *, name):
    rows = v.shape[0]

    def body(v_ref, o_ref, gath, send_sems, recv_sems):
        x, y, c, _ = _place()
        me = 4 * x + 2 * y + c
        gath[me] = v_ref[...]
        sends = []
        for msk in range(1, N_DEV):
            peer = (x ^ (msk >> 2), y ^ ((msk >> 1) & 1), c ^ (msk & 1))
            cp = pltpu.make_async_remote_copy(src_ref=v_ref, dst_ref=gath.at[me], send_sem=send_sems.at[msk - 1],
                                              recv_sem=recv_sems.at[msk - 1], device_id=peer, device_id_type=MESH)
            cp.start()
            sends.append(cp)
        for msk in range(1, N_DEV):
            pltpu.make_async_remote_copy(src_ref=v_ref, dst_ref=gath.at[me ^ msk], send_sem=send_sems.at[msk - 1],
                                         recv_sem=recv_sems.at[msk - 1], device_id=(x, y, c), device_id_type=MESH).wait_recv()
        for cp in sends:
            cp.wait_send()
        tot = gath[0]
        for q in range(1, N_DEV):
            tot = tot + gath[q]
        o_ref[...] = tot

    vm = pl.BlockSpec(memory_space=pltpu.VMEM)
    return pl.pallas_call(
        body, name=name, in_specs=[vm], out_specs=vm, out_shape=jax.ShapeDtypeStruct(v.shape, F32),
        scratch_shapes=[pltpu.VMEM((N_DEV, rows, LANES), F32), pltpu.SemaphoreType.DMA((N_DEV - 1,)),
                        pltpu.SemaphoreType.DMA((N_DEV - 1,))],
    )(v)


def _ffn_fwd(xin, gain, w_up, w_conv, w_down, bl, tag):
    n = rmsnorm_fwd(xin, gain, name=f"{tag}_ffn_norm")
    u = matmul(n, w_up, out_dtype=F32, name=f"{tag}_ffn_up")
    h = conv_silu_fwd(u, w_conv, bl, name=f"{tag}_ffn_conv")
    xout = matmul(h, w_down, out_dtype=F32, add=xin, name=f"{tag}_ffn_down")
    return xout, (n, u, h)


def _ffn_bwd(dxout, xin, saved, gain, w_up, w_conv, w_down, bl, tag, shard):
    n, u, h = saved
    dh = matmul(dxout, w_down, tb=True, name=f"{tag}_ffn_down_dx")
    g_down = matmul(h, dxout, ta=True, name=f"{tag}_ffn_down_dw")
    du, g_conv = conv_silu_bwd(u, w_conv, dh, bl, name=f"{tag}_ffn_conv_bwd")
    dn = matmul(du, w_up, tb=True, out_dtype=F32, name=f"{tag}_ffn_up_dx")
    g_up = matmul(n, du, ta=True, shard_cols=shard, name=f"{tag}_ffn_up_dw")
    dxin, g_norm = rmsnorm_bwd(xin, gain, dn, dxout, name=f"{tag}_ffn_norm_bwd")
    return dxin, g_norm, g_up, g_conv, g_down


def _mem_kv(mem2, gain, w_kv, tag):
    mn = rmsnorm_fwd(mem2, gain, name=f"{tag}_mem_norm")
    return mn, matmul(mn, w_kv, name=f"{tag}_mem_kv")


def _mem_kv_bwd(mem2, mn, gain, w_kv, dkvm, tag):
    g_kv = matmul(mn, dkvm, ta=True, name=f"{tag}_mem_kv_dw")
    dmn = matmul(dkvm, w_kv, tb=True, out_dtype=F32, name=f"{tag}_mem_kv_dx")
    _, g_norm = rmsnorm_bwd(mem2, gain, dmn, None, name=f"{tag}_mem_norm_bwd")
    return g_norm, g_kv


def local_step(x, mem, target, w, shard_of):
    bl, s, d = x.shape
    t = bl * s
    x0 = x.reshape(t, d)
    mem2 = mem.reshape(bl * MEM_LEN, d)
    tgt = target.reshape(t, d)
    g = {}

    n1 = rmsnorm_fwd(x0, w["a_norm_attn"], name="a_attn_norm")
    proj_a = matmul(n1, w["a_w_in"], name="a_w_in")
    mn_a, kvm_a = _mem_kv(mem2, w["a_norm_mem"], w["a_w_mem_kv"], "a")
    o_sb, o_sb_f32 = sb_fwd(proj_a, bl, name="a_sb_fwd")
    o_mem_a = mem_fwd(proj_a, kvm_a, bl, name="a_mem_fwd")
    cat_a = jnp.concatenate([o_sb, o_mem_a], axis=1)
    x1 = matmul(cat_a, w["a_w_out"], out_dtype=F32, add=x0, name="a_w_out")
    x2, ffn_a = _ffn_fwd(x1, w["a_norm_ffn"], w["a_ffn_up"], w["a_ffn_conv"], w["a_ffn_down"], bl, "a")

    nkv = rmsnorm_fwd(x2, w["kv_norm"], name="kv_norm")
    kv = matmul(nkv, w["w_kv_shared"], name="w_kv")
    n3 = rmsnorm_fwd(x2, w["b_norm_attn"], name="b_attn_norm")
    proj_b = matmul(n3, w["b_w_in"], name="b_w_in")
    mn_b, kvm_b = _mem_kv(mem2, w["b_norm_mem"], w["b_w_mem_kv"], "b")
    dil = [dil_fwd(proj_b, kv, gi, bl, name=f"b_dil_fwd{gi}") for gi in range(3)]
    os_, lses = [o for o, _ in dil], [l for _, l in dil]
    o_dil = dil_combine_fwd(os_, lses, name="b_dil_combine")
    o_mem_b = mem_fwd(proj_b, kvm_b, bl, name="b_mem_fwd")
    cat_b = jnp.concatenate([o_dil, o_mem_b], axis=1)
    x3 = matmul(cat_b, w["b_w_out"], out_dtype=F32, add=x2, name="b_w_out")
    x4, ffn_b = _ffn_fwd(x3, w["b_norm_ffn"], w["b_ffn_up"], w["b_ffn_conv"], w["b_ffn_down"], bl, "b")

    dx4, g["final_norm"], lossvec = final_loss(x4, w["final_norm"], tgt, name="final_loss")

    dx3, g["b_norm_ffn"], g["b_ffn_up"], g["b_ffn_conv"], g["b_ffn_down"] = _ffn_bwd(
        dx4, x3, ffn_b, w["b_norm_ffn"], w["b_ffn_up"], w["b_ffn_conv"], w["b_ffn_down"], bl, "b", shard_of["b_ffn_up"])
    dcat_b = matmul(dx3, w["b_w_out"], tb=True, name="b_w_out_dx")
    g["b_w_out"] = matmul(cat_b, dx3, ta=True, name="b_w_out_dw")
    dog, dshift = dil_combine_bwd(os_, lses, dcat_b, name="b_dil_combine_bwd")
    dqs, dks, dvs = [], [], []
    for gi in range(3):
        dq_g, dk_g, dv_g = dil_bwd(proj_b, kv, lses[gi], dog, dshift, gi, bl, name=f"b_dil_bwd{gi}")
        dqs.append(dq_g), dks.append(dk_g), dvs.append(dv_g)
    dq_mem_b, dkvm_b = mem_bwd(proj_b, kvm_b, dcat_b, bl, name="b_mem_bwd")
    g["b_norm_mem"], g["b_w_mem_kv"] = _mem_kv_bwd(mem2, mn_b, w["b_norm_mem"], w["b_w_mem_kv"], dkvm_b, "b")
    dproj_b = jnp.concatenate(dqs + [dq_mem_b], axis=1)
    dn3 = matmul(dproj_b, w["b_w_in"], tb=True, out_dtype=F32, name="b_w_in_dx")
    g["b_w_in"] = matmul(n3, dproj_b, ta=True, name="b_w_in_dw")
    dx2, g["b_norm_attn"] = rmsnorm_bwd(x2, w["b_norm_attn"], dn3, dx3, name="b_attn_norm_bwd")
    dkv = jnp.concatenate(dks + dvs, axis=1).astype(BF)
    dnkv = matmul(dkv, w["w_kv_shared"], tb=True, out_dtype=F32, name="w_kv_dx")
    g["w_kv_shared"] = matmul(nkv, dkv, ta=True, shard_cols=shard_of["w_kv_shared"], name="w_kv_dw")
    dx2, g["kv_norm"] = rmsnorm_bwd(x2, w["kv_norm"], dnkv, dx2, name="kv_norm_bwd")

    dx1, g["a_norm_ffn"], g["a_ffn_up"], g["a_ffn_conv"], g["a_ffn_down"] = _ffn_bwd(
        dx2, x1, ffn_a, w["a_norm_ffn"], w["a_ffn_up"], w["a_ffn_conv"], w["a_ffn_down"], bl, "a", shard_of["a_ffn_up"])
    dcat_a = matmul(dx1, w["a_w_out"], tb=True, name="a_w_out_dx")
    g["a_w_out"] = matmul(cat_a, dx1, ta=True, name="a_w_out_dw")
    dq_sb, dk_sb, dv_sb = sb_bwd(proj_a, o_sb_f32, dcat_a, bl, name="a_sb_bwd")
    dq_mem_a, dkvm_a = mem_bwd(proj_a, kvm_a, dcat_a, bl, name="a_mem_bwd")
    g["a_norm_mem"], g["a_w_mem_kv"] = _mem_kv_bwd(mem2, mn_a, w["a_norm_mem"], w["a_w_mem_kv"], dkvm_a, "a")
    dproj_a = jnp.concatenate([dq_sb, dk_sb.astype(BF), dv_sb.astype(BF), dq_mem_a], axis=1)
    dn1 = matmul(dproj_a, w["a_w_in"], tb=True, out_dtype=F32, name="a_w_in_dx")
    g["a_w_in"] = matmul(n1, dproj_a, ta=True, shard_cols=shard_of["a_w_in"], name="a_w_in_dw")
    dx0, g["a_norm_attn"] = rmsnorm_bwd(x0, w["a_norm_attn"], dn1, dx1, name="a_attn_norm_bwd")
    return lossvec, dx0, g


MATRICES = ("a_w_in", "a_w_out", "a_w_mem_kv", "a_ffn_up", "a_ffn_down", "w_kv_shared", "b_w_in", "b_w_out",
            "b_w_mem_kv", "b_ffn_up", "b_ffn_down")
COL_SHARDED = ("a_w_in", "a_ffn_up", "w_kv_shared", "b_ffn_up")
SMALL_SHARDED = ("a_norm_attn", "a_norm_mem", "a_norm_ffn", "a_ffn_conv", "b_ffn_conv")
SMALL_REPLICATED = ("kv_norm", "b_norm_attn", "b_norm_mem", "b_norm_ffn", "final_norm")
WEIGHTS = ("a_norm_attn", "a_w_in", "a_w_out", "a_norm_mem", "a_w_mem_kv", "a_norm_ffn", "a_ffn_up", "a_ffn_conv",
           "a_ffn_down", "kv_norm", "w_kv_shared", "b_norm_attn", "b_w_in", "b_w_out", "b_norm_mem", "b_w_mem_kv",
           "b_norm_ffn", "b_ffn_up", "b_ffn_conv", "b_ffn_down", "final_norm")


def _two_d(a):
    if a.ndim == 1:
        return a.reshape(1, -1)
    return a.reshape(a.shape[-2], a.shape[-1])


def kernel(x, mem, a_norm_attn, a_w_in, a_w_out, a_norm_mem, a_w_mem_kv, a_norm_ffn, a_ffn_up, a_ffn_conv, a_ffn_down, kv_norm, w_kv_shared, b_norm_attn, b_w_in, b_w_out, b_norm_mem, b_w_mem_kv, b_norm_ffn, b_ffn_up, b_ffn_conv, b_ffn_down, final_norm, loss_target, m_a_norm_attn, m_a_w_in, m_a_w_out, m_a_norm_mem, m_a_w_mem_kv, m_a_norm_ffn, m_a_ffn_up, m_a_ffn_conv, m_a_ffn_down, m_kv_norm, m_w_kv_shared, m_b_norm_attn, m_b_w_in, m_b_w_out, m_b_norm_mem, m_b_w_mem_kv, m_b_norm_ffn, m_b_ffn_up, m_b_ffn_conv, m_b_ffn_down, m_final_norm, v_a_norm_attn, v_a_w_in, v_a_w_out, v_a_norm_mem, v_a_w_mem_kv, v_a_norm_ffn, v_a_ffn_up, v_a_ffn_conv, v_a_ffn_down, v_kv_norm, v_w_kv_shared, v_b_norm_attn, v_b_w_in, v_b_w_out, v_b_norm_mem, v_b_w_mem_kv, v_b_norm_ffn, v_b_ffn_up, v_b_ffn_conv, v_b_ffn_down, v_final_norm):
    given = dict(locals())
    wl = {n: _two_d(given[n]) for n in WEIGHTS}
    ml = {n: _two_d(given["m_" + n]) for n in WEIGHTS}
    vl = {n: _two_d(given["v_" + n]) for n in WEIGHTS}
    chip = 2 * lax.axis_index("x") + lax.axis_index("y")

    packed = jnp.concatenate([wl[n].reshape(-1, LANES) for n in SMALL_SHARDED], axis=0)
    shards = [wl[n].astype(BF) for n in MATRICES] + [packed]
    axes = [1 if n in COL_SHARDED else 0 for n in MATRICES] + [0]
    full = gather_shards(shards, axes, name="gather_weights")
    w = dict(zip(MATRICES, full[:-1]))
    rows = packed.shape[0]
    per_chip = full[-1].reshape(N_CHIPS, rows, LANES)
    r0 = 0
    for n in SMALL_SHARDED:
        nr = wl[n].size // LANES
        piece = per_chip[:, r0:r0 + nr].reshape(N_CHIPS, wl[n].shape[0], wl[n].shape[1])
        w[n] = jnp.concatenate([piece[q] for q in range(N_CHIPS)], axis=1)
        r0 += nr
    for n in SMALL_REPLICATED:
        w[n] = wl[n]

    shard_of = {n: wl[n].shape[1] for n in COL_SHARDED}
    lossvec, dx0, g = local_step(x, mem, loss_target, w, shard_of)
    loss = lax.psum(0.5 * jnp.sum(lossvec) / x.shape[-1], ("x", "y", "c"))

    g4 = [g[n] if n in COL_SHARDED else g[n].reshape((N_CHIPS,) + wl[n].shape) for n in MATRICES]
    landed = scatter_partials(g4, name="scatter_grads")
    sums = [sum4(lax.dynamic_index_in_dim(g4[k], chip, 0, keepdims=False), landed[k], name=f"sum4_{n}")
            for k, n in enumerate(MATRICES)]
    theirs = swap_with_sibling(sums, name="swap_sums")
    out = {}
    for k, n in enumerate(MATRICES):
        out[n] = adamw(wl[n], sums[k], theirs[k], ml[n], vl[n], name=f"adamw_{n}")

    small = SMALL_SHARDED + SMALL_REPLICATED
    flat = jnp.concatenate([g[n].reshape(-1, LANES) for n in small], axis=0)
    tot = all_reduce_small(flat, name="all_reduce_small")
    r0 = 0
    for n in small:
        nr = g[n].size // LANES
        gn = tot[r0:r0 + nr].reshape(g[n].shape)
        r0 += nr
        if n in SMALL_SHARDED:
            gn = lax.dynamic_slice_in_dim(gn, chip * wl[n].shape[1], wl[n].shape[1], axis=1)
        out[n] = adamw(wl[n], gn, None, ml[n], vl[n], name=f"adamw_{n}")

    res = [loss, dx0.reshape(x.shape)]
    for slot in range(4):
        res += [out[n][slot].reshape(given[n].shape) for n in WEIGHTS]
    return tuple(res)
```

```python
import functools
import math

import jax
import jax.numpy as jnp
from jax import lax
from jax.experimental import pallas as pl
from jax.experimental.pallas import tpu as pltpu

F32 = jnp.float32
BF = jnp.bfloat16
MESH = pl.DeviceIdType.MESH

HEAD_DIM = 64
LANES = 128
SB_WIDTH = 12 * HEAD_DIM
MEM_WIDTH = 4 * HEAD_DIM
DIL_WIDTH = 12 * HEAD_DIM
MEM_LEN = 256
DIL_GROUPS = ((128, 1), (512, 4), (2048, 16))
QB = 128
EPS = 1e-6
SCALE = HEAD_DIM ** -0.5
NEG = -1e30
ALIBI = tuple(2.0 ** (-8.0 * i / 12) for i in range(1, 13))
N_CHIPS = 4
N_DEV = 8

ADAM_LR, ADAM_B1, ADAM_B2, ADAM_EPS, ADAM_WD, ADAM_STEP = 0.001, 0.9, 0.999, 1e-08, 0.01, 10

VMEM_LIMIT = 48 * 1024 * 1024


def _mo(v, m):
    return v if isinstance(v, int) else pl.multiple_of(v, m)


def _pick(n, prefs):
    for t in prefs:
        if n % t == 0:
            return t
    return n


def _params(*sem):
    return pltpu.CompilerParams(dimension_semantics=sem, vmem_limit_bytes=VMEM_LIMIT)


def matmul(a, b, *, ta=False, tb=False, out_dtype=BF, add=None, shard_cols=0, name):
    m, k = (a.shape[1], a.shape[0]) if ta else a.shape
    n = b.shape[0] if tb else b.shape[1]
    tm = _pick(m, (512, 1408, 256, 128))
    tk = _pick(k, (2048, 2816, 2560, 1536, 1024, 512, 256, 128))
    tn = shard_cols if shard_cols else _pick(n, (1024, 1408, 512, 256, 128))
    nk = k // tk
    dims = (((0,) if ta else (1,), (1,) if tb else (0,)), ((), ()))

    def body(*refs):
        a_ref, b_ref = refs[:2]
        add_ref = refs[2] if add is not None else None
        o_ref = refs[3 if add is not None else 2]
        part = lax.dot_general(a_ref[...].astype(BF), b_ref[...].astype(BF), dims, preferred_element_type=F32)

        def finish(r):
            if add is not None:
                r = r + add_ref[...]
            o_ref[...] = r.astype(o_ref.dtype)

        if nk == 1:
            finish(part)
            return
        acc_ref = refs[-1]
        kk = pl.program_id(2)

        @pl.when(kk == 0)
        def _():
            acc_ref[...] = part

        @pl.when(kk > 0)
        def _():
            acc_ref[...] += part

        @pl.when(kk == nk - 1)
        def _():
            finish(acc_ref[...])

    a_spec = pl.BlockSpec((tk, tm), lambda i, j, q: (q, i)) if ta else pl.BlockSpec((tm, tk), lambda i, j, q: (i, q))
    b_spec = pl.BlockSpec((tn, tk), lambda i, j, q: (j, q)) if tb else pl.BlockSpec((tk, tn), lambda i, j, q: (q, j))
    in_specs = [a_spec, b_spec]
    args = [a, b]
    if add is not None:
        in_specs.append(pl.BlockSpec((tm, tn), lambda i, j, q: (i, j)))
        args.append(add)
    if shard_cols:
        out_shape = jax.ShapeDtypeStruct((N_CHIPS, m, shard_cols), out_dtype)
        out_spec = pl.BlockSpec((None, tm, tn), lambda i, j, q: (j, i, 0))
    else:
        out_shape = jax.ShapeDtypeStruct((m, n), out_dtype)
        out_spec = pl.BlockSpec((tm, tn), lambda i, j, q: (i, j))
    return pl.pallas_call(
        body, name=name, grid=(m // tm, n // tn, nk), in_specs=in_specs, out_specs=out_spec, out_shape=out_shape,
        scratch_shapes=[pltpu.VMEM((tm, tn), F32)] if nk > 1 else [],
        compiler_params=_params("parallel", "parallel", "arbitrary"),
    )(*args)


def rmsnorm_fwd(x, g, *, name):
    t, d = x.shape
    tr = _pick(t, (512, 256))

    def body(x_ref, g_ref, o_ref):
        xv = x_ref[...]
        r = lax.rsqrt(jnp.mean(xv * xv, axis=-1, keepdims=True) + EPS)
        o_ref[...] = ((xv * r) * g_ref[...]).astype(o_ref.dtype)

    return pl.pallas_call(
        body, name=name, grid=(t // tr,),
        in_specs=[pl.BlockSpec((tr, d), lambda i: (i, 0)), pl.BlockSpec((1, d), lambda i: (0, 0))],
        out_specs=pl.BlockSpec((tr, d), lambda i: (i, 0)), out_shape=jax.ShapeDtypeStruct((t, d), BF),
        compiler_params=_params("parallel"),
    )(x, g)


def rmsnorm_bwd(x, g, dn, dres, *, name):
    t, d = x.shape
    tr = _pick(t, (512, 256))
    want_dx = dres is not None

    def body(*refs):
        if want_dx:
            x_ref, g_ref, dn_ref, dres_ref, dx_ref, dxb_ref, dg_ref = refs
        else:
            x_ref, g_ref, dn_ref, dg_ref = refs

        @pl.when(pl.program_id(0) == 0)
        def _():
            dg_ref[...] = jnp.zeros_like(dg_ref)

        xv = x_ref[...]
        r = lax.rsqrt(jnp.mean(xv * xv, axis=-1, keepdims=True) + EPS)
        xn = xv * r
        dnv = dn_ref[...].astype(F32)
        dg_ref[...] += jnp.sum(dnv * xn, axis=0, keepdims=True)
        if want_dx:
            dyg = dnv * g_ref[...]
            cm = jnp.mean(dyg * xn, axis=-1, keepdims=True)
            dx = dres_ref[...] + r * (dyg - xn * cm)
            dx_ref[...] = dx
            dxb_ref[...] = dx.astype(BF)

    row = pl.BlockSpec((tr, d), lambda i: (i, 0))
    vec = pl.BlockSpec((1, d), lambda i: (0, 0))
    if want_dx:
        return pl.pallas_call(
            body, name=name, grid=(t // tr,), in_specs=[row, vec, row, row], out_specs=[row, row, vec],
            out_shape=[jax.ShapeDtypeStruct((t, d), F32), jax.ShapeDtypeStruct((t, d), BF), jax.ShapeDtypeStruct((1, d), F32)],
            compiler_params=_params("arbitrary"),
        )(x, g, dn, dres)
    return None, None, pl.pallas_call(
        body, name=name, grid=(t // tr,), in_specs=[row, vec, row], out_specs=vec,
        out_shape=jax.ShapeDtypeStruct((1, d), F32), compiler_params=_params("arbitrary"),
    )(x, g, dn)


def final_loss(x, g, target, *, name):
    t, d = x.shape
    tr = _pick(t, (512, 256))

    def body(x_ref, g_ref, t_ref, dx_ref, dxb_ref, dg_ref, lv_ref):
        @pl.when(pl.program_id(0) == 0)
        def _():
            dg_ref[...] = jnp.zeros_like(dg_ref)
            lv_ref[...] = jnp.zeros_like(lv_ref)

        xv = x_ref[...]
        r = lax.rsqrt(jnp.mean(xv * xv, axis=-1, keepdims=True) + EPS)
        xn = xv * r
        err = xn * g_ref[...] - t_ref[...]
        lv_ref[...] += jnp.sum(err * err, axis=0, keepdims=True)
        dy = err * (1.0 / d)
        dg_ref[...] += jnp.sum(dy * xn, axis=0, keepdims=True)
        dyg = dy * g_ref[...]
        cm = jnp.mean(dyg * xn, axis=-1, keepdims=True)
        dx = r * (dyg - xn * cm)
        dx_ref[...] = dx
        dxb_ref[...] = dx.astype(BF)

    row = pl.BlockSpec((tr, d), lambda i: (i, 0))
    vec = pl.BlockSpec((1, d), lambda i: (0, 0))
    return pl.pallas_call(
        body, name=name, grid=(t // tr,), in_specs=[row, vec, row], out_specs=[row, row, vec, vec],
        out_shape=[jax.ShapeDtypeStruct((t, d), F32), jax.ShapeDtypeStruct((t, d), BF), jax.ShapeDtypeStruct((1, d), F32),
                   jax.ShapeDtypeStruct((1, d), F32)],
        compiler_params=_params("arbitrary"),
    )(x, g, target)


CONV_ROWS = 256
HALO = 16


def _conv_taps(ext, w, rows):
    s0 = ext[HALO:HALO + rows]
    s1 = pltpu.roll(ext, 1, 0)[HALO:HALO + rows]
    s2 = pltpu.roll(ext, 2, 0)[HALO:HALO + rows]
    return (w[0:1] * s2 + w[1:2] * s1) + w[2:3] * s0, s0, s1, s2


def conv_silu_fwd(u, w, bl, *, name):
    t, f2 = u.shape
    f = f2 // 2
    s = t // bl
    tc = _pick(f, (256, 128))
    nf = f // tc
    nch = s // CONV_ROWS

    def body(ua_ref, ug_ref, wa_ref, wg_ref, h_ref):
        wa, wg = wa_ref[...], wg_ref[...]

        def chunk(ci, carry):
            r0 = pl.multiple_of(ci * CONV_ROWS, CONV_ROWS)
            ps = pl.multiple_of(jnp.maximum(r0 - HALO, 0), HALO)
            keep = (ci > 0).astype(F32)

            def conv(ref, wv):
                ext = jnp.concatenate([ref[0, pl.ds(ps, HALO), :].astype(F32) * keep,
                                       ref[0, pl.ds(r0, CONV_ROWS), :].astype(F32)], axis=0)
                return _conv_taps(ext, wv, CONV_ROWS)[0]

            ca, cg = conv(ua_ref, wa), conv(ug_ref, wg)
            sg = 1.0 / (1.0 + jnp.exp(-cg))
            h_ref[0, pl.ds(r0, CONV_ROWS), :] = ((cg * sg) * ca).astype(h_ref.dtype)
            return carry

        lax.fori_loop(0, nch, chunk, 0)

    u3 = u.reshape(bl, s, f2)
    blk = lambda off: pl.BlockSpec((1, s, tc), lambda j, b: (b, 0, j + off))
    wblk = lambda off: pl.BlockSpec((3, tc), lambda j, b: (0, j + off))
    h = pl.pallas_call(
        body, name=name, grid=(nf, bl), in_specs=[blk(0), blk(nf), wblk(0), wblk(nf)],
        out_specs=pl.BlockSpec((1, s, tc), lambda j, b: (b, 0, j)), out_shape=jax.ShapeDtypeStruct((bl, s, f), BF),
        compiler_params=_params("parallel", "parallel"),
    )(u3, u3, w, w)
    return h.reshape(t, f)


def conv_silu_bwd(u, w, dh, bl, *, name):
    t, f2 = u.shape
    f = f2 // 2
    s = t // bl
    tc = _pick(f, (256, 128))
    nf = f // tc
    nch = s // CONV_ROWS
    ext_rows = CONV_ROWS + HALO

    def body(ua_ref, ug_ref, wa_ref, wg_ref, dh_ref, dua_ref, dug_ref, dwa_ref, dwg_ref):
        wa, wg = wa_ref[...], wg_ref[...]

        @pl.when(pl.program_id(1) == 0)
        def _():
            dwa_ref[...] = jnp.zeros_like(dwa_ref)
            dwg_ref[...] = jnp.zeros_like(dwg_ref)

        def chunk(ci, carry):
            r0 = pl.multiple_of(ci * CONV_ROWS, CONV_ROWS)
            ps = pl.multiple_of(jnp.maximum(r0 - HALO, 0), HALO)
            ns = pl.multiple_of(jnp.minimum(r0 + CONV_ROWS, s - HALO), HALO)
            keep_p = (ci > 0).astype(F32)
            keep_n = (ci < nch - 1).astype(F32)

            def ext_of(ref):
                return jnp.concatenate([ref[0, pl.ds(ps, HALO), :].astype(F32) * keep_p,
                                        ref[0, pl.ds(r0, CONV_ROWS), :].astype(F32),
                                        ref[0, pl.ds(ns, HALO), :].astype(F32) * keep_n], axis=0)

            ca, a0, a1, a2 = _conv_taps(ext_of(ua_ref), wa, ext_rows)
            cg, g0, g1, g2 = _conv_taps(ext_of(ug_ref), wg, ext_rows)
            dhe = jnp.concatenate([dh_ref[0, pl.ds(r0, CONV_ROWS), :].astype(F32),
                                   dh_ref[0, pl.ds(ns, HALO), :].astype(F32) * keep_n], axis=0)
            sg = 1.0 / (1.0 + jnp.exp(-cg))
            dca = dhe * (cg * sg)
            dcg = dhe * ca * (sg * (1.0 + cg * (1.0 - sg)))

            def back(dc, wv):
                n1 = pltpu.roll(dc, ext_rows - 1, 0)
                n2 = pltpu.roll(dc, ext_rows - 2, 0)
                return ((wv[2:3] * dc + wv[1:2] * n1) + wv[0:1] * n2)[:CONV_ROWS]

            dua_ref[0, pl.ds(r0, CONV_ROWS), :] = back(dca, wa).astype(dua_ref.dtype)
            dug_ref[0, pl.ds(r0, CONV_ROWS), :] = back(dcg, wg).astype(dug_ref.dtype)

            def wsum(dc, taps):
                d0 = dc[:CONV_ROWS]
                return [jnp.sum(d0 * tp[:CONV_ROWS], axis=0, keepdims=True) for tp in taps]

            sa = wsum(dca, (a2, a1, a0))
            sgs = wsum(dcg, (g2, g1, g0))
            for j in range(3):
                dwa_ref[j:j + 1, :] += sa[j]
                dwg_ref[j:j + 1, :] += sgs[j]
            return carry

        lax.fori_loop(0, nch, chunk, 0)

    u3 = u.reshape(bl, s, f2)
    dh3 = dh.reshape(bl, s, f)
    blk = lambda off: pl.BlockSpec((1, s, tc), lambda j, b: (b, 0, j + off))
    wblk = lambda off: pl.BlockSpec((3, tc), lambda j, b: (0, j + off))
    dua, dug, dwa, dwg = pl.pallas_call(
        body, name=name, grid=(nf, bl), in_specs=[blk(0), blk(nf), wblk(0), wblk(nf), blk(0)],
        out_specs=[blk(0), blk(0), wblk(0), wblk(0)],
        out_shape=[jax.ShapeDtypeStruct((bl, s, f), BF), jax.ShapeDtypeStruct((bl, s, f), BF),
                   jax.ShapeDtypeStruct((3, f), F32), jax.ShapeDtypeStruct((3, f), F32)],
        compiler_params=_params("parallel", "arbitrary"),
    )(u3, u3, w, w, dh3)
    du = jnp.concatenate([dua.reshape(t, f), dug.reshape(t, f)], axis=1)
    return du, jnp.concatenate([dwa, dwg], axis=1)


def _lane_masks(rows):
    lane = lax.broadcasted_iota(jnp.int32, (rows, LANES), 1)
    return lane < HEAD_DIM, lane >= HEAD_DIM


def _nt(a, b):
    return lax.dot_general(a, b, (((1,), (1,)), ((), ())), preferred_element_type=F32)


def _tn(a, b):
    return lax.dot_general(a, b, (((0,), (0,)), ((), ())), preferred_element_type=F32)


def _nn(a, b):
    return jnp.dot(a, b, preferred_element_type=F32)


def _suffix_ones():
    j = lax.broadcasted_iota(jnp.int32, (2 * QB, QB), 0) % QB
    s = lax.broadcasted_iota(jnp.int32, (2 * QB, QB), 1)
    return (j >= s).astype(BF)


def _suffix_sum(v, uu):
    hi = v.astype(BF)
    lo = (v - hi.astype(F32)).astype(BF)
    return _nn(jnp.concatenate([hi, lo], axis=1), uu)


def _softplus(z):
    return jnp.maximum(z, 0.0) + jnp.log(1.0 + jnp.exp(-jnp.abs(z)))


SB_BLOCK = 256
SB_DEAD = -100.0


def _sb_rel():
    return (lax.broadcasted_iota(jnp.int32, (SB_BLOCK, SB_BLOCK), 1)
            - lax.broadcasted_iota(jnp.int32, (SB_BLOCK, SB_BLOCK), 0))


def _sb_weights(qh, kb, causal, carry, uu):
    z = _nt(qh, kb) * SCALE
    ls = jnp.where(causal, -_softplus(z), 0.0)
    nsub = SB_BLOCK // QB
    cins = [None] * nsub
    offs = [None] * nsub
    for u in reversed(range(nsub)):
        cins[u] = _suffix_sum(ls[:, u * QB:(u + 1) * QB], uu)
        offs[u] = carry
        carry = carry + cins[u][:, 0:1]
    e = jnp.concatenate([z[:, u * QB:(u + 1) * QB] + cins[u] + offs[u] for u in range(nsub)], axis=1)
    return z, jnp.where(causal, jnp.exp(e), 0.0), carry, ls


def _sb_alive(ca, cb):
    return (jnp.maximum(jnp.max(ca), jnp.max(cb)) > SB_DEAD).astype(jnp.int32)


def sb_fwd(proj, bl, *, name):
    t, width = proj.shape
    s = t // bl
    npair = SB_WIDTH // LANES
    nq = s // SB_BLOCK

    def body(q_ref, k_ref, v_ref, o_ref, of_ref):
        i = pl.program_id(2)
        uu = _suffix_ones()
        ma, mb = _lane_masks(SB_BLOCK)
        q = q_ref[0]
        rel = _sb_rel()

        def chunk(c, qh, carry, acc):
            start = pl.multiple_of(c * SB_BLOCK, SB_BLOCK)
            kb = k_ref[0, pl.ds(start, SB_BLOCK), :]
            vb = v_ref[0, pl.ds(start, SB_BLOCK), :]
            causal = rel < (i - c) * SB_BLOCK
            z, w, carry, _ = _sb_weights(qh, kb, causal, carry, uu)
            return carry, acc + _nn(w.astype(BF), vb)

        def live(c, st):
            _, ca, acca, cb, accb = st
            ca, acca = chunk(c, jnp.where(ma, q, jnp.zeros_like(q)), ca, acca)
            cb, accb = chunk(c, jnp.where(mb, q, jnp.zeros_like(q)), cb, accb)
            return _sb_alive(ca, cb), ca, acca, cb, accb

        def step(n, st):
            return lax.cond(st[0] > 0, functools.partial(live, i - n), lambda st: st, st)

        zc = jnp.zeros((SB_BLOCK, 1), F32)
        za = jnp.zeros((SB_BLOCK, LANES), F32)
        st = lax.fori_loop(0, i + 1, step, (jnp.int32(1), zc, za, zc, za))
        both = jnp.where(ma, st[2], st[4])
        o_ref[0] = both.astype(o_ref.dtype)
        of_ref[0] = both

    p3 = proj.reshape(bl, s, width)
    qblk = pl.BlockSpec((1, SB_BLOCK, LANES), lambda b, p, i: (b, i, p))
    o, of = pl.pallas_call(
        body, name=name, grid=(bl, npair, nq),
        in_specs=[qblk, pl.BlockSpec((1, s, LANES), lambda b, p, i: (b, 0, npair + p)),
                  pl.BlockSpec((1, s, LANES), lambda b, p, i: (b, 0, 2 * npair + p))],
        out_specs=[qblk, qblk],
        out_shape=[jax.ShapeDtypeStruct((bl, s, SB_WIDTH), BF), jax.ShapeDtypeStruct((bl, s, SB_WIDTH), F32)],
        compiler_params=_params("parallel", "parallel", "parallel"),
    )(p3, p3, p3)
    return o.reshape(t, SB_WIDTH), of.reshape(t, SB_WIDTH)


def sb_bwd(proj, o, dcat, bl, *, name):
    t, width = proj.shape
    s = t // bl
    npair = SB_WIDTH // LANES
    nq = s // SB_BLOCK
    nsub = SB_BLOCK // QB

    def body(q_ref, k_ref, v_ref, o_ref, do_ref, dq_ref, dk_ref, dv_ref):
        i = pl.program_id(2)

        @pl.when(i == 0)
        def _():
            dk_ref[...] = jnp.zeros_like(dk_ref)
            dv_ref[...] = jnp.zeros_like(dv_ref)

        uu = _suffix_ones()
        ma, mb = _lane_masks(SB_BLOCK)
        q = q_ref[0]
        do = do_ref[0]
        prod = do.astype(F32) * o_ref[0]
        rel = _sb_rel()

        def chunk(c, m, carry_c, carry_s, dq):
            qh = jnp.where(m, q, jnp.zeros_like(q))
            doh = jnp.where(m, do, jnp.zeros_like(do))
            dsum = jnp.sum(jnp.where(m, prod, 0.0), axis=1, keepdims=True)
            start = pl.multiple_of(c * SB_BLOCK, SB_BLOCK)
            kb = k_ref[0, pl.ds(start, SB_BLOCK), :]
            vb = v_ref[0, pl.ds(start, SB_BLOCK), :]
            causal = rel < (i - c) * SB_BLOCK
            z, w, carry_c, ls = _sb_weights(qh, kb, causal, carry_c, uu)
            wb = w.astype(BF)
            da = wb.astype(F32) * _nt(doh, vb)
            pres = [None] * nsub
            for u in reversed(range(nsub)):
                dau = da[:, u * QB:(u + 1) * QB]
                sin = _suffix_sum(dau, uu)
                pres[u] = dsum - (sin - dau + carry_s)
                carry_s = carry_s + sin[:, 0:1]
            pre = jnp.concatenate(pres, axis=1)
            dz = jnp.where(causal, da - jnp.exp(z + ls) * pre, 0.0)
            dzs = (dz * SCALE).astype(BF)
            dk_ref[0, pl.ds(start, SB_BLOCK), :] += _tn(dzs, qh)
            dv_ref[0, pl.ds(start, SB_BLOCK), :] += _tn(wb, doh)
            return carry_c, carry_s, dq + _nn(dzs, kb)

        def live(c, st):
            _, sa, sb = st
            sa = chunk(c, ma, *sa)
            sb = chunk(c, mb, *sb)
            return _sb_alive(sa[0], sb[0]), sa, sb

        def step(n, st):
            return lax.cond(st[0] > 0, functools.partial(live, i - n), lambda st: st, st)

        zc = jnp.zeros((SB_BLOCK, 1), F32)
        init = (zc, zc, jnp.zeros((SB_BLOCK, LANES), F32))
        st = lax.fori_loop(0, i + 1, step, (jnp.int32(1), init, init))
        dq_ref[0] = jnp.where(ma, st[1][2], st[2][2]).astype(dq_ref.dtype)

    p3 = proj.reshape(bl, s, width)
    o3 = o.reshape(bl, s, SB_WIDTH)
    d3 = dcat.reshape(bl, s, dcat.shape[1])
    qblk = pl.BlockSpec((1, SB_BLOCK, LANES), lambda b, p, i: (b, i, p))
    full = pl.BlockSpec((1, s, LANES), lambda b, p, i: (b, 0, p))
    dq, dk, dv = pl.pallas_call(
        body, name=name, grid=(bl, npair, nq),
        in_specs=[qblk, pl.BlockSpec((1, s, LANES), lambda b, p, i: (b, 0, npair + p)),
                  pl.BlockSpec((1, s, LANES), lambda b, p, i: (b, 0, 2 * npair + p)), qblk, qblk],
        out_specs=[qblk, full, full],
        out_shape=[jax.ShapeDtypeStruct((bl, s, SB_WIDTH), BF), jax.ShapeDtypeStruct((bl, s, SB_WIDTH), F32),
                   jax.ShapeDtypeStruct((bl, s, SB_WIDTH), F32)],
        compiler_params=_params("parallel", "parallel", "arbitrary"),
    )(p3, p3, p3, o3, d3)
    return dq.reshape(t, SB_WIDTH), dk.reshape(t, SB_WIDTH), dv.reshape(t, SB_WIDTH)


MEM_TQ = 512


def mem_fwd(proj, kvm, bl, *, name):
    t, width = proj.shape
    s = t // bl
    qoff = (width - MEM_WIDTH) // LANES
    npair = MEM_WIDTH // LANES

    def body(q_ref, k_ref, v_ref, o_ref):
        ma, mb = _lane_masks(MEM_TQ)
        q = q_ref[...]
        outs = []
        for m in (ma, mb):
            qh = jnp.where(m, q, jnp.zeros_like(q))
            sc = _nt(qh, k_ref[...]) * SCALE
            p = jnp.exp(sc - jnp.max(sc, axis=-1, keepdims=True))
            p = p / jnp.sum(p, axis=-1, keepdims=True)
            outs.append(_nn(p.astype(BF), v_ref[...]))
        o_ref[...] = jnp.where(ma, outs[0], outs[1]).astype(o_ref.dtype)

    nt = s // MEM_TQ
    return pl.pallas_call(
        body, name=name, grid=(bl, npair, nt),
        in_specs=[pl.BlockSpec((MEM_TQ, LANES), lambda b, p, i: (b * nt + i, qoff + p)),
                  pl.BlockSpec((MEM_LEN, LANES), lambda b, p, i: (b, p)),
                  pl.BlockSpec((MEM_LEN, LANES), lambda b, p, i: (b, npair + p))],
        out_specs=pl.BlockSpec((MEM_TQ, LANES), lambda b, p, i: (b * nt + i, p)),
        out_shape=jax.ShapeDtypeStruct((t, MEM_WIDTH), BF),
        compiler_params=_params("parallel", "parallel", "parallel"),
    )(proj, kvm, kvm)


def mem_bwd(proj, kvm, dcat, bl, *, name):
    t, width = proj.shape
    s = t // bl
    qoff = (width - MEM_WIDTH) // LANES
    doff = (dcat.shape[1] - MEM_WIDTH) // LANES
    npair = MEM_WIDTH // LANES
    nt = s // MEM_TQ

    def body(q_ref, k_ref, v_ref, do_ref, dq_ref, dk_ref, dv_ref):
        @pl.when(pl.program_id(2) == 0)
        def _():
            dk_ref[...] = jnp.zeros_like(dk_ref)
            dv_ref[...] = jnp.zeros_like(dv_ref)

        ma, mb = _lane_masks(MEM_TQ)
        q = q_ref[...]
        do = do_ref[...]
        kb, vb = k_ref[...], v_ref[...]
        dqs = []
        for m in (ma, mb):
            qh = jnp.where(m, q, jnp.zeros_like(q))
            doh = jnp.where(m, do, jnp.zeros_like(do))
            sc = _nt(qh, kb) * SCALE
            p = jnp.exp(sc - jnp.max(sc, axis=-1, keepdims=True))
            p = p / jnp.sum(p, axis=-1, keepdims=True)
            dp = _nt(doh, vb)
            ds = p * (dp - jnp.sum(p * dp, axis=-1, keepdims=True))
            dss = (ds * SCALE).astype(BF)
            dk_ref[...] += _tn(dss, qh)
            dv_ref[...] += _tn(p.astype(BF), doh)
            dqs.append(_nn(dss, kb))
        dq_ref[...] = jnp.where(ma, dqs[0], dqs[1]).astype(dq_ref.dtype)

    kblk = pl.BlockSpec((MEM_LEN, LANES), lambda b, p, i: (b, p))
    dq, dk, dv = pl.pallas_call(
        body, name=name, grid=(bl, npair, nt),
        in_specs=[pl.BlockSpec((MEM_TQ, LANES), lambda b, p, i: (b * nt + i, qoff + p)), kblk,
                  pl.BlockSpec((MEM_LEN, LANES), lambda b, p, i: (b, npair + p)),
                  pl.BlockSpec((MEM_TQ, LANES), lambda b, p, i: (b * nt + i, doff + p))],
        out_specs=[pl.BlockSpec((MEM_TQ, LANES), lambda b, p, i: (b * nt + i, p)), kblk, kblk],
        out_shape=[jax.ShapeDtypeStruct((t, MEM_WIDTH), BF), jax.ShapeDtypeStruct((bl * MEM_LEN, MEM_WIDTH), F32),
                   jax.ShapeDtypeStruct((bl * MEM_LEN, MEM_WIDTH), F32)],
        compiler_params=_params("parallel", "parallel", "arbitrary"),
    )(proj, kvm, kvm, dcat)
    return dq, jnp.concatenate([dk, dv], axis=1).astype(BF)


def _dil_scores(qh, kb, slope_d, n_keys):
    i = lax.broadcasted_iota(jnp.int32, (QB, n_keys), 0)
    j = lax.broadcasted_iota(jnp.int32, (QB, n_keys), 1)
    delta = i + (n_keys - QB) - j
    valid = (delta >= 0) & (delta <= QB)
    sc = _nt(qh, kb) * SCALE - slope_d * delta.astype(F32)
    return jnp.where(valid, sc, NEG)


def _dil_slopes(g, dil):
    p = pl.program_id(2)
    sa = jnp.where(p == 0, ALIBI[4 * g] * dil, ALIBI[4 * g + 2] * dil).astype(F32)
    sb = jnp.where(p == 0, ALIBI[4 * g + 1] * dil, ALIBI[4 * g + 3] * dil).astype(F32)
    return sa, sb


def dil_fwd(projb, kv, g, bl, *, name):
    _, dil = DIL_GROUPS[g]
    t, wq = projb.shape
    wk = kv.shape[1]
    s = t // bl
    ln = s // dil
    nb = ln // QB
    gw = 4 * HEAD_DIM
    cq, ck, co = wq // LANES, wk // LANES, gw // LANES

    def body(q_ref, k_ref, v_ref, o_ref, lse_ref):
        sa, sb = _dil_slopes(g, dil)
        ma, mb = _lane_masks(QB)

        def block(n, n_keys):
            q0 = _mo(n * QB, QB)
            k0 = _mo(q0 - (n_keys - QB), QB)
            q = q_ref[0, pl.ds(q0, QB), :]
            kb = k_ref[0, pl.ds(k0, n_keys), :]
            vb = v_ref[0, pl.ds(k0, n_keys), :]
            outs, lses = [], []
            for m, sl in ((ma, sa), (mb, sb)):
                qh = jnp.where(m, q, jnp.zeros_like(q))
                sc = _dil_scores(qh, kb, sl, n_keys)
                mx = jnp.max(sc, axis=-1, keepdims=True)
                p = jnp.exp(sc - mx)
                den = jnp.sum(p, axis=-1, keepdims=True)
                outs.append(_nn(p.astype(BF), vb) / den)
                lses.append(mx + jnp.log(den))
            o_ref[0, pl.ds(q0, QB), :] = jnp.where(ma, outs[0], outs[1])
            lse_ref[0, pl.ds(q0, QB), :] = jnp.where(ma, lses[0], lses[1])

        block(0, QB)
        if nb > 1:
            def step(n, c):
                block(n, 2 * QB)
                return c
            lax.fori_loop(1, nb, step, 0)

    q3 = projb.reshape(bl, ln, dil * wq)
    k3 = kv.reshape(bl, ln, dil * wk)
    oblk = pl.BlockSpec((1, ln, LANES), lambda b, r, p: (b, 0, r * co + p))
    o, lse = pl.pallas_call(
        body, name=name, grid=(bl, dil, co),
        in_specs=[pl.BlockSpec((1, ln, LANES), lambda b, r, p: (b, 0, r * cq + g * co + p)),
                  pl.BlockSpec((1, ln, LANES), lambda b, r, p: (b, 0, r * ck + g * co + p)),
                  pl.BlockSpec((1, ln, LANES), lambda b, r, p: (b, 0, r * ck + ck // 2 + g * co + p))],
        out_specs=[oblk, oblk],
        out_shape=[jax.ShapeDtypeStruct((bl, ln, dil * gw), F32), jax.ShapeDtypeStruct((bl, ln, dil * gw), F32)],
        compiler_params=_params("parallel", "parallel", "parallel"),
    )(q3, k3, k3)
    return o.reshape(t, gw), lse.reshape(t, gw)


def dil_bwd(projb, kv, lse, dog, dshift, g, bl, *, name):
    _, dil = DIL_GROUPS[g]
    t, wq = projb.shape
    wk = kv.shape[1]
    s = t // bl
    ln = s // dil
    nb = ln // QB
    gw = 4 * HEAD_DIM
    cq, ck, co, cd = wq // LANES, wk // LANES, gw // LANES, DIL_WIDTH // LANES

    def body(q_ref, k_ref, v_ref, lse_ref, do_ref, sh_ref, dq_ref, dk_ref, dv_ref):
        sa, sb = _dil_slopes(g, dil)
        ma, mb = _lane_masks(QB)
        dk_ref[...] = jnp.zeros_like(dk_ref)
        dv_ref[...] = jnp.zeros_like(dv_ref)

        def block(n, n_keys):
            q0 = _mo(n * QB, QB)
            k0 = _mo(q0 - (n_keys - QB), QB)
            q = q_ref[0, pl.ds(q0, QB), :]
            do = do_ref[0, pl.ds(q0, QB), :]
            lse_b = lse_ref[0, pl.ds(q0, QB), :]
            sh_b = sh_ref[0, pl.ds(q0, QB), :]
            kb = k_ref[0, pl.ds(k0, n_keys), :]
            vb = v_ref[0, pl.ds(k0, n_keys), :]
            dqs = []
            for m, sl, c0 in ((ma, sa, 0), (mb, sb, HEAD_DIM)):
                qh = jnp.where(m, q, jnp.zeros_like(q))
                doh = jnp.where(m, do, jnp.zeros_like(do))
                sc = _dil_scores(qh, kb, sl, n_keys)
                p = jnp.exp(sc - lse_b[:, c0:c0 + 1])
                ds = p * (_nt(doh, vb) - sh_b[:, c0:c0 + 1])
                dss = (ds * SCALE).astype(BF)
                dk_ref[0, pl.ds(k0, n_keys), :] += _tn(dss, qh)
                dv_ref[0, pl.ds(k0, n_keys), :] += _tn(p.astype(BF), doh)
                dqs.append(_nn(dss, kb))
            dq_ref[0, pl.ds(q0, QB), :] = jnp.where(ma, dqs[0], dqs[1]).astype(dq_ref.dtype)

        block(0, QB)
        if nb > 1:
            def step(n, c):
                block(n, 2 * QB)
                return c
            lax.fori_loop(1, nb, step, 0)

    q3 = projb.reshape(bl, ln, dil * wq)
    k3 = kv.reshape(bl, ln, dil * wk)
    l3 = lse.reshape(bl, ln, dil * gw)
    d3 = dog.reshape(bl, ln, dil * DIL_WIDTH)
    s3 = dshift.reshape(bl, ln, dil * DIL_WIDTH)
    oblk = pl.BlockSpec((1, ln, LANES), lambda b, r, p: (b, 0, r * co + p))
    dblk = pl.BlockSpec((1, ln, LANES), lambda b, r, p: (b, 0, r * cd + g * co + p))
    dq, dk, dv = pl.pallas_call(
        body, name=name, grid=(bl, dil, co),
        in_specs=[pl.BlockSpec((1, ln, LANES), lambda b, r, p: (b, 0, r * cq + g * co + p)),
                  pl.BlockSpec((1, ln, LANES), lambda b, r, p: (b, 0, r * ck + g * co + p)),
                  pl.BlockSpec((1, ln, LANES), lambda b, r, p: (b, 0, r * ck + ck // 2 + g * co + p)),
                  oblk, dblk, dblk],
        out_specs=[oblk, oblk, oblk],
        out_shape=[jax.ShapeDtypeStruct((bl, ln, dil * gw), BF), jax.ShapeDtypeStruct((bl, ln, dil * gw), F32),
                   jax.ShapeDtypeStruct((bl, ln, dil * gw), F32)],
        compiler_params=_params("parallel", "parallel", "parallel"),
    )(q3, k3, k3, l3, d3, s3)
    return dq.reshape(t, gw), dk.reshape(t, gw), dv.reshape(t, gw)


def _group_weights(lses):
    mx = jnp.maximum(jnp.maximum(lses[0], lses[1]), lses[2])
    es = [jnp.exp(l - mx) for l in lses]
    tot = es[0] + es[1] + es[2]
    return [e / tot for e in es]


def dil_combine_fwd(os_, lses, *, name):
    t, gw = os_[0].shape
    tr = _pick(t, (512, 256))

    def body(o0, o1, o2, l0, l1, l2, out_ref):
        al = _group_weights([l0[...], l1[...], l2[...]])
        for g, o_ref in enumerate((o0, o1, o2)):
            out_ref[:, g * gw:(g + 1) * gw] = (o_ref[...] * al[g]).astype(out_ref.dtype)

    blk = pl.BlockSpec((tr, gw), lambda i: (i, 0))
    return pl.pallas_call(
        body, name=name, grid=(t // tr,), in_specs=[blk] * 6,
        out_specs=pl.BlockSpec((tr, 3 * gw), lambda i: (i, 0)), out_shape=jax.ShapeDtypeStruct((t, 3 * gw), BF),
        compiler_params=_params("parallel"),
    )(*os_, *lses)


def dil_combine_bwd(os_, lses, dcat, *, name):
    t, gw = os_[0].shape
    tr = _pick(t, (512, 256))

    def head_sum(v):
        parts = []
        for c in range(gw // LANES):
            blk = v[:, c * LANES:(c + 1) * LANES]
            ma, _ = _lane_masks(tr)
            sa = jnp.sum(jnp.where(ma, blk, 0.0), axis=1, keepdims=True)
            sb = jnp.sum(blk, axis=1, keepdims=True) - sa
            parts.append(jnp.where(ma, sa, sb))
        return jnp.concatenate(parts, axis=1)

    def body(o0, o1, o2, l0, l1, l2, d_ref, dog_ref, sh_ref):
        al = _group_weights([l0[...], l1[...], l2[...]])
        dos = [d_ref[:, g * gw:(g + 1) * gw].astype(F32) for g in range(3)]
        dal = [head_sum(dos[g] * o_ref[...]) for g, o_ref in enumerate((o0, o1, o2))]
        mix = al[0] * dal[0] + al[1] * dal[1] + al[2] * dal[2]
        for g in range(3):
            dog_ref[:, g * gw:(g + 1) * gw] = (al[g] * dos[g]).astype(dog_ref.dtype)
            sh_ref[:, g * gw:(g + 1) * gw] = al[g] * mix

    blk = pl.BlockSpec((tr, gw), lambda i: (i, 0))
    wide = pl.BlockSpec((tr, 3 * gw), lambda i: (i, 0))
    return pl.pallas_call(
        body, name=name, grid=(t // tr,), in_specs=[blk] * 6 + [wide], out_specs=[wide, wide],
        out_shape=[jax.ShapeDtypeStruct((t, 3 * gw), BF), jax.ShapeDtypeStruct((t, 3 * gw), F32)],
        compiler_params=_params("parallel"),
    )(*os_, *lses, dcat)


def adamw(w, g1, g2, m, v, *, name):
    r, c = w.shape
    tr = r
    for cand in (256, 128, 64, 32, 16, 8):
        if r % cand == 0 and cand * c * 4 <= (1 << 20):
            tr = cand
            break
    two = g2 is not None

    def body(*refs):
        if two:
            w_ref, g1_ref, g2_ref, m_ref, v_ref, g_ref, d_ref, nm_ref, nv_ref = refs
            g = g1_ref[...] + g2_ref[...]
        else:
            w_ref, g1_ref, m_ref, v_ref, g_ref, d_ref, nm_ref, nv_ref = refs
            g = g1_ref[...]
        nm = ADAM_B1 * m_ref[...] + (1.0 - ADAM_B1) * g
        nv = ADAM_B2 * v_ref[...] + (1.0 - ADAM_B2) * (g * g)
        m_hat = nm / (1.0 - ADAM_B1 ** ADAM_STEP)
        v_hat = nv / (1.0 - ADAM_B2 ** ADAM_STEP)
        g_ref[...] = g
        d_ref[...] = -ADAM_LR * (m_hat / (jnp.sqrt(v_hat) + ADAM_EPS) + ADAM_WD * w_ref[...])
        nm_ref[...] = nm
        nv_ref[...] = nv

    blk = pl.BlockSpec((tr, c), lambda i: (i, 0))
    args = [w, g1] + ([g2] if two else []) + [m, v]
    return pl.pallas_call(
        body, name=name, grid=(r // tr,), in_specs=[blk] * len(args), out_specs=[blk] * 4,
        out_shape=[jax.ShapeDtypeStruct((r, c), F32)] * 4, compiler_params=_params("parallel"),
    )(*args)


def sum4(own, land, *, name):
    r, c = own.shape
    tr = _pick(r, (256, 128, 64))

    def body(o_ref, l_ref, s_ref):
        s_ref[...] = ((o_ref[...].astype(F32) + l_ref[0].astype(F32)) + l_ref[1].astype(F32)) + l_ref[2].astype(F32)

    return pl.pallas_call(
        body, name=name, grid=(r // tr,),
        in_specs=[pl.BlockSpec((tr, c), lambda i: (i, 0)), pl.BlockSpec((3, tr, c), lambda i: (0, i, 0))],
        out_specs=pl.BlockSpec((tr, c), lambda i: (i, 0)), out_shape=jax.ShapeDtypeStruct((r, c), F32),
        compiler_params=_params("parallel"),
    )(own, land)


ANY = pl.BlockSpec(memory_space=pl.ANY)


def _place():
    x, y, c = lax.axis_index("x"), lax.axis_index("y"), lax.axis_index("c")
    chips = [(1 - x, y), (x, 1 - y), (1 - x, 1 - y)]
    return x, y, c, chips


def gather_shards(shards, axes, *, name):
    n = len(shards)

    def body(*refs):
        ins, outs = refs[:n], refs[n:2 * n]
        send_sems, recv_sems, local_sems = refs[2 * n:]
        x, y, c, chips = _place()

        def slot(a, q):
            size = shards[a].shape[axes[a]]
            start = pl.multiple_of(q * size, size)
            return outs[a].at[pl.ds(start, size), :] if axes[a] == 0 else outs[a].at[:, pl.ds(start, size)]

        me = 2 * x + y
        copies = []
        for a in range(n):
            loc = pltpu.make_async_copy(ins[a], slot(a, me), local_sems.at[a])
            loc.start()
            copies.append(loc)
        sends = []
        for a in range(n):
            for k, (px, py) in enumerate(chips):
                cp = pltpu.make_async_remote_copy(src_ref=ins[a], dst_ref=slot(a, me), send_sem=send_sems.at[3 * a + k],
                                                  recv_sem=recv_sems.at[3 * a + k], device_id=(px, py, c), device_id_type=MESH)
                cp.start()
                sends.append(cp)
        for a in range(n):
            for k, (px, py) in enumerate(chips):
                pltpu.make_async_remote_copy(src_ref=ins[a], dst_ref=slot(a, 2 * px + py), send_sem=send_sems.at[3 * a + k],
                                             recv_sem=recv_sems.at[3 * a + k], device_id=(px, py, c),
                                             device_id_type=MESH).wait_recv()
        for cp in sends:
            cp.wait_send()
        for loc in copies:
            loc.wait()

    out_shape = []
    for a, sh in enumerate(shards):
        full = list(sh.shape)
        full[axes[a]] *= N_CHIPS
        out_shape.append(jax.ShapeDtypeStruct(tuple(full), sh.dtype))
    return pl.pallas_call(
        body, name=name, in_specs=[ANY] * n, out_specs=[ANY] * n, out_shape=out_shape,
        scratch_shapes=[pltpu.SemaphoreType.DMA((3 * n,)), pltpu.SemaphoreType.DMA((3 * n,)), pltpu.SemaphoreType.DMA((n,))],
    )(*shards)


def scatter_partials(grads, *, name):
    n = len(grads)

    def body(*refs):
        ins, outs = refs[:n], refs[n:2 * n]
        send_sems, recv_sems = refs[2 * n:]
        x, y, c, chips = _place()
        sends = []
        for a in range(n):
            for k, (px, py) in enumerate(chips):
                cp = pltpu.make_async_remote_copy(src_ref=ins[a].at[2 * px + py], dst_ref=outs[a].at[k],
                                                  send_sem=send_sems.at[3 * a + k], recv_sem=recv_sems.at[3 * a + k],
                                                  device_id=(px, py, c), device_id_type=MESH)
                cp.start()
                sends.append(cp)
        for cp in sends:
            cp.wait_recv()
        for cp in sends:
            cp.wait_send()

    return pl.pallas_call(
        body, name=name, in_specs=[ANY] * n, out_specs=[ANY] * n,
        out_shape=[jax.ShapeDtypeStruct((3,) + g.shape[1:], g.dtype) for g in grads],
        scratch_shapes=[pltpu.SemaphoreType.DMA((3 * n,)), pltpu.SemaphoreType.DMA((3 * n,))],
    )(*grads)


def swap_with_sibling(arrs, *, name):
    n = len(arrs)

    def body(*refs):
        ins, outs = refs[:n], refs[n:2 * n]
        send_sems, recv_sems = refs[2 * n:]
        x, y, c, _ = _place()
        sends = []
        for a in range(n):
            cp = pltpu.make_async_remote_copy(src_ref=ins[a], dst_ref=outs[a], send_sem=send_sems.at[a],
                                              recv_sem=recv_sems.at[a], device_id=(x, y, 1 - c), device_id_type=MESH)
            cp.start()
            sends.append(cp)
        for cp in sends:
            cp.wait_recv()
        for cp in sends:
            cp.wait_send()

    return pl.pallas_call(
        body, name=name, in_specs=[ANY] * n, out_specs=[ANY] * n,
        out_shape=[jax.ShapeDtypeStruct(a.shape, a.dtype) for a in arrs],
        scratch_shapes=[pltpu.SemaphoreType.DMA((n,)), pltpu.SemaphoreType.DMA((n,))],
    )(*arrs)


def all_reduce_small(v, *, name):
    rows = v.shape[0]

    def body(v_ref, o_ref, gath, send_sems, recv_sems):
        x, y, c, _ = _place()
        me = 4 * x + 2 * y + c
        gath[me] = v_ref[...]
        sends = []
        for msk in range(1, N_DEV):
            peer = (x ^ (msk >> 2), y ^ ((msk >> 1) & 1), c ^ (msk & 1))
            cp = pltpu.make_async_remote_copy(src_ref=v_ref, dst_ref=gath.at[me], send_sem=send_sems.at[msk - 1],
                                              recv_sem=recv_sems.at[msk - 1], device_id=peer, device_id_type=MESH)
            cp.start()
            sends.append(cp)
        for msk in range(1, N_DEV):
            pltpu.make_async_remote_copy(src_ref=v_ref, dst_ref=gath.at[me ^ msk], send_sem=send_sems.at[msk - 1],
                                         recv_sem=recv_sems.at[msk - 1], device_id=(x, y, c), device_id_type=MESH).wait_recv()
        for cp in sends:
            cp.wait_send()
        tot = gath[0]
        for q in range(1, N_DEV):
            tot = tot + gath[q]
        o_ref[...] = tot

    vm = pl.BlockSpec(memory_space=pltpu.VMEM)
    return pl.pallas_call(
        body, name=name, in_specs=[vm], out_specs=vm, out_shape=jax.ShapeDtypeStruct(v.shape, F32),
        scratch_shapes=[pltpu.VMEM((N_DEV, rows, LANES), F32), pltpu.SemaphoreType.DMA((N_DEV - 1,)),
                        pltpu.SemaphoreType.DMA((N_DEV - 1,))],
    )(v)


def _ffn_fwd(xin, gain, w_up, w_conv, w_down, bl, tag):
    n = rmsnorm_fwd(xin, gain, name=f"{tag}_ffn_norm")
    u = matmul(n, w_up, name=f"{tag}_ffn_up")
    h = conv_silu_fwd(u, w_conv, bl, name=f"{tag}_ffn_conv")
    xout = matmul(h, w_down, out_dtype=F32, add=xin, name=f"{tag}_ffn_down")
    return xout, (n, u, h)


def _ffn_bwd(dxout, dxout_b, xin, saved, gain, w_up, w_conv, w_down, bl, tag, shard):
    n, u, h = saved
    dh = matmul(dxout_b, w_down, tb=True, name=f"{tag}_ffn_down_dx")
    g_down = matmul(h, dxout_b, ta=True, name=f"{tag}_ffn_down_dw")
    du, g_conv = conv_silu_bwd(u, w_conv, dh, bl, name=f"{tag}_ffn_conv_bwd")
    dn = matmul(du, w_up, tb=True, out_dtype=F32, name=f"{tag}_ffn_up_dx")
    g_up = matmul(n, du, ta=True, shard_cols=shard, name=f"{tag}_ffn_up_dw")
    dxin, dxin_b, g_norm = rmsnorm_bwd(xin, gain, dn, dxout, name=f"{tag}_ffn_norm_bwd")
    return dxin, dxin_b, g_norm, g_up, g_conv, g_down


def _mem_kv(mem2, gain, w_kv, tag):
    mn = rmsnorm_fwd(mem2, gain, name=f"{tag}_mem_norm")
    return mn, matmul(mn, w_kv, name=f"{tag}_mem_kv")


def _mem_kv_bwd(mem2, mn, gain, w_kv, dkvm, tag):
    g_kv = matmul(mn, dkvm, ta=True, name=f"{tag}_mem_kv_dw")
    dmn = matmul(dkvm, w_kv, tb=True, out_dtype=F32, name=f"{tag}_mem_kv_dx")
    _, _, g_norm = rmsnorm_bwd(mem2, gain, dmn, None, name=f"{tag}_mem_norm_bwd")
    return g_norm, g_kv


def local_step(x, mem, target, w, shard_of):
    bl, s, d = x.shape
    t = bl * s
    x0 = x.reshape(t, d)
    mem2 = mem.reshape(bl * MEM_LEN, d)
    tgt = target.reshape(t, d)
    g = {}

    n1 = rmsnorm_fwd(x0, w["a_norm_attn"], name="a_attn_norm")
    proj_a = matmul(n1, w["a_w_in"], name="a_w_in")
    mn_a, kvm_a = _mem_kv(mem2, w["a_norm_mem"], w["a_w_mem_kv"], "a")
    o_sb, o_sb_f32 = sb_fwd(proj_a, bl, name="a_sb_fwd")
    o_mem_a = mem_fwd(proj_a, kvm_a, bl, name="a_mem_fwd")
    cat_a = jnp.concatenate([o_sb, o_mem_a], axis=1)
    x1 = matmul(cat_a, w["a_w_out"], out_dtype=F32, add=x0, name="a_w_out")
    x2, ffn_a = _ffn_fwd(x1, w["a_norm_ffn"], w["a_ffn_up"], w["a_ffn_conv"], w["a_ffn_down"], bl, "a")

    nkv = rmsnorm_fwd(x2, w["kv_norm"], name="kv_norm")
    kv = matmul(nkv, w["w_kv_shared"], name="w_kv")
    n3 = rmsnorm_fwd(x2, w["b_norm_attn"], name="b_attn_norm")
    proj_b = matmul(n3, w["b_w_in"], name="b_w_in")
    mn_b, kvm_b = _mem_kv(mem2, w["b_norm_mem"], w["b_w_mem_kv"], "b")
    dil = [dil_fwd(proj_b, kv, gi, bl, name=f"b_dil_fwd{gi}") for gi in range(3)]
    os_, lses = [o for o, _ in dil], [l for _, l in dil]
    o_dil = dil_combine_fwd(os_, lses, name="b_dil_combine")
    o_mem_b = mem_fwd(proj_b, kvm_b, bl, name="b_mem_fwd")
    cat_b = jnp.concatenate([o_dil, o_mem_b], axis=1)
    x3 = matmul(cat_b, w["b_w_out"], out_dtype=F32, add=x2, name="b_w_out")
    x4, ffn_b = _ffn_fwd(x3, w["b_norm_ffn"], w["b_ffn_up"], w["b_ffn_conv"], w["b_ffn_down"], bl, "b")

    dx4, dx4b, g["final_norm"], lossvec = final_loss(x4, w["final_norm"], tgt, name="final_loss")

    dx3, dx3b, g["b_norm_ffn"], g["b_ffn_up"], g["b_ffn_conv"], g["b_ffn_down"] = _ffn_bwd(
        dx4, dx4b, x3, ffn_b, w["b_norm_ffn"], w["b_ffn_up"], w["b_ffn_conv"], w["b_ffn_down"], bl, "b",
        shard_of["b_ffn_up"])
    dcat_b = matmul(dx3b, w["b_w_out"], tb=True, name="b_w_out_dx")
    g["b_w_out"] = matmul(cat_b, dx3b, ta=True, name="b_w_out_dw")
    dog, dshift = dil_combine_bwd(os_, lses, dcat_b, name="b_dil_combine_bwd")
    dqs, dks, dvs = [], [], []
    for gi in range(3):
        dq_g, dk_g, dv_g = dil_bwd(proj_b, kv, lses[gi], dog, dshift, gi, bl, name=f"b_dil_bwd{gi}")
        dqs.append(dq_g), dks.append(dk_g), dvs.append(dv_g)
    dq_mem_b, dkvm_b = mem_bwd(proj_b, kvm_b, dcat_b, bl, name="b_mem_bwd")
    g["b_norm_mem"], g["b_w_mem_kv"] = _mem_kv_bwd(mem2, mn_b, w["b_norm_mem"], w["b_w_mem_kv"], dkvm_b, "b")
    dproj_b = jnp.concatenate(dqs + [dq_mem_b], axis=1)
    dn3 = matmul(dproj_b, w["b_w_in"], tb=True, out_dtype=F32, name="b_w_in_dx")
    g["b_w_in"] = matmul(n3, dproj_b, ta=True, name="b_w_in_dw")
    dx2, _, g["b_norm_attn"] = rmsnorm_bwd(x2, w["b_norm_attn"], dn3, dx3, name="b_attn_norm_bwd")
    dkv = jnp.concatenate(dks + dvs, axis=1).astype(BF)
    dnkv = matmul(dkv, w["w_kv_shared"], tb=True, out_dtype=F32, name="w_kv_dx")
    g["w_kv_shared"] = matmul(nkv, dkv, ta=True, shard_cols=shard_of["w_kv_shared"], name="w_kv_dw")
    dx2, dx2b, g["kv_norm"] = rmsnorm_bwd(x2, w["kv_norm"], dnkv, dx2, name="kv_norm_bwd")

    dx1, dx1b, g["a_norm_ffn"], g["a_ffn_up"], g["a_ffn_conv"], g["a_ffn_down"] = _ffn_bwd(
        dx2, dx2b, x1, ffn_a, w["a_norm_ffn"], w["a_ffn_up"], w["a_ffn_conv"], w["a_ffn_down"], bl, "a",
        shard_of["a_ffn_up"])
    dcat_a = matmul(dx1b, w["a_w_out"], tb=True, name="a_w_out_dx")
    g["a_w_out"] = matmul(cat_a, dx1b, ta=True, name="a_w_out_dw")
    dq_sb, dk_sb, dv_sb = sb_bwd(proj_a, o_sb_f32, dcat_a, bl, name="a_sb_bwd")
    dq_mem_a, dkvm_a = mem_bwd(proj_a, kvm_a, dcat_a, bl, name="a_mem_bwd")
    g["a_norm_mem"], g["a_w_mem_kv"] = _mem_kv_bwd(mem2, mn_a, w["a_norm_mem"], w["a_w_mem_kv"], dkvm_a, "a")
    dproj_a = jnp.concatenate([dq_sb, dk_sb.astype(BF), dv_sb.astype(BF), dq_mem_a], axis=1)
    dn1 = matmul(dproj_a, w["a_w_in"], tb=True, out_dtype=F32, name="a_w_in_dx")
    g["a_w_in"] = matmul(n1, dproj_a, ta=True, shard_cols=shard_of["a_w_in"], name="a_w_in_dw")
    dx0, _, g["a_norm_attn"] = rmsnorm_bwd(x0, w["a_norm_attn"], dn1, dx1, name="a_attn_norm_bwd")
    return lossvec, dx0, g


MATRICES = ("a_w_in", "a_w_out", "a_w_mem_kv", "a_ffn_up", "a_ffn_down", "w_kv_shared", "b_w_in", "b_w_out",
            "b_w_mem_kv", "b_ffn_up", "b_ffn_down")
COL_SHARDED = ("a_w_in", "a_ffn_up", "w_kv_shared", "b_ffn_up")
SMALL_SHARDED = ("a_norm_attn", "a_norm_mem", "a_norm_ffn", "a_ffn_conv", "b_ffn_conv")
SMALL_REPLICATED = ("kv_norm", "b_norm_attn", "b_norm_mem", "b_norm_ffn", "final_norm")
WEIGHTS = ("a_norm_attn", "a_w_in", "a_w_out", "a_norm_mem", "a_w_mem_kv", "a_norm_ffn", "a_ffn_up", "a_ffn_conv",
           "a_ffn_down", "kv_norm", "w_kv_shared", "b_norm_attn", "b_w_in", "b_w_out", "b_norm_mem", "b_w_mem_kv",
           "b_norm_ffn", "b_ffn_up", "b_ffn_conv", "b_ffn_down", "final_norm")


def _two_d(a):
    if a.ndim == 1:
        return a.reshape(1, -1)
    return a.reshape(a.shape[-2], a.shape[-1])


def kernel(x, mem, a_norm_attn, a_w_in, a_w_out, a_norm_mem, a_w_mem_kv, a_norm_ffn, a_ffn_up, a_ffn_conv, a_ffn_down, kv_norm, w_kv_shared, b_norm_attn, b_w_in, b_w_out, b_norm_mem, b_w_mem_kv, b_norm_ffn, b_ffn_up, b_ffn_conv, b_ffn_down, final_norm, loss_target, m_a_norm_attn, m_a_w_in, m_a_w_out, m_a_norm_mem, m_a_w_mem_kv, m_a_norm_ffn, m_a_ffn_up, m_a_ffn_conv, m_a_ffn_down, m_kv_norm, m_w_kv_shared, m_b_norm_attn, m_b_w_in, m_b_w_out, m_b_norm_mem, m_b_w_mem_kv, m_b_norm_ffn, m_b_ffn_up, m_b_ffn_conv, m_b_ffn_down, m_final_norm, v_a_norm_attn, v_a_w_in, v_a_w_out, v_a_norm_mem, v_a_w_mem_kv, v_a_norm_ffn, v_a_ffn_up, v_a_ffn_conv, v_a_ffn_down, v_kv_norm, v_w_kv_shared, v_b_norm_attn, v_b_w_in, v_b_w_out, v_b_norm_mem, v_b_w_mem_kv, v_b_norm_ffn, v_b_ffn_up, v_b_ffn_conv, v_b_ffn_down, v_final_norm):
    given = dict(locals())
    wl = {n: _two_d(given[n]) for n in WEIGHTS}
    ml = {n: _two_d(given["m_" + n]) for n in WEIGHTS}
    vl = {n: _two_d(given["v_" + n]) for n in WEIGHTS}
    chip = 2 * lax.axis_index("x") + lax.axis_index("y")

    packed = jnp.concatenate([wl[n].reshape(-1, LANES) for n in SMALL_SHARDED], axis=0)
    shards = [wl[n].astype(BF) for n in MATRICES] + [packed]
    axes = [1 if n in COL_SHARDED else 0 for n in MATRICES] + [0]
    full = gather_shards(shards, axes, name="gather_weights")
    w = dict(zip(MATRICES, full[:-1]))
    rows = packed.shape[0]
    per_chip = full[-1].reshape(N_CHIPS, rows, LANES)
    r0 = 0
    for n in SMALL_SHARDED:
        nr = wl[n].size // LANES
        piece = per_chip[:, r0:r0 + nr].reshape(N_CHIPS, wl[n].shape[0], wl[n].shape[1])
        w[n] = jnp.concatenate([piece[q] for q in range(N_CHIPS)], axis=1)
        r0 += nr
    for n in SMALL_REPLICATED:
        w[n] = wl[n]

    shard_of = {n: wl[n].shape[1] for n in COL_SHARDED}
    lossvec, dx0, g = local_step(x, mem, loss_target, w, shard_of)
    loss = lax.psum(0.5 * jnp.sum(lossvec) / x.shape[-1], ("x", "y", "c"))

    g4 = [g[n] if n in COL_SHARDED else g[n].reshape((N_CHIPS,) + wl[n].shape) for n in MATRICES]
    landed = scatter_partials(g4, name="scatter_grads")
    sums = [sum4(lax.dynamic_index_in_dim(g4[k], chip, 0, keepdims=False), landed[k], name=f"sum4_{n}")
            for k, n in enumerate(MATRICES)]
    theirs = swap_with_sibling(sums, name="swap_sums")
    out = {}
    for k, n in enumerate(MATRICES):
        out[n] = adamw(wl[n], sums[k], theirs[k], ml[n], vl[n], name=f"adamw_{n}")

    small = SMALL_SHARDED + SMALL_REPLICATED
    flat = jnp.concatenate([g[n].reshape(-1, LANES) for n in small], axis=0)
    tot = all_reduce_small(flat, name="all_reduce_small")
    r0 = 0
    for n in small:
        nr = g[n].size // LANES
        gn = tot[r0:r0 + nr].reshape(g[n].shape)
        r0 += nr
        if n in SMALL_SHARDED:
            gn = lax.dynamic_slice_in_dim(gn, chip * wl[n].shape[1], wl[n].shape[1], axis=1)
        out[n] = adamw(wl[n], gn, None, ml[n], vl[n], name=f"adamw_{n}")

    res = [loss, dx0.reshape(x.shape)]
    for slot in range(4):
        res += [out[n][slot].reshape(given[n].shape) for n in WEIGHTS]
    return tuple(res)
```

```python
import functools
import math

import jax
import jax.numpy as jnp
from jax import lax
from jax.experimental import pallas as pl
from jax.experimental.pallas import tpu as pltpu

F32 = jnp.float32
BF = jnp.bfloat16
MESH = pl.DeviceIdType.MESH

HEAD_DIM = 64
LANES = 128
SB_WIDTH = 12 * HEAD_DIM
MEM_WIDTH = 4 * HEAD_DIM
DIL_WIDTH = 12 * HEAD_DIM
MEM_LEN = 256
DIL_GROUPS = ((128, 1), (512, 4), (2048, 16))
QB = 128
EPS = 1e-6
SCALE = HEAD_DIM ** -0.5
NEG = -1e30
ALIBI = tuple(2.0 ** (-8.0 * i / 12) for i in range(1, 13))
N_CHIPS = 4
N_DEV = 8

ADAM_LR, ADAM_B1, ADAM_B2, ADAM_EPS, ADAM_WD, ADAM_STEP = 0.001, 0.9, 0.999, 1e-08, 0.01, 10

VMEM_LIMIT = 48 * 1024 * 1024


def _mo(v, m):
    return v if isinstance(v, int) else pl.multiple_of(v, m)


def _pick(n, prefs):
    for t in prefs:
        if n % t == 0:
            return t
    return n


def _params(*sem):
    return pltpu.CompilerParams(dimension_semantics=sem, vmem_limit_bytes=VMEM_LIMIT)


def matmul(a, b, *, ta=False, tb=False, out_dtype=BF, add=None, shard_cols=0, name):
    m, k = (a.shape[1], a.shape[0]) if ta else a.shape
    n = b.shape[0] if tb else b.shape[1]
    tm = _pick(m, (512, 1408, 256, 128))
    tk = _pick(k, (2048, 2816, 2560, 1536, 1024, 512, 256, 128))
    tn = shard_cols if shard_cols else _pick(n, (1024, 1408, 512, 256, 128))
    nk = k // tk
    dims = (((0,) if ta else (1,), (1,) if tb else (0,)), ((), ()))

    def body(*refs):
        a_ref, b_ref = refs[:2]
        add_ref = refs[2] if add is not None else None
        o_ref = refs[3 if add is not None else 2]
        part = lax.dot_general(a_ref[...].astype(BF), b_ref[...].astype(BF), dims, preferred_element_type=F32)

        def finish(r):
            if add is not None:
                r = r + add_ref[...]
            o_ref[...] = r.astype(o_ref.dtype)

        if nk == 1:
            finish(part)
            return
        acc_ref = refs[-1]
        kk = pl.program_id(2)

        @pl.when(kk == 0)
        def _():
            acc_ref[...] = part

        @pl.when(kk > 0)
        def _():
            acc_ref[...] += part

        @pl.when(kk == nk - 1)
        def _():
            finish(acc_ref[...])

    a_spec = pl.BlockSpec((tk, tm), lambda i, j, q: (q, i)) if ta else pl.BlockSpec((tm, tk), lambda i, j, q: (i, q))
    b_spec = pl.BlockSpec((tn, tk), lambda i, j, q: (j, q)) if tb else pl.BlockSpec((tk, tn), lambda i, j, q: (q, j))
    in_specs = [a_spec, b_spec]
    args = [a, b]
    if add is not None:
        in_specs.append(pl.BlockSpec((tm, tn), lambda i, j, q: (i, j)))
        args.append(add)
    if shard_cols:
        out_shape = jax.ShapeDtypeStruct((N_CHIPS, m, shard_cols), out_dtype)
        out_spec = pl.BlockSpec((None, tm, tn), lambda i, j, q: (j, i, 0))
    else:
        out_shape = jax.ShapeDtypeStruct((m, n), out_dtype)
        out_spec = pl.BlockSpec((tm, tn), lambda i, j, q: (i, j))
    return pl.pallas_call(
        body, name=name, grid=(m // tm, n // tn, nk), in_specs=in_specs, out_specs=out_spec, out_shape=out_shape,
        scratch_shapes=[pltpu.VMEM((tm, tn), F32)] if nk > 1 else [],
        compiler_params=_params("parallel", "parallel", "arbitrary"),
    )(*args)


def rmsnorm_fwd(x, g, *, name):
    t, d = x.shape
    tr = _pick(t, (512, 256))

    def body(x_ref, g_ref, o_ref):
        xv = x_ref[...]
        r = lax.rsqrt(jnp.mean(xv * xv, axis=-1, keepdims=True) + EPS)
        o_ref[...] = ((xv * r) * g_ref[...]).astype(o_ref.dtype)

    return pl.pallas_call(
        body, name=name, grid=(t // tr,),
        in_specs=[pl.BlockSpec((tr, d), lambda i: (i, 0)), pl.BlockSpec((1, d), lambda i: (0, 0))],
        out_specs=pl.BlockSpec((tr, d), lambda i: (i, 0)), out_shape=jax.ShapeDtypeStruct((t, d), BF),
        compiler_params=_params("parallel"),
    )(x, g)


def rmsnorm_bwd(x, g, dn, dres, *, name):
    t, d = x.shape
    tr = _pick(t, (512, 256))
    want_dx = dres is not None

    def body(*refs):
        if want_dx:
            x_ref, g_ref, dn_ref, dres_ref, dx_ref, dxb_ref, dg_ref = refs
        else:
            x_ref, g_ref, dn_ref, dg_ref = refs

        @pl.when(pl.program_id(0) == 0)
        def _():
            dg_ref[...] = jnp.zeros_like(dg_ref)

        xv = x_ref[...]
        r = lax.rsqrt(jnp.mean(xv * xv, axis=-1, keepdims=True) + EPS)
        xn = xv * r
        dnv = dn_ref[...].astype(F32)
        dg_ref[...] += jnp.sum(dnv * xn, axis=0, keepdims=True)
        if want_dx:
            dyg = dnv * g_ref[...]
            cm = jnp.mean(dyg * xn, axis=-1, keepdims=True)
            dx = dres_ref[...] + r * (dyg - xn * cm)
            dx_ref[...] = dx
            dxb_ref[...] = dx.astype(BF)

    row = pl.BlockSpec((tr, d), lambda i: (i, 0))
    vec = pl.BlockSpec((1, d), lambda i: (0, 0))
    if want_dx:
        return pl.pallas_call(
            body, name=name, grid=(t // tr,), in_specs=[row, vec, row, row], out_specs=[row, row, vec],
            out_shape=[jax.ShapeDtypeStruct((t, d), F32), jax.ShapeDtypeStruct((t, d), BF), jax.ShapeDtypeStruct((1, d), F32)],
            compiler_params=_params("arbitrary"),
        )(x, g, dn, dres)
    return None, None, pl.pallas_call(
        body, name=name, grid=(t // tr,), in_specs=[row, vec, row], out_specs=vec,
        out_shape=jax.ShapeDtypeStruct((1, d), F32), compiler_params=_params("arbitrary"),
    )(x, g, dn)


def final_loss(x, g, target, *, name):
    t, d = x.shape
    tr = _pick(t, (512, 256))

    def body(x_ref, g_ref, t_ref, dx_ref, dxb_ref, dg_ref, lv_ref):
        @pl.when(pl.program_id(0) == 0)
        def _():
            dg_ref[...] = jnp.zeros_like(dg_ref)
            lv_ref[...] = jnp.zeros_like(lv_ref)

        xv = x_ref[...]
        r = lax.rsqrt(jnp.mean(xv * xv, axis=-1, keepdims=True) + EPS)
        xn = xv * r
        err = xn * g_ref[...] - t_ref[...]
        lv_ref[...] += jnp.sum(err * err, axis=0, keepdims=True)
        dy = err * (1.0 / d)
        dg_ref[...] += jnp.sum(dy * xn, axis=0, keepdims=True)
        dyg = dy * g_ref[...]
        cm = jnp.mean(dyg * xn, axis=-1, keepdims=True)
        dx = r * (dyg - xn * cm)
        dx_ref[...] = dx
        dxb_ref[...] = dx.astype(BF)

    row = pl.BlockSpec((tr, d), lambda i: (i, 0))
    vec = pl.BlockSpec((1, d), lambda i: (0, 0))
    return pl.pallas_call(
        body, name=name, grid=(t // tr,), in_specs=[row, vec, row], out_specs=[row, row, vec, vec],
        out_shape=[jax.ShapeDtypeStruct((t, d), F32), jax.ShapeDtypeStruct((t, d), BF), jax.ShapeDtypeStruct((1, d), F32),
                   jax.ShapeDtypeStruct((1, d), F32)],
        compiler_params=_params("arbitrary"),
    )(x, g, target)


CONV_ROWS = 256
HALO = 16


def _conv_taps(ext, w, rows):
    s0 = ext[HALO:HALO + rows]
    s1 = pltpu.roll(ext, 1, 0)[HALO:HALO + rows]
    s2 = pltpu.roll(ext, 2, 0)[HALO:HALO + rows]
    return (w[0:1] * s2 + w[1:2] * s1) + w[2:3] * s0, s0, s1, s2


def conv_silu_fwd(u, w, bl, *, name):
    t, f2 = u.shape
    f = f2 // 2
    s = t // bl
    tc = _pick(f, (256, 128))
    nf = f // tc
    nch = s // CONV_ROWS

    def body(ua_ref, ug_ref, wa_ref, wg_ref, h_ref):
        wa, wg = wa_ref[...], wg_ref[...]

        def chunk(ci, carry):
            r0 = pl.multiple_of(ci * CONV_ROWS, CONV_ROWS)
            ps = pl.multiple_of(jnp.maximum(r0 - HALO, 0), HALO)
            keep = (ci > 0).astype(F32)

            def conv(ref, wv):
                ext = jnp.concatenate([ref[0, pl.ds(ps, HALO), :].astype(F32) * keep,
                                       ref[0, pl.ds(r0, CONV_ROWS), :].astype(F32)], axis=0)
                return _conv_taps(ext, wv, CONV_ROWS)[0]

            ca, cg = conv(ua_ref, wa), conv(ug_ref, wg)
            sg = 1.0 / (1.0 + jnp.exp(-cg))
            h_ref[0, pl.ds(r0, CONV_ROWS), :] = ((cg * sg) * ca).astype(h_ref.dtype)
            return carry

        lax.fori_loop(0, nch, chunk, 0)

    u3 = u.reshape(bl, s, f2)
    blk = lambda off: pl.BlockSpec((1, s, tc), lambda j, b: (b, 0, j + off))
    wblk = lambda off: pl.BlockSpec((3, tc), lambda j, b: (0, j + off))
    h = pl.pallas_call(
        body, name=name, grid=(nf, bl), in_specs=[blk(0), blk(nf), wblk(0), wblk(nf)],
        out_specs=pl.BlockSpec((1, s, tc), lambda j, b: (b, 0, j)), out_shape=jax.ShapeDtypeStruct((bl, s, f), BF),
        compiler_params=_params("parallel", "parallel"),
    )(u3, u3, w, w)
    return h.reshape(t, f)


def conv_silu_bwd(u, w, dh, bl, *, name):
    t, f2 = u.shape
    f = f2 // 2
    s = t // bl
    tc = _pick(f, (256, 128))
    nf = f // tc
    nch = s // CONV_ROWS
    ext_rows = CONV_ROWS + HALO

    def body(ua_ref, ug_ref, wa_ref, wg_ref, dh_ref, dua_ref, dug_ref, dwa_ref, dwg_ref):
        wa, wg = wa_ref[...], wg_ref[...]

        @pl.when(pl.program_id(1) == 0)
        def _():
            dwa_ref[...] = jnp.zeros_like(dwa_ref)
            dwg_ref[...] = jnp.zeros_like(dwg_ref)

        def chunk(ci, carry):
            r0 = pl.multiple_of(ci * CONV_ROWS, CONV_ROWS)
            ps = pl.multiple_of(jnp.maximum(r0 - HALO, 0), HALO)
            ns = pl.multiple_of(jnp.minimum(r0 + CONV_ROWS, s - HALO), HALO)
            keep_p = (ci > 0).astype(F32)
            keep_n = (ci < nch - 1).astype(F32)

            def ext_of(ref):
                return jnp.concatenate([ref[0, pl.ds(ps, HALO), :].astype(F32) * keep_p,
                                        ref[0, pl.ds(r0, CONV_ROWS), :].astype(F32),
                                        ref[0, pl.ds(ns, HALO), :].astype(F32) * keep_n], axis=0)

            ca, a0, a1, a2 = _conv_taps(ext_of(ua_ref), wa, ext_rows)
            cg, g0, g1, g2 = _conv_taps(ext_of(ug_ref), wg, ext_rows)
            dhe = jnp.concatenate([dh_ref[0, pl.ds(r0, CONV_ROWS), :].astype(F32),
                                   dh_ref[0, pl.ds(ns, HALO), :].astype(F32) * keep_n], axis=0)
            sg = 1.0 / (1.0 + jnp.exp(-cg))
            dca = dhe * (cg * sg)
            dcg = dhe * ca * (sg * (1.0 + cg * (1.0 - sg)))

            def back(dc, wv):
                n1 = pltpu.roll(dc, ext_rows - 1, 0)
                n2 = pltpu.roll(dc, ext_rows - 2, 0)
                return ((wv[2:3] * dc + wv[1:2] * n1) + wv[0:1] * n2)[:CONV_ROWS]

            dua_ref[0, pl.ds(r0, CONV_ROWS), :] = back(dca, wa).astype(dua_ref.dtype)
            dug_ref[0, pl.ds(r0, CONV_ROWS), :] = back(dcg, wg).astype(dug_ref.dtype)

            def wsum(dc, taps):
                d0 = dc[:CONV_ROWS]
                return [jnp.sum(d0 * tp[:CONV_ROWS], axis=0, keepdims=True) for tp in taps]

            sa = wsum(dca, (a2, a1, a0))
            sgs = wsum(dcg, (g2, g1, g0))
            for j in range(3):
                dwa_ref[j:j + 1, :] += sa[j]
                dwg_ref[j:j + 1, :] += sgs[j]
            return carry

        lax.fori_loop(0, nch, chunk, 0)

    u3 = u.reshape(bl, s, f2)
    dh3 = dh.reshape(bl, s, f)
    blk = lambda off: pl.BlockSpec((1, s, tc), lambda j, b: (b, 0, j + off))
    wblk = lambda off: pl.BlockSpec((3, tc), lambda j, b: (0, j + off))
    dua, dug, dwa, dwg = pl.pallas_call(
        body, name=name, grid=(nf, bl), in_specs=[blk(0), blk(nf), wblk(0), wblk(nf), blk(0)],
        out_specs=[blk(0), blk(0), wblk(0), wblk(0)],
        out_shape=[jax.ShapeDtypeStruct((bl, s, f), BF), jax.ShapeDtypeStruct((bl, s, f), BF),
                   jax.ShapeDtypeStruct((3, f), F32), jax.ShapeDtypeStruct((3, f), F32)],
        compiler_params=_params("parallel", "arbitrary"),
    )(u3, u3, w, w, dh3)
    du = jnp.concatenate([dua.reshape(t, f), dug.reshape(t, f)], axis=1)
    return du, jnp.concatenate([dwa, dwg], axis=1)


def _lane_masks(rows):
    lane = lax.broadcasted_iota(jnp.int32, (rows, LANES), 1)
    return lane < HEAD_DIM, lane >= HEAD_DIM


def _nt(a, b):
    return lax.dot_general(a, b, (((1,), (1,)), ((), ())), preferred_element_type=F32)


def _tn(a, b):
    return lax.dot_general(a, b, (((0,), (0,)), ((), ())), preferred_element_type=F32)


def _nn(a, b):
    return jnp.dot(a, b, preferred_element_type=F32)


def _suffix_ones():
    j = lax.broadcasted_iota(jnp.int32, (2 * QB, QB), 0) % QB
    s = lax.broadcasted_iota(jnp.int32, (2 * QB, QB), 1)
    return (j >= s).astype(BF)


def _suffix_sum(v, uu):
    hi = v.astype(BF)
    lo = (v - hi.astype(F32)).astype(BF)
    return _nn(jnp.concatenate([hi, lo], axis=1), uu)


def _softplus(z):
    return jnp.maximum(z, 0.0) + jnp.log(1.0 + jnp.exp(-jnp.abs(z)))


SB_BLOCK = 256
SB_DEAD = -100.0


def _sb_rel():
    return (lax.broadcasted_iota(jnp.int32, (SB_BLOCK, SB_BLOCK), 1)
            - lax.broadcasted_iota(jnp.int32, (SB_BLOCK, SB_BLOCK), 0))


def _sb_weights(qh, kb, causal, carry, uu):
    z = _nt(qh, kb) * SCALE
    ls = jnp.where(causal, -_softplus(z), 0.0)
    nsub = SB_BLOCK // QB
    cins = [None] * nsub
    offs = [None] * nsub
    for u in reversed(range(nsub)):
        cins[u] = _suffix_sum(ls[:, u * QB:(u + 1) * QB], uu)
        offs[u] = carry
        carry = carry + cins[u][:, 0:1]
    e = jnp.concatenate([z[:, u * QB:(u + 1) * QB] + cins[u] + offs[u] for u in range(nsub)], axis=1)
    return z, jnp.where(causal, jnp.exp(e), 0.0), carry, ls


def _sb_alive(ca, cb):
    return (jnp.maximum(jnp.max(ca), jnp.max(cb)) > SB_DEAD).astype(jnp.int32)


def _ride_hooks(ride, refs, n_in, n_out, grid):
    if ride is None:
        return refs[:n_in], refs[n_in:n_in + n_out], lambda: None, lambda: None
    ni, no = len(ride.arrays), len(ride.out_shape)
    own_in, rin = refs[:n_in], refs[n_in:n_in + ni]
    own_out, rout = refs[n_in + ni:n_in + ni + n_out], refs[n_in + ni + n_out:n_in + ni + n_out + no]
    sems = refs[n_in + ni + n_out + no:]
    ids = [pl.program_id(a) for a in range(len(grid))]

    def start():
        first = functools.reduce(lambda u, v: u & v, [i == 0 for i in ids])
        pl.when(first)(lambda: ride.start(rin, rout, sems))

    def wait():
        last = functools.reduce(lambda u, v: u & v, [i == n - 1 for i, n in zip(ids, grid)])
        pl.when(last)(lambda: ride.wait(rin, rout, sems))

    return own_in, own_out, start, wait


def sb_fwd(proj, bl, *, name, ride=None):
    t, width = proj.shape
    s = t // bl
    npair = SB_WIDTH // LANES
    nq = s // SB_BLOCK
    grid = (bl, npair, nq)

    def body(*refs):
        (q_ref, k_ref, v_ref), (o_ref, of_ref), ride_start, ride_wait = _ride_hooks(ride, refs, 3, 2, grid)
        ride_start()
        i = pl.program_id(2)
        uu = _suffix_ones()
        ma, mb = _lane_masks(SB_BLOCK)
        q = q_ref[0]
        rel = _sb_rel()

        def chunk(c, qh, carry, acc):
            start = pl.multiple_of(c * SB_BLOCK, SB_BLOCK)
            kb = k_ref[0, pl.ds(start, SB_BLOCK), :]
            vb = v_ref[0, pl.ds(start, SB_BLOCK), :]
            causal = rel < (i - c) * SB_BLOCK
            z, w, carry, _ = _sb_weights(qh, kb, causal, carry, uu)
            return carry, acc + _nn(w.astype(BF), vb)

        def live(c, st):
            _, ca, acca, cb, accb = st
            ca, acca = chunk(c, jnp.where(ma, q, jnp.zeros_like(q)), ca, acca)
            cb, accb = chunk(c, jnp.where(mb, q, jnp.zeros_like(q)), cb, accb)
            return _sb_alive(ca, cb), ca, acca, cb, accb

        def step(n, st):
            return lax.cond(st[0] > 0, functools.partial(live, i - n), lambda st: st, st)

        zc = jnp.zeros((SB_BLOCK, 1), F32)
        za = jnp.zeros((SB_BLOCK, LANES), F32)
        st = lax.fori_loop(0, i + 1, step, (jnp.int32(1), zc, za, zc, za))
        both = jnp.where(ma, st[2], st[4])
        o_ref[0] = both.astype(o_ref.dtype)
        of_ref[0] = both
        ride_wait()

    p3 = proj.reshape(bl, s, width)
    qblk = pl.BlockSpec((1, SB_BLOCK, LANES), lambda b, p, i: (b, i, p))
    extra = ride.arrays if ride else []
    o, of, *delivered = pl.pallas_call(
        body, name=name, grid=grid,
        in_specs=[qblk, pl.BlockSpec((1, s, LANES), lambda b, p, i: (b, 0, npair + p)),
                  pl.BlockSpec((1, s, LANES), lambda b, p, i: (b, 0, 2 * npair + p))] + [ANY] * len(extra),
        out_specs=[qblk, qblk] + [ANY] * (len(ride.out_shape) if ride else 0),
        out_shape=[jax.ShapeDtypeStruct((bl, s, SB_WIDTH), BF), jax.ShapeDtypeStruct((bl, s, SB_WIDTH), F32)]
        + (ride.out_shape if ride else []),
        scratch_shapes=ride.scratch if ride else [],
        compiler_params=_params("arbitrary", "arbitrary", "arbitrary"),
    )(p3, p3, p3, *extra)
    return o.reshape(t, SB_WIDTH), of.reshape(t, SB_WIDTH), delivered


def sb_bwd(proj, o, dcat, bl, *, name, ride=None):
    t, width = proj.shape
    s = t // bl
    npair = SB_WIDTH // LANES
    nq = s // SB_BLOCK
    nsub = SB_BLOCK // QB
    grid = (bl, npair, nq)

    def body(*refs):
        (q_ref, k_ref, v_ref, o_ref, do_ref), (dq_ref, dk_ref, dv_ref), ride_start, ride_wait = _ride_hooks(
            ride, refs, 5, 3, grid)
        ride_start()
        i = pl.program_id(2)

        @pl.when(i == 0)
        def _():
            dk_ref[...] = jnp.zeros_like(dk_ref)
            dv_ref[...] = jnp.zeros_like(dv_ref)

        uu = _suffix_ones()
        ma, mb = _lane_masks(SB_BLOCK)
        q = q_ref[0]
        do = do_ref[0]
        prod = do.astype(F32) * o_ref[0]
        rel = _sb_rel()

        def chunk(c, m, carry_c, carry_s, dq):
            qh = jnp.where(m, q, jnp.zeros_like(q))
            doh = jnp.where(m, do, jnp.zeros_like(do))
            dsum = jnp.sum(jnp.where(m, prod, 0.0), axis=1, keepdims=True)
            start = pl.multiple_of(c * SB_BLOCK, SB_BLOCK)
            kb = k_ref[0, pl.ds(start, SB_BLOCK), :]
            vb = v_ref[0, pl.ds(start, SB_BLOCK), :]
            causal = rel < (i - c) * SB_BLOCK
            z, w, carry_c, ls = _sb_weights(qh, kb, causal, carry_c, uu)
            wb = w.astype(BF)
            da = wb.astype(F32) * _nt(doh, vb)
            pres = [None] * nsub
            for u in reversed(range(nsub)):
                dau = da[:, u * QB:(u + 1) * QB]
                sin = _suffix_sum(dau, uu)
                pres[u] = dsum - (sin - dau + carry_s)
                carry_s = carry_s + sin[:, 0:1]
            pre = jnp.concatenate(pres, axis=1)
            dz = jnp.where(causal, da - jnp.exp(z + ls) * pre, 0.0)
            dzs = (dz * SCALE).astype(BF)
            dk_ref[0, pl.ds(start, SB_BLOCK), :] += _tn(dzs, qh)
            dv_ref[0, pl.ds(start, SB_BLOCK), :] += _tn(wb, doh)
            return carry_c, carry_s, dq + _nn(dzs, kb)

        def live(c, st):
            _, sa, sb = st
            sa = chunk(c, ma, *sa)
            sb = chunk(c, mb, *sb)
            return _sb_alive(sa[0], sb[0]), sa, sb

        def step(n, st):
            return lax.cond(st[0] > 0, functools.partial(live, i - n), lambda st: st, st)

        zc = jnp.zeros((SB_BLOCK, 1), F32)
        init = (zc, zc, jnp.zeros((SB_BLOCK, LANES), F32))
        st = lax.fori_loop(0, i + 1, step, (jnp.int32(1), init, init))
        dq_ref[0] = jnp.where(ma, st[1][2], st[2][2]).astype(dq_ref.dtype)
        ride_wait()

    p3 = proj.reshape(bl, s, width)
    o3 = o.reshape(bl, s, SB_WIDTH)
    d3 = dcat.reshape(bl, s, dcat.shape[1])
    qblk = pl.BlockSpec((1, SB_BLOCK, LANES), lambda b, p, i: (b, i, p))
    full = pl.BlockSpec((1, s, LANES), lambda b, p, i: (b, 0, p))
    extra = ride.arrays if ride else []
    dq, dk, dv, *delivered = pl.pallas_call(
        body, name=name, grid=grid,
        in_specs=[qblk, pl.BlockSpec((1, s, LANES), lambda b, p, i: (b, 0, npair + p)),
                  pl.BlockSpec((1, s, LANES), lambda b, p, i: (b, 0, 2 * npair + p)), qblk, qblk] + [ANY] * len(extra),
        out_specs=[qblk, full, full] + [ANY] * (len(ride.out_shape) if ride else 0),
        out_shape=[jax.ShapeDtypeStruct((bl, s, SB_WIDTH), BF), jax.ShapeDtypeStruct((bl, s, SB_WIDTH), F32),
                   jax.ShapeDtypeStruct((bl, s, SB_WIDTH), F32)] + (ride.out_shape if ride else []),
        scratch_shapes=ride.scratch if ride else [],
        compiler_params=_params("arbitrary", "arbitrary", "arbitrary"),
    )(p3, p3, p3, o3, d3, *extra)
    return dq.reshape(t, SB_WIDTH), dk.reshape(t, SB_WIDTH), dv.reshape(t, SB_WIDTH), delivered


MEM_TQ = 512


def mem_fwd(proj, kvm, bl, *, name):
    t, width = proj.shape
    s = t // bl
    qoff = (width - MEM_WIDTH) // LANES
    npair = MEM_WIDTH // LANES

    def body(q_ref, k_ref, v_ref, o_ref):
        ma, mb = _lane_masks(MEM_TQ)
        q = q_ref[...]
        outs = []
        for m in (ma, mb):
            qh = jnp.where(m, q, jnp.zeros_like(q))
            sc = _nt(qh, k_ref[...]) * SCALE
            p = jnp.exp(sc - jnp.max(sc, axis=-1, keepdims=True))
            p = p / jnp.sum(p, axis=-1, keepdims=True)
            outs.append(_nn(p.astype(BF), v_ref[...]))
        o_ref[...] = jnp.where(ma, outs[0], outs[1]).astype(o_ref.dtype)

    nt = s // MEM_TQ
    return pl.pallas_call(
        body, name=name, grid=(bl, npair, nt),
        in_specs=[pl.BlockSpec((MEM_TQ, LANES), lambda b, p, i: (b * nt + i, qoff + p)),
                  pl.BlockSpec((MEM_LEN, LANES), lambda b, p, i: (b, p)),
                  pl.BlockSpec((MEM_LEN, LANES), lambda b, p, i: (b, npair + p))],
        out_specs=pl.BlockSpec((MEM_TQ, LANES), lambda b, p, i: (b * nt + i, p)),
        out_shape=jax.ShapeDtypeStruct((t, MEM_WIDTH), BF),
        compiler_params=_params("parallel", "parallel", "parallel"),
    )(proj, kvm, kvm)


def mem_bwd(proj, kvm, dcat, bl, *, name):
    t, width = proj.shape
    s = t // bl
    qoff = (width - MEM_WIDTH) // LANES
    doff = (dcat.shape[1] - MEM_WIDTH) // LANES
    npair = MEM_WIDTH // LANES
    nt = s // MEM_TQ

    def body(q_ref, k_ref, v_ref, do_ref, dq_ref, dk_ref, dv_ref):
        @pl.when(pl.program_id(2) == 0)
        def _():
            dk_ref[...] = jnp.zeros_like(dk_ref)
            dv_ref[...] = jnp.zeros_like(dv_ref)

        ma, mb = _lane_masks(MEM_TQ)
        q = q_ref[...]
        do = do_ref[...]
        kb, vb = k_ref[...], v_ref[...]
        dqs = []
        for m in (ma, mb):
            qh = jnp.where(m, q, jnp.zeros_like(q))
            doh = jnp.where(m, do, jnp.zeros_like(do))
            sc = _nt(qh, kb) * SCALE
            p = jnp.exp(sc - jnp.max(sc, axis=-1, keepdims=True))
            p = p / jnp.sum(p, axis=-1, keepdims=True)
            dp = _nt(doh, vb)
            ds = p * (dp - jnp.sum(p * dp, axis=-1, keepdims=True))
            dss = (ds * SCALE).astype(BF)
            dk_ref[...] += _tn(dss, qh)
            dv_ref[...] += _tn(p.astype(BF), doh)
            dqs.append(_nn(dss, kb))
        dq_ref[...] = jnp.where(ma, dqs[0], dqs[1]).astype(dq_ref.dtype)

    kblk = pl.BlockSpec((MEM_LEN, LANES), lambda b, p, i: (b, p))
    dq, dk, dv = pl.pallas_call(
        body, name=name, grid=(bl, npair, nt),
        in_specs=[pl.BlockSpec((MEM_TQ, LANES), lambda b, p, i: (b * nt + i, qoff + p)), kblk,
                  pl.BlockSpec((MEM_LEN, LANES), lambda b, p, i: (b, npair + p)),
                  pl.BlockSpec((MEM_TQ, LANES), lambda b, p, i: (b * nt + i, doff + p))],
        out_specs=[pl.BlockSpec((MEM_TQ, LANES), lambda b, p, i: (b * nt + i, p)), kblk, kblk],
        out_shape=[jax.ShapeDtypeStruct((t, MEM_WIDTH), BF), jax.ShapeDtypeStruct((bl * MEM_LEN, MEM_WIDTH), F32),
                   jax.ShapeDtypeStruct((bl * MEM_LEN, MEM_WIDTH), F32)],
        compiler_params=_params("parallel", "parallel", "arbitrary"),
    )(proj, kvm, kvm, dcat)
    return dq, jnp.concatenate([dk, dv], axis=1).astype(BF)


def _dil_scores(qh, kb, slope_d, n_keys):
    i = lax.broadcasted_iota(jnp.int32, (QB, n_keys), 0)
    j = lax.broadcasted_iota(jnp.int32, (QB, n_keys), 1)
    delta = i + (n_keys - QB) - j
    valid = (delta >= 0) & (delta <= QB)
    sc = _nt(qh, kb) * SCALE - slope_d * delta.astype(F32)
    return jnp.where(valid, sc, NEG)


def _dil_slopes(g, dil):
    p = pl.program_id(2)
    sa = jnp.where(p == 0, ALIBI[4 * g] * dil, ALIBI[4 * g + 2] * dil).astype(F32)
    sb = jnp.where(p == 0, ALIBI[4 * g + 1] * dil, ALIBI[4 * g + 3] * dil).astype(F32)
    return sa, sb


def dil_fwd(projb, kv, g, bl, *, name):
    _, dil = DIL_GROUPS[g]
    t, wq = projb.shape
    wk = kv.shape[1]
    s = t // bl
    ln = s // dil
    nb = ln // QB
    gw = 4 * HEAD_DIM
    cq, ck, co = wq // LANES, wk // LANES, gw // LANES

    def body(q_ref, k_ref, v_ref, o_ref, lse_ref):
        sa, sb = _dil_slopes(g, dil)
        ma, mb = _lane_masks(QB)

        def block(n, n_keys):
            q0 = _mo(n * QB, QB)
            k0 = _mo(q0 - (n_keys - QB), QB)
            q = q_ref[0, pl.ds(q0, QB), :]
            kb = k_ref[0, pl.ds(k0, n_keys), :]
            vb = v_ref[0, pl.ds(k0, n_keys), :]
            outs, lses = [], []
            for m, sl in ((ma, sa), (mb, sb)):
                qh = jnp.where(m, q, jnp.zeros_like(q))
                sc = _dil_scores(qh, kb, sl, n_keys)
                mx = jnp.max(sc, axis=-1, keepdims=True)
                p = jnp.exp(sc - mx)
                den = jnp.sum(p, axis=-1, keepdims=True)
                outs.append(_nn(p.astype(BF), vb) / den)
                lses.append(mx + jnp.log(den))
            o_ref[0, pl.ds(q0, QB), :] = jnp.where(ma, outs[0], outs[1])
            lse_ref[0, pl.ds(q0, QB), :] = jnp.where(ma, lses[0], lses[1])

        block(0, QB)
        if nb > 1:
            def step(n, c):
                block(n, 2 * QB)
                return c
            lax.fori_loop(1, nb, step, 0)

    q3 = projb.reshape(bl, ln, dil * wq)
    k3 = kv.reshape(bl, ln, dil * wk)
    oblk = pl.BlockSpec((1, ln, LANES), lambda b, r, p: (b, 0, r * co + p))
    o, lse = pl.pallas_call(
        body, name=name, grid=(bl, dil, co),
        in_specs=[pl.BlockSpec((1, ln, LANES), lambda b, r, p: (b, 0, r * cq + g * co + p)),
                  pl.BlockSpec((1, ln, LANES), lambda b, r, p: (b, 0, r * ck + g * co + p)),
                  pl.BlockSpec((1, ln, LANES), lambda b, r, p: (b, 0, r * ck + ck // 2 + g * co + p))],
        out_specs=[oblk, oblk],
        out_shape=[jax.ShapeDtypeStruct((bl, ln, dil * gw), F32), jax.ShapeDtypeStruct((bl, ln, dil * gw), F32)],
        compiler_params=_params("parallel", "parallel", "parallel"),
    )(q3, k3, k3)
    return o.reshape(t, gw), lse.reshape(t, gw)


def dil_bwd(projb, kv, lse, dog, dshift, g, bl, *, name):
    _, dil = DIL_GROUPS[g]
    t, wq = projb.shape
    wk = kv.shape[1]
    s = t // bl
    ln = s // dil
    nb = ln // QB
    gw = 4 * HEAD_DIM
    cq, ck, co, cd = wq // LANES, wk // LANES, gw // LANES, DIL_WIDTH // LANES

    def body(q_ref, k_ref, v_ref, lse_ref, do_ref, sh_ref, dq_ref, dk_ref, dv_ref):
        sa, sb = _dil_slopes(g, dil)
        ma, mb = _lane_masks(QB)
        dk_ref[...] = jnp.zeros_like(dk_ref)
        dv_ref[...] = jnp.zeros_like(dv_ref)

        def block(n, n_keys):
            q0 = _mo(n * QB, QB)
            k0 = _mo(q0 - (n_keys - QB), QB)
            q = q_ref[0, pl.ds(q0, QB), :]
            do = do_ref[0, pl.ds(q0, QB), :]
            lse_b = lse_ref[0, pl.ds(q0, QB), :]
            sh_b = sh_ref[0, pl.ds(q0, QB), :]
            kb = k_ref[0, pl.ds(k0, n_keys), :]
            vb = v_ref[0, pl.ds(k0, n_keys), :]
            dqs = []
            for m, sl, c0 in ((ma, sa, 0), (mb, sb, HEAD_DIM)):
                qh = jnp.where(m, q, jnp.zeros_like(q))
                doh = jnp.where(m, do, jnp.zeros_like(do))
                sc = _dil_scores(qh, kb, sl, n_keys)
                p = jnp.exp(sc - lse_b[:, c0:c0 + 1])
                ds = p * (_nt(doh, vb) - sh_b[:, c0:c0 + 1])
                dss = (ds * SCALE).astype(BF)
                dk_ref[0, pl.ds(k0, n_keys), :] += _tn(dss, qh)
                dv_ref[0, pl.ds(k0, n_keys), :] += _tn(p.astype(BF), doh)
                dqs.append(_nn(dss, kb))
            dq_ref[0, pl.ds(q0, QB), :] = jnp.where(ma, dqs[0], dqs[1]).astype(dq_ref.dtype)

        block(0, QB)
        if nb > 1:
            def step(n, c):
                block(n, 2 * QB)
                return c
            lax.fori_loop(1, nb, step, 0)

    q3 = projb.reshape(bl, ln, dil * wq)
    k3 = kv.reshape(bl, ln, dil * wk)
    l3 = lse.reshape(bl, ln, dil * gw)
    d3 = dog.reshape(bl, ln, dil * DIL_WIDTH)
    s3 = dshift.reshape(bl, ln, dil * DIL_WIDTH)
    oblk = pl.BlockSpec((1, ln, LANES), lambda b, r, p: (b, 0, r * co + p))
    dblk = pl.BlockSpec((1, ln, LANES), lambda b, r, p: (b, 0, r * cd + g * co + p))
    dq, dk, dv = pl.pallas_call(
        body, name=name, grid=(bl, dil, co),
        in_specs=[pl.BlockSpec((1, ln, LANES), lambda b, r, p: (b, 0, r * cq + g * co + p)),
                  pl.BlockSpec((1, ln, LANES), lambda b, r, p: (b, 0, r * ck + g * co + p)),
                  pl.BlockSpec((1, ln, LANES), lambda b, r, p: (b, 0, r * ck + ck // 2 + g * co + p)),
                  oblk, dblk, dblk],
        out_specs=[oblk, oblk, oblk],
        out_shape=[jax.ShapeDtypeStruct((bl, ln, dil * gw), BF), jax.ShapeDtypeStruct((bl, ln, dil * gw), F32),
                   jax.ShapeDtypeStruct((bl, ln, dil * gw), F32)],
        compiler_params=_params("parallel", "parallel", "parallel"),
    )(q3, k3, k3, l3, d3, s3)
    return dq.reshape(t, gw), dk.reshape(t, gw), dv.reshape(t, gw)


def _group_weights(lses):
    mx = jnp.maximum(jnp.maximum(lses[0], lses[1]), lses[2])
    es = [jnp.exp(l - mx) for l in lses]
    tot = es[0] + es[1] + es[2]
    return [e / tot for e in es]


def dil_combine_fwd(os_, lses, *, name):
    t, gw = os_[0].shape
    tr = _pick(t, (512, 256))

    def body(o0, o1, o2, l0, l1, l2, out_ref):
        al = _group_weights([l0[...], l1[...], l2[...]])
        for g, o_ref in enumerate((o0, o1, o2)):
            out_ref[:, g * gw:(g + 1) * gw] = (o_ref[...] * al[g]).astype(out_ref.dtype)

    blk = pl.BlockSpec((tr, gw), lambda i: (i, 0))
    return pl.pallas_call(
        body, name=name, grid=(t // tr,), in_specs=[blk] * 6,
        out_specs=pl.BlockSpec((tr, 3 * gw), lambda i: (i, 0)), out_shape=jax.ShapeDtypeStruct((t, 3 * gw), BF),
        compiler_params=_params("parallel"),
    )(*os_, *lses)


def dil_combine_bwd(os_, lses, dcat, *, name):
    t, gw = os_[0].shape
    tr = _pick(t, (512, 256))

    def head_sum(v):
        parts = []
        for c in range(gw // LANES):
            blk = v[:, c * LANES:(c + 1) * LANES]
            ma, _ = _lane_masks(tr)
            sa = jnp.sum(jnp.where(ma, blk, 0.0), axis=1, keepdims=True)
            sb = jnp.sum(blk, axis=1, keepdims=True) - sa
            parts.append(jnp.where(ma, sa, sb))
        return jnp.concatenate(parts, axis=1)

    def body(o0, o1, o2, l0, l1, l2, d_ref, dog_ref, sh_ref):
        al = _group_weights([l0[...], l1[...], l2[...]])
        dos = [d_ref[:, g * gw:(g + 1) * gw].astype(F32) for g in range(3)]
        dal = [head_sum(dos[g] * o_ref[...]) for g, o_ref in enumerate((o0, o1, o2))]
        mix = al[0] * dal[0] + al[1] * dal[1] + al[2] * dal[2]
        for g in range(3):
            dog_ref[:, g * gw:(g + 1) * gw] = (al[g] * dos[g]).astype(dog_ref.dtype)
            sh_ref[:, g * gw:(g + 1) * gw] = al[g] * mix

    blk = pl.BlockSpec((tr, gw), lambda i: (i, 0))
    wide = pl.BlockSpec((tr, 3 * gw), lambda i: (i, 0))
    return pl.pallas_call(
        body, name=name, grid=(t // tr,), in_specs=[blk] * 6 + [wide], out_specs=[wide, wide],
        out_shape=[jax.ShapeDtypeStruct((t, 3 * gw), BF), jax.ShapeDtypeStruct((t, 3 * gw), F32)],
        compiler_params=_params("parallel"),
    )(*os_, *lses, dcat)


def adamw(w, g1, g2, m, v, *, name):
    r, c = w.shape
    tr = r
    for cand in (256, 128, 64, 32, 16, 8):
        if r % cand == 0 and cand * c * 4 <= (1 << 20):
            tr = cand
            break
    two = g2 is not None

    def body(*refs):
        if two:
            w_ref, g1_ref, g2_ref, m_ref, v_ref, g_ref, d_ref, nm_ref, nv_ref = refs
            g = g1_ref[...] + g2_ref[...]
        else:
            w_ref, g1_ref, m_ref, v_ref, g_ref, d_ref, nm_ref, nv_ref = refs
            g = g1_ref[...]
        nm = ADAM_B1 * m_ref[...] + (1.0 - ADAM_B1) * g
        nv = ADAM_B2 * v_ref[...] + (1.0 - ADAM_B2) * (g * g)
        m_hat = nm / (1.0 - ADAM_B1 ** ADAM_STEP)
        v_hat = nv / (1.0 - ADAM_B2 ** ADAM_STEP)
        g_ref[...] = g
        d_ref[...] = -ADAM_LR * (m_hat / (jnp.sqrt(v_hat) + ADAM_EPS) + ADAM_WD * w_ref[...])
        nm_ref[...] = nm
        nv_ref[...] = nv

    blk = pl.BlockSpec((tr, c), lambda i: (i, 0))
    args = [w, g1] + ([g2] if two else []) + [m, v]
    return pl.pallas_call(
        body, name=name, grid=(r // tr,), in_specs=[blk] * len(args), out_specs=[blk] * 4,
        out_shape=[jax.ShapeDtypeStruct((r, c), F32)] * 4, compiler_params=_params("parallel"),
    )(*args)


def sum4(own, land, *, name):
    r, c = own.shape
    tr = _pick(r, (256, 128, 64))

    def body(o_ref, l_ref, s_ref):
        s_ref[...] = ((o_ref[...].astype(F32) + l_ref[0].astype(F32)) + l_ref[1].astype(F32)) + l_ref[2].astype(F32)

    return pl.pallas_call(
        body, name=name, grid=(r // tr,),
        in_specs=[pl.BlockSpec((tr, c), lambda i: (i, 0)), pl.BlockSpec((3, tr, c), lambda i: (0, i, 0))],
        out_specs=pl.BlockSpec((tr, c), lambda i: (i, 0)), out_shape=jax.ShapeDtypeStruct((r, c), F32),
        compiler_params=_params("parallel"),
    )(own, land)


ANY = pl.BlockSpec(memory_space=pl.ANY)


def _place():
    x, y, c = lax.axis_index("x"), lax.axis_index("y"), lax.axis_index("c")
    chips = [(1 - x, y), (x, 1 - y), (1 - x, 1 - y)]
    return x, y, c, chips


class Ride:
    def __init__(self, arrays, out_shape, copies):
        self.arrays, self.out_shape, self.copies = list(arrays), list(out_shape), copies
        n = 3 * len(self.arrays)
        self.scratch = [pltpu.SemaphoreType.DMA((n,)), pltpu.SemaphoreType.DMA((n,)),
                        pltpu.SemaphoreType.DMA((len(self.arrays),))]

    def split(self, refs):
        n, m = len(self.arrays), len(self.out_shape)
        return refs[:n], refs[n:n + m], refs[n + m:]

    def start(self, ins, outs, sems):
        local, sends, _ = self.copies(ins, outs, *sems)
        for cp in local + sends:
            cp.start()

    def wait(self, ins, outs, sems):
        local, sends, arrivals = self.copies(ins, outs, *sems)
        for cp in arrivals:
            cp.wait_recv()
        for cp in sends:
            cp.wait_send()
        for cp in local:
            cp.wait()

    def run(self, name):
        def body(*refs):
            ins, outs, sems = self.split(refs)
            self.start(ins, outs, sems)
            self.wait(ins, outs, sems)

        return pl.pallas_call(body, name=name, in_specs=[ANY] * len(self.arrays), out_specs=[ANY] * len(self.out_shape),
                              out_shape=self.out_shape, scratch_shapes=self.scratch)(*self.arrays)


def gather_ride(shards, axes):
    def copies(ins, outs, send_sems, recv_sems, local_sems):
        x, y, c, chips = _place()

        def slot(a, q):
            size = shards[a].shape[axes[a]]
            start = pl.multiple_of(q * size, size)
            return outs[a].at[pl.ds(start, size), :] if axes[a] == 0 else outs[a].at[:, pl.ds(start, size)]

        def remote(a, k, q):
            px, py = chips[k]
            return pltpu.make_async_remote_copy(src_ref=ins[a], dst_ref=slot(a, q), send_sem=send_sems.at[3 * a + k],
                                                recv_sem=recv_sems.at[3 * a + k], device_id=(px, py, c), device_id_type=MESH)

        me = 2 * x + y
        n = len(shards)
        local = [pltpu.make_async_copy(ins[a], slot(a, me), local_sems.at[a]) for a in range(n)]
        sends = [remote(a, k, me) for a in range(n) for k in range(3)]
        arrivals = [remote(a, k, 2 * chips[k][0] + chips[k][1]) for a in range(n) for k in range(3)]
        return local, sends, arrivals

    out_shape = []
    for a, sh in enumerate(shards):
        full = list(sh.shape)
        full[axes[a]] *= N_CHIPS
        out_shape.append(jax.ShapeDtypeStruct(tuple(full), sh.dtype))
    return Ride(shards, out_shape, copies)


def scatter_ride(grads):
    def copies(ins, outs, send_sems, recv_sems, local_sems):
        x, y, c, chips = _place()
        sends = [pltpu.make_async_remote_copy(src_ref=ins[a].at[2 * px + py], dst_ref=outs[a].at[k],
                                              send_sem=send_sems.at[3 * a + k], recv_sem=recv_sems.at[3 * a + k],
                                              device_id=(px, py, c), device_id_type=MESH)
                 for a in range(len(grads)) for k, (px, py) in enumerate(chips)]
        return [], sends, sends

    return Ride(grads, [jax.ShapeDtypeStruct((3,) + g.shape[1:], g.dtype) for g in grads], copies)


def swap_with_sibling(arrs, *, name):
    n = len(arrs)

    def body(*refs):
        ins, outs = refs[:n], refs[n:2 * n]
        send_sems, recv_sems = refs[2 * n:]
        x, y, c, _ = _place()
        sends = []
        for a in range(n):
            cp = pltpu.make_async_remote_copy(src_ref=ins[a], dst_ref=outs[a], send_sem=send_sems.at[a],
                                              recv_sem=recv_sems.at[a], device_id=(x, y, 1 - c), device_id_type=MESH)
            cp.start()
            sends.append(cp)
        for cp in sends:
            cp.wait_recv()
        for cp in sends:
            cp.wait_send()

    return pl.pallas_call(
        body, name=name, in_specs=[ANY] * n, out_specs=[ANY] * n,
        out_shape=[jax.ShapeDtypeStruct(a.shape, a.dtype) for a in arrs],
        scratch_shapes=[pltpu.SemaphoreType.DMA((n,)), pltpu.SemaphoreType.DMA((n,))],
    )(*arrs)


def all_reduce_small(v, *, name):
    rows = v.shape[0]

    def body(v_ref, o_ref, gath, send_sems, recv_sems):
        x, y, c, _ = _place()
        me = 4 * x + 2 * y + c
        gath[me] = v_ref[...]
        sends = []
        for msk in range(1, N_DEV):
            peer = (x ^ (msk >> 2), y ^ ((msk >> 1) & 1), c ^ (msk & 1))
            cp = pltpu.make_async_remote_copy(src_ref=v_ref, dst_ref=gath.at[me], send_sem=send_sems.at[msk - 1],
                                              recv_sem=recv_sems.at[msk - 1], device_id=peer, device_id_type=MESH)
            cp.start()
            sends.append(cp)
        for msk in range(1, N_DEV):
            pltpu.make_async_remote_copy(src_ref=v_ref, dst_ref=gath.at[me ^ msk], send_sem=send_sems.at[msk - 1],
                                         recv_sem=recv_sems.at[msk - 1], device_id=(x, y, c), device_id_type=MESH).wait_recv()
        for cp in sends:
            cp.wait_send()
        tot = gath[0]
        for q in range(1, N_DEV):
            tot = tot + gath[q]
        o_ref[...] = tot

    vm = pl.BlockSpec(memory_space=pltpu.VMEM)
    return pl.pallas_call(
        body, name=name, in_specs=[vm], out_specs=vm, out_shape=jax.ShapeDtypeStruct(v.shape, F32),
        scratch_shapes=[pltpu.VMEM((N_DEV, rows, LANES), F32), pltpu.SemaphoreType.DMA((N_DEV - 1,)),
                        pltpu.SemaphoreType.DMA((N_DEV - 1,))],
    )(v)


def _ffn_fwd(xin, gain, w_up, w_conv, w_down, bl, tag):
    n = rmsnorm_fwd(xin, gain, name=f"{tag}_ffn_norm")
    u = matmul(n, w_up, name=f"{tag}_ffn_up")
    h = conv_silu_fwd(u, w_conv, bl, name=f"{tag}_ffn_conv")
    xout = matmul(h, w_down, out_dtype=F32, add=xin, name=f"{tag}_ffn_down")
    return xout, (n, u, h)


def _ffn_bwd(dxout, dxout_b, xin, saved, gain, w_up, w_conv, w_down, bl, tag, shard):
    n, u, h = saved
    dh = matmul(dxout_b, w_down, tb=True, name=f"{tag}_ffn_down_dx")
    g_down = matmul(h, dxout_b, ta=True, name=f"{tag}_ffn_down_dw")
    du, g_conv = conv_silu_bwd(u, w_conv, dh, bl, name=f"{tag}_ffn_conv_bwd")
    dn = matmul(du, w_up, tb=True, out_dtype=F32, name=f"{tag}_ffn_up_dx")
    g_up = matmul(n, du, ta=True, shard_cols=shard, name=f"{tag}_ffn_up_dw")
    dxin, dxin_b, g_norm = rmsnorm_bwd(xin, gain, dn, dxout, name=f"{tag}_ffn_norm_bwd")
    return dxin, dxin_b, g_norm, g_up, g_conv, g_down


def _mem_kv(mem2, gain, w_kv, tag):
    mn = rmsnorm_fwd(mem2, gain, name=f"{tag}_mem_norm")
    return mn, matmul(mn, w_kv, name=f"{tag}_mem_kv")


def _mem_kv_bwd(mem2, mn, gain, w_kv, dkvm, tag):
    g_kv = matmul(mn, dkvm, ta=True, name=f"{tag}_mem_kv_dw")
    dmn = matmul(dkvm, w_kv, tb=True, out_dtype=F32, name=f"{tag}_mem_kv_dx")
    _, _, g_norm = rmsnorm_bwd(mem2, gain, dmn, None, name=f"{tag}_mem_norm_bwd")
    return g_norm, g_kv


def _shard_major(name, grad):
    if name in COL_SHARDED:
        return grad
    return grad.reshape(N_CHIPS, grad.shape[0] // N_CHIPS, grad.shape[1])


def local_step(x, mem, target, w, shard_of, late_gather=None, early_scatter=()):
    bl, s, d = x.shape
    t = bl * s
    x0 = x.reshape(t, d)
    mem2 = mem.reshape(bl * MEM_LEN, d)
    tgt = target.reshape(t, d)
    g = {}

    n1 = rmsnorm_fwd(x0, w["a_norm_attn"], name="a_attn_norm")
    proj_a = matmul(n1, w["a_w_in"], name="a_w_in")
    mn_a, kvm_a = _mem_kv(mem2, w["a_norm_mem"], w["a_w_mem_kv"], "a")
    o_sb, o_sb_f32, delivered = sb_fwd(proj_a, bl, name="a_sb_fwd", ride=late_gather[0] if late_gather else None)
    if late_gather:
        w = {**w, **dict(zip(late_gather[1], delivered))}
    o_mem_a = mem_fwd(proj_a, kvm_a, bl, name="a_mem_fwd")
    cat_a = jnp.concatenate([o_sb, o_mem_a], axis=1)
    x1 = matmul(cat_a, w["a_w_out"], out_dtype=F32, add=x0, name="a_w_out")
    x2, ffn_a = _ffn_fwd(x1, w["a_norm_ffn"], w["a_ffn_up"], w["a_ffn_conv"], w["a_ffn_down"], bl, "a")

    nkv = rmsnorm_fwd(x2, w["kv_norm"], name="kv_norm")
    kv = matmul(nkv, w["w_kv_shared"], name="w_kv")
    n3 = rmsnorm_fwd(x2, w["b_norm_attn"], name="b_attn_norm")
    proj_b = matmul(n3, w["b_w_in"], name="b_w_in")
    mn_b, kvm_b = _mem_kv(mem2, w["b_norm_mem"], w["b_w_mem_kv"], "b")
    dil = [dil_fwd(proj_b, kv, gi, bl, name=f"b_dil_fwd{gi}") for gi in range(3)]
    os_, lses = [o for o, _ in dil], [l for _, l in dil]
    o_dil = dil_combine_fwd(os_, lses, name="b_dil_combine")
    o_mem_b = mem_fwd(proj_b, kvm_b, bl, name="b_mem_fwd")
    cat_b = jnp.concatenate([o_dil, o_mem_b], axis=1)
    x3 = matmul(cat_b, w["b_w_out"], out_dtype=F32, add=x2, name="b_w_out")
    x4, ffn_b = _ffn_fwd(x3, w["b_norm_ffn"], w["b_ffn_up"], w["b_ffn_conv"], w["b_ffn_down"], bl, "b")

    dx4, dx4b, g["final_norm"], lossvec = final_loss(x4, w["final_norm"], tgt, name="final_loss")

    dx3, dx3b, g["b_norm_ffn"], g["b_ffn_up"], g["b_ffn_conv"], g["b_ffn_down"] = _ffn_bwd(
        dx4, dx4b, x3, ffn_b, w["b_norm_ffn"], w["b_ffn_up"], w["b_ffn_conv"], w["b_ffn_down"], bl, "b",
        shard_of["b_ffn_up"])
    dcat_b = matmul(dx3b, w["b_w_out"], tb=True, name="b_w_out_dx")
    g["b_w_out"] = matmul(cat_b, dx3b, ta=True, name="b_w_out_dw")
    dog, dshift = dil_combine_bwd(os_, lses, dcat_b, name="b_dil_combine_bwd")
    dqs, dks, dvs = [], [], []
    for gi in range(3):
        dq_g, dk_g, dv_g = dil_bwd(proj_b, kv, lses[gi], dog, dshift, gi, bl, name=f"b_dil_bwd{gi}")
        dqs.append(dq_g), dks.append(dk_g), dvs.append(dv_g)
    dq_mem_b, dkvm_b = mem_bwd(proj_b, kvm_b, dcat_b, bl, name="b_mem_bwd")
    g["b_norm_mem"], g["b_w_mem_kv"] = _mem_kv_bwd(mem2, mn_b, w["b_norm_mem"], w["b_w_mem_kv"], dkvm_b, "b")
    dproj_b = jnp.concatenate(dqs + [dq_mem_b], axis=1)
    dn3 = matmul(dproj_b, w["b_w_in"], tb=True, out_dtype=F32, name="b_w_in_dx")
    g["b_w_in"] = matmul(n3, dproj_b, ta=True, name="b_w_in_dw")
    dx2, _, g["b_norm_attn"] = rmsnorm_bwd(x2, w["b_norm_attn"], dn3, dx3, name="b_attn_norm_bwd")
    dkv = jnp.concatenate(dks + dvs, axis=1).astype(BF)
    dnkv = matmul(dkv, w["w_kv_shared"], tb=True, out_dtype=F32, name="w_kv_dx")
    g["w_kv_shared"] = matmul(nkv, dkv, ta=True, shard_cols=shard_of["w_kv_shared"], name="w_kv_dw")
    dx2, dx2b, g["kv_norm"] = rmsnorm_bwd(x2, w["kv_norm"], dnkv, dx2, name="kv_norm_bwd")

    dx1, dx1b, g["a_norm_ffn"], g["a_ffn_up"], g["a_ffn_conv"], g["a_ffn_down"] = _ffn_bwd(
        dx2, dx2b, x1, ffn_a, w["a_norm_ffn"], w["a_ffn_up"], w["a_ffn_conv"], w["a_ffn_down"], bl, "a",
        shard_of["a_ffn_up"])
    dcat_a = matmul(dx1b, w["a_w_out"], tb=True, name="a_w_out_dx")
    g["a_w_out"] = matmul(cat_a, dx1b, ta=True, name="a_w_out_dw")
    leaving = scatter_ride([_shard_major(n, g[n]) for n in early_scatter]) if early_scatter else None
    dq_sb, dk_sb, dv_sb, landed = sb_bwd(proj_a, o_sb_f32, dcat_a, bl, name="a_sb_bwd", ride=leaving)
    landed = dict(zip(early_scatter, landed))
    dq_mem_a, dkvm_a = mem_bwd(proj_a, kvm_a, dcat_a, bl, name="a_mem_bwd")
    g["a_norm_mem"], g["a_w_mem_kv"] = _mem_kv_bwd(mem2, mn_a, w["a_norm_mem"], w["a_w_mem_kv"], dkvm_a, "a")
    dproj_a = jnp.concatenate([dq_sb, dk_sb.astype(BF), dv_sb.astype(BF), dq_mem_a], axis=1)
    dn1 = matmul(dproj_a, w["a_w_in"], tb=True, out_dtype=F32, name="a_w_in_dx")
    g["a_w_in"] = matmul(n1, dproj_a, ta=True, shard_cols=shard_of["a_w_in"], name="a_w_in_dw")
    dx0, _, g["a_norm_attn"] = rmsnorm_bwd(x0, w["a_norm_attn"], dn1, dx1, name="a_attn_norm_bwd")
    return lossvec, dx0, g, landed


MATRICES = ("a_w_in", "a_w_out", "a_w_mem_kv", "a_ffn_up", "a_ffn_down", "w_kv_shared", "b_w_in", "b_w_out",
            "b_w_mem_kv", "b_ffn_up", "b_ffn_down")
COL_SHARDED = ("a_w_in", "a_ffn_up", "w_kv_shared", "b_ffn_up")
FIRST_NEEDED = ("a_w_in", "a_w_mem_kv")
SMALL_SHARDED = ("a_norm_attn", "a_norm_mem", "a_norm_ffn", "a_ffn_conv", "b_ffn_conv")
SMALL_REPLICATED = ("kv_norm", "b_norm_attn", "b_norm_mem", "b_norm_ffn", "final_norm")
WEIGHTS = ("a_norm_attn", "a_w_in", "a_w_out", "a_norm_mem", "a_w_mem_kv", "a_norm_ffn", "a_ffn_up", "a_ffn_conv",
           "a_ffn_down", "kv_norm", "w_kv_shared", "b_norm_attn", "b_w_in", "b_w_out", "b_norm_mem", "b_w_mem_kv",
           "b_norm_ffn", "b_ffn_up", "b_ffn_conv", "b_ffn_down", "final_norm")


def _two_d(a):
    if a.ndim == 1:
        return a.reshape(1, -1)
    return a.reshape(a.shape[-2], a.shape[-1])


def kernel(x, mem, a_norm_attn, a_w_in, a_w_out, a_norm_mem, a_w_mem_kv, a_norm_ffn, a_ffn_up, a_ffn_conv, a_ffn_down, kv_norm, w_kv_shared, b_norm_attn, b_w_in, b_w_out, b_norm_mem, b_w_mem_kv, b_norm_ffn, b_ffn_up, b_ffn_conv, b_ffn_down, final_norm, loss_target, m_a_norm_attn, m_a_w_in, m_a_w_out, m_a_norm_mem, m_a_w_mem_kv, m_a_norm_ffn, m_a_ffn_up, m_a_ffn_conv, m_a_ffn_down, m_kv_norm, m_w_kv_shared, m_b_norm_attn, m_b_w_in, m_b_w_out, m_b_norm_mem, m_b_w_mem_kv, m_b_norm_ffn, m_b_ffn_up, m_b_ffn_conv, m_b_ffn_down, m_final_norm, v_a_norm_attn, v_a_w_in, v_a_w_out, v_a_norm_mem, v_a_w_mem_kv, v_a_norm_ffn, v_a_ffn_up, v_a_ffn_conv, v_a_ffn_down, v_kv_norm, v_w_kv_shared, v_b_norm_attn, v_b_w_in, v_b_w_out, v_b_norm_mem, v_b_w_mem_kv, v_b_norm_ffn, v_b_ffn_up, v_b_ffn_conv, v_b_ffn_down, v_final_norm):
    given = dict(locals())
    wl = {n: _two_d(given[n]) for n in WEIGHTS}
    ml = {n: _two_d(given["m_" + n]) for n in WEIGHTS}
    vl = {n: _two_d(given["v_" + n]) for n in WEIGHTS}
    chip = 2 * lax.axis_index("x") + lax.axis_index("y")

    packed = jnp.concatenate([wl[n].reshape(-1, LANES) for n in SMALL_SHARDED], axis=0)
    axis_of = lambda n: 1 if n in COL_SHARDED else 0
    late = tuple(n for n in MATRICES if n not in FIRST_NEEDED)
    full = gather_ride([wl[n].astype(BF) for n in FIRST_NEEDED] + [packed],
                       [axis_of(n) for n in FIRST_NEEDED] + [0]).run("gather_first")
    w = dict(zip(FIRST_NEEDED, full[:-1]))
    late_gather = (gather_ride([wl[n].astype(BF) for n in late], [axis_of(n) for n in late]), late)
    rows = packed.shape[0]
    per_chip = full[-1].reshape(N_CHIPS, rows, LANES)
    r0 = 0
    for n in SMALL_SHARDED:
        nr = wl[n].size // LANES
        piece = per_chip[:, r0:r0 + nr].reshape(N_CHIPS, wl[n].shape[0], wl[n].shape[1])
        w[n] = jnp.concatenate([piece[q] for q in range(N_CHIPS)], axis=1)
        r0 += nr
    for n in SMALL_REPLICATED:
        w[n] = wl[n]

    shard_of = {n: wl[n].shape[1] for n in COL_SHARDED}
    lossvec, dx0, g, landed = local_step(x, mem, loss_target, w, shard_of, late_gather=late_gather, early_scatter=late)
    loss = lax.psum(0.5 * jnp.sum(lossvec) / x.shape[-1], ("x", "y", "c"))

    g4 = {n: _shard_major(n, g[n]) for n in MATRICES}
    landed.update(zip(FIRST_NEEDED, scatter_ride([g4[n] for n in FIRST_NEEDED]).run("scatter_last")))
    sums = [sum4(lax.dynamic_index_in_dim(g4[n], chip, 0, keepdims=False), landed[n], name=f"sum4_{n}")
            for n in MATRICES]
    theirs = swap_with_sibling(sums, name="swap_sums")
    out = {}
    for k, n in enumerate(MATRICES):
        out[n] = adamw(wl[n], sums[k], theirs[k], ml[n], vl[n], name=f"adamw_{n}")

    small = SMALL_SHARDED + SMALL_REPLICATED
    flat = jnp.concatenate([g[n].reshape(-1, LANES) for n in small], axis=0)
    tot = all_reduce_small(flat, name="all_reduce_small")
    r0 = 0
    for n in small:
        nr = g[n].size // LANES
        gn = tot[r0:r0 + nr].reshape(g[n].shape)
        r0 += nr
        if n in SMALL_SHARDED:
            gn = lax.dynamic_slice_in_dim(gn, chip * wl[n].shape[1], wl[n].shape[1], axis=1)
        out[n] = adamw(wl[n], gn, None, ml[n], vl[n], name=f"adamw_{n}")

    res = [loss, dx0.reshape(x.shape)]
    for slot in range(4):
        res += [out[n][slot].reshape(given[n].shape) for n in WEIGHTS]
    return tuple(res)
```

```python
import functools
import math

import jax
import jax.numpy as jnp
from jax import lax
from jax.experimental import pallas as pl
from jax.experimental.pallas import tpu as pltpu

F32 = jnp.float32
BF = jnp.bfloat16
MESH = pl.DeviceIdType.MESH

HEAD_DIM = 64
LANES = 128
SB_WIDTH = 12 * HEAD_DIM
MEM_WIDTH = 4 * HEAD_DIM
DIL_WIDTH = 12 * HEAD_DIM
MEM_LEN = 256
DIL_GROUPS = ((128, 1), (512, 4), (2048, 16))
QB = 128
EPS = 1e-6
SCALE = HEAD_DIM ** -0.5
NEG = -1e30
ALIBI = tuple(2.0 ** (-8.0 * i / 12) for i in range(1, 13))
N_CHIPS = 4
N_DEV = 8

ADAM_LR, ADAM_B1, ADAM_B2, ADAM_EPS, ADAM_WD, ADAM_STEP = 0.001, 0.9, 0.999, 1e-08, 0.01, 10

VMEM_LIMIT = 48 * 1024 * 1024


def _mo(v, m):
    return v if isinstance(v, int) else pl.multiple_of(v, m)


def _pick(n, prefs):
    for t in prefs:
        if n % t == 0:
            return t
    return n


def _params(*sem):
    return pltpu.CompilerParams(dimension_semantics=sem, vmem_limit_bytes=VMEM_LIMIT)


def matmul(a, b, *, ta=False, tb=False, out_dtype=BF, add=None, shard_cols=0, name):
    m, k = (a.shape[1], a.shape[0]) if ta else a.shape
    n = b.shape[0] if tb else b.shape[1]
    tm = _pick(m, (512, 1408, 256, 128))
    tk = _pick(k, (2048, 2816, 2560, 1536, 1024, 512, 256, 128))
    tn = shard_cols if shard_cols else _pick(n, (1024, 1408, 512, 256, 128))
    nk = k // tk
    dims = (((0,) if ta else (1,), (1,) if tb else (0,)), ((), ()))

    def body(*refs):
        a_ref, b_ref = refs[:2]
        add_ref = refs[2] if add is not None else None
        o_ref = refs[3 if add is not None else 2]
        part = lax.dot_general(a_ref[...].astype(BF), b_ref[...].astype(BF), dims, preferred_element_type=F32)

        def finish(r):
            if add is not None:
                r = r + add_ref[...]
            o_ref[...] = r.astype(o_ref.dtype)

        if nk == 1:
            finish(part)
            return
        acc_ref = refs[-1]
        kk = pl.program_id(2)

        @pl.when(kk == 0)
        def _():
            acc_ref[...] = part

        @pl.when(kk > 0)
        def _():
            acc_ref[...] += part

        @pl.when(kk == nk - 1)
        def _():
            finish(acc_ref[...])

    a_spec = pl.BlockSpec((tk, tm), lambda i, j, q: (q, i)) if ta else pl.BlockSpec((tm, tk), lambda i, j, q: (i, q))
    b_spec = pl.BlockSpec((tn, tk), lambda i, j, q: (j, q)) if tb else pl.BlockSpec((tk, tn), lambda i, j, q: (q, j))
    in_specs = [a_spec, b_spec]
    args = [a, b]
    if add is not None:
        in_specs.append(pl.BlockSpec((tm, tn), lambda i, j, q: (i, j)))
        args.append(add)
    if shard_cols:
        out_shape = jax.ShapeDtypeStruct((N_CHIPS, m, shard_cols), out_dtype)
        out_spec = pl.BlockSpec((None, tm, tn), lambda i, j, q: (j, i, 0))
    else:
        out_shape = jax.ShapeDtypeStruct((m, n), out_dtype)
        out_spec = pl.BlockSpec((tm, tn), lambda i, j, q: (i, j))
    return pl.pallas_call(
        body, name=name, grid=(m // tm, n // tn, nk), in_specs=in_specs, out_specs=out_spec, out_shape=out_shape,
        scratch_shapes=[pltpu.VMEM((tm, tn), F32)] if nk > 1 else [],
        compiler_params=_params("parallel", "parallel", "arbitrary"),
    )(*args)


def rmsnorm_fwd(x, g, *, name):
    t, d = x.shape
    tr = _pick(t, (512, 256))

    def body(x_ref, g_ref, o_ref):
        xv = x_ref[...]
        r = lax.rsqrt(jnp.mean(xv * xv, axis=-1, keepdims=True) + EPS)
        o_ref[...] = ((xv * r) * g_ref[...]).astype(o_ref.dtype)

    return pl.pallas_call(
        body, name=name, grid=(t // tr,),
        in_specs=[pl.BlockSpec((tr, d), lambda i: (i, 0)), pl.BlockSpec((1, d), lambda i: (0, 0))],
        out_specs=pl.BlockSpec((tr, d), lambda i: (i, 0)), out_shape=jax.ShapeDtypeStruct((t, d), BF),
        compiler_params=_params("parallel"),
    )(x, g)


def rmsnorm_bwd(x, g, dn, dres, *, name):
    t, d = x.shape
    tr = _pick(t, (512, 256))
    want_dx = dres is not None

    def body(*refs):
        if want_dx:
            x_ref, g_ref, dn_ref, dres_ref, dx_ref, dxb_ref, dg_ref = refs
        else:
            x_ref, g_ref, dn_ref, dg_ref = refs

        @pl.when(pl.program_id(0) == 0)
        def _():
            dg_ref[...] = jnp.zeros_like(dg_ref)

        xv = x_ref[...]
        r = lax.rsqrt(jnp.mean(xv * xv, axis=-1, keepdims=True) + EPS)
        xn = xv * r
        dnv = dn_ref[...].astype(F32)
        dg_ref[...] += jnp.sum(dnv * xn, axis=0, keepdims=True)
        if want_dx:
            dyg = dnv * g_ref[...]
            cm = jnp.mean(dyg * xn, axis=-1, keepdims=True)
            dx = dres_ref[...] + r * (dyg - xn * cm)
            dx_ref[...] = dx
            dxb_ref[...] = dx.astype(BF)

    row = pl.BlockSpec((tr, d), lambda i: (i, 0))
    vec = pl.BlockSpec((1, d), lambda i: (0, 0))
    if want_dx:
        return pl.pallas_call(
            body, name=name, grid=(t // tr,), in_specs=[row, vec, row, row], out_specs=[row, row, vec],
            out_shape=[jax.ShapeDtypeStruct((t, d), F32), jax.ShapeDtypeStruct((t, d), BF), jax.ShapeDtypeStruct((1, d), F32)],
            compiler_params=_params("arbitrary"),
        )(x, g, dn, dres)
    return None, None, pl.pallas_call(
        body, name=name, grid=(t // tr,), in_specs=[row, vec, row], out_specs=vec,
        out_shape=jax.ShapeDtypeStruct((1, d), F32), compiler_params=_params("arbitrary"),
    )(x, g, dn)


def final_loss(x, g, target, *, name):
    t, d = x.shape
    tr = _pick(t, (512, 256))

    def body(x_ref, g_ref, t_ref, dx_ref, dxb_ref, dg_ref, lv_ref):
        @pl.when(pl.program_id(0) == 0)
        def _():
            dg_ref[...] = jnp.zeros_like(dg_ref)
            lv_ref[...] = jnp.zeros_like(lv_ref)

        xv = x_ref[...]
        r = lax.rsqrt(jnp.mean(xv * xv, axis=-1, keepdims=True) + EPS)
        xn = xv * r
        err = xn * g_ref[...] - t_ref[...]
        lv_ref[...] += jnp.sum(err * err, axis=0, keepdims=True)
        dy = err * (1.0 / d)
        dg_ref[...] += jnp.sum(dy * xn, axis=0, keepdims=True)
        dyg = dy * g_ref[...]
        cm = jnp.mean(dyg * xn, axis=-1, keepdims=True)
        dx = r * (dyg - xn * cm)
        dx_ref[...] = dx
        dxb_ref[...] = dx.astype(BF)

    row = pl.BlockSpec((tr, d), lambda i: (i, 0))
    vec = pl.BlockSpec((1, d), lambda i: (0, 0))
    return pl.pallas_call(
        body, name=name, grid=(t // tr,), in_specs=[row, vec, row], out_specs=[row, row, vec, vec],
        out_shape=[jax.ShapeDtypeStruct((t, d), F32), jax.ShapeDtypeStruct((t, d), BF), jax.ShapeDtypeStruct((1, d), F32),
                   jax.ShapeDtypeStruct((1, d), F32)],
        compiler_params=_params("arbitrary"),
    )(x, g, target)


CONV_ROWS = 256
HALO = 16


def _conv_taps(ext, w, rows):
    s0 = ext[HALO:HALO + rows]
    s1 = pltpu.roll(ext, 1, 0)[HALO:HALO + rows]
    s2 = pltpu.roll(ext, 2, 0)[HALO:HALO + rows]
    return (w[0:1] * s2 + w[1:2] * s1) + w[2:3] * s0, s0, s1, s2


def conv_silu_fwd(u, w, bl, *, name):
    t, f2 = u.shape
    f = f2 // 2
    s = t // bl
    tc = _pick(f, (256, 128))
    nf = f // tc
    nch = s // CONV_ROWS

    def body(ua_ref, ug_ref, wa_ref, wg_ref, h_ref):
        wa, wg = wa_ref[...], wg_ref[...]

        def chunk(ci, carry):
            r0 = pl.multiple_of(ci * CONV_ROWS, CONV_ROWS)
            ps = pl.multiple_of(jnp.maximum(r0 - HALO, 0), HALO)
            keep = (ci > 0).astype(F32)

            def conv(ref, wv):
                ext = jnp.concatenate([ref[0, pl.ds(ps, HALO), :].astype(F32) * keep,
                                       ref[0, pl.ds(r0, CONV_ROWS), :].astype(F32)], axis=0)
                return _conv_taps(ext, wv, CONV_ROWS)[0]

            ca, cg = conv(ua_ref, wa), conv(ug_ref, wg)
            sg = pl.reciprocal(1.0 + jnp.exp(-cg), approx=True)
            h_ref[0, pl.ds(r0, CONV_ROWS), :] = ((cg * sg) * ca).astype(h_ref.dtype)
            return carry

        lax.fori_loop(0, nch, chunk, 0)

    u3 = u.reshape(bl, s, f2)
    blk = lambda off: pl.BlockSpec((1, s, tc), lambda j, b: (b, 0, j + off))
    wblk = lambda off: pl.BlockSpec((3, tc), lambda j, b: (0, j + off))
    h = pl.pallas_call(
        body, name=name, grid=(nf, bl), in_specs=[blk(0), blk(nf), wblk(0), wblk(nf)],
        out_specs=pl.BlockSpec((1, s, tc), lambda j, b: (b, 0, j)), out_shape=jax.ShapeDtypeStruct((bl, s, f), BF),
        compiler_params=_params("parallel", "parallel"),
    )(u3, u3, w, w)
    return h.reshape(t, f)


def conv_silu_bwd(u, w, dh, bl, *, name):
    t, f2 = u.shape
    f = f2 // 2
    s = t // bl
    tc = _pick(f, (256, 128))
    nf = f // tc
    nch = s // CONV_ROWS
    ext_rows = CONV_ROWS + HALO

    def body(ua_ref, ug_ref, wa_ref, wg_ref, dh_ref, dua_ref, dug_ref, dwa_ref, dwg_ref):
        wa, wg = wa_ref[...], wg_ref[...]

        @pl.when(pl.program_id(1) == 0)
        def _():
            dwa_ref[...] = jnp.zeros_like(dwa_ref)
            dwg_ref[...] = jnp.zeros_like(dwg_ref)

        def chunk(ci, carry):
            r0 = pl.multiple_of(ci * CONV_ROWS, CONV_ROWS)
            ps = pl.multiple_of(jnp.maximum(r0 - HALO, 0), HALO)
            ns = pl.multiple_of(jnp.minimum(r0 + CONV_ROWS, s - HALO), HALO)
            keep_p = (ci > 0).astype(F32)
            keep_n = (ci < nch - 1).astype(F32)

            def ext_of(ref):
                return jnp.concatenate([ref[0, pl.ds(ps, HALO), :].astype(F32) * keep_p,
                                        ref[0, pl.ds(r0, CONV_ROWS), :].astype(F32),
                                        ref[0, pl.ds(ns, HALO), :].astype(F32) * keep_n], axis=0)

            ca, a0, a1, a2 = _conv_taps(ext_of(ua_ref), wa, ext_rows)
            cg, g0, g1, g2 = _conv_taps(ext_of(ug_ref), wg, ext_rows)
            dhe = jnp.concatenate([dh_ref[0, pl.ds(r0, CONV_ROWS), :].astype(F32),
                                   dh_ref[0, pl.ds(ns, HALO), :].astype(F32) * keep_n], axis=0)
            sg = pl.reciprocal(1.0 + jnp.exp(-cg), approx=True)
            dca = dhe * (cg * sg)
            dcg = dhe * ca * (sg * (1.0 + cg * (1.0 - sg)))

            def back(dc, wv):
                n1 = pltpu.roll(dc, ext_rows - 1, 0)
                n2 = pltpu.roll(dc, ext_rows - 2, 0)
                return ((wv[2:3] * dc + wv[1:2] * n1) + wv[0:1] * n2)[:CONV_ROWS]

            dua_ref[0, pl.ds(r0, CONV_ROWS), :] = back(dca, wa).astype(dua_ref.dtype)
            dug_ref[0, pl.ds(r0, CONV_ROWS), :] = back(dcg, wg).astype(dug_ref.dtype)

            def wsum(dc, taps):
                d0 = dc[:CONV_ROWS]
                return [jnp.sum(d0 * tp[:CONV_ROWS], axis=0, keepdims=True) for tp in taps]

            sa = wsum(dca, (a2, a1, a0))
            sgs = wsum(dcg, (g2, g1, g0))
            for j in range(3):
                dwa_ref[j:j + 1, :] += sa[j]
                dwg_ref[j:j + 1, :] += sgs[j]
            return carry

        lax.fori_loop(0, nch, chunk, 0)

    u3 = u.reshape(bl, s, f2)
    dh3 = dh.reshape(bl, s, f)
    blk = lambda off: pl.BlockSpec((1, s, tc), lambda j, b: (b, 0, j + off))
    wblk = lambda off: pl.BlockSpec((3, tc), lambda j, b: (0, j + off))
    dua, dug, dwa, dwg = pl.pallas_call(
        body, name=name, grid=(nf, bl), in_specs=[blk(0), blk(nf), wblk(0), wblk(nf), blk(0)],
        out_specs=[blk(0), blk(0), wblk(0), wblk(0)],
        out_shape=[jax.ShapeDtypeStruct((bl, s, f), BF), jax.ShapeDtypeStruct((bl, s, f), BF),
                   jax.ShapeDtypeStruct((3, f), F32), jax.ShapeDtypeStruct((3, f), F32)],
        compiler_params=_params("parallel", "arbitrary"),
    )(u3, u3, w, w, dh3)
    du = jnp.concatenate([dua.reshape(t, f), dug.reshape(t, f)], axis=1)
    return du, jnp.concatenate([dwa, dwg], axis=1)


def _lane_masks(rows):
    lane = lax.broadcasted_iota(jnp.int32, (rows, LANES), 1)
    return lane < HEAD_DIM, lane >= HEAD_DIM


def _nt(a, b):
    return lax.dot_general(a, b, (((1,), (1,)), ((), ())), preferred_element_type=F32)


def _tn(a, b):
    return lax.dot_general(a, b, (((0,), (0,)), ((), ())), preferred_element_type=F32)


def _nn(a, b):
    return jnp.dot(a, b, preferred_element_type=F32)


def _suffix_ones():
    j = lax.broadcasted_iota(jnp.int32, (2 * QB, QB), 0) % QB
    s = lax.broadcasted_iota(jnp.int32, (2 * QB, QB), 1)
    return (j >= s).astype(BF)


def _softplus(z):
    return jnp.maximum(z, 0.0) + jnp.log(1.0 + jnp.exp(-jnp.abs(z)))


SB_BLOCK = 256
SB_DEAD = -100.0


SB_STRIP = 32
SB_SUB = SB_BLOCK // QB


def _sb_scratch(backward):
    blk = (2, SB_BLOCK, SB_BLOCK)
    per_head = pltpu.VMEM((2, SB_BLOCK, LANES), F32)
    scr = [pltpu.VMEM((2 * QB, QB), BF),
           pltpu.VMEM(blk, F32),
           pltpu.VMEM((2, SB_SUB) + blk[1:], BF),
           pltpu.VMEM(blk, F32),
           pltpu.VMEM(blk, BF),
           per_head]
    if not backward:
        return scr + [per_head]
    return scr + [pltpu.VMEM(blk, F32),
                  pltpu.VMEM(blk, F32),
                  pltpu.VMEM(blk, F32),
                  pltpu.VMEM(blk, BF),
                  per_head,
                  per_head,
                  per_head]


def _sb_strips(fn):
    def step(r, c):
        fn(pl.ds(pl.multiple_of(r * SB_STRIP, SB_STRIP), SB_STRIP), r)
        return c
    lax.fori_loop(0, SB_BLOCK // SB_STRIP, step, 0, unroll=True)


def _sb_causal(thr, r):
    rel = (lax.broadcasted_iota(jnp.int32, (SB_STRIP, SB_BLOCK), 1)
           - lax.broadcasted_iota(jnp.int32, (SB_STRIP, SB_BLOCK), 0))
    return rel < thr + r * SB_STRIP


def _sb_split_store(sp_scr, rows, v):
    hi = v.astype(BF)
    lo = (v - hi.astype(F32)).astype(BF)
    for u in range(SB_SUB):
        sp_scr[u, rows, 0:QB] = hi[:, u * QB:(u + 1) * QB]
        sp_scr[u, rows, QB:2 * QB] = lo[:, u * QB:(u + 1) * QB]


def _sb_suffix_sums(uu_scr, sp_scr, out_scr):
    for u in range(SB_SUB):
        out_scr[:, u * QB:(u + 1) * QB] = _nn(sp_scr[u], uu_scr[...])


def _sb_fold(sums, off):
    offs = [None] * SB_SUB
    for u in reversed(range(SB_SUB)):
        offs[u] = off
        off = off + jnp.broadcast_to(sums[:, u * QB:u * QB + 1], (SB_STRIP, LANES))
    return offs, off


def _sb_first(thr, z_scr, sp_scr, ls_scr=None):
    def strip(rows, r):
        ls = jnp.where(_sb_causal(thr, r), -_softplus(z_scr[rows, :]), 0.0)
        if ls_scr is not None:
            ls_scr[rows, :] = ls
        _sb_split_store(sp_scr, rows, ls)

    _sb_strips(strip)


def _sb_second(thr, z_scr, cin_scr, w_scr, carry_scr):
    def strip(rows, r):
        z, cin = z_scr[rows, :], cin_scr[rows, :]
        offs, carry_scr[rows, :] = _sb_fold(cin, carry_scr[rows, :])
        e = jnp.concatenate([z[:, u * QB:(u + 1) * QB] + cin[:, u * QB:(u + 1) * QB] + offs[u] for u in range(SB_SUB)],
                            axis=1)
        w_scr[rows, :] = jnp.where(_sb_causal(thr, r), jnp.exp(e), 0.0).astype(BF)

    _sb_strips(strip)


def _sb_third(w_scr, dw_scr, da_scr, sp_scr):
    def strip(rows, r):
        da = w_scr[rows, :].astype(F32) * dw_scr[rows, :]
        da_scr[rows, :] = da
        _sb_split_store(sp_scr, rows, da)

    _sb_strips(strip)


def _sb_fourth(thr, z_scr, ls_scr, da_scr, sin_scr, dz_scr, carry_s_scr, dsum_scr):
    def strip(rows, r):
        da, sin = da_scr[rows, :], sin_scr[rows, :]
        offs, carry_s_scr[rows, :] = _sb_fold(sin, carry_s_scr[rows, :])
        dsum = dsum_scr[rows, :]
        pre = jnp.concatenate([dsum - (sin[:, u * QB:(u + 1) * QB] - da[:, u * QB:(u + 1) * QB] + offs[u])
                               for u in range(SB_SUB)], axis=1)
        sig = jnp.exp(z_scr[rows, :] + ls_scr[rows, :])
        dz_scr[rows, :] = jnp.where(_sb_causal(thr, r), da - sig * pre, 0.0).astype(BF)

    _sb_strips(strip)


def _sb_alive(carry_scr):
    return (jnp.max(carry_scr[...]) > SB_DEAD).astype(jnp.int32)


def _ride_hooks(ride, refs, n_in, n_out, n_scratch, grid):
    if ride is None:
        return refs[:n_in], refs[n_in:n_in + n_out], refs[n_in + n_out:], lambda: None, lambda: None
    ni, no = len(ride.arrays), len(ride.out_shape)
    own_in, rin = refs[:n_in], refs[n_in:n_in + ni]
    own_out, rout = refs[n_in + ni:n_in + ni + n_out], refs[n_in + ni + n_out:n_in + ni + n_out + no]
    rest = refs[n_in + ni + n_out + no:]
    own_scr, sems = rest[:n_scratch], rest[n_scratch:]
    ids = [pl.program_id(a) for a in range(len(grid))]

    def start():
        first = functools.reduce(lambda u, v: u & v, [i == 0 for i in ids])
        pl.when(first)(lambda: ride.start(rin, rout, sems))

    def wait():
        last = functools.reduce(lambda u, v: u & v, [i == n - 1 for i, n in zip(ids, grid)])
        pl.when(last)(lambda: ride.wait(rin, rout, sems))

    return own_in, own_out, own_scr, start, wait


def sb_fwd(proj, bl, *, name, ride=None):
    t, width = proj.shape
    s = t // bl
    npair = SB_WIDTH // LANES
    nq = s // SB_BLOCK
    grid = (bl, npair, nq)

    own_scratch = _sb_scratch(backward=False)

    def body(*refs):
        (q_ref, k_ref, v_ref), (o_ref, of_ref), scr, ride_start, ride_wait = _ride_hooks(
            ride, refs, 3, 2, len(own_scratch), grid)
        uu_scr, z_scr, sp_scr, cin_scr, w_scr, carry_scr, acc_scr = scr
        ride_start()
        i = pl.program_id(2)
        uu_scr[...] = _suffix_ones()
        carry_scr[...] = jnp.zeros_like(carry_scr)
        acc_scr[...] = jnp.zeros_like(acc_scr)
        qs = (q_ref[0].astype(F32) * SCALE).astype(BF)
        masks = _lane_masks(SB_BLOCK)

        z, sp, cin, w, car, acc = ([r.at[h] for h in range(2)] for r in (z_scr, sp_scr, cin_scr, w_scr, carry_scr, acc_scr))

        def live(c):
            start = pl.multiple_of(c * SB_BLOCK, SB_BLOCK)
            thr = (i - c) * SB_BLOCK
            kb, vb = k_ref[0, pl.ds(start, SB_BLOCK), :], v_ref[0, pl.ds(start, SB_BLOCK), :]
            for h in range(2):
                z[h][...] = _nt(jnp.where(masks[h], qs, jnp.zeros_like(qs)), kb)
            _sb_first(thr, z[0], sp[0])
            _sb_suffix_sums(uu_scr, sp[0], cin[0])
            _sb_first(thr, z[1], sp[1])
            _sb_suffix_sums(uu_scr, sp[1], cin[1])
            _sb_second(thr, z[0], cin[0], w[0], car[0])
            acc[0][...] += _nn(w[0][...], vb)
            _sb_second(thr, z[1], cin[1], w[1], car[1])
            acc[1][...] += _nn(w[1][...], vb)
            return _sb_alive(carry_scr)

        def step(n, alive):
            return lax.cond(alive > 0, lambda: live(i - n), lambda: alive)

        lax.fori_loop(0, i + 1, step, jnp.int32(1))
        both = jnp.where(masks[0], acc_scr[0], acc_scr[1])
        o_ref[0] = both.astype(o_ref.dtype)
        of_ref[0] = both
        ride_wait()

    p3 = proj.reshape(bl, s, width)
    qblk = pl.BlockSpec((1, SB_BLOCK, LANES), lambda b, p, i: (b, i, p))
    extra = ride.arrays if ride else []
    o, of, *delivered = pl.pallas_call(
        body, name=name, grid=grid,
        in_specs=[qblk, pl.BlockSpec((1, s, LANES), lambda b, p, i: (b, 0, npair + p)),
                  pl.BlockSpec((1, s, LANES), lambda b, p, i: (b, 0, 2 * npair + p))] + [ANY] * len(extra),
        out_specs=[qblk, qblk] + [ANY] * (len(ride.out_shape) if ride else 0),
        out_shape=[jax.ShapeDtypeStruct((bl, s, SB_WIDTH), BF), jax.ShapeDtypeStruct((bl, s, SB_WIDTH), F32)]
        + (ride.out_shape if ride else []),
        scratch_shapes=own_scratch + (ride.scratch if ride else []),
        compiler_params=_params("arbitrary", "arbitrary", "arbitrary"),
    )(p3, p3, p3, *extra)
    return o.reshape(t, SB_WIDTH), of.reshape(t, SB_WIDTH), delivered


def sb_bwd(proj, o, dcat, bl, *, name, ride=None):
    t, width = proj.shape
    s = t // bl
    npair = SB_WIDTH // LANES
    nq = s // SB_BLOCK
    grid = (bl, npair, nq)
    own_scratch = _sb_scratch(backward=True)

    def body(*refs):
        (q_ref, k_ref, v_ref, o_ref, do_ref), (dq_ref, dk_ref, dv_ref), scr, ride_start, ride_wait = _ride_hooks(
            ride, refs, 5, 3, len(own_scratch), grid)
        (uu_scr, z_scr, sp_scr, cin_scr, w_scr, carry_scr,
         ls_scr, dw_scr, da_scr, dz_scr, carry_s_scr, dsum_scr, dq_scr) = scr
        ride_start()
        i = pl.program_id(2)

        @pl.when(i == 0)
        def _():
            dk_ref[...] = jnp.zeros_like(dk_ref)
            dv_ref[...] = jnp.zeros_like(dv_ref)

        uu_scr[...] = _suffix_ones()
        for ref in (carry_scr, carry_s_scr, dq_scr):
            ref[...] = jnp.zeros_like(ref)
        masks = _lane_masks(SB_BLOCK)
        qs = (q_ref[0].astype(F32) * SCALE).astype(BF)
        do = do_ref[0]
        prod = do.astype(F32) * o_ref[0]
        for h in range(2):
            dsum_scr[h] = jnp.broadcast_to(jnp.sum(jnp.where(masks[h], prod, 0.0), axis=1, keepdims=True),
                                           (SB_BLOCK, LANES))

        z, sp, cin, w, car, ls, dw, da, dz, cars, dsum, dq = (
            [r.at[h] for h in range(2)] for r in (z_scr, sp_scr, cin_scr, w_scr, carry_scr, ls_scr, dw_scr, da_scr,
                                                  dz_scr, carry_s_scr, dsum_scr, dq_scr))

        def live(c):
            start = pl.multiple_of(c * SB_BLOCK, SB_BLOCK)
            thr = (i - c) * SB_BLOCK
            keys = pl.ds(start, SB_BLOCK)
            kb, vb = k_ref[0, keys, :], v_ref[0, keys, :]
            qh = [jnp.where(masks[h], qs, jnp.zeros_like(qs)) for h in range(2)]
            doh = [jnp.where(masks[h], do, jnp.zeros_like(do)) for h in range(2)]
            for h in range(2):
                z[h][...] = _nt(qh[h], kb)
                dw[h][...] = _nt(doh[h], vb)

            def weights(h):
                _sb_second(thr, z[h], cin[h], w[h], car[h])
                _sb_third(w[h], dw[h], da[h], sp[h])
                _sb_suffix_sums(uu_scr, sp[h], cin[h])

            def grads(h):
                _sb_fourth(thr, z[h], ls[h], da[h], cin[h], dz[h], cars[h], dsum[h])
                dk_ref[0, keys, :] += _tn(dz[h][...], qh[h])
                dv_ref[0, keys, :] += _tn(w[h][...], doh[h])
                dq[h][...] += _nn(dz[h][...], kb)

            _sb_first(thr, z[0], sp[0], ls[0])
            _sb_suffix_sums(uu_scr, sp[0], cin[0])
            _sb_first(thr, z[1], sp[1], ls[1])
            _sb_suffix_sums(uu_scr, sp[1], cin[1])
            weights(0)
            weights(1)
            grads(0)
            grads(1)
            return _sb_alive(carry_scr)

        def step(n, alive):
            return lax.cond(alive > 0, lambda: live(i - n), lambda: alive)

        lax.fori_loop(0, i + 1, step, jnp.int32(1))
        dq_ref[0] = (jnp.where(masks[0], dq_scr[0], dq_scr[1]) * SCALE).astype(dq_ref.dtype)
        ride_wait()

    p3 = proj.reshape(bl, s, width)
    o3 = o.reshape(bl, s, SB_WIDTH)
    d3 = dcat.reshape(bl, s, dcat.shape[1])
    qblk = pl.BlockSpec((1, SB_BLOCK, LANES), lambda b, p, i: (b, i, p))
    full = pl.BlockSpec((1, s, LANES), lambda b, p, i: (b, 0, p))
    extra = ride.arrays if ride else []
    dq, dk, dv, *delivered = pl.pallas_call(
        body, name=name, grid=grid,
        in_specs=[qblk, pl.BlockSpec((1, s, LANES), lambda b, p, i: (b, 0, npair + p)),
                  pl.BlockSpec((1, s, LANES), lambda b, p, i: (b, 0, 2 * npair + p)), qblk, qblk] + [ANY] * len(extra),
        out_specs=[qblk, full, full] + [ANY] * (len(ride.out_shape) if ride else 0),
        out_shape=[jax.ShapeDtypeStruct((bl, s, SB_WIDTH), BF), jax.ShapeDtypeStruct((bl, s, SB_WIDTH), F32),
                   jax.ShapeDtypeStruct((bl, s, SB_WIDTH), F32)] + (ride.out_shape if ride else []),
        scratch_shapes=own_scratch + (ride.scratch if ride else []),
        compiler_params=_params("arbitrary", "arbitrary", "arbitrary"),
    )(p3, p3, p3, o3, d3, *extra)
    return dq.reshape(t, SB_WIDTH), dk.reshape(t, SB_WIDTH), dv.reshape(t, SB_WIDTH), delivered


MEM_TQ = 512


def mem_fwd(proj, kvm, bl, *, name):
    t, width = proj.shape
    s = t // bl
    qoff = (width - MEM_WIDTH) // LANES
    npair = MEM_WIDTH // LANES

    def body(q_ref, k_ref, v_ref, o_ref):
        ma, mb = _lane_masks(MEM_TQ)
        q = q_ref[...].astype(BF)
        outs = []
        for m in (ma, mb):
            qh = jnp.where(m, q, jnp.zeros_like(q))
            sc = _nt(qh, k_ref[...]) * SCALE
            p = jnp.exp(sc - jnp.max(sc, axis=-1, keepdims=True))
            p = p * (1.0 / jnp.sum(p, axis=-1, keepdims=True))
            outs.append(_nn(p.astype(BF), v_ref[...]))
        o_ref[...] = jnp.where(ma, outs[0], outs[1]).astype(o_ref.dtype)

    nt = s // MEM_TQ
    return pl.pallas_call(
        body, name=name, grid=(bl, npair, nt),
        in_specs=[pl.BlockSpec((MEM_TQ, LANES), lambda b, p, i: (b * nt + i, qoff + p)),
                  pl.BlockSpec((MEM_LEN, LANES), lambda b, p, i: (b, p)),
                  pl.BlockSpec((MEM_LEN, LANES), lambda b, p, i: (b, npair + p))],
        out_specs=pl.BlockSpec((MEM_TQ, LANES), lambda b, p, i: (b * nt + i, p)),
        out_shape=jax.ShapeDtypeStruct((t, MEM_WIDTH), BF),
        compiler_params=_params("parallel", "parallel", "parallel"),
    )(proj, kvm, kvm)


def mem_bwd(proj, kvm, dcat, bl, *, name):
    t, width = proj.shape
    s = t // bl
    qoff = (width - MEM_WIDTH) // LANES
    doff = (dcat.shape[1] - MEM_WIDTH) // LANES
    npair = MEM_WIDTH // LANES
    nt = s // MEM_TQ

    def body(q_ref, k_ref, v_ref, do_ref, dq_ref, dk_ref, dv_ref):
        @pl.when(pl.program_id(2) == 0)
        def _():
            dk_ref[...] = jnp.zeros_like(dk_ref)
            dv_ref[...] = jnp.zeros_like(dv_ref)

        ma, mb = _lane_masks(MEM_TQ)
        q = q_ref[...].astype(BF)
        do = do_ref[...]
        kb, vb = k_ref[...], v_ref[...]
        dqs = []
        for m in (ma, mb):
            qh = jnp.where(m, q, jnp.zeros_like(q))
            doh = jnp.where(m, do, jnp.zeros_like(do))
            sc = _nt(qh, kb) * SCALE
            p = jnp.exp(sc - jnp.max(sc, axis=-1, keepdims=True))
            p = p * (1.0 / jnp.sum(p, axis=-1, keepdims=True))
            dp = _nt(doh, vb)
            ds = p * (dp - jnp.sum(p * dp, axis=-1, keepdims=True))
            dss = (ds * SCALE).astype(BF)
            dk_ref[...] += _tn(dss, qh)
            dv_ref[...] += _tn(p.astype(BF), doh)
            dqs.append(_nn(dss, kb))
        dq_ref[...] = jnp.where(ma, dqs[0], dqs[1]).astype(dq_ref.dtype)

    kblk = pl.BlockSpec((MEM_LEN, LANES), lambda b, p, i: (b, p))
    dq, dk, dv = pl.pallas_call(
        body, name=name, grid=(bl, npair, nt),
        in_specs=[pl.BlockSpec((MEM_TQ, LANES), lambda b, p, i: (b * nt + i, qoff + p)), kblk,
                  pl.BlockSpec((MEM_LEN, LANES), lambda b, p, i: (b, npair + p)),
                  pl.BlockSpec((MEM_TQ, LANES), lambda b, p, i: (b * nt + i, doff + p))],
        out_specs=[pl.BlockSpec((MEM_TQ, LANES), lambda b, p, i: (b * nt + i, p)), kblk, kblk],
        out_shape=[jax.ShapeDtypeStruct((t, MEM_WIDTH), BF), jax.ShapeDtypeStruct((bl * MEM_LEN, MEM_WIDTH), F32),
                   jax.ShapeDtypeStruct((bl * MEM_LEN, MEM_WIDTH), F32)],
        compiler_params=_params("parallel", "parallel", "arbitrary"),
    )(proj, kvm, kvm, dcat)
    return dq, jnp.concatenate([dk, dv], axis=1).astype(BF)


def _dil_rows(r, u0, size, dil):
    if dil == 1:
        return pl.ds(_mo(u0, QB), size)
    return pl.ds(u0 * dil + r, size, stride=dil)


def _dil_walk(block, dil, nb):
    def residue(r, c):
        block(r, 0, QB)
        if nb > 1:
            def step(n, c2):
                block(r, n, 2 * QB)
                return c2
            lax.fori_loop(1, nb, step, 0)
        return c

    if dil == 1:
        residue(0, 0)
    else:
        lax.fori_loop(0, dil, residue, 0)


def _dil_scores(qh, kb, slope_d, n_keys):
    i = lax.broadcasted_iota(jnp.int32, (QB, n_keys), 0)
    j = lax.broadcasted_iota(jnp.int32, (QB, n_keys), 1)
    delta = i + (n_keys - QB) - j
    valid = (delta >= 0) & (delta <= QB)
    sc = _nt(qh, kb) * SCALE - slope_d * delta.astype(F32)
    return jnp.where(valid, sc, NEG)


def _dil_slopes(g, dil):
    p = pl.program_id(1)
    sa = jnp.where(p == 0, ALIBI[4 * g] * dil, ALIBI[4 * g + 2] * dil).astype(F32)
    sb = jnp.where(p == 0, ALIBI[4 * g + 1] * dil, ALIBI[4 * g + 3] * dil).astype(F32)
    return sa, sb


def dil_fwd(projb, kv, g, bl, *, name):
    _, dil = DIL_GROUPS[g]
    t, wq = projb.shape
    wk = kv.shape[1]
    s = t // bl
    ln = s // dil
    nb = ln // QB
    gw = 4 * HEAD_DIM
    ck, co = wk // LANES, gw // LANES

    def body(q_ref, k_ref, v_ref, o_ref, lse_ref):
        sa, sb = _dil_slopes(g, dil)
        ma, mb = _lane_masks(QB)

        def block(r, n, n_keys):
            q0 = n * QB
            k0 = q0 - (n_keys - QB)
            q = q_ref[0, _dil_rows(r, q0, QB, dil), :].astype(BF)
            kb = k_ref[0, _dil_rows(r, k0, n_keys, dil), :].astype(BF)
            vb = v_ref[0, _dil_rows(r, k0, n_keys, dil), :].astype(BF)
            outs, lses = [], []
            for m, sl in ((ma, sa), (mb, sb)):
                qh = jnp.where(m, q, jnp.zeros_like(q))
                sc = _dil_scores(qh, kb, sl, n_keys)
                mx = jnp.max(sc, axis=-1, keepdims=True)
                p = jnp.exp(sc - mx)
                den = jnp.sum(p, axis=-1, keepdims=True)
                outs.append(_nn(p.astype(BF), vb) * (1.0 / den))
                lses.append(mx + jnp.log(den))
            o_ref[0, _dil_rows(r, q0, QB, dil), :] = jnp.where(ma, outs[0], outs[1])
            lse_ref[0, _dil_rows(r, q0, QB, dil), :] = jnp.where(ma, lses[0], lses[1])

        _dil_walk(block, dil, nb)

    colblk = lambda off: pl.BlockSpec((1, s, LANES), lambda b, p: (b, 0, off + p))
    o, lse = pl.pallas_call(
        body, name=name, grid=(bl, co),
        in_specs=[colblk(g * co), colblk(g * co), colblk(ck // 2 + g * co)],
        out_specs=[colblk(0), colblk(0)],
        out_shape=[jax.ShapeDtypeStruct((bl, s, gw), F32), jax.ShapeDtypeStruct((bl, s, gw), F32)],
        compiler_params=_params("parallel", "parallel"),
    )(projb.reshape(bl, s, wq), kv.reshape(bl, s, wk), kv.reshape(bl, s, wk))
    return o.reshape(t, gw), lse.reshape(t, gw)


def dil_bwd(projb, kv, lse, dog, dshift, g, bl, *, name):
    _, dil = DIL_GROUPS[g]
    t, wq = projb.shape
    wk = kv.shape[1]
    s = t // bl
    ln = s // dil
    nb = ln // QB
    gw = 4 * HEAD_DIM
    ck, co = wk // LANES, gw // LANES

    def body(q_ref, k_ref, v_ref, lse_ref, do_ref, sh_ref, dq_ref, dk_ref, dv_ref):
        sa, sb = _dil_slopes(g, dil)
        ma, mb = _lane_masks(QB)
        dk_ref[...] = jnp.zeros_like(dk_ref)
        dv_ref[...] = jnp.zeros_like(dv_ref)

        def block(r, n, n_keys):
            q0 = n * QB
            k0 = q0 - (n_keys - QB)
            qrows, krows = _dil_rows(r, q0, QB, dil), _dil_rows(r, k0, n_keys, dil)
            q = q_ref[0, qrows, :].astype(BF)
            do = do_ref[0, qrows, :].astype(BF)
            lse_b = lse_ref[0, qrows, :]
            sh_b = sh_ref[0, qrows, :]
            kb = k_ref[0, krows, :].astype(BF)
            vb = v_ref[0, krows, :].astype(BF)
            dqs = []
            for m, sl, c0 in ((ma, sa, 0), (mb, sb, HEAD_DIM)):
                qh = jnp.where(m, q, jnp.zeros_like(q))
                doh = jnp.where(m, do, jnp.zeros_like(do))
                sc = _dil_scores(qh, kb, sl, n_keys)
                p = jnp.exp(sc - lse_b[:, c0:c0 + 1])
                ds = p * (_nt(doh, vb) - sh_b[:, c0:c0 + 1])
                dss = (ds * SCALE).astype(BF)
                dk_ref[0, krows, :] += _tn(dss, qh)
                dv_ref[0, krows, :] += _tn(p.astype(BF), doh)
                dqs.append(_nn(dss, kb))
            dq_ref[0, qrows, :] = jnp.where(ma, dqs[0], dqs[1])

        _dil_walk(block, dil, nb)

    colblk = lambda off: pl.BlockSpec((1, s, LANES), lambda b, p: (b, 0, off + p))
    dq, dk, dv = pl.pallas_call(
        body, name=name, grid=(bl, co),
        in_specs=[colblk(g * co), colblk(g * co), colblk(ck // 2 + g * co), colblk(0), colblk(g * co), colblk(g * co)],
        out_specs=[colblk(0)] * 3,
        out_shape=[jax.ShapeDtypeStruct((bl, s, gw), F32)] * 3,
        compiler_params=_params("parallel", "parallel"),
    )(projb.reshape(bl, s, wq), kv.reshape(bl, s, wk), kv.reshape(bl, s, wk), lse.reshape(bl, s, gw),
      dog.reshape(bl, s, DIL_WIDTH), dshift.reshape(bl, s, DIL_WIDTH))
    return dq.reshape(t, gw), dk.reshape(t, gw), dv.reshape(t, gw)


def _group_weights(lses):
    mx = jnp.maximum(jnp.maximum(lses[0], lses[1]), lses[2])
    es = [jnp.exp(l - mx) for l in lses]
    inv = 1.0 / (es[0] + es[1] + es[2])
    return [e * inv for e in es]


def dil_combine_fwd(os_, lses, *, name):
    t, gw = os_[0].shape
    tr = _pick(t, (512, 256))

    def body(o0, o1, o2, l0, l1, l2, out_ref):
        al = _group_weights([l0[...], l1[...], l2[...]])
        for g, o_ref in enumerate((o0, o1, o2)):
            out_ref[:, g * gw:(g + 1) * gw] = (o_ref[...] * al[g]).astype(out_ref.dtype)

    blk = pl.BlockSpec((tr, gw), lambda i: (i, 0))
    return pl.pallas_call(
        body, name=name, grid=(t // tr,), in_specs=[blk] * 6,
        out_specs=pl.BlockSpec((tr, 3 * gw), lambda i: (i, 0)), out_shape=jax.ShapeDtypeStruct((t, 3 * gw), BF),
        compiler_params=_params("parallel"),
    )(*os_, *lses)


def dil_combine_bwd(os_, lses, dcat, *, name):
    t, gw = os_[0].shape
    tr = _pick(t, (512, 256))

    def head_sum(v):
        parts = []
        for c in range(gw // LANES):
            blk = v[:, c * LANES:(c + 1) * LANES]
            ma, _ = _lane_masks(tr)
            sa = jnp.sum(jnp.where(ma, blk, 0.0), axis=1, keepdims=True)
            sb = jnp.sum(blk, axis=1, keepdims=True) - sa
            parts.append(jnp.where(ma, sa, sb))
        return jnp.concatenate(parts, axis=1)

    def body(o0, o1, o2, l0, l1, l2, d_ref, dog_ref, sh_ref):
        al = _group_weights([l0[...], l1[...], l2[...]])
        dos = [d_ref[:, g * gw:(g + 1) * gw].astype(F32) for g in range(3)]
        dal = [head_sum(dos[g] * o_ref[...]) for g, o_ref in enumerate((o0, o1, o2))]
        mix = al[0] * dal[0] + al[1] * dal[1] + al[2] * dal[2]
        for g in range(3):
            dog_ref[:, g * gw:(g + 1) * gw] = (al[g] * dos[g]).astype(dog_ref.dtype)
            sh_ref[:, g * gw:(g + 1) * gw] = al[g] * mix

    blk = pl.BlockSpec((tr, gw), lambda i: (i, 0))
    wide = pl.BlockSpec((tr, 3 * gw), lambda i: (i, 0))
    return pl.pallas_call(
        body, name=name, grid=(t // tr,), in_specs=[blk] * 6 + [wide], out_specs=[wide, wide],
        out_shape=[jax.ShapeDtypeStruct((t, 3 * gw), F32), jax.ShapeDtypeStruct((t, 3 * gw), F32)],
        compiler_params=_params("parallel"),
    )(*os_, *lses, dcat)


def adamw(w, g1, g2, m, v, *, name):
    r, c = w.shape
    tr = r
    for cand in (256, 128, 64, 32, 16, 8):
        if r % cand == 0 and cand * c * 4 <= (1 << 20):
            tr = cand
            break
    two = g2 is not None

    def body(*refs):
        if two:
            w_ref, g1_ref, g2_ref, m_ref, v_ref, g_ref, d_ref, nm_ref, nv_ref = refs
            g = g1_ref[...] + g2_ref[...]
        else:
            w_ref, g1_ref, m_ref, v_ref, g_ref, d_ref, nm_ref, nv_ref = refs
            g = g1_ref[...]
        nm = ADAM_B1 * m_ref[...] + (1.0 - ADAM_B1) * g
        nv = ADAM_B2 * v_ref[...] + (1.0 - ADAM_B2) * (g * g)
        m_hat = nm / (1.0 - ADAM_B1 ** ADAM_STEP)
        v_hat = nv / (1.0 - ADAM_B2 ** ADAM_STEP)
        g_ref[...] = g
        d_ref[...] = -ADAM_LR * (m_hat / (jnp.sqrt(v_hat) + ADAM_EPS) + ADAM_WD * w_ref[...])
        nm_ref[...] = nm
        nv_ref[...] = nv

    blk = pl.BlockSpec((tr, c), lambda i: (i, 0))
    args = [w, g1] + ([g2] if two else []) + [m, v]
    return pl.pallas_call(
        body, name=name, grid=(r // tr,), in_specs=[blk] * len(args), out_specs=[blk] * 4,
        out_shape=[jax.ShapeDtypeStruct((r, c), F32)] * 4, compiler_params=_params("parallel"),
    )(*args)


def sum4(own, land, *, name):
    r, c = own.shape
    tr = _pick(r, (256, 128, 64))

    def body(o_ref, l_ref, s_ref):
        s_ref[...] = ((o_ref[...].astype(F32) + l_ref[0].astype(F32)) + l_ref[1].astype(F32)) + l_ref[2].astype(F32)

    return pl.pallas_call(
        body, name=name, grid=(r // tr,),
        in_specs=[pl.BlockSpec((tr, c), lambda i: (i, 0)), pl.BlockSpec((3, tr, c), lambda i: (0, i, 0))],
        out_specs=pl.BlockSpec((tr, c), lambda i: (i, 0)), out_shape=jax.ShapeDtypeStruct((r, c), F32),
        compiler_params=_params("parallel"),
    )(own, land)


ANY = pl.BlockSpec(memory_space=pl.ANY)


def _place():
    x, y, c = lax.axis_index("x"), lax.axis_index("y"), lax.axis_index("c")
    chips = [(1 - x, y), (x, 1 - y), (1 - x, 1 - y)]
    return x, y, c, chips


class Ride:
    def __init__(self, arrays, out_shape, copies):
        self.arrays, self.out_shape, self.copies = list(arrays), list(out_shape), copies
        n = 3 * len(self.arrays)
        self.scratch = [pltpu.SemaphoreType.DMA((n,)), pltpu.SemaphoreType.DMA((n,)),
                        pltpu.SemaphoreType.DMA((len(self.arrays),))]

    def split(self, refs):
        n, m = len(self.arrays), len(self.out_shape)
        return refs[:n], refs[n:n + m], refs[n + m:]

    def start(self, ins, outs, sems):
        local, sends, _ = self.copies(ins, outs, *sems)
        for cp in local + sends:
            cp.start()

    def wait(self, ins, outs, sems):
        local, sends, arrivals = self.copies(ins, outs, *sems)
        for cp in arrivals:
            cp.wait_recv()
        for cp in sends:
            cp.wait_send()
        for cp in local:
            cp.wait()

    def run(self, name):
        def body(*refs):
            ins, outs, sems = self.split(refs)
            self.start(ins, outs, sems)
            self.wait(ins, outs, sems)

        return pl.pallas_call(body, name=name, in_specs=[ANY] * len(self.arrays), out_specs=[ANY] * len(self.out_shape),
                              out_shape=self.out_shape, scratch_shapes=self.scratch)(*self.arrays)


def gather_ride(shards, axes):
    def copies(ins, outs, send_sems, recv_sems, local_sems):
        x, y, c, chips = _place()

        def slot(a, q):
            size = shards[a].shape[axes[a]]
            start = pl.multiple_of(q * size, size)
            return outs[a].at[pl.ds(start, size), :] if axes[a] == 0 else outs[a].at[:, pl.ds(start, size)]

        def remote(a, k, q):
            px, py = chips[k]
            return pltpu.make_async_remote_copy(src_ref=ins[a], dst_ref=slot(a, q), send_sem=send_sems.at[3 * a + k],
                                                recv_sem=recv_sems.at[3 * a + k], device_id=(px, py, c), device_id_type=MESH)

        me = 2 * x + y
        n = len(shards)
        local = [pltpu.make_async_copy(ins[a], slot(a, me), local_sems.at[a]) for a in range(n)]
        sends = [remote(a, k, me) for a in range(n) for k in range(3)]
        arrivals = [remote(a, k, 2 * chips[k][0] + chips[k][1]) for a in range(n) for k in range(3)]
        return local, sends, arrivals

    out_shape = []
    for a, sh in enumerate(shards):
        full = list(sh.shape)
        full[axes[a]] *= N_CHIPS
        out_shape.append(jax.ShapeDtypeStruct(tuple(full), sh.dtype))
    return Ride(shards, out_shape, copies)


def scatter_ride(grads):
    def copies(ins, outs, send_sems, recv_sems, local_sems):
        x, y, c, chips = _place()
        sends = [pltpu.make_async_remote_copy(src_ref=ins[a].at[2 * px + py], dst_ref=outs[a].at[k],
                                              send_sem=send_sems.at[3 * a + k], recv_sem=recv_sems.at[3 * a + k],
                                              device_id=(px, py, c), device_id_type=MESH)
                 for a in range(len(grads)) for k, (px, py) in enumerate(chips)]
        return [], sends, sends

    return Ride(grads, [jax.ShapeDtypeStruct((3,) + g.shape[1:], g.dtype) for g in grads], copies)


def swap_with_sibling(arrs, *, name):
    n = len(arrs)

    def body(*refs):
        ins, outs = refs[:n], refs[n:2 * n]
        send_sems, recv_sems = refs[2 * n:]
        x, y, c, _ = _place()
        sends = []
        for a in range(n):
            cp = pltpu.make_async_remote_copy(src_ref=ins[a], dst_ref=outs[a], send_sem=send_sems.at[a],
                                              recv_sem=recv_sems.at[a], device_id=(x, y, 1 - c), device_id_type=MESH)
            cp.start()
            sends.append(cp)
        for cp in sends:
            cp.wait_recv()
        for cp in sends:
            cp.wait_send()

    return pl.pallas_call(
        body, name=name, in_specs=[ANY] * n, out_specs=[ANY] * n,
        out_shape=[jax.ShapeDtypeStruct(a.shape, a.dtype) for a in arrs],
        scratch_shapes=[pltpu.SemaphoreType.DMA((n,)), pltpu.SemaphoreType.DMA((n,))],
    )(*arrs)


def all_reduce_small(v, *, name):
    rows = v.shape[0]

    def body(v_ref, o_ref, gath, send_sems, recv_sems):
        x, y, c, _ = _place()
        me = 4 * x + 2 * y + c
        gath[me] = v_ref[...]
        sends = []
        for msk in range(1, N_DEV):
            peer = (x ^ (msk >> 2), y ^ ((msk >> 1) & 1), c ^ (msk & 1))
            cp = pltpu.make_async_remote_copy(src_ref=v_ref, dst_ref=gath.at[me], send_sem=send_sems.at[msk - 1],
                                              recv_sem=recv_sems.at[msk - 1], device_id=peer, device_id_type=MESH)
            cp.start()
            sends.append(cp)
        for msk in range(1, N_DEV):
            pltpu.make_async_remote_copy(src_ref=v_ref, dst_ref=gath.at[me ^ msk], send_sem=send_sems.at[msk - 1],
                                         recv_sem=recv_sems.at[msk - 1], device_id=(x, y, c), device_id_type=MESH).wait_recv()
        for cp in sends:
            cp.wait_send()
        tot = gath[0]
        for q in range(1, N_DEV):
            tot = tot + gath[q]
        o_ref[...] = tot

    vm = pl.BlockSpec(memory_space=pltpu.VMEM)
    return pl.pallas_call(
        body, name=name, in_specs=[vm], out_specs=vm, out_shape=jax.ShapeDtypeStruct(v.shape, F32),
        scratch_shapes=[pltpu.VMEM((N_DEV, rows, LANES), F32), pltpu.SemaphoreType.DMA((N_DEV - 1,)),
                        pltpu.SemaphoreType.DMA((N_DEV - 1,))],
    )(v)


def _ffn_fwd(xin, gain, w_up, w_conv, w_down, bl, tag):
    n = rmsnorm_fwd(xin, gain, name=f"{tag}_ffn_norm")
    u = matmul(n, w_up, name=f"{tag}_ffn_up")
    h = conv_silu_fwd(u, w_conv, bl, name=f"{tag}_ffn_conv")
    xout = matmul(h, w_down, out_dtype=F32, add=xin, name=f"{tag}_ffn_down")
    return xout, (n, u, h)


def _ffn_bwd(dxout, dxout_b, xin, saved, gain, w_up, w_conv, w_down, bl, tag, shard):
    n, u, h = saved
    dh = matmul(dxout_b, w_down, tb=True, name=f"{tag}_ffn_down_dx")
    g_down = matmul(h, dxout_b, ta=True, name=f"{tag}_ffn_down_dw")
    du, g_conv = conv_silu_bwd(u, w_conv, dh, bl, name=f"{tag}_ffn_conv_bwd")
    dn = matmul(du, w_up, tb=True, out_dtype=F32, name=f"{tag}_ffn_up_dx")
    g_up = matmul(n, du, ta=True, shard_cols=shard, name=f"{tag}_ffn_up_dw")
    dxin, dxin_b, g_norm = rmsnorm_bwd(xin, gain, dn, dxout, name=f"{tag}_ffn_norm_bwd")
    return dxin, dxin_b, g_norm, g_up, g_conv, g_down


def _mem_kv(mem2, gain, w_kv, tag):
    mn = rmsnorm_fwd(mem2, gain, name=f"{tag}_mem_norm")
    return mn, matmul(mn, w_kv, name=f"{tag}_mem_kv")


def _mem_kv_bwd(mem2, mn, gain, w_kv, dkvm, tag):
    g_kv = matmul(mn, dkvm, ta=True, name=f"{tag}_mem_kv_dw")
    dmn = matmul(dkvm, w_kv, tb=True, out_dtype=F32, name=f"{tag}_mem_kv_dx")
    _, _, g_norm = rmsnorm_bwd(mem2, gain, dmn, None, name=f"{tag}_mem_norm_bwd")
    return g_norm, g_kv


def _shard_major(name, grad):
    if name in COL_SHARDED:
        return grad
    return grad.reshape(N_CHIPS, grad.shape[0] // N_CHIPS, grad.shape[1])


def local_step(x, mem, target, w, shard_of, late_gather=None, early_scatter=()):
    bl, s, d = x.shape
    t = bl * s
    x0 = x.reshape(t, d)
    mem2 = mem.reshape(bl * MEM_LEN, d)
    tgt = target.reshape(t, d)
    g = {}

    n1 = rmsnorm_fwd(x0, w["a_norm_attn"], name="a_attn_norm")
    proj_a = matmul(n1, w["a_w_in"], name="a_w_in")
    mn_a, kvm_a = _mem_kv(mem2, w["a_norm_mem"], w["a_w_mem_kv"], "a")
    o_sb, o_sb_f32, delivered = sb_fwd(proj_a, bl, name="a_sb_fwd", ride=late_gather[0] if late_gather else None)
    if late_gather:
        w = {**w, **dict(zip(late_gather[1], delivered))}
    o_mem_a = mem_fwd(proj_a, kvm_a, bl, name="a_mem_fwd")
    cat_a = jnp.concatenate([o_sb, o_mem_a], axis=1)
    x1 = matmul(cat_a, w["a_w_out"], out_dtype=F32, add=x0, name="a_w_out")
    x2, ffn_a = _ffn_fwd(x1, w["a_norm_ffn"], w["a_ffn_up"], w["a_ffn_conv"], w["a_ffn_down"], bl, "a")

    nkv = rmsnorm_fwd(x2, w["kv_norm"], name="kv_norm")
    kv = matmul(nkv, w["w_kv_shared"], out_dtype=F32, name="w_kv")
    n3 = rmsnorm_fwd(x2, w["b_norm_attn"], name="b_attn_norm")
    proj_b = matmul(n3, w["b_w_in"], out_dtype=F32, name="b_w_in")
    mn_b, kvm_b = _mem_kv(mem2, w["b_norm_mem"], w["b_w_mem_kv"], "b")
    dil = [dil_fwd(proj_b, kv, gi, bl, name=f"b_dil_fwd{gi}") for gi in range(3)]
    os_, lses = [o for o, _ in dil], [l for _, l in dil]
    o_dil = dil_combine_fwd(os_, lses, name="b_dil_combine")
    o_mem_b = mem_fwd(proj_b, kvm_b, bl, name="b_mem_fwd")
    cat_b = jnp.concatenate([o_dil, o_mem_b], axis=1)
    x3 = matmul(cat_b, w["b_w_out"], out_dtype=F32, add=x2, name="b_w_out")
    x4, ffn_b = _ffn_fwd(x3, w["b_norm_ffn"], w["b_ffn_up"], w["b_ffn_conv"], w["b_ffn_down"], bl, "b")

    dx4, dx4b, g["final_norm"], lossvec = final_loss(x4, w["final_norm"], tgt, name="final_loss")

    dx3, dx3b, g["b_norm_ffn"], g["b_ffn_up"], g["b_ffn_conv"], g["b_ffn_down"] = _ffn_bwd(
        dx4, dx4b, x3, ffn_b, w["b_norm_ffn"], w["b_ffn_up"], w["b_ffn_conv"], w["b_ffn_down"], bl, "b",
        shard_of["b_ffn_up"])
    dcat_b = matmul(dx3b, w["b_w_out"], tb=True, name="b_w_out_dx")
    g["b_w_out"] = matmul(cat_b, dx3b, ta=True, name="b_w_out_dw")
    dog, dshift = dil_combine_bwd(os_, lses, dcat_b, name="b_dil_combine_bwd")
    dqs, dks, dvs = [], [], []
    for gi in range(3):
        dq_g, dk_g, dv_g = dil_bwd(proj_b, kv, lses[gi], dog, dshift, gi, bl, name=f"b_dil_bwd{gi}")
        dqs.append(dq_g), dks.append(dk_g), dvs.append(dv_g)
    dq_mem_b, dkvm_b = mem_bwd(proj_b, kvm_b, dcat_b, bl, name="b_mem_bwd")
    g["b_norm_mem"], g["b_w_mem_kv"] = _mem_kv_bwd(mem2, mn_b, w["b_norm_mem"], w["b_w_mem_kv"], dkvm_b, "b")
    dproj_b = jnp.concatenate([dq_g.astype(BF) for dq_g in dqs] + [dq_mem_b], axis=1)
    dn3 = matmul(dproj_b, w["b_w_in"], tb=True, out_dtype=F32, name="b_w_in_dx")
    g["b_w_in"] = matmul(n3, dproj_b, ta=True, name="b_w_in_dw")
    dx2, _, g["b_norm_attn"] = rmsnorm_bwd(x2, w["b_norm_attn"], dn3, dx3, name="b_attn_norm_bwd")
    dkv = jnp.concatenate(dks + dvs, axis=1).astype(BF)
    dnkv = matmul(dkv, w["w_kv_shared"], tb=True, out_dtype=F32, name="w_kv_dx")
    g["w_kv_shared"] = matmul(nkv, dkv, ta=True, shard_cols=shard_of["w_kv_shared"], name="w_kv_dw")
    dx2, dx2b, g["kv_norm"] = rmsnorm_bwd(x2, w["kv_norm"], dnkv, dx2, name="kv_norm_bwd")

    dx1, dx1b, g["a_norm_ffn"], g["a_ffn_up"], g["a_ffn_conv"], g["a_ffn_down"] = _ffn_bwd(
        dx2, dx2b, x1, ffn_a, w["a_norm_ffn"], w["a_ffn_up"], w["a_ffn_conv"], w["a_ffn_down"], bl, "a",
        shard_of["a_ffn_up"])
    dcat_a = matmul(dx1b, w["a_w_out"], tb=True, name="a_w_out_dx")
    g["a_w_out"] = matmul(cat_a, dx1b, ta=True, name="a_w_out_dw")
    leaving = scatter_ride([_shard_major(n, g[n]) for n in early_scatter]) if early_scatter else None
    dq_sb, dk_sb, dv_sb, landed = sb_bwd(proj_a, o_sb_f32, dcat_a, bl, name="a_sb_bwd", ride=leaving)
    landed = dict(zip(early_scatter, landed))
    dq_mem_a, dkvm_a = mem_bwd(proj_a, kvm_a, dcat_a, bl, name="a_mem_bwd")
    g["a_norm_mem"], g["a_w_mem_kv"] = _mem_kv_bwd(mem2, mn_a, w["a_norm_mem"], w["a_w_mem_kv"], dkvm_a, "a")
    dproj_a = jnp.concatenate([dq_sb, dk_sb.astype(BF), dv_sb.astype(BF), dq_mem_a], axis=1)
    dn1 = matmul(dproj_a, w["a_w_in"], tb=True, out_dtype=F32, name="a_w_in_dx")
    g["a_w_in"] = matmul(n1, dproj_a, ta=True, shard_cols=shard_of["a_w_in"], name="a_w_in_dw")
    dx0, _, g["a_norm_attn"] = rmsnorm_bwd(x0, w["a_norm_attn"], dn1, dx1, name="a_attn_norm_bwd")
    return lossvec, dx0, g, landed


MATRICES = ("a_w_in", "a_w_out", "a_w_mem_kv", "a_ffn_up", "a_ffn_down", "w_kv_shared", "b_w_in", "b_w_out",
            "b_w_mem_kv", "b_ffn_up", "b_ffn_down")
COL_SHARDED = ("a_w_in", "a_ffn_up", "w_kv_shared", "b_ffn_up")
FIRST_NEEDED = ("a_w_in", "a_w_mem_kv")
SMALL_SHARDED = ("a_norm_attn", "a_norm_mem", "a_norm_ffn", "a_ffn_conv", "b_ffn_conv")
SMALL_REPLICATED = ("kv_norm", "b_norm_attn", "b_norm_mem", "b_norm_ffn", "final_norm")
WEIGHTS = ("a_norm_attn", "a_w_in", "a_w_out", "a_norm_mem", "a_w_mem_kv", "a_norm_ffn", "a_ffn_up", "a_ffn_conv",
           "a_ffn_down", "kv_norm", "w_kv_shared", "b_norm_attn", "b_w_in", "b_w_out", "b_norm_mem", "b_w_mem_kv",
           "b_norm_ffn", "b_ffn_up", "b_ffn_conv", "b_ffn_down", "final_norm")


def _two_d(a):
    if a.ndim == 1:
        return a.reshape(1, -1)
    return a.reshape(a.shape[-2], a.shape[-1])


def kernel(x, mem, a_norm_attn, a_w_in, a_w_out, a_norm_mem, a_w_mem_kv, a_norm_ffn, a_ffn_up, a_ffn_conv, a_ffn_down, kv_norm, w_kv_shared, b_norm_attn, b_w_in, b_w_out, b_norm_mem, b_w_mem_kv, b_norm_ffn, b_ffn_up, b_ffn_conv, b_ffn_down, final_norm, loss_target, m_a_norm_attn, m_a_w_in, m_a_w_out, m_a_norm_mem, m_a_w_mem_kv, m_a_norm_ffn, m_a_ffn_up, m_a_ffn_conv, m_a_ffn_down, m_kv_norm, m_w_kv_shared, m_b_norm_attn, m_b_w_in, m_b_w_out, m_b_norm_mem, m_b_w_mem_kv, m_b_norm_ffn, m_b_ffn_up, m_b_ffn_conv, m_b_ffn_down, m_final_norm, v_a_norm_attn, v_a_w_in, v_a_w_out, v_a_norm_mem, v_a_w_mem_kv, v_a_norm_ffn, v_a_ffn_up, v_a_ffn_conv, v_a_ffn_down, v_kv_norm, v_w_kv_shared, v_b_norm_attn, v_b_w_in, v_b_w_out, v_b_norm_mem, v_b_w_mem_kv, v_b_norm_ffn, v_b_ffn_up, v_b_ffn_conv, v_b_ffn_down, v_final_norm):
    given = dict(locals())
    wl = {n: _two_d(given[n]) for n in WEIGHTS}
    ml = {n: _two_d(given["m_" + n]) for n in WEIGHTS}
    vl = {n: _two_d(given["v_" + n]) for n in WEIGHTS}
    chip = 2 * lax.axis_index("x") + lax.axis_index("y")

    packed = jnp.concatenate([wl[n].reshape(-1, LANES) for n in SMALL_SHARDED], axis=0)
    axis_of = lambda n: 1 if n in COL_SHARDED else 0
    late = tuple(n for n in MATRICES if n not in FIRST_NEEDED)
    full = gather_ride([wl[n].astype(BF) for n in FIRST_NEEDED] + [packed],
                       [axis_of(n) for n in FIRST_NEEDED] + [0]).run("gather_first")
    w = dict(zip(FIRST_NEEDED, full[:-1]))
    late_gather = (gather_ride([wl[n].astype(BF) for n in late], [axis_of(n) for n in late]), late)
    rows = packed.shape[0]
    per_chip = full[-1].reshape(N_CHIPS, rows, LANES)
    r0 = 0
    for n in SMALL_SHARDED:
        nr = wl[n].size // LANES
        piece = per_chip[:, r0:r0 + nr].reshape(N_CHIPS, wl[n].shape[0], wl[n].shape[1])
        w[n] = jnp.concatenate([piece[q] for q in range(N_CHIPS)], axis=1)
        r0 += nr
    for n in SMALL_REPLICATED:
        w[n] = wl[n]

    shard_of = {n: wl[n].shape[1] for n in COL_SHARDED}
    lossvec, dx0, g, landed = local_step(x, mem, loss_target, w, shard_of, late_gather=late_gather, early_scatter=late)
    loss = lax.psum(0.5 * jnp.sum(lossvec) / x.shape[-1], ("x", "y", "c"))

    g4 = {n: _shard_major(n, g[n]) for n in MATRICES}
    landed.update(zip(FIRST_NEEDED, scatter_ride([g4[n] for n in FIRST_NEEDED]).run("scatter_last")))
    sums = [sum4(lax.dynamic_index_in_dim(g4[n], chip, 0, keepdims=False), landed[n], name=f"sum4_{n}")
            for n in MATRICES]
    theirs = swap_with_sibling(sums, name="swap_sums")
    out = {}
    for k, n in enumerate(MATRICES):
        out[n] = adamw(wl[n], sums[k], theirs[k], ml[n], vl[n], name=f"adamw_{n}")

    small = SMALL_SHARDED + SMALL_REPLICATED
    flat = jnp.concatenate([g[n].reshape(-1, LANES) for n in small], axis=0)
    tot = all_reduce_small(flat, name="all_reduce_small")
    r0 = 0
    for n in small:
        nr = g[n].size // LANES
        gn = tot[r0:r0 + nr].reshape(g[n].shape)
        r0 += nr
        if n in SMALL_SHARDED:
            gn = lax.dynamic_slice_in_dim(gn, chip * wl[n].shape[1], wl[n].shape[1], axis=1)
        out[n] = adamw(wl[n], gn, None, ml[n], vl[n], name=f"adamw_{n}")

    res = [loss, dx0.reshape(x.shape)]
    for slot in range(4):
        res += [out[n][slot].reshape(given[n].shape) for n in WEIGHTS]
    return tuple(res)
```

```python
import functools
import math

import jax
import jax.numpy as jnp
from jax import lax
from jax.experimental import pallas as pl
from jax.experimental.pallas import tpu as pltpu

F32 = jnp.float32
BF = jnp.bfloat16
MESH = pl.DeviceIdType.MESH

HEAD_DIM = 64
LANES = 128
SB_WIDTH = 12 * HEAD_DIM
MEM_WIDTH = 4 * HEAD_DIM
DIL_WIDTH = 12 * HEAD_DIM
MEM_LEN = 256
DIL_GROUPS = ((128, 1), (512, 4), (2048, 16))
QB = 128
EPS = 1e-6
SCALE = HEAD_DIM ** -0.5
NEG = -1e30
ALIBI = tuple(2.0 ** (-8.0 * i / 12) for i in range(1, 13))
N_CHIPS = 4
N_DEV = 8

ADAM_LR, ADAM_B1, ADAM_B2, ADAM_EPS, ADAM_WD, ADAM_STEP = 0.001, 0.9, 0.999, 1e-08, 0.01, 10

VMEM_LIMIT = 48 * 1024 * 1024


def _mo(v, m):
    return v if isinstance(v, int) else pl.multiple_of(v, m)


def _pick(n, prefs):
    for t in prefs:
        if n % t == 0:
            return t
    return n


def _params(*sem):
    return pltpu.CompilerParams(dimension_semantics=sem, vmem_limit_bytes=VMEM_LIMIT)


def matmul(a, b, *, ta=False, tb=False, out_dtype=BF, add=None, shard_cols=0, a2=None, b2=None, name):
    m, k = (a.shape[1], a.shape[0]) if ta else a.shape
    n = b.shape[0] if tb else b.shape[1]
    if a2 is not None:
        assert not ta and a2.shape == a.shape
        k *= 2
    if b2 is not None:
        assert not tb and b2.shape == b.shape
        n *= 2
    tm = _pick(m, (512, 1408, 256, 128))
    tk = _pick(a.shape[1] if a2 is not None else k,
               (2048, 2816, 2560, 1536, 1024, 512, 256, 128))
    tn = shard_cols if shard_cols else _pick(b.shape[1] if b2 is not None else n, (1024, 1408, 512, 256, 128))
    nk = k // tk
    nk1 = nk // 2
    nj1 = (n // tn) // 2
    dims = (((0,) if ta else (1,), (1,) if tb else (0,)), ((), ()))
    n_in = 2 + (a2 is not None) + (b2 is not None) + (add is not None)

    def body(*refs):
        ins = list(refs[:n_in])
        o_ref = refs[n_in]
        a_tile = ins.pop(0)[...]
        b_tile = ins.pop(0)[...]
        if a2 is not None:
            a_tile = jnp.where(pl.program_id(2) < nk1, a_tile, ins.pop(0)[...])
        if b2 is not None:
            b_tile = jnp.where(pl.program_id(1) < nj1, b_tile, ins.pop(0)[...])
        add_ref = ins.pop(0) if add is not None else None
        part = lax.dot_general(a_tile.astype(BF), b_tile.astype(BF), dims, preferred_element_type=F32)

        def finish(r):
            if add is not None:
                r = r + add_ref[...]
            o_ref[...] = r.astype(o_ref.dtype)

        if nk == 1:
            finish(part)
            return
        acc_ref = refs[-1]
        kk = pl.program_id(2)

        @pl.when(kk == 0)
        def _():
            acc_ref[...] = part

        @pl.when(kk > 0)
        def _():
            acc_ref[...] += part

        @pl.when(kk == nk - 1)
        def _():
            finish(acc_ref[...])

    a_spec = pl.BlockSpec((tk, tm), lambda i, j, q: (q, i)) if ta else pl.BlockSpec((tm, tk), lambda i, j, q: (i, q))
    b_spec = pl.BlockSpec((tn, tk), lambda i, j, q: (j, q)) if tb else pl.BlockSpec((tk, tn), lambda i, j, q: (q, j))
    if a2 is not None:
        a_spec = pl.BlockSpec((tm, tk), lambda i, j, q: (i, jnp.minimum(q, nk1 - 1)))
    if b2 is not None:
        b_spec = pl.BlockSpec((tk, tn), lambda i, j, q: (jnp.where(j < nj1, q, 0), jnp.minimum(j, nj1 - 1)))
    in_specs = [a_spec, b_spec]
    args = [a, b]
    if a2 is not None:
        in_specs.append(pl.BlockSpec((tm, tk), lambda i, j, q: (i, jnp.maximum(q - nk1, 0))))
        args.append(a2)
    if b2 is not None:
        in_specs.append(pl.BlockSpec((tk, tn), lambda i, j, q: (jnp.where(j < nj1, 0, q), jnp.maximum(j - nj1, 0))))
        args.append(b2)
    if add is not None:
        in_specs.append(pl.BlockSpec((tm, tn), lambda i, j, q: (i, j)))
        args.append(add)
    if shard_cols:
        out_shape = jax.ShapeDtypeStruct((N_CHIPS, m, shard_cols), out_dtype)
        out_spec = pl.BlockSpec((None, tm, tn), lambda i, j, q: (j, i, 0))
    else:
        out_shape = jax.ShapeDtypeStruct((m, n), out_dtype)
        out_spec = pl.BlockSpec((tm, tn), lambda i, j, q: (i, j))
    return pl.pallas_call(
        body, name=name, grid=(m // tm, n // tn, nk), in_specs=in_specs, out_specs=out_spec, out_shape=out_shape,
        scratch_shapes=[pltpu.VMEM((tm, tn), F32)] if nk > 1 else [],
        compiler_params=_params("parallel", "parallel", "arbitrary"),
    )(*args)


def rmsnorm_fwd(x, g, *, name):
    t, d = x.shape
    tr = _pick(t, (512, 256))

    def body(x_ref, g_ref, o_ref):
        xv = x_ref[...]
        r = lax.rsqrt(jnp.mean(xv * xv, axis=-1, keepdims=True) + EPS)
        o_ref[...] = ((xv * r) * g_ref[...]).astype(o_ref.dtype)

    return pl.pallas_call(
        body, name=name, grid=(t // tr,),
        in_specs=[pl.BlockSpec((tr, d), lambda i: (i, 0)), pl.BlockSpec((1, d), lambda i: (0, 0))],
        out_specs=pl.BlockSpec((tr, d), lambda i: (i, 0)), out_shape=jax.ShapeDtypeStruct((t, d), BF),
        compiler_params=_params("parallel"),
    )(x, g)


def rmsnorm_bwd(x, g, dn, dres, *, name):
    t, d = x.shape
    tr = _pick(t, (512, 256))
    want_dx = dres is not None

    def body(*refs):
        if want_dx:
            x_ref, g_ref, dn_ref, dres_ref, dx_ref, dxb_ref, dg_ref = refs
        else:
            x_ref, g_ref, dn_ref, dg_ref = refs

        @pl.when(pl.program_id(0) == 0)
        def _():
            dg_ref[...] = jnp.zeros_like(dg_ref)

        xv = x_ref[...]
        r = lax.rsqrt(jnp.mean(xv * xv, axis=-1, keepdims=True) + EPS)
        xn = xv * r
        dnv = dn_ref[...].astype(F32)
        dg_ref[...] += jnp.sum(dnv * xn, axis=0, keepdims=True)
        if want_dx:
            dyg = dnv * g_ref[...]
            cm = jnp.mean(dyg * xn, axis=-1, keepdims=True)
            dx = dres_ref[...] + r * (dyg - xn * cm)
            dx_ref[...] = dx
            dxb_ref[...] = dx.astype(BF)

    row = pl.BlockSpec((tr, d), lambda i: (i, 0))
    vec = pl.BlockSpec((1, d), lambda i: (0, 0))
    if want_dx:
        return pl.pallas_call(
            body, name=name, grid=(t // tr,), in_specs=[row, vec, row, row], out_specs=[row, row, vec],
            out_shape=[jax.ShapeDtypeStruct((t, d), F32), jax.ShapeDtypeStruct((t, d), BF), jax.ShapeDtypeStruct((1, d), F32)],
            compiler_params=_params("arbitrary"),
        )(x, g, dn, dres)
    return None, None, pl.pallas_call(
        body, name=name, grid=(t // tr,), in_specs=[row, vec, row], out_specs=vec,
        out_shape=jax.ShapeDtypeStruct((1, d), F32), compiler_params=_params("arbitrary"),
    )(x, g, dn)


def final_loss(x, g, target, *, name):
    t, d = x.shape
    tr = _pick(t, (512, 256))

    def body(x_ref, g_ref, t_ref, dx_ref, dxb_ref, dg_ref, lv_ref):
        @pl.when(pl.program_id(0) == 0)
        def _():
            dg_ref[...] = jnp.zeros_like(dg_ref)
            lv_ref[...] = jnp.zeros_like(lv_ref)

        xv = x_ref[...]
        r = lax.rsqrt(jnp.mean(xv * xv, axis=-1, keepdims=True) + EPS)
        xn = xv * r
        err = xn * g_ref[...] - t_ref[...]
        lv_ref[...] += jnp.sum(err * err, axis=0, keepdims=True)
        dy = err * (1.0 / d)
        dg_ref[...] += jnp.sum(dy * xn, axis=0, keepdims=True)
        dyg = dy * g_ref[...]
        cm = jnp.mean(dyg * xn, axis=-1, keepdims=True)
        dx = r * (dyg - xn * cm)
        dx_ref[...] = dx
        dxb_ref[...] = dx.astype(BF)

    row = pl.BlockSpec((tr, d), lambda i: (i, 0))
    vec = pl.BlockSpec((1, d), lambda i: (0, 0))
    return pl.pallas_call(
        body, name=name, grid=(t // tr,), in_specs=[row, vec, row], out_specs=[row, row, vec, vec],
        out_shape=[jax.ShapeDtypeStruct((t, d), F32), jax.ShapeDtypeStruct((t, d), BF), jax.ShapeDtypeStruct((1, d), F32),
                   jax.ShapeDtypeStruct((1, d), F32)],
        compiler_params=_params("arbitrary"),
    )(x, g, target)


CONV_ROWS = 256
HALO = 16


def _conv_taps(ext, w, rows):
    s0 = ext[HALO:HALO + rows]
    s1 = pltpu.roll(ext, 1, 0)[HALO:HALO + rows]
    s2 = pltpu.roll(ext, 2, 0)[HALO:HALO + rows]
    return (w[0:1] * s2 + w[1:2] * s1) + w[2:3] * s0, s0, s1, s2


def conv_silu_fwd(u, w, bl, *, name):
    t, f2 = u.shape
    f = f2 // 2
    s = t // bl
    tc = _pick(f, (256, 128))
    nf = f // tc
    nch = s // CONV_ROWS

    def body(ua_ref, ug_ref, wa_ref, wg_ref, h_ref):
        wa, wg = wa_ref[...], wg_ref[...]

        def chunk(ci, carry):
            r0 = pl.multiple_of(ci * CONV_ROWS, CONV_ROWS)
            ps = pl.multiple_of(jnp.maximum(r0 - HALO, 0), HALO)
            keep = (ci > 0).astype(F32)

            def conv(ref, wv):
                ext = jnp.concatenate([ref[0, pl.ds(ps, HALO), :].astype(F32) * keep,
                                       ref[0, pl.ds(r0, CONV_ROWS), :].astype(F32)], axis=0)
                return _conv_taps(ext, wv, CONV_ROWS)[0]

            ca, cg = conv(ua_ref, wa), conv(ug_ref, wg)
            sg = pl.reciprocal(1.0 + jnp.exp(-cg), approx=True)
            h_ref[0, pl.ds(r0, CONV_ROWS), :] = ((cg * sg) * ca).astype(h_ref.dtype)
            return carry

        lax.fori_loop(0, nch, chunk, 0)

    u3 = u.reshape(bl, s, f2)
    blk = lambda off: pl.BlockSpec((1, s, tc), lambda j, b: (b, 0, j + off))
    wblk = lambda off: pl.BlockSpec((3, tc), lambda j, b: (0, j + off))
    h = pl.pallas_call(
        body, name=name, grid=(nf, bl), in_specs=[blk(0), blk(nf), wblk(0), wblk(nf)],
        out_specs=pl.BlockSpec((1, s, tc), lambda j, b: (b, 0, j)), out_shape=jax.ShapeDtypeStruct((bl, s, f), BF),
        compiler_params=_params("parallel", "parallel"),
    )(u3, u3, w, w)
    return h.reshape(t, f)


def conv_silu_bwd(u, w, dh, bl, *, name):
    t, f2 = u.shape
    f = f2 // 2
    s = t // bl
    tc = _pick(f, (256, 128))
    nf = f // tc
    nch = s // CONV_ROWS
    ext_rows = CONV_ROWS + HALO

    def body(ua_ref, ug_ref, wa_ref, wg_ref, dh_ref, dua_ref, dug_ref, dwa_ref, dwg_ref):
        wa, wg = wa_ref[...], wg_ref[...]

        @pl.when(pl.program_id(1) == 0)
        def _():
            dwa_ref[...] = jnp.zeros_like(dwa_ref)
            dwg_ref[...] = jnp.zeros_like(dwg_ref)

        def chunk(ci, carry):
            r0 = pl.multiple_of(ci * CONV_ROWS, CONV_ROWS)
            ps = pl.multiple_of(jnp.maximum(r0 - HALO, 0), HALO)
            ns = pl.multiple_of(jnp.minimum(r0 + CONV_ROWS, s - HALO), HALO)
            keep_p = (ci > 0).astype(F32)
            keep_n = (ci < nch - 1).astype(F32)

            def ext_of(ref):
                return jnp.concatenate([ref[0, pl.ds(ps, HALO), :].astype(F32) * keep_p,
                                        ref[0, pl.ds(r0, CONV_ROWS), :].astype(F32),
                                        ref[0, pl.ds(ns, HALO), :].astype(F32) * keep_n], axis=0)

            ca, a0, a1, a2 = _conv_taps(ext_of(ua_ref), wa, ext_rows)
            cg, g0, g1, g2 = _conv_taps(ext_of(ug_ref), wg, ext_rows)
            dhe = jnp.concatenate([dh_ref[0, pl.ds(r0, CONV_ROWS), :].astype(F32),
                                   dh_ref[0, pl.ds(ns, HALO), :].astype(F32) * keep_n], axis=0)
            sg = pl.reciprocal(1.0 + jnp.exp(-cg), approx=True)
            dca = dhe * (cg * sg)
            dcg = dhe * ca * (sg * (1.0 + cg * (1.0 - sg)))

            def back(dc, wv):
                n1 = pltpu.roll(dc, ext_rows - 1, 0)
                n2 = pltpu.roll(dc, ext_rows - 2, 0)
                return ((wv[2:3] * dc + wv[1:2] * n1) + wv[0:1] * n2)[:CONV_ROWS]

            dua_ref[0, pl.ds(r0, CONV_ROWS), :] = back(dca, wa).astype(dua_ref.dtype)
            dug_ref[0, pl.ds(r0, CONV_ROWS), :] = back(dcg, wg).astype(dug_ref.dtype)

            def wsum(dc, taps):
                d0 = dc[:CONV_ROWS]
                return [jnp.sum(d0 * tp[:CONV_ROWS], axis=0, keepdims=True) for tp in taps]

            sa = wsum(dca, (a2, a1, a0))
            sgs = wsum(dcg, (g2, g1, g0))
            for j in range(3):
                dwa_ref[j:j + 1, :] += sa[j]
                dwg_ref[j:j + 1, :] += sgs[j]
            return carry

        lax.fori_loop(0, nch, chunk, 0)

    u3 = u.reshape(bl, s, f2)
    dh3 = dh.reshape(bl, s, f)
    blk = lambda off: pl.BlockSpec((1, s, tc), lambda j, b: (b, 0, j + off))
    wblk = lambda off: pl.BlockSpec((3, tc), lambda j, b: (0, j + off))
    dua, dug, dwa, dwg = pl.pallas_call(
        body, name=name, grid=(nf, bl), in_specs=[blk(0), blk(nf), wblk(0), wblk(nf), blk(0)],
        out_specs=[blk(0), blk(0), wblk(0), wblk(0)],
        out_shape=[jax.ShapeDtypeStruct((bl, s, f), BF), jax.ShapeDtypeStruct((bl, s, f), BF),
                   jax.ShapeDtypeStruct((3, f), F32), jax.ShapeDtypeStruct((3, f), F32)],
        compiler_params=_params("parallel", "arbitrary"),
    )(u3, u3, w, w, dh3)
    return dua.reshape(t, f), dug.reshape(t, f), jnp.concatenate([dwa, dwg], axis=1)


def _lane_masks(rows):
    lane = lax.broadcasted_iota(jnp.int32, (rows, LANES), 1)
    return lane < HEAD_DIM, lane >= HEAD_DIM


def _nt(a, b):
    return lax.dot_general(a, b, (((1,), (1,)), ((), ())), preferred_element_type=F32)


def _tn(a, b):
    return lax.dot_general(a, b, (((0,), (0,)), ((), ())), preferred_element_type=F32)


def _nn(a, b):
    return jnp.dot(a, b, preferred_element_type=F32)


def _suffix_ones():
    j = lax.broadcasted_iota(jnp.int32, (2 * QB, QB), 0) % QB
    s = lax.broadcasted_iota(jnp.int32, (2 * QB, QB), 1)
    return (j >= s).astype(BF)


def _softplus(z):
    return jnp.maximum(z, 0.0) + jnp.log(1.0 + jnp.exp(-jnp.abs(z)))


SB_BLOCK = 256
SB_DEAD = -100.0


SB_STRIP = 32
SB_SUB = SB_BLOCK // QB


def _sb_scratch(backward):
    blk = (2, SB_BLOCK, SB_BLOCK)
    per_head = pltpu.VMEM((2, SB_BLOCK, LANES), F32)
    scr = [pltpu.VMEM((2 * QB, QB), BF),
           pltpu.VMEM(blk, F32),
           pltpu.VMEM((2, SB_SUB) + blk[1:], BF),
           pltpu.VMEM(blk, F32),
           pltpu.VMEM(blk, BF),
           per_head]
    if not backward:
        return scr + [per_head]
    return scr + [pltpu.VMEM(blk, F32),
                  pltpu.VMEM(blk, F32),
                  pltpu.VMEM(blk, F32),
                  pltpu.VMEM(blk, BF),
                  per_head,
                  per_head,
                  per_head]


def _sb_strips(fn):
    def step(r, c):
        fn(pl.ds(pl.multiple_of(r * SB_STRIP, SB_STRIP), SB_STRIP), r)
        return c
    lax.fori_loop(0, SB_BLOCK // SB_STRIP, step, 0, unroll=True)


def _sb_causal(thr, r):
    rel = (lax.broadcasted_iota(jnp.int32, (SB_STRIP, SB_BLOCK), 1)
           - lax.broadcasted_iota(jnp.int32, (SB_STRIP, SB_BLOCK), 0))
    return rel < thr + r * SB_STRIP


def _sb_split_store(sp_scr, rows, v):
    hi = v.astype(BF)
    lo = (v - hi.astype(F32)).astype(BF)
    for u in range(SB_SUB):
        sp_scr[u, rows, 0:QB] = hi[:, u * QB:(u + 1) * QB]
        sp_scr[u, rows, QB:2 * QB] = lo[:, u * QB:(u + 1) * QB]


def _sb_suffix_sums(uu_scr, sp_scr, out_scr):
    for u in range(SB_SUB):
        out_scr[:, u * QB:(u + 1) * QB] = _nn(sp_scr[u], uu_scr[...])


def _sb_fold(sums, off):
    offs = [None] * SB_SUB
    for u in reversed(range(SB_SUB)):
        offs[u] = off
        off = off + jnp.broadcast_to(sums[:, u * QB:u * QB + 1], (SB_STRIP, LANES))
    return offs, off


def _sb_first(thr, z_scr, sp_scr, ls_scr=None):
    def strip(rows, r):
        ls = jnp.where(_sb_causal(thr, r), -_softplus(z_scr[rows, :]), 0.0)
        if ls_scr is not None:
            ls_scr[rows, :] = ls
        _sb_split_store(sp_scr, rows, ls)

    _sb_strips(strip)


def _sb_second(thr, z_scr, cin_scr, w_scr, carry_scr):
    def strip(rows, r):
        z, cin = z_scr[rows, :], cin_scr[rows, :]
        offs, carry_scr[rows, :] = _sb_fold(cin, carry_scr[rows, :])
        e = jnp.concatenate([z[:, u * QB:(u + 1) * QB] + cin[:, u * QB:(u + 1) * QB] + offs[u] for u in range(SB_SUB)],
                            axis=1)
        w_scr[rows, :] = jnp.where(_sb_causal(thr, r), jnp.exp(e), 0.0).astype(BF)

    _sb_strips(strip)


def _sb_third(w_scr, dw_scr, da_scr, sp_scr):
    def strip(rows, r):
        da = w_scr[rows, :].astype(F32) * dw_scr[rows, :]
        da_scr[rows, :] = da
        _sb_split_store(sp_scr, rows, da)

    _sb_strips(strip)


def _sb_fourth(thr, z_scr, ls_scr, da_scr, sin_scr, dz_scr, carry_s_scr, dsum_scr):
    def strip(rows, r):
        da, sin = da_scr[rows, :], sin_scr[rows, :]
        offs, carry_s_scr[rows, :] = _sb_fold(sin, carry_s_scr[rows, :])
        dsum = dsum_scr[rows, :]
        pre = jnp.concatenate([dsum - (sin[:, u * QB:(u + 1) * QB] - da[:, u * QB:(u + 1) * QB] + offs[u])
                               for u in range(SB_SUB)], axis=1)
        sig = jnp.exp(z_scr[rows, :] + ls_scr[rows, :])
        dz_scr[rows, :] = jnp.where(_sb_causal(thr, r), da - sig * pre, 0.0).astype(BF)

    _sb_strips(strip)


def _sb_alive(carry_scr):
    return (jnp.max(carry_scr[...]) > SB_DEAD).astype(jnp.int32)


def _ride_hooks(ride, refs, n_in, n_out, n_scratch, grid):
    if ride is None:
        return refs[:n_in], refs[n_in:n_in + n_out], refs[n_in + n_out:], lambda: None, lambda: None
    ni, no = len(ride.arrays), len(ride.out_shape)
    own_in, rin = refs[:n_in], refs[n_in:n_in + ni]
    own_out, rout = refs[n_in + ni:n_in + ni + n_out], refs[n_in + ni + n_out:n_in + ni + n_out + no]
    rest = refs[n_in + ni + n_out + no:]
    own_scr, sems = rest[:n_scratch], rest[n_scratch:]
    ids = [pl.program_id(a) for a in range(len(grid))]

    def start():
        first = functools.reduce(lambda u, v: u & v, [i == 0 for i in ids])
        pl.when(first)(lambda: ride.start(rin, rout, sems))

    def wait():
        last = functools.reduce(lambda u, v: u & v, [i == n - 1 for i, n in zip(ids, grid)])
        pl.when(last)(lambda: ride.wait(rin, rout, sems))

    return own_in, own_out, own_scr, start, wait


def sb_fwd(proj, bl, *, name, ride=None):
    t, width = proj.shape
    s = t // bl
    npair = SB_WIDTH // LANES
    nq = s // SB_BLOCK
    grid = (bl, npair, nq)

    own_scratch = _sb_scratch(backward=False)

    def body(*refs):
        (q_ref, k_ref, v_ref), (o_ref, of_ref), scr, ride_start, ride_wait = _ride_hooks(
            ride, refs, 3, 2, len(own_scratch), grid)
        uu_scr, z_scr, sp_scr, cin_scr, w_scr, carry_scr, acc_scr = scr
        ride_start()
        i = pl.program_id(2)
        uu_scr[...] = _suffix_ones()
        carry_scr[...] = jnp.zeros_like(carry_scr)
        acc_scr[...] = jnp.zeros_like(acc_scr)
        qs = (q_ref[0].astype(F32) * SCALE).astype(BF)
        masks = _lane_masks(SB_BLOCK)

        z, sp, cin, w, car, acc = ([r.at[h] for h in range(2)] for r in (z_scr, sp_scr, cin_scr, w_scr, carry_scr, acc_scr))

        def live(c):
            start = pl.multiple_of(c * SB_BLOCK, SB_BLOCK)
            thr = (i - c) * SB_BLOCK
            kb, vb = k_ref[0, pl.ds(start, SB_BLOCK), :], v_ref[0, pl.ds(start, SB_BLOCK), :]
            for h in range(2):
                z[h][...] = _nt(jnp.where(masks[h], qs, jnp.zeros_like(qs)), kb)
            _sb_first(thr, z[0], sp[0])
            _sb_suffix_sums(uu_scr, sp[0], cin[0])
            _sb_first(thr, z[1], sp[1])
            _sb_suffix_sums(uu_scr, sp[1], cin[1])
            _sb_second(thr, z[0], cin[0], w[0], car[0])
            acc[0][...] += _nn(w[0][...], vb)
            _sb_second(thr, z[1], cin[1], w[1], car[1])
            acc[1][...] += _nn(w[1][...], vb)
            return _sb_alive(carry_scr)

        def step(n, alive):
            return lax.cond(alive > 0, lambda: live(i - n), lambda: alive)

        lax.fori_loop(0, i + 1, step, jnp.int32(1))
        both = jnp.where(masks[0], acc_scr[0], acc_scr[1])
        o_ref[0] = both.astype(o_ref.dtype)
        of_ref[0] = both
        ride_wait()

    p3 = proj.reshape(bl, s, width)
    qblk = pl.BlockSpec((1, SB_BLOCK, LANES), lambda b, p, i: (b, i, p))
    extra = ride.arrays if ride else []
    o, of, *delivered = pl.pallas_call(
        body, name=name, grid=grid,
        in_specs=[qblk, pl.BlockSpec((1, s, LANES), lambda b, p, i: (b, 0, npair + p)),
                  pl.BlockSpec((1, s, LANES), lambda b, p, i: (b, 0, 2 * npair + p))] + [ANY] * len(extra),
        out_specs=[qblk, qblk] + [ANY] * (len(ride.out_shape) if ride else 0),
        out_shape=[jax.ShapeDtypeStruct((bl, s, SB_WIDTH), BF), jax.ShapeDtypeStruct((bl, s, SB_WIDTH), F32)]
        + (ride.out_shape if ride else []),
        scratch_shapes=own_scratch + (ride.scratch if ride else []),
        compiler_params=_params("arbitrary", "arbitrary", "arbitrary"),
    )(p3, p3, p3, *extra)
    return o.reshape(t, SB_WIDTH), of.reshape(t, SB_WIDTH), delivered


def sb_bwd(proj, o, dcat, bl, *, name, ride=None):
    t, width = proj.shape
    s = t // bl
    npair = SB_WIDTH // LANES
    nq = s // SB_BLOCK
    grid = (bl, npair, nq)
    own_scratch = _sb_scratch(backward=True)

    def body(*refs):
        (q_ref, k_ref, v_ref, o_ref, do_ref), (dq_ref, dk_ref, dv_ref), scr, ride_start, ride_wait = _ride_hooks(
            ride, refs, 5, 3, len(own_scratch), grid)
        (uu_scr, z_scr, sp_scr, cin_scr, w_scr, carry_scr,
         ls_scr, dw_scr, da_scr, dz_scr, carry_s_scr, dsum_scr, dq_scr) = scr
        ride_start()
        i = pl.program_id(2)

        @pl.when(i == 0)
        def _():
            dk_ref[...] = jnp.zeros_like(dk_ref)
            dv_ref[...] = jnp.zeros_like(dv_ref)

        uu_scr[...] = _suffix_ones()
        for ref in (carry_scr, carry_s_scr, dq_scr):
            ref[...] = jnp.zeros_like(ref)
        masks = _lane_masks(SB_BLOCK)
        qs = (q_ref[0].astype(F32) * SCALE).astype(BF)
        do = do_ref[0]
        prod = do.astype(F32) * o_ref[0]
        for h in range(2):
            dsum_scr[h] = jnp.broadcast_to(jnp.sum(jnp.where(masks[h], prod, 0.0), axis=1, keepdims=True),
                                           (SB_BLOCK, LANES))

        z, sp, cin, w, car, ls, dw, da, dz, cars, dsum, dq = (
            [r.at[h] for h in range(2)] for r in (z_scr, sp_scr, cin_scr, w_scr, carry_scr, ls_scr, dw_scr, da_scr,
                                                  dz_scr, carry_s_scr, dsum_scr, dq_scr))

        def live(c):
            start = pl.multiple_of(c * SB_BLOCK, SB_BLOCK)
            thr = (i - c) * SB_BLOCK
            keys = pl.ds(start, SB_BLOCK)
            kb, vb = k_ref[0, keys, :], v_ref[0, keys, :]
            qh = [jnp.where(masks[h], qs, jnp.zeros_like(qs)) for h in range(2)]
            doh = [jnp.where(masks[h], do, jnp.zeros_like(do)) for h in range(2)]
            for h in range(2):
                z[h][...] = _nt(qh[h], kb)
                dw[h][...] = _nt(doh[h], vb)

            def weights(h):
                _sb_second(thr, z[h], cin[h], w[h], car[h])
                _sb_third(w[h], dw[h], da[h], sp[h])
                _sb_suffix_sums(uu_scr, sp[h], cin[h])

            def grads(h):
                _sb_fourth(thr, z[h], ls[h], da[h], cin[h], dz[h], cars[h], dsum[h])
                dk_ref[0, keys, :] += _tn(dz[h][...], qh[h])
                dv_ref[0, keys, :] += _tn(w[h][...], doh[h])
                dq[h][...] += _nn(dz[h][...], kb)

            _sb_first(thr, z[0], sp[0], ls[0])
            _sb_suffix_sums(uu_scr, sp[0], cin[0])
            _sb_first(thr, z[1], sp[1], ls[1])
            _sb_suffix_sums(uu_scr, sp[1], cin[1])
            weights(0)
            weights(1)
            grads(0)
            grads(1)
            return _sb_alive(carry_scr)

        def step(n, alive):
            return lax.cond(alive > 0, lambda: live(i - n), lambda: alive)

        lax.fori_loop(0, i + 1, step, jnp.int32(1))
        dq_ref[0] = (jnp.where(masks[0], dq_scr[0], dq_scr[1]) * SCALE).astype(dq_ref.dtype)
        ride_wait()

    p3 = proj.reshape(bl, s, width)
    o3 = o.reshape(bl, s, SB_WIDTH)
    d3 = dcat.reshape(bl, s, dcat.shape[1])
    qblk = pl.BlockSpec((1, SB_BLOCK, LANES), lambda b, p, i: (b, i, p))
    full = pl.BlockSpec((1, s, LANES), lambda b, p, i: (b, 0, p))
    extra = ride.arrays if ride else []
    dq, dk, dv, *delivered = pl.pallas_call(
        body, name=name, grid=grid,
        in_specs=[qblk, pl.BlockSpec((1, s, LANES), lambda b, p, i: (b, 0, npair + p)),
                  pl.BlockSpec((1, s, LANES), lambda b, p, i: (b, 0, 2 * npair + p)), qblk, qblk] + [ANY] * len(extra),
        out_specs=[qblk, full, full] + [ANY] * (len(ride.out_shape) if ride else 0),
        out_shape=[jax.ShapeDtypeStruct((bl, s, SB_WIDTH), BF), jax.ShapeDtypeStruct((bl, s, SB_WIDTH), F32),
                   jax.ShapeDtypeStruct((bl, s, SB_WIDTH), F32)] + (ride.out_shape if ride else []),
        scratch_shapes=own_scratch + (ride.scratch if ride else []),
        compiler_params=_params("arbitrary", "arbitrary", "arbitrary"),
    )(p3, p3, p3, o3, d3, *extra)
    return dq.reshape(t, SB_WIDTH), dk.reshape(t, SB_WIDTH), dv.reshape(t, SB_WIDTH), delivered


MEM_TQ = 512


def mem_fwd(proj, kvm, bl, *, name):
    t, width = proj.shape
    s = t // bl
    qoff = (width - MEM_WIDTH) // LANES
    npair = MEM_WIDTH // LANES

    def body(q_ref, k_ref, v_ref, o_ref):
        ma, mb = _lane_masks(MEM_TQ)
        q = q_ref[...].astype(BF)
        outs = []
        for m in (ma, mb):
            qh = jnp.where(m, q, jnp.zeros_like(q))
            sc = _nt(qh, k_ref[...]) * SCALE
            p = jnp.exp(sc - jnp.max(sc, axis=-1, keepdims=True))
            p = p * (1.0 / jnp.sum(p, axis=-1, keepdims=True))
            outs.append(_nn(p.astype(BF), v_ref[...]))
        o_ref[...] = jnp.where(ma, outs[0], outs[1]).astype(o_ref.dtype)

    nt = s // MEM_TQ
    return pl.pallas_call(
        body, name=name, grid=(bl, npair, nt),
        in_specs=[pl.BlockSpec((MEM_TQ, LANES), lambda b, p, i: (b * nt + i, qoff + p)),
                  pl.BlockSpec((MEM_LEN, LANES), lambda b, p, i: (b, p)),
                  pl.BlockSpec((MEM_LEN, LANES), lambda b, p, i: (b, npair + p))],
        out_specs=pl.BlockSpec((MEM_TQ, LANES), lambda b, p, i: (b * nt + i, p)),
        out_shape=jax.ShapeDtypeStruct((t, MEM_WIDTH), BF),
        compiler_params=_params("parallel", "parallel", "parallel"),
    )(proj, kvm, kvm)


def mem_bwd(proj, kvm, dcat, bl, *, name):
    t, width = proj.shape
    s = t // bl
    qoff = (width - MEM_WIDTH) // LANES
    doff = (dcat.shape[1] - MEM_WIDTH) // LANES
    npair = MEM_WIDTH // LANES
    nt = s // MEM_TQ

    def body(q_ref, k_ref, v_ref, do_ref, dq_ref, dk_ref, dv_ref):
        @pl.when(pl.program_id(2) == 0)
        def _():
            dk_ref[...] = jnp.zeros_like(dk_ref)
            dv_ref[...] = jnp.zeros_like(dv_ref)

        ma, mb = _lane_masks(MEM_TQ)
        q = q_ref[...].astype(BF)
        do = do_ref[...]
        kb, vb = k_ref[...], v_ref[...]
        dqs = []
        for m in (ma, mb):
            qh = jnp.where(m, q, jnp.zeros_like(q))
            doh = jnp.where(m, do, jnp.zeros_like(do))
            sc = _nt(qh, kb) * SCALE
            p = jnp.exp(sc - jnp.max(sc, axis=-1, keepdims=True))
            p = p * (1.0 / jnp.sum(p, axis=-1, keepdims=True))
            dp = _nt(doh, vb)
            ds = p * (dp - jnp.sum(p * dp, axis=-1, keepdims=True))
            dss = (ds * SCALE).astype(BF)
            dk_ref[...] += _tn(dss, qh)
            dv_ref[...] += _tn(p.astype(BF), doh)
            dqs.append(_nn(dss, kb))
        dq_ref[...] = jnp.where(ma, dqs[0], dqs[1]).astype(dq_ref.dtype)

    kblk = pl.BlockSpec((MEM_LEN, LANES), lambda b, p, i: (b, p))
    dq, dk, dv = pl.pallas_call(
        body, name=name, grid=(bl, npair, nt),
        in_specs=[pl.BlockSpec((MEM_TQ, LANES), lambda b, p, i: (b * nt + i, qoff + p)), kblk,
                  pl.BlockSpec((MEM_LEN, LANES), lambda b, p, i: (b, npair + p)),
                  pl.BlockSpec((MEM_TQ, LANES), lambda b, p, i: (b * nt + i, doff + p))],
        out_specs=[pl.BlockSpec((MEM_TQ, LANES), lambda b, p, i: (b * nt + i, p)), kblk, kblk],
        out_shape=[jax.ShapeDtypeStruct((t, MEM_WIDTH), BF), jax.ShapeDtypeStruct((bl * MEM_LEN, MEM_WIDTH), F32),
                   jax.ShapeDtypeStruct((bl * MEM_LEN, MEM_WIDTH), F32)],
        compiler_params=_params("parallel", "parallel", "arbitrary"),
    )(proj, kvm, kvm, dcat)
    return dq, jnp.concatenate([dk, dv], axis=1).astype(BF)


def _dil_rows(r, u0, size, dil):
    if dil == 1:
        return pl.ds(_mo(u0, QB), size)
    return pl.ds(u0 * dil + r, size, stride=dil)


def _dil_walk(block, dil, nb):
    def residue(r, c):
        block(r, 0, QB)
        if nb > 1:
            def step(n, c2):
                block(r, n, 2 * QB)
                return c2
            lax.fori_loop(1, nb, step, 0, unroll=3)
        return c

    if dil == 1:
        residue(0, 0)
    else:
        lax.fori_loop(0, dil, residue, 0, unroll=4 if nb == 1 else 1)


def _dil_scores(qh, kb, slope_d, n_keys):
    i = lax.broadcasted_iota(jnp.int32, (QB, n_keys), 0)
    j = lax.broadcasted_iota(jnp.int32, (QB, n_keys), 1)
    delta = i + (n_keys - QB) - j
    valid = (delta >= 0) & (delta <= QB)
    sc = _nt(qh, kb) * SCALE - slope_d * delta.astype(F32)
    return jnp.where(valid, sc, NEG)


def _dil_slopes(g, dil):
    p = pl.program_id(1)
    sa = jnp.where(p == 0, ALIBI[4 * g] * dil, ALIBI[4 * g + 2] * dil).astype(F32)
    sb = jnp.where(p == 0, ALIBI[4 * g + 1] * dil, ALIBI[4 * g + 3] * dil).astype(F32)
    return sa, sb


def dil_fwd(projb, kv, g, bl, *, name):
    _, dil = DIL_GROUPS[g]
    t, wq = projb.shape
    wk = kv.shape[1]
    s = t // bl
    ln = s // dil
    nb = ln // QB
    gw = 4 * HEAD_DIM
    ck, co = wk // LANES, gw // LANES

    def body(q_ref, k_ref, v_ref, o_ref, lse_ref):
        sa, sb = _dil_slopes(g, dil)
        ma, mb = _lane_masks(QB)

        def block(r, n, n_keys):
            q0 = n * QB
            k0 = q0 - (n_keys - QB)
            q = q_ref[0, _dil_rows(r, q0, QB, dil), :].astype(BF)
            kb = k_ref[0, _dil_rows(r, k0, n_keys, dil), :].astype(BF)
            vb = v_ref[0, _dil_rows(r, k0, n_keys, dil), :].astype(BF)
            outs, lses = [], []
            for m, sl in ((ma, sa), (mb, sb)):
                qh = jnp.where(m, q, jnp.zeros_like(q))
                sc = _dil_scores(qh, kb, sl, n_keys)
                mx = jnp.max(sc, axis=-1, keepdims=True)
                p = jnp.exp(sc - mx)
                den = jnp.sum(p, axis=-1, keepdims=True)
                outs.append(_nn(p.astype(BF), vb) * (1.0 / den))
                lses.append(mx + jnp.log(den))
            o_ref[0, _dil_rows(r, q0, QB, dil), :] = jnp.where(ma, outs[0], outs[1])
            lse_ref[0, _dil_rows(r, q0, QB, dil), :] = jnp.where(ma, lses[0], lses[1])

        _dil_walk(block, dil, nb)

    colblk = lambda off: pl.BlockSpec((1, s, LANES), lambda b, p: (b, 0, off + p))
    o, lse = pl.pallas_call(
        body, name=name, grid=(bl, co),
        in_specs=[colblk(g * co), colblk(g * co), colblk(ck // 2 + g * co)],
        out_specs=[colblk(0), colblk(0)],
        out_shape=[jax.ShapeDtypeStruct((bl, s, gw), F32), jax.ShapeDtypeStruct((bl, s, gw), F32)],
        compiler_params=_params("parallel", "parallel"),
    )(projb.reshape(bl, s, wq), kv.reshape(bl, s, wk), kv.reshape(bl, s, wk))
    return o.reshape(t, gw), lse.reshape(t, gw)


def dil_bwd(projb, kv, lse, dog, dshift, g, bl, *, name):
    _, dil = DIL_GROUPS[g]
    t, wq = projb.shape
    wk = kv.shape[1]
    s = t // bl
    ln = s // dil
    nb = ln // QB
    gw = 4 * HEAD_DIM
    ck, co = wk // LANES, gw // LANES

    def body(q_ref, k_ref, v_ref, lse_ref, do_ref, sh_ref, dq_ref, dk_ref, dv_ref):
        sa, sb = _dil_slopes(g, dil)
        ma, mb = _lane_masks(QB)
        dk_ref[...] = jnp.zeros_like(dk_ref)
        dv_ref[...] = jnp.zeros_like(dv_ref)

        def block(r, n, n_keys):
            q0 = n * QB
            k0 = q0 - (n_keys - QB)
            qrows, krows = _dil_rows(r, q0, QB, dil), _dil_rows(r, k0, n_keys, dil)
            q = q_ref[0, qrows, :].astype(BF)
            do = do_ref[0, qrows, :].astype(BF)
            lse_b = lse_ref[0, qrows, :]
            sh_b = sh_ref[0, qrows, :]
            kb = k_ref[0, krows, :].astype(BF)
            vb = v_ref[0, krows, :].astype(BF)
            dqs = []
            for m, sl, c0 in ((ma, sa, 0), (mb, sb, HEAD_DIM)):
                qh = jnp.where(m, q, jnp.zeros_like(q))
                doh = jnp.where(m, do, jnp.zeros_like(do))
                sc = _dil_scores(qh, kb, sl, n_keys)
                p = jnp.exp(sc - lse_b[:, c0:c0 + 1])
                ds = p * (_nt(doh, vb) - sh_b[:, c0:c0 + 1])
                dss = (ds * SCALE).astype(BF)
                dk_ref[0, krows, :] += _tn(dss, qh)
                dv_ref[0, krows, :] += _tn(p.astype(BF), doh)
                dqs.append(_nn(dss, kb))
            dq_ref[0, qrows, :] = jnp.where(ma, dqs[0], dqs[1])

        _dil_walk(block, dil, nb)

    colblk = lambda off: pl.BlockSpec((1, s, LANES), lambda b, p: (b, 0, off + p))
    dq, dk, dv = pl.pallas_call(
        body, name=name, grid=(bl, co),
        in_specs=[colblk(g * co), colblk(g * co), colblk(ck // 2 + g * co), colblk(0), colblk(g * co), colblk(g * co)],
        out_specs=[colblk(0)] * 3,
        out_shape=[jax.ShapeDtypeStruct((bl, s, gw), F32)] * 3,
        compiler_params=_params("parallel", "parallel"),
    )(projb.reshape(bl, s, wq), kv.reshape(bl, s, wk), kv.reshape(bl, s, wk), lse.reshape(bl, s, gw),
      dog.reshape(bl, s, DIL_WIDTH), dshift.reshape(bl, s, DIL_WIDTH))
    return dq.reshape(t, gw), dk.reshape(t, gw), dv.reshape(t, gw)


def _group_weights(lses):
    mx = jnp.maximum(jnp.maximum(lses[0], lses[1]), lses[2])
    es = [jnp.exp(l - mx) for l in lses]
    inv = 1.0 / (es[0] + es[1] + es[2])
    return [e * inv for e in es]


def dil_combine_fwd(os_, lses, *, name):
    t, gw = os_[0].shape
    tr = _pick(t, (512, 256))

    def body(o0, o1, o2, l0, l1, l2, out_ref):
        al = _group_weights([l0[...], l1[...], l2[...]])
        for g, o_ref in enumerate((o0, o1, o2)):
            out_ref[:, g * gw:(g + 1) * gw] = (o_ref[...] * al[g]).astype(out_ref.dtype)

    blk = pl.BlockSpec((tr, gw), lambda i: (i, 0))
    return pl.pallas_call(
        body, name=name, grid=(t // tr,), in_specs=[blk] * 6,
        out_specs=pl.BlockSpec((tr, 3 * gw), lambda i: (i, 0)), out_shape=jax.ShapeDtypeStruct((t, 3 * gw), BF),
        compiler_params=_params("parallel"),
    )(*os_, *lses)


def dil_combine_bwd(os_, lses, dcat, *, name):
    t, gw = os_[0].shape
    tr = _pick(t, (512, 256))

    def head_sum(v):
        parts = []
        for c in range(gw // LANES):
            blk = v[:, c * LANES:(c + 1) * LANES]
            ma, _ = _lane_masks(tr)
            sa = jnp.sum(jnp.where(ma, blk, 0.0), axis=1, keepdims=True)
            sb = jnp.sum(blk, axis=1, keepdims=True) - sa
            parts.append(jnp.where(ma, sa, sb))
        return jnp.concatenate(parts, axis=1)

    def body(o0, o1, o2, l0, l1, l2, d_ref, dog_ref, sh_ref):
        al = _group_weights([l0[...], l1[...], l2[...]])
        dos = [d_ref[:, g * gw:(g + 1) * gw].astype(F32) for g in range(3)]
        dal = [head_sum(dos[g] * o_ref[...]) for g, o_ref in enumerate((o0, o1, o2))]
        mix = al[0] * dal[0] + al[1] * dal[1] + al[2] * dal[2]
        for g in range(3):
            dog_ref[:, g * gw:(g + 1) * gw] = (al[g] * dos[g]).astype(dog_ref.dtype)
            sh_ref[:, g * gw:(g + 1) * gw] = al[g] * mix

    blk = pl.BlockSpec((tr, gw), lambda i: (i, 0))
    wide = pl.BlockSpec((tr, 3 * gw), lambda i: (i, 0))
    return pl.pallas_call(
        body, name=name, grid=(t // tr,), in_specs=[blk] * 6 + [wide], out_specs=[wide, wide],
        out_shape=[jax.ShapeDtypeStruct((t, 3 * gw), F32), jax.ShapeDtypeStruct((t, 3 * gw), F32)],
        compiler_params=_params("parallel"),
    )(*os_, *lses, dcat)


def adamw(w, g1, g2, m, v, *, name):
    r, c = w.shape
    tr = r
    for cand in (256, 128, 64, 32, 16, 8):
        if r % cand == 0 and cand * c * 4 <= (1 << 20):
            tr = cand
            break
    two = g2 is not None

    def body(*refs):
        if two:
            w_ref, g1_ref, g2_ref, m_ref, v_ref, g_ref, d_ref, nm_ref, nv_ref = refs
            g = g1_ref[...] + g2_ref[...]
        else:
            w_ref, g1_ref, m_ref, v_ref, g_ref, d_ref, nm_ref, nv_ref = refs
            g = g1_ref[...]
        nm = ADAM_B1 * m_ref[...] + (1.0 - ADAM_B1) * g
        nv = ADAM_B2 * v_ref[...] + (1.0 - ADAM_B2) * (g * g)
        m_hat = nm / (1.0 - ADAM_B1 ** ADAM_STEP)
        v_hat = nv / (1.0 - ADAM_B2 ** ADAM_STEP)
        g_ref[...] = g
        d_ref[...] = -ADAM_LR * (m_hat / (jnp.sqrt(v_hat) + ADAM_EPS) + ADAM_WD * w_ref[...])
        nm_ref[...] = nm
        nv_ref[...] = nv

    blk = pl.BlockSpec((tr, c), lambda i: (i, 0))
    args = [w, g1] + ([g2] if two else []) + [m, v]
    return pl.pallas_call(
        body, name=name, grid=(r // tr,), in_specs=[blk] * len(args), out_specs=[blk] * 4,
        out_shape=[jax.ShapeDtypeStruct((r, c), F32)] * 4, compiler_params=_params("parallel"),
    )(*args)


def sum4(own, land, *, name):
    r, c = own.shape
    tr = _pick(r, (256, 128, 64))

    def body(o_ref, l_ref, s_ref):
        s_ref[...] = ((o_ref[...].astype(F32) + l_ref[0].astype(F32)) + l_ref[1].astype(F32)) + l_ref[2].astype(F32)

    return pl.pallas_call(
        body, name=name, grid=(r // tr,),
        in_specs=[pl.BlockSpec((tr, c), lambda i: (i, 0)), pl.BlockSpec((3, tr, c), lambda i: (0, i, 0))],
        out_specs=pl.BlockSpec((tr, c), lambda i: (i, 0)), out_shape=jax.ShapeDtypeStruct((r, c), F32),
        compiler_params=_params("parallel"),
    )(own, land)


ANY = pl.BlockSpec(memory_space=pl.ANY)


def _place():
    x, y, c = lax.axis_index("x"), lax.axis_index("y"), lax.axis_index("c")
    chips = [(1 - x, y), (x, 1 - y), (1 - x, 1 - y)]
    return x, y, c, chips


class Ride:
    def __init__(self, arrays, out_shape, copies):
        self.arrays, self.out_shape, self.copies = list(arrays), list(out_shape), copies
        n = 3 * len(self.arrays)
        self.scratch = [pltpu.SemaphoreType.DMA((n,)), pltpu.SemaphoreType.DMA((n,)),
                        pltpu.SemaphoreType.DMA((len(self.arrays),))]

    def split(self, refs):
        n, m = len(self.arrays), len(self.out_shape)
        return refs[:n], refs[n:n + m], refs[n + m:]

    def start(self, ins, outs, sems):
        local, sends, _ = self.copies(ins, outs, *sems)
        for cp in local + sends:
            cp.start()

    def wait(self, ins, outs, sems):
        local, sends, arrivals = self.copies(ins, outs, *sems)
        for cp in arrivals:
            cp.wait_recv()
        for cp in sends:
            cp.wait_send()
        for cp in local:
            cp.wait()

    def run(self, name):
        def body(*refs):
            ins, outs, sems = self.split(refs)
            self.start(ins, outs, sems)
            self.wait(ins, outs, sems)

        return pl.pallas_call(body, name=name, in_specs=[ANY] * len(self.arrays), out_specs=[ANY] * len(self.out_shape),
                              out_shape=self.out_shape, scratch_shapes=self.scratch)(*self.arrays)


def gather_ride(shards, axes):
    def copies(ins, outs, send_sems, recv_sems, local_sems):
        x, y, c, chips = _place()

        def slot(a, q):
            size = shards[a].shape[axes[a]]
            start = pl.multiple_of(q * size, size)
            return outs[a].at[pl.ds(start, size), :] if axes[a] == 0 else outs[a].at[:, pl.ds(start, size)]

        def remote(a, k, q):
            px, py = chips[k]
            return pltpu.make_async_remote_copy(src_ref=ins[a], dst_ref=slot(a, q), send_sem=send_sems.at[3 * a + k],
                                                recv_sem=recv_sems.at[3 * a + k], device_id=(px, py, c), device_id_type=MESH)

        me = 2 * x + y
        n = len(shards)
        local = [pltpu.make_async_copy(ins[a], slot(a, me), local_sems.at[a]) for a in range(n)]
        sends = [remote(a, k, me) for a in range(n) for k in range(3)]
        arrivals = [remote(a, k, 2 * chips[k][0] + chips[k][1]) for a in range(n) for k in range(3)]
        return local, sends, arrivals

    out_shape = []
    for a, sh in enumerate(shards):
        full = list(sh.shape)
        full[axes[a]] *= N_CHIPS
        out_shape.append(jax.ShapeDtypeStruct(tuple(full), sh.dtype))
    return Ride(shards, out_shape, copies)


def scatter_ride(grads):
    def copies(ins, outs, send_sems, recv_sems, local_sems):
        x, y, c, chips = _place()
        sends = [pltpu.make_async_remote_copy(src_ref=ins[a].at[2 * px + py], dst_ref=outs[a].at[k],
                                              send_sem=send_sems.at[3 * a + k], recv_sem=recv_sems.at[3 * a + k],
                                              device_id=(px, py, c), device_id_type=MESH)
                 for a in range(len(grads)) for k, (px, py) in enumerate(chips)]
        return [], sends, sends

    return Ride(grads, [jax.ShapeDtypeStruct((3,) + g.shape[1:], g.dtype) for g in grads], copies)


def swap_with_sibling(arrs, *, name):
    n = len(arrs)

    def body(*refs):
        ins, outs = refs[:n], refs[n:2 * n]
        send_sems, recv_sems = refs[2 * n:]
        x, y, c, _ = _place()
        sends = []
        for a in range(n):
            cp = pltpu.make_async_remote_copy(src_ref=ins[a], dst_ref=outs[a], send_sem=send_sems.at[a],
                                              recv_sem=recv_sems.at[a], device_id=(x, y, 1 - c), device_id_type=MESH)
            cp.start()
            sends.append(cp)
        for cp in sends:
            cp.wait_recv()
        for cp in sends:
            cp.wait_send()

    return pl.pallas_call(
        body, name=name, in_specs=[ANY] * n, out_specs=[ANY] * n,
        out_shape=[jax.ShapeDtypeStruct(a.shape, a.dtype) for a in arrs],
        scratch_shapes=[pltpu.SemaphoreType.DMA((n,)), pltpu.SemaphoreType.DMA((n,))],
    )(*arrs)


def all_reduce_small(v, *, name):
    rows = v.shape[0]

    def body(v_ref, o_ref, gath, send_sems, recv_sems):
        x, y, c, _ = _place()
        me = 4 * x + 2 * y + c
        gath[me] = v_ref[...]
        sends = []
        for msk in range(1, N_DEV):
            peer = (x ^ (msk >> 2), y ^ ((msk >> 1) & 1), c ^ (msk & 1))
            cp = pltpu.make_async_remote_copy(src_ref=v_ref, dst_ref=gath.at[me], send_sem=send_sems.at[msk - 1],
                                              recv_sem=recv_sems.at[msk - 1], device_id=peer, device_id_type=MESH)
            cp.start()
            sends.append(cp)
        for msk in range(1, N_DEV):
            pltpu.make_async_remote_copy(src_ref=v_ref, dst_ref=gath.at[me ^ msk], send_sem=send_sems.at[msk - 1],
                                         recv_sem=recv_sems.at[msk - 1], device_id=(x, y, c), device_id_type=MESH).wait_recv()
        for cp in sends:
            cp.wait_send()
        tot = gath[0]
        for q in range(1, N_DEV):
            tot = tot + gath[q]
        o_ref[...] = tot

    vm = pl.BlockSpec(memory_space=pltpu.VMEM)
    return pl.pallas_call(
        body, name=name, in_specs=[vm], out_specs=vm, out_shape=jax.ShapeDtypeStruct(v.shape, F32),
        scratch_shapes=[pltpu.VMEM((N_DEV, rows, LANES), F32), pltpu.SemaphoreType.DMA((N_DEV - 1,)),
                        pltpu.SemaphoreType.DMA((N_DEV - 1,))],
    )(v)


def _ffn_fwd(xin, gain, w_up, w_conv, w_down, bl, tag):
    n = rmsnorm_fwd(xin, gain, name=f"{tag}_ffn_norm")
    u = matmul(n, w_up, name=f"{tag}_ffn_up")
    h = conv_silu_fwd(u, w_conv, bl, name=f"{tag}_ffn_conv")
    xout = matmul(h, w_down, out_dtype=F32, add=xin, name=f"{tag}_ffn_down")
    return xout, (n, u, h)


def _ffn_bwd(dxout, dxout_b, xin, saved, gain, w_up, w_conv, w_down, bl, tag, shard):
    n, u, h = saved
    dh = matmul(dxout_b, w_down, tb=True, name=f"{tag}_ffn_down_dx")
    g_down = matmul(h, dxout_b, ta=True, name=f"{tag}_ffn_down_dw")
    dua, dug, g_conv = conv_silu_bwd(u, w_conv, dh, bl, name=f"{tag}_ffn_conv_bwd")
    dn = matmul(dua, w_up, a2=dug, tb=True, out_dtype=F32, name=f"{tag}_ffn_up_dx")
    g_up = matmul(n, dua, b2=dug, ta=True, shard_cols=shard, name=f"{tag}_ffn_up_dw")
    dxin, dxin_b, g_norm = rmsnorm_bwd(xin, gain, dn, dxout, name=f"{tag}_ffn_norm_bwd")
    return dxin, dxin_b, g_norm, g_up, g_conv, g_down


def _mem_kv(mem2, gain, w_kv, tag):
    mn = rmsnorm_fwd(mem2, gain, name=f"{tag}_mem_norm")
    return mn, matmul(mn, w_kv, name=f"{tag}_mem_kv")


def _mem_kv_bwd(mem2, mn, gain, w_kv, dkvm, tag):
    g_kv = matmul(mn, dkvm, ta=True, name=f"{tag}_mem_kv_dw")
    dmn = matmul(dkvm, w_kv, tb=True, out_dtype=F32, name=f"{tag}_mem_kv_dx")
    _, _, g_norm = rmsnorm_bwd(mem2, gain, dmn, None, name=f"{tag}_mem_norm_bwd")
    return g_norm, g_kv


def _shard_major(name, grad):
    if name in COL_SHARDED:
        return grad
    return grad.reshape(N_CHIPS, grad.shape[0] // N_CHIPS, grad.shape[1])


def local_step(x, mem, target, w, shard_of, late_gather=None, early_scatter=()):
    bl, s, d = x.shape
    t = bl * s
    x0 = x.reshape(t, d)
    mem2 = mem.reshape(bl * MEM_LEN, d)
    tgt = target.reshape(t, d)
    g = {}

    n1 = rmsnorm_fwd(x0, w["a_norm_attn"], name="a_attn_norm")
    proj_a = matmul(n1, w["a_w_in"], name="a_w_in")
    mn_a, kvm_a = _mem_kv(mem2, w["a_norm_mem"], w["a_w_mem_kv"], "a")
    o_sb, o_sb_f32, delivered = sb_fwd(proj_a, bl, name="a_sb_fwd", ride=late_gather[0] if late_gather else None)
    if late_gather:
        w = {**w, **dict(zip(late_gather[1], delivered))}
    o_mem_a = mem_fwd(proj_a, kvm_a, bl, name="a_mem_fwd")
    cat_a = jnp.concatenate([o_sb, o_mem_a], axis=1)
    x1 = matmul(cat_a, w["a_w_out"], out_dtype=F32, add=x0, name="a_w_out")
    x2, ffn_a = _ffn_fwd(x1, w["a_norm_ffn"], w["a_ffn_up"], w["a_ffn_conv"], w["a_ffn_down"], bl, "a")

    nkv = rmsnorm_fwd(x2, w["kv_norm"], name="kv_norm")
    kv = matmul(nkv, w["w_kv_shared"], out_dtype=F32, name="w_kv")
    n3 = rmsnorm_fwd(x2, w["b_norm_attn"], name="b_attn_norm")
    proj_b = matmul(n3, w["b_w_in"], out_dtype=F32, name="b_w_in")
    mn_b, kvm_b = _mem_kv(mem2, w["b_norm_mem"], w["b_w_mem_kv"], "b")
    dil = [dil_fwd(proj_b, kv, gi, bl, name=f"b_dil_fwd{gi}") for gi in range(3)]
    os_, lses = [o for o, _ in dil], [l for _, l in dil]
    o_dil = dil_combine_fwd(os_, lses, name="b_dil_combine")
    o_mem_b = mem_fwd(proj_b, kvm_b, bl, name="b_mem_fwd")
    cat_b = jnp.concatenate([o_dil, o_mem_b], axis=1)
    x3 = matmul(cat_b, w["b_w_out"], out_dtype=F32, add=x2, name="b_w_out")
    x4, ffn_b = _ffn_fwd(x3, w["b_norm_ffn"], w["b_ffn_up"], w["b_ffn_conv"], w["b_ffn_down"], bl, "b")

    dx4, dx4b, g["final_norm"], lossvec = final_loss(x4, w["final_norm"], tgt, name="final_loss")

    dx3, dx3b, g["b_norm_ffn"], g["b_ffn_up"], g["b_ffn_conv"], g["b_ffn_down"] = _ffn_bwd(
        dx4, dx4b, x3, ffn_b, w["b_norm_ffn"], w["b_ffn_up"], w["b_ffn_conv"], w["b_ffn_down"], bl, "b",
        shard_of["b_ffn_up"])
    dcat_b = matmul(dx3b, w["b_w_out"], tb=True, name="b_w_out_dx")
    g["b_w_out"] = matmul(cat_b, dx3b, ta=True, name="b_w_out_dw")
    dog, dshift = dil_combine_bwd(os_, lses, dcat_b, name="b_dil_combine_bwd")
    dqs, dks, dvs = [], [], []
    for gi in range(3):
        dq_g, dk_g, dv_g = dil_bwd(proj_b, kv, lses[gi], dog, dshift, gi, bl, name=f"b_dil_bwd{gi}")
        dqs.append(dq_g), dks.append(dk_g), dvs.append(dv_g)
    dq_mem_b, dkvm_b = mem_bwd(proj_b, kvm_b, dcat_b, bl, name="b_mem_bwd")
    g["b_norm_mem"], g["b_w_mem_kv"] = _mem_kv_bwd(mem2, mn_b, w["b_norm_mem"], w["b_w_mem_kv"], dkvm_b, "b")
    dproj_b = jnp.concatenate([dq_g.astype(BF) for dq_g in dqs] + [dq_mem_b], axis=1)
    dn3 = matmul(dproj_b, w["b_w_in"], tb=True, out_dtype=F32, name="b_w_in_dx")
    g["b_w_in"] = matmul(n3, dproj_b, ta=True, name="b_w_in_dw")
    dx2, _, g["b_norm_attn"] = rmsnorm_bwd(x2, w["b_norm_attn"], dn3, dx3, name="b_attn_norm_bwd")
    dkv = jnp.concatenate(dks + dvs, axis=1).astype(BF)
    dnkv = matmul(dkv, w["w_kv_shared"], tb=True, out_dtype=F32, name="w_kv_dx")
    g["w_kv_shared"] = matmul(nkv, dkv, ta=True, shard_cols=shard_of["w_kv_shared"], name="w_kv_dw")
    dx2, dx2b, g["kv_norm"] = rmsnorm_bwd(x2, w["kv_norm"], dnkv, dx2, name="kv_norm_bwd")

    dx1, dx1b, g["a_norm_ffn"], g["a_ffn_up"], g["a_ffn_conv"], g["a_ffn_down"] = _ffn_bwd(
        dx2, dx2b, x1, ffn_a, w["a_norm_ffn"], w["a_ffn_up"], w["a_ffn_conv"], w["a_ffn_down"], bl, "a",
        shard_of["a_ffn_up"])
    dcat_a = matmul(dx1b, w["a_w_out"], tb=True, name="a_w_out_dx")
    g["a_w_out"] = matmul(cat_a, dx1b, ta=True, name="a_w_out_dw")
    leaving = scatter_ride([_shard_major(n, g[n]) for n in early_scatter]) if early_scatter else None
    dq_sb, dk_sb, dv_sb, landed = sb_bwd(proj_a, o_sb_f32, dcat_a, bl, name="a_sb_bwd", ride=leaving)
    landed = dict(zip(early_scatter, landed))
    dq_mem_a, dkvm_a = mem_bwd(proj_a, kvm_a, dcat_a, bl, name="a_mem_bwd")
    g["a_norm_mem"], g["a_w_mem_kv"] = _mem_kv_bwd(mem2, mn_a, w["a_norm_mem"], w["a_w_mem_kv"], dkvm_a, "a")
    dproj_a = jnp.concatenate([dq_sb, dk_sb.astype(BF), dv_sb.astype(BF), dq_mem_a], axis=1)
    dn1 = matmul(dproj_a, w["a_w_in"], tb=True, out_dtype=F32, name="a_w_in_dx")
    g["a_w_in"] = matmul(n1, dproj_a, ta=True, shard_cols=shard_of["a_w_in"], name="a_w_in_dw")
    dx0, _, g["a_norm_attn"] = rmsnorm_bwd(x0, w["a_norm_attn"], dn1, dx1, name="a_attn_norm_bwd")
    return lossvec, dx0, g, landed


MATRICES = ("a_w_in", "a_w_out", "a_w_mem_kv", "a_ffn_up", "a_ffn_down", "w_kv_shared", "b_w_in", "b_w_out",
            "b_w_mem_kv", "b_ffn_up", "b_ffn_down")
COL_SHARDED = ("a_w_in", "a_ffn_up", "w_kv_shared", "b_ffn_up")
FIRST_NEEDED = ("a_w_in", "a_w_mem_kv")
SMALL_SHARDED = ("a_norm_attn", "a_norm_mem", "a_norm_ffn", "a_ffn_conv", "b_ffn_conv")
SMALL_REPLICATED = ("kv_norm", "b_norm_attn", "b_norm_mem", "b_norm_ffn", "final_norm")
WEIGHTS = ("a_norm_attn", "a_w_in", "a_w_out", "a_norm_mem", "a_w_mem_kv", "a_norm_ffn", "a_ffn_up", "a_ffn_conv",
           "a_ffn_down", "kv_norm", "w_kv_shared", "b_norm_attn", "b_w_in", "b_w_out", "b_norm_mem", "b_w_mem_kv",
           "b_norm_ffn", "b_ffn_up", "b_ffn_conv", "b_ffn_down", "final_norm")


def _two_d(a):
    if a.ndim == 1:
        return a.reshape(1, -1)
    return a.reshape(a.shape[-2], a.shape[-1])


def kernel(x, mem, a_norm_attn, a_w_in, a_w_out, a_norm_mem, a_w_mem_kv, a_norm_ffn, a_ffn_up, a_ffn_conv, a_ffn_down, kv_norm, w_kv_shared, b_norm_attn, b_w_in, b_w_out, b_norm_mem, b_w_mem_kv, b_norm_ffn, b_ffn_up, b_ffn_conv, b_ffn_down, final_norm, loss_target, m_a_norm_attn, m_a_w_in, m_a_w_out, m_a_norm_mem, m_a_w_mem_kv, m_a_norm_ffn, m_a_ffn_up, m_a_ffn_conv, m_a_ffn_down, m_kv_norm, m_w_kv_shared, m_b_norm_attn, m_b_w_in, m_b_w_out, m_b_norm_mem, m_b_w_mem_kv, m_b_norm_ffn, m_b_ffn_up, m_b_ffn_conv, m_b_ffn_down, m_final_norm, v_a_norm_attn, v_a_w_in, v_a_w_out, v_a_norm_mem, v_a_w_mem_kv, v_a_norm_ffn, v_a_ffn_up, v_a_ffn_conv, v_a_ffn_down, v_kv_norm, v_w_kv_shared, v_b_norm_attn, v_b_w_in, v_b_w_out, v_b_norm_mem, v_b_w_mem_kv, v_b_norm_ffn, v_b_ffn_up, v_b_ffn_conv, v_b_ffn_down, v_final_norm):
    given = dict(locals())
    wl = {n: _two_d(given[n]) for n in WEIGHTS}
    ml = {n: _two_d(given["m_" + n]) for n in WEIGHTS}
    vl = {n: _two_d(given["v_" + n]) for n in WEIGHTS}
    chip = 2 * lax.axis_index("x") + lax.axis_index("y")

    packed = jnp.concatenate([wl[n].reshape(-1, LANES) for n in SMALL_SHARDED], axis=0)
    axis_of = lambda n: 1 if n in COL_SHARDED else 0
    late = tuple(n for n in MATRICES if n not in FIRST_NEEDED)
    full = gather_ride([wl[n].astype(BF) for n in FIRST_NEEDED] + [packed],
                       [axis_of(n) for n in FIRST_NEEDED] + [0]).run("gather_first")
    w = dict(zip(FIRST_NEEDED, full[:-1]))
    late_gather = (gather_ride([wl[n].astype(BF) for n in late], [axis_of(n) for n in late]), late)
    rows = packed.shape[0]
    per_chip = full[-1].reshape(N_CHIPS, rows, LANES)
    r0 = 0
    for n in SMALL_SHARDED:
        nr = wl[n].size // LANES
        piece = per_chip[:, r0:r0 + nr].reshape(N_CHIPS, wl[n].shape[0], wl[n].shape[1])
        w[n] = jnp.concatenate([piece[q] for q in range(N_CHIPS)], axis=1)
        r0 += nr
    for n in SMALL_REPLICATED:
        w[n] = wl[n]

    shard_of = {n: wl[n].shape[1] for n in COL_SHARDED}
    lossvec, dx0, g, landed = local_step(x, mem, loss_target, w, shard_of, late_gather=late_gather, early_scatter=late)
    loss = lax.psum(0.5 * jnp.sum(lossvec) / x.shape[-1], ("x", "y", "c"))

    g4 = {n: _shard_major(n, g[n]) for n in MATRICES}
    landed.update(zip(FIRST_NEEDED, scatter_ride([g4[n] for n in FIRST_NEEDED]).run("scatter_last")))
    sums = [sum4(lax.dynamic_index_in_dim(g4[n], chip, 0, keepdims=False), landed[n], name=f"sum4_{n}")
            for n in MATRICES]
    theirs = swap_with_sibling(sums, name="swap_sums")
    out = {}
    for k, n in enumerate(MATRICES):
        out[n] = adamw(wl[n], sums[k], theirs[k], ml[n], vl[n], name=f"adamw_{n}")

    small = SMALL_SHARDED + SMALL_REPLICATED
    flat = jnp.concatenate([g[n].reshape(-1, LANES) for n in small], axis=0)
    tot = all_reduce_small(flat, name="all_reduce_small")
    r0 = 0
    for n in small:
        nr = g[n].size // LANES
        gn = tot[r0:r0 + nr].reshape(g[n].shape)
        r0 += nr
        if n in SMALL_SHARDED:
            gn = lax.dynamic_slice_in_dim(gn, chip * wl[n].shape[1], wl[n].shape[1], axis=1)
        out[n] = adamw(wl[n], gn, None, ml[n], vl[n], name=f"adamw_{n}")

    res = [loss, dx0.reshape(x.shape)]
    for slot in range(4):
        res += [out[n][slot].reshape(given[n].shape) for n in WEIGHTS]
    return tuple(res)
```

```python
import functools
import math

import jax
import jax.numpy as jnp
from jax import lax
from jax.experimental import pallas as pl
from jax.experimental.pallas import tpu as pltpu

F32 = jnp.float32
BF = jnp.bfloat16
MESH = pl.DeviceIdType.MESH

HEAD_DIM = 64
LANES = 128
SB_WIDTH = 12 * HEAD_DIM
MEM_WIDTH = 4 * HEAD_DIM
DIL_WIDTH = 12 * HEAD_DIM
MEM_LEN = 256
DIL_GROUPS = ((128, 1), (512, 4), (2048, 16))
QB = 128
EPS = 1e-6
SCALE = HEAD_DIM ** -0.5
NEG = -1e30
ALIBI = tuple(2.0 ** (-8.0 * i / 12) for i in range(1, 13))
N_CHIPS = 4
N_DEV = 8

ADAM_LR, ADAM_B1, ADAM_B2, ADAM_EPS, ADAM_WD, ADAM_STEP = 0.001, 0.9, 0.999, 1e-08, 0.01, 10

VMEM_LIMIT = 48 * 1024 * 1024


def _mo(v, m):
    return v if isinstance(v, int) else pl.multiple_of(v, m)


def _pick(n, prefs):
    for t in prefs:
        if n % t == 0:
            return t
    return n


def _params(*sem):
    return pltpu.CompilerParams(dimension_semantics=sem, vmem_limit_bytes=VMEM_LIMIT)


def matmul(a, b, *, ta=False, tb=False, out_dtype=BF, add=None, shard_cols=0, a2=None, b2=None, name):
    m, k = (a.shape[1], a.shape[0]) if ta else a.shape
    n = b.shape[0] if tb else b.shape[1]
    if a2 is not None:
        assert not ta and a2.shape == a.shape
        k *= 2
    if b2 is not None:
        assert not tb and b2.shape == b.shape
        n *= 2
    tm = _pick(m, (512, 1408, 256, 128))
    tk = _pick(a.shape[1] if a2 is not None else k,
               (2048, 2816, 2560, 1536, 1024, 512, 256, 128))
    tn = shard_cols if shard_cols else _pick(b.shape[1] if b2 is not None else n,
                                             (2816, 2560, 1536, 1024, 1408, 512, 256, 128))
    nk = k // tk
    nk1 = nk // 2
    nj1 = (n // tn) // 2
    dims = (((0,) if ta else (1,), (1,) if tb else (0,)), ((), ()))
    n_in = 2 + (a2 is not None) + (b2 is not None) + (add is not None)

    def body(*refs):
        ins = list(refs[:n_in])
        o_ref = refs[n_in]
        a_tile = ins.pop(0)[...]
        b_tile = ins.pop(0)[...]
        if a2 is not None:
            a_tile = jnp.where(pl.program_id(2) < nk1, a_tile, ins.pop(0)[...])
        if b2 is not None:
            b_tile = jnp.where(pl.program_id(1) < nj1, b_tile, ins.pop(0)[...])
        add_ref = ins.pop(0) if add is not None else None
        part = lax.dot_general(a_tile.astype(BF), b_tile.astype(BF), dims, preferred_element_type=F32)

        def finish(r):
            if add is not None:
                r = r + add_ref[...]
            o_ref[...] = r.astype(o_ref.dtype)

        if nk == 1:
            finish(part)
            return
        acc_ref = refs[-1]
        kk = pl.program_id(2)

        @pl.when(kk == 0)
        def _():
            acc_ref[...] = part

        @pl.when(kk > 0)
        def _():
            acc_ref[...] += part

        @pl.when(kk == nk - 1)
        def _():
            finish(acc_ref[...])

    a_spec = pl.BlockSpec((tk, tm), lambda i, j, q: (q, i)) if ta else pl.BlockSpec((tm, tk), lambda i, j, q: (i, q))
    b_spec = pl.BlockSpec((tn, tk), lambda i, j, q: (j, q)) if tb else pl.BlockSpec((tk, tn), lambda i, j, q: (q, j))
    if a2 is not None:
        a_spec = pl.BlockSpec((tm, tk), lambda i, j, q: (i, jnp.minimum(q, nk1 - 1)))
    if b2 is not None:
        b_spec = pl.BlockSpec((tk, tn), lambda i, j, q: (jnp.where(j < nj1, q, 0), jnp.minimum(j, nj1 - 1)))
    in_specs = [a_spec, b_spec]
    args = [a, b]
    if a2 is not None:
        in_specs.append(pl.BlockSpec((tm, tk), lambda i, j, q: (i, jnp.maximum(q - nk1, 0))))
        args.append(a2)
    if b2 is not None:
        in_specs.append(pl.BlockSpec((tk, tn), lambda i, j, q: (jnp.where(j < nj1, 0, q), jnp.maximum(j - nj1, 0))))
        args.append(b2)
    if add is not None:
        in_specs.append(pl.BlockSpec((tm, tn), lambda i, j, q: (i, j)))
        args.append(add)
    if shard_cols:
        out_shape = jax.ShapeDtypeStruct((N_CHIPS, m, shard_cols), out_dtype)
        out_spec = pl.BlockSpec((None, tm, tn), lambda i, j, q: (j, i, 0))
    else:
        out_shape = jax.ShapeDtypeStruct((m, n), out_dtype)
        out_spec = pl.BlockSpec((tm, tn), lambda i, j, q: (i, j))
    return pl.pallas_call(
        body, name=name, grid=(m // tm, n // tn, nk), in_specs=in_specs, out_specs=out_spec, out_shape=out_shape,
        scratch_shapes=[pltpu.VMEM((tm, tn), F32)] if nk > 1 else [],
        compiler_params=_params("parallel", "parallel", "arbitrary"),
    )(*args)


def rmsnorm_fwd(x, g, *, name):
    t, d = x.shape
    tr = _pick(t, (512, 256))

    def body(x_ref, g_ref, o_ref):
        xv = x_ref[...]
        r = lax.rsqrt(jnp.mean(xv * xv, axis=-1, keepdims=True) + EPS)
        o_ref[...] = ((xv * r) * g_ref[...]).astype(o_ref.dtype)

    return pl.pallas_call(
        body, name=name, grid=(t // tr,),
        in_specs=[pl.BlockSpec((tr, d), lambda i: (i, 0)), pl.BlockSpec((1, d), lambda i: (0, 0))],
        out_specs=pl.BlockSpec((tr, d), lambda i: (i, 0)), out_shape=jax.ShapeDtypeStruct((t, d), BF),
        compiler_params=_params("parallel"),
    )(x, g)


def rmsnorm_bwd(x, g, dn, dres, *, name):
    t, d = x.shape
    tr = _pick(t, (512, 256))
    want_dx = dres is not None

    def body(*refs):
        if want_dx:
            x_ref, g_ref, dn_ref, dres_ref, dx_ref, dxb_ref, dg_ref = refs
        else:
            x_ref, g_ref, dn_ref, dg_ref = refs

        @pl.when(pl.program_id(0) == 0)
        def _():
            dg_ref[...] = jnp.zeros_like(dg_ref)

        xv = x_ref[...]
        r = lax.rsqrt(jnp.mean(xv * xv, axis=-1, keepdims=True) + EPS)
        xn = xv * r
        dnv = dn_ref[...].astype(F32)
        dg_ref[...] += jnp.sum(dnv * xn, axis=0, keepdims=True)
        if want_dx:
            dyg = dnv * g_ref[...]
            cm = jnp.mean(dyg * xn, axis=-1, keepdims=True)
            dx = dres_ref[...] + r * (dyg - xn * cm)
            dx_ref[...] = dx
            dxb_ref[...] = dx.astype(BF)

    row = pl.BlockSpec((tr, d), lambda i: (i, 0))
    vec = pl.BlockSpec((1, d), lambda i: (0, 0))
    if want_dx:
        return pl.pallas_call(
            body, name=name, grid=(t // tr,), in_specs=[row, vec, row, row], out_specs=[row, row, vec],
            out_shape=[jax.ShapeDtypeStruct((t, d), F32), jax.ShapeDtypeStruct((t, d), BF), jax.ShapeDtypeStruct((1, d), F32)],
            compiler_params=_params("arbitrary"),
        )(x, g, dn, dres)
    return None, None, pl.pallas_call(
        body, name=name, grid=(t // tr,), in_specs=[row, vec, row], out_specs=vec,
        out_shape=jax.ShapeDtypeStruct((1, d), F32), compiler_params=_params("arbitrary"),
    )(x, g, dn)


def final_loss(x, g, target, *, name):
    t, d = x.shape
    tr = _pick(t, (512, 256))

    def body(x_ref, g_ref, t_ref, dx_ref, dxb_ref, dg_ref, lv_ref):
        @pl.when(pl.program_id(0) == 0)
        def _():
            dg_ref[...] = jnp.zeros_like(dg_ref)
            lv_ref[...] = jnp.zeros_like(lv_ref)

        xv = x_ref[...]
        r = lax.rsqrt(jnp.mean(xv * xv, axis=-1, keepdims=True) + EPS)
        xn = xv * r
        err = xn * g_ref[...] - t_ref[...]
        lv_ref[...] += jnp.sum(err * err, axis=0, keepdims=True)
        dy = err * (1.0 / d)
        dg_ref[...] += jnp.sum(dy * xn, axis=0, keepdims=True)
        dyg = dy * g_ref[...]
        cm = jnp.mean(dyg * xn, axis=-1, keepdims=True)
        dx = r * (dyg - xn * cm)
        dx_ref[...] = dx
        dxb_ref[...] = dx.astype(BF)

    row = pl.BlockSpec((tr, d), lambda i: (i, 0))
    vec = pl.BlockSpec((1, d), lambda i: (0, 0))
    return pl.pallas_call(
        body, name=name, grid=(t // tr,), in_specs=[row, vec, row], out_specs=[row, row, vec, vec],
        out_shape=[jax.ShapeDtypeStruct((t, d), F32), jax.ShapeDtypeStruct((t, d), BF), jax.ShapeDtypeStruct((1, d), F32),
                   jax.ShapeDtypeStruct((1, d), F32)],
        compiler_params=_params("arbitrary"),
    )(x, g, target)


CONV_ROWS = 256
HALO = 16


def _conv_taps(ext, w, rows):
    s0 = ext[HALO:HALO + rows]
    s1 = pltpu.roll(ext, 1, 0)[HALO:HALO + rows]
    s2 = pltpu.roll(ext, 2, 0)[HALO:HALO + rows]
    return (w[0:1] * s2 + w[1:2] * s1) + w[2:3] * s0, s0, s1, s2


def conv_silu_fwd(u, w, bl, *, name):
    t, f2 = u.shape
    f = f2 // 2
    s = t // bl
    tc = _pick(f, (256, 128))
    nf = f // tc
    nch = s // CONV_ROWS

    def body(ua_ref, ug_ref, wa_ref, wg_ref, h_ref, ca_ref, cg_ref):
        wa, wg = wa_ref[...], wg_ref[...]

        def chunk(ci, carry):
            r0 = pl.multiple_of(ci * CONV_ROWS, CONV_ROWS)
            ps = pl.multiple_of(jnp.maximum(r0 - HALO, 0), HALO)
            keep = (ci > 0).astype(F32)
            rows = pl.ds(r0, CONV_ROWS)

            def conv(ref, wv):
                ext = jnp.concatenate([ref[0, pl.ds(ps, HALO), :].astype(F32) * keep, ref[0, rows, :].astype(F32)], axis=0)
                return _conv_taps(ext, wv, CONV_ROWS)[0]

            ca, cg = conv(ua_ref, wa), conv(ug_ref, wg)
            sg = pl.reciprocal(1.0 + jnp.exp(-cg), approx=True)
            h_ref[0, rows, :] = ((cg * sg) * ca).astype(h_ref.dtype)
            ca_ref[0, rows, :] = ca.astype(ca_ref.dtype)
            cg_ref[0, rows, :] = cg.astype(cg_ref.dtype)
            return carry

        lax.fori_loop(0, nch, chunk, 0)

    u3 = u.reshape(bl, s, f2)
    blk = lambda off: pl.BlockSpec((1, s, tc), lambda j, b: (b, 0, j + off))
    wblk = lambda off: pl.BlockSpec((3, tc), lambda j, b: (0, j + off))
    h, ca, cg = pl.pallas_call(
        body, name=name, grid=(nf, bl), in_specs=[blk(0), blk(nf), wblk(0), wblk(nf)],
        out_specs=[blk(0)] * 3, out_shape=[jax.ShapeDtypeStruct((bl, s, f), BF)] * 3,
        compiler_params=_params("parallel", "parallel"),
    )(u3, u3, w, w)
    return h.reshape(t, f), ca.reshape(t, f), cg.reshape(t, f)


def conv_silu_bwd(u, ca, cg, w, dh, bl, *, name):
    t, f2 = u.shape
    f = f2 // 2
    s = t // bl
    tc = _pick(f, (256, 128))
    nf = f // tc
    nch = s // CONV_ROWS
    ext_rows = CONV_ROWS + HALO

    def body(ua_ref, ug_ref, ca_ref, cg_ref, wa_ref, wg_ref, dh_ref, dua_ref, dug_ref, dwa_ref, dwg_ref):
        wa, wg = wa_ref[...], wg_ref[...]

        @pl.when(pl.program_id(1) == 0)
        def _():
            dwa_ref[...] = jnp.zeros_like(dwa_ref)
            dwg_ref[...] = jnp.zeros_like(dwg_ref)

        def chunk(ci, carry):
            r0 = pl.multiple_of(ci * CONV_ROWS, CONV_ROWS)
            ns = pl.multiple_of(jnp.minimum(r0 + CONV_ROWS, s - HALO), HALO)
            keep_n = (ci < nch - 1).astype(F32)
            rows = pl.ds(r0, CONV_ROWS)

            def ext_of(ref):
                return jnp.concatenate([ref[0, rows, :].astype(F32), ref[0, pl.ds(ns, HALO), :].astype(F32)], axis=0)

            dhe = jnp.concatenate([dh_ref[0, rows, :].astype(F32), dh_ref[0, pl.ds(ns, HALO), :].astype(F32) * keep_n],
                                  axis=0)
            cae, cge = ext_of(ca_ref), ext_of(cg_ref)
            sg = pl.reciprocal(1.0 + jnp.exp(-cge), approx=True)
            dca = dhe * (cge * sg)
            dcg = dhe * cae * (sg * (1.0 + cge * (1.0 - sg)))

            def back(dc, wv, u_ref, du_ref, dw_ref):
                d0 = dc[:CONV_ROWS]
                n1 = pltpu.roll(dc, ext_rows - 1, 0)[:CONV_ROWS]
                n2 = pltpu.roll(dc, ext_rows - 2, 0)[:CONV_ROWS]
                du_ref[0, rows, :] = ((wv[2:3] * d0 + wv[1:2] * n1) + wv[0:1] * n2).astype(du_ref.dtype)
                uc = u_ref[0, rows, :].astype(F32)
                for k, shifted in enumerate((n2, n1, d0)):
                    dw_ref[k:k + 1, :] += jnp.sum(shifted * uc, axis=0, keepdims=True)

            back(dca, wa, ua_ref, dua_ref, dwa_ref)
            back(dcg, wg, ug_ref, dug_ref, dwg_ref)
            return carry

        lax.fori_loop(0, nch, chunk, 0)

    u3 = u.reshape(bl, s, f2)
    blk = lambda off: pl.BlockSpec((1, s, tc), lambda j, b: (b, 0, j + off))
    wblk = lambda off: pl.BlockSpec((3, tc), lambda j, b: (0, j + off))
    dua, dug, dwa, dwg = pl.pallas_call(
        body, name=name, grid=(nf, bl), in_specs=[blk(0), blk(nf), blk(0), blk(0), wblk(0), wblk(nf), blk(0)],
        out_specs=[blk(0), blk(0), wblk(0), wblk(0)],
        out_shape=[jax.ShapeDtypeStruct((bl, s, f), BF), jax.ShapeDtypeStruct((bl, s, f), BF),
                   jax.ShapeDtypeStruct((3, f), F32), jax.ShapeDtypeStruct((3, f), F32)],
        compiler_params=_params("parallel", "arbitrary"),
    )(u3, u3, ca.reshape(bl, s, f), cg.reshape(bl, s, f), w, w, dh.reshape(bl, s, f))
    return dua.reshape(t, f), dug.reshape(t, f), jnp.concatenate([dwa, dwg], axis=1)


def _lane_masks(rows):
    lane = lax.broadcasted_iota(jnp.int32, (rows, LANES), 1)
    return lane < HEAD_DIM, lane >= HEAD_DIM


def _nt(a, b):
    return lax.dot_general(a, b, (((1,), (1,)), ((), ())), preferred_element_type=F32)


def _tn(a, b):
    return lax.dot_general(a, b, (((0,), (0,)), ((), ())), preferred_element_type=F32)


def _nn(a, b):
    return jnp.dot(a, b, preferred_element_type=F32)


def _suffix_ones():
    j = lax.broadcasted_iota(jnp.int32, (2 * QB, QB), 0) % QB
    s = lax.broadcasted_iota(jnp.int32, (2 * QB, QB), 1)
    return (j >= s).astype(BF)


def _softplus(z):
    return jnp.maximum(z, 0.0) + jnp.log(1.0 + jnp.exp(-jnp.abs(z)))


SB_BLOCK = 256
SB_DEAD = -100.0


SB_STRIP = 32
SB_SUB = SB_BLOCK // QB


def _sb_scratch(backward):
    blk = (2, SB_BLOCK, SB_BLOCK)
    per_head = pltpu.VMEM((2, SB_BLOCK, LANES), F32)
    scr = [pltpu.VMEM((2 * QB, QB), BF),
           pltpu.VMEM(blk, F32),
           pltpu.VMEM((2, SB_SUB) + blk[1:], BF),
           pltpu.VMEM(blk, F32),
           pltpu.VMEM(blk, BF),
           per_head]
    if not backward:
        return scr + [per_head]
    return scr + [pltpu.VMEM(blk, F32),
                  pltpu.VMEM(blk, F32),
                  pltpu.VMEM(blk, F32),
                  pltpu.VMEM(blk, BF),
                  per_head,
                  per_head,
                  per_head]


def _sb_strips(fn):
    def step(r, c):
        fn(pl.ds(pl.multiple_of(r * SB_STRIP, SB_STRIP), SB_STRIP), r)
        return c
    lax.fori_loop(0, SB_BLOCK // SB_STRIP, step, 0, unroll=True)


def _sb_causal(thr, r):
    rel = (lax.broadcasted_iota(jnp.int32, (SB_STRIP, SB_BLOCK), 1)
           - lax.broadcasted_iota(jnp.int32, (SB_STRIP, SB_BLOCK), 0))
    return rel < thr + r * SB_STRIP


def _sb_split_store(sp_scr, rows, v):
    hi = v.astype(BF)
    lo = (v - hi.astype(F32)).astype(BF)
    for u in range(SB_SUB):
        sp_scr[u, rows, 0:QB] = hi[:, u * QB:(u + 1) * QB]
        sp_scr[u, rows, QB:2 * QB] = lo[:, u * QB:(u + 1) * QB]


def _sb_suffix_sums(uu_scr, sp_scr, out_scr):
    for u in range(SB_SUB):
        out_scr[:, u * QB:(u + 1) * QB] = _nn(sp_scr[u], uu_scr[...])


def _sb_fold(sums, off):
    offs = [None] * SB_SUB
    for u in reversed(range(SB_SUB)):
        offs[u] = off
        off = off + jnp.broadcast_to(sums[:, u * QB:u * QB + 1], (SB_STRIP, LANES))
    return offs, off


def _sb_first(thr, z_scr, sp_scr, ls_scr=None):
    def strip(rows, r):
        ls = jnp.where(_sb_causal(thr, r), -_softplus(z_scr[rows, :]), 0.0)
        if ls_scr is not None:
            ls_scr[rows, :] = ls
        _sb_split_store(sp_scr, rows, ls)

    _sb_strips(strip)


def _sb_second(thr, z_scr, cin_scr, w_scr, carry_scr):
    def strip(rows, r):
        z, cin = z_scr[rows, :], cin_scr[rows, :]
        offs, carry_scr[rows, :] = _sb_fold(cin, carry_scr[rows, :])
        e = jnp.concatenate([z[:, u * QB:(u + 1) * QB] + cin[:, u * QB:(u + 1) * QB] + offs[u] for u in range(SB_SUB)],
                            axis=1)
        w_scr[rows, :] = jnp.where(_sb_causal(thr, r), jnp.exp(e), 0.0).astype(BF)

    _sb_strips(strip)


def _sb_third(w_scr, dw_scr, da_scr, sp_scr):
    def strip(rows, r):
        da = w_scr[rows, :].astype(F32) * dw_scr[rows, :]
        da_scr[rows, :] = da
        _sb_split_store(sp_scr, rows, da)

    _sb_strips(strip)


def _sb_fourth(thr, z_scr, ls_scr, da_scr, sin_scr, dz_scr, carry_s_scr, dsum_scr):
    def strip(rows, r):
        da, sin = da_scr[rows, :], sin_scr[rows, :]
        offs, carry_s_scr[rows, :] = _sb_fold(sin, carry_s_scr[rows, :])
        dsum = dsum_scr[rows, :]
        pre = jnp.concatenate([dsum - (sin[:, u * QB:(u + 1) * QB] - da[:, u * QB:(u + 1) * QB] + offs[u])
                               for u in range(SB_SUB)], axis=1)
        sig = jnp.exp(z_scr[rows, :] + ls_scr[rows, :])
        dz_scr[rows, :] = jnp.where(_sb_causal(thr, r), da - sig * pre, 0.0).astype(BF)

    _sb_strips(strip)


def _sb_alive(carry_scr):
    return (jnp.max(carry_scr[...]) > SB_DEAD).astype(jnp.int32)


def _ride_hooks(ride, refs, n_in, n_out, n_scratch, grid):
    if ride is None:
        return refs[:n_in], refs[n_in:n_in + n_out], refs[n_in + n_out:], lambda: None, lambda: None
    ni, no = len(ride.arrays), len(ride.out_shape)
    own_in, rin = refs[:n_in], refs[n_in:n_in + ni]
    own_out, rout = refs[n_in + ni:n_in + ni + n_out], refs[n_in + ni + n_out:n_in + ni + n_out + no]
    rest = refs[n_in + ni + n_out + no:]
    own_scr, sems = rest[:n_scratch], rest[n_scratch:]
    ids = [pl.program_id(a) for a in range(len(grid))]

    def start():
        first = functools.reduce(lambda u, v: u & v, [i == 0 for i in ids])
        pl.when(first)(lambda: ride.start(rin, rout, sems))

    def wait():
        last = functools.reduce(lambda u, v: u & v, [i == n - 1 for i, n in zip(ids, grid)])
        pl.when(last)(lambda: ride.wait(rin, rout, sems))

    return own_in, own_out, own_scr, start, wait


def sb_fwd(proj, bl, *, name, ride=None):
    t, width = proj.shape
    s = t // bl
    npair = SB_WIDTH // LANES
    nq = s // SB_BLOCK
    grid = (bl, npair, nq)

    own_scratch = _sb_scratch(backward=False)

    def body(*refs):
        (q_ref, k_ref, v_ref), (o_ref, of_ref), scr, ride_start, ride_wait = _ride_hooks(
            ride, refs, 3, 2, len(own_scratch), grid)
        uu_scr, z_scr, sp_scr, cin_scr, w_scr, carry_scr, acc_scr = scr
        ride_start()
        i = pl.program_id(2)
        uu_scr[...] = _suffix_ones()
        carry_scr[...] = jnp.zeros_like(carry_scr)
        acc_scr[...] = jnp.zeros_like(acc_scr)
        qs = (q_ref[0].astype(F32) * SCALE).astype(BF)
        masks = _lane_masks(SB_BLOCK)

        z, sp, cin, w, car, acc = ([r.at[h] for h in range(2)] for r in (z_scr, sp_scr, cin_scr, w_scr, carry_scr, acc_scr))

        def live(c):
            start = pl.multiple_of(c * SB_BLOCK, SB_BLOCK)
            thr = (i - c) * SB_BLOCK
            kb, vb = k_ref[0, pl.ds(start, SB_BLOCK), :], v_ref[0, pl.ds(start, SB_BLOCK), :]
            for h in range(2):
                z[h][...] = _nt(jnp.where(masks[h], qs, jnp.zeros_like(qs)), kb)
            _sb_first(thr, z[0], sp[0])
            _sb_suffix_sums(uu_scr, sp[0], cin[0])
            _sb_first(thr, z[1], sp[1])
            _sb_suffix_sums(uu_scr, sp[1], cin[1])
            _sb_second(thr, z[0], cin[0], w[0], car[0])
            acc[0][...] += _nn(w[0][...], vb)
            _sb_second(thr, z[1], cin[1], w[1], car[1])
            acc[1][...] += _nn(w[1][...], vb)
            return _sb_alive(carry_scr)

        def step(n, alive):
            return lax.cond(alive > 0, lambda: live(i - n), lambda: alive)

        lax.fori_loop(0, i + 1, step, jnp.int32(1))
        both = jnp.where(masks[0], acc_scr[0], acc_scr[1])
        o_ref[0] = both.astype(o_ref.dtype)
        of_ref[0] = both
        ride_wait()

    p3 = proj.reshape(bl, s, width)
    qblk = pl.BlockSpec((1, SB_BLOCK, LANES), lambda b, p, i: (b, i, p))
    extra = ride.arrays if ride else []
    o, of, *delivered = pl.pallas_call(
        body, name=name, grid=grid,
        in_specs=[qblk, pl.BlockSpec((1, s, LANES), lambda b, p, i: (b, 0, npair + p)),
                  pl.BlockSpec((1, s, LANES), lambda b, p, i: (b, 0, 2 * npair + p))] + [ANY] * len(extra),
        out_specs=[qblk, qblk] + [ANY] * (len(ride.out_shape) if ride else 0),
        out_shape=[jax.ShapeDtypeStruct((bl, s, SB_WIDTH), BF), jax.ShapeDtypeStruct((bl, s, SB_WIDTH), F32)]
        + (ride.out_shape if ride else []),
        scratch_shapes=own_scratch + (ride.scratch if ride else []),
        compiler_params=_params("arbitrary", "arbitrary", "arbitrary"),
    )(p3, p3, p3, *extra)
    return o.reshape(t, SB_WIDTH), of.reshape(t, SB_WIDTH), delivered


def sb_bwd(proj, o, dcat, bl, *, name, ride=None):
    t, width = proj.shape
    s = t // bl
    npair = SB_WIDTH // LANES
    nq = s // SB_BLOCK
    grid = (bl, npair, nq)
    own_scratch = _sb_scratch(backward=True)

    def body(*refs):
        (q_ref, k_ref, v_ref, o_ref, do_ref), (dq_ref, dk_ref, dv_ref), scr, ride_start, ride_wait = _ride_hooks(
            ride, refs, 5, 3, len(own_scratch), grid)
        (uu_scr, z_scr, sp_scr, cin_scr, w_scr, carry_scr,
         ls_scr, dw_scr, da_scr, dz_scr, carry_s_scr, dsum_scr, dq_scr) = scr
        ride_start()
        i = pl.program_id(2)

        @pl.when(i == 0)
        def _():
            dk_ref[...] = jnp.zeros_like(dk_ref)
            dv_ref[...] = jnp.zeros_like(dv_ref)

        uu_scr[...] = _suffix_ones()
        for ref in (carry_scr, carry_s_scr, dq_scr):
            ref[...] = jnp.zeros_like(ref)
        masks = _lane_masks(SB_BLOCK)
        qs = (q_ref[0].astype(F32) * SCALE).astype(BF)
        do = do_ref[0]
        prod = do.astype(F32) * o_ref[0]
        for h in range(2):
            dsum_scr[h] = jnp.broadcast_to(jnp.sum(jnp.where(masks[h], prod, 0.0), axis=1, keepdims=True),
                                           (SB_BLOCK, LANES))

        z, sp, cin, w, car, ls, dw, da, dz, cars, dsum, dq = (
            [r.at[h] for h in range(2)] for r in (z_scr, sp_scr, cin_scr, w_scr, carry_scr, ls_scr, dw_scr, da_scr,
                                                  dz_scr, carry_s_scr, dsum_scr, dq_scr))

        def live(c):
            start = pl.multiple_of(c * SB_BLOCK, SB_BLOCK)
            thr = (i - c) * SB_BLOCK
            keys = pl.ds(start, SB_BLOCK)
            kb, vb = k_ref[0, keys, :], v_ref[0, keys, :]
            qh = [jnp.where(masks[h], qs, jnp.zeros_like(qs)) for h in range(2)]
            doh = [jnp.where(masks[h], do, jnp.zeros_like(do)) for h in range(2)]
            for h in range(2):
                z[h][...] = _nt(qh[h], kb)
                dw[h][...] = _nt(doh[h], vb)

            def weights(h):
                _sb_second(thr, z[h], cin[h], w[h], car[h])
                _sb_third(w[h], dw[h], da[h], sp[h])
                _sb_suffix_sums(uu_scr, sp[h], cin[h])

            def grads(h):
                _sb_fourth(thr, z[h], ls[h], da[h], cin[h], dz[h], cars[h], dsum[h])
                dk_ref[0, keys, :] += _tn(dz[h][...], qh[h])
                dv_ref[0, keys, :] += _tn(w[h][...], doh[h])
                dq[h][...] += _nn(dz[h][...], kb)

            _sb_first(thr, z[0], sp[0], ls[0])
            _sb_suffix_sums(uu_scr, sp[0], cin[0])
            _sb_first(thr, z[1], sp[1], ls[1])
            _sb_suffix_sums(uu_scr, sp[1], cin[1])
            weights(0)
            weights(1)
            grads(0)
            grads(1)
            return _sb_alive(carry_scr)

        def step(n, alive):
            return lax.cond(alive > 0, lambda: live(i - n), lambda: alive)

        lax.fori_loop(0, i + 1, step, jnp.int32(1))
        dq_ref[0] = (jnp.where(masks[0], dq_scr[0], dq_scr[1]) * SCALE).astype(dq_ref.dtype)
        ride_wait()

    p3 = proj.reshape(bl, s, width)
    o3 = o.reshape(bl, s, SB_WIDTH)
    d3 = dcat.reshape(bl, s, dcat.shape[1])
    qblk = pl.BlockSpec((1, SB_BLOCK, LANES), lambda b, p, i: (b, i, p))
    full = pl.BlockSpec((1, s, LANES), lambda b, p, i: (b, 0, p))
    extra = ride.arrays if ride else []
    dq, dk, dv, *delivered = pl.pallas_call(
        body, name=name, grid=grid,
        in_specs=[qblk, pl.BlockSpec((1, s, LANES), lambda b, p, i: (b, 0, npair + p)),
                  pl.BlockSpec((1, s, LANES), lambda b, p, i: (b, 0, 2 * npair + p)), qblk, qblk] + [ANY] * len(extra),
        out_specs=[qblk, full, full] + [ANY] * (len(ride.out_shape) if ride else 0),
        out_shape=[jax.ShapeDtypeStruct((bl, s, SB_WIDTH), BF), jax.ShapeDtypeStruct((bl, s, SB_WIDTH), F32),
                   jax.ShapeDtypeStruct((bl, s, SB_WIDTH), F32)] + (ride.out_shape if ride else []),
        scratch_shapes=own_scratch + (ride.scratch if ride else []),
        compiler_params=_params("arbitrary", "arbitrary", "arbitrary"),
    )(p3, p3, p3, o3, d3, *extra)
    return dq.reshape(t, SB_WIDTH), dk.reshape(t, SB_WIDTH), dv.reshape(t, SB_WIDTH), delivered


MEM_TQ = 512


def mem_fwd(proj, kvm, bl, *, name):
    t, width = proj.shape
    s = t // bl
    qoff = (width - MEM_WIDTH) // LANES
    npair = MEM_WIDTH // LANES

    def body(q_ref, k_ref, v_ref, o_ref):
        ma, mb = _lane_masks(MEM_TQ)
        q = q_ref[...].astype(BF)
        outs = []
        for m in (ma, mb):
            qh = jnp.where(m, q, jnp.zeros_like(q))
            sc = _nt(qh, k_ref[...]) * SCALE
            p = jnp.exp(sc - jnp.max(sc, axis=-1, keepdims=True))
            p = p * (1.0 / jnp.sum(p, axis=-1, keepdims=True))
            outs.append(_nn(p.astype(BF), v_ref[...]))
        o_ref[...] = jnp.where(ma, outs[0], outs[1]).astype(o_ref.dtype)

    nt = s // MEM_TQ
    return pl.pallas_call(
        body, name=name, grid=(bl, npair, nt),
        in_specs=[pl.BlockSpec((MEM_TQ, LANES), lambda b, p, i: (b * nt + i, qoff + p)),
                  pl.BlockSpec((MEM_LEN, LANES), lambda b, p, i: (b, p)),
                  pl.BlockSpec((MEM_LEN, LANES), lambda b, p, i: (b, npair + p))],
        out_specs=pl.BlockSpec((MEM_TQ, LANES), lambda b, p, i: (b * nt + i, p)),
        out_shape=jax.ShapeDtypeStruct((t, MEM_WIDTH), BF),
        compiler_params=_params("parallel", "parallel", "parallel"),
    )(proj, kvm, kvm)


def mem_bwd(proj, kvm, dcat, bl, *, name):
    t, width = proj.shape
    s = t // bl
    qoff = (width - MEM_WIDTH) // LANES
    doff = (dcat.shape[1] - MEM_WIDTH) // LANES
    npair = MEM_WIDTH // LANES
    nt = s // MEM_TQ

    def body(q_ref, k_ref, v_ref, do_ref, dq_ref, dk_ref, dv_ref):
        @pl.when(pl.program_id(2) == 0)
        def _():
            dk_ref[...] = jnp.zeros_like(dk_ref)
            dv_ref[...] = jnp.zeros_like(dv_ref)

        ma, mb = _lane_masks(MEM_TQ)
        q = q_ref[...].astype(BF)
        do = do_ref[...]
        kb, vb = k_ref[...], v_ref[...]
        dqs = []
        for m in (ma, mb):
            qh = jnp.where(m, q, jnp.zeros_like(q))
            doh = jnp.where(m, do, jnp.zeros_like(do))
            sc = _nt(qh, kb) * SCALE
            p = jnp.exp(sc - jnp.max(sc, axis=-1, keepdims=True))
            p = p * (1.0 / jnp.sum(p, axis=-1, keepdims=True))
            dp = _nt(doh, vb)
            ds = p * (dp - jnp.sum(p * dp, axis=-1, keepdims=True))
            dss = (ds * SCALE).astype(BF)
            dk_ref[...] += _tn(dss, qh)
            dv_ref[...] += _tn(p.astype(BF), doh)
            dqs.append(_nn(dss, kb))
        dq_ref[...] = jnp.where(ma, dqs[0], dqs[1]).astype(dq_ref.dtype)

    kblk = pl.BlockSpec((MEM_LEN, LANES), lambda b, p, i: (b, p))
    dq, dk, dv = pl.pallas_call(
        body, name=name, grid=(bl, npair, nt),
        in_specs=[pl.BlockSpec((MEM_TQ, LANES), lambda b, p, i: (b * nt + i, qoff + p)), kblk,
                  pl.BlockSpec((MEM_LEN, LANES), lambda b, p, i: (b, npair + p)),
                  pl.BlockSpec((MEM_TQ, LANES), lambda b, p, i: (b * nt + i, doff + p))],
        out_specs=[pl.BlockSpec((MEM_TQ, LANES), lambda b, p, i: (b * nt + i, p)), kblk, kblk],
        out_shape=[jax.ShapeDtypeStruct((t, MEM_WIDTH), BF), jax.ShapeDtypeStruct((bl * MEM_LEN, MEM_WIDTH), F32),
                   jax.ShapeDtypeStruct((bl * MEM_LEN, MEM_WIDTH), F32)],
        compiler_params=_params("parallel", "parallel", "arbitrary"),
    )(proj, kvm, kvm, dcat)
    return dq, jnp.concatenate([dk, dv], axis=1).astype(BF)


def _dil_rows(r, u0, size, dil):
    if dil == 1:
        return pl.ds(_mo(u0, QB), size)
    return pl.ds(u0 * dil + r, size, stride=dil)


def _dil_walk(block, dil, nb):
    def residue(r, c):
        block(r, 0, QB)
        if nb > 1:
            def step(n, c2):
                block(r, n, 2 * QB)
                return c2
            lax.fori_loop(1, nb, step, 0, unroll=3)
        return c

    if dil == 1:
        residue(0, 0)
    else:
        lax.fori_loop(0, dil, residue, 0, unroll=4 if nb == 1 else 1)


def _dil_scores(qh, kb, slope_d, n_keys):
    i = lax.broadcasted_iota(jnp.int32, (QB, n_keys), 0)
    j = lax.broadcasted_iota(jnp.int32, (QB, n_keys), 1)
    delta = i + (n_keys - QB) - j
    valid = (delta >= 0) & (delta <= QB)
    sc = _nt(qh, kb) * SCALE - slope_d * delta.astype(F32)
    return jnp.where(valid, sc, NEG)


def _dil_slopes(g, dil):
    p = pl.program_id(1)
    sa = jnp.where(p == 0, ALIBI[4 * g] * dil, ALIBI[4 * g + 2] * dil).astype(F32)
    sb = jnp.where(p == 0, ALIBI[4 * g + 1] * dil, ALIBI[4 * g + 3] * dil).astype(F32)
    return sa, sb


def dil_fwd(projb, kv, g, bl, *, name):
    _, dil = DIL_GROUPS[g]
    t, wq = projb.shape
    wk = kv.shape[1]
    s = t // bl
    ln = s // dil
    nb = ln // QB
    gw = 4 * HEAD_DIM
    ck, co = wk // LANES, gw // LANES

    def body(q_ref, k_ref, v_ref, o_ref, lse_ref):
        sa, sb = _dil_slopes(g, dil)
        ma, mb = _lane_masks(QB)

        def block(r, n, n_keys):
            q0 = n * QB
            k0 = q0 - (n_keys - QB)
            q = q_ref[0, _dil_rows(r, q0, QB, dil), :].astype(BF)
            kb = k_ref[0, _dil_rows(r, k0, n_keys, dil), :].astype(BF)
            vb = v_ref[0, _dil_rows(r, k0, n_keys, dil), :].astype(BF)
            outs, lses = [], []
            for m, sl in ((ma, sa), (mb, sb)):
                qh = jnp.where(m, q, jnp.zeros_like(q))
                sc = _dil_scores(qh, kb, sl, n_keys)
                mx = jnp.max(sc, axis=-1, keepdims=True)
                p = jnp.exp(sc - mx)
                den = jnp.sum(p, axis=-1, keepdims=True)
                outs.append(_nn(p.astype(BF), vb) * (1.0 / den))
                lses.append(mx + jnp.log(den))
            o_ref[0, _dil_rows(r, q0, QB, dil), :] = jnp.where(ma, outs[0], outs[1])
            lse_ref[0, _dil_rows(r, q0, QB, dil), :] = jnp.where(ma, lses[0], lses[1])

        _dil_walk(block, dil, nb)

    colblk = lambda off: pl.BlockSpec((1, s, LANES), lambda b, p: (b, 0, off + p))
    o, lse = pl.pallas_call(
        body, name=name, grid=(bl, co),
        in_specs=[colblk(g * co), colblk(g * co), colblk(ck // 2 + g * co)],
        out_specs=[colblk(0), colblk(0)],
        out_shape=[jax.ShapeDtypeStruct((bl, s, gw), F32), jax.ShapeDtypeStruct((bl, s, gw), F32)],
        compiler_params=_params("parallel", "parallel"),
    )(projb.reshape(bl, s, wq), kv.reshape(bl, s, wk), kv.reshape(bl, s, wk))
    return o.reshape(t, gw), lse.reshape(t, gw)


def dil_bwd(projb, kv, lse, dog, dshift, g, bl, *, name):
    _, dil = DIL_GROUPS[g]
    t, wq = projb.shape
    wk = kv.shape[1]
    s = t // bl
    ln = s // dil
    nb = ln // QB
    gw = 4 * HEAD_DIM
    ck, co = wk // LANES, gw // LANES

    def body(q_ref, k_ref, v_ref, lse_ref, do_ref, sh_ref, dq_ref, dk_ref, dv_ref):
        sa, sb = _dil_slopes(g, dil)
        ma, mb = _lane_masks(QB)
        dk_ref[...] = jnp.zeros_like(dk_ref)
        dv_ref[...] = jnp.zeros_like(dv_ref)

        def block(r, n, n_keys):
            q0 = n * QB
            k0 = q0 - (n_keys - QB)
            qrows, krows = _dil_rows(r, q0, QB, dil), _dil_rows(r, k0, n_keys, dil)
            q = q_ref[0, qrows, :].astype(BF)
            do = do_ref[0, qrows, :].astype(BF)
            lse_b = lse_ref[0, qrows, :]
            sh_b = sh_ref[0, qrows, :]
            kb = k_ref[0, krows, :].astype(BF)
            vb = v_ref[0, krows, :].astype(BF)
            dqs = []
            for m, sl, c0 in ((ma, sa, 0), (mb, sb, HEAD_DIM)):
                qh = jnp.where(m, q, jnp.zeros_like(q))
                doh = jnp.where(m, do, jnp.zeros_like(do))
                sc = _dil_scores(qh, kb, sl, n_keys)
                p = jnp.exp(sc - lse_b[:, c0:c0 + 1])
                ds = p * (_nt(doh, vb) - sh_b[:, c0:c0 + 1])
                dss = (ds * SCALE).astype(BF)
                dk_ref[0, krows, :] += _tn(dss, qh)
                dv_ref[0, krows, :] += _tn(p.astype(BF), doh)
                dqs.append(_nn(dss, kb))
            dq_ref[0, qrows, :] = jnp.where(ma, dqs[0], dqs[1])

        _dil_walk(block, dil, nb)

    colblk = lambda off: pl.BlockSpec((1, s, LANES), lambda b, p: (b, 0, off + p))
    dq, dk, dv = pl.pallas_call(
        body, name=name, grid=(bl, co),
        in_specs=[colblk(g * co), colblk(g * co), colblk(ck // 2 + g * co), colblk(0), colblk(g * co), colblk(g * co)],
        out_specs=[colblk(0)] * 3,
        out_shape=[jax.ShapeDtypeStruct((bl, s, gw), F32)] * 3,
        compiler_params=_params("parallel", "parallel"),
    )(projb.reshape(bl, s, wq), kv.reshape(bl, s, wk), kv.reshape(bl, s, wk), lse.reshape(bl, s, gw),
      dog.reshape(bl, s, DIL_WIDTH), dshift.reshape(bl, s, DIL_WIDTH))
    return dq.reshape(t, gw), dk.reshape(t, gw), dv.reshape(t, gw)


def _group_weights(lses):
    mx = jnp.maximum(jnp.maximum(lses[0], lses[1]), lses[2])
    es = [jnp.exp(l - mx) for l in lses]
    inv = 1.0 / (es[0] + es[1] + es[2])
    return [e * inv for e in es]


def dil_combine_fwd(os_, lses, *, name):
    t, gw = os_[0].shape
    tr = _pick(t, (512, 256))

    def body(o0, o1, o2, l0, l1, l2, out_ref):
        al = _group_weights([l0[...], l1[...], l2[...]])
        for g, o_ref in enumerate((o0, o1, o2)):
            out_ref[:, g * gw:(g + 1) * gw] = (o_ref[...] * al[g]).astype(out_ref.dtype)

    blk = pl.BlockSpec((tr, gw), lambda i: (i, 0))
    return pl.pallas_call(
        body, name=name, grid=(t // tr,), in_specs=[blk] * 6,
        out_specs=pl.BlockSpec((tr, 3 * gw), lambda i: (i, 0)), out_shape=jax.ShapeDtypeStruct((t, 3 * gw), BF),
        compiler_params=_params("parallel"),
    )(*os_, *lses)


def dil_combine_bwd(os_, lses, dcat, *, name):
    t, gw = os_[0].shape
    tr = _pick(t, (512, 256))

    def head_sum(v):
        parts = []
        for c in range(gw // LANES):
            blk = v[:, c * LANES:(c + 1) * LANES]
            ma, _ = _lane_masks(tr)
            sa = jnp.sum(jnp.where(ma, blk, 0.0), axis=1, keepdims=True)
            sb = jnp.sum(blk, axis=1, keepdims=True) - sa
            parts.append(jnp.where(ma, sa, sb))
        return jnp.concatenate(parts, axis=1)

    def body(o0, o1, o2, l0, l1, l2, d_ref, dog_ref, sh_ref):
        al = _group_weights([l0[...], l1[...], l2[...]])
        dos = [d_ref[:, g * gw:(g + 1) * gw].astype(F32) for g in range(3)]
        dal = [head_sum(dos[g] * o_ref[...]) for g, o_ref in enumerate((o0, o1, o2))]
        mix = al[0] * dal[0] + al[1] * dal[1] + al[2] * dal[2]
        for g in range(3):
            dog_ref[:, g * gw:(g + 1) * gw] = (al[g] * dos[g]).astype(dog_ref.dtype)
            sh_ref[:, g * gw:(g + 1) * gw] = al[g] * mix

    blk = pl.BlockSpec((tr, gw), lambda i: (i, 0))
    wide = pl.BlockSpec((tr, 3 * gw), lambda i: (i, 0))
    return pl.pallas_call(
        body, name=name, grid=(t // tr,), in_specs=[blk] * 6 + [wide], out_specs=[wide, wide],
        out_shape=[jax.ShapeDtypeStruct((t, 3 * gw), F32), jax.ShapeDtypeStruct((t, 3 * gw), F32)],
        compiler_params=_params("parallel"),
    )(*os_, *lses, dcat)


def adamw(w, g1, g2, m, v, *, name):
    r, c = w.shape
    tr = r
    for cand in (256, 128, 64, 32, 16, 8):
        if r % cand == 0 and cand * c * 4 <= (1 << 20):
            tr = cand
            break
    two = g2 is not None

    def body(*refs):
        if two:
            w_ref, g1_ref, g2_ref, m_ref, v_ref, g_ref, d_ref, nm_ref, nv_ref = refs
            g = g1_ref[...] + g2_ref[...]
        else:
            w_ref, g1_ref, m_ref, v_ref, g_ref, d_ref, nm_ref, nv_ref = refs
            g = g1_ref[...]
        nm = ADAM_B1 * m_ref[...] + (1.0 - ADAM_B1) * g
        nv = ADAM_B2 * v_ref[...] + (1.0 - ADAM_B2) * (g * g)
        m_hat = nm / (1.0 - ADAM_B1 ** ADAM_STEP)
        v_hat = nv / (1.0 - ADAM_B2 ** ADAM_STEP)
        g_ref[...] = g
        d_ref[...] = -ADAM_LR * (m_hat / (jnp.sqrt(v_hat) + ADAM_EPS) + ADAM_WD * w_ref[...])
        nm_ref[...] = nm
        nv_ref[...] = nv

    blk = pl.BlockSpec((tr, c), lambda i: (i, 0))
    args = [w, g1] + ([g2] if two else []) + [m, v]
    return pl.pallas_call(
        body, name=name, grid=(r // tr,), in_specs=[blk] * len(args), out_specs=[blk] * 4,
        out_shape=[jax.ShapeDtypeStruct((r, c), F32)] * 4, compiler_params=_params("parallel"),
    )(*args)


def sum4(own, land, *, name):
    r, c = own.shape
    tr = _pick(r, (256, 128, 64))

    def body(o_ref, l_ref, s_ref):
        s_ref[...] = ((o_ref[...].astype(F32) + l_ref[0].astype(F32)) + l_ref[1].astype(F32)) + l_ref[2].astype(F32)

    return pl.pallas_call(
        body, name=name, grid=(r // tr,),
        in_specs=[pl.BlockSpec((tr, c), lambda i: (i, 0)), pl.BlockSpec((3, tr, c), lambda i: (0, i, 0))],
        out_specs=pl.BlockSpec((tr, c), lambda i: (i, 0)), out_shape=jax.ShapeDtypeStruct((r, c), F32),
        compiler_params=_params("parallel"),
    )(own, land)


ANY = pl.BlockSpec(memory_space=pl.ANY)


def _place():
    x, y, c = lax.axis_index("x"), lax.axis_index("y"), lax.axis_index("c")
    chips = [(1 - x, y), (x, 1 - y), (1 - x, 1 - y)]
    return x, y, c, chips


class Ride:
    def __init__(self, arrays, out_shape, copies):
        self.arrays, self.out_shape, self.copies = list(arrays), list(out_shape), copies
        n = 3 * len(self.arrays)
        self.scratch = [pltpu.SemaphoreType.DMA((n,)), pltpu.SemaphoreType.DMA((n,)),
                        pltpu.SemaphoreType.DMA((len(self.arrays),))]

    def split(self, refs):
        n, m = len(self.arrays), len(self.out_shape)
        return refs[:n], refs[n:n + m], refs[n + m:]

    def start(self, ins, outs, sems):
        local, sends, _ = self.copies(ins, outs, *sems)
        for cp in local + sends:
            cp.start()

    def wait(self, ins, outs, sems):
        local, sends, arrivals = self.copies(ins, outs, *sems)
        for cp in arrivals:
            cp.wait_recv()
        for cp in sends:
            cp.wait_send()
        for cp in local:
            cp.wait()

    def run(self, name):
        def body(*refs):
            ins, outs, sems = self.split(refs)
            self.start(ins, outs, sems)
            self.wait(ins, outs, sems)

        return pl.pallas_call(body, name=name, in_specs=[ANY] * len(self.arrays), out_specs=[ANY] * len(self.out_shape),
                              out_shape=self.out_shape, scratch_shapes=self.scratch)(*self.arrays)


def gather_ride(shards, axes):
    def copies(ins, outs, send_sems, recv_sems, local_sems):
        x, y, c, chips = _place()

        def slot(a, q):
            size = shards[a].shape[axes[a]]
            start = pl.multiple_of(q * size, size)
            return outs[a].at[pl.ds(start, size), :] if axes[a] == 0 else outs[a].at[:, pl.ds(start, size)]

        def remote(a, k, q):
            px, py = chips[k]
            return pltpu.make_async_remote_copy(src_ref=ins[a], dst_ref=slot(a, q), send_sem=send_sems.at[3 * a + k],
                                                recv_sem=recv_sems.at[3 * a + k], device_id=(px, py, c), device_id_type=MESH)

        me = 2 * x + y
        n = len(shards)
        local = [pltpu.make_async_copy(ins[a], slot(a, me), local_sems.at[a]) for a in range(n)]
        sends = [remote(a, k, me) for a in range(n) for k in range(3)]
        arrivals = [remote(a, k, 2 * chips[k][0] + chips[k][1]) for a in range(n) for k in range(3)]
        return local, sends, arrivals

    out_shape = []
    for a, sh in enumerate(shards):
        full = list(sh.shape)
        full[axes[a]] *= N_CHIPS
        out_shape.append(jax.ShapeDtypeStruct(tuple(full), sh.dtype))
    return Ride(shards, out_shape, copies)


def scatter_ride(grads):
    def copies(ins, outs, send_sems, recv_sems, local_sems):
        x, y, c, chips = _place()
        sends = [pltpu.make_async_remote_copy(src_ref=ins[a].at[2 * px + py], dst_ref=outs[a].at[k],
                                              send_sem=send_sems.at[3 * a + k], recv_sem=recv_sems.at[3 * a + k],
                                              device_id=(px, py, c), device_id_type=MESH)
                 for a in range(len(grads)) for k, (px, py) in enumerate(chips)]
        return [], sends, sends

    return Ride(grads, [jax.ShapeDtypeStruct((3,) + g.shape[1:], g.dtype) for g in grads], copies)


def swap_with_sibling(arrs, *, name):
    n = len(arrs)

    def body(*refs):
        ins, outs = refs[:n], refs[n:2 * n]
        send_sems, recv_sems = refs[2 * n:]
        x, y, c, _ = _place()
        sends = []
        for a in range(n):
            cp = pltpu.make_async_remote_copy(src_ref=ins[a], dst_ref=outs[a], send_sem=send_sems.at[a],
                                              recv_sem=recv_sems.at[a], device_id=(x, y, 1 - c), device_id_type=MESH)
            cp.start()
            sends.append(cp)
        for cp in sends:
            cp.wait_recv()
        for cp in sends:
            cp.wait_send()

    return pl.pallas_call(
        body, name=name, in_specs=[ANY] * n, out_specs=[ANY] * n,
        out_shape=[jax.ShapeDtypeStruct(a.shape, a.dtype) for a in arrs],
        scratch_shapes=[pltpu.SemaphoreType.DMA((n,)), pltpu.SemaphoreType.DMA((n,))],
    )(*arrs)


def all_reduce_small(v, *, name):
    rows = v.shape[0]

    def body(v_ref, o_ref, gath, send_sems, recv_sems):
        x, y, c, _ = _place()
        me = 4 * x + 2 * y + c
        gath[me] = v_ref[...]
        sends = []
        for msk in range(1, N_DEV):
            peer = (x ^ (msk >> 2), y ^ ((msk >> 1) & 1), c ^ (msk & 1))
            cp = pltpu.make_async_remote_copy(src_ref=v_ref, dst_ref=gath.at[me], send_sem=send_sems.at[msk - 1],
                                              recv_sem=recv_sems.at[msk - 1], device_id=peer, device_id_type=MESH)
            cp.start()
            sends.append(cp)
        for msk in range(1, N_DEV):
            pltpu.make_async_remote_copy(src_ref=v_ref, dst_ref=gath.at[me ^ msk], send_sem=send_sems.at[msk - 1],
                                         recv_sem=recv_sems.at[msk - 1], device_id=(x, y, c), device_id_type=MESH).wait_recv()
        for cp in sends:
            cp.wait_send()
        tot = gath[0]
        for q in range(1, N_DEV):
            tot = tot + gath[q]
        o_ref[...] = tot

    vm = pl.BlockSpec(memory_space=pltpu.VMEM)
    return pl.pallas_call(
        body, name=name, in_specs=[vm], out_specs=vm, out_shape=jax.ShapeDtypeStruct(v.shape, F32),
        scratch_shapes=[pltpu.VMEM((N_DEV, rows, LANES), F32), pltpu.SemaphoreType.DMA((N_DEV - 1,)),
                        pltpu.SemaphoreType.DMA((N_DEV - 1,))],
    )(v)


def _ffn_fwd(xin, gain, w_up, w_conv, w_down, bl, tag):
    n = rmsnorm_fwd(xin, gain, name=f"{tag}_ffn_norm")
    u = matmul(n, w_up, name=f"{tag}_ffn_up")
    h, ca, cg = conv_silu_fwd(u, w_conv, bl, name=f"{tag}_ffn_conv")
    xout = matmul(h, w_down, out_dtype=F32, add=xin, name=f"{tag}_ffn_down")
    return xout, (n, u, h, ca, cg)


def _ffn_bwd(dxout, dxout_b, xin, saved, gain, w_up, w_conv, w_down, bl, tag, shard):
    n, u, h, ca, cg = saved
    dh = matmul(dxout_b, w_down, tb=True, name=f"{tag}_ffn_down_dx")
    g_down = matmul(h, dxout_b, ta=True, name=f"{tag}_ffn_down_dw")
    dua, dug, g_conv = conv_silu_bwd(u, ca, cg, w_conv, dh, bl, name=f"{tag}_ffn_conv_bwd")
    dn = matmul(dua, w_up, a2=dug, tb=True, out_dtype=F32, name=f"{tag}_ffn_up_dx")
    g_up = matmul(n, dua, b2=dug, ta=True, shard_cols=shard, name=f"{tag}_ffn_up_dw")
    dxin, dxin_b, g_norm = rmsnorm_bwd(xin, gain, dn, dxout, name=f"{tag}_ffn_norm_bwd")
    return dxin, dxin_b, g_norm, g_up, g_conv, g_down


def _mem_kv(mem2, gain, w_kv, tag):
    mn = rmsnorm_fwd(mem2, gain, name=f"{tag}_mem_norm")
    return mn, matmul(mn, w_kv, name=f"{tag}_mem_kv")


def _mem_kv_bwd(mem2, mn, gain, w_kv, dkvm, tag):
    g_kv = matmul(mn, dkvm, ta=True, name=f"{tag}_mem_kv_dw")
    dmn = matmul(dkvm, w_kv, tb=True, out_dtype=F32, name=f"{tag}_mem_kv_dx")
    _, _, g_norm = rmsnorm_bwd(mem2, gain, dmn, None, name=f"{tag}_mem_norm_bwd")
    return g_norm, g_kv


def _shard_major(name, grad):
    if name in COL_SHARDED:
        return grad
    return grad.reshape(N_CHIPS, grad.shape[0] // N_CHIPS, grad.shape[1])


def local_step(x, mem, target, w, shard_of, late_gather=None, early_scatter=()):
    bl, s, d = x.shape
    t = bl * s
    x0 = x.reshape(t, d)
    mem2 = mem.reshape(bl * MEM_LEN, d)
    tgt = target.reshape(t, d)
    g = {}

    n1 = rmsnorm_fwd(x0, w["a_norm_attn"], name="a_attn_norm")
    proj_a = matmul(n1, w["a_w_in"], name="a_w_in")
    mn_a, kvm_a = _mem_kv(mem2, w["a_norm_mem"], w["a_w_mem_kv"], "a")
    o_sb, o_sb_f32, delivered = sb_fwd(proj_a, bl, name="a_sb_fwd", ride=late_gather[0] if late_gather else None)
    if late_gather:
        w = {**w, **dict(zip(late_gather[1], delivered))}
    o_mem_a = mem_fwd(proj_a, kvm_a, bl, name="a_mem_fwd")
    cat_a = jnp.concatenate([o_sb, o_mem_a], axis=1)
    x1 = matmul(cat_a, w["a_w_out"], out_dtype=F32, add=x0, name="a_w_out")
    x2, ffn_a = _ffn_fwd(x1, w["a_norm_ffn"], w["a_ffn_up"], w["a_ffn_conv"], w["a_ffn_down"], bl, "a")

    nkv = rmsnorm_fwd(x2, w["kv_norm"], name="kv_norm")
    kv = matmul(nkv, w["w_kv_shared"], out_dtype=F32, name="w_kv")
    n3 = rmsnorm_fwd(x2, w["b_norm_attn"], name="b_attn_norm")
    proj_b = matmul(n3, w["b_w_in"], out_dtype=F32, name="b_w_in")
    mn_b, kvm_b = _mem_kv(mem2, w["b_norm_mem"], w["b_w_mem_kv"], "b")
    dil = [dil_fwd(proj_b, kv, gi, bl, name=f"b_dil_fwd{gi}") for gi in range(3)]
    os_, lses = [o for o, _ in dil], [l for _, l in dil]
    o_dil = dil_combine_fwd(os_, lses, name="b_dil_combine")
    o_mem_b = mem_fwd(proj_b, kvm_b, bl, name="b_mem_fwd")
    cat_b = jnp.concatenate([o_dil, o_mem_b], axis=1)
    x3 = matmul(cat_b, w["b_w_out"], out_dtype=F32, add=x2, name="b_w_out")
    x4, ffn_b = _ffn_fwd(x3, w["b_norm_ffn"], w["b_ffn_up"], w["b_ffn_conv"], w["b_ffn_down"], bl, "b")

    dx4, dx4b, g["final_norm"], lossvec = final_loss(x4, w["final_norm"], tgt, name="final_loss")

    dx3, dx3b, g["b_norm_ffn"], g["b_ffn_up"], g["b_ffn_conv"], g["b_ffn_down"] = _ffn_bwd(
        dx4, dx4b, x3, ffn_b, w["b_norm_ffn"], w["b_ffn_up"], w["b_ffn_conv"], w["b_ffn_down"], bl, "b",
        shard_of["b_ffn_up"])
    dcat_b = matmul(dx3b, w["b_w_out"], tb=True, name="b_w_out_dx")
    g["b_w_out"] = matmul(cat_b, dx3b, ta=True, name="b_w_out_dw")
    dog, dshift = dil_combine_bwd(os_, lses, dcat_b, name="b_dil_combine_bwd")
    dqs, dks, dvs = [], [], []
    for gi in range(3):
        dq_g, dk_g, dv_g = dil_bwd(proj_b, kv, lses[gi], dog, dshift, gi, bl, name=f"b_dil_bwd{gi}")
        dqs.append(dq_g), dks.append(dk_g), dvs.append(dv_g)
    dq_mem_b, dkvm_b = mem_bwd(proj_b, kvm_b, dcat_b, bl, name="b_mem_bwd")
    g["b_norm_mem"], g["b_w_mem_kv"] = _mem_kv_bwd(mem2, mn_b, w["b_norm_mem"], w["b_w_mem_kv"], dkvm_b, "b")
    dproj_b = jnp.concatenate([dq_g.astype(BF) for dq_g in dqs] + [dq_mem_b], axis=1)
    dn3 = matmul(dproj_b, w["b_w_in"], tb=True, out_dtype=F32, name="b_w_in_dx")
    g["b_w_in"] = matmul(n3, dproj_b, ta=True, name="b_w_in_dw")
    dx2, _, g["b_norm_attn"] = rmsnorm_bwd(x2, w["b_norm_attn"], dn3, dx3, name="b_attn_norm_bwd")
    dkv = jnp.concatenate(dks + dvs, axis=1).astype(BF)
    dnkv = matmul(dkv, w["w_kv_shared"], tb=True, out_dtype=F32, name="w_kv_dx")
    g["w_kv_shared"] = matmul(nkv, dkv, ta=True, shard_cols=shard_of["w_kv_shared"], name="w_kv_dw")
    dx2, dx2b, g["kv_norm"] = rmsnorm_bwd(x2, w["kv_norm"], dnkv, dx2, name="kv_norm_bwd")

    dx1, dx1b, g["a_norm_ffn"], g["a_ffn_up"], g["a_ffn_conv"], g["a_ffn_down"] = _ffn_bwd(
        dx2, dx2b, x1, ffn_a, w["a_norm_ffn"], w["a_ffn_up"], w["a_ffn_conv"], w["a_ffn_down"], bl, "a",
        shard_of["a_ffn_up"])
    dcat_a = matmul(dx1b, w["a_w_out"], tb=True, name="a_w_out_dx")
    g["a_w_out"] = matmul(cat_a, dx1b, ta=True, name="a_w_out_dw")
    leaving = scatter_ride([_shard_major(n, g[n]) for n in early_scatter]) if early_scatter else None
    dq_sb, dk_sb, dv_sb, landed = sb_bwd(proj_a, o_sb_f32, dcat_a, bl, name="a_sb_bwd", ride=leaving)
    landed = dict(zip(early_scatter, landed))
    dq_mem_a, dkvm_a = mem_bwd(proj_a, kvm_a, dcat_a, bl, name="a_mem_bwd")
    g["a_norm_mem"], g["a_w_mem_kv"] = _mem_kv_bwd(mem2, mn_a, w["a_norm_mem"], w["a_w_mem_kv"], dkvm_a, "a")
    dproj_a = jnp.concatenate([dq_sb, dk_sb.astype(BF), dv_sb.astype(BF), dq_mem_a], axis=1)
    dn1 = matmul(dproj_a, w["a_w_in"], tb=True, out_dtype=F32, name="a_w_in_dx")
    g["a_w_in"] = matmul(n1, dproj_a, ta=True, shard_cols=shard_of["a_w_in"], name="a_w_in_dw")
    dx0, _, g["a_norm_attn"] = rmsnorm_bwd(x0, w["a_norm_attn"], dn1, dx1, name="a_attn_norm_bwd")
    return lossvec, dx0, g, landed


MATRICES = ("a_w_in", "a_w_out", "a_w_mem_kv", "a_ffn_up", "a_ffn_down", "w_kv_shared", "b_w_in", "b_w_out",
            "b_w_mem_kv", "b_ffn_up", "b_ffn_down")
COL_SHARDED = ("a_w_in", "a_ffn_up", "w_kv_shared", "b_ffn_up")
FIRST_NEEDED = ("a_w_in", "a_w_mem_kv")
SMALL_SHARDED = ("a_norm_attn", "a_norm_mem", "a_norm_ffn", "a_ffn_conv", "b_ffn_conv")
SMALL_REPLICATED = ("kv_norm", "b_norm_attn", "b_norm_mem", "b_norm_ffn", "final_norm")
WEIGHTS = ("a_norm_attn", "a_w_in", "a_w_out", "a_norm_mem", "a_w_mem_kv", "a_norm_ffn", "a_ffn_up", "a_ffn_conv",
           "a_ffn_down", "kv_norm", "w_kv_shared", "b_norm_attn", "b_w_in", "b_w_out", "b_norm_mem", "b_w_mem_kv",
           "b_norm_ffn", "b_ffn_up", "b_ffn_conv", "b_ffn_down", "final_norm")


def _two_d(a):
    if a.ndim == 1:
        return a.reshape(1, -1)
    return a.reshape(a.shape[-2], a.shape[-1])


def kernel(x, mem, a_norm_attn, a_w_in, a_w_out, a_norm_mem, a_w_mem_kv, a_norm_ffn, a_ffn_up, a_ffn_conv, a_ffn_down, kv_norm, w_kv_shared, b_norm_attn, b_w_in, b_w_out, b_norm_mem, b_w_mem_kv, b_norm_ffn, b_ffn_up, b_ffn_conv, b_ffn_down, final_norm, loss_target, m_a_norm_attn, m_a_w_in, m_a_w_out, m_a_norm_mem, m_a_w_mem_kv, m_a_norm_ffn, m_a_ffn_up, m_a_ffn_conv, m_a_ffn_down, m_kv_norm, m_w_kv_shared, m_b_norm_attn, m_b_w_in, m_b_w_out, m_b_norm_mem, m_b_w_mem_kv, m_b_norm_ffn, m_b_ffn_up, m_b_ffn_conv, m_b_ffn_down, m_final_norm, v_a_norm_attn, v_a_w_in, v_a_w_out, v_a_norm_mem, v_a_w_mem_kv, v_a_norm_ffn, v_a_ffn_up, v_a_ffn_conv, v_a_ffn_down, v_kv_norm, v_w_kv_shared, v_b_norm_attn, v_b_w_in, v_b_w_out, v_b_norm_mem, v_b_w_mem_kv, v_b_norm_ffn, v_b_ffn_up, v_b_ffn_conv, v_b_ffn_down, v_final_norm):
    given = dict(locals())
    wl = {n: _two_d(given[n]) for n in WEIGHTS}
    ml = {n: _two_d(given["m_" + n]) for n in WEIGHTS}
    vl = {n: _two_d(given["v_" + n]) for n in WEIGHTS}
    chip = 2 * lax.axis_index("x") + lax.axis_index("y")

    packed = jnp.concatenate([wl[n].reshape(-1, LANES) for n in SMALL_SHARDED], axis=0)
    axis_of = lambda n: 1 if n in COL_SHARDED else 0
    late = tuple(n for n in MATRICES if n not in FIRST_NEEDED)
    full = gather_ride([wl[n].astype(BF) for n in FIRST_NEEDED] + [packed],
                       [axis_of(n) for n in FIRST_NEEDED] + [0]).run("gather_first")
    w = dict(zip(FIRST_NEEDED, full[:-1]))
    late_gather = (gather_ride([wl[n].astype(BF) for n in late], [axis_of(n) for n in late]), late)
    rows = packed.shape[0]
    per_chip = full[-1].reshape(N_CHIPS, rows, LANES)
    r0 = 0
    for n in SMALL_SHARDED:
        nr = wl[n].size // LANES
        piece = per_chip[:, r0:r0 + nr].reshape(N_CHIPS, wl[n].shape[0], wl[n].shape[1])
        w[n] = jnp.concatenate([piece[q] for q in range(N_CHIPS)], axis=1)
        r0 += nr
    for n in SMALL_REPLICATED:
        w[n] = wl[n]

    shard_of = {n: wl[n].shape[1] for n in COL_SHARDED}
    lossvec, dx0, g, landed = local_step(x, mem, loss_target, w, shard_of, late_gather=late_gather, early_scatter=late)
    loss = lax.psum(0.5 * jnp.sum(lossvec) / x.shape[-1], ("x", "y", "c"))

    g4 = {n: _shard_major(n, g[n]) for n in MATRICES}
    landed.update(zip(FIRST_NEEDED, scatter_ride([g4[n] for n in FIRST_NEEDED]).run("scatter_last")))
    sums = [sum4(lax.dynamic_index_in_dim(g4[n], chip, 0, keepdims=False), landed[n], name=f"sum4_{n}")
            for n in MATRICES]
    theirs = swap_with_sibling(sums, name="swap_sums")
    out = {}
    for k, n in enumerate(MATRICES):
        out[n] = adamw(wl[n], sums[k], theirs[k], ml[n], vl[n], name=f"adamw_{n}")

    small = SMALL_SHARDED + SMALL_REPLICATED
    flat = jnp.concatenate([g[n].reshape(-1, LANES) for n in small], axis=0)
    tot = all_reduce_small(flat, name="all_reduce_small")
    r0 = 0
    for n in small:
        nr = g[n].size // LANES
        gn = tot[r0:r0 + nr].reshape(g[n].shape)
        r0 += nr
        if n in SMALL_SHARDED:
            gn = lax.dynamic_slice_in_dim(gn, chip * wl[n].shape[1], wl[n].shape[1], axis=1)
        out[n] = adamw(wl[n], gn, None, ml[n], vl[n], name=f"adamw_{n}")

    res = [loss, dx0.reshape(x.shape)]
    for slot in range(4):
        res += [out[n][slot].reshape(given[n].shape) for n in WEIGHTS]
    return tuple(res)
```

```python
import functools
import math

import jax
import jax.numpy as jnp
from jax import lax
from jax.experimental import pallas as pl
from jax.experimental.pallas import tpu as pltpu

F32 = jnp.float32
BF = jnp.bfloat16
MESH = pl.DeviceIdType.MESH

HEAD_DIM = 64
LANES = 128
SB_WIDTH = 12 * HEAD_DIM
MEM_WIDTH = 4 * HEAD_DIM
DIL_WIDTH = 12 * HEAD_DIM
MEM_LEN = 256
DIL_GROUPS = ((128, 1), (512, 4), (2048, 16))
QB = 128
EPS = 1e-6
SCALE = HEAD_DIM ** -0.5
NEG = -1e30
ALIBI = tuple(2.0 ** (-8.0 * i / 12) for i in range(1, 13))
N_CHIPS = 4
N_DEV = 8

ADAM_LR, ADAM_B1, ADAM_B2, ADAM_EPS, ADAM_WD, ADAM_STEP = 0.001, 0.9, 0.999, 1e-08, 0.01, 10

VMEM_LIMIT = 52 * 1024 * 1024
MATMUL_VMEM_BUDGET = 44 * 1024 * 1024
MATMUL_MAX_TN = 2816
HBM_BYTES_PER_S = 3.0e12
VMEM_BYTES_PER_S = 6.0e12
STEP_OVERHEAD_S = 0.4e-6


def _mo(v, m):
    return v if isinstance(v, int) else pl.multiple_of(v, m)


def _pick(n, prefs):
    for t in prefs:
        if n % t == 0:
            return t
    return n


def _params(*sem):
    return pltpu.CompilerParams(dimension_semantics=sem, vmem_limit_bytes=VMEM_LIMIT)


def matmul(a, b, *, ta=False, tb=False, out_dtype=BF, add=None, shard_cols=0, a2=None, b2=None, ride=None, name):
    m, k = (a.shape[1], a.shape[0]) if ta else a.shape
    n = b.shape[0] if tb else b.shape[1]
    if a2 is not None:
        assert not ta and a2.shape == a.shape
        k *= 2
    if b2 is not None:
        assert not tb and b2.shape == b.shape
        n *= 2
    k_piece = a.shape[1] if a2 is not None else k
    n_piece = b.shape[1] if b2 is not None else n
    out_bytes = jnp.dtype(out_dtype).itemsize
    pieces_a, pieces_b = (2 if a2 is not None else 1), (2 if b2 is not None else 1)
    if shard_cols:
        tn_choices = [c * shard_cols for c in (4, 2, 1) if c * shard_cols <= MATMUL_MAX_TN and c <= 4 // pieces_b]
    else:
        tn_choices = [c for c in (2816, 2560, 1536, 1408, 1024, 512, 256, 128) if n_piece % c == 0]

    def vmem_bytes(tm_, tn_, tk_):
        blocks = (tm_ * tk_ * a.dtype.itemsize * pieces_a + tk_ * tn_ * b.dtype.itemsize * pieces_b
                  + tm_ * tn_ * out_bytes + (tm_ * tn_ * 4 if add is not None else 0))
        return 2 * blocks + (tm_ * tn_ * 4 if k // tk_ > 1 else 0) + tm_ * tn_ * 4

    def cost(c):
        tm_, tn_, tk_ = c
        steps = (m // tm_) * (n // tn_) * (k // tk_)
        hbm = (m * k * a.dtype.itemsize * (n // tn_) + k * n * b.dtype.itemsize * (m // tm_)
               + m * n * (out_bytes + (4 if add is not None else 0)))
        return hbm / HBM_BYTES_PER_S + steps * STEP_OVERHEAD_S + m * n * 8 * (k // tk_ - 1) / VMEM_BYTES_PER_S

    tm, tn, tk = min(((tm_, tn_, tk_) for tm_ in (1024, 512, 1408, 256, 128) if m % tm_ == 0 for tn_ in tn_choices
                      for tk_ in (2816, 2560, 2048, 1536, 1024, 512, 256, 128) if k_piece % tk_ == 0
                      if vmem_bytes(tm_, tn_, tk_) <= MATMUL_VMEM_BUDGET), key=cost)
    ns = tn // shard_cols if shard_cols else 1
    nk = k // tk
    nk1 = nk // 2
    nj1 = (n // tn) // 2
    dims = (((0,) if ta else (1,), (1,) if tb else (0,)), ((), ()))
    n_in = 2 + (a2 is not None) + (b2 is not None) + (add is not None)
    grid = (m // tm, n // tn, nk)

    def body(*refs):
        own_in, (o_ref,), scr, ride_start, ride_wait = _ride_hooks(ride, refs, n_in, 1, int(nk > 1), grid)
        ride_start()
        ins = list(own_in)
        a_tile = ins.pop(0)[...]
        b_tile = ins.pop(0)[...]
        if a2 is not None:
            a_tile = jnp.where(pl.program_id(2) < nk1, a_tile, ins.pop(0)[...])
        if b2 is not None:
            b_tile = jnp.where(pl.program_id(1) < nj1, b_tile, ins.pop(0)[...])
        add_ref = ins.pop(0) if add is not None else None
        part = lax.dot_general(a_tile.astype(BF), b_tile.astype(BF), dims, preferred_element_type=F32)

        def finish(r):
            if add is not None:
                r = r + add_ref[...]
            if shard_cols:
                for c in range(ns):
                    o_ref[c] = r[:, c * shard_cols:(c + 1) * shard_cols].astype(o_ref.dtype)
            else:
                o_ref[...] = r.astype(o_ref.dtype)

        if nk == 1:
            finish(part)
        else:
            acc_ref = scr[0]
            kk = pl.program_id(2)

            @pl.when(kk == 0)
            def _():
                acc_ref[...] = part

            @pl.when(kk > 0)
            def _():
                acc_ref[...] += part

            @pl.when(kk == nk - 1)
            def _():
                finish(acc_ref[...])
        ride_wait()

    a_spec = pl.BlockSpec((tk, tm), lambda i, j, q: (q, i)) if ta else pl.BlockSpec((tm, tk), lambda i, j, q: (i, q))
    b_spec = pl.BlockSpec((tn, tk), lambda i, j, q: (j, q)) if tb else pl.BlockSpec((tk, tn), lambda i, j, q: (q, j))
    if a2 is not None:
        a_spec = pl.BlockSpec((tm, tk), lambda i, j, q: (i, jnp.minimum(q, nk1 - 1)))
    if b2 is not None:
        b_spec = pl.BlockSpec((tk, tn), lambda i, j, q: (jnp.where(j < nj1, q, 0), jnp.minimum(j, nj1 - 1)))
    in_specs = [a_spec, b_spec]
    args = [a, b]
    if a2 is not None:
        in_specs.append(pl.BlockSpec((tm, tk), lambda i, j, q: (i, jnp.maximum(q - nk1, 0))))
        args.append(a2)
    if b2 is not None:
        in_specs.append(pl.BlockSpec((tk, tn), lambda i, j, q: (jnp.where(j < nj1, 0, q), jnp.maximum(j - nj1, 0))))
        args.append(b2)
    if add is not None:
        in_specs.append(pl.BlockSpec((tm, tn), lambda i, j, q: (i, j)))
        args.append(add)
    if shard_cols:
        out_shape = jax.ShapeDtypeStruct((N_CHIPS, m, shard_cols), out_dtype)
        out_spec = pl.BlockSpec((ns, tm, shard_cols), lambda i, j, q: (j, i, 0))
    else:
        out_shape = jax.ShapeDtypeStruct((m, n), out_dtype)
        out_spec = pl.BlockSpec((tm, tn), lambda i, j, q: (i, j))
    own_scratch = [pltpu.VMEM((tm, tn), F32)] if nk > 1 else []
    if ride is None:
        return pl.pallas_call(
            body, name=name, grid=grid, in_specs=in_specs, out_specs=out_spec, out_shape=out_shape,
            scratch_shapes=own_scratch, compiler_params=_params("parallel", "parallel", "arbitrary"),
        )(*args)
    out, *delivered = pl.pallas_call(
        body, name=name, grid=grid, in_specs=in_specs + [ANY] * len(ride.arrays),
        out_specs=[out_spec] + [ANY] * len(ride.out_shape), out_shape=[out_shape] + ride.out_shape,
        scratch_shapes=own_scratch + ride.scratch, compiler_params=_params("arbitrary", "arbitrary", "arbitrary"),
    )(*args, *ride.arrays)
    return out, delivered


def rmsnorm_fwd(x, g, *, name):
    t, d = x.shape
    tr = _pick(t, (512, 256))

    def body(x_ref, g_ref, o_ref):
        xv = x_ref[...]
        r = lax.rsqrt(jnp.mean(xv * xv, axis=-1, keepdims=True) + EPS)
        o_ref[...] = ((xv * r) * g_ref[...]).astype(o_ref.dtype)

    return pl.pallas_call(
        body, name=name, grid=(t // tr,),
        in_specs=[pl.BlockSpec((tr, d), lambda i: (i, 0)), pl.BlockSpec((1, d), lambda i: (0, 0))],
        out_specs=pl.BlockSpec((tr, d), lambda i: (i, 0)), out_shape=jax.ShapeDtypeStruct((t, d), BF),
        compiler_params=_params("parallel"),
    )(x, g)


def rmsnorm_bwd(x, g, dn, dres, *, name):
    t, d = x.shape
    tr = _pick(t, (512, 256))
    want_dx = dres is not None

    def body(*refs):
        if want_dx:
            x_ref, g_ref, dn_ref, dres_ref, dx_ref, dxb_ref, dg_ref = refs
        else:
            x_ref, g_ref, dn_ref, dg_ref = refs

        @pl.when(pl.program_id(0) == 0)
        def _():
            dg_ref[...] = jnp.zeros_like(dg_ref)

        xv = x_ref[...]
        r = lax.rsqrt(jnp.mean(xv * xv, axis=-1, keepdims=True) + EPS)
        xn = xv * r
        dnv = dn_ref[...].astype(F32)
        dg_ref[...] += jnp.sum(dnv * xn, axis=0, keepdims=True)
        if want_dx:
            dyg = dnv * g_ref[...]
            cm = jnp.mean(dyg * xn, axis=-1, keepdims=True)
            dx = dres_ref[...] + r * (dyg - xn * cm)
            dx_ref[...] = dx
            dxb_ref[...] = dx.astype(BF)

    row = pl.BlockSpec((tr, d), lambda i: (i, 0))
    vec = pl.BlockSpec((1, d), lambda i: (0, 0))
    if want_dx:
        return pl.pallas_call(
            body, name=name, grid=(t // tr,), in_specs=[row, vec, row, row], out_specs=[row, row, vec],
            out_shape=[jax.ShapeDtypeStruct((t, d), F32), jax.ShapeDtypeStruct((t, d), BF), jax.ShapeDtypeStruct((1, d), F32)],
            compiler_params=_params("arbitrary"),
        )(x, g, dn, dres)
    return None, None, pl.pallas_call(
        body, name=name, grid=(t // tr,), in_specs=[row, vec, row], out_specs=vec,
        out_shape=jax.ShapeDtypeStruct((1, d), F32), compiler_params=_params("arbitrary"),
    )(x, g, dn)


def final_loss(x, g, target, *, name):
    t, d = x.shape
    tr = _pick(t, (512, 256))

    def body(x_ref, g_ref, t_ref, dx_ref, dxb_ref, dg_ref, lv_ref):
        @pl.when(pl.program_id(0) == 0)
        def _():
            dg_ref[...] = jnp.zeros_like(dg_ref)
            lv_ref[...] = jnp.zeros_like(lv_ref)

        xv = x_ref[...]
        r = lax.rsqrt(jnp.mean(xv * xv, axis=-1, keepdims=True) + EPS)
        xn = xv * r
        err = xn * g_ref[...] - t_ref[...]
        lv_ref[...] += jnp.sum(err * err, axis=0, keepdims=True)
        dy = err * (1.0 / d)
        dg_ref[...] += jnp.sum(dy * xn, axis=0, keepdims=True)
        dyg = dy * g_ref[...]
        cm = jnp.mean(dyg * xn, axis=-1, keepdims=True)
        dx = r * (dyg - xn * cm)
        dx_ref[...] = dx
        dxb_ref[...] = dx.astype(BF)

    row = pl.BlockSpec((tr, d), lambda i: (i, 0))
    vec = pl.BlockSpec((1, d), lambda i: (0, 0))
    return pl.pallas_call(
        body, name=name, grid=(t // tr,), in_specs=[row, vec, row], out_specs=[row, row, vec, vec],
        out_shape=[jax.ShapeDtypeStruct((t, d), F32), jax.ShapeDtypeStruct((t, d), BF), jax.ShapeDtypeStruct((1, d), F32),
                   jax.ShapeDtypeStruct((1, d), F32)],
        compiler_params=_params("arbitrary"),
    )(x, g, target)


CONV_ROWS = 256
HALO = 16


def _conv_taps(ext, w, rows):
    s0 = ext[HALO:HALO + rows]
    s1 = pltpu.roll(ext, 1, 0)[HALO:HALO + rows]
    s2 = pltpu.roll(ext, 2, 0)[HALO:HALO + rows]
    return (w[0:1] * s2 + w[1:2] * s1) + w[2:3] * s0, s0, s1, s2


def conv_silu_fwd(u, w, bl, *, name, ride=None):
    t, f2 = u.shape
    f = f2 // 2
    s = t // bl
    tc = _pick(f, (256, 128))
    nf = f // tc
    nch = s // CONV_ROWS

    def body(*refs):
        (ua_ref, ug_ref, wa_ref, wg_ref), (h_ref, ca_ref, cg_ref), _, ride_start, ride_wait = _ride_hooks(
            ride, refs, 4, 3, 0, (nf, bl))
        ride_start()
        wa, wg = wa_ref[...], wg_ref[...]

        def chunk(ci, carry):
            r0 = pl.multiple_of(ci * CONV_ROWS, CONV_ROWS)
            ps = pl.multiple_of(jnp.maximum(r0 - HALO, 0), HALO)
            keep = (ci > 0).astype(F32)
            rows = pl.ds(r0, CONV_ROWS)

            def conv(ref, wv):
                ext = jnp.concatenate([ref[0, pl.ds(ps, HALO), :].astype(F32) * keep, ref[0, rows, :].astype(F32)], axis=0)
                return _conv_taps(ext, wv, CONV_ROWS)[0]

            ca, cg = conv(ua_ref, wa), conv(ug_ref, wg)
            sg = pl.reciprocal(1.0 + jnp.exp(-cg), approx=True)
            h_ref[0, rows, :] = ((cg * sg) * ca).astype(h_ref.dtype)
            ca_ref[0, rows, :] = ca.astype(ca_ref.dtype)
            cg_ref[0, rows, :] = cg.astype(cg_ref.dtype)
            return carry

        lax.fori_loop(0, nch, chunk, 0)
        ride_wait()

    u3 = u.reshape(bl, s, f2)
    blk = lambda off: pl.BlockSpec((1, s, tc), lambda j, b: (b, 0, j + off))
    wblk = lambda off: pl.BlockSpec((3, tc), lambda j, b: (0, j + off))
    extra = ride.arrays if ride else []
    h, ca, cg, *delivered = pl.pallas_call(
        body, name=name, grid=(nf, bl), in_specs=[blk(0), blk(nf), wblk(0), wblk(nf)] + [ANY] * len(extra),
        out_specs=[blk(0)] * 3 + [ANY] * (len(ride.out_shape) if ride else 0),
        out_shape=[jax.ShapeDtypeStruct((bl, s, f), BF)] * 3 + (ride.out_shape if ride else []),
        scratch_shapes=ride.scratch if ride else [],
        compiler_params=_params(*(("arbitrary",) * 2 if ride else ("parallel",) * 2)),
    )(u3, u3, w, w, *extra)
    return h.reshape(t, f), ca.reshape(t, f), cg.reshape(t, f), delivered


def conv_silu_bwd(u, ca, cg, w, dh, bl, *, name):
    t, f2 = u.shape
    f = f2 // 2
    s = t // bl
    tc = _pick(f, (256, 128))
    nf = f // tc
    nch = s // CONV_ROWS
    ext_rows = CONV_ROWS + HALO

    def body(ua_ref, ug_ref, ca_ref, cg_ref, wa_ref, wg_ref, dh_ref, dua_ref, dug_ref, dwa_ref, dwg_ref):
        wa, wg = wa_ref[...], wg_ref[...]

        @pl.when(pl.program_id(1) == 0)
        def _():
            dwa_ref[...] = jnp.zeros_like(dwa_ref)
            dwg_ref[...] = jnp.zeros_like(dwg_ref)

        def chunk(ci, carry):
            r0 = pl.multiple_of(ci * CONV_ROWS, CONV_ROWS)
            ns = pl.multiple_of(jnp.minimum(r0 + CONV_ROWS, s - HALO), HALO)
            keep_n = (ci < nch - 1).astype(F32)
            rows = pl.ds(r0, CONV_ROWS)

            def ext_of(ref):
                return jnp.concatenate([ref[0, rows, :].astype(F32), ref[0, pl.ds(ns, HALO), :].astype(F32)], axis=0)

            dhe = jnp.concatenate([dh_ref[0, rows, :].astype(F32), dh_ref[0, pl.ds(ns, HALO), :].astype(F32) * keep_n],
                                  axis=0)
            cae, cge = ext_of(ca_ref), ext_of(cg_ref)
            sg = pl.reciprocal(1.0 + jnp.exp(-cge), approx=True)
            dca = dhe * (cge * sg)
            dcg = dhe * cae * (sg * (1.0 + cge * (1.0 - sg)))

            def back(dc, wv, u_ref, du_ref, dw_ref):
                d0 = dc[:CONV_ROWS]
                n1 = pltpu.roll(dc, ext_rows - 1, 0)[:CONV_ROWS]
                n2 = pltpu.roll(dc, ext_rows - 2, 0)[:CONV_ROWS]
                du_ref[0, rows, :] = ((wv[2:3] * d0 + wv[1:2] * n1) + wv[0:1] * n2).astype(du_ref.dtype)
                uc = u_ref[0, rows, :].astype(F32)
                for k, shifted in enumerate((n2, n1, d0)):
                    dw_ref[k:k + 1, :] += jnp.sum(shifted * uc, axis=0, keepdims=True)

            back(dca, wa, ua_ref, dua_ref, dwa_ref)
            back(dcg, wg, ug_ref, dug_ref, dwg_ref)
            return carry

        lax.fori_loop(0, nch, chunk, 0)

    u3 = u.reshape(bl, s, f2)
    blk = lambda off: pl.BlockSpec((1, s, tc), lambda j, b: (b, 0, j + off))
    wblk = lambda off: pl.BlockSpec((3, tc), lambda j, b: (0, j + off))
    dua, dug, dwa, dwg = pl.pallas_call(
        body, name=name, grid=(nf, bl), in_specs=[blk(0), blk(nf), blk(0), blk(0), wblk(0), wblk(nf), blk(0)],
        out_specs=[blk(0), blk(0), wblk(0), wblk(0)],
        out_shape=[jax.ShapeDtypeStruct((bl, s, f), BF), jax.ShapeDtypeStruct((bl, s, f), BF),
                   jax.ShapeDtypeStruct((3, f), F32), jax.ShapeDtypeStruct((3, f), F32)],
        compiler_params=_params("parallel", "arbitrary"),
    )(u3, u3, ca.reshape(bl, s, f), cg.reshape(bl, s, f), w, w, dh.reshape(bl, s, f))
    return dua.reshape(t, f), dug.reshape(t, f), jnp.concatenate([dwa, dwg], axis=1)


def _lane_masks(rows):
    lane = lax.broadcasted_iota(jnp.int32, (rows, LANES), 1)
    return lane < HEAD_DIM, lane >= HEAD_DIM


def _nt(a, b):
    return lax.dot_general(a, b, (((1,), (1,)), ((), ())), preferred_element_type=F32)


def _tn(a, b):
    return lax.dot_general(a, b, (((0,), (0,)), ((), ())), preferred_element_type=F32)


def _nn(a, b):
    return jnp.dot(a, b, preferred_element_type=F32)


def _suffix_ones():
    j = lax.broadcasted_iota(jnp.int32, (2 * QB, QB), 0) % QB
    s = lax.broadcasted_iota(jnp.int32, (2 * QB, QB), 1)
    return (j >= s).astype(BF)


def _softplus(z):
    return jnp.maximum(z, 0.0) + jnp.log(1.0 + jnp.exp(-jnp.abs(z)))


SB_BLOCK = 256
SB_DEAD = -100.0


SB_STRIP = 32
SB_SUB = SB_BLOCK // QB


def _sb_scratch(backward):
    blk = (2, SB_BLOCK, SB_BLOCK)
    per_head = pltpu.VMEM((2, SB_BLOCK, LANES), F32)
    scr = [pltpu.VMEM((2 * QB, QB), BF),
           pltpu.VMEM(blk, F32),
           pltpu.VMEM((2, SB_SUB) + blk[1:], BF),
           pltpu.VMEM(blk, F32),
           pltpu.VMEM(blk, BF),
           per_head]
    if not backward:
        return scr + [per_head]
    return scr + [pltpu.VMEM(blk, F32),
                  pltpu.VMEM(blk, F32),
                  pltpu.VMEM(blk, F32),
                  pltpu.VMEM(blk, BF),
                  per_head,
                  per_head,
                  per_head]


def _sb_strips(fn):
    def step(r, c):
        fn(pl.ds(pl.multiple_of(r * SB_STRIP, SB_STRIP), SB_STRIP), r)
        return c
    lax.fori_loop(0, SB_BLOCK // SB_STRIP, step, 0, unroll=True)


def _sb_causal(thr, r):
    rel = (lax.broadcasted_iota(jnp.int32, (SB_STRIP, SB_BLOCK), 1)
           - lax.broadcasted_iota(jnp.int32, (SB_STRIP, SB_BLOCK), 0))
    return rel < thr + r * SB_STRIP


def _sb_split_store(sp_scr, rows, v):
    hi = v.astype(BF)
    lo = (v - hi.astype(F32)).astype(BF)
    for u in range(SB_SUB):
        sp_scr[u, rows, 0:QB] = hi[:, u * QB:(u + 1) * QB]
        sp_scr[u, rows, QB:2 * QB] = lo[:, u * QB:(u + 1) * QB]


def _sb_suffix_sums(uu_scr, sp_scr, out_scr):
    for u in range(SB_SUB):
        out_scr[:, u * QB:(u + 1) * QB] = _nn(sp_scr[u], uu_scr[...])


def _sb_fold(sums, off):
    offs = [None] * SB_SUB
    for u in reversed(range(SB_SUB)):
        offs[u] = off
        off = off + jnp.broadcast_to(sums[:, u * QB:u * QB + 1], (SB_STRIP, LANES))
    return offs, off


def _sb_first(thr, z_scr, sp_scr, ls_scr=None):
    def strip(rows, r):
        ls = jnp.where(_sb_causal(thr, r), -_softplus(z_scr[rows, :]), 0.0)
        if ls_scr is not None:
            ls_scr[rows, :] = ls
        _sb_split_store(sp_scr, rows, ls)

    _sb_strips(strip)


def _sb_second(thr, z_scr, cin_scr, w_scr, carry_scr):
    def strip(rows, r):
        z, cin = z_scr[rows, :], cin_scr[rows, :]
        offs, carry_scr[rows, :] = _sb_fold(cin, carry_scr[rows, :])
        e = jnp.concatenate([z[:, u * QB:(u + 1) * QB] + cin[:, u * QB:(u + 1) * QB] + offs[u] for u in range(SB_SUB)],
                            axis=1)
        w_scr[rows, :] = jnp.where(_sb_causal(thr, r), jnp.exp(e), 0.0).astype(BF)

    _sb_strips(strip)


def _sb_third(w_scr, dw_scr, da_scr, sp_scr):
    def strip(rows, r):
        da = w_scr[rows, :].astype(F32) * dw_scr[rows, :]
        da_scr[rows, :] = da
        _sb_split_store(sp_scr, rows, da)

    _sb_strips(strip)


def _sb_fourth(thr, z_scr, ls_scr, da_scr, sin_scr, dz_scr, carry_s_scr, dsum_scr):
    def strip(rows, r):
        da, sin = da_scr[rows, :], sin_scr[rows, :]
        offs, carry_s_scr[rows, :] = _sb_fold(sin, carry_s_scr[rows, :])
        dsum = dsum_scr[rows, :]
        pre = jnp.concatenate([dsum - (sin[:, u * QB:(u + 1) * QB] - da[:, u * QB:(u + 1) * QB] + offs[u])
                               for u in range(SB_SUB)], axis=1)
        sig = jnp.exp(z_scr[rows, :] + ls_scr[rows, :])
        dz_scr[rows, :] = jnp.where(_sb_causal(thr, r), da - sig * pre, 0.0).astype(BF)

    _sb_strips(strip)


def _sb_alive(carry_scr):
    return (jnp.max(carry_scr[...]) > SB_DEAD).astype(jnp.int32)


def _ride_hooks(ride, refs, n_in, n_out, n_scratch, grid):
    if ride is None:
        return refs[:n_in], refs[n_in:n_in + n_out], refs[n_in + n_out:], lambda: None, lambda: None
    ni, no = len(ride.arrays), len(ride.out_shape)
    own_in, rin = refs[:n_in], refs[n_in:n_in + ni]
    own_out, rout = refs[n_in + ni:n_in + ni + n_out], refs[n_in + ni + n_out:n_in + ni + n_out + no]
    rest = refs[n_in + ni + n_out + no:]
    own_scr, sems = rest[:n_scratch], rest[n_scratch:]
    ids = [pl.program_id(a) for a in range(len(grid))]

    def start():
        first = functools.reduce(lambda u, v: u & v, [i == 0 for i in ids])
        pl.when(first)(lambda: ride.start(rin, rout, sems))

    def wait():
        last = functools.reduce(lambda u, v: u & v, [i == n - 1 for i, n in zip(ids, grid)])
        pl.when(last)(lambda: ride.wait(rin, rout, sems))

    return own_in, own_out, own_scr, start, wait


def sb_fwd(proj, bl, *, name, ride=None):
    t, width = proj.shape
    s = t // bl
    npair = SB_WIDTH // LANES
    nq = s // SB_BLOCK
    grid = (bl, npair, nq)

    own_scratch = _sb_scratch(backward=False)

    def body(*refs):
        (q_ref, k_ref, v_ref), (o_ref, of_ref), scr, ride_start, ride_wait = _ride_hooks(
            ride, refs, 3, 2, len(own_scratch), grid)
        uu_scr, z_scr, sp_scr, cin_scr, w_scr, carry_scr, acc_scr = scr
        ride_start()
        i = pl.program_id(2)
        uu_scr[...] = _suffix_ones()
        carry_scr[...] = jnp.zeros_like(carry_scr)
        acc_scr[...] = jnp.zeros_like(acc_scr)
        qs = (q_ref[0].astype(F32) * SCALE).astype(BF)
        masks = _lane_masks(SB_BLOCK)

        z, sp, cin, w, car, acc = ([r.at[h] for h in range(2)] for r in (z_scr, sp_scr, cin_scr, w_scr, carry_scr, acc_scr))

        def live(c):
            start = pl.multiple_of(c * SB_BLOCK, SB_BLOCK)
            thr = (i - c) * SB_BLOCK
            kb, vb = k_ref[0, pl.ds(start, SB_BLOCK), :], v_ref[0, pl.ds(start, SB_BLOCK), :]
            for h in range(2):
                z[h][...] = _nt(jnp.where(masks[h], qs, jnp.zeros_like(qs)), kb)
            _sb_first(thr, z[0], sp[0])
            _sb_suffix_sums(uu_scr, sp[0], cin[0])
            _sb_first(thr, z[1], sp[1])
            _sb_suffix_sums(uu_scr, sp[1], cin[1])
            _sb_second(thr, z[0], cin[0], w[0], car[0])
            acc[0][...] += _nn(w[0][...], vb)
            _sb_second(thr, z[1], cin[1], w[1], car[1])
            acc[1][...] += _nn(w[1][...], vb)
            return _sb_alive(carry_scr)

        def step(n, alive):
            return lax.cond(alive > 0, lambda: live(i - n), lambda: alive)

        lax.fori_loop(0, i + 1, step, jnp.int32(1))
        both = jnp.where(masks[0], acc_scr[0], acc_scr[1])
        o_ref[0] = both.astype(o_ref.dtype)
        of_ref[0] = both
        ride_wait()

    p3 = proj.reshape(bl, s, width)
    qblk = pl.BlockSpec((1, SB_BLOCK, LANES), lambda b, p, i: (b, i, p))
    extra = ride.arrays if ride else []
    o, of, *delivered = pl.pallas_call(
        body, name=name, grid=grid,
        in_specs=[qblk, pl.BlockSpec((1, s, LANES), lambda b, p, i: (b, 0, npair + p)),
                  pl.BlockSpec((1, s, LANES), lambda b, p, i: (b, 0, 2 * npair + p))] + [ANY] * len(extra),
        out_specs=[qblk, qblk] + [ANY] * (len(ride.out_shape) if ride else 0),
        out_shape=[jax.ShapeDtypeStruct((bl, s, SB_WIDTH), BF), jax.ShapeDtypeStruct((bl, s, SB_WIDTH), F32)]
        + (ride.out_shape if ride else []),
        scratch_shapes=own_scratch + (ride.scratch if ride else []),
        compiler_params=_params("arbitrary", "arbitrary", "arbitrary"),
    )(p3, p3, p3, *extra)
    return o.reshape(t, SB_WIDTH), of.reshape(t, SB_WIDTH), delivered


def sb_bwd(proj, o, dcat, bl, *, name, ride=None):
    t, width = proj.shape
    s = t // bl
    npair = SB_WIDTH // LANES
    nq = s // SB_BLOCK
    grid = (bl, npair, nq)
    own_scratch = _sb_scratch(backward=True)

    def body(*refs):
        (q_ref, k_ref, v_ref, o_ref, do_ref), (dq_ref, dk_ref, dv_ref), scr, ride_start, ride_wait = _ride_hooks(
            ride, refs, 5, 3, len(own_scratch), grid)
        (uu_scr, z_scr, sp_scr, cin_scr, w_scr, carry_scr,
         ls_scr, dw_scr, da_scr, dz_scr, carry_s_scr, dsum_scr, dq_scr) = scr
        ride_start()
        i = pl.program_id(2)

        @pl.when(i == 0)
        def _():
            dk_ref[...] = jnp.zeros_like(dk_ref)
            dv_ref[...] = jnp.zeros_like(dv_ref)

        uu_scr[...] = _suffix_ones()
        for ref in (carry_scr, carry_s_scr, dq_scr):
            ref[...] = jnp.zeros_like(ref)
        masks = _lane_masks(SB_BLOCK)
        qs = (q_ref[0].astype(F32) * SCALE).astype(BF)
        do = do_ref[0]
        prod = do.astype(F32) * o_ref[0]
        for h in range(2):
            dsum_scr[h] = jnp.broadcast_to(jnp.sum(jnp.where(masks[h], prod, 0.0), axis=1, keepdims=True),
                                           (SB_BLOCK, LANES))

        z, sp, cin, w, car, ls, dw, da, dz, cars, dsum, dq = (
            [r.at[h] for h in range(2)] for r in (z_scr, sp_scr, cin_scr, w_scr, carry_scr, ls_scr, dw_scr, da_scr,
                                                  dz_scr, carry_s_scr, dsum_scr, dq_scr))

        def live(c):
            start = pl.multiple_of(c * SB_BLOCK, SB_BLOCK)
            thr = (i - c) * SB_BLOCK
            keys = pl.ds(start, SB_BLOCK)
            kb, vb = k_ref[0, keys, :], v_ref[0, keys, :]
            qh = [jnp.where(masks[h], qs, jnp.zeros_like(qs)) for h in range(2)]
            doh = [jnp.where(masks[h], do, jnp.zeros_like(do)) for h in range(2)]
            for h in range(2):
                z[h][...] = _nt(qh[h], kb)
                dw[h][...] = _nt(doh[h], vb)

            def weights(h):
                _sb_second(thr, z[h], cin[h], w[h], car[h])
                _sb_third(w[h], dw[h], da[h], sp[h])
                _sb_suffix_sums(uu_scr, sp[h], cin[h])

            def grads(h):
                _sb_fourth(thr, z[h], ls[h], da[h], cin[h], dz[h], cars[h], dsum[h])
                dk_ref[0, keys, :] += _tn(dz[h][...], qh[h])
                dv_ref[0, keys, :] += _tn(w[h][...], doh[h])
                dq[h][...] += _nn(dz[h][...], kb)

            _sb_first(thr, z[0], sp[0], ls[0])
            _sb_suffix_sums(uu_scr, sp[0], cin[0])
            _sb_first(thr, z[1], sp[1], ls[1])
            _sb_suffix_sums(uu_scr, sp[1], cin[1])
            weights(0)
            weights(1)
            grads(0)
            grads(1)
            return _sb_alive(carry_scr)

        def step(n, alive):
            return lax.cond(alive > 0, lambda: live(i - n), lambda: alive)

        lax.fori_loop(0, i + 1, step, jnp.int32(1))
        dq_ref[0] = (jnp.where(masks[0], dq_scr[0], dq_scr[1]) * SCALE).astype(dq_ref.dtype)
        ride_wait()

    p3 = proj.reshape(bl, s, width)
    o3 = o.reshape(bl, s, SB_WIDTH)
    d3 = dcat.reshape(bl, s, dcat.shape[1])
    qblk = pl.BlockSpec((1, SB_BLOCK, LANES), lambda b, p, i: (b, i, p))
    full = pl.BlockSpec((1, s, LANES), lambda b, p, i: (b, 0, p))
    extra = ride.arrays if ride else []
    dq, dk, dv, *delivered = pl.pallas_call(
        body, name=name, grid=grid,
        in_specs=[qblk, pl.BlockSpec((1, s, LANES), lambda b, p, i: (b, 0, npair + p)),
                  pl.BlockSpec((1, s, LANES), lambda b, p, i: (b, 0, 2 * npair + p)), qblk, qblk] + [ANY] * len(extra),
        out_specs=[qblk, full, full] + [ANY] * (len(ride.out_shape) if ride else 0),
        out_shape=[jax.ShapeDtypeStruct((bl, s, SB_WIDTH), BF), jax.ShapeDtypeStruct((bl, s, SB_WIDTH), F32),
                   jax.ShapeDtypeStruct((bl, s, SB_WIDTH), F32)] + (ride.out_shape if ride else []),
        scratch_shapes=own_scratch + (ride.scratch if ride else []),
        compiler_params=_params("arbitrary", "arbitrary", "arbitrary"),
    )(p3, p3, p3, o3, d3, *extra)
    return dq.reshape(t, SB_WIDTH), dk.reshape(t, SB_WIDTH), dv.reshape(t, SB_WIDTH), delivered


MEM_TQ = 512


def mem_fwd(proj, kvm, bl, *, name):
    t, width = proj.shape
    s = t // bl
    qoff = (width - MEM_WIDTH) // LANES
    npair = MEM_WIDTH // LANES

    def body(q_ref, k_ref, v_ref, o_ref):
        ma, mb = _lane_masks(MEM_TQ)
        q = q_ref[...].astype(BF)
        outs = []
        for m in (ma, mb):
            qh = jnp.where(m, q, jnp.zeros_like(q))
            sc = _nt(qh, k_ref[...]) * SCALE
            p = jnp.exp(sc - jnp.max(sc, axis=-1, keepdims=True))
            p = p * (1.0 / jnp.sum(p, axis=-1, keepdims=True))
            outs.append(_nn(p.astype(BF), v_ref[...]))
        o_ref[...] = jnp.where(ma, outs[0], outs[1]).astype(o_ref.dtype)

    nt = s // MEM_TQ
    return pl.pallas_call(
        body, name=name, grid=(bl, npair, nt),
        in_specs=[pl.BlockSpec((MEM_TQ, LANES), lambda b, p, i: (b * nt + i, qoff + p)),
                  pl.BlockSpec((MEM_LEN, LANES), lambda b, p, i: (b, p)),
                  pl.BlockSpec((MEM_LEN, LANES), lambda b, p, i: (b, npair + p))],
        out_specs=pl.BlockSpec((MEM_TQ, LANES), lambda b, p, i: (b * nt + i, p)),
        out_shape=jax.ShapeDtypeStruct((t, MEM_WIDTH), BF),
        compiler_params=_params("parallel", "parallel", "parallel"),
    )(proj, kvm, kvm)


def mem_bwd(proj, kvm, dcat, bl, *, name):
    t, width = proj.shape
    s = t // bl
    qoff = (width - MEM_WIDTH) // LANES
    doff = (dcat.shape[1] - MEM_WIDTH) // LANES
    npair = MEM_WIDTH // LANES
    nt = s // MEM_TQ

    def body(q_ref, k_ref, v_ref, do_ref, dq_ref, dk_ref, dv_ref):
        @pl.when(pl.program_id(2) == 0)
        def _():
            dk_ref[...] = jnp.zeros_like(dk_ref)
            dv_ref[...] = jnp.zeros_like(dv_ref)

        ma, mb = _lane_masks(MEM_TQ)
        q = q_ref[...].astype(BF)
        do = do_ref[...]
        kb, vb = k_ref[...], v_ref[...]
        dqs = []
        for m in (ma, mb):
            qh = jnp.where(m, q, jnp.zeros_like(q))
            doh = jnp.where(m, do, jnp.zeros_like(do))
            sc = _nt(qh, kb) * SCALE
            p = jnp.exp(sc - jnp.max(sc, axis=-1, keepdims=True))
            p = p * (1.0 / jnp.sum(p, axis=-1, keepdims=True))
            dp = _nt(doh, vb)
            ds = p * (dp - jnp.sum(p * dp, axis=-1, keepdims=True))
            dss = (ds * SCALE).astype(BF)
            dk_ref[...] += _tn(dss, qh)
            dv_ref[...] += _tn(p.astype(BF), doh)
            dqs.append(_nn(dss, kb))
        dq_ref[...] = jnp.where(ma, dqs[0], dqs[1]).astype(dq_ref.dtype)

    kblk = pl.BlockSpec((MEM_LEN, LANES), lambda b, p, i: (b, p))
    dq, dk, dv = pl.pallas_call(
        body, name=name, grid=(bl, npair, nt),
        in_specs=[pl.BlockSpec((MEM_TQ, LANES), lambda b, p, i: (b * nt + i, qoff + p)), kblk,
                  pl.BlockSpec((MEM_LEN, LANES), lambda b, p, i: (b, npair + p)),
                  pl.BlockSpec((MEM_TQ, LANES), lambda b, p, i: (b * nt + i, doff + p))],
        out_specs=[pl.BlockSpec((MEM_TQ, LANES), lambda b, p, i: (b * nt + i, p)), kblk, kblk],
        out_shape=[jax.ShapeDtypeStruct((t, MEM_WIDTH), BF), jax.ShapeDtypeStruct((bl * MEM_LEN, MEM_WIDTH), F32),
                   jax.ShapeDtypeStruct((bl * MEM_LEN, MEM_WIDTH), F32)],
        compiler_params=_params("parallel", "parallel", "arbitrary"),
    )(proj, kvm, kvm, dcat)
    return dq, jnp.concatenate([dk, dv], axis=1).astype(BF)


def _dil_rows(r, u0, size, dil):
    if dil == 1:
        return pl.ds(_mo(u0, QB), size)
    return pl.ds(u0 * dil + r, size, stride=dil)


def _dil_walk(block, dil, nb):
    def residue(r, c):
        block(r, 0, QB)
        if nb > 1:
            def step(n, c2):
                block(r, n, 2 * QB)
                return c2
            lax.fori_loop(1, nb, step, 0, unroll=3)
        return c

    if dil == 1:
        residue(0, 0)
    else:
        lax.fori_loop(0, dil, residue, 0, unroll=4 if nb == 1 else 1)


def _dil_scores(qh, kb, slope_d, n_keys):
    i = lax.broadcasted_iota(jnp.int32, (QB, n_keys), 0)
    j = lax.broadcasted_iota(jnp.int32, (QB, n_keys), 1)
    delta = i + (n_keys - QB) - j
    valid = (delta >= 0) & (delta <= QB)
    sc = _nt(qh, kb) * SCALE - slope_d * delta.astype(F32)
    return jnp.where(valid, sc, NEG)


def _dil_slopes(g, dil):
    p = pl.program_id(1)
    sa = jnp.where(p == 0, ALIBI[4 * g] * dil, ALIBI[4 * g + 2] * dil).astype(F32)
    sb = jnp.where(p == 0, ALIBI[4 * g + 1] * dil, ALIBI[4 * g + 3] * dil).astype(F32)
    return sa, sb


def dil_fwd(projb, kv, g, bl, *, name):
    _, dil = DIL_GROUPS[g]
    t, wq = projb.shape
    wk = kv.shape[1]
    s = t // bl
    ln = s // dil
    nb = ln // QB
    gw = 4 * HEAD_DIM
    ck, co = wk // LANES, gw // LANES

    def body(q_ref, k_ref, v_ref, o_ref, lse_ref):
        sa, sb = _dil_slopes(g, dil)
        ma, mb = _lane_masks(QB)

        def block(r, n, n_keys):
            q0 = n * QB
            k0 = q0 - (n_keys - QB)
            q = q_ref[0, _dil_rows(r, q0, QB, dil), :].astype(BF)
            kb = k_ref[0, _dil_rows(r, k0, n_keys, dil), :].astype(BF)
            vb = v_ref[0, _dil_rows(r, k0, n_keys, dil), :].astype(BF)
            outs, lses = [], []
            for m, sl in ((ma, sa), (mb, sb)):
                qh = jnp.where(m, q, jnp.zeros_like(q))
                sc = _dil_scores(qh, kb, sl, n_keys)
                mx = jnp.max(sc, axis=-1, keepdims=True)
                p = jnp.exp(sc - mx)
                den = jnp.sum(p, axis=-1, keepdims=True)
                outs.append(_nn(p.astype(BF), vb) * (1.0 / den))
                lses.append(mx + jnp.log(den))
            o_ref[0, _dil_rows(r, q0, QB, dil), :] = jnp.where(ma, outs[0], outs[1])
            lse_ref[0, _dil_rows(r, q0, QB, dil), :] = jnp.where(ma, lses[0], lses[1])

        _dil_walk(block, dil, nb)

    colblk = lambda off: pl.BlockSpec((1, s, LANES), lambda b, p: (b, 0, off + p))
    o, lse = pl.pallas_call(
        body, name=name, grid=(bl, co),
        in_specs=[colblk(g * co), colblk(g * co), colblk(ck // 2 + g * co)],
        out_specs=[colblk(0), colblk(0)],
        out_shape=[jax.ShapeDtypeStruct((bl, s, gw), F32), jax.ShapeDtypeStruct((bl, s, gw), F32)],
        compiler_params=_params("parallel", "parallel"),
    )(projb.reshape(bl, s, wq), kv.reshape(bl, s, wk), kv.reshape(bl, s, wk))
    return o.reshape(t, gw), lse.reshape(t, gw)


def dil_bwd(projb, kv, lse, dog, dshift, g, bl, *, name):
    _, dil = DIL_GROUPS[g]
    t, wq = projb.shape
    wk = kv.shape[1]
    s = t // bl
    ln = s // dil
    nb = ln // QB
    gw = 4 * HEAD_DIM
    ck, co = wk // LANES, gw // LANES

    def body(q_ref, k_ref, v_ref, lse_ref, do_ref, sh_ref, dq_ref, dk_ref, dv_ref):
        sa, sb = _dil_slopes(g, dil)
        ma, mb = _lane_masks(QB)
        dk_ref[...] = jnp.zeros_like(dk_ref)
        dv_ref[...] = jnp.zeros_like(dv_ref)

        def block(r, n, n_keys):
            q0 = n * QB
            k0 = q0 - (n_keys - QB)
            qrows, krows = _dil_rows(r, q0, QB, dil), _dil_rows(r, k0, n_keys, dil)
            q = q_ref[0, qrows, :].astype(BF)
            do = do_ref[0, qrows, :].astype(BF)
            lse_b = lse_ref[0, qrows, :]
            sh_b = sh_ref[0, qrows, :]
            kb = k_ref[0, krows, :].astype(BF)
            vb = v_ref[0, krows, :].astype(BF)
            dqs = []
            for m, sl, c0 in ((ma, sa, 0), (mb, sb, HEAD_DIM)):
                qh = jnp.where(m, q, jnp.zeros_like(q))
                doh = jnp.where(m, do, jnp.zeros_like(do))
                sc = _dil_scores(qh, kb, sl, n_keys)
                p = jnp.exp(sc - lse_b[:, c0:c0 + 1])
                ds = p * (_nt(doh, vb) - sh_b[:, c0:c0 + 1])
                dss = (ds * SCALE).astype(BF)
                dk_ref[0, krows, :] += _tn(dss, qh)
                dv_ref[0, krows, :] += _tn(p.astype(BF), doh)
                dqs.append(_nn(dss, kb))
            dq_ref[0, qrows, :] = jnp.where(ma, dqs[0], dqs[1])

        _dil_walk(block, dil, nb)

    colblk = lambda off: pl.BlockSpec((1, s, LANES), lambda b, p: (b, 0, off + p))
    dq, dk, dv = pl.pallas_call(
        body, name=name, grid=(bl, co),
        in_specs=[colblk(g * co), colblk(g * co), colblk(ck // 2 + g * co), colblk(0), colblk(g * co), colblk(g * co)],
        out_specs=[colblk(0)] * 3,
        out_shape=[jax.ShapeDtypeStruct((bl, s, gw), F32)] * 3,
        compiler_params=_params("parallel", "parallel"),
    )(projb.reshape(bl, s, wq), kv.reshape(bl, s, wk), kv.reshape(bl, s, wk), lse.reshape(bl, s, gw),
      dog.reshape(bl, s, DIL_WIDTH), dshift.reshape(bl, s, DIL_WIDTH))
    return dq.reshape(t, gw), dk.reshape(t, gw), dv.reshape(t, gw)


def _group_weights(lses):
    mx = jnp.maximum(jnp.maximum(lses[0], lses[1]), lses[2])
    es = [jnp.exp(l - mx) for l in lses]
    inv = 1.0 / (es[0] + es[1] + es[2])
    return [e * inv for e in es]


def dil_combine_fwd(os_, lses, *, name):
    t, gw = os_[0].shape
    tr = _pick(t, (512, 256))

    def body(o0, o1, o2, l0, l1, l2, out_ref):
        al = _group_weights([l0[...], l1[...], l2[...]])
        for g, o_ref in enumerate((o0, o1, o2)):
            out_ref[:, g * gw:(g + 1) * gw] = (o_ref[...] * al[g]).astype(out_ref.dtype)

    blk = pl.BlockSpec((tr, gw), lambda i: (i, 0))
    return pl.pallas_call(
        body, name=name, grid=(t // tr,), in_specs=[blk] * 6,
        out_specs=pl.BlockSpec((tr, 3 * gw), lambda i: (i, 0)), out_shape=jax.ShapeDtypeStruct((t, 3 * gw), BF),
        compiler_params=_params("parallel"),
    )(*os_, *lses)


def dil_combine_bwd(os_, lses, dcat, *, name):
    t, gw = os_[0].shape
    tr = _pick(t, (512, 256))

    def head_sum(v):
        parts = []
        for c in range(gw // LANES):
            blk = v[:, c * LANES:(c + 1) * LANES]
            ma, _ = _lane_masks(tr)
            sa = jnp.sum(jnp.where(ma, blk, 0.0), axis=1, keepdims=True)
            sb = jnp.sum(blk, axis=1, keepdims=True) - sa
            parts.append(jnp.where(ma, sa, sb))
        return jnp.concatenate(parts, axis=1)

    def body(o0, o1, o2, l0, l1, l2, d_ref, dog_ref, sh_ref):
        al = _group_weights([l0[...], l1[...], l2[...]])
        dos = [d_ref[:, g * gw:(g + 1) * gw].astype(F32) for g in range(3)]
        dal = [head_sum(dos[g] * o_ref[...]) for g, o_ref in enumerate((o0, o1, o2))]
        mix = al[0] * dal[0] + al[1] * dal[1] + al[2] * dal[2]
        for g in range(3):
            dog_ref[:, g * gw:(g + 1) * gw] = (al[g] * dos[g]).astype(dog_ref.dtype)
            sh_ref[:, g * gw:(g + 1) * gw] = al[g] * mix

    blk = pl.BlockSpec((tr, gw), lambda i: (i, 0))
    wide = pl.BlockSpec((tr, 3 * gw), lambda i: (i, 0))
    return pl.pallas_call(
        body, name=name, grid=(t // tr,), in_specs=[blk] * 6 + [wide], out_specs=[wide, wide],
        out_shape=[jax.ShapeDtypeStruct((t, 3 * gw), F32), jax.ShapeDtypeStruct((t, 3 * gw), F32)],
        compiler_params=_params("parallel"),
    )(*os_, *lses, dcat)


def adamw(w, g1, g2, m, v, *, name):
    r, c = w.shape
    tr = r
    for cand in (256, 128, 64, 32, 16, 8):
        if r % cand == 0 and cand * c * 4 <= (1 << 20):
            tr = cand
            break
    two = g2 is not None

    def body(*refs):
        if two:
            w_ref, g1_ref, g2_ref, m_ref, v_ref, g_ref, d_ref, nm_ref, nv_ref = refs
            g = g1_ref[...] + g2_ref[...]
        else:
            w_ref, g1_ref, m_ref, v_ref, g_ref, d_ref, nm_ref, nv_ref = refs
            g = g1_ref[...]
        nm = ADAM_B1 * m_ref[...] + (1.0 - ADAM_B1) * g
        nv = ADAM_B2 * v_ref[...] + (1.0 - ADAM_B2) * (g * g)
        m_hat = nm / (1.0 - ADAM_B1 ** ADAM_STEP)
        v_hat = nv / (1.0 - ADAM_B2 ** ADAM_STEP)
        g_ref[...] = g
        d_ref[...] = -ADAM_LR * (m_hat / (jnp.sqrt(v_hat) + ADAM_EPS) + ADAM_WD * w_ref[...])
        nm_ref[...] = nm
        nv_ref[...] = nv

    blk = pl.BlockSpec((tr, c), lambda i: (i, 0))
    args = [w, g1] + ([g2] if two else []) + [m, v]
    return pl.pallas_call(
        body, name=name, grid=(r // tr,), in_specs=[blk] * len(args), out_specs=[blk] * 4,
        out_shape=[jax.ShapeDtypeStruct((r, c), F32)] * 4, compiler_params=_params("parallel"),
    )(*args)


def sum4(own, land, *, name):
    r, c = own.shape
    tr = _pick(r, (256, 128, 64))

    def body(o_ref, l_ref, s_ref):
        s_ref[...] = ((o_ref[...].astype(F32) + l_ref[0].astype(F32)) + l_ref[1].astype(F32)) + l_ref[2].astype(F32)

    return pl.pallas_call(
        body, name=name, grid=(r // tr,),
        in_specs=[pl.BlockSpec((tr, c), lambda i: (i, 0)), pl.BlockSpec((3, tr, c), lambda i: (0, i, 0))],
        out_specs=pl.BlockSpec((tr, c), lambda i: (i, 0)), out_shape=jax.ShapeDtypeStruct((r, c), F32),
        compiler_params=_params("parallel"),
    )(own, land)


ANY = pl.BlockSpec(memory_space=pl.ANY)


def _place():
    x, y, c = lax.axis_index("x"), lax.axis_index("y"), lax.axis_index("c")
    chips = [(1 - x, y), (x, 1 - y), (1 - x, 1 - y)]
    return x, y, c, chips


class Ride:
    def __init__(self, arrays, out_shape, copies):
        self.arrays, self.out_shape, self.copies = list(arrays), list(out_shape), copies
        n = 3 * len(self.arrays)
        self.scratch = [pltpu.SemaphoreType.DMA((n,)), pltpu.SemaphoreType.DMA((n,)),
                        pltpu.SemaphoreType.DMA((len(self.arrays),))]

    def split(self, refs):
        n, m = len(self.arrays), len(self.out_shape)
        return refs[:n], refs[n:n + m], refs[n + m:]

    def start(self, ins, outs, sems):
        local, sends, _ = self.copies(ins, outs, *sems)
        for cp in local + sends:
            cp.start()

    def wait(self, ins, outs, sems):
        local, sends, arrivals = self.copies(ins, outs, *sems)
        for cp in arrivals:
            cp.wait_recv()
        for cp in sends:
            cp.wait_send()
        for cp in local:
            cp.wait()

    def run(self, name):
        def body(*refs):
            ins, outs, sems = self.split(refs)
            self.start(ins, outs, sems)
            self.wait(ins, outs, sems)

        return pl.pallas_call(body, name=name, in_specs=[ANY] * len(self.arrays), out_specs=[ANY] * len(self.out_shape),
                              out_shape=self.out_shape, scratch_shapes=self.scratch)(*self.arrays)


def gather_ride(shards, axes):
    def copies(ins, outs, send_sems, recv_sems, local_sems):
        x, y, c, chips = _place()

        def slot(a, q):
            size = shards[a].shape[axes[a]]
            start = pl.multiple_of(q * size, size)
            return outs[a].at[pl.ds(start, size), :] if axes[a] == 0 else outs[a].at[:, pl.ds(start, size)]

        def remote(a, k, q):
            px, py = chips[k]
            return pltpu.make_async_remote_copy(src_ref=ins[a], dst_ref=slot(a, q), send_sem=send_sems.at[3 * a + k],
                                                recv_sem=recv_sems.at[3 * a + k], device_id=(px, py, c), device_id_type=MESH)

        me = 2 * x + y
        n = len(shards)
        local = [pltpu.make_async_copy(ins[a], slot(a, me), local_sems.at[a]) for a in range(n)]
        sends = [remote(a, k, me) for a in range(n) for k in range(3)]
        arrivals = [remote(a, k, 2 * chips[k][0] + chips[k][1]) for a in range(n) for k in range(3)]
        return local, sends, arrivals

    out_shape = []
    for a, sh in enumerate(shards):
        full = list(sh.shape)
        full[axes[a]] *= N_CHIPS
        out_shape.append(jax.ShapeDtypeStruct(tuple(full), sh.dtype))
    return Ride(shards, out_shape, copies)


def scatter_ride(grads):
    def copies(ins, outs, send_sems, recv_sems, local_sems):
        x, y, c, chips = _place()
        sends = [pltpu.make_async_remote_copy(src_ref=ins[a].at[2 * px + py], dst_ref=outs[a].at[k],
                                              send_sem=send_sems.at[3 * a + k], recv_sem=recv_sems.at[3 * a + k],
                                              device_id=(px, py, c), device_id_type=MESH)
                 for a in range(len(grads)) for k, (px, py) in enumerate(chips)]
        return [], sends, sends

    return Ride(grads, [jax.ShapeDtypeStruct((3,) + g.shape[1:], g.dtype) for g in grads], copies)


def swap_with_sibling(arrs, *, name):
    n = len(arrs)

    def body(*refs):
        ins, outs = refs[:n], refs[n:2 * n]
        send_sems, recv_sems = refs[2 * n:]
        x, y, c, _ = _place()
        sends = []
        for a in range(n):
            cp = pltpu.make_async_remote_copy(src_ref=ins[a], dst_ref=outs[a], send_sem=send_sems.at[a],
                                              recv_sem=recv_sems.at[a], device_id=(x, y, 1 - c), device_id_type=MESH)
            cp.start()
            sends.append(cp)
        for cp in sends:
            cp.wait_recv()
        for cp in sends:
            cp.wait_send()

    return pl.pallas_call(
        body, name=name, in_specs=[ANY] * n, out_specs=[ANY] * n,
        out_shape=[jax.ShapeDtypeStruct(a.shape, a.dtype) for a in arrs],
        scratch_shapes=[pltpu.SemaphoreType.DMA((n,)), pltpu.SemaphoreType.DMA((n,))],
    )(*arrs)


def all_reduce_small(v, *, name):
    rows = v.shape[0]

    def body(v_ref, o_ref, gath, send_sems, recv_sems):
        x, y, c, _ = _place()
        me = 4 * x + 2 * y + c
        gath[me] = v_ref[...]
        sends = []
        for msk in range(1, N_DEV):
            peer = (x ^ (msk >> 2), y ^ ((msk >> 1) & 1), c ^ (msk & 1))
            cp = pltpu.make_async_remote_copy(src_ref=v_ref, dst_ref=gath.at[me], send_sem=send_sems.at[msk - 1],
                                              recv_sem=recv_sems.at[msk - 1], device_id=peer, device_id_type=MESH)
            cp.start()
            sends.append(cp)
        for msk in range(1, N_DEV):
            pltpu.make_async_remote_copy(src_ref=v_ref, dst_ref=gath.at[me ^ msk], send_sem=send_sems.at[msk - 1],
                                         recv_sem=recv_sems.at[msk - 1], device_id=(x, y, c), device_id_type=MESH).wait_recv()
        for cp in sends:
            cp.wait_send()
        tot = gath[0]
        for q in range(1, N_DEV):
            tot = tot + gath[q]
        o_ref[...] = tot

    vm = pl.BlockSpec(memory_space=pltpu.VMEM)
    return pl.pallas_call(
        body, name=name, in_specs=[vm], out_specs=vm, out_shape=jax.ShapeDtypeStruct(v.shape, F32),
        scratch_shapes=[pltpu.VMEM((N_DEV, rows, LANES), F32), pltpu.SemaphoreType.DMA((N_DEV - 1,)),
                        pltpu.SemaphoreType.DMA((N_DEV - 1,))],
    )(v)


def _ffn_fwd(xin, gain, w_up, w_conv, w_down, bl, tag, rides=(None, None, None)):
    got = {}

    def carried(k, result, delivered):
        if rides[k] is not None:
            got.update(zip(rides[k][1], delivered))
        return result

    n = rmsnorm_fwd(xin, gain, name=f"{tag}_ffn_norm")
    if rides[0] is None:
        u = matmul(n, w_up, name=f"{tag}_ffn_up")
    else:
        u = carried(0, *matmul(n, w_up, ride=rides[0][0], name=f"{tag}_ffn_up"))
    *hc, delivered = conv_silu_fwd(u, w_conv, bl, name=f"{tag}_ffn_conv", ride=rides[1][0] if rides[1] else None)
    h, ca, cg = carried(1, hc, delivered)
    if rides[2] is None:
        xout = matmul(h, w_down, out_dtype=F32, add=xin, name=f"{tag}_ffn_down")
    else:
        xout = carried(2, *matmul(h, w_down, out_dtype=F32, add=xin, ride=rides[2][0], name=f"{tag}_ffn_down"))
    return xout, (n, u, h, ca, cg), got


def _ffn_bwd(dxout, dxout_b, xin, saved, gain, w_up, w_conv, w_down, bl, tag, shard):
    n, u, h, ca, cg = saved
    dh = matmul(dxout_b, w_down, tb=True, name=f"{tag}_ffn_down_dx")
    g_down = matmul(h, dxout_b, ta=True, name=f"{tag}_ffn_down_dw")
    dua, dug, g_conv = conv_silu_bwd(u, ca, cg, w_conv, dh, bl, name=f"{tag}_ffn_conv_bwd")
    dn = matmul(dua, w_up, a2=dug, tb=True, out_dtype=F32, name=f"{tag}_ffn_up_dx")
    g_up = matmul(n, dua, b2=dug, ta=True, shard_cols=shard, name=f"{tag}_ffn_up_dw")
    dxin, dxin_b, g_norm = rmsnorm_bwd(xin, gain, dn, dxout, name=f"{tag}_ffn_norm_bwd")
    return dxin, dxin_b, g_norm, g_up, g_conv, g_down


def _mem_kv(mem2, gain, w_kv, tag):
    mn = rmsnorm_fwd(mem2, gain, name=f"{tag}_mem_norm")
    return mn, matmul(mn, w_kv, name=f"{tag}_mem_kv")


def _mem_kv_bwd(mem2, mn, gain, w_kv, dkvm, tag):
    g_kv = matmul(mn, dkvm, ta=True, name=f"{tag}_mem_kv_dw")
    dmn = matmul(dkvm, w_kv, tb=True, out_dtype=F32, name=f"{tag}_mem_kv_dx")
    _, _, g_norm = rmsnorm_bwd(mem2, gain, dmn, None, name=f"{tag}_mem_norm_bwd")
    return g_norm, g_kv


def _shard_major(name, grad):
    if name in COL_SHARDED:
        return grad
    return grad.reshape(N_CHIPS, grad.shape[0] // N_CHIPS, grad.shape[1])


def local_step(x, mem, target, w, shard_of, late_gather=None, early_scatter=()):
    bl, s, d = x.shape
    t = bl * s
    x0 = x.reshape(t, d)
    mem2 = mem.reshape(bl * MEM_LEN, d)
    tgt = target.reshape(t, d)
    g = {}

    n1 = rmsnorm_fwd(x0, w["a_norm_attn"], name="a_attn_norm")
    proj_a = matmul(n1, w["a_w_in"], name="a_w_in")
    mn_a, kvm_a = _mem_kv(mem2, w["a_norm_mem"], w["a_w_mem_kv"], "a")
    late_gather = late_gather or {}
    behind_sb = late_gather.get("sb")
    o_sb, o_sb_f32, delivered = sb_fwd(proj_a, bl, name="a_sb_fwd", ride=behind_sb[0] if behind_sb else None)
    if behind_sb:
        w = {**w, **dict(zip(behind_sb[1], delivered))}
    o_mem_a = mem_fwd(proj_a, kvm_a, bl, name="a_mem_fwd")
    cat_a = jnp.concatenate([o_sb, o_mem_a], axis=1)
    x1 = matmul(cat_a, w["a_w_out"], out_dtype=F32, add=x0, name="a_w_out")
    x2, ffn_a, got = _ffn_fwd(x1, w["a_norm_ffn"], w["a_ffn_up"], w["a_ffn_conv"], w["a_ffn_down"], bl, "a",
                              rides=tuple(late_gather.get(host) for host in ("up", "conv", "down")))
    w = {**w, **got}

    nkv = rmsnorm_fwd(x2, w["kv_norm"], name="kv_norm")
    kv = matmul(nkv, w["w_kv_shared"], out_dtype=F32, name="w_kv")
    n3 = rmsnorm_fwd(x2, w["b_norm_attn"], name="b_attn_norm")
    proj_b = matmul(n3, w["b_w_in"], out_dtype=F32, name="b_w_in")
    mn_b, kvm_b = _mem_kv(mem2, w["b_norm_mem"], w["b_w_mem_kv"], "b")
    dil = [dil_fwd(proj_b, kv, gi, bl, name=f"b_dil_fwd{gi}") for gi in range(3)]
    os_, lses = [o for o, _ in dil], [l for _, l in dil]
    o_dil = dil_combine_fwd(os_, lses, name="b_dil_combine")
    o_mem_b = mem_fwd(proj_b, kvm_b, bl, name="b_mem_fwd")
    cat_b = jnp.concatenate([o_dil, o_mem_b], axis=1)
    x3 = matmul(cat_b, w["b_w_out"], out_dtype=F32, add=x2, name="b_w_out")
    x4, ffn_b, _ = _ffn_fwd(x3, w["b_norm_ffn"], w["b_ffn_up"], w["b_ffn_conv"], w["b_ffn_down"], bl, "b")

    dx4, dx4b, g["final_norm"], lossvec = final_loss(x4, w["final_norm"], tgt, name="final_loss")

    dx3, dx3b, g["b_norm_ffn"], g["b_ffn_up"], g["b_ffn_conv"], g["b_ffn_down"] = _ffn_bwd(
        dx4, dx4b, x3, ffn_b, w["b_norm_ffn"], w["b_ffn_up"], w["b_ffn_conv"], w["b_ffn_down"], bl, "b",
        shard_of["b_ffn_up"])
    dcat_b = matmul(dx3b, w["b_w_out"], tb=True, name="b_w_out_dx")
    g["b_w_out"] = matmul(cat_b, dx3b, ta=True, name="b_w_out_dw")
    dog, dshift = dil_combine_bwd(os_, lses, dcat_b, name="b_dil_combine_bwd")
    dqs, dks, dvs = [], [], []
    for gi in range(3):
        dq_g, dk_g, dv_g = dil_bwd(proj_b, kv, lses[gi], dog, dshift, gi, bl, name=f"b_dil_bwd{gi}")
        dqs.append(dq_g), dks.append(dk_g), dvs.append(dv_g)
    dq_mem_b, dkvm_b = mem_bwd(proj_b, kvm_b, dcat_b, bl, name="b_mem_bwd")
    g["b_norm_mem"], g["b_w_mem_kv"] = _mem_kv_bwd(mem2, mn_b, w["b_norm_mem"], w["b_w_mem_kv"], dkvm_b, "b")
    dproj_b = jnp.concatenate([dq_g.astype(BF) for dq_g in dqs] + [dq_mem_b], axis=1)
    dn3 = matmul(dproj_b, w["b_w_in"], tb=True, out_dtype=F32, name="b_w_in_dx")
    g["b_w_in"] = matmul(n3, dproj_b, ta=True, name="b_w_in_dw")
    dx2, _, g["b_norm_attn"] = rmsnorm_bwd(x2, w["b_norm_attn"], dn3, dx3, name="b_attn_norm_bwd")
    dkv = jnp.concatenate(dks + dvs, axis=1).astype(BF)
    dnkv = matmul(dkv, w["w_kv_shared"], tb=True, out_dtype=F32, name="w_kv_dx")
    g["w_kv_shared"] = matmul(nkv, dkv, ta=True, shard_cols=shard_of["w_kv_shared"], name="w_kv_dw")
    dx2, dx2b, g["kv_norm"] = rmsnorm_bwd(x2, w["kv_norm"], dnkv, dx2, name="kv_norm_bwd")

    dx1, dx1b, g["a_norm_ffn"], g["a_ffn_up"], g["a_ffn_conv"], g["a_ffn_down"] = _ffn_bwd(
        dx2, dx2b, x1, ffn_a, w["a_norm_ffn"], w["a_ffn_up"], w["a_ffn_conv"], w["a_ffn_down"], bl, "a",
        shard_of["a_ffn_up"])
    dcat_a = matmul(dx1b, w["a_w_out"], tb=True, name="a_w_out_dx")
    g["a_w_out"] = matmul(cat_a, dx1b, ta=True, name="a_w_out_dw")
    leaving = scatter_ride([_shard_major(n, g[n]) for n in early_scatter]) if early_scatter else None
    dq_sb, dk_sb, dv_sb, landed = sb_bwd(proj_a, o_sb_f32, dcat_a, bl, name="a_sb_bwd", ride=leaving)
    landed = dict(zip(early_scatter, landed))
    dq_mem_a, dkvm_a = mem_bwd(proj_a, kvm_a, dcat_a, bl, name="a_mem_bwd")
    g["a_norm_mem"], g["a_w_mem_kv"] = _mem_kv_bwd(mem2, mn_a, w["a_norm_mem"], w["a_w_mem_kv"], dkvm_a, "a")
    dproj_a = jnp.concatenate([dq_sb, dk_sb.astype(BF), dv_sb.astype(BF), dq_mem_a], axis=1)
    dn1 = matmul(dproj_a, w["a_w_in"], tb=True, out_dtype=F32, name="a_w_in_dx")
    g["a_w_in"] = matmul(n1, dproj_a, ta=True, shard_cols=shard_of["a_w_in"], name="a_w_in_dw")
    dx0, _, g["a_norm_attn"] = rmsnorm_bwd(x0, w["a_norm_attn"], dn1, dx1, name="a_attn_norm_bwd")
    return lossvec, dx0, g, landed


MATRICES = ("a_w_in", "a_w_out", "a_w_mem_kv", "a_ffn_up", "a_ffn_down", "w_kv_shared", "b_w_in", "b_w_out",
            "b_w_mem_kv", "b_ffn_up", "b_ffn_down")
COL_SHARDED = ("a_w_in", "a_ffn_up", "w_kv_shared", "b_ffn_up")
FIRST_NEEDED = ("a_w_in", "a_w_mem_kv")
GATHER_BEHIND = {"sb": ("a_w_out", "a_ffn_up", "a_ffn_down"), "up": ("b_ffn_up",),
                 "conv": ("b_ffn_down", "w_kv_shared", "b_w_in"), "down": ("b_w_out", "b_w_mem_kv")}
SMALL_SHARDED = ("a_norm_attn", "a_norm_mem", "a_norm_ffn", "a_ffn_conv", "b_ffn_conv")
SMALL_REPLICATED = ("kv_norm", "b_norm_attn", "b_norm_mem", "b_norm_ffn", "final_norm")
WEIGHTS = ("a_norm_attn", "a_w_in", "a_w_out", "a_norm_mem", "a_w_mem_kv", "a_norm_ffn", "a_ffn_up", "a_ffn_conv",
           "a_ffn_down", "kv_norm", "w_kv_shared", "b_norm_attn", "b_w_in", "b_w_out", "b_norm_mem", "b_w_mem_kv",
           "b_norm_ffn", "b_ffn_up", "b_ffn_conv", "b_ffn_down", "final_norm")


def _two_d(a):
    if a.ndim == 1:
        return a.reshape(1, -1)
    return a.reshape(a.shape[-2], a.shape[-1])


def kernel(x, mem, a_norm_attn, a_w_in, a_w_out, a_norm_mem, a_w_mem_kv, a_norm_ffn, a_ffn_up, a_ffn_conv, a_ffn_down, kv_norm, w_kv_shared, b_norm_attn, b_w_in, b_w_out, b_norm_mem, b_w_mem_kv, b_norm_ffn, b_ffn_up, b_ffn_conv, b_ffn_down, final_norm, loss_target, m_a_norm_attn, m_a_w_in, m_a_w_out, m_a_norm_mem, m_a_w_mem_kv, m_a_norm_ffn, m_a_ffn_up, m_a_ffn_conv, m_a_ffn_down, m_kv_norm, m_w_kv_shared, m_b_norm_attn, m_b_w_in, m_b_w_out, m_b_norm_mem, m_b_w_mem_kv, m_b_norm_ffn, m_b_ffn_up, m_b_ffn_conv, m_b_ffn_down, m_final_norm, v_a_norm_attn, v_a_w_in, v_a_w_out, v_a_norm_mem, v_a_w_mem_kv, v_a_norm_ffn, v_a_ffn_up, v_a_ffn_conv, v_a_ffn_down, v_kv_norm, v_w_kv_shared, v_b_norm_attn, v_b_w_in, v_b_w_out, v_b_norm_mem, v_b_w_mem_kv, v_b_norm_ffn, v_b_ffn_up, v_b_ffn_conv, v_b_ffn_down, v_final_norm):
    given = dict(locals())
    wl = {n: _two_d(given[n]) for n in WEIGHTS}
    ml = {n: _two_d(given["m_" + n]) for n in WEIGHTS}
    vl = {n: _two_d(given["v_" + n]) for n in WEIGHTS}
    chip = 2 * lax.axis_index("x") + lax.axis_index("y")

    packed = jnp.concatenate([wl[n].reshape(-1, LANES) for n in SMALL_SHARDED], axis=0)
    axis_of = lambda n: 1 if n in COL_SHARDED else 0
    late = tuple(n for n in MATRICES if n not in FIRST_NEEDED)
    full = gather_ride([wl[n].astype(BF) for n in FIRST_NEEDED] + [packed],
                       [axis_of(n) for n in FIRST_NEEDED] + [0]).run("gather_first")
    w = dict(zip(FIRST_NEEDED, full[:-1]))
    late_gather = {host: (gather_ride([wl[n].astype(BF) for n in names], [axis_of(n) for n in names]), names)
                   for host, names in GATHER_BEHIND.items()}
    rows = packed.shape[0]
    per_chip = full[-1].reshape(N_CHIPS, rows, LANES)
    r0 = 0
    for n in SMALL_SHARDED:
        nr = wl[n].size // LANES
        piece = per_chip[:, r0:r0 + nr].reshape(N_CHIPS, wl[n].shape[0], wl[n].shape[1])
        w[n] = jnp.concatenate([piece[q] for q in range(N_CHIPS)], axis=1)
        r0 += nr
    for n in SMALL_REPLICATED:
        w[n] = wl[n]

    shard_of = {n: wl[n].shape[1] for n in COL_SHARDED}
    lossvec, dx0, g, landed = local_step(x, mem, loss_target, w, shard_of, late_gather=late_gather, early_scatter=late)
    loss = lax.psum(0.5 * jnp.sum(lossvec) / x.shape[-1], ("x", "y", "c"))

    g4 = {n: _shard_major(n, g[n]) for n in MATRICES}
    landed.update(zip(FIRST_NEEDED, scatter_ride([g4[n] for n in FIRST_NEEDED]).run("scatter_last")))
    sums = [sum4(lax.dynamic_index_in_dim(g4[n], chip, 0, keepdims=False), landed[n], name=f"sum4_{n}")
            for n in MATRICES]
    theirs = swap_with_sibling(sums, name="swap_sums")
    out = {}
    for k, n in enumerate(MATRICES):
        out[n] = adamw(wl[n], sums[k], theirs[k], ml[n], vl[n], name=f"adamw_{n}")

    small = SMALL_SHARDED + SMALL_REPLICATED
    flat = jnp.concatenate([g[n].reshape(-1, LANES) for n in small], axis=0)
    tot = all_reduce_small(flat, name="all_reduce_small")
    r0 = 0
    for n in small:
        nr = g[n].size // LANES
        gn = tot[r0:r0 + nr].reshape(g[n].shape)
        r0 += nr
        if n in SMALL_SHARDED:
            gn = lax.dynamic_slice_in_dim(gn, chip * wl[n].shape[1], wl[n].shape[1], axis=1)
        out[n] = adamw(wl[n], gn, None, ml[n], vl[n], name=f"adamw_{n}")

    res = [loss, dx0.reshape(x.shape)]
    for slot in range(4):
        res += [out[n][slot].reshape(given[n].shape) for n in WEIGHTS]
    return tuple(res)
```

```python
import functools
import math

import jax
import jax.numpy as jnp
from jax import lax
from jax.experimental import pallas as pl
from jax.experimental.pallas import tpu as pltpu

F32 = jnp.float32
BF = jnp.bfloat16
MESH = pl.DeviceIdType.MESH

HEAD_DIM = 64
LANES = 128
SB_WIDTH = 12 * HEAD_DIM
MEM_WIDTH = 4 * HEAD_DIM
DIL_WIDTH = 12 * HEAD_DIM
MEM_LEN = 256
DIL_GROUPS = ((128, 1), (512, 4), (2048, 16))
QB = 128
EPS = 1e-6
SCALE = HEAD_DIM ** -0.5
NEG = -1e30
ALIBI = tuple(2.0 ** (-8.0 * i / 12) for i in range(1, 13))
N_CHIPS = 4
N_DEV = 8

ADAM_LR, ADAM_B1, ADAM_B2, ADAM_EPS, ADAM_WD, ADAM_STEP = 0.001, 0.9, 0.999, 1e-08, 0.01, 10

VMEM_LIMIT = 52 * 1024 * 1024
MATMUL_VMEM_BUDGET = 44 * 1024 * 1024
MATMUL_MAX_TN = 2816
HBM_BYTES_PER_S = 3.0e12
VMEM_BYTES_PER_S = 6.0e12
STEP_OVERHEAD_S = 0.4e-6


def _mo(v, m):
    return v if isinstance(v, int) else pl.multiple_of(v, m)


def _pick(n, prefs):
    for t in prefs:
        if n % t == 0:
            return t
    return n


def _params(*sem):
    return pltpu.CompilerParams(dimension_semantics=sem, vmem_limit_bytes=VMEM_LIMIT)


def matmul(a, b, *, ta=False, tb=False, out_dtype=BF, add=None, shard_cols=0, a2=None, b2=None, ride=None, name):
    m, k = (a.shape[1], a.shape[0]) if ta else a.shape
    n = b.shape[0] if tb else b.shape[1]
    if a2 is not None:
        assert not ta and a2.shape == a.shape
        k *= 2
    if b2 is not None:
        assert not tb and b2.shape == b.shape
        n *= 2
    k_piece = a.shape[1] if a2 is not None else k
    n_piece = b.shape[1] if b2 is not None else n
    out_bytes = jnp.dtype(out_dtype).itemsize
    pieces_a, pieces_b = (2 if a2 is not None else 1), (2 if b2 is not None else 1)
    if shard_cols:
        tn_choices = [c * shard_cols for c in (4, 2, 1) if c * shard_cols <= MATMUL_MAX_TN and c <= 4 // pieces_b]
    else:
        tn_choices = [c for c in (2816, 2560, 1536, 1408, 1024, 512, 256, 128) if n_piece % c == 0]

    def vmem_bytes(tm_, tn_, tk_):
        blocks = (tm_ * tk_ * a.dtype.itemsize * pieces_a + tk_ * tn_ * b.dtype.itemsize * pieces_b
                  + tm_ * tn_ * out_bytes + (tm_ * tn_ * 4 if add is not None else 0))
        return 2 * blocks + (tm_ * tn_ * 4 if k // tk_ > 1 else 0) + tm_ * tn_ * 4

    def cost(c):
        tm_, tn_, tk_ = c
        steps = (m // tm_) * (n // tn_) * (k // tk_)
        hbm = (m * k * a.dtype.itemsize * (n // tn_) + k * n * b.dtype.itemsize * (m // tm_)
               + m * n * (out_bytes + (4 if add is not None else 0)))
        return hbm / HBM_BYTES_PER_S + steps * STEP_OVERHEAD_S + m * n * 8 * (k // tk_ - 1) / VMEM_BYTES_PER_S

    tm, tn, tk = min(((tm_, tn_, tk_) for tm_ in (1024, 512, 1408, 256, 128) if m % tm_ == 0 for tn_ in tn_choices
                      for tk_ in (2816, 2560, 2048, 1536, 1024, 512, 256, 128) if k_piece % tk_ == 0
                      if vmem_bytes(tm_, tn_, tk_) <= MATMUL_VMEM_BUDGET), key=cost)
    ns = tn // shard_cols if shard_cols else 1
    nk = k // tk
    nk1 = nk // 2
    nj1 = (n // tn) // 2
    dims = (((0,) if ta else (1,), (1,) if tb else (0,)), ((), ()))
    n_in = 2 + (a2 is not None) + (b2 is not None) + (add is not None)
    grid = (m // tm, n // tn, nk)

    def body(*refs):
        own_in, (o_ref,), scr, ride_start, ride_wait = _ride_hooks(ride, refs, n_in, 1, int(nk > 1), grid)
        ride_start()
        ins = list(own_in)
        a_ref, b_ref = ins.pop(0), ins.pop(0)
        a2_ref = ins.pop(0) if a2 is not None else None
        b2_ref = ins.pop(0) if b2 is not None else None
        add_ref = ins.pop(0) if add is not None else None

        def dot(ar, br):
            return lambda: lax.dot_general(ar[...].astype(BF), br[...].astype(BF), dims, preferred_element_type=F32)

        if a2 is not None:
            part = lax.cond(pl.program_id(2) < nk1, dot(a_ref, b_ref), dot(a2_ref, b_ref))
        elif b2 is not None:
            part = lax.cond(pl.program_id(1) < nj1, dot(a_ref, b_ref), dot(a_ref, b2_ref))
        else:
            part = dot(a_ref, b_ref)()

        def finish(r):
            if add is not None:
                r = r + add_ref[...]
            if shard_cols:
                for c in range(ns):
                    o_ref[c] = r[:, c * shard_cols:(c + 1) * shard_cols].astype(o_ref.dtype)
            else:
                o_ref[...] = r.astype(o_ref.dtype)

        if nk == 1:
            finish(part)
        else:
            acc_ref = scr[0]
            kk = pl.program_id(2)

            @pl.when(kk == 0)
            def _():
                acc_ref[...] = part

            @pl.when(kk > 0)
            def _():
                acc_ref[...] += part

            @pl.when(kk == nk - 1)
            def _():
                finish(acc_ref[...])
        ride_wait()

    a_spec = pl.BlockSpec((tk, tm), lambda i, j, q: (q, i)) if ta else pl.BlockSpec((tm, tk), lambda i, j, q: (i, q))
    b_spec = pl.BlockSpec((tn, tk), lambda i, j, q: (j, q)) if tb else pl.BlockSpec((tk, tn), lambda i, j, q: (q, j))
    if a2 is not None:
        a_spec = pl.BlockSpec((tm, tk), lambda i, j, q: (i, jnp.minimum(q, nk1 - 1)))
    if b2 is not None:
        b_spec = pl.BlockSpec((tk, tn), lambda i, j, q: (jnp.where(j < nj1, q, 0), jnp.minimum(j, nj1 - 1)))
    in_specs = [a_spec, b_spec]
    args = [a, b]
    if a2 is not None:
        in_specs.append(pl.BlockSpec((tm, tk), lambda i, j, q: (i, jnp.maximum(q - nk1, 0))))
        args.append(a2)
    if b2 is not None:
        in_specs.append(pl.BlockSpec((tk, tn), lambda i, j, q: (jnp.where(j < nj1, 0, q), jnp.maximum(j - nj1, 0))))
        args.append(b2)
    if add is not None:
        in_specs.append(pl.BlockSpec((tm, tn), lambda i, j, q: (i, j)))
        args.append(add)
    if shard_cols:
        out_shape = jax.ShapeDtypeStruct((N_CHIPS, m, shard_cols), out_dtype)
        out_spec = pl.BlockSpec((ns, tm, shard_cols), lambda i, j, q: (j, i, 0))
    else:
        out_shape = jax.ShapeDtypeStruct((m, n), out_dtype)
        out_spec = pl.BlockSpec((tm, tn), lambda i, j, q: (i, j))
    own_scratch = [pltpu.VMEM((tm, tn), F32)] if nk > 1 else []
    if ride is None:
        return pl.pallas_call(
            body, name=name, grid=grid, in_specs=in_specs, out_specs=out_spec, out_shape=out_shape,
            scratch_shapes=own_scratch, compiler_params=_params("parallel", "parallel", "arbitrary"),
        )(*args)
    out, *delivered = pl.pallas_call(
        body, name=name, grid=grid, in_specs=in_specs + [ANY] * len(ride.arrays),
        out_specs=[out_spec] + [ANY] * len(ride.out_shape), out_shape=[out_shape] + ride.out_shape,
        scratch_shapes=own_scratch + ride.scratch, compiler_params=_params("arbitrary", "arbitrary", "arbitrary"),
    )(*args, *ride.arrays)
    return out, delivered


def rmsnorm_fwd(x, g, *, name):
    t, d = x.shape
    tr = _pick(t, (512, 256))

    def body(x_ref, g_ref, o_ref):
        xv = x_ref[...]
        r = lax.rsqrt(jnp.mean(xv * xv, axis=-1, keepdims=True) + EPS)
        o_ref[...] = ((xv * r) * g_ref[...]).astype(o_ref.dtype)

    return pl.pallas_call(
        body, name=name, grid=(t // tr,),
        in_specs=[pl.BlockSpec((tr, d), lambda i: (i, 0)), pl.BlockSpec((1, d), lambda i: (0, 0))],
        out_specs=pl.BlockSpec((tr, d), lambda i: (i, 0)), out_shape=jax.ShapeDtypeStruct((t, d), BF),
        compiler_params=_params("parallel"),
    )(x, g)


def rmsnorm_bwd(x, g, dn, dres, *, name):
    t, d = x.shape
    tr = _pick(t, (512, 256))
    want_dx = dres is not None

    def body(*refs):
        if want_dx:
            x_ref, g_ref, dn_ref, dres_ref, dx_ref, dxb_ref, dg_ref = refs
        else:
            x_ref, g_ref, dn_ref, dg_ref = refs

        @pl.when(pl.program_id(0) == 0)
        def _():
            dg_ref[...] = jnp.zeros_like(dg_ref)

        xv = x_ref[...]
        r = lax.rsqrt(jnp.mean(xv * xv, axis=-1, keepdims=True) + EPS)
        xn = xv * r
        dnv = dn_ref[...].astype(F32)
        dg_ref[...] += jnp.sum(dnv * xn, axis=0, keepdims=True)
        if want_dx:
            dyg = dnv * g_ref[...]
            cm = jnp.mean(dyg * xn, axis=-1, keepdims=True)
            dx = dres_ref[...] + r * (dyg - xn * cm)
            dx_ref[...] = dx
            dxb_ref[...] = dx.astype(BF)

    row = pl.BlockSpec((tr, d), lambda i: (i, 0))
    vec = pl.BlockSpec((1, d), lambda i: (0, 0))
    if want_dx:
        return pl.pallas_call(
            body, name=name, grid=(t // tr,), in_specs=[row, vec, row, row], out_specs=[row, row, vec],
            out_shape=[jax.ShapeDtypeStruct((t, d), F32), jax.ShapeDtypeStruct((t, d), BF), jax.ShapeDtypeStruct((1, d), F32)],
            compiler_params=_params("arbitrary"),
        )(x, g, dn, dres)
    return None, None, pl.pallas_call(
        body, name=name, grid=(t // tr,), in_specs=[row, vec, row], out_specs=vec,
        out_shape=jax.ShapeDtypeStruct((1, d), F32), compiler_params=_params("arbitrary"),
    )(x, g, dn)


def final_loss(x, g, target, *, name):
    t, d = x.shape
    tr = _pick(t, (512, 256))

    def body(x_ref, g_ref, t_ref, dx_ref, dxb_ref, dg_ref, lv_ref):
        @pl.when(pl.program_id(0) == 0)
        def _():
            dg_ref[...] = jnp.zeros_like(dg_ref)
            lv_ref[...] = jnp.zeros_like(lv_ref)

        xv = x_ref[...]
        r = lax.rsqrt(jnp.mean(xv * xv, axis=-1, keepdims=True) + EPS)
        xn = xv * r
        err = xn * g_ref[...] - t_ref[...]
        lv_ref[...] += jnp.sum(err * err, axis=0, keepdims=True)
        dy = err * (1.0 / d)
        dg_ref[...] += jnp.sum(dy * xn, axis=0, keepdims=True)
        dyg = dy * g_ref[...]
        cm = jnp.mean(dyg * xn, axis=-1, keepdims=True)
        dx = r * (dyg - xn * cm)
        dx_ref[...] = dx
        dxb_ref[...] = dx.astype(BF)

    row = pl.BlockSpec((tr, d), lambda i: (i, 0))
    vec = pl.BlockSpec((1, d), lambda i: (0, 0))
    return pl.pallas_call(
        body, name=name, grid=(t // tr,), in_specs=[row, vec, row], out_specs=[row, row, vec, vec],
        out_shape=[jax.ShapeDtypeStruct((t, d), F32), jax.ShapeDtypeStruct((t, d), BF), jax.ShapeDtypeStruct((1, d), F32),
                   jax.ShapeDtypeStruct((1, d), F32)],
        compiler_params=_params("arbitrary"),
    )(x, g, target)


CONV_ROWS = 256
HALO = 16


def _conv_taps(ext, w, rows):
    s0 = ext[HALO:HALO + rows]
    s1 = pltpu.roll(ext, 1, 0)[HALO:HALO + rows]
    s2 = pltpu.roll(ext, 2, 0)[HALO:HALO + rows]
    return (w[0:1] * s2 + w[1:2] * s1) + w[2:3] * s0, s0, s1, s2


def conv_silu_fwd(u, w, bl, *, name, ride=None):
    t, f2 = u.shape
    f = f2 // 2
    s = t // bl
    tc = _pick(f, (256, 128))
    nf = f // tc
    nch = s // CONV_ROWS

    def body(*refs):
        (ua_ref, ug_ref, wa_ref, wg_ref), (h_ref, ca_ref, cg_ref), _, ride_start, ride_wait = _ride_hooks(
            ride, refs, 4, 3, 0, (nf, bl))
        ride_start()
        wa, wg = wa_ref[...], wg_ref[...]

        def chunk(ci, carry):
            r0 = pl.multiple_of(ci * CONV_ROWS, CONV_ROWS)
            ps = pl.multiple_of(jnp.maximum(r0 - HALO, 0), HALO)
            keep = (ci > 0).astype(F32)
            rows = pl.ds(r0, CONV_ROWS)

            def conv(ref, wv):
                ext = jnp.concatenate([ref[0, pl.ds(ps, HALO), :].astype(F32) * keep, ref[0, rows, :].astype(F32)], axis=0)
                return _conv_taps(ext, wv, CONV_ROWS)[0]

            ca, cg = conv(ua_ref, wa), conv(ug_ref, wg)
            sg = pl.reciprocal(1.0 + jnp.exp(-cg), approx=True)
            h_ref[0, rows, :] = ((cg * sg) * ca).astype(h_ref.dtype)
            ca_ref[0, rows, :] = ca.astype(ca_ref.dtype)
            cg_ref[0, rows, :] = cg.astype(cg_ref.dtype)
            return carry

        lax.fori_loop(0, nch, chunk, 0)
        ride_wait()

    u3 = u.reshape(bl, s, f2)
    blk = lambda off: pl.BlockSpec((1, s, tc), lambda j, b: (b, 0, j + off))
    wblk = lambda off: pl.BlockSpec((3, tc), lambda j, b: (0, j + off))
    extra = ride.arrays if ride else []
    h, ca, cg, *delivered = pl.pallas_call(
        body, name=name, grid=(nf, bl), in_specs=[blk(0), blk(nf), wblk(0), wblk(nf)] + [ANY] * len(extra),
        out_specs=[blk(0)] * 3 + [ANY] * (len(ride.out_shape) if ride else 0),
        out_shape=[jax.ShapeDtypeStruct((bl, s, f), BF)] * 3 + (ride.out_shape if ride else []),
        scratch_shapes=ride.scratch if ride else [],
        compiler_params=_params(*(("arbitrary",) * 2 if ride else ("parallel",) * 2)),
    )(u3, u3, w, w, *extra)
    return h.reshape(t, f), ca.reshape(t, f), cg.reshape(t, f), delivered


def conv_silu_bwd(u, ca, cg, w, dh, bl, *, name):
    t, f2 = u.shape
    f = f2 // 2
    s = t // bl
    tc = _pick(f, (256, 128))
    nf = f // tc
    nch = s // CONV_ROWS
    ext_rows = CONV_ROWS + HALO

    def body(ua_ref, ug_ref, ca_ref, cg_ref, wa_ref, wg_ref, dh_ref, dua_ref, dug_ref, dwa_ref, dwg_ref):
        wa, wg = wa_ref[...], wg_ref[...]

        @pl.when(pl.program_id(1) == 0)
        def _():
            dwa_ref[...] = jnp.zeros_like(dwa_ref)
            dwg_ref[...] = jnp.zeros_like(dwg_ref)

        def chunk(ci, carry):
            r0 = pl.multiple_of(ci * CONV_ROWS, CONV_ROWS)
            ns = pl.multiple_of(jnp.minimum(r0 + CONV_ROWS, s - HALO), HALO)
            keep_n = (ci < nch - 1).astype(F32)
            rows = pl.ds(r0, CONV_ROWS)

            def ext_of(ref):
                return jnp.concatenate([ref[0, rows, :].astype(F32), ref[0, pl.ds(ns, HALO), :].astype(F32)], axis=0)

            dhe = jnp.concatenate([dh_ref[0, rows, :].astype(F32), dh_ref[0, pl.ds(ns, HALO), :].astype(F32) * keep_n],
                                  axis=0)
            cae, cge = ext_of(ca_ref), ext_of(cg_ref)
            sg = pl.reciprocal(1.0 + jnp.exp(-cge), approx=True)
            dca = dhe * (cge * sg)
            dcg = dhe * cae * (sg * (1.0 + cge * (1.0 - sg)))

            def back(dc, wv, u_ref, du_ref, dw_ref):
                d0 = dc[:CONV_ROWS]
                n1 = pltpu.roll(dc, ext_rows - 1, 0)[:CONV_ROWS]
                n2 = pltpu.roll(dc, ext_rows - 2, 0)[:CONV_ROWS]
                du_ref[0, rows, :] = ((wv[2:3] * d0 + wv[1:2] * n1) + wv[0:1] * n2).astype(du_ref.dtype)
                uc = u_ref[0, rows, :].astype(F32)
                for k, shifted in enumerate((n2, n1, d0)):
                    dw_ref[k:k + 1, :] += jnp.sum(shifted * uc, axis=0, keepdims=True)

            back(dca, wa, ua_ref, dua_ref, dwa_ref)
            back(dcg, wg, ug_ref, dug_ref, dwg_ref)
            return carry

        lax.fori_loop(0, nch, chunk, 0)

    u3 = u.reshape(bl, s, f2)
    blk = lambda off: pl.BlockSpec((1, s, tc), lambda j, b: (b, 0, j + off))
    wblk = lambda off: pl.BlockSpec((3, tc), lambda j, b: (0, j + off))
    dua, dug, dwa, dwg = pl.pallas_call(
        body, name=name, grid=(nf, bl), in_specs=[blk(0), blk(nf), blk(0), blk(0), wblk(0), wblk(nf), blk(0)],
        out_specs=[blk(0), blk(0), wblk(0), wblk(0)],
        out_shape=[jax.ShapeDtypeStruct((bl, s, f), BF), jax.ShapeDtypeStruct((bl, s, f), BF),
                   jax.ShapeDtypeStruct((3, f), F32), jax.ShapeDtypeStruct((3, f), F32)],
        compiler_params=_params("parallel", "arbitrary"),
    )(u3, u3, ca.reshape(bl, s, f), cg.reshape(bl, s, f), w, w, dh.reshape(bl, s, f))
    return dua.reshape(t, f), dug.reshape(t, f), jnp.concatenate([dwa, dwg], axis=1)


def _lane_masks(rows):
    lane = lax.broadcasted_iota(jnp.int32, (rows, LANES), 1)
    return lane < HEAD_DIM, lane >= HEAD_DIM


def _nt(a, b):
    return lax.dot_general(a, b, (((1,), (1,)), ((), ())), preferred_element_type=F32)


def _tn(a, b):
    return lax.dot_general(a, b, (((0,), (0,)), ((), ())), preferred_element_type=F32)


def _nn(a, b):
    return jnp.dot(a, b, preferred_element_type=F32)


def _suffix_ones():
    j = lax.broadcasted_iota(jnp.int32, (2 * QB, QB), 0) % QB
    s = lax.broadcasted_iota(jnp.int32, (2 * QB, QB), 1)
    return (j >= s).astype(BF)


def _softplus(z):
    return jnp.maximum(z, 0.0) + jnp.log(1.0 + jnp.exp(-jnp.abs(z)))


SB_BLOCK = 256
SB_DEAD = -100.0


SB_STRIP = 32
SB_SUB = SB_BLOCK // QB


def _sb_scratch(backward):
    blk = (2, SB_BLOCK, SB_BLOCK)
    per_head = pltpu.VMEM((2, SB_BLOCK, LANES), F32)
    scr = [pltpu.VMEM((2 * QB, QB), BF),
           pltpu.VMEM(blk, F32),
           pltpu.VMEM((2, SB_SUB) + blk[1:], BF),
           pltpu.VMEM(blk, F32),
           pltpu.VMEM(blk, BF),
           per_head]
    if not backward:
        return scr + [per_head]
    return scr + [pltpu.VMEM(blk, F32),
                  pltpu.VMEM(blk, F32),
                  pltpu.VMEM(blk, F32),
                  pltpu.VMEM(blk, BF),
                  per_head,
                  per_head,
                  per_head]


def _sb_strips(fn):
    for r in range(SB_BLOCK // SB_STRIP):
        fn(pl.ds(r * SB_STRIP, SB_STRIP), r)


def _sb_keep(diag, r, v):
    if not diag:
        return v
    rel = (lax.broadcasted_iota(jnp.int32, (SB_STRIP, SB_BLOCK), 1)
           - lax.broadcasted_iota(jnp.int32, (SB_STRIP, SB_BLOCK), 0))
    return jnp.where(rel < r * SB_STRIP, v, 0.0)


def _sb_split_store(sp_scr, rows, v):
    hi = v.astype(BF)
    lo = (v - hi.astype(F32)).astype(BF)
    for u in range(SB_SUB):
        sp_scr[u, rows, 0:QB] = hi[:, u * QB:(u + 1) * QB]
        sp_scr[u, rows, QB:2 * QB] = lo[:, u * QB:(u + 1) * QB]


def _sb_suffix_sums(uu_scr, sp_scr, out_scr):
    for u in range(SB_SUB):
        out_scr[:, u * QB:(u + 1) * QB] = _nn(sp_scr[u], uu_scr[...])


def _sb_fold(sums, off):
    offs = [None] * SB_SUB
    for u in reversed(range(SB_SUB)):
        offs[u] = off
        off = off + jnp.broadcast_to(sums[:, u * QB:u * QB + 1], (SB_STRIP, LANES))
    return offs, off


def _sb_first(diag, z_scr, sp_scr, ls_scr=None):
    def strip(rows, r):
        ls = _sb_keep(diag, r, -_softplus(z_scr[rows, :]))
        if ls_scr is not None:
            ls_scr[rows, :] = ls
        _sb_split_store(sp_scr, rows, ls)

    _sb_strips(strip)


def _sb_second(diag, z_scr, cin_scr, w_scr, carry_scr):
    def strip(rows, r):
        z, cin = z_scr[rows, :], cin_scr[rows, :]
        offs, carry_scr[rows, :] = _sb_fold(cin, carry_scr[rows, :])
        e = jnp.concatenate([z[:, u * QB:(u + 1) * QB] + cin[:, u * QB:(u + 1) * QB] + offs[u] for u in range(SB_SUB)],
                            axis=1)
        w_scr[rows, :] = _sb_keep(diag, r, jnp.exp(e)).astype(BF)

    _sb_strips(strip)


def _sb_third(w_scr, dw_scr, da_scr, sp_scr):
    def strip(rows, r):
        da = w_scr[rows, :].astype(F32) * dw_scr[rows, :]
        da_scr[rows, :] = da
        _sb_split_store(sp_scr, rows, da)

    _sb_strips(strip)


def _sb_fourth(diag, z_scr, ls_scr, da_scr, sin_scr, dz_scr, carry_s_scr, dsum_scr):
    def strip(rows, r):
        da, sin = da_scr[rows, :], sin_scr[rows, :]
        offs, carry_s_scr[rows, :] = _sb_fold(sin, carry_s_scr[rows, :])
        dsum = dsum_scr[rows, :]
        pre = jnp.concatenate([dsum - (sin[:, u * QB:(u + 1) * QB] - da[:, u * QB:(u + 1) * QB] + offs[u])
                               for u in range(SB_SUB)], axis=1)
        sig = jnp.exp(z_scr[rows, :] + ls_scr[rows, :])
        dz_scr[rows, :] = _sb_keep(diag, r, da - sig * pre).astype(BF)

    _sb_strips(strip)


def _sb_alive(carry_scr):
    return (jnp.max(carry_scr[...]) > SB_DEAD).astype(jnp.int32)


def _ride_hooks(ride, refs, n_in, n_out, n_scratch, grid):
    if ride is None:
        return refs[:n_in], refs[n_in:n_in + n_out], refs[n_in + n_out:], lambda: None, lambda: None
    ni, no = len(ride.arrays), len(ride.out_shape)
    own_in, rin = refs[:n_in], refs[n_in:n_in + ni]
    own_out, rout = refs[n_in + ni:n_in + ni + n_out], refs[n_in + ni + n_out:n_in + ni + n_out + no]
    rest = refs[n_in + ni + n_out + no:]
    own_scr, sems = rest[:n_scratch], rest[n_scratch:]
    ids = [pl.program_id(a) for a in range(len(grid))]

    def start():
        first = functools.reduce(lambda u, v: u & v, [i == 0 for i in ids])
        pl.when(first)(lambda: ride.start(rin, rout, sems))

    def wait():
        last = functools.reduce(lambda u, v: u & v, [i == n - 1 for i, n in zip(ids, grid)])
        pl.when(last)(lambda: ride.wait(rin, rout, sems))

    return own_in, own_out, own_scr, start, wait


def sb_fwd(proj, bl, *, name, ride=None):
    t, width = proj.shape
    s = t // bl
    npair = SB_WIDTH // LANES
    nq = s // SB_BLOCK
    grid = (bl, npair, nq)

    own_scratch = _sb_scratch(backward=False)

    def body(*refs):
        (q_ref, k_ref, v_ref), (o_ref, of_ref), scr, ride_start, ride_wait = _ride_hooks(
            ride, refs, 3, 2, len(own_scratch), grid)
        uu_scr, z_scr, sp_scr, cin_scr, w_scr, carry_scr, acc_scr = scr
        ride_start()
        i = pl.program_id(2)
        uu_scr[...] = _suffix_ones()
        carry_scr[...] = jnp.zeros_like(carry_scr)
        acc_scr[...] = jnp.zeros_like(acc_scr)
        qs = (q_ref[0].astype(F32) * SCALE).astype(BF)
        masks = _lane_masks(SB_BLOCK)

        z, sp, cin, w, car, acc = ([r.at[h] for h in range(2)] for r in (z_scr, sp_scr, cin_scr, w_scr, carry_scr, acc_scr))

        def live(c, diag):
            start = pl.multiple_of(c * SB_BLOCK, SB_BLOCK)
            kb, vb = k_ref[0, pl.ds(start, SB_BLOCK), :], v_ref[0, pl.ds(start, SB_BLOCK), :]
            for h in range(2):
                z[h][...] = _nt(jnp.where(masks[h], qs, jnp.zeros_like(qs)), kb)
            _sb_first(diag, z[0], sp[0])
            _sb_suffix_sums(uu_scr, sp[0], cin[0])
            _sb_first(diag, z[1], sp[1])
            _sb_suffix_sums(uu_scr, sp[1], cin[1])
            _sb_second(diag, z[0], cin[0], w[0], car[0])
            acc[0][...] += _nn(w[0][...], vb)
            _sb_second(diag, z[1], cin[1], w[1], car[1])
            acc[1][...] += _nn(w[1][...], vb)
            return _sb_alive(carry_scr)

        def step(n, alive):
            return lax.cond(alive > 0, lambda: live(i - n, False), lambda: alive)

        lax.fori_loop(1, i + 1, step, live(i, True))
        both = jnp.where(masks[0], acc_scr[0], acc_scr[1])
        o_ref[0] = both.astype(o_ref.dtype)
        of_ref[0] = both
        ride_wait()

    p3 = proj.reshape(bl, s, width)
    qblk = pl.BlockSpec((1, SB_BLOCK, LANES), lambda b, p, i: (b, i, p))
    extra = ride.arrays if ride else []
    o, of, *delivered = pl.pallas_call(
        body, name=name, grid=grid,
        in_specs=[qblk, pl.BlockSpec((1, s, LANES), lambda b, p, i: (b, 0, npair + p)),
                  pl.BlockSpec((1, s, LANES), lambda b, p, i: (b, 0, 2 * npair + p))] + [ANY] * len(extra),
        out_specs=[qblk, qblk] + [ANY] * (len(ride.out_shape) if ride else 0),
        out_shape=[jax.ShapeDtypeStruct((bl, s, SB_WIDTH), BF), jax.ShapeDtypeStruct((bl, s, SB_WIDTH), F32)]
        + (ride.out_shape if ride else []),
        scratch_shapes=own_scratch + (ride.scratch if ride else []),
        compiler_params=_params("arbitrary", "arbitrary", "arbitrary"),
    )(p3, p3, p3, *extra)
    return o.reshape(t, SB_WIDTH), of.reshape(t, SB_WIDTH), delivered


def sb_bwd(proj, o, dcat, bl, *, name, ride=None):
    t, width = proj.shape
    s = t // bl
    npair = SB_WIDTH // LANES
    nq = s // SB_BLOCK
    grid = (bl, npair, nq)
    own_scratch = _sb_scratch(backward=True)

    def body(*refs):
        (q_ref, k_ref, v_ref, o_ref, do_ref), (dq_ref, dk_ref, dv_ref), scr, ride_start, ride_wait = _ride_hooks(
            ride, refs, 5, 3, len(own_scratch), grid)
        (uu_scr, z_scr, sp_scr, cin_scr, w_scr, carry_scr,
         ls_scr, dw_scr, da_scr, dz_scr, carry_s_scr, dsum_scr, dq_scr) = scr
        ride_start()
        i = pl.program_id(2)

        @pl.when(i == 0)
        def _():
            dk_ref[...] = jnp.zeros_like(dk_ref)
            dv_ref[...] = jnp.zeros_like(dv_ref)

        uu_scr[...] = _suffix_ones()
        for ref in (carry_scr, carry_s_scr, dq_scr):
            ref[...] = jnp.zeros_like(ref)
        masks = _lane_masks(SB_BLOCK)
        qs = (q_ref[0].astype(F32) * SCALE).astype(BF)
        do = do_ref[0]
        prod = do.astype(F32) * o_ref[0]
        for h in range(2):
            dsum_scr[h] = jnp.broadcast_to(jnp.sum(jnp.where(masks[h], prod, 0.0), axis=1, keepdims=True),
                                           (SB_BLOCK, LANES))

        z, sp, cin, w, car, ls, dw, da, dz, cars, dsum, dq = (
            [r.at[h] for h in range(2)] for r in (z_scr, sp_scr, cin_scr, w_scr, carry_scr, ls_scr, dw_scr, da_scr,
                                                  dz_scr, carry_s_scr, dsum_scr, dq_scr))

        def live(c, diag):
            start = pl.multiple_of(c * SB_BLOCK, SB_BLOCK)
            keys = pl.ds(start, SB_BLOCK)
            kb, vb = k_ref[0, keys, :], v_ref[0, keys, :]
            qh = [jnp.where(masks[h], qs, jnp.zeros_like(qs)) for h in range(2)]
            doh = [jnp.where(masks[h], do, jnp.zeros_like(do)) for h in range(2)]
            for h in range(2):
                z[h][...] = _nt(qh[h], kb)
                dw[h][...] = _nt(doh[h], vb)

            def weights(h):
                _sb_second(diag, z[h], cin[h], w[h], car[h])
                _sb_third(w[h], dw[h], da[h], sp[h])
                _sb_suffix_sums(uu_scr, sp[h], cin[h])

            def grads(h):
                _sb_fourth(diag, z[h], ls[h], da[h], cin[h], dz[h], cars[h], dsum[h])
                dk_ref[0, keys, :] += _tn(dz[h][...], qh[h])
                dv_ref[0, keys, :] += _tn(w[h][...], doh[h])
                dq[h][...] += _nn(dz[h][...], kb)

            _sb_first(diag, z[0], sp[0], ls[0])
            _sb_suffix_sums(uu_scr, sp[0], cin[0])
            _sb_first(diag, z[1], sp[1], ls[1])
            _sb_suffix_sums(uu_scr, sp[1], cin[1])
            weights(0)
            weights(1)
            grads(0)
            grads(1)
            return _sb_alive(carry_scr)

        def step(n, alive):
            return lax.cond(alive > 0, lambda: live(i - n, False), lambda: alive)

        lax.fori_loop(1, i + 1, step, live(i, True))
        dq_ref[0] = (jnp.where(masks[0], dq_scr[0], dq_scr[1]) * SCALE).astype(dq_ref.dtype)
        ride_wait()

    p3 = proj.reshape(bl, s, width)
    o3 = o.reshape(bl, s, SB_WIDTH)
    d3 = dcat.reshape(bl, s, dcat.shape[1])
    qblk = pl.BlockSpec((1, SB_BLOCK, LANES), lambda b, p, i: (b, i, p))
    full = pl.BlockSpec((1, s, LANES), lambda b, p, i: (b, 0, p))
    extra = ride.arrays if ride else []
    dq, dk, dv, *delivered = pl.pallas_call(
        body, name=name, grid=grid,
        in_specs=[qblk, pl.BlockSpec((1, s, LANES), lambda b, p, i: (b, 0, npair + p)),
                  pl.BlockSpec((1, s, LANES), lambda b, p, i: (b, 0, 2 * npair + p)), qblk, qblk] + [ANY] * len(extra),
        out_specs=[qblk, full, full] + [ANY] * (len(ride.out_shape) if ride else 0),
        out_shape=[jax.ShapeDtypeStruct((bl, s, SB_WIDTH), BF), jax.ShapeDtypeStruct((bl, s, SB_WIDTH), F32),
                   jax.ShapeDtypeStruct((bl, s, SB_WIDTH), F32)] + (ride.out_shape if ride else []),
        scratch_shapes=own_scratch + (ride.scratch if ride else []),
        compiler_params=_params("arbitrary", "arbitrary", "arbitrary"),
    )(p3, p3, p3, o3, d3, *extra)
    return dq.reshape(t, SB_WIDTH), dk.reshape(t, SB_WIDTH), dv.reshape(t, SB_WIDTH), delivered


MEM_TQ = 512


def mem_fwd(proj, kvm, bl, *, name):
    t, width = proj.shape
    s = t // bl
    qoff = (width - MEM_WIDTH) // LANES
    npair = MEM_WIDTH // LANES

    def body(q_ref, k_ref, v_ref, o_ref):
        ma, mb = _lane_masks(MEM_TQ)
        q = q_ref[...].astype(BF)
        outs = []
        for m in (ma, mb):
            qh = jnp.where(m, q, jnp.zeros_like(q))
            sc = _nt(qh, k_ref[...]) * SCALE
            p = jnp.exp(sc - jnp.max(sc, axis=-1, keepdims=True))
            p = p * (1.0 / jnp.sum(p, axis=-1, keepdims=True))
            outs.append(_nn(p.astype(BF), v_ref[...]))
        o_ref[...] = jnp.where(ma, outs[0], outs[1]).astype(o_ref.dtype)

    nt = s // MEM_TQ
    return pl.pallas_call(
        body, name=name, grid=(bl, npair, nt),
        in_specs=[pl.BlockSpec((MEM_TQ, LANES), lambda b, p, i: (b * nt + i, qoff + p)),
                  pl.BlockSpec((MEM_LEN, LANES), lambda b, p, i: (b, p)),
                  pl.BlockSpec((MEM_LEN, LANES), lambda b, p, i: (b, npair + p))],
        out_specs=pl.BlockSpec((MEM_TQ, LANES), lambda b, p, i: (b * nt + i, p)),
        out_shape=jax.ShapeDtypeStruct((t, MEM_WIDTH), BF),
        compiler_params=_params("parallel", "parallel", "parallel"),
    )(proj, kvm, kvm)


def mem_bwd(proj, kvm, dcat, bl, *, name):
    t, width = proj.shape
    s = t // bl
    qoff = (width - MEM_WIDTH) // LANES
    doff = (dcat.shape[1] - MEM_WIDTH) // LANES
    npair = MEM_WIDTH // LANES
    nt = s // MEM_TQ

    def body(q_ref, k_ref, v_ref, do_ref, dq_ref, dk_ref, dv_ref):
        @pl.when(pl.program_id(2) == 0)
        def _():
            dk_ref[...] = jnp.zeros_like(dk_ref)
            dv_ref[...] = jnp.zeros_like(dv_ref)

        ma, mb = _lane_masks(MEM_TQ)
        q = q_ref[...].astype(BF)
        do = do_ref[...]
        kb, vb = k_ref[...], v_ref[...]
        dqs = []
        for m in (ma, mb):
            qh = jnp.where(m, q, jnp.zeros_like(q))
            doh = jnp.where(m, do, jnp.zeros_like(do))
            sc = _nt(qh, kb) * SCALE
            p = jnp.exp(sc - jnp.max(sc, axis=-1, keepdims=True))
            p = p * (1.0 / jnp.sum(p, axis=-1, keepdims=True))
            dp = _nt(doh, vb)
            ds = p * (dp - jnp.sum(p * dp, axis=-1, keepdims=True))
            dss = (ds * SCALE).astype(BF)
            dk_ref[...] += _tn(dss, qh)
            dv_ref[...] += _tn(p.astype(BF), doh)
            dqs.append(_nn(dss, kb))
        dq_ref[...] = jnp.where(ma, dqs[0], dqs[1]).astype(dq_ref.dtype)

    kblk = pl.BlockSpec((MEM_LEN, LANES), lambda b, p, i: (b, p))
    dq, dk, dv = pl.pallas_call(
        body, name=name, grid=(bl, npair, nt),
        in_specs=[pl.BlockSpec((MEM_TQ, LANES), lambda b, p, i: (b * nt + i, qoff + p)), kblk,
                  pl.BlockSpec((MEM_LEN, LANES), lambda b, p, i: (b, npair + p)),
                  pl.BlockSpec((MEM_TQ, LANES), lambda b, p, i: (b * nt + i, doff + p))],
        out_specs=[pl.BlockSpec((MEM_TQ, LANES), lambda b, p, i: (b * nt + i, p)), kblk, kblk],
        out_shape=[jax.ShapeDtypeStruct((t, MEM_WIDTH), BF), jax.ShapeDtypeStruct((bl * MEM_LEN, MEM_WIDTH), F32),
                   jax.ShapeDtypeStruct((bl * MEM_LEN, MEM_WIDTH), F32)],
        compiler_params=_params("parallel", "parallel", "arbitrary"),
    )(proj, kvm, kvm, dcat)
    return dq, jnp.concatenate([dk, dv], axis=1).astype(BF)


def _dil_rows(r, u0, size, dil):
    if dil == 1:
        return pl.ds(_mo(u0, QB), size)
    return pl.ds(u0 * dil + r, size, stride=dil)


def _dil_walk(block, dil, nb):
    def residue(r, c):
        block(r, 0, QB)
        if nb > 1:
            def step(n, c2):
                block(r, n, 2 * QB)
                return c2
            lax.fori_loop(1, nb, step, 0, unroll=3)
        return c

    if dil == 1:
        residue(0, 0)
    else:
        lax.fori_loop(0, dil, residue, 0, unroll=4 if nb == 1 else 1)


def _dil_scores(qh, kb, slope_d, n_keys):
    i = lax.broadcasted_iota(jnp.int32, (QB, n_keys), 0)
    j = lax.broadcasted_iota(jnp.int32, (QB, n_keys), 1)
    delta = i + (n_keys - QB) - j
    valid = (delta >= 0) & (delta <= QB)
    sc = _nt(qh, kb) * SCALE - slope_d * delta.astype(F32)
    return jnp.where(valid, sc, NEG)


def _dil_slopes(g, dil):
    p = pl.program_id(1)
    sa = jnp.where(p == 0, ALIBI[4 * g] * dil, ALIBI[4 * g + 2] * dil).astype(F32)
    sb = jnp.where(p == 0, ALIBI[4 * g + 1] * dil, ALIBI[4 * g + 3] * dil).astype(F32)
    return sa, sb


def dil_fwd(projb, kv, g, bl, *, name):
    _, dil = DIL_GROUPS[g]
    t, wq = projb.shape
    wk = kv.shape[1]
    s = t // bl
    ln = s // dil
    nb = ln // QB
    gw = 4 * HEAD_DIM
    ck, co = wk // LANES, gw // LANES

    def body(q_ref, k_ref, v_ref, o_ref, lse_ref):
        sa, sb = _dil_slopes(g, dil)
        ma, mb = _lane_masks(QB)

        def block(r, n, n_keys):
            q0 = n * QB
            k0 = q0 - (n_keys - QB)
            q = q_ref[0, _dil_rows(r, q0, QB, dil), :].astype(BF)
            kb = k_ref[0, _dil_rows(r, k0, n_keys, dil), :].astype(BF)
            vb = v_ref[0, _dil_rows(r, k0, n_keys, dil), :].astype(BF)
            outs, lses = [], []
            for m, sl in ((ma, sa), (mb, sb)):
                qh = jnp.where(m, q, jnp.zeros_like(q))
                sc = _dil_scores(qh, kb, sl, n_keys)
                mx = jnp.max(sc, axis=-1, keepdims=True)
                p = jnp.exp(sc - mx)
                den = jnp.sum(p, axis=-1, keepdims=True)
                outs.append(_nn(p.astype(BF), vb) * (1.0 / den))
                lses.append(mx + jnp.log(den))
            o_ref[0, _dil_rows(r, q0, QB, dil), :] = jnp.where(ma, outs[0], outs[1])
            lse_ref[0, _dil_rows(r, q0, QB, dil), :] = jnp.where(ma, lses[0], lses[1])

        _dil_walk(block, dil, nb)

    colblk = lambda off: pl.BlockSpec((1, s, LANES), lambda b, p: (b, 0, off + p))
    o, lse = pl.pallas_call(
        body, name=name, grid=(bl, co),
        in_specs=[colblk(g * co), colblk(g * co), colblk(ck // 2 + g * co)],
        out_specs=[colblk(0), colblk(0)],
        out_shape=[jax.ShapeDtypeStruct((bl, s, gw), F32), jax.ShapeDtypeStruct((bl, s, gw), F32)],
        compiler_params=_params("parallel", "parallel"),
    )(projb.reshape(bl, s, wq), kv.reshape(bl, s, wk), kv.reshape(bl, s, wk))
    return o.reshape(t, gw), lse.reshape(t, gw)


def dil_bwd(projb, kv, lse, dog, dshift, g, bl, *, name):
    _, dil = DIL_GROUPS[g]
    t, wq = projb.shape
    wk = kv.shape[1]
    s = t // bl
    ln = s // dil
    nb = ln // QB
    gw = 4 * HEAD_DIM
    ck, co = wk // LANES, gw // LANES

    def body(q_ref, k_ref, v_ref, lse_ref, do_ref, sh_ref, dq_ref, dk_ref, dv_ref):
        sa, sb = _dil_slopes(g, dil)
        ma, mb = _lane_masks(QB)
        dk_ref[...] = jnp.zeros_like(dk_ref)
        dv_ref[...] = jnp.zeros_like(dv_ref)

        def block(r, n, n_keys):
            q0 = n * QB
            k0 = q0 - (n_keys - QB)
            qrows, krows = _dil_rows(r, q0, QB, dil), _dil_rows(r, k0, n_keys, dil)
            q = q_ref[0, qrows, :].astype(BF)
            do = do_ref[0, qrows, :].astype(BF)
            lse_b = lse_ref[0, qrows, :]
            sh_b = sh_ref[0, qrows, :]
            kb = k_ref[0, krows, :].astype(BF)
            vb = v_ref[0, krows, :].astype(BF)
            dqs = []
            for m, sl, c0 in ((ma, sa, 0), (mb, sb, HEAD_DIM)):
                qh = jnp.where(m, q, jnp.zeros_like(q))
                doh = jnp.where(m, do, jnp.zeros_like(do))
                sc = _dil_scores(qh, kb, sl, n_keys)
                p = jnp.exp(sc - lse_b[:, c0:c0 + 1])
                ds = p * (_nt(doh, vb) - sh_b[:, c0:c0 + 1])
                dss = (ds * SCALE).astype(BF)
                dk_ref[0, krows, :] += _tn(dss, qh)
                dv_ref[0, krows, :] += _tn(p.astype(BF), doh)
                dqs.append(_nn(dss, kb))
            dq_ref[0, qrows, :] = jnp.where(ma, dqs[0], dqs[1])

        _dil_walk(block, dil, nb)

    colblk = lambda off: pl.BlockSpec((1, s, LANES), lambda b, p: (b, 0, off + p))
    dq, dk, dv = pl.pallas_call(
        body, name=name, grid=(bl, co),
        in_specs=[colblk(g * co), colblk(g * co), colblk(ck // 2 + g * co), colblk(0), colblk(g * co), colblk(g * co)],
        out_specs=[colblk(0)] * 3,
        out_shape=[jax.ShapeDtypeStruct((bl, s, gw), F32)] * 3,
        compiler_params=_params("parallel", "parallel"),
    )(projb.reshape(bl, s, wq), kv.reshape(bl, s, wk), kv.reshape(bl, s, wk), lse.reshape(bl, s, gw),
      dog.reshape(bl, s, DIL_WIDTH), dshift.reshape(bl, s, DIL_WIDTH))
    return dq.reshape(t, gw), dk.reshape(t, gw), dv.reshape(t, gw)


def _group_weights(lses):
    mx = jnp.maximum(jnp.maximum(lses[0], lses[1]), lses[2])
    es = [jnp.exp(l - mx) for l in lses]
    inv = 1.0 / (es[0] + es[1] + es[2])
    return [e * inv for e in es]


def dil_combine_fwd(os_, lses, *, name):
    t, gw = os_[0].shape
    tr = _pick(t, (512, 256))

    def body(o0, o1, o2, l0, l1, l2, out_ref):
        al = _group_weights([l0[...], l1[...], l2[...]])
        for g, o_ref in enumerate((o0, o1, o2)):
            out_ref[:, g * gw:(g + 1) * gw] = (o_ref[...] * al[g]).astype(out_ref.dtype)

    blk = pl.BlockSpec((tr, gw), lambda i: (i, 0))
    return pl.pallas_call(
        body, name=name, grid=(t // tr,), in_specs=[blk] * 6,
        out_specs=pl.BlockSpec((tr, 3 * gw), lambda i: (i, 0)), out_shape=jax.ShapeDtypeStruct((t, 3 * gw), BF),
        compiler_params=_params("parallel"),
    )(*os_, *lses)


def dil_combine_bwd(os_, lses, dcat, *, name):
    t, gw = os_[0].shape
    tr = _pick(t, (512, 256))

    def head_sum(v):
        parts = []
        for c in range(gw // LANES):
            blk = v[:, c * LANES:(c + 1) * LANES]
            ma, _ = _lane_masks(tr)
            sa = jnp.sum(jnp.where(ma, blk, 0.0), axis=1, keepdims=True)
            sb = jnp.sum(blk, axis=1, keepdims=True) - sa
            parts.append(jnp.where(ma, sa, sb))
        return jnp.concatenate(parts, axis=1)

    def body(o0, o1, o2, l0, l1, l2, d_ref, dog_ref, sh_ref):
        al = _group_weights([l0[...], l1[...], l2[...]])
        dos = [d_ref[:, g * gw:(g + 1) * gw].astype(F32) for g in range(3)]
        dal = [head_sum(dos[g] * o_ref[...]) for g, o_ref in enumerate((o0, o1, o2))]
        mix = al[0] * dal[0] + al[1] * dal[1] + al[2] * dal[2]
        for g in range(3):
            dog_ref[:, g * gw:(g + 1) * gw] = (al[g] * dos[g]).astype(dog_ref.dtype)
            sh_ref[:, g * gw:(g + 1) * gw] = al[g] * mix

    blk = pl.BlockSpec((tr, gw), lambda i: (i, 0))
    wide = pl.BlockSpec((tr, 3 * gw), lambda i: (i, 0))
    return pl.pallas_call(
        body, name=name, grid=(t // tr,), in_specs=[blk] * 6 + [wide], out_specs=[wide, wide],
        out_shape=[jax.ShapeDtypeStruct((t, 3 * gw), F32), jax.ShapeDtypeStruct((t, 3 * gw), F32)],
        compiler_params=_params("parallel"),
    )(*os_, *lses, dcat)


def adamw(w, g1, g2, m, v, *, name):
    r, c = w.shape
    tr = r
    for cand in (256, 128, 64, 32, 16, 8):
        if r % cand == 0 and cand * c * 4 <= (1 << 20):
            tr = cand
            break
    two = g2 is not None

    def body(*refs):
        if two:
            w_ref, g1_ref, g2_ref, m_ref, v_ref, g_ref, d_ref, nm_ref, nv_ref = refs
            g = g1_ref[...] + g2_ref[...]
        else:
            w_ref, g1_ref, m_ref, v_ref, g_ref, d_ref, nm_ref, nv_ref = refs
            g = g1_ref[...]
        nm = ADAM_B1 * m_ref[...] + (1.0 - ADAM_B1) * g
        nv = ADAM_B2 * v_ref[...] + (1.0 - ADAM_B2) * (g * g)
        m_hat = nm / (1.0 - ADAM_B1 ** ADAM_STEP)
        v_hat = nv / (1.0 - ADAM_B2 ** ADAM_STEP)
        g_ref[...] = g
        d_ref[...] = -ADAM_LR * (m_hat / (jnp.sqrt(v_hat) + ADAM_EPS) + ADAM_WD * w_ref[...])
        nm_ref[...] = nm
        nv_ref[...] = nv

    blk = pl.BlockSpec((tr, c), lambda i: (i, 0))
    args = [w, g1] + ([g2] if two else []) + [m, v]
    return pl.pallas_call(
        body, name=name, grid=(r // tr,), in_specs=[blk] * len(args), out_specs=[blk] * 4,
        out_shape=[jax.ShapeDtypeStruct((r, c), F32)] * 4, compiler_params=_params("parallel"),
    )(*args)


def sum4(own, land, *, name):
    r, c = own.shape
    tr = _pick(r, (256, 128, 64))

    def body(o_ref, l_ref, s_ref):
        s_ref[...] = ((o_ref[...].astype(F32) + l_ref[0].astype(F32)) + l_ref[1].astype(F32)) + l_ref[2].astype(F32)

    return pl.pallas_call(
        body, name=name, grid=(r // tr,),
        in_specs=[pl.BlockSpec((tr, c), lambda i: (i, 0)), pl.BlockSpec((3, tr, c), lambda i: (0, i, 0))],
        out_specs=pl.BlockSpec((tr, c), lambda i: (i, 0)), out_shape=jax.ShapeDtypeStruct((r, c), F32),
        compiler_params=_params("parallel"),
    )(own, land)


ANY = pl.BlockSpec(memory_space=pl.ANY)


def _place():
    x, y, c = lax.axis_index("x"), lax.axis_index("y"), lax.axis_index("c")
    chips = [(1 - x, y), (x, 1 - y), (1 - x, 1 - y)]
    return x, y, c, chips


class Ride:
    def __init__(self, arrays, out_shape, copies):
        self.arrays, self.out_shape, self.copies = list(arrays), list(out_shape), copies
        n = 3 * len(self.arrays)
        self.scratch = [pltpu.SemaphoreType.DMA((n,)), pltpu.SemaphoreType.DMA((n,)),
                        pltpu.SemaphoreType.DMA((len(self.arrays),))]

    def split(self, refs):
        n, m = len(self.arrays), len(self.out_shape)
        return refs[:n], refs[n:n + m], refs[n + m:]

    def start(self, ins, outs, sems):
        local, sends, _ = self.copies(ins, outs, *sems)
        for cp in local + sends:
            cp.start()

    def wait(self, ins, outs, sems):
        local, sends, arrivals = self.copies(ins, outs, *sems)
        for cp in arrivals:
            cp.wait_recv()
        for cp in sends:
            cp.wait_send()
        for cp in local:
            cp.wait()

    def run(self, name):
        def body(*refs):
            ins, outs, sems = self.split(refs)
            self.start(ins, outs, sems)
            self.wait(ins, outs, sems)

        return pl.pallas_call(body, name=name, in_specs=[ANY] * len(self.arrays), out_specs=[ANY] * len(self.out_shape),
                              out_shape=self.out_shape, scratch_shapes=self.scratch)(*self.arrays)


def gather_ride(shards, axes):
    def copies(ins, outs, send_sems, recv_sems, local_sems):
        x, y, c, chips = _place()

        def slot(a, q):
            size = shards[a].shape[axes[a]]
            start = pl.multiple_of(q * size, size)
            return outs[a].at[pl.ds(start, size), :] if axes[a] == 0 else outs[a].at[:, pl.ds(start, size)]

        def remote(a, k, q):
            px, py = chips[k]
            return pltpu.make_async_remote_copy(src_ref=ins[a], dst_ref=slot(a, q), send_sem=send_sems.at[3 * a + k],
                                                recv_sem=recv_sems.at[3 * a + k], device_id=(px, py, c), device_id_type=MESH)

        me = 2 * x + y
        n = len(shards)
        local = [pltpu.make_async_copy(ins[a], slot(a, me), local_sems.at[a]) for a in range(n)]
        sends = [remote(a, k, me) for a in range(n) for k in range(3)]
        arrivals = [remote(a, k, 2 * chips[k][0] + chips[k][1]) for a in range(n) for k in range(3)]
        return local, sends, arrivals

    out_shape = []
    for a, sh in enumerate(shards):
        full = list(sh.shape)
        full[axes[a]] *= N_CHIPS
        out_shape.append(jax.ShapeDtypeStruct(tuple(full), sh.dtype))
    return Ride(shards, out_shape, copies)


def scatter_ride(grads):
    def copies(ins, outs, send_sems, recv_sems, local_sems):
        x, y, c, chips = _place()
        sends = [pltpu.make_async_remote_copy(src_ref=ins[a].at[2 * px + py], dst_ref=outs[a].at[k],
                                              send_sem=send_sems.at[3 * a + k], recv_sem=recv_sems.at[3 * a + k],
                                              device_id=(px, py, c), device_id_type=MESH)
                 for a in range(len(grads)) for k, (px, py) in enumerate(chips)]
        return [], sends, sends

    return Ride(grads, [jax.ShapeDtypeStruct((3,) + g.shape[1:], g.dtype) for g in grads], copies)


def swap_with_sibling(arrs, *, name):
    n = len(arrs)

    def body(*refs):
        ins, outs = refs[:n], refs[n:2 * n]
        send_sems, recv_sems = refs[2 * n:]
        x, y, c, _ = _place()
        sends = []
        for a in range(n):
            cp = pltpu.make_async_remote_copy(src_ref=ins[a], dst_ref=outs[a], send_sem=send_sems.at[a],
                                              recv_sem=recv_sems.at[a], device_id=(x, y, 1 - c), device_id_type=MESH)
            cp.start()
            sends.append(cp)
        for cp in sends:
            cp.wait_recv()
        for cp in sends:
            cp.wait_send()

    return pl.pallas_call(
        body, name=name, in_specs=[ANY] * n, out_specs=[ANY] * n,
        out_shape=[jax.ShapeDtypeStruct(a.shape, a.dtype) for a in arrs],
        scratch_shapes=[pltpu.SemaphoreType.DMA((n,)), pltpu.SemaphoreType.DMA((n,))],
    )(*arrs)


def all_reduce_small(v, *, name):
    rows = v.shape[0]

    def body(v_ref, o_ref, gath, send_sems, recv_sems):
        x, y, c, _ = _place()
        me = 4 * x + 2 * y + c
        gath[me] = v_ref[...]
        sends = []
        for msk in range(1, N_DEV):
            peer = (x ^ (msk >> 2), y ^ ((msk >> 1) & 1), c ^ (msk & 1))
            cp = pltpu.make_async_remote_copy(src_ref=v_ref, dst_ref=gath.at[me], send_sem=send_sems.at[msk - 1],
                                              recv_sem=recv_sems.at[msk - 1], device_id=peer, device_id_type=MESH)
            cp.start()
            sends.append(cp)
        for msk in range(1, N_DEV):
            pltpu.make_async_remote_copy(src_ref=v_ref, dst_ref=gath.at[me ^ msk], send_sem=send_sems.at[msk - 1],
                                         recv_sem=recv_sems.at[msk - 1], device_id=(x, y, c), device_id_type=MESH).wait_recv()
        for cp in sends:
            cp.wait_send()
        tot = gath[0]
        for q in range(1, N_DEV):
            tot = tot + gath[q]
        o_ref[...] = tot

    vm = pl.BlockSpec(memory_space=pltpu.VMEM)
    return pl.pallas_call(
        body, name=name, in_specs=[vm], out_specs=vm, out_shape=jax.ShapeDtypeStruct(v.shape, F32),
        scratch_shapes=[pltpu.VMEM((N_DEV, rows, LANES), F32), pltpu.SemaphoreType.DMA((N_DEV - 1,)),
                        pltpu.SemaphoreType.DMA((N_DEV - 1,))],
    )(v)


def _ffn_fwd(xin, gain, w_up, w_conv, w_down, bl, tag, rides=(None, None, None)):
    got = {}

    def carried(k, result, delivered):
        if rides[k] is not None:
            got.update(zip(rides[k][1], delivered))
        return result

    n = rmsnorm_fwd(xin, gain, name=f"{tag}_ffn_norm")
    if rides[0] is None:
        u = matmul(n, w_up, name=f"{tag}_ffn_up")
    else:
        u = carried(0, *matmul(n, w_up, ride=rides[0][0], name=f"{tag}_ffn_up"))
    *hc, delivered = conv_silu_fwd(u, w_conv, bl, name=f"{tag}_ffn_conv", ride=rides[1][0] if rides[1] else None)
    h, ca, cg = carried(1, hc, delivered)
    if rides[2] is None:
        xout = matmul(h, w_down, out_dtype=F32, add=xin, name=f"{tag}_ffn_down")
    else:
        xout = carried(2, *matmul(h, w_down, out_dtype=F32, add=xin, ride=rides[2][0], name=f"{tag}_ffn_down"))
    return xout, (n, u, h, ca, cg), got


def _ffn_bwd(dxout, dxout_b, xin, saved, gain, w_up, w_conv, w_down, bl, tag, shard):
    n, u, h, ca, cg = saved
    dh = matmul(dxout_b, w_down, tb=True, name=f"{tag}_ffn_down_dx")
    g_down = matmul(h, dxout_b, ta=True, name=f"{tag}_ffn_down_dw")
    dua, dug, g_conv = conv_silu_bwd(u, ca, cg, w_conv, dh, bl, name=f"{tag}_ffn_conv_bwd")
    dn = matmul(dua, w_up, a2=dug, tb=True, out_dtype=F32, name=f"{tag}_ffn_up_dx")
    g_up = matmul(n, dua, b2=dug, ta=True, shard_cols=shard, name=f"{tag}_ffn_up_dw")
    dxin, dxin_b, g_norm = rmsnorm_bwd(xin, gain, dn, dxout, name=f"{tag}_ffn_norm_bwd")
    return dxin, dxin_b, g_norm, g_up, g_conv, g_down


def _mem_kv(mem2, gain, w_kv, tag):
    mn = rmsnorm_fwd(mem2, gain, name=f"{tag}_mem_norm")
    return mn, matmul(mn, w_kv, name=f"{tag}_mem_kv")


def _mem_kv_bwd(mem2, mn, gain, w_kv, dkvm, tag):
    g_kv = matmul(mn, dkvm, ta=True, name=f"{tag}_mem_kv_dw")
    dmn = matmul(dkvm, w_kv, tb=True, out_dtype=F32, name=f"{tag}_mem_kv_dx")
    _, _, g_norm = rmsnorm_bwd(mem2, gain, dmn, None, name=f"{tag}_mem_norm_bwd")
    return g_norm, g_kv


def _shard_major(name, grad):
    if name in COL_SHARDED:
        return grad
    return grad.reshape(N_CHIPS, grad.shape[0] // N_CHIPS, grad.shape[1])


def local_step(x, mem, target, w, shard_of, late_gather=None, early_scatter=()):
    bl, s, d = x.shape
    t = bl * s
    x0 = x.reshape(t, d)
    mem2 = mem.reshape(bl * MEM_LEN, d)
    tgt = target.reshape(t, d)
    g = {}

    n1 = rmsnorm_fwd(x0, w["a_norm_attn"], name="a_attn_norm")
    proj_a = matmul(n1, w["a_w_in"], name="a_w_in")
    mn_a, kvm_a = _mem_kv(mem2, w["a_norm_mem"], w["a_w_mem_kv"], "a")
    late_gather = late_gather or {}
    behind_sb = late_gather.get("sb")
    o_sb, o_sb_f32, delivered = sb_fwd(proj_a, bl, name="a_sb_fwd", ride=behind_sb[0] if behind_sb else None)
    if behind_sb:
        w = {**w, **dict(zip(behind_sb[1], delivered))}
    o_mem_a = mem_fwd(proj_a, kvm_a, bl, name="a_mem_fwd")
    cat_a = jnp.concatenate([o_sb, o_mem_a], axis=1)
    x1 = matmul(cat_a, w["a_w_out"], out_dtype=F32, add=x0, name="a_w_out")
    x2, ffn_a, got = _ffn_fwd(x1, w["a_norm_ffn"], w["a_ffn_up"], w["a_ffn_conv"], w["a_ffn_down"], bl, "a",
                              rides=tuple(late_gather.get(host) for host in ("up", "conv", "down")))
    w = {**w, **got}

    nkv = rmsnorm_fwd(x2, w["kv_norm"], name="kv_norm")
    kv = matmul(nkv, w["w_kv_shared"], out_dtype=F32, name="w_kv")
    n3 = rmsnorm_fwd(x2, w["b_norm_attn"], name="b_attn_norm")
    proj_b = matmul(n3, w["b_w_in"], out_dtype=F32, name="b_w_in")
    mn_b, kvm_b = _mem_kv(mem2, w["b_norm_mem"], w["b_w_mem_kv"], "b")
    dil = [dil_fwd(proj_b, kv, gi, bl, name=f"b_dil_fwd{gi}") for gi in range(3)]
    os_, lses = [o for o, _ in dil], [l for _, l in dil]
    o_dil = dil_combine_fwd(os_, lses, name="b_dil_combine")
    o_mem_b = mem_fwd(proj_b, kvm_b, bl, name="b_mem_fwd")
    cat_b = jnp.concatenate([o_dil, o_mem_b], axis=1)
    x3 = matmul(cat_b, w["b_w_out"], out_dtype=F32, add=x2, name="b_w_out")
    x4, ffn_b, _ = _ffn_fwd(x3, w["b_norm_ffn"], w["b_ffn_up"], w["b_ffn_conv"], w["b_ffn_down"], bl, "b")

    dx4, dx4b, g["final_norm"], lossvec = final_loss(x4, w["final_norm"], tgt, name="final_loss")

    dx3, dx3b, g["b_norm_ffn"], g["b_ffn_up"], g["b_ffn_conv"], g["b_ffn_down"] = _ffn_bwd(
        dx4, dx4b, x3, ffn_b, w["b_norm_ffn"], w["b_ffn_up"], w["b_ffn_conv"], w["b_ffn_down"], bl, "b",
        shard_of["b_ffn_up"])
    dcat_b = matmul(dx3b, w["b_w_out"], tb=True, name="b_w_out_dx")
    g["b_w_out"] = matmul(cat_b, dx3b, ta=True, name="b_w_out_dw")
    dog, dshift = dil_combine_bwd(os_, lses, dcat_b, name="b_dil_combine_bwd")
    dqs, dks, dvs = [], [], []
    for gi in range(3):
        dq_g, dk_g, dv_g = dil_bwd(proj_b, kv, lses[gi], dog, dshift, gi, bl, name=f"b_dil_bwd{gi}")
        dqs.append(dq_g), dks.append(dk_g), dvs.append(dv_g)
    dq_mem_b, dkvm_b = mem_bwd(proj_b, kvm_b, dcat_b, bl, name="b_mem_bwd")
    g["b_norm_mem"], g["b_w_mem_kv"] = _mem_kv_bwd(mem2, mn_b, w["b_norm_mem"], w["b_w_mem_kv"], dkvm_b, "b")
    dproj_b = jnp.concatenate([dq_g.astype(BF) for dq_g in dqs] + [dq_mem_b], axis=1)
    dn3 = matmul(dproj_b, w["b_w_in"], tb=True, out_dtype=F32, name="b_w_in_dx")
    g["b_w_in"] = matmul(n3, dproj_b, ta=True, name="b_w_in_dw")
    dx2, _, g["b_norm_attn"] = rmsnorm_bwd(x2, w["b_norm_attn"], dn3, dx3, name="b_attn_norm_bwd")
    dkv = jnp.concatenate(dks + dvs, axis=1).astype(BF)
    dnkv = matmul(dkv, w["w_kv_shared"], tb=True, out_dtype=F32, name="w_kv_dx")
    g["w_kv_shared"] = matmul(nkv, dkv, ta=True, shard_cols=shard_of["w_kv_shared"], name="w_kv_dw")
    dx2, dx2b, g["kv_norm"] = rmsnorm_bwd(x2, w["kv_norm"], dnkv, dx2, name="kv_norm_bwd")

    dx1, dx1b, g["a_norm_ffn"], g["a_ffn_up"], g["a_ffn_conv"], g["a_ffn_down"] = _ffn_bwd(
        dx2, dx2b, x1, ffn_a, w["a_norm_ffn"], w["a_ffn_up"], w["a_ffn_conv"], w["a_ffn_down"], bl, "a",
        shard_of["a_ffn_up"])
    dcat_a = matmul(dx1b, w["a_w_out"], tb=True, name="a_w_out_dx")
    g["a_w_out"] = matmul(cat_a, dx1b, ta=True, name="a_w_out_dw")
    leaving = scatter_ride([_shard_major(n, g[n]) for n in early_scatter]) if early_scatter else None
    dq_sb, dk_sb, dv_sb, landed = sb_bwd(proj_a, o_sb_f32, dcat_a, bl, name="a_sb_bwd", ride=leaving)
    landed = dict(zip(early_scatter, landed))
    dq_mem_a, dkvm_a = mem_bwd(proj_a, kvm_a, dcat_a, bl, name="a_mem_bwd")
    g["a_norm_mem"], g["a_w_mem_kv"] = _mem_kv_bwd(mem2, mn_a, w["a_norm_mem"], w["a_w_mem_kv"], dkvm_a, "a")
    dproj_a = jnp.concatenate([dq_sb, dk_sb.astype(BF), dv_sb.astype(BF), dq_mem_a], axis=1)
    dn1 = matmul(dproj_a, w["a_w_in"], tb=True, out_dtype=F32, name="a_w_in_dx")
    g["a_w_in"] = matmul(n1, dproj_a, ta=True, shard_cols=shard_of["a_w_in"], name="a_w_in_dw")
    dx0, _, g["a_norm_attn"] = rmsnorm_bwd(x0, w["a_norm_attn"], dn1, dx1, name="a_attn_norm_bwd")
    return lossvec, dx0, g, landed


MATRICES = ("a_w_in", "a_w_out", "a_w_mem_kv", "a_ffn_up", "a_ffn_down", "w_kv_shared", "b_w_in", "b_w_out",
            "b_w_mem_kv", "b_ffn_up", "b_ffn_down")
COL_SHARDED = ("a_w_in", "a_ffn_up", "w_kv_shared", "b_ffn_up")
FIRST_NEEDED = ("a_w_in", "a_w_mem_kv")
GATHER_BEHIND = {"sb": ("a_w_out", "a_ffn_up", "a_ffn_down", "b_ffn_up"), "up": ("b_ffn_down", "w_kv_shared"),
                 "conv": ("b_w_in", "b_w_out", "b_w_mem_kv")}
SMALL_SHARDED = ("a_norm_attn", "a_norm_mem", "a_norm_ffn", "a_ffn_conv", "b_ffn_conv")
SMALL_REPLICATED = ("kv_norm", "b_norm_attn", "b_norm_mem", "b_norm_ffn", "final_norm")
WEIGHTS = ("a_norm_attn", "a_w_in", "a_w_out", "a_norm_mem", "a_w_mem_kv", "a_norm_ffn", "a_ffn_up", "a_ffn_conv",
           "a_ffn_down", "kv_norm", "w_kv_shared", "b_norm_attn", "b_w_in", "b_w_out", "b_norm_mem", "b_w_mem_kv",
           "b_norm_ffn", "b_ffn_up", "b_ffn_conv", "b_ffn_down", "final_norm")


def _two_d(a):
    if a.ndim == 1:
        return a.reshape(1, -1)
    return a.reshape(a.shape[-2], a.shape[-1])


def kernel(x, mem, a_norm_attn, a_w_in, a_w_out, a_norm_mem, a_w_mem_kv, a_norm_ffn, a_ffn_up, a_ffn_conv, a_ffn_down, kv_norm, w_kv_shared, b_norm_attn, b_w_in, b_w_out, b_norm_mem, b_w_mem_kv, b_norm_ffn, b_ffn_up, b_ffn_conv, b_ffn_down, final_norm, loss_target, m_a_norm_attn, m_a_w_in, m_a_w_out, m_a_norm_mem, m_a_w_mem_kv, m_a_norm_ffn, m_a_ffn_up, m_a_ffn_conv, m_a_ffn_down, m_kv_norm, m_w_kv_shared, m_b_norm_attn, m_b_w_in, m_b_w_out, m_b_norm_mem, m_b_w_mem_kv, m_b_norm_ffn, m_b_ffn_up, m_b_ffn_conv, m_b_ffn_down, m_final_norm, v_a_norm_attn, v_a_w_in, v_a_w_out, v_a_norm_mem, v_a_w_mem_kv, v_a_norm_ffn, v_a_ffn_up, v_a_ffn_conv, v_a_ffn_down, v_kv_norm, v_w_kv_shared, v_b_norm_attn, v_b_w_in, v_b_w_out, v_b_norm_mem, v_b_w_mem_kv, v_b_norm_ffn, v_b_ffn_up, v_b_ffn_conv, v_b_ffn_down, v_final_norm):
    given = dict(locals())
    wl = {n: _two_d(given[n]) for n in WEIGHTS}
    ml = {n: _two_d(given["m_" + n]) for n in WEIGHTS}
    vl = {n: _two_d(given["v_" + n]) for n in WEIGHTS}
    chip = 2 * lax.axis_index("x") + lax.axis_index("y")

    packed = jnp.concatenate([wl[n].reshape(-1, LANES) for n in SMALL_SHARDED], axis=0)
    axis_of = lambda n: 1 if n in COL_SHARDED else 0
    late = tuple(n for n in MATRICES if n not in FIRST_NEEDED)
    full = gather_ride([wl[n].astype(BF) for n in FIRST_NEEDED] + [packed],
                       [axis_of(n) for n in FIRST_NEEDED] + [0]).run("gather_first")
    w = dict(zip(FIRST_NEEDED, full[:-1]))
    late_gather = {host: (gather_ride([wl[n].astype(BF) for n in names], [axis_of(n) for n in names]), names)
                   for host, names in GATHER_BEHIND.items()}
    rows = packed.shape[0]
    per_chip = full[-1].reshape(N_CHIPS, rows, LANES)
    r0 = 0
    for n in SMALL_SHARDED:
        nr = wl[n].size // LANES
        piece = per_chip[:, r0:r0 + nr].reshape(N_CHIPS, wl[n].shape[0], wl[n].shape[1])
        w[n] = jnp.concatenate([piece[q] for q in range(N_CHIPS)], axis=1)
        r0 += nr
    for n in SMALL_REPLICATED:
        w[n] = wl[n]

    shard_of = {n: wl[n].shape[1] for n in COL_SHARDED}
    lossvec, dx0, g, landed = local_step(x, mem, loss_target, w, shard_of, late_gather=late_gather, early_scatter=late)
    loss = lax.psum(0.5 * jnp.sum(lossvec) / x.shape[-1], ("x", "y", "c"))

    g4 = {n: _shard_major(n, g[n]) for n in MATRICES}
    landed.update(zip(FIRST_NEEDED, scatter_ride([g4[n] for n in FIRST_NEEDED]).run("scatter_last")))
    sums = [sum4(lax.dynamic_index_in_dim(g4[n], chip, 0, keepdims=False), landed[n], name=f"sum4_{n}")
            for n in MATRICES]
    theirs = swap_with_sibling(sums, name="swap_sums")
    out = {}
    for k, n in enumerate(MATRICES):
        out[n] = adamw(wl[n], sums[k], theirs[k], ml[n], vl[n], name=f"adamw_{n}")

    small = SMALL_SHARDED + SMALL_REPLICATED
    flat = jnp.concatenate([g[n].reshape(-1, LANES) for n in small], axis=0)
    tot = all_reduce_small(flat, name="all_reduce_small")
    r0 = 0
    for n in small:
        nr = g[n].size // LANES
        gn = tot[r0:r0 + nr].reshape(g[n].shape)
        r0 += nr
        if n in SMALL_SHARDED:
            gn = lax.dynamic_slice_in_dim(gn, chip * wl[n].shape[1], wl[n].shape[1], axis=1)
        out[n] = adamw(wl[n], gn, None, ml[n], vl[n], name=f"adamw_{n}")

    res = [loss, dx0.reshape(x.shape)]
    for slot in range(4):
        res += [out[n][slot].reshape(given[n].shape) for n in WEIGHTS]
    return tuple(res)
```

```python
import functools
import math

import jax
import jax.numpy as jnp
from jax import lax
from jax.experimental import pallas as pl
from jax.experimental.pallas import tpu as pltpu

F32 = jnp.float32
BF = jnp.bfloat16
MESH = pl.DeviceIdType.MESH

HEAD_DIM = 64
LANES = 128
SB_WIDTH = 12 * HEAD_DIM
MEM_WIDTH = 4 * HEAD_DIM
DIL_WIDTH = 12 * HEAD_DIM
MEM_LEN = 256
DIL_GROUPS = ((128, 1), (512, 4), (2048, 16))
QB = 128
EPS = 1e-6
SCALE = HEAD_DIM ** -0.5
NEG = -1e30
ALIBI = tuple(2.0 ** (-8.0 * i / 12) for i in range(1, 13))
N_CHIPS = 4
N_DEV = 8

ADAM_LR, ADAM_B1, ADAM_B2, ADAM_EPS, ADAM_WD, ADAM_STEP = 0.001, 0.9, 0.999, 1e-08, 0.01, 10

VMEM_LIMIT = 52 * 1024 * 1024
MATMUL_VMEM_BUDGET = 44 * 1024 * 1024
MATMUL_MAX_TN = 2816
HBM_BYTES_PER_S = 3.0e12
VMEM_BYTES_PER_S = 6.0e12
STEP_OVERHEAD_S = 0.4e-6


def _mo(v, m):
    return v if isinstance(v, int) else pl.multiple_of(v, m)


def _pick(n, prefs):
    for t in prefs:
        if n % t == 0:
            return t
    return n


def _params(*sem):
    return pltpu.CompilerParams(dimension_semantics=sem, vmem_limit_bytes=VMEM_LIMIT)


def matmul(a, b, *, ta=False, tb=False, out_dtype=BF, add=None, shard_cols=0, a2=None, b2=None, ride=None, name):
    m, k = (a.shape[1], a.shape[0]) if ta else a.shape
    n = b.shape[0] if tb else b.shape[1]
    if a2 is not None:
        assert not ta and a2.shape == a.shape
        k *= 2
    if b2 is not None:
        assert not tb and b2.shape == b.shape
        n *= 2
    k_piece = a.shape[1] if a2 is not None else k
    n_piece = b.shape[1] if b2 is not None else n
    out_bytes = jnp.dtype(out_dtype).itemsize
    pieces_a, pieces_b = (2 if a2 is not None else 1), (2 if b2 is not None else 1)
    if shard_cols:
        tn_choices = [c * shard_cols for c in (4, 2, 1) if c * shard_cols <= MATMUL_MAX_TN and c <= 4 // pieces_b]
    else:
        tn_choices = [c for c in (2816, 2560, 1536, 1408, 1024, 512, 256, 128) if n_piece % c == 0]

    def vmem_bytes(tm_, tn_, tk_):
        blocks = (tm_ * tk_ * a.dtype.itemsize * pieces_a + tk_ * tn_ * b.dtype.itemsize * pieces_b
                  + tm_ * tn_ * out_bytes + (tm_ * tn_ * 4 if add is not None else 0))
        return 2 * blocks + (tm_ * tn_ * 4 if k // tk_ > 1 else 0) + tm_ * tn_ * 4

    def cost(c):
        tm_, tn_, tk_ = c
        steps = (m // tm_) * (n // tn_) * (k // tk_)
        hbm = (m * k * a.dtype.itemsize * (n // tn_) + k * n * b.dtype.itemsize * (m // tm_)
               + m * n * (out_bytes + (4 if add is not None else 0)))
        return hbm / HBM_BYTES_PER_S + steps * STEP_OVERHEAD_S + m * n * 8 * (k // tk_ - 1) / VMEM_BYTES_PER_S

    tm, tn, tk = min(((tm_, tn_, tk_) for tm_ in (1024, 512, 1408, 256, 128) if m % tm_ == 0 for tn_ in tn_choices
                      for tk_ in (2816, 2560, 2048, 1536, 1024, 512, 256, 128) if k_piece % tk_ == 0
                      if vmem_bytes(tm_, tn_, tk_) <= MATMUL_VMEM_BUDGET), key=cost)
    ns = tn // shard_cols if shard_cols else 1
    nk = k // tk
    nk1 = nk // 2
    nj1 = (n // tn) // 2
    dims = (((0,) if ta else (1,), (1,) if tb else (0,)), ((), ()))
    n_in = 2 + (a2 is not None) + (b2 is not None) + (add is not None)
    grid = (m // tm, n // tn, nk)

    def body(*refs):
        own_in, (o_ref,), scr, ride_start, ride_wait = _ride_hooks(ride, refs, n_in, 1, int(nk > 1), grid)
        ride_start()
        ins = list(own_in)
        a_ref, b_ref = ins.pop(0), ins.pop(0)
        a2_ref = ins.pop(0) if a2 is not None else None
        b2_ref = ins.pop(0) if b2 is not None else None
        add_ref = ins.pop(0) if add is not None else None

        a_tile, b_tile = a_ref[...], b_ref[...]
        if a2 is not None:
            a_tile = jnp.where(pl.program_id(2) < nk1, a_tile, a2_ref[...])
        if b2 is not None:
            b_tile = jnp.where(pl.program_id(1) < nj1, b_tile, b2_ref[...])
        part = lax.dot_general(a_tile.astype(BF), b_tile.astype(BF), dims, preferred_element_type=F32)

        def finish(r):
            if add is not None:
                r = r + add_ref[...]
            if shard_cols:
                for c in range(ns):
                    o_ref[c] = r[:, c * shard_cols:(c + 1) * shard_cols].astype(o_ref.dtype)
            else:
                o_ref[...] = r.astype(o_ref.dtype)

        if nk == 1:
            finish(part)
        else:
            acc_ref = scr[0]
            kk = pl.program_id(2)

            @pl.when(kk == 0)
            def _():
                acc_ref[...] = part

            @pl.when(kk > 0)
            def _():
                acc_ref[...] += part

            @pl.when(kk == nk - 1)
            def _():
                finish(acc_ref[...])
        ride_wait()

    a_spec = pl.BlockSpec((tk, tm), lambda i, j, q: (q, i)) if ta else pl.BlockSpec((tm, tk), lambda i, j, q: (i, q))
    b_spec = pl.BlockSpec((tn, tk), lambda i, j, q: (j, q)) if tb else pl.BlockSpec((tk, tn), lambda i, j, q: (q, j))
    if a2 is not None:
        a_spec = pl.BlockSpec((tm, tk), lambda i, j, q: (i, jnp.minimum(q, nk1 - 1)))
    if b2 is not None:
        b_spec = pl.BlockSpec((tk, tn), lambda i, j, q: (jnp.where(j < nj1, q, 0), jnp.minimum(j, nj1 - 1)))
    in_specs = [a_spec, b_spec]
    args = [a, b]
    if a2 is not None:
        in_specs.append(pl.BlockSpec((tm, tk), lambda i, j, q: (i, jnp.maximum(q - nk1, 0))))
        args.append(a2)
    if b2 is not None:
        in_specs.append(pl.BlockSpec((tk, tn), lambda i, j, q: (jnp.where(j < nj1, 0, q), jnp.maximum(j - nj1, 0))))
        args.append(b2)
    if add is not None:
        in_specs.append(pl.BlockSpec((tm, tn), lambda i, j, q: (i, j)))
        args.append(add)
    if shard_cols:
        out_shape = jax.ShapeDtypeStruct((N_CHIPS, m, shard_cols), out_dtype)
        out_spec = pl.BlockSpec((ns, tm, shard_cols), lambda i, j, q: (j, i, 0))
    else:
        out_shape = jax.ShapeDtypeStruct((m, n), out_dtype)
        out_spec = pl.BlockSpec((tm, tn), lambda i, j, q: (i, j))
    own_scratch = [pltpu.VMEM((tm, tn), F32)] if nk > 1 else []
    if ride is None:
        return pl.pallas_call(
            body, name=name, grid=grid, in_specs=in_specs, out_specs=out_spec, out_shape=out_shape,
            scratch_shapes=own_scratch, compiler_params=_params("parallel", "parallel", "arbitrary"),
        )(*args)
    out, *delivered = pl.pallas_call(
        body, name=name, grid=grid, in_specs=in_specs + [ANY] * len(ride.arrays),
        out_specs=[out_spec] + [ANY] * len(ride.out_shape), out_shape=[out_shape] + ride.out_shape,
        scratch_shapes=own_scratch + ride.scratch, compiler_params=_params("arbitrary", "arbitrary", "arbitrary"),
    )(*args, *ride.arrays)
    return out, delivered


def rmsnorm_fwd(x, g, *, name):
    t, d = x.shape
    tr = _pick(t, (512, 256))

    def body(x_ref, g_ref, o_ref):
        xv = x_ref[...]
        r = lax.rsqrt(jnp.mean(xv * xv, axis=-1, keepdims=True) + EPS)
        o_ref[...] = ((xv * r) * g_ref[...]).astype(o_ref.dtype)

    return pl.pallas_call(
        body, name=name, grid=(t // tr,),
        in_specs=[pl.BlockSpec((tr, d), lambda i: (i, 0)), pl.BlockSpec((1, d), lambda i: (0, 0))],
        out_specs=pl.BlockSpec((tr, d), lambda i: (i, 0)), out_shape=jax.ShapeDtypeStruct((t, d), BF),
        compiler_params=_params("parallel"),
    )(x, g)


def rmsnorm_bwd(x, g, dn, dres, *, name):
    t, d = x.shape
    tr = _pick(t, (512, 256))
    want_dx = dres is not None

    def body(*refs):
        if want_dx:
            x_ref, g_ref, dn_ref, dres_ref, dx_ref, dxb_ref, dg_ref = refs
        else:
            x_ref, g_ref, dn_ref, dg_ref = refs

        @pl.when(pl.program_id(0) == 0)
        def _():
            dg_ref[...] = jnp.zeros_like(dg_ref)

        xv = x_ref[...]
        r = lax.rsqrt(jnp.mean(xv * xv, axis=-1, keepdims=True) + EPS)
        xn = xv * r
        dnv = dn_ref[...].astype(F32)
        dg_ref[...] += jnp.sum(dnv * xn, axis=0, keepdims=True)
        if want_dx:
            dyg = dnv * g_ref[...]
            cm = jnp.mean(dyg * xn, axis=-1, keepdims=True)
            dx = dres_ref[...] + r * (dyg - xn * cm)
            dx_ref[...] = dx
            dxb_ref[...] = dx.astype(BF)

    row = pl.BlockSpec((tr, d), lambda i: (i, 0))
    vec = pl.BlockSpec((1, d), lambda i: (0, 0))
    if want_dx:
        return pl.pallas_call(
            body, name=name, grid=(t // tr,), in_specs=[row, vec, row, row], out_specs=[row, row, vec],
            out_shape=[jax.ShapeDtypeStruct((t, d), F32), jax.ShapeDtypeStruct((t, d), BF), jax.ShapeDtypeStruct((1, d), F32)],
            compiler_params=_params("arbitrary"),
        )(x, g, dn, dres)
    return None, None, pl.pallas_call(
        body, name=name, grid=(t // tr,), in_specs=[row, vec, row], out_specs=vec,
        out_shape=jax.ShapeDtypeStruct((1, d), F32), compiler_params=_params("arbitrary"),
    )(x, g, dn)


def rmsnorm_fwd2(x, g1, g2, *, name):
    t, d = x.shape
    tr = _pick(t, (512, 256))

    def body(x_ref, g1_ref, g2_ref, o1_ref, o2_ref):
        xv = x_ref[...]
        xn = xv * lax.rsqrt(jnp.mean(xv * xv, axis=-1, keepdims=True) + EPS)
        o1_ref[...] = (xn * g1_ref[...]).astype(o1_ref.dtype)
        o2_ref[...] = (xn * g2_ref[...]).astype(o2_ref.dtype)

    row = pl.BlockSpec((tr, d), lambda i: (i, 0))
    vec = pl.BlockSpec((1, d), lambda i: (0, 0))
    return pl.pallas_call(
        body, name=name, grid=(t // tr,), in_specs=[row, vec, vec], out_specs=[row, row],
        out_shape=[jax.ShapeDtypeStruct((t, d), BF)] * 2, compiler_params=_params("parallel"),
    )(x, g1, g2)


def rmsnorm_bwd2(x, g1, g2, dn1, dn2, dres, *, name):
    t, d = x.shape
    tr = _pick(t, (512, 256))

    def body(x_ref, g1_ref, g2_ref, dn1_ref, dn2_ref, dres_ref, dx_ref, dxb_ref, dg1_ref, dg2_ref):
        @pl.when(pl.program_id(0) == 0)
        def _():
            dg1_ref[...] = jnp.zeros_like(dg1_ref)
            dg2_ref[...] = jnp.zeros_like(dg2_ref)

        xv = x_ref[...]
        r = lax.rsqrt(jnp.mean(xv * xv, axis=-1, keepdims=True) + EPS)
        xn = xv * r
        d1, d2 = dn1_ref[...].astype(F32), dn2_ref[...].astype(F32)
        dg1_ref[...] += jnp.sum(d1 * xn, axis=0, keepdims=True)
        dg2_ref[...] += jnp.sum(d2 * xn, axis=0, keepdims=True)
        dyg = d1 * g1_ref[...] + d2 * g2_ref[...]
        cm = jnp.mean(dyg * xn, axis=-1, keepdims=True)
        dx = dres_ref[...] + r * (dyg - xn * cm)
        dx_ref[...] = dx
        dxb_ref[...] = dx.astype(BF)

    row = pl.BlockSpec((tr, d), lambda i: (i, 0))
    vec = pl.BlockSpec((1, d), lambda i: (0, 0))
    return pl.pallas_call(
        body, name=name, grid=(t // tr,), in_specs=[row, vec, vec, row, row, row], out_specs=[row, row, vec, vec],
        out_shape=[jax.ShapeDtypeStruct((t, d), F32), jax.ShapeDtypeStruct((t, d), BF), jax.ShapeDtypeStruct((1, d), F32),
                   jax.ShapeDtypeStruct((1, d), F32)],
        compiler_params=_params("arbitrary"),
    )(x, g1, g2, dn1, dn2, dres)


def final_loss(x, g, target, *, name):
    t, d = x.shape
    tr = _pick(t, (512, 256))

    def body(x_ref, g_ref, t_ref, dx_ref, dxb_ref, dg_ref, lv_ref):
        @pl.when(pl.program_id(0) == 0)
        def _():
            dg_ref[...] = jnp.zeros_like(dg_ref)
            lv_ref[...] = jnp.zeros_like(lv_ref)

        xv = x_ref[...]
        r = lax.rsqrt(jnp.mean(xv * xv, axis=-1, keepdims=True) + EPS)
        xn = xv * r
        err = xn * g_ref[...] - t_ref[...]
        lv_ref[...] += jnp.sum(err * err, axis=0, keepdims=True)
        dy = err * (1.0 / d)
        dg_ref[...] += jnp.sum(dy * xn, axis=0, keepdims=True)
        dyg = dy * g_ref[...]
        cm = jnp.mean(dyg * xn, axis=-1, keepdims=True)
        dx = r * (dyg - xn * cm)
        dx_ref[...] = dx
        dxb_ref[...] = dx.astype(BF)

    row = pl.BlockSpec((tr, d), lambda i: (i, 0))
    vec = pl.BlockSpec((1, d), lambda i: (0, 0))
    return pl.pallas_call(
        body, name=name, grid=(t // tr,), in_specs=[row, vec, row], out_specs=[row, row, vec, vec],
        out_shape=[jax.ShapeDtypeStruct((t, d), F32), jax.ShapeDtypeStruct((t, d), BF), jax.ShapeDtypeStruct((1, d), F32),
                   jax.ShapeDtypeStruct((1, d), F32)],
        compiler_params=_params("arbitrary"),
    )(x, g, target)


CONV_ROWS = 256
HALO = 16


def _conv_taps(ext, w, rows):
    s0 = ext[HALO:HALO + rows]
    s1 = pltpu.roll(ext, 1, 0)[HALO:HALO + rows]
    s2 = pltpu.roll(ext, 2, 0)[HALO:HALO + rows]
    return (w[0:1] * s2 + w[1:2] * s1) + w[2:3] * s0, s0, s1, s2


def conv_silu_fwd(u, w, bl, *, name, ride=None):
    t, f2 = u.shape
    f = f2 // 2
    s = t // bl
    tc = _pick(f, (256, 128))
    nf = f // tc
    nch = s // CONV_ROWS

    def body(*refs):
        (ua_ref, ug_ref, wa_ref, wg_ref), (h_ref, ca_ref, cg_ref), _, ride_start, ride_wait = _ride_hooks(
            ride, refs, 4, 3, 0, (nf, bl))
        ride_start()
        wa, wg = wa_ref[...], wg_ref[...]

        def chunk(ci, carry):
            r0 = pl.multiple_of(ci * CONV_ROWS, CONV_ROWS)
            ps = pl.multiple_of(jnp.maximum(r0 - HALO, 0), HALO)
            keep = (ci > 0).astype(F32)
            rows = pl.ds(r0, CONV_ROWS)

            def conv(ref, wv):
                ext = jnp.concatenate([ref[0, pl.ds(ps, HALO), :].astype(F32) * keep, ref[0, rows, :].astype(F32)], axis=0)
                return _conv_taps(ext, wv, CONV_ROWS)[0]

            ca, cg = conv(ua_ref, wa), conv(ug_ref, wg)
            sg = pl.reciprocal(1.0 + jnp.exp(-cg), approx=True)
            h_ref[0, rows, :] = ((cg * sg) * ca).astype(h_ref.dtype)
            ca_ref[0, rows, :] = ca.astype(ca_ref.dtype)
            cg_ref[0, rows, :] = cg.astype(cg_ref.dtype)
            return carry

        lax.fori_loop(0, nch, chunk, 0)
        ride_wait()

    u3 = u.reshape(bl, s, f2)
    blk = lambda off: pl.BlockSpec((1, s, tc), lambda j, b: (b, 0, j + off))
    wblk = lambda off: pl.BlockSpec((3, tc), lambda j, b: (0, j + off))
    extra = ride.arrays if ride else []
    h, ca, cg, *delivered = pl.pallas_call(
        body, name=name, grid=(nf, bl), in_specs=[blk(0), blk(nf), wblk(0), wblk(nf)] + [ANY] * len(extra),
        out_specs=[blk(0)] * 3 + [ANY] * (len(ride.out_shape) if ride else 0),
        out_shape=[jax.ShapeDtypeStruct((bl, s, f), BF)] * 3 + (ride.out_shape if ride else []),
        scratch_shapes=ride.scratch if ride else [],
        compiler_params=_params(*(("arbitrary",) * 2 if ride else ("parallel",) * 2)),
    )(u3, u3, w, w, *extra)
    return h.reshape(t, f), ca.reshape(t, f), cg.reshape(t, f), delivered


def conv_silu_bwd(u, ca, cg, w, dh, bl, *, name):
    t, f2 = u.shape
    f = f2 // 2
    s = t // bl
    tc = _pick(f, (256, 128))
    nf = f // tc
    nch = s // CONV_ROWS
    ext_rows = CONV_ROWS + HALO

    def body(ua_ref, ug_ref, ca_ref, cg_ref, wa_ref, wg_ref, dh_ref, dua_ref, dug_ref, dwa_ref, dwg_ref):
        wa, wg = wa_ref[...], wg_ref[...]

        @pl.when(pl.program_id(1) == 0)
        def _():
            dwa_ref[...] = jnp.zeros_like(dwa_ref)
            dwg_ref[...] = jnp.zeros_like(dwg_ref)

        def chunk(ci, carry):
            r0 = pl.multiple_of(ci * CONV_ROWS, CONV_ROWS)
            ns = pl.multiple_of(jnp.minimum(r0 + CONV_ROWS, s - HALO), HALO)
            keep_n = (ci < nch - 1).astype(F32)
            rows = pl.ds(r0, CONV_ROWS)

            def ext_of(ref):
                return jnp.concatenate([ref[0, rows, :].astype(F32), ref[0, pl.ds(ns, HALO), :].astype(F32)], axis=0)

            dhe = jnp.concatenate([dh_ref[0, rows, :].astype(F32), dh_ref[0, pl.ds(ns, HALO), :].astype(F32) * keep_n],
                                  axis=0)
            cae, cge = ext_of(ca_ref), ext_of(cg_ref)
            sg = pl.reciprocal(1.0 + jnp.exp(-cge), approx=True)
            dca = dhe * (cge * sg)
            dcg = dhe * cae * (sg * (1.0 + cge * (1.0 - sg)))

            def back(dc, wv, u_ref, du_ref, dw_ref):
                d0 = dc[:CONV_ROWS]
                n1 = pltpu.roll(dc, ext_rows - 1, 0)[:CONV_ROWS]
                n2 = pltpu.roll(dc, ext_rows - 2, 0)[:CONV_ROWS]
                du_ref[0, rows, :] = ((wv[2:3] * d0 + wv[1:2] * n1) + wv[0:1] * n2).astype(du_ref.dtype)
                uc = u_ref[0, rows, :].astype(F32)
                for k, shifted in enumerate((n2, n1, d0)):
                    dw_ref[k:k + 1, :] += jnp.sum(shifted * uc, axis=0, keepdims=True)

            back(dca, wa, ua_ref, dua_ref, dwa_ref)
            back(dcg, wg, ug_ref, dug_ref, dwg_ref)
            return carry

        lax.fori_loop(0, nch, chunk, 0)

    u3 = u.reshape(bl, s, f2)
    blk = lambda off: pl.BlockSpec((1, s, tc), lambda j, b: (b, 0, j + off))
    wblk = lambda off: pl.BlockSpec((3, tc), lambda j, b: (0, j + off))
    dua, dug, dwa, dwg = pl.pallas_call(
        body, name=name, grid=(nf, bl), in_specs=[blk(0), blk(nf), blk(0), blk(0), wblk(0), wblk(nf), blk(0)],
        out_specs=[blk(0), blk(0), wblk(0), wblk(0)],
        out_shape=[jax.ShapeDtypeStruct((bl, s, f), BF), jax.ShapeDtypeStruct((bl, s, f), BF),
                   jax.ShapeDtypeStruct((3, f), F32), jax.ShapeDtypeStruct((3, f), F32)],
        compiler_params=_params("parallel", "arbitrary"),
    )(u3, u3, ca.reshape(bl, s, f), cg.reshape(bl, s, f), w, w, dh.reshape(bl, s, f))
    return dua.reshape(t, f), dug.reshape(t, f), jnp.concatenate([dwa, dwg], axis=1)


def _lane_masks(rows):
    lane = lax.broadcasted_iota(jnp.int32, (rows, LANES), 1)
    return lane < HEAD_DIM, lane >= HEAD_DIM


def _nt(a, b):
    return lax.dot_general(a, b, (((1,), (1,)), ((), ())), preferred_element_type=F32)


def _tn(a, b):
    return lax.dot_general(a, b, (((0,), (0,)), ((), ())), preferred_element_type=F32)


def _nn(a, b):
    return jnp.dot(a, b, preferred_element_type=F32)


def _suffix_ones():
    j = lax.broadcasted_iota(jnp.int32, (2 * QB, QB), 0) % QB
    s = lax.broadcasted_iota(jnp.int32, (2 * QB, QB), 1)
    return (j >= s).astype(BF)


def _softplus(z):
    return jnp.maximum(z, 0.0) + jnp.log(1.0 + jnp.exp(-jnp.abs(z)))


SB_BLOCK = 256
SB_DEAD = -100.0


SB_STRIP = 32
SB_SUB = SB_BLOCK // QB


def _sb_scratch(backward):
    blk = (2, SB_BLOCK, SB_BLOCK)
    per_head = pltpu.VMEM((2, SB_BLOCK, LANES), F32)
    scr = [pltpu.VMEM((2 * QB, QB), BF),
           pltpu.VMEM(blk, F32),
           pltpu.VMEM((2, SB_SUB) + blk[1:], BF),
           pltpu.VMEM(blk, F32),
           pltpu.VMEM(blk, BF),
           per_head]
    if not backward:
        return scr + [per_head]
    return scr + [pltpu.VMEM(blk, F32),
                  pltpu.VMEM(blk, F32),
                  pltpu.VMEM(blk, F32),
                  pltpu.VMEM(blk, BF),
                  per_head,
                  per_head,
                  per_head]


def _sb_strips(fn):
    for r in range(SB_BLOCK // SB_STRIP):
        fn(pl.ds(r * SB_STRIP, SB_STRIP), r)


def _sb_keep(diag, r, v):
    if not diag:
        return v
    rel = (lax.broadcasted_iota(jnp.int32, (SB_STRIP, SB_BLOCK), 1)
           - lax.broadcasted_iota(jnp.int32, (SB_STRIP, SB_BLOCK), 0))
    return jnp.where(rel < r * SB_STRIP, v, 0.0)


def _sb_split_store(sp_scr, rows, v):
    hi = v.astype(BF)
    lo = (v - hi.astype(F32)).astype(BF)
    for u in range(SB_SUB):
        sp_scr[u, rows, 0:QB] = hi[:, u * QB:(u + 1) * QB]
        sp_scr[u, rows, QB:2 * QB] = lo[:, u * QB:(u + 1) * QB]


def _sb_suffix_sums(uu_scr, sp_scr, out_scr):
    for u in range(SB_SUB):
        out_scr[:, u * QB:(u + 1) * QB] = _nn(sp_scr[u], uu_scr[...])


def _sb_fold(sums, off):
    offs = [None] * SB_SUB
    for u in reversed(range(SB_SUB)):
        offs[u] = off
        off = off + jnp.broadcast_to(sums[:, u * QB:u * QB + 1], (SB_STRIP, LANES))
    return offs, off


def _sb_first(diag, z_scr, sp_scr, ls_scr=None):
    def strip(rows, r):
        ls = _sb_keep(diag, r, -_softplus(z_scr[rows, :]))
        if ls_scr is not None:
            ls_scr[rows, :] = ls
        _sb_split_store(sp_scr, rows, ls)

    _sb_strips(strip)


def _sb_second(diag, z_scr, cin_scr, w_scr, carry_scr):
    def strip(rows, r):
        z, cin = z_scr[rows, :], cin_scr[rows, :]
        offs, carry_scr[rows, :] = _sb_fold(cin, carry_scr[rows, :])
        e = jnp.concatenate([z[:, u * QB:(u + 1) * QB] + cin[:, u * QB:(u + 1) * QB] + offs[u] for u in range(SB_SUB)],
                            axis=1)
        w_scr[rows, :] = _sb_keep(diag, r, jnp.exp(e)).astype(BF)

    _sb_strips(strip)


def _sb_third(w_scr, dw_scr, da_scr, sp_scr):
    def strip(rows, r):
        da = w_scr[rows, :].astype(F32) * dw_scr[rows, :]
        da_scr[rows, :] = da
        _sb_split_store(sp_scr, rows, da)

    _sb_strips(strip)


def _sb_fourth(diag, z_scr, ls_scr, da_scr, sin_scr, dz_scr, carry_s_scr, dsum_scr):
    def strip(rows, r):
        da, sin = da_scr[rows, :], sin_scr[rows, :]
        offs, carry_s_scr[rows, :] = _sb_fold(sin, carry_s_scr[rows, :])
        dsum = dsum_scr[rows, :]
        pre = jnp.concatenate([dsum - (sin[:, u * QB:(u + 1) * QB] - da[:, u * QB:(u + 1) * QB] + offs[u])
                               for u in range(SB_SUB)], axis=1)
        sig = jnp.exp(z_scr[rows, :] + ls_scr[rows, :])
        dz_scr[rows, :] = _sb_keep(diag, r, da - sig * pre).astype(BF)

    _sb_strips(strip)


def _sb_alive(carry_scr):
    return (jnp.max(carry_scr[...]) > SB_DEAD).astype(jnp.int32)


def _ride_hooks(ride, refs, n_in, n_out, n_scratch, grid):
    if ride is None:
        return refs[:n_in], refs[n_in:n_in + n_out], refs[n_in + n_out:], lambda: None, lambda: None
    ni, no = len(ride.arrays), len(ride.out_shape)
    own_in, rin = refs[:n_in], refs[n_in:n_in + ni]
    own_out, rout = refs[n_in + ni:n_in + ni + n_out], refs[n_in + ni + n_out:n_in + ni + n_out + no]
    rest = refs[n_in + ni + n_out + no:]
    own_scr, sems = rest[:n_scratch], rest[n_scratch:]
    ids = [pl.program_id(a) for a in range(len(grid))]

    def start():
        first = functools.reduce(lambda u, v: u & v, [i == 0 for i in ids])
        pl.when(first)(lambda: ride.start(rin, rout, sems))

    def wait():
        last = functools.reduce(lambda u, v: u & v, [i == n - 1 for i, n in zip(ids, grid)])
        pl.when(last)(lambda: ride.wait(rin, rout, sems))

    return own_in, own_out, own_scr, start, wait


def sb_fwd(proj, bl, *, name, ride=None):
    t, width = proj.shape
    s = t // bl
    npair = SB_WIDTH // LANES
    nq = s // SB_BLOCK
    grid = (bl, npair, nq)

    own_scratch = _sb_scratch(backward=False)

    def body(*refs):
        (q_ref, k_ref, v_ref), (o_ref, of_ref), scr, ride_start, ride_wait = _ride_hooks(
            ride, refs, 3, 2, len(own_scratch), grid)
        uu_scr, z_scr, sp_scr, cin_scr, w_scr, carry_scr, acc_scr = scr
        ride_start()
        i = pl.program_id(2)
        uu_scr[...] = _suffix_ones()
        carry_scr[...] = jnp.zeros_like(carry_scr)
        acc_scr[...] = jnp.zeros_like(acc_scr)
        qs = (q_ref[0].astype(F32) * SCALE).astype(BF)
        masks = _lane_masks(SB_BLOCK)

        z, sp, cin, w, car, acc = ([r.at[h] for h in range(2)] for r in (z_scr, sp_scr, cin_scr, w_scr, carry_scr, acc_scr))

        def live(c, diag):
            start = pl.multiple_of(c * SB_BLOCK, SB_BLOCK)
            kb, vb = k_ref[0, pl.ds(start, SB_BLOCK), :], v_ref[0, pl.ds(start, SB_BLOCK), :]
            for h in range(2):
                z[h][...] = _nt(jnp.where(masks[h], qs, jnp.zeros_like(qs)), kb)
            _sb_first(diag, z[0], sp[0])
            _sb_suffix_sums(uu_scr, sp[0], cin[0])
            _sb_first(diag, z[1], sp[1])
            _sb_suffix_sums(uu_scr, sp[1], cin[1])
            _sb_second(diag, z[0], cin[0], w[0], car[0])
            acc[0][...] += _nn(w[0][...], vb)
            _sb_second(diag, z[1], cin[1], w[1], car[1])
            acc[1][...] += _nn(w[1][...], vb)
            return _sb_alive(carry_scr)

        def step(n, alive):
            return lax.cond(alive > 0, lambda: live(i - n, False), lambda: alive)

        lax.fori_loop(1, i + 1, step, live(i, True))
        both = jnp.where(masks[0], acc_scr[0], acc_scr[1])
        o_ref[0] = both.astype(o_ref.dtype)
        of_ref[0] = both
        ride_wait()

    p3 = proj.reshape(bl, s, width)
    qblk = pl.BlockSpec((1, SB_BLOCK, LANES), lambda b, p, i: (b, i, p))
    extra = ride.arrays if ride else []
    o, of, *delivered = pl.pallas_call(
        body, name=name, grid=grid,
        in_specs=[qblk, pl.BlockSpec((1, s, LANES), lambda b, p, i: (b, 0, npair + p)),
                  pl.BlockSpec((1, s, LANES), lambda b, p, i: (b, 0, 2 * npair + p))] + [ANY] * len(extra),
        out_specs=[qblk, qblk] + [ANY] * (len(ride.out_shape) if ride else 0),
        out_shape=[jax.ShapeDtypeStruct((bl, s, SB_WIDTH), BF), jax.ShapeDtypeStruct((bl, s, SB_WIDTH), F32)]
        + (ride.out_shape if ride else []),
        scratch_shapes=own_scratch + (ride.scratch if ride else []),
        compiler_params=_params("arbitrary", "arbitrary", "arbitrary"),
    )(p3, p3, p3, *extra)
    return o.reshape(t, SB_WIDTH), of.reshape(t, SB_WIDTH), delivered


def sb_bwd(proj, o, dcat, bl, *, name, ride=None):
    t, width = proj.shape
    s = t // bl
    npair = SB_WIDTH // LANES
    nq = s // SB_BLOCK
    grid = (bl, npair, nq)
    own_scratch = _sb_scratch(backward=True)

    def body(*refs):
        (q_ref, k_ref, v_ref, o_ref, do_ref), (dq_ref, dk_ref, dv_ref), scr, ride_start, ride_wait = _ride_hooks(
            ride, refs, 5, 3, len(own_scratch), grid)
        (uu_scr, z_scr, sp_scr, cin_scr, w_scr, carry_scr,
         ls_scr, dw_scr, da_scr, dz_scr, carry_s_scr, dsum_scr, dq_scr) = scr
        ride_start()
        i = pl.program_id(2)

        @pl.when(i == 0)
        def _():
            dk_ref[...] = jnp.zeros_like(dk_ref)
            dv_ref[...] = jnp.zeros_like(dv_ref)

        uu_scr[...] = _suffix_ones()
        for ref in (carry_scr, carry_s_scr, dq_scr):
            ref[...] = jnp.zeros_like(ref)
        masks = _lane_masks(SB_BLOCK)
        qs = (q_ref[0].astype(F32) * SCALE).astype(BF)
        do = do_ref[0]
        prod = do.astype(F32) * o_ref[0]
        for h in range(2):
            dsum_scr[h] = jnp.broadcast_to(jnp.sum(jnp.where(masks[h], prod, 0.0), axis=1, keepdims=True),
                                           (SB_BLOCK, LANES))

        z, sp, cin, w, car, ls, dw, da, dz, cars, dsum, dq = (
            [r.at[h] for h in range(2)] for r in (z_scr, sp_scr, cin_scr, w_scr, carry_scr, ls_scr, dw_scr, da_scr,
                                                  dz_scr, carry_s_scr, dsum_scr, dq_scr))

        def live(c, diag):
            start = pl.multiple_of(c * SB_BLOCK, SB_BLOCK)
            keys = pl.ds(start, SB_BLOCK)
            kb, vb = k_ref[0, keys, :], v_ref[0, keys, :]
            qh = [jnp.where(masks[h], qs, jnp.zeros_like(qs)) for h in range(2)]
            doh = [jnp.where(masks[h], do, jnp.zeros_like(do)) for h in range(2)]
            for h in range(2):
                z[h][...] = _nt(qh[h], kb)
                dw[h][...] = _nt(doh[h], vb)

            def weights(h):
                _sb_second(diag, z[h], cin[h], w[h], car[h])
                _sb_third(w[h], dw[h], da[h], sp[h])
                _sb_suffix_sums(uu_scr, sp[h], cin[h])

            def grads(h):
                _sb_fourth(diag, z[h], ls[h], da[h], cin[h], dz[h], cars[h], dsum[h])
                dk_ref[0, keys, :] += _tn(dz[h][...], qh[h])
                dv_ref[0, keys, :] += _tn(w[h][...], doh[h])
                dq[h][...] += _nn(dz[h][...], kb)

            _sb_first(diag, z[0], sp[0], ls[0])
            _sb_suffix_sums(uu_scr, sp[0], cin[0])
            _sb_first(diag, z[1], sp[1], ls[1])
            _sb_suffix_sums(uu_scr, sp[1], cin[1])
            weights(0)
            weights(1)
            grads(0)
            grads(1)
            return _sb_alive(carry_scr)

        def step(n, alive):
            return lax.cond(alive > 0, lambda: live(i - n, False), lambda: alive)

        lax.fori_loop(1, i + 1, step, live(i, True))
        dq_ref[0] = (jnp.where(masks[0], dq_scr[0], dq_scr[1]) * SCALE).astype(dq_ref.dtype)
        ride_wait()

    p3 = proj.reshape(bl, s, width)
    o3 = o.reshape(bl, s, SB_WIDTH)
    d3 = dcat.reshape(bl, s, dcat.shape[1])
    qblk = pl.BlockSpec((1, SB_BLOCK, LANES), lambda b, p, i: (b, i, p))
    full = pl.BlockSpec((1, s, LANES), lambda b, p, i: (b, 0, p))
    extra = ride.arrays if ride else []
    dq, dk, dv, *delivered = pl.pallas_call(
        body, name=name, grid=grid,
        in_specs=[qblk, pl.BlockSpec((1, s, LANES), lambda b, p, i: (b, 0, npair + p)),
                  pl.BlockSpec((1, s, LANES), lambda b, p, i: (b, 0, 2 * npair + p)), qblk, qblk] + [ANY] * len(extra),
        out_specs=[qblk, full, full] + [ANY] * (len(ride.out_shape) if ride else 0),
        out_shape=[jax.ShapeDtypeStruct((bl, s, SB_WIDTH), BF), jax.ShapeDtypeStruct((bl, s, SB_WIDTH), F32),
                   jax.ShapeDtypeStruct((bl, s, SB_WIDTH), F32)] + (ride.out_shape if ride else []),
        scratch_shapes=own_scratch + (ride.scratch if ride else []),
        compiler_params=_params("arbitrary", "arbitrary", "arbitrary"),
    )(p3, p3, p3, o3, d3, *extra)
    return dq.reshape(t, SB_WIDTH), dk.reshape(t, SB_WIDTH), dv.reshape(t, SB_WIDTH), delivered


MEM_TQ = 512


def mem_fwd(proj, kvm, bl, *, name):
    t, width = proj.shape
    s = t // bl
    qoff = (width - MEM_WIDTH) // LANES
    npair = MEM_WIDTH // LANES

    def body(q_ref, k_ref, v_ref, o_ref):
        ma, mb = _lane_masks(MEM_TQ)
        q = q_ref[...].astype(BF)
        outs = []
        for m in (ma, mb):
            qh = jnp.where(m, q, jnp.zeros_like(q))
            sc = _nt(qh, k_ref[...]) * SCALE
            p = jnp.exp(sc - jnp.max(sc, axis=-1, keepdims=True))
            p = p * (1.0 / jnp.sum(p, axis=-1, keepdims=True))
            outs.append(_nn(p.astype(BF), v_ref[...]))
        o_ref[...] = jnp.where(ma, outs[0], outs[1]).astype(o_ref.dtype)

    nt = s // MEM_TQ
    return pl.pallas_call(
        body, name=name, grid=(bl, npair, nt),
        in_specs=[pl.BlockSpec((MEM_TQ, LANES), lambda b, p, i: (b * nt + i, qoff + p)),
                  pl.BlockSpec((MEM_LEN, LANES), lambda b, p, i: (b, p)),
                  pl.BlockSpec((MEM_LEN, LANES), lambda b, p, i: (b, npair + p))],
        out_specs=pl.BlockSpec((MEM_TQ, LANES), lambda b, p, i: (b * nt + i, p)),
        out_shape=jax.ShapeDtypeStruct((t, MEM_WIDTH), BF),
        compiler_params=_params("parallel", "parallel", "parallel"),
    )(proj, kvm, kvm)


def mem_bwd(proj, kvm, dcat, bl, *, name):
    t, width = proj.shape
    s = t // bl
    qoff = (width - MEM_WIDTH) // LANES
    doff = (dcat.shape[1] - MEM_WIDTH) // LANES
    npair = MEM_WIDTH // LANES
    nt = s // MEM_TQ

    def body(q_ref, k_ref, v_ref, do_ref, dq_ref, dk_ref, dv_ref):
        @pl.when(pl.program_id(2) == 0)
        def _():
            dk_ref[...] = jnp.zeros_like(dk_ref)
            dv_ref[...] = jnp.zeros_like(dv_ref)

        ma, mb = _lane_masks(MEM_TQ)
        q = q_ref[...].astype(BF)
        do = do_ref[...]
        kb, vb = k_ref[...], v_ref[...]
        dqs = []
        for m in (ma, mb):
            qh = jnp.where(m, q, jnp.zeros_like(q))
            doh = jnp.where(m, do, jnp.zeros_like(do))
            sc = _nt(qh, kb) * SCALE
            p = jnp.exp(sc - jnp.max(sc, axis=-1, keepdims=True))
            p = p * (1.0 / jnp.sum(p, axis=-1, keepdims=True))
            dp = _nt(doh, vb)
            ds = p * (dp - jnp.sum(p * dp, axis=-1, keepdims=True))
            dss = (ds * SCALE).astype(BF)
            dk_ref[...] += _tn(dss, qh)
            dv_ref[...] += _tn(p.astype(BF), doh)
            dqs.append(_nn(dss, kb))
        dq_ref[...] = jnp.where(ma, dqs[0], dqs[1]).astype(dq_ref.dtype)

    kblk = pl.BlockSpec((MEM_LEN, LANES), lambda b, p, i: (b, p))
    dq, dk, dv = pl.pallas_call(
        body, name=name, grid=(bl, npair, nt),
        in_specs=[pl.BlockSpec((MEM_TQ, LANES), lambda b, p, i: (b * nt + i, qoff + p)), kblk,
                  pl.BlockSpec((MEM_LEN, LANES), lambda b, p, i: (b, npair + p)),
                  pl.BlockSpec((MEM_TQ, LANES), lambda b, p, i: (b * nt + i, doff + p))],
        out_specs=[pl.BlockSpec((MEM_TQ, LANES), lambda b, p, i: (b * nt + i, p)), kblk, kblk],
        out_shape=[jax.ShapeDtypeStruct((t, MEM_WIDTH), BF), jax.ShapeDtypeStruct((bl * MEM_LEN, MEM_WIDTH), F32),
                   jax.ShapeDtypeStruct((bl * MEM_LEN, MEM_WIDTH), F32)],
        compiler_params=_params("parallel", "parallel", "arbitrary"),
    )(proj, kvm, kvm, dcat)
    return dq, jnp.concatenate([dk, dv], axis=1).astype(BF)


def _dil_rows(r, u0, size, dil):
    if dil == 1:
        return pl.ds(_mo(u0, QB), size)
    return pl.ds(u0 * dil + r, size, stride=dil)


def _dil_walk(block, dil, nb):
    def residue(r, c):
        block(r, 0, QB)
        if nb > 1:
            def step(n, c2):
                block(r, n, 2 * QB)
                return c2
            lax.fori_loop(1, nb, step, 0, unroll=3)
        return c

    if dil == 1:
        residue(0, 0)
    else:
        lax.fori_loop(0, dil, residue, 0, unroll=4 if nb == 1 else 1)


def _dil_scores(qh, kb, slope_d, n_keys):
    i = lax.broadcasted_iota(jnp.int32, (QB, n_keys), 0)
    j = lax.broadcasted_iota(jnp.int32, (QB, n_keys), 1)
    delta = i + (n_keys - QB) - j
    valid = (delta >= 0) & (delta <= QB)
    sc = _nt(qh, kb) * SCALE - slope_d * delta.astype(F32)
    return jnp.where(valid, sc, NEG)


def _dil_slopes(g, dil):
    p = pl.program_id(1)
    sa = jnp.where(p == 0, ALIBI[4 * g] * dil, ALIBI[4 * g + 2] * dil).astype(F32)
    sb = jnp.where(p == 0, ALIBI[4 * g + 1] * dil, ALIBI[4 * g + 3] * dil).astype(F32)
    return sa, sb


def dil_fwd(projb, kv, g, bl, *, name):
    _, dil = DIL_GROUPS[g]
    t, wq = projb.shape
    wk = kv.shape[1]
    s = t // bl
    ln = s // dil
    nb = ln // QB
    gw = 4 * HEAD_DIM
    ck, co = wk // LANES, gw // LANES

    def body(q_ref, k_ref, v_ref, o_ref, lse_ref):
        sa, sb = _dil_slopes(g, dil)
        ma, mb = _lane_masks(QB)

        def block(r, n, n_keys):
            q0 = n * QB
            k0 = q0 - (n_keys - QB)
            q = q_ref[0, _dil_rows(r, q0, QB, dil), :].astype(BF)
            kb = k_ref[0, _dil_rows(r, k0, n_keys, dil), :].astype(BF)
            vb = v_ref[0, _dil_rows(r, k0, n_keys, dil), :].astype(BF)
            outs, lses = [], []
            for m, sl in ((ma, sa), (mb, sb)):
                qh = jnp.where(m, q, jnp.zeros_like(q))
                sc = _dil_scores(qh, kb, sl, n_keys)
                mx = jnp.max(sc, axis=-1, keepdims=True)
                p = jnp.exp(sc - mx)
                den = jnp.sum(p, axis=-1, keepdims=True)
                outs.append(_nn(p.astype(BF), vb) * (1.0 / den))
                lses.append(mx + jnp.log(den))
            o_ref[0, _dil_rows(r, q0, QB, dil), :] = jnp.where(ma, outs[0], outs[1])
            lse_ref[0, _dil_rows(r, q0, QB, dil), :] = jnp.where(ma, lses[0], lses[1])

        _dil_walk(block, dil, nb)

    colblk = lambda off: pl.BlockSpec((1, s, LANES), lambda b, p: (b, 0, off + p))
    o, lse = pl.pallas_call(
        body, name=name, grid=(bl, co),
        in_specs=[colblk(g * co), colblk(g * co), colblk(ck // 2 + g * co)],
        out_specs=[colblk(0), colblk(0)],
        out_shape=[jax.ShapeDtypeStruct((bl, s, gw), F32), jax.ShapeDtypeStruct((bl, s, gw), F32)],
        compiler_params=_params("parallel", "parallel"),
    )(projb.reshape(bl, s, wq), kv.reshape(bl, s, wk), kv.reshape(bl, s, wk))
    return o.reshape(t, gw), lse.reshape(t, gw)


def dil_bwd(projb, kv, lse, dog, dshift, g, bl, *, name):
    _, dil = DIL_GROUPS[g]
    t, wq = projb.shape
    wk = kv.shape[1]
    s = t // bl
    ln = s // dil
    nb = ln // QB
    gw = 4 * HEAD_DIM
    ck, co = wk // LANES, gw // LANES

    def body(q_ref, k_ref, v_ref, lse_ref, do_ref, sh_ref, dq_ref, dk_ref, dv_ref):
        sa, sb = _dil_slopes(g, dil)
        ma, mb = _lane_masks(QB)
        dk_ref[...] = jnp.zeros_like(dk_ref)
        dv_ref[...] = jnp.zeros_like(dv_ref)

        def block(r, n, n_keys):
            q0 = n * QB
            k0 = q0 - (n_keys - QB)
            qrows, krows = _dil_rows(r, q0, QB, dil), _dil_rows(r, k0, n_keys, dil)
            q = q_ref[0, qrows, :].astype(BF)
            do = do_ref[0, qrows, :].astype(BF)
            lse_b = lse_ref[0, qrows, :]
            sh_b = sh_ref[0, qrows, :]
            kb = k_ref[0, krows, :].astype(BF)
            vb = v_ref[0, krows, :].astype(BF)
            dqs = []
            for m, sl, c0 in ((ma, sa, 0), (mb, sb, HEAD_DIM)):
                qh = jnp.where(m, q, jnp.zeros_like(q))
                doh = jnp.where(m, do, jnp.zeros_like(do))
                sc = _dil_scores(qh, kb, sl, n_keys)
                p = jnp.exp(sc - lse_b[:, c0:c0 + 1])
                ds = p * (_nt(doh, vb) - sh_b[:, c0:c0 + 1])
                dss = (ds * SCALE).astype(BF)
                dk_ref[0, krows, :] += _tn(dss, qh)
                dv_ref[0, krows, :] += _tn(p.astype(BF), doh)
                dqs.append(_nn(dss, kb))
            dq_ref[0, qrows, :] = jnp.where(ma, dqs[0], dqs[1])

        _dil_walk(block, dil, nb)

    colblk = lambda off: pl.BlockSpec((1, s, LANES), lambda b, p: (b, 0, off + p))
    dq, dk, dv = pl.pallas_call(
        body, name=name, grid=(bl, co),
        in_specs=[colblk(g * co), colblk(g * co), colblk(ck // 2 + g * co), colblk(0), colblk(g * co), colblk(g * co)],
        out_specs=[colblk(0)] * 3,
        out_shape=[jax.ShapeDtypeStruct((bl, s, gw), F32)] * 3,
        compiler_params=_params("parallel", "parallel"),
    )(projb.reshape(bl, s, wq), kv.reshape(bl, s, wk), kv.reshape(bl, s, wk), lse.reshape(bl, s, gw),
      dog.reshape(bl, s, DIL_WIDTH), dshift.reshape(bl, s, DIL_WIDTH))
    return dq.reshape(t, gw), dk.reshape(t, gw), dv.reshape(t, gw)


def _group_weights(lses):
    mx = jnp.maximum(jnp.maximum(lses[0], lses[1]), lses[2])
    es = [jnp.exp(l - mx) for l in lses]
    inv = 1.0 / (es[0] + es[1] + es[2])
    return [e * inv for e in es]


def dil_combine_fwd(os_, lses, *, name):
    t, gw = os_[0].shape
    tr = _pick(t, (512, 256))

    def body(o0, o1, o2, l0, l1, l2, out_ref):
        al = _group_weights([l0[...], l1[...], l2[...]])
        for g, o_ref in enumerate((o0, o1, o2)):
            out_ref[:, g * gw:(g + 1) * gw] = (o_ref[...] * al[g]).astype(out_ref.dtype)

    blk = pl.BlockSpec((tr, gw), lambda i: (i, 0))
    return pl.pallas_call(
        body, name=name, grid=(t // tr,), in_specs=[blk] * 6,
        out_specs=pl.BlockSpec((tr, 3 * gw), lambda i: (i, 0)), out_shape=jax.ShapeDtypeStruct((t, 3 * gw), BF),
        compiler_params=_params("parallel"),
    )(*os_, *lses)


def dil_combine_bwd(os_, lses, dcat, *, name):
    t, gw = os_[0].shape
    tr = _pick(t, (512, 256))

    def head_sum(v):
        parts = []
        for c in range(gw // LANES):
            blk = v[:, c * LANES:(c + 1) * LANES]
            ma, _ = _lane_masks(tr)
            sa = jnp.sum(jnp.where(ma, blk, 0.0), axis=1, keepdims=True)
            sb = jnp.sum(blk, axis=1, keepdims=True) - sa
            parts.append(jnp.where(ma, sa, sb))
        return jnp.concatenate(parts, axis=1)

    def body(o0, o1, o2, l0, l1, l2, d_ref, dog_ref, sh_ref):
        al = _group_weights([l0[...], l1[...], l2[...]])
        dos = [d_ref[:, g * gw:(g + 1) * gw].astype(F32) for g in range(3)]
        dal = [head_sum(dos[g] * o_ref[...]) for g, o_ref in enumerate((o0, o1, o2))]
        mix = al[0] * dal[0] + al[1] * dal[1] + al[2] * dal[2]
        for g in range(3):
            dog_ref[:, g * gw:(g + 1) * gw] = (al[g] * dos[g]).astype(dog_ref.dtype)
            sh_ref[:, g * gw:(g + 1) * gw] = al[g] * mix

    blk = pl.BlockSpec((tr, gw), lambda i: (i, 0))
    wide = pl.BlockSpec((tr, 3 * gw), lambda i: (i, 0))
    return pl.pallas_call(
        body, name=name, grid=(t // tr,), in_specs=[blk] * 6 + [wide], out_specs=[wide, wide],
        out_shape=[jax.ShapeDtypeStruct((t, 3 * gw), F32), jax.ShapeDtypeStruct((t, 3 * gw), F32)],
        compiler_params=_params("parallel"),
    )(*os_, *lses, dcat)


def adamw(w, g1, g2, m, v, *, name):
    r, c = w.shape
    tr = r
    for cand in (256, 128, 64, 32, 16, 8):
        if r % cand == 0 and cand * c * 4 <= (1 << 20):
            tr = cand
            break
    two = g2 is not None

    def body(*refs):
        if two:
            w_ref, g1_ref, g2_ref, m_ref, v_ref, g_ref, d_ref, nm_ref, nv_ref = refs
            g = g1_ref[...] + g2_ref[...]
        else:
            w_ref, g1_ref, m_ref, v_ref, g_ref, d_ref, nm_ref, nv_ref = refs
            g = g1_ref[...]
        nm = ADAM_B1 * m_ref[...] + (1.0 - ADAM_B1) * g
        nv = ADAM_B2 * v_ref[...] + (1.0 - ADAM_B2) * (g * g)
        m_hat = nm / (1.0 - ADAM_B1 ** ADAM_STEP)
        v_hat = nv / (1.0 - ADAM_B2 ** ADAM_STEP)
        g_ref[...] = g
        d_ref[...] = -ADAM_LR * (m_hat / (jnp.sqrt(v_hat) + ADAM_EPS) + ADAM_WD * w_ref[...])
        nm_ref[...] = nm
        nv_ref[...] = nv

    blk = pl.BlockSpec((tr, c), lambda i: (i, 0))
    args = [w, g1] + ([g2] if two else []) + [m, v]
    return pl.pallas_call(
        body, name=name, grid=(r // tr,), in_specs=[blk] * len(args), out_specs=[blk] * 4,
        out_shape=[jax.ShapeDtypeStruct((r, c), F32)] * 4, compiler_params=_params("parallel"),
    )(*args)


def sum4(own, land, *, name):
    r, c = own.shape
    tr = _pick(r, (256, 128, 64))

    def body(o_ref, l_ref, s_ref):
        s_ref[...] = ((o_ref[...].astype(F32) + l_ref[0].astype(F32)) + l_ref[1].astype(F32)) + l_ref[2].astype(F32)

    return pl.pallas_call(
        body, name=name, grid=(r // tr,),
        in_specs=[pl.BlockSpec((tr, c), lambda i: (i, 0)), pl.BlockSpec((3, tr, c), lambda i: (0, i, 0))],
        out_specs=pl.BlockSpec((tr, c), lambda i: (i, 0)), out_shape=jax.ShapeDtypeStruct((r, c), F32),
        compiler_params=_params("parallel"),
    )(own, land)


ANY = pl.BlockSpec(memory_space=pl.ANY)


def _place():
    x, y, c = lax.axis_index("x"), lax.axis_index("y"), lax.axis_index("c")
    chips = [(1 - x, y), (x, 1 - y), (1 - x, 1 - y)]
    return x, y, c, chips


class Ride:
    def __init__(self, arrays, out_shape, copies):
        self.arrays, self.out_shape, self.copies = list(arrays), list(out_shape), copies
        n = 3 * len(self.arrays)
        self.scratch = [pltpu.SemaphoreType.DMA((n,)), pltpu.SemaphoreType.DMA((n,)),
                        pltpu.SemaphoreType.DMA((len(self.arrays),))]

    def split(self, refs):
        n, m = len(self.arrays), len(self.out_shape)
        return refs[:n], refs[n:n + m], refs[n + m:]

    def start(self, ins, outs, sems):
        local, sends, _ = self.copies(ins, outs, *sems)
        for cp in local + sends:
            cp.start()

    def wait(self, ins, outs, sems):
        local, sends, arrivals = self.copies(ins, outs, *sems)
        for cp in arrivals:
            cp.wait_recv()
        for cp in sends:
            cp.wait_send()
        for cp in local:
            cp.wait()

    def run(self, name):
        def body(*refs):
            ins, outs, sems = self.split(refs)
            self.start(ins, outs, sems)
            self.wait(ins, outs, sems)

        return pl.pallas_call(body, name=name, in_specs=[ANY] * len(self.arrays), out_specs=[ANY] * len(self.out_shape),
                              out_shape=self.out_shape, scratch_shapes=self.scratch)(*self.arrays)


def gather_ride(shards, axes):
    def copies(ins, outs, send_sems, recv_sems, local_sems):
        x, y, c, chips = _place()

        def slot(a, q):
            size = shards[a].shape[axes[a]]
            start = pl.multiple_of(q * size, size)
            return outs[a].at[pl.ds(start, size), :] if axes[a] == 0 else outs[a].at[:, pl.ds(start, size)]

        def remote(a, k, q):
            px, py = chips[k]
            return pltpu.make_async_remote_copy(src_ref=ins[a], dst_ref=slot(a, q), send_sem=send_sems.at[3 * a + k],
                                                recv_sem=recv_sems.at[3 * a + k], device_id=(px, py, c), device_id_type=MESH)

        me = 2 * x + y
        n = len(shards)
        local = [pltpu.make_async_copy(ins[a], slot(a, me), local_sems.at[a]) for a in range(n)]
        sends = [remote(a, k, me) for a in range(n) for k in range(3)]
        arrivals = [remote(a, k, 2 * chips[k][0] + chips[k][1]) for a in range(n) for k in range(3)]
        return local, sends, arrivals

    out_shape = []
    for a, sh in enumerate(shards):
        full = list(sh.shape)
        full[axes[a]] *= N_CHIPS
        out_shape.append(jax.ShapeDtypeStruct(tuple(full), sh.dtype))
    return Ride(shards, out_shape, copies)


def scatter_ride(grads):
    def copies(ins, outs, send_sems, recv_sems, local_sems):
        x, y, c, chips = _place()
        sends = [pltpu.make_async_remote_copy(src_ref=ins[a].at[2 * px + py], dst_ref=outs[a].at[k],
                                              send_sem=send_sems.at[3 * a + k], recv_sem=recv_sems.at[3 * a + k],
                                              device_id=(px, py, c), device_id_type=MESH)
                 for a in range(len(grads)) for k, (px, py) in enumerate(chips)]
        return [], sends, sends

    return Ride(grads, [jax.ShapeDtypeStruct((3,) + g.shape[1:], g.dtype) for g in grads], copies)


def swap_with_sibling(arrs, *, name):
    n = len(arrs)

    def body(*refs):
        ins, outs = refs[:n], refs[n:2 * n]
        send_sems, recv_sems = refs[2 * n:]
        x, y, c, _ = _place()
        sends = []
        for a in range(n):
            cp = pltpu.make_async_remote_copy(src_ref=ins[a], dst_ref=outs[a], send_sem=send_sems.at[a],
                                              recv_sem=recv_sems.at[a], device_id=(x, y, 1 - c), device_id_type=MESH)
            cp.start()
            sends.append(cp)
        for cp in sends:
            cp.wait_recv()
        for cp in sends:
            cp.wait_send()

    return pl.pallas_call(
        body, name=name, in_specs=[ANY] * n, out_specs=[ANY] * n,
        out_shape=[jax.ShapeDtypeStruct(a.shape, a.dtype) for a in arrs],
        scratch_shapes=[pltpu.SemaphoreType.DMA((n,)), pltpu.SemaphoreType.DMA((n,))],
    )(*arrs)


def all_reduce_small(v, *, name):
    rows = v.shape[0]

    def body(v_ref, o_ref, gath, send_sems, recv_sems):
        x, y, c, _ = _place()
        me = 4 * x + 2 * y + c
        gath[me] = v_ref[...]
        sends = []
        for msk in range(1, N_DEV):
            peer = (x ^ (msk >> 2), y ^ ((msk >> 1) & 1), c ^ (msk & 1))
            cp = pltpu.make_async_remote_copy(src_ref=v_ref, dst_ref=gath.at[me], send_sem=send_sems.at[msk - 1],
                                              recv_sem=recv_sems.at[msk - 1], device_id=peer, device_id_type=MESH)
            cp.start()
            sends.append(cp)
        for msk in range(1, N_DEV):
            pltpu.make_async_remote_copy(src_ref=v_ref, dst_ref=gath.at[me ^ msk], send_sem=send_sems.at[msk - 1],
                                         recv_sem=recv_sems.at[msk - 1], device_id=(x, y, c), device_id_type=MESH).wait_recv()
        for cp in sends:
            cp.wait_send()
        tot = gath[0]
        for q in range(1, N_DEV):
            tot = tot + gath[q]
        o_ref[...] = tot

    vm = pl.BlockSpec(memory_space=pltpu.VMEM)
    return pl.pallas_call(
        body, name=name, in_specs=[vm], out_specs=vm, out_shape=jax.ShapeDtypeStruct(v.shape, F32),
        scratch_shapes=[pltpu.VMEM((N_DEV, rows, LANES), F32), pltpu.SemaphoreType.DMA((N_DEV - 1,)),
                        pltpu.SemaphoreType.DMA((N_DEV - 1,))],
    )(v)


def _ffn_fwd(xin, gain, w_up, w_conv, w_down, bl, tag, rides=(None, None, None)):
    got = {}

    def carried(k, result, delivered):
        if rides[k] is not None:
            got.update(zip(rides[k][1], delivered))
        return result

    n = rmsnorm_fwd(xin, gain, name=f"{tag}_ffn_norm")
    if rides[0] is None:
        u = matmul(n, w_up, name=f"{tag}_ffn_up")
    else:
        u = carried(0, *matmul(n, w_up, ride=rides[0][0], name=f"{tag}_ffn_up"))
    *hc, delivered = conv_silu_fwd(u, w_conv, bl, name=f"{tag}_ffn_conv", ride=rides[1][0] if rides[1] else None)
    h, ca, cg = carried(1, hc, delivered)
    if rides[2] is None:
        xout = matmul(h, w_down, out_dtype=F32, add=xin, name=f"{tag}_ffn_down")
    else:
        xout = carried(2, *matmul(h, w_down, out_dtype=F32, add=xin, ride=rides[2][0], name=f"{tag}_ffn_down"))
    return xout, (n, u, h, ca, cg), got


def _ffn_bwd(dxout, dxout_b, xin, saved, gain, w_up, w_conv, w_down, bl, tag, shard):
    n, u, h, ca, cg = saved
    dh = matmul(dxout_b, w_down, tb=True, name=f"{tag}_ffn_down_dx")
    g_down = matmul(h, dxout_b, ta=True, name=f"{tag}_ffn_down_dw")
    dua, dug, g_conv = conv_silu_bwd(u, ca, cg, w_conv, dh, bl, name=f"{tag}_ffn_conv_bwd")
    dn = matmul(dua, w_up, a2=dug, tb=True, out_dtype=F32, name=f"{tag}_ffn_up_dx")
    g_up = matmul(n, dua, b2=dug, ta=True, shard_cols=shard, name=f"{tag}_ffn_up_dw")
    dxin, dxin_b, g_norm = rmsnorm_bwd(xin, gain, dn, dxout, name=f"{tag}_ffn_norm_bwd")
    return dxin, dxin_b, g_norm, g_up, g_conv, g_down


def _mem_kv(mem2, gain, w_kv, tag):
    mn = rmsnorm_fwd(mem2, gain, name=f"{tag}_mem_norm")
    return mn, matmul(mn, w_kv, name=f"{tag}_mem_kv")


def _mem_kv_bwd(mem2, mn, gain, w_kv, dkvm, tag):
    g_kv = matmul(mn, dkvm, ta=True, name=f"{tag}_mem_kv_dw")
    dmn = matmul(dkvm, w_kv, tb=True, out_dtype=F32, name=f"{tag}_mem_kv_dx")
    _, _, g_norm = rmsnorm_bwd(mem2, gain, dmn, None, name=f"{tag}_mem_norm_bwd")
    return g_norm, g_kv


def _shard_major(name, grad):
    if name in COL_SHARDED:
        return grad
    return grad.reshape(N_CHIPS, grad.shape[0] // N_CHIPS, grad.shape[1])


def local_step(x, mem, target, w, shard_of, late_gather=None, early_scatter=()):
    bl, s, d = x.shape
    t = bl * s
    x0 = x.reshape(t, d)
    mem2 = mem.reshape(bl * MEM_LEN, d)
    tgt = target.reshape(t, d)
    g = {}

    n1 = rmsnorm_fwd(x0, w["a_norm_attn"], name="a_attn_norm")
    proj_a = matmul(n1, w["a_w_in"], name="a_w_in")
    mn_a, kvm_a = _mem_kv(mem2, w["a_norm_mem"], w["a_w_mem_kv"], "a")
    late_gather = late_gather or {}
    behind_sb = late_gather.get("sb")
    o_sb, o_sb_f32, delivered = sb_fwd(proj_a, bl, name="a_sb_fwd", ride=behind_sb[0] if behind_sb else None)
    if behind_sb:
        w = {**w, **dict(zip(behind_sb[1], delivered))}
    o_mem_a = mem_fwd(proj_a, kvm_a, bl, name="a_mem_fwd")
    cat_a = jnp.concatenate([o_sb, o_mem_a], axis=1)
    x1 = matmul(cat_a, w["a_w_out"], out_dtype=F32, add=x0, name="a_w_out")
    x2, ffn_a, got = _ffn_fwd(x1, w["a_norm_ffn"], w["a_ffn_up"], w["a_ffn_conv"], w["a_ffn_down"], bl, "a",
                              rides=tuple(late_gather.get(host) for host in ("up", "conv", "down")))
    w = {**w, **got}

    nkv, n3 = rmsnorm_fwd2(x2, w["kv_norm"], w["b_norm_attn"], name="kv_b_attn_norm")
    kv = matmul(nkv, w["w_kv_shared"], out_dtype=F32, name="w_kv")
    proj_b = matmul(n3, w["b_w_in"], out_dtype=F32, name="b_w_in")
    mn_b, kvm_b = _mem_kv(mem2, w["b_norm_mem"], w["b_w_mem_kv"], "b")
    dil = [dil_fwd(proj_b, kv, gi, bl, name=f"b_dil_fwd{gi}") for gi in range(3)]
    os_, lses = [o for o, _ in dil], [l for _, l in dil]
    o_dil = dil_combine_fwd(os_, lses, name="b_dil_combine")
    o_mem_b = mem_fwd(proj_b, kvm_b, bl, name="b_mem_fwd")
    cat_b = jnp.concatenate([o_dil, o_mem_b], axis=1)
    x3 = matmul(cat_b, w["b_w_out"], out_dtype=F32, add=x2, name="b_w_out")
    x4, ffn_b, _ = _ffn_fwd(x3, w["b_norm_ffn"], w["b_ffn_up"], w["b_ffn_conv"], w["b_ffn_down"], bl, "b")

    dx4, dx4b, g["final_norm"], lossvec = final_loss(x4, w["final_norm"], tgt, name="final_loss")

    dx3, dx3b, g["b_norm_ffn"], g["b_ffn_up"], g["b_ffn_conv"], g["b_ffn_down"] = _ffn_bwd(
        dx4, dx4b, x3, ffn_b, w["b_norm_ffn"], w["b_ffn_up"], w["b_ffn_conv"], w["b_ffn_down"], bl, "b",
        shard_of["b_ffn_up"])
    dcat_b = matmul(dx3b, w["b_w_out"], tb=True, name="b_w_out_dx")
    g["b_w_out"] = matmul(cat_b, dx3b, ta=True, name="b_w_out_dw")
    dog, dshift = dil_combine_bwd(os_, lses, dcat_b, name="b_dil_combine_bwd")
    dqs, dks, dvs = [], [], []
    for gi in range(3):
        dq_g, dk_g, dv_g = dil_bwd(proj_b, kv, lses[gi], dog, dshift, gi, bl, name=f"b_dil_bwd{gi}")
        dqs.append(dq_g), dks.append(dk_g), dvs.append(dv_g)
    dq_mem_b, dkvm_b = mem_bwd(proj_b, kvm_b, dcat_b, bl, name="b_mem_bwd")
    g["b_norm_mem"], g["b_w_mem_kv"] = _mem_kv_bwd(mem2, mn_b, w["b_norm_mem"], w["b_w_mem_kv"], dkvm_b, "b")
    dproj_b = jnp.concatenate([dq_g.astype(BF) for dq_g in dqs] + [dq_mem_b], axis=1)
    dn3 = matmul(dproj_b, w["b_w_in"], tb=True, out_dtype=F32, name="b_w_in_dx")
    g["b_w_in"] = matmul(n3, dproj_b, ta=True, name="b_w_in_dw")
    dkv = jnp.concatenate(dks + dvs, axis=1).astype(BF)
    dnkv = matmul(dkv, w["w_kv_shared"], tb=True, out_dtype=F32, name="w_kv_dx")
    g["w_kv_shared"] = matmul(nkv, dkv, ta=True, shard_cols=shard_of["w_kv_shared"], name="w_kv_dw")
    dx2, dx2b, g["kv_norm"], g["b_norm_attn"] = rmsnorm_bwd2(x2, w["kv_norm"], w["b_norm_attn"], dnkv, dn3, dx3,
                                                             name="kv_b_attn_norm_bwd")

    dx1, dx1b, g["a_norm_ffn"], g["a_ffn_up"], g["a_ffn_conv"], g["a_ffn_down"] = _ffn_bwd(
        dx2, dx2b, x1, ffn_a, w["a_norm_ffn"], w["a_ffn_up"], w["a_ffn_conv"], w["a_ffn_down"], bl, "a",
        shard_of["a_ffn_up"])
    dcat_a = matmul(dx1b, w["a_w_out"], tb=True, name="a_w_out_dx")
    g["a_w_out"] = matmul(cat_a, dx1b, ta=True, name="a_w_out_dw")
    leaving = scatter_ride([_shard_major(n, g[n]) for n in early_scatter]) if early_scatter else None
    dq_sb, dk_sb, dv_sb, landed = sb_bwd(proj_a, o_sb_f32, dcat_a, bl, name="a_sb_bwd", ride=leaving)
    landed = dict(zip(early_scatter, landed))
    dq_mem_a, dkvm_a = mem_bwd(proj_a, kvm_a, dcat_a, bl, name="a_mem_bwd")
    g["a_norm_mem"], g["a_w_mem_kv"] = _mem_kv_bwd(mem2, mn_a, w["a_norm_mem"], w["a_w_mem_kv"], dkvm_a, "a")
    dproj_a = jnp.concatenate([dq_sb, dk_sb.astype(BF), dv_sb.astype(BF), dq_mem_a], axis=1)
    dn1 = matmul(dproj_a, w["a_w_in"], tb=True, out_dtype=F32, name="a_w_in_dx")
    g["a_w_in"] = matmul(n1, dproj_a, ta=True, shard_cols=shard_of["a_w_in"], name="a_w_in_dw")
    dx0, _, g["a_norm_attn"] = rmsnorm_bwd(x0, w["a_norm_attn"], dn1, dx1, name="a_attn_norm_bwd")
    return lossvec, dx0, g, landed


MATRICES = ("a_w_in", "a_w_out", "a_w_mem_kv", "a_ffn_up", "a_ffn_down", "w_kv_shared", "b_w_in", "b_w_out",
            "b_w_mem_kv", "b_ffn_up", "b_ffn_down")
COL_SHARDED = ("a_w_in", "a_ffn_up", "w_kv_shared", "b_ffn_up")
FIRST_NEEDED = ("a_w_in", "a_w_mem_kv")
GATHER_BEHIND = {"sb": ("a_w_out", "a_ffn_up", "a_ffn_down", "b_ffn_up"), "up": ("b_ffn_down", "w_kv_shared"),
                 "conv": ("b_w_in", "b_w_out", "b_w_mem_kv")}
SMALL_SHARDED = ("a_norm_attn", "a_norm_mem", "a_norm_ffn", "a_ffn_conv", "b_ffn_conv")
SMALL_REPLICATED = ("kv_norm", "b_norm_attn", "b_norm_mem", "b_norm_ffn", "final_norm")
WEIGHTS = ("a_norm_attn", "a_w_in", "a_w_out", "a_norm_mem", "a_w_mem_kv", "a_norm_ffn", "a_ffn_up", "a_ffn_conv",
           "a_ffn_down", "kv_norm", "w_kv_shared", "b_norm_attn", "b_w_in", "b_w_out", "b_norm_mem", "b_w_mem_kv",
           "b_norm_ffn", "b_ffn_up", "b_ffn_conv", "b_ffn_down", "final_norm")


def _two_d(a):
    if a.ndim == 1:
        return a.reshape(1, -1)
    return a.reshape(a.shape[-2], a.shape[-1])


def kernel(x, mem, a_norm_attn, a_w_in, a_w_out, a_norm_mem, a_w_mem_kv, a_norm_ffn, a_ffn_up, a_ffn_conv, a_ffn_down, kv_norm, w_kv_shared, b_norm_attn, b_w_in, b_w_out, b_norm_mem, b_w_mem_kv, b_norm_ffn, b_ffn_up, b_ffn_conv, b_ffn_down, final_norm, loss_target, m_a_norm_attn, m_a_w_in, m_a_w_out, m_a_norm_mem, m_a_w_mem_kv, m_a_norm_ffn, m_a_ffn_up, m_a_ffn_conv, m_a_ffn_down, m_kv_norm, m_w_kv_shared, m_b_norm_attn, m_b_w_in, m_b_w_out, m_b_norm_mem, m_b_w_mem_kv, m_b_norm_ffn, m_b_ffn_up, m_b_ffn_conv, m_b_ffn_down, m_final_norm, v_a_norm_attn, v_a_w_in, v_a_w_out, v_a_norm_mem, v_a_w_mem_kv, v_a_norm_ffn, v_a_ffn_up, v_a_ffn_conv, v_a_ffn_down, v_kv_norm, v_w_kv_shared, v_b_norm_attn, v_b_w_in, v_b_w_out, v_b_norm_mem, v_b_w_mem_kv, v_b_norm_ffn, v_b_ffn_up, v_b_ffn_conv, v_b_ffn_down, v_final_norm):
    given = dict(locals())
    wl = {n: _two_d(given[n]) for n in WEIGHTS}
    ml = {n: _two_d(given["m_" + n]) for n in WEIGHTS}
    vl = {n: _two_d(given["v_" + n]) for n in WEIGHTS}
    chip = 2 * lax.axis_index("x") + lax.axis_index("y")

    packed = jnp.concatenate([wl[n].reshape(-1, LANES) for n in SMALL_SHARDED], axis=0)
    axis_of = lambda n: 1 if n in COL_SHARDED else 0
    late = tuple(n for n in MATRICES if n not in FIRST_NEEDED)
    full = gather_ride([wl[n].astype(BF) for n in FIRST_NEEDED] + [packed],
                       [axis_of(n) for n in FIRST_NEEDED] + [0]).run("gather_first")
    w = dict(zip(FIRST_NEEDED, full[:-1]))
    late_gather = {host: (gather_ride([wl[n].astype(BF) for n in names], [axis_of(n) for n in names]), names)
                   for host, names in GATHER_BEHIND.items()}
    rows = packed.shape[0]
    per_chip = full[-1].reshape(N_CHIPS, rows, LANES)
    r0 = 0
    for n in SMALL_SHARDED:
        nr = wl[n].size // LANES
        piece = per_chip[:, r0:r0 + nr].reshape(N_CHIPS, wl[n].shape[0], wl[n].shape[1])
        w[n] = jnp.concatenate([piece[q] for q in range(N_CHIPS)], axis=1)
        r0 += nr
    for n in SMALL_REPLICATED:
        w[n] = wl[n]

    shard_of = {n: wl[n].shape[1] for n in COL_SHARDED}
    lossvec, dx0, g, landed = local_step(x, mem, loss_target, w, shard_of, late_gather=late_gather, early_scatter=late)
    loss = lax.psum(0.5 * jnp.sum(lossvec) / x.shape[-1], ("x", "y", "c"))

    g4 = {n: _shard_major(n, g[n]) for n in MATRICES}
    landed.update(zip(FIRST_NEEDED, scatter_ride([g4[n] for n in FIRST_NEEDED]).run("scatter_last")))
    sums = [sum4(lax.dynamic_index_in_dim(g4[n], chip, 0, keepdims=False), landed[n], name=f"sum4_{n}")
            for n in MATRICES]
    theirs = swap_with_sibling(sums, name="swap_sums")
    out = {}
    for k, n in enumerate(MATRICES):
        out[n] = adamw(wl[n], sums[k], theirs[k], ml[n], vl[n], name=f"adamw_{n}")

    small = SMALL_SHARDED + SMALL_REPLICATED
    flat = jnp.concatenate([g[n].reshape(-1, LANES) for n in small], axis=0)
    tot = all_reduce_small(flat, name="all_reduce_small")
    r0 = 0
    for n in small:
        nr = g[n].size // LANES
        gn = tot[r0:r0 + nr].reshape(g[n].shape)
        r0 += nr
        if n in SMALL_SHARDED:
            gn = lax.dynamic_slice_in_dim(gn, chip * wl[n].shape[1], wl[n].shape[1], axis=1)
        out[n] = adamw(wl[n], gn, None, ml[n], vl[n], name=f"adamw_{n}")

    res = [loss, dx0.reshape(x.shape)]
    for slot in range(4):
        res += [out[n][slot].reshape(given[n].shape) for n in WEIGHTS]
    return tuple(res)
```

```python
import functools
import math

import jax
import jax.numpy as jnp
from jax import lax
from jax.experimental import pallas as pl
from jax.experimental.pallas import tpu as pltpu

F32 = jnp.float32
BF = jnp.bfloat16
MESH = pl.DeviceIdType.MESH

HEAD_DIM = 64
LANES = 128
SB_WIDTH = 12 * HEAD_DIM
MEM_WIDTH = 4 * HEAD_DIM
DIL_WIDTH = 12 * HEAD_DIM
MEM_LEN = 256
DIL_GROUPS = ((128, 1), (512, 4), (2048, 16))
QB = 128
EPS = 1e-6
SCALE = HEAD_DIM ** -0.5
NEG = -1e30
ALIBI = tuple(2.0 ** (-8.0 * i / 12) for i in range(1, 13))
N_CHIPS = 4
N_DEV = 8

ADAM_LR, ADAM_B1, ADAM_B2, ADAM_EPS, ADAM_WD, ADAM_STEP = 0.001, 0.9, 0.999, 1e-08, 0.01, 10

VMEM_LIMIT = 52 * 1024 * 1024
MATMUL_VMEM_BUDGET = 44 * 1024 * 1024
MATMUL_MAX_TN = 2816
HBM_BYTES_PER_S = 3.0e12
VMEM_BYTES_PER_S = 6.0e12
STEP_OVERHEAD_S = 0.4e-6


def _mo(v, m):
    return v if isinstance(v, int) else pl.multiple_of(v, m)


def _pick(n, prefs):
    for t in prefs:
        if n % t == 0:
            return t
    return n


def _params(*sem):
    return pltpu.CompilerParams(dimension_semantics=sem, vmem_limit_bytes=VMEM_LIMIT)


def matmul(a, b, *, ta=False, tb=False, out_dtype=BF, add=None, shard_cols=0, a2=None, b2=None, ride=None, name):
    m, k = (a.shape[1], a.shape[0]) if ta else a.shape
    n = b.shape[0] if tb else b.shape[1]
    if a2 is not None:
        assert not ta and a2.shape == a.shape
        k *= 2
    if b2 is not None:
        assert not tb and b2.shape == b.shape
        n *= 2
    k_piece = a.shape[1] if a2 is not None else k
    n_piece = b.shape[1] if b2 is not None else n
    out_bytes = jnp.dtype(out_dtype).itemsize
    pieces_a, pieces_b = (2 if a2 is not None else 1), (2 if b2 is not None else 1)
    if shard_cols:
        tn_choices = [c * shard_cols for c in (4, 2, 1) if c * shard_cols <= MATMUL_MAX_TN and c <= 4 // pieces_b]
    else:
        tn_choices = [c for c in (2816, 2560, 1536, 1408, 1024, 512, 256, 128) if n_piece % c == 0]

    def vmem_bytes(tm_, tn_, tk_):
        blocks = (tm_ * tk_ * a.dtype.itemsize * pieces_a + tk_ * tn_ * b.dtype.itemsize * pieces_b
                  + tm_ * tn_ * out_bytes + (tm_ * tn_ * 4 if add is not None else 0))
        return 2 * blocks + (tm_ * tn_ * 4 if k // tk_ > 1 else 0) + tm_ * tn_ * 4

    def cost(c):
        tm_, tn_, tk_ = c
        steps = (m // tm_) * (n // tn_) * (k // tk_)
        hbm = (m * k * a.dtype.itemsize * (n // tn_) + k * n * b.dtype.itemsize * (m // tm_)
               + m * n * (out_bytes + (4 if add is not None else 0)))
        return hbm / HBM_BYTES_PER_S + steps * STEP_OVERHEAD_S + m * n * 8 * (k // tk_ - 1) / VMEM_BYTES_PER_S

    tm, tn, tk = min(((tm_, tn_, tk_) for tm_ in (1024, 512, 1408, 256, 128) if m % tm_ == 0 for tn_ in tn_choices
                      for tk_ in (2816, 2560, 2048, 1536, 1024, 512, 256, 128) if k_piece % tk_ == 0
                      if vmem_bytes(tm_, tn_, tk_) <= MATMUL_VMEM_BUDGET), key=cost)
    ns = tn // shard_cols if shard_cols else 1
    nk = k // tk
    nk1 = nk // 2
    nj1 = (n // tn) // 2
    dims = (((0,) if ta else (1,), (1,) if tb else (0,)), ((), ()))
    n_in = 2 + (a2 is not None) + (b2 is not None) + (add is not None)
    grid = (m // tm, n // tn, nk)

    def body(*refs):
        own_in, (o_ref,), scr, ride_start, ride_wait = _ride_hooks(ride, refs, n_in, 1, int(nk > 1), grid)
        ride_start()
        ins = list(own_in)
        a_ref, b_ref = ins.pop(0), ins.pop(0)
        a2_ref = ins.pop(0) if a2 is not None else None
        b2_ref = ins.pop(0) if b2 is not None else None
        add_ref = ins.pop(0) if add is not None else None

        a_tile, b_tile = a_ref[...], b_ref[...]
        if a2 is not None:
            a_tile = jnp.where(pl.program_id(2) < nk1, a_tile, a2_ref[...])
        if b2 is not None:
            b_tile = jnp.where(pl.program_id(1) < nj1, b_tile, b2_ref[...])
        part = lax.dot_general(a_tile.astype(BF), b_tile.astype(BF), dims, preferred_element_type=F32)

        def finish(r):
            if add is not None:
                r = r + add_ref[...]
            if shard_cols:
                for c in range(ns):
                    o_ref[c] = r[:, c * shard_cols:(c + 1) * shard_cols].astype(o_ref.dtype)
            else:
                o_ref[...] = r.astype(o_ref.dtype)

        if nk == 1:
            finish(part)
        else:
            acc_ref = scr[0]
            kk = pl.program_id(2)

            @pl.when(kk == 0)
            def _():
                acc_ref[...] = part

            @pl.when(kk > 0)
            def _():
                acc_ref[...] += part

            @pl.when(kk == nk - 1)
            def _():
                finish(acc_ref[...])
        ride_wait()

    a_spec = pl.BlockSpec((tk, tm), lambda i, j, q: (q, i)) if ta else pl.BlockSpec((tm, tk), lambda i, j, q: (i, q))
    b_spec = pl.BlockSpec((tn, tk), lambda i, j, q: (j, q)) if tb else pl.BlockSpec((tk, tn), lambda i, j, q: (q, j))
    if a2 is not None:
        a_spec = pl.BlockSpec((tm, tk), lambda i, j, q: (i, jnp.minimum(q, nk1 - 1)))
    if b2 is not None:
        b_spec = pl.BlockSpec((tk, tn), lambda i, j, q: (jnp.where(j < nj1, q, 0), jnp.minimum(j, nj1 - 1)))
    in_specs = [a_spec, b_spec]
    args = [a, b]
    if a2 is not None:
        in_specs.append(pl.BlockSpec((tm, tk), lambda i, j, q: (i, jnp.maximum(q - nk1, 0))))
        args.append(a2)
    if b2 is not None:
        in_specs.append(pl.BlockSpec((tk, tn), lambda i, j, q: (jnp.where(j < nj1, 0, q), jnp.maximum(j - nj1, 0))))
        args.append(b2)
    if add is not None:
        in_specs.append(pl.BlockSpec((tm, tn), lambda i, j, q: (i, j)))
        args.append(add)
    if shard_cols:
        out_shape = jax.ShapeDtypeStruct((N_CHIPS, m, shard_cols), out_dtype)
        out_spec = pl.BlockSpec((ns, tm, shard_cols), lambda i, j, q: (j, i, 0))
    else:
        out_shape = jax.ShapeDtypeStruct((m, n), out_dtype)
        out_spec = pl.BlockSpec((tm, tn), lambda i, j, q: (i, j))
    own_scratch = [pltpu.VMEM((tm, tn), F32)] if nk > 1 else []
    if ride is None:
        return pl.pallas_call(
            body, name=name, grid=grid, in_specs=in_specs, out_specs=out_spec, out_shape=out_shape,
            scratch_shapes=own_scratch, compiler_params=_params("parallel", "parallel", "arbitrary"),
        )(*args)
    out, *delivered = pl.pallas_call(
        body, name=name, grid=grid, in_specs=in_specs + [ANY] * len(ride.arrays),
        out_specs=[out_spec] + [ANY] * len(ride.out_shape), out_shape=[out_shape] + ride.out_shape,
        scratch_shapes=own_scratch + ride.scratch, compiler_params=_params("arbitrary", "arbitrary", "arbitrary"),
    )(*args, *ride.arrays)
    return out, delivered


def rmsnorm_fwd(x, g, *, name):
    t, d = x.shape
    tr = _pick(t, (512, 256))

    def body(x_ref, g_ref, o_ref):
        xv = x_ref[...]
        r = lax.rsqrt(jnp.mean(xv * xv, axis=-1, keepdims=True) + EPS)
        o_ref[...] = ((xv * r) * g_ref[...]).astype(o_ref.dtype)

    return pl.pallas_call(
        body, name=name, grid=(t // tr,),
        in_specs=[pl.BlockSpec((tr, d), lambda i: (i, 0)), pl.BlockSpec((1, d), lambda i: (0, 0))],
        out_specs=pl.BlockSpec((tr, d), lambda i: (i, 0)), out_shape=jax.ShapeDtypeStruct((t, d), BF),
        compiler_params=_params("parallel"),
    )(x, g)


def rmsnorm_bwd(x, g, dn, dres, *, name, ride=None):
    t, d = x.shape
    tr = _pick(t, (512, 256))
    want_dx = dres is not None

    def body(*refs):
        if want_dx:
            (x_ref, g_ref, dn_ref, dres_ref), (dx_ref, dxb_ref, dg_ref), _, ride_start, ride_wait = _ride_hooks(
                ride, refs, 4, 3, 0, (t // tr,))
            ride_start()
        else:
            x_ref, g_ref, dn_ref, dg_ref = refs

        @pl.when(pl.program_id(0) == 0)
        def _():
            dg_ref[...] = jnp.zeros_like(dg_ref)

        xv = x_ref[...]
        r = lax.rsqrt(jnp.mean(xv * xv, axis=-1, keepdims=True) + EPS)
        xn = xv * r
        dnv = dn_ref[...].astype(F32)
        dg_ref[...] += jnp.sum(dnv * xn, axis=0, keepdims=True)
        if want_dx:
            dyg = dnv * g_ref[...]
            cm = jnp.mean(dyg * xn, axis=-1, keepdims=True)
            dx = dres_ref[...] + r * (dyg - xn * cm)
            dx_ref[...] = dx
            dxb_ref[...] = dx.astype(BF)
            ride_wait()

    row = pl.BlockSpec((tr, d), lambda i: (i, 0))
    vec = pl.BlockSpec((1, d), lambda i: (0, 0))
    if want_dx:
        extra = ride.arrays if ride else []
        dx, dxb, dg, *delivered = pl.pallas_call(
            body, name=name, grid=(t // tr,), in_specs=[row, vec, row, row] + [ANY] * len(extra),
            out_specs=[row, row, vec] + [ANY] * (len(ride.out_shape) if ride else 0),
            out_shape=[jax.ShapeDtypeStruct((t, d), F32), jax.ShapeDtypeStruct((t, d), BF), jax.ShapeDtypeStruct((1, d), F32)]
            + (ride.out_shape if ride else []),
            scratch_shapes=ride.scratch if ride else [], compiler_params=_params("arbitrary"),
        )(x, g, dn, dres, *extra)
        return (dx, dxb, dg, delivered) if ride else (dx, dxb, dg)
    return None, None, pl.pallas_call(
        body, name=name, grid=(t // tr,), in_specs=[row, vec, row], out_specs=vec,
        out_shape=jax.ShapeDtypeStruct((1, d), F32), compiler_params=_params("arbitrary"),
    )(x, g, dn)


def rmsnorm_fwd2(x, g1, g2, *, name):
    t, d = x.shape
    tr = _pick(t, (512, 256))

    def body(x_ref, g1_ref, g2_ref, o1_ref, o2_ref):
        xv = x_ref[...]
        xn = xv * lax.rsqrt(jnp.mean(xv * xv, axis=-1, keepdims=True) + EPS)
        o1_ref[...] = (xn * g1_ref[...]).astype(o1_ref.dtype)
        o2_ref[...] = (xn * g2_ref[...]).astype(o2_ref.dtype)

    row = pl.BlockSpec((tr, d), lambda i: (i, 0))
    vec = pl.BlockSpec((1, d), lambda i: (0, 0))
    return pl.pallas_call(
        body, name=name, grid=(t // tr,), in_specs=[row, vec, vec], out_specs=[row, row],
        out_shape=[jax.ShapeDtypeStruct((t, d), BF)] * 2, compiler_params=_params("parallel"),
    )(x, g1, g2)


def rmsnorm_bwd2(x, g1, g2, dn1, dn2, dres, *, name):
    t, d = x.shape
    tr = _pick(t, (512, 256))

    def body(x_ref, g1_ref, g2_ref, dn1_ref, dn2_ref, dres_ref, dx_ref, dxb_ref, dg1_ref, dg2_ref):
        @pl.when(pl.program_id(0) == 0)
        def _():
            dg1_ref[...] = jnp.zeros_like(dg1_ref)
            dg2_ref[...] = jnp.zeros_like(dg2_ref)

        xv = x_ref[...]
        r = lax.rsqrt(jnp.mean(xv * xv, axis=-1, keepdims=True) + EPS)
        xn = xv * r
        d1, d2 = dn1_ref[...].astype(F32), dn2_ref[...].astype(F32)
        dg1_ref[...] += jnp.sum(d1 * xn, axis=0, keepdims=True)
        dg2_ref[...] += jnp.sum(d2 * xn, axis=0, keepdims=True)
        dyg = d1 * g1_ref[...] + d2 * g2_ref[...]
        cm = jnp.mean(dyg * xn, axis=-1, keepdims=True)
        dx = dres_ref[...] + r * (dyg - xn * cm)
        dx_ref[...] = dx
        dxb_ref[...] = dx.astype(BF)

    row = pl.BlockSpec((tr, d), lambda i: (i, 0))
    vec = pl.BlockSpec((1, d), lambda i: (0, 0))
    return pl.pallas_call(
        body, name=name, grid=(t // tr,), in_specs=[row, vec, vec, row, row, row], out_specs=[row, row, vec, vec],
        out_shape=[jax.ShapeDtypeStruct((t, d), F32), jax.ShapeDtypeStruct((t, d), BF), jax.ShapeDtypeStruct((1, d), F32),
                   jax.ShapeDtypeStruct((1, d), F32)],
        compiler_params=_params("arbitrary"),
    )(x, g1, g2, dn1, dn2, dres)


def final_loss(x, g, target, *, name):
    t, d = x.shape
    tr = _pick(t, (512, 256))

    def body(x_ref, g_ref, t_ref, dx_ref, dxb_ref, dg_ref, lv_ref):
        @pl.when(pl.program_id(0) == 0)
        def _():
            dg_ref[...] = jnp.zeros_like(dg_ref)
            lv_ref[...] = jnp.zeros_like(lv_ref)

        xv = x_ref[...]
        r = lax.rsqrt(jnp.mean(xv * xv, axis=-1, keepdims=True) + EPS)
        xn = xv * r
        err = xn * g_ref[...] - t_ref[...]
        lv_ref[...] += jnp.sum(err * err, axis=0, keepdims=True)
        dy = err * (1.0 / d)
        dg_ref[...] += jnp.sum(dy * xn, axis=0, keepdims=True)
        dyg = dy * g_ref[...]
        cm = jnp.mean(dyg * xn, axis=-1, keepdims=True)
        dx = r * (dyg - xn * cm)
        dx_ref[...] = dx
        dxb_ref[...] = dx.astype(BF)

    row = pl.BlockSpec((tr, d), lambda i: (i, 0))
    vec = pl.BlockSpec((1, d), lambda i: (0, 0))
    return pl.pallas_call(
        body, name=name, grid=(t // tr,), in_specs=[row, vec, row], out_specs=[row, row, vec, vec],
        out_shape=[jax.ShapeDtypeStruct((t, d), F32), jax.ShapeDtypeStruct((t, d), BF), jax.ShapeDtypeStruct((1, d), F32),
                   jax.ShapeDtypeStruct((1, d), F32)],
        compiler_params=_params("arbitrary"),
    )(x, g, target)


CONV_ROWS = 256
HALO = 16


def _conv_taps(ext, w, rows):
    s0 = ext[HALO:HALO + rows]
    s1 = pltpu.roll(ext, 1, 0)[HALO:HALO + rows]
    s2 = pltpu.roll(ext, 2, 0)[HALO:HALO + rows]
    return (w[0:1] * s2 + w[1:2] * s1) + w[2:3] * s0, s0, s1, s2


def conv_silu_fwd(u, w, bl, *, name, ride=None):
    t, f2 = u.shape
    f = f2 // 2
    s = t // bl
    tc = _pick(f, (256, 128))
    nf = f // tc
    nch = s // CONV_ROWS

    def body(*refs):
        (ua_ref, ug_ref, wa_ref, wg_ref), (h_ref, ca_ref, cg_ref), _, ride_start, ride_wait = _ride_hooks(
            ride, refs, 4, 3, 0, (nf, bl))
        ride_start()
        wa, wg = wa_ref[...], wg_ref[...]

        def chunk(ci, carry):
            r0 = pl.multiple_of(ci * CONV_ROWS, CONV_ROWS)
            ps = pl.multiple_of(jnp.maximum(r0 - HALO, 0), HALO)
            keep = (ci > 0).astype(F32)
            rows = pl.ds(r0, CONV_ROWS)

            def conv(ref, wv):
                ext = jnp.concatenate([ref[0, pl.ds(ps, HALO), :].astype(F32) * keep, ref[0, rows, :].astype(F32)], axis=0)
                return _conv_taps(ext, wv, CONV_ROWS)[0]

            ca, cg = conv(ua_ref, wa), conv(ug_ref, wg)
            sg = pl.reciprocal(1.0 + jnp.exp(-cg), approx=True)
            h_ref[0, rows, :] = ((cg * sg) * ca).astype(h_ref.dtype)
            ca_ref[0, rows, :] = ca.astype(ca_ref.dtype)
            cg_ref[0, rows, :] = cg.astype(cg_ref.dtype)
            return carry

        lax.fori_loop(0, nch, chunk, 0)
        ride_wait()

    u3 = u.reshape(bl, s, f2)
    blk = lambda off: pl.BlockSpec((1, s, tc), lambda j, b: (b, 0, j + off))
    wblk = lambda off: pl.BlockSpec((3, tc), lambda j, b: (0, j + off))
    extra = ride.arrays if ride else []
    h, ca, cg, *delivered = pl.pallas_call(
        body, name=name, grid=(nf, bl), in_specs=[blk(0), blk(nf), wblk(0), wblk(nf)] + [ANY] * len(extra),
        out_specs=[blk(0)] * 3 + [ANY] * (len(ride.out_shape) if ride else 0),
        out_shape=[jax.ShapeDtypeStruct((bl, s, f), BF)] * 3 + (ride.out_shape if ride else []),
        scratch_shapes=ride.scratch if ride else [],
        compiler_params=_params(*(("arbitrary",) * 2 if ride else ("parallel",) * 2)),
    )(u3, u3, w, w, *extra)
    return h.reshape(t, f), ca.reshape(t, f), cg.reshape(t, f), delivered


def conv_silu_bwd(u, ca, cg, w, dh, bl, *, name):
    t, f2 = u.shape
    f = f2 // 2
    s = t // bl
    tc = _pick(f, (256, 128))
    nf = f // tc
    nch = s // CONV_ROWS
    ext_rows = CONV_ROWS + HALO

    def body(ua_ref, ug_ref, ca_ref, cg_ref, wa_ref, wg_ref, dh_ref, dua_ref, dug_ref, dwa_ref, dwg_ref):
        wa, wg = wa_ref[...], wg_ref[...]

        @pl.when(pl.program_id(1) == 0)
        def _():
            dwa_ref[...] = jnp.zeros_like(dwa_ref)
            dwg_ref[...] = jnp.zeros_like(dwg_ref)

        def chunk(ci, carry):
            r0 = pl.multiple_of(ci * CONV_ROWS, CONV_ROWS)
            ns = pl.multiple_of(jnp.minimum(r0 + CONV_ROWS, s - HALO), HALO)
            keep_n = (ci < nch - 1).astype(F32)
            rows = pl.ds(r0, CONV_ROWS)

            def ext_of(ref):
                return jnp.concatenate([ref[0, rows, :].astype(F32), ref[0, pl.ds(ns, HALO), :].astype(F32)], axis=0)

            dhe = jnp.concatenate([dh_ref[0, rows, :].astype(F32), dh_ref[0, pl.ds(ns, HALO), :].astype(F32) * keep_n],
                                  axis=0)
            cae, cge = ext_of(ca_ref), ext_of(cg_ref)
            sg = pl.reciprocal(1.0 + jnp.exp(-cge), approx=True)
            dca = dhe * (cge * sg)
            dcg = dhe * cae * (sg * (1.0 + cge * (1.0 - sg)))

            def back(dc, wv, u_ref, du_ref, dw_ref):
                d0 = dc[:CONV_ROWS]
                n1 = pltpu.roll(dc, ext_rows - 1, 0)[:CONV_ROWS]
                n2 = pltpu.roll(dc, ext_rows - 2, 0)[:CONV_ROWS]
                du_ref[0, rows, :] = ((wv[2:3] * d0 + wv[1:2] * n1) + wv[0:1] * n2).astype(du_ref.dtype)
                uc = u_ref[0, rows, :].astype(F32)
                for k, shifted in enumerate((n2, n1, d0)):
                    dw_ref[k:k + 1, :] += jnp.sum(shifted * uc, axis=0, keepdims=True)

            back(dca, wa, ua_ref, dua_ref, dwa_ref)
            back(dcg, wg, ug_ref, dug_ref, dwg_ref)
            return carry

        lax.fori_loop(0, nch, chunk, 0)

    u3 = u.reshape(bl, s, f2)
    blk = lambda off: pl.BlockSpec((1, s, tc), lambda j, b: (b, 0, j + off))
    wblk = lambda off: pl.BlockSpec((3, tc), lambda j, b: (0, j + off))
    dua, dug, dwa, dwg = pl.pallas_call(
        body, name=name, grid=(nf, bl), in_specs=[blk(0), blk(nf), blk(0), blk(0), wblk(0), wblk(nf), blk(0)],
        out_specs=[blk(0), blk(0), wblk(0), wblk(0)],
        out_shape=[jax.ShapeDtypeStruct((bl, s, f), BF), jax.ShapeDtypeStruct((bl, s, f), BF),
                   jax.ShapeDtypeStruct((3, f), F32), jax.ShapeDtypeStruct((3, f), F32)],
        compiler_params=_params("parallel", "arbitrary"),
    )(u3, u3, ca.reshape(bl, s, f), cg.reshape(bl, s, f), w, w, dh.reshape(bl, s, f))
    return dua.reshape(t, f), dug.reshape(t, f), jnp.concatenate([dwa, dwg], axis=1)


def _lane_masks(rows):
    lane = lax.broadcasted_iota(jnp.int32, (rows, LANES), 1)
    return lane < HEAD_DIM, lane >= HEAD_DIM


def _nt(a, b):
    return lax.dot_general(a, b, (((1,), (1,)), ((), ())), preferred_element_type=F32)


def _tn(a, b):
    return lax.dot_general(a, b, (((0,), (0,)), ((), ())), preferred_element_type=F32)


def _nn(a, b):
    return jnp.dot(a, b, preferred_element_type=F32)


def _suffix_ones():
    j = lax.broadcasted_iota(jnp.int32, (2 * QB, QB), 0) % QB
    s = lax.broadcasted_iota(jnp.int32, (2 * QB, QB), 1)
    return (j >= s).astype(BF)


def _softplus(z):
    return jnp.maximum(z, 0.0) + jnp.log(1.0 + jnp.exp(-jnp.abs(z)))


SB_BLOCK = 256
SB_DEAD = -100.0


SB_STRIP = 32
SB_SUB = SB_BLOCK // QB


def _sb_scratch(backward):
    blk = (2, SB_BLOCK, SB_BLOCK)
    per_head = pltpu.VMEM((2, SB_BLOCK, LANES), F32)
    scr = [pltpu.VMEM((2 * QB, QB), BF),
           pltpu.VMEM(blk, F32),
           pltpu.VMEM((2, SB_SUB) + blk[1:], BF),
           pltpu.VMEM(blk, F32),
           pltpu.VMEM(blk, BF),
           per_head]
    if not backward:
        return scr + [per_head]
    return scr + [pltpu.VMEM(blk, F32),
                  pltpu.VMEM(blk, F32),
                  pltpu.VMEM(blk, F32),
                  pltpu.VMEM(blk, BF),
                  per_head,
                  per_head,
                  per_head]


def _sb_strips(fn):
    for r in range(SB_BLOCK // SB_STRIP):
        fn(pl.ds(r * SB_STRIP, SB_STRIP), r)


def _sb_keep(diag, r, v):
    if not diag:
        return v
    rel = (lax.broadcasted_iota(jnp.int32, (SB_STRIP, SB_BLOCK), 1)
           - lax.broadcasted_iota(jnp.int32, (SB_STRIP, SB_BLOCK), 0))
    return jnp.where(rel < r * SB_STRIP, v, 0.0)


def _sb_split_store(sp_scr, rows, v):
    hi = v.astype(BF)
    lo = (v - hi.astype(F32)).astype(BF)
    for u in range(SB_SUB):
        sp_scr[u, rows, 0:QB] = hi[:, u * QB:(u + 1) * QB]
        sp_scr[u, rows, QB:2 * QB] = lo[:, u * QB:(u + 1) * QB]


def _sb_suffix_sums(uu_scr, sp_scr, out_scr):
    for u in range(SB_SUB):
        out_scr[:, u * QB:(u + 1) * QB] = _nn(sp_scr[u], uu_scr[...])


def _sb_fold(sums, off):
    offs = [None] * SB_SUB
    for u in reversed(range(SB_SUB)):
        offs[u] = off
        off = off + jnp.broadcast_to(sums[:, u * QB:u * QB + 1], (SB_STRIP, LANES))
    return offs, off


def _sb_first(diag, z_scr, sp_scr, ls_scr=None):
    def strip(rows, r):
        ls = _sb_keep(diag, r, -_softplus(z_scr[rows, :]))
        if ls_scr is not None:
            ls_scr[rows, :] = ls
        _sb_split_store(sp_scr, rows, ls)

    _sb_strips(strip)


def _sb_second(diag, z_scr, cin_scr, w_scr, carry_scr):
    def strip(rows, r):
        z, cin = z_scr[rows, :], cin_scr[rows, :]
        offs, carry_scr[rows, :] = _sb_fold(cin, carry_scr[rows, :])
        e = jnp.concatenate([z[:, u * QB:(u + 1) * QB] + cin[:, u * QB:(u + 1) * QB] + offs[u] for u in range(SB_SUB)],
                            axis=1)
        w_scr[rows, :] = _sb_keep(diag, r, jnp.exp(e)).astype(BF)

    _sb_strips(strip)


def _sb_third(w_scr, dw_scr, da_scr, sp_scr):
    def strip(rows, r):
        da = w_scr[rows, :].astype(F32) * dw_scr[rows, :]
        da_scr[rows, :] = da
        _sb_split_store(sp_scr, rows, da)

    _sb_strips(strip)


def _sb_fourth(diag, z_scr, ls_scr, da_scr, sin_scr, dz_scr, carry_s_scr, dsum_scr):
    def strip(rows, r):
        da, sin = da_scr[rows, :], sin_scr[rows, :]
        offs, carry_s_scr[rows, :] = _sb_fold(sin, carry_s_scr[rows, :])
        dsum = dsum_scr[rows, :]
        pre = jnp.concatenate([dsum - (sin[:, u * QB:(u + 1) * QB] - da[:, u * QB:(u + 1) * QB] + offs[u])
                               for u in range(SB_SUB)], axis=1)
        sig = jnp.exp(z_scr[rows, :] + ls_scr[rows, :])
        dz_scr[rows, :] = _sb_keep(diag, r, da - sig * pre).astype(BF)

    _sb_strips(strip)


def _sb_alive(carry_scr):
    return (jnp.max(carry_scr[...]) > SB_DEAD).astype(jnp.int32)


def _ride_hooks(ride, refs, n_in, n_out, n_scratch, grid):
    if ride is None:
        return refs[:n_in], refs[n_in:n_in + n_out], refs[n_in + n_out:], lambda: None, lambda: None
    ni, no = len(ride.arrays), len(ride.out_shape)
    own_in, rin = refs[:n_in], refs[n_in:n_in + ni]
    own_out, rout = refs[n_in + ni:n_in + ni + n_out], refs[n_in + ni + n_out:n_in + ni + n_out + no]
    rest = refs[n_in + ni + n_out + no:]
    own_scr, sems = rest[:n_scratch], rest[n_scratch:]
    ids = [pl.program_id(a) for a in range(len(grid))]

    def start():
        first = functools.reduce(lambda u, v: u & v, [i == 0 for i in ids])
        pl.when(first)(lambda: ride.start(rin, rout, sems))

    def wait():
        last = functools.reduce(lambda u, v: u & v, [i == n - 1 for i, n in zip(ids, grid)])
        pl.when(last)(lambda: ride.wait(rin, rout, sems))

    return own_in, own_out, own_scr, start, wait


def sb_fwd(proj, bl, *, name, ride=None):
    t, width = proj.shape
    s = t // bl
    npair = SB_WIDTH // LANES
    nq = s // SB_BLOCK
    grid = (bl, npair, nq)

    own_scratch = _sb_scratch(backward=False)

    def body(*refs):
        (q_ref, k_ref, v_ref), (o_ref, of_ref), scr, ride_start, ride_wait = _ride_hooks(
            ride, refs, 3, 2, len(own_scratch), grid)
        uu_scr, z_scr, sp_scr, cin_scr, w_scr, carry_scr, acc_scr = scr
        ride_start()
        i = pl.program_id(2)
        uu_scr[...] = _suffix_ones()
        carry_scr[...] = jnp.zeros_like(carry_scr)
        acc_scr[...] = jnp.zeros_like(acc_scr)
        qs = (q_ref[0].astype(F32) * SCALE).astype(BF)
        masks = _lane_masks(SB_BLOCK)

        z, sp, cin, w, car, acc = ([r.at[h] for h in range(2)] for r in (z_scr, sp_scr, cin_scr, w_scr, carry_scr, acc_scr))

        def live(c, diag):
            start = pl.multiple_of(c * SB_BLOCK, SB_BLOCK)
            kb, vb = k_ref[0, pl.ds(start, SB_BLOCK), :], v_ref[0, pl.ds(start, SB_BLOCK), :]
            for h in range(2):
                z[h][...] = _nt(jnp.where(masks[h], qs, jnp.zeros_like(qs)), kb)
            _sb_first(diag, z[0], sp[0])
            _sb_suffix_sums(uu_scr, sp[0], cin[0])
            _sb_first(diag, z[1], sp[1])
            _sb_suffix_sums(uu_scr, sp[1], cin[1])
            _sb_second(diag, z[0], cin[0], w[0], car[0])
            acc[0][...] += _nn(w[0][...], vb)
            _sb_second(diag, z[1], cin[1], w[1], car[1])
            acc[1][...] += _nn(w[1][...], vb)
            return _sb_alive(carry_scr)

        def step(n, alive):
            return lax.cond(alive > 0, lambda: live(i - n, False), lambda: alive)

        lax.fori_loop(1, i + 1, step, live(i, True))
        both = jnp.where(masks[0], acc_scr[0], acc_scr[1])
        o_ref[0] = both.astype(o_ref.dtype)
        of_ref[0] = both
        ride_wait()

    p3 = proj.reshape(bl, s, width)
    qblk = pl.BlockSpec((1, SB_BLOCK, LANES), lambda b, p, i: (b, i, p))
    extra = ride.arrays if ride else []
    o, of, *delivered = pl.pallas_call(
        body, name=name, grid=grid,
        in_specs=[qblk, pl.BlockSpec((1, s, LANES), lambda b, p, i: (b, 0, npair + p)),
                  pl.BlockSpec((1, s, LANES), lambda b, p, i: (b, 0, 2 * npair + p))] + [ANY] * len(extra),
        out_specs=[qblk, qblk] + [ANY] * (len(ride.out_shape) if ride else 0),
        out_shape=[jax.ShapeDtypeStruct((bl, s, SB_WIDTH), BF), jax.ShapeDtypeStruct((bl, s, SB_WIDTH), F32)]
        + (ride.out_shape if ride else []),
        scratch_shapes=own_scratch + (ride.scratch if ride else []),
        compiler_params=_params("arbitrary", "arbitrary", "arbitrary"),
    )(p3, p3, p3, *extra)
    return o.reshape(t, SB_WIDTH), of.reshape(t, SB_WIDTH), delivered


def sb_bwd(proj, o, dcat, bl, *, name, ride=None):
    t, width = proj.shape
    s = t // bl
    npair = SB_WIDTH // LANES
    nq = s // SB_BLOCK
    grid = (bl, npair, nq)
    own_scratch = _sb_scratch(backward=True)

    def body(*refs):
        (q_ref, k_ref, v_ref, o_ref, do_ref), (dq_ref, dk_ref, dv_ref), scr, ride_start, ride_wait = _ride_hooks(
            ride, refs, 5, 3, len(own_scratch), grid)
        (uu_scr, z_scr, sp_scr, cin_scr, w_scr, carry_scr,
         ls_scr, dw_scr, da_scr, dz_scr, carry_s_scr, dsum_scr, dq_scr) = scr
        ride_start()
        i = pl.program_id(2)

        @pl.when(i == 0)
        def _():
            dk_ref[...] = jnp.zeros_like(dk_ref)
            dv_ref[...] = jnp.zeros_like(dv_ref)

        uu_scr[...] = _suffix_ones()
        for ref in (carry_scr, carry_s_scr, dq_scr):
            ref[...] = jnp.zeros_like(ref)
        masks = _lane_masks(SB_BLOCK)
        qs = (q_ref[0].astype(F32) * SCALE).astype(BF)
        do = do_ref[0]
        prod = do.astype(F32) * o_ref[0]
        for h in range(2):
            dsum_scr[h] = jnp.broadcast_to(jnp.sum(jnp.where(masks[h], prod, 0.0), axis=1, keepdims=True),
                                           (SB_BLOCK, LANES))

        z, sp, cin, w, car, ls, dw, da, dz, cars, dsum, dq = (
            [r.at[h] for h in range(2)] for r in (z_scr, sp_scr, cin_scr, w_scr, carry_scr, ls_scr, dw_scr, da_scr,
                                                  dz_scr, carry_s_scr, dsum_scr, dq_scr))

        def live(c, diag):
            start = pl.multiple_of(c * SB_BLOCK, SB_BLOCK)
            keys = pl.ds(start, SB_BLOCK)
            kb, vb = k_ref[0, keys, :], v_ref[0, keys, :]
            qh = [jnp.where(masks[h], qs, jnp.zeros_like(qs)) for h in range(2)]
            doh = [jnp.where(masks[h], do, jnp.zeros_like(do)) for h in range(2)]
            for h in range(2):
                z[h][...] = _nt(qh[h], kb)
                dw[h][...] = _nt(doh[h], vb)

            def weights(h):
                _sb_second(diag, z[h], cin[h], w[h], car[h])
                _sb_third(w[h], dw[h], da[h], sp[h])
                _sb_suffix_sums(uu_scr, sp[h], cin[h])

            def grads(h):
                _sb_fourth(diag, z[h], ls[h], da[h], cin[h], dz[h], cars[h], dsum[h])
                dk_ref[0, keys, :] += _tn(dz[h][...], qh[h])
                dv_ref[0, keys, :] += _tn(w[h][...], doh[h])
                dq[h][...] += _nn(dz[h][...], kb)

            _sb_first(diag, z[0], sp[0], ls[0])
            _sb_suffix_sums(uu_scr, sp[0], cin[0])
            _sb_first(diag, z[1], sp[1], ls[1])
            _sb_suffix_sums(uu_scr, sp[1], cin[1])
            weights(0)
            weights(1)
            grads(0)
            grads(1)
            return _sb_alive(carry_scr)

        def step(n, alive):
            return lax.cond(alive > 0, lambda: live(i - n, False), lambda: alive)

        lax.fori_loop(1, i + 1, step, live(i, True))
        dq_ref[0] = (jnp.where(masks[0], dq_scr[0], dq_scr[1]) * SCALE).astype(dq_ref.dtype)
        ride_wait()

    p3 = proj.reshape(bl, s, width)
    o3 = o.reshape(bl, s, SB_WIDTH)
    d3 = dcat.reshape(bl, s, dcat.shape[1])
    qblk = pl.BlockSpec((1, SB_BLOCK, LANES), lambda b, p, i: (b, i, p))
    full = pl.BlockSpec((1, s, LANES), lambda b, p, i: (b, 0, p))
    extra = ride.arrays if ride else []
    dq, dk, dv, *delivered = pl.pallas_call(
        body, name=name, grid=grid,
        in_specs=[qblk, pl.BlockSpec((1, s, LANES), lambda b, p, i: (b, 0, npair + p)),
                  pl.BlockSpec((1, s, LANES), lambda b, p, i: (b, 0, 2 * npair + p)), qblk, qblk] + [ANY] * len(extra),
        out_specs=[qblk, full, full] + [ANY] * (len(ride.out_shape) if ride else 0),
        out_shape=[jax.ShapeDtypeStruct((bl, s, SB_WIDTH), BF), jax.ShapeDtypeStruct((bl, s, SB_WIDTH), F32),
                   jax.ShapeDtypeStruct((bl, s, SB_WIDTH), F32)] + (ride.out_shape if ride else []),
        scratch_shapes=own_scratch + (ride.scratch if ride else []),
        compiler_params=_params("arbitrary", "arbitrary", "arbitrary"),
    )(p3, p3, p3, o3, d3, *extra)
    return dq.reshape(t, SB_WIDTH), dk.reshape(t, SB_WIDTH), dv.reshape(t, SB_WIDTH), delivered


MEM_TQ = 512


def mem_fwd(proj, kvm, bl, *, name):
    t, width = proj.shape
    s = t // bl
    qoff = (width - MEM_WIDTH) // LANES
    npair = MEM_WIDTH // LANES

    def body(q_ref, k_ref, v_ref, o_ref):
        ma, mb = _lane_masks(MEM_TQ)
        q = q_ref[...].astype(BF)
        outs = []
        for m in (ma, mb):
            qh = jnp.where(m, q, jnp.zeros_like(q))
            sc = _nt(qh, k_ref[...]) * SCALE
            p = jnp.exp(sc - jnp.max(sc, axis=-1, keepdims=True))
            p = p * (1.0 / jnp.sum(p, axis=-1, keepdims=True))
            outs.append(_nn(p.astype(BF), v_ref[...]))
        o_ref[...] = jnp.where(ma, outs[0], outs[1]).astype(o_ref.dtype)

    nt = s // MEM_TQ
    return pl.pallas_call(
        body, name=name, grid=(bl, npair, nt),
        in_specs=[pl.BlockSpec((MEM_TQ, LANES), lambda b, p, i: (b * nt + i, qoff + p)),
                  pl.BlockSpec((MEM_LEN, LANES), lambda b, p, i: (b, p)),
                  pl.BlockSpec((MEM_LEN, LANES), lambda b, p, i: (b, npair + p))],
        out_specs=pl.BlockSpec((MEM_TQ, LANES), lambda b, p, i: (b * nt + i, p)),
        out_shape=jax.ShapeDtypeStruct((t, MEM_WIDTH), BF),
        compiler_params=_params("parallel", "parallel", "parallel"),
    )(proj, kvm, kvm)


def mem_bwd(proj, kvm, dcat, bl, *, name):
    t, width = proj.shape
    s = t // bl
    qoff = (width - MEM_WIDTH) // LANES
    doff = (dcat.shape[1] - MEM_WIDTH) // LANES
    npair = MEM_WIDTH // LANES
    nt = s // MEM_TQ

    def body(q_ref, k_ref, v_ref, do_ref, dq_ref, dk_ref, dv_ref):
        @pl.when(pl.program_id(2) == 0)
        def _():
            dk_ref[...] = jnp.zeros_like(dk_ref)
            dv_ref[...] = jnp.zeros_like(dv_ref)

        ma, mb = _lane_masks(MEM_TQ)
        q = q_ref[...].astype(BF)
        do = do_ref[...]
        kb, vb = k_ref[...], v_ref[...]
        dqs = []
        for m in (ma, mb):
            qh = jnp.where(m, q, jnp.zeros_like(q))
            doh = jnp.where(m, do, jnp.zeros_like(do))
            sc = _nt(qh, kb) * SCALE
            p = jnp.exp(sc - jnp.max(sc, axis=-1, keepdims=True))
            p = p * (1.0 / jnp.sum(p, axis=-1, keepdims=True))
            dp = _nt(doh, vb)
            ds = p * (dp - jnp.sum(p * dp, axis=-1, keepdims=True))
            dss = (ds * SCALE).astype(BF)
            dk_ref[...] += _tn(dss, qh)
            dv_ref[...] += _tn(p.astype(BF), doh)
            dqs.append(_nn(dss, kb))
        dq_ref[...] = jnp.where(ma, dqs[0], dqs[1]).astype(dq_ref.dtype)

    kblk = pl.BlockSpec((MEM_LEN, LANES), lambda b, p, i: (b, p))
    dq, dk, dv = pl.pallas_call(
        body, name=name, grid=(bl, npair, nt),
        in_specs=[pl.BlockSpec((MEM_TQ, LANES), lambda b, p, i: (b * nt + i, qoff + p)), kblk,
                  pl.BlockSpec((MEM_LEN, LANES), lambda b, p, i: (b, npair + p)),
                  pl.BlockSpec((MEM_TQ, LANES), lambda b, p, i: (b * nt + i, doff + p))],
        out_specs=[pl.BlockSpec((MEM_TQ, LANES), lambda b, p, i: (b * nt + i, p)), kblk, kblk],
        out_shape=[jax.ShapeDtypeStruct((t, MEM_WIDTH), BF), jax.ShapeDtypeStruct((bl * MEM_LEN, MEM_WIDTH), F32),
                   jax.ShapeDtypeStruct((bl * MEM_LEN, MEM_WIDTH), F32)],
        compiler_params=_params("parallel", "parallel", "arbitrary"),
    )(proj, kvm, kvm, dcat)
    return dq, jnp.concatenate([dk, dv], axis=1).astype(BF)


def _dil_rows(r, u0, size, dil):
    if dil == 1:
        return pl.ds(_mo(u0, QB), size)
    return pl.ds(u0 * dil + r, size, stride=dil)


def _dil_walk(block, dil, nb):
    def residue(r, c):
        block(r, 0, QB)
        if nb > 1:
            def step(n, c2):
                block(r, n, 2 * QB)
                return c2
            lax.fori_loop(1, nb, step, 0, unroll=3)
        return c

    if dil == 1:
        residue(0, 0)
    else:
        lax.fori_loop(0, dil, residue, 0, unroll=4 if nb == 1 else 1)


def _dil_scores(qh, kb, slope_d, n_keys):
    i = lax.broadcasted_iota(jnp.int32, (QB, n_keys), 0)
    j = lax.broadcasted_iota(jnp.int32, (QB, n_keys), 1)
    delta = i + (n_keys - QB) - j
    valid = (delta >= 0) & (delta <= QB)
    sc = _nt(qh, kb) * SCALE - slope_d * delta.astype(F32)
    return jnp.where(valid, sc, NEG)


def _dil_slopes(g, dil):
    p = pl.program_id(1)
    sa = jnp.where(p == 0, ALIBI[4 * g] * dil, ALIBI[4 * g + 2] * dil).astype(F32)
    sb = jnp.where(p == 0, ALIBI[4 * g + 1] * dil, ALIBI[4 * g + 3] * dil).astype(F32)
    return sa, sb


def dil_fwd(projb, kv, g, bl, *, name):
    _, dil = DIL_GROUPS[g]
    t, wq = projb.shape
    wk = kv.shape[1]
    s = t // bl
    ln = s // dil
    nb = ln // QB
    gw = 4 * HEAD_DIM
    ck, co = wk // LANES, gw // LANES

    def body(q_ref, k_ref, v_ref, o_ref, lse_ref):
        sa, sb = _dil_slopes(g, dil)
        ma, mb = _lane_masks(QB)

        def block(r, n, n_keys):
            q0 = n * QB
            k0 = q0 - (n_keys - QB)
            q = q_ref[0, _dil_rows(r, q0, QB, dil), :].astype(BF)
            kb = k_ref[0, _dil_rows(r, k0, n_keys, dil), :].astype(BF)
            vb = v_ref[0, _dil_rows(r, k0, n_keys, dil), :].astype(BF)
            outs, lses = [], []
            for m, sl in ((ma, sa), (mb, sb)):
                qh = jnp.where(m, q, jnp.zeros_like(q))
                sc = _dil_scores(qh, kb, sl, n_keys)
                mx = jnp.max(sc, axis=-1, keepdims=True)
                p = jnp.exp(sc - mx)
                den = jnp.sum(p, axis=-1, keepdims=True)
                outs.append(_nn(p.astype(BF), vb) * (1.0 / den))
                lses.append(mx + jnp.log(den))
            o_ref[0, _dil_rows(r, q0, QB, dil), :] = jnp.where(ma, outs[0], outs[1])
            lse_ref[0, _dil_rows(r, q0, QB, dil), :] = jnp.where(ma, lses[0], lses[1])

        _dil_walk(block, dil, nb)

    colblk = lambda off: pl.BlockSpec((1, s, LANES), lambda b, p: (b, 0, off + p))
    o, lse = pl.pallas_call(
        body, name=name, grid=(bl, co),
        in_specs=[colblk(g * co), colblk(g * co), colblk(ck // 2 + g * co)],
        out_specs=[colblk(0), colblk(0)],
        out_shape=[jax.ShapeDtypeStruct((bl, s, gw), F32), jax.ShapeDtypeStruct((bl, s, gw), F32)],
        compiler_params=_params("parallel", "parallel"),
    )(projb.reshape(bl, s, wq), kv.reshape(bl, s, wk), kv.reshape(bl, s, wk))
    return o.reshape(t, gw), lse.reshape(t, gw)


def dil_bwd(projb, kv, lse, dog, dshift, g, bl, *, name):
    _, dil = DIL_GROUPS[g]
    t, wq = projb.shape
    wk = kv.shape[1]
    s = t // bl
    ln = s // dil
    nb = ln // QB
    gw = 4 * HEAD_DIM
    ck, co = wk // LANES, gw // LANES

    def body(q_ref, k_ref, v_ref, lse_ref, do_ref, sh_ref, dq_ref, dk_ref, dv_ref):
        sa, sb = _dil_slopes(g, dil)
        ma, mb = _lane_masks(QB)
        dk_ref[...] = jnp.zeros_like(dk_ref)
        dv_ref[...] = jnp.zeros_like(dv_ref)

        def block(r, n, n_keys):
            q0 = n * QB
            k0 = q0 - (n_keys - QB)
            qrows, krows = _dil_rows(r, q0, QB, dil), _dil_rows(r, k0, n_keys, dil)
            q = q_ref[0, qrows, :].astype(BF)
            do = do_ref[0, qrows, :].astype(BF)
            lse_b = lse_ref[0, qrows, :]
            sh_b = sh_ref[0, qrows, :]
            kb = k_ref[0, krows, :].astype(BF)
            vb = v_ref[0, krows, :].astype(BF)
            dqs = []
            for m, sl, c0 in ((ma, sa, 0), (mb, sb, HEAD_DIM)):
                qh = jnp.where(m, q, jnp.zeros_like(q))
                doh = jnp.where(m, do, jnp.zeros_like(do))
                sc = _dil_scores(qh, kb, sl, n_keys)
                p = jnp.exp(sc - lse_b[:, c0:c0 + 1])
                ds = p * (_nt(doh, vb) - sh_b[:, c0:c0 + 1])
                dss = (ds * SCALE).astype(BF)
                dk_ref[0, krows, :] += _tn(dss, qh)
                dv_ref[0, krows, :] += _tn(p.astype(BF), doh)
                dqs.append(_nn(dss, kb))
            dq_ref[0, qrows, :] = jnp.where(ma, dqs[0], dqs[1])

        _dil_walk(block, dil, nb)

    colblk = lambda off: pl.BlockSpec((1, s, LANES), lambda b, p: (b, 0, off + p))
    dq, dk, dv = pl.pallas_call(
        body, name=name, grid=(bl, co),
        in_specs=[colblk(g * co), colblk(g * co), colblk(ck // 2 + g * co), colblk(0), colblk(g * co), colblk(g * co)],
        out_specs=[colblk(0)] * 3,
        out_shape=[jax.ShapeDtypeStruct((bl, s, gw), F32)] * 3,
        compiler_params=_params("parallel", "parallel"),
    )(projb.reshape(bl, s, wq), kv.reshape(bl, s, wk), kv.reshape(bl, s, wk), lse.reshape(bl, s, gw),
      dog.reshape(bl, s, DIL_WIDTH), dshift.reshape(bl, s, DIL_WIDTH))
    return dq.reshape(t, gw), dk.reshape(t, gw), dv.reshape(t, gw)


def _group_weights(lses):
    mx = jnp.maximum(jnp.maximum(lses[0], lses[1]), lses[2])
    es = [jnp.exp(l - mx) for l in lses]
    inv = 1.0 / (es[0] + es[1] + es[2])
    return [e * inv for e in es]


def dil_combine_fwd(os_, lses, *, name):
    t, gw = os_[0].shape
    tr = _pick(t, (512, 256))

    def body(o0, o1, o2, l0, l1, l2, out_ref):
        al = _group_weights([l0[...], l1[...], l2[...]])
        for g, o_ref in enumerate((o0, o1, o2)):
            out_ref[:, g * gw:(g + 1) * gw] = (o_ref[...] * al[g]).astype(out_ref.dtype)

    blk = pl.BlockSpec((tr, gw), lambda i: (i, 0))
    return pl.pallas_call(
        body, name=name, grid=(t // tr,), in_specs=[blk] * 6,
        out_specs=pl.BlockSpec((tr, 3 * gw), lambda i: (i, 0)), out_shape=jax.ShapeDtypeStruct((t, 3 * gw), BF),
        compiler_params=_params("parallel"),
    )(*os_, *lses)


def dil_combine_bwd(os_, lses, dcat, *, name):
    t, gw = os_[0].shape
    tr = _pick(t, (512, 256))

    def head_sum(v):
        parts = []
        for c in range(gw // LANES):
            blk = v[:, c * LANES:(c + 1) * LANES]
            ma, _ = _lane_masks(tr)
            sa = jnp.sum(jnp.where(ma, blk, 0.0), axis=1, keepdims=True)
            sb = jnp.sum(blk, axis=1, keepdims=True) - sa
            parts.append(jnp.where(ma, sa, sb))
        return jnp.concatenate(parts, axis=1)

    def body(o0, o1, o2, l0, l1, l2, d_ref, dog_ref, sh_ref):
        al = _group_weights([l0[...], l1[...], l2[...]])
        dos = [d_ref[:, g * gw:(g + 1) * gw].astype(F32) for g in range(3)]
        dal = [head_sum(dos[g] * o_ref[...]) for g, o_ref in enumerate((o0, o1, o2))]
        mix = al[0] * dal[0] + al[1] * dal[1] + al[2] * dal[2]
        for g in range(3):
            dog_ref[:, g * gw:(g + 1) * gw] = (al[g] * dos[g]).astype(dog_ref.dtype)
            sh_ref[:, g * gw:(g + 1) * gw] = al[g] * mix

    blk = pl.BlockSpec((tr, gw), lambda i: (i, 0))
    wide = pl.BlockSpec((tr, 3 * gw), lambda i: (i, 0))
    return pl.pallas_call(
        body, name=name, grid=(t // tr,), in_specs=[blk] * 6 + [wide], out_specs=[wide, wide],
        out_shape=[jax.ShapeDtypeStruct((t, 3 * gw), F32), jax.ShapeDtypeStruct((t, 3 * gw), F32)],
        compiler_params=_params("parallel"),
    )(*os_, *lses, dcat)


def adamw(w, g1, g2, m, v, *, name):
    r, c = w.shape
    tr = r
    for cand in (256, 128, 64, 32, 16, 8):
        if r % cand == 0 and cand * c * 4 <= (1 << 20):
            tr = cand
            break
    two = g2 is not None

    def body(*refs):
        if two:
            w_ref, g1_ref, g2_ref, m_ref, v_ref, g_ref, d_ref, nm_ref, nv_ref = refs
            g = g1_ref[...] + g2_ref[...]
        else:
            w_ref, g1_ref, m_ref, v_ref, g_ref, d_ref, nm_ref, nv_ref = refs
            g = g1_ref[...]
        nm = ADAM_B1 * m_ref[...] + (1.0 - ADAM_B1) * g
        nv = ADAM_B2 * v_ref[...] + (1.0 - ADAM_B2) * (g * g)
        m_hat = nm / (1.0 - ADAM_B1 ** ADAM_STEP)
        v_hat = nv / (1.0 - ADAM_B2 ** ADAM_STEP)
        g_ref[...] = g
        d_ref[...] = -ADAM_LR * (m_hat / (jnp.sqrt(v_hat) + ADAM_EPS) + ADAM_WD * w_ref[...])
        nm_ref[...] = nm
        nv_ref[...] = nv

    blk = pl.BlockSpec((tr, c), lambda i: (i, 0))
    args = [w, g1] + ([g2] if two else []) + [m, v]
    return pl.pallas_call(
        body, name=name, grid=(r // tr,), in_specs=[blk] * len(args), out_specs=[blk] * 4,
        out_shape=[jax.ShapeDtypeStruct((r, c), F32)] * 4, compiler_params=_params("parallel"),
    )(*args)


def sum4(own, land, *, name):
    r, c = own.shape
    tr = _pick(r, (256, 128, 64))

    def body(o_ref, l_ref, s_ref):
        s_ref[...] = ((o_ref[...].astype(F32) + l_ref[0].astype(F32)) + l_ref[1].astype(F32)) + l_ref[2].astype(F32)

    return pl.pallas_call(
        body, name=name, grid=(r // tr,),
        in_specs=[pl.BlockSpec((tr, c), lambda i: (i, 0)), pl.BlockSpec((3, tr, c), lambda i: (0, i, 0))],
        out_specs=pl.BlockSpec((tr, c), lambda i: (i, 0)), out_shape=jax.ShapeDtypeStruct((r, c), F32),
        compiler_params=_params("parallel"),
    )(own, land)


ANY = pl.BlockSpec(memory_space=pl.ANY)


def _place():
    x, y, c = lax.axis_index("x"), lax.axis_index("y"), lax.axis_index("c")
    chips = [(1 - x, y), (x, 1 - y), (1 - x, 1 - y)]
    return x, y, c, chips


class Ride:
    def __init__(self, arrays, out_shape, copies):
        self.arrays, self.out_shape, self.copies = list(arrays), list(out_shape), copies
        n = 3 * len(self.arrays)
        self.scratch = [pltpu.SemaphoreType.DMA((n,)), pltpu.SemaphoreType.DMA((n,)),
                        pltpu.SemaphoreType.DMA((len(self.arrays),))]

    def split(self, refs):
        n, m = len(self.arrays), len(self.out_shape)
        return refs[:n], refs[n:n + m], refs[n + m:]

    def start(self, ins, outs, sems):
        local, sends, _ = self.copies(ins, outs, *sems)
        for cp in local + sends:
            cp.start()

    def wait(self, ins, outs, sems):
        local, sends, arrivals = self.copies(ins, outs, *sems)
        for cp in arrivals:
            cp.wait_recv()
        for cp in sends:
            cp.wait_send()
        for cp in local:
            cp.wait()

    def run(self, name):
        def body(*refs):
            ins, outs, sems = self.split(refs)
            self.start(ins, outs, sems)
            self.wait(ins, outs, sems)

        return pl.pallas_call(body, name=name, in_specs=[ANY] * len(self.arrays), out_specs=[ANY] * len(self.out_shape),
                              out_shape=self.out_shape, scratch_shapes=self.scratch)(*self.arrays)


def gather_ride(shards, axes):
    def copies(ins, outs, send_sems, recv_sems, local_sems):
        x, y, c, chips = _place()

        def slot(a, q):
            size = shards[a].shape[axes[a]]
            start = pl.multiple_of(q * size, size)
            return outs[a].at[pl.ds(start, size), :] if axes[a] == 0 else outs[a].at[:, pl.ds(start, size)]

        def remote(a, k, q):
            px, py = chips[k]
            return pltpu.make_async_remote_copy(src_ref=ins[a], dst_ref=slot(a, q), send_sem=send_sems.at[3 * a + k],
                                                recv_sem=recv_sems.at[3 * a + k], device_id=(px, py, c), device_id_type=MESH)

        me = 2 * x + y
        n = len(shards)
        local = [pltpu.make_async_copy(ins[a], slot(a, me), local_sems.at[a]) for a in range(n)]
        sends = [remote(a, k, me) for a in range(n) for k in range(3)]
        arrivals = [remote(a, k, 2 * chips[k][0] + chips[k][1]) for a in range(n) for k in range(3)]
        return local, sends, arrivals

    out_shape = []
    for a, sh in enumerate(shards):
        full = list(sh.shape)
        full[axes[a]] *= N_CHIPS
        out_shape.append(jax.ShapeDtypeStruct(tuple(full), sh.dtype))
    return Ride(shards, out_shape, copies)


def scatter_ride(grads):
    def copies(ins, outs, send_sems, recv_sems, local_sems):
        x, y, c, chips = _place()
        sends = [pltpu.make_async_remote_copy(src_ref=ins[a].at[2 * px + py], dst_ref=outs[a].at[k],
                                              send_sem=send_sems.at[3 * a + k], recv_sem=recv_sems.at[3 * a + k],
                                              device_id=(px, py, c), device_id_type=MESH)
                 for a in range(len(grads)) for k, (px, py) in enumerate(chips)]
        return [], sends, sends

    return Ride(grads, [jax.ShapeDtypeStruct((3,) + g.shape[1:], g.dtype) for g in grads], copies)


def swap_ride(arrs):
    def copies(ins, outs, send_sems, recv_sems, local_sems):
        x, y, c, _ = _place()
        sends = [pltpu.make_async_remote_copy(src_ref=ins[a], dst_ref=outs[a], send_sem=send_sems.at[a],
                                              recv_sem=recv_sems.at[a], device_id=(x, y, 1 - c), device_id_type=MESH)
                 for a in range(len(arrs))]
        return [], sends, sends

    return Ride(arrs, [jax.ShapeDtypeStruct(a.shape, a.dtype) for a in arrs], copies)


def all_reduce_small(v, *, name):
    rows = v.shape[0]

    def body(v_ref, o_ref, gath, send_sems, recv_sems):
        x, y, c, _ = _place()
        me = 4 * x + 2 * y + c
        gath[me] = v_ref[...]
        sends = []
        for msk in range(1, N_DEV):
            peer = (x ^ (msk >> 2), y ^ ((msk >> 1) & 1), c ^ (msk & 1))
            cp = pltpu.make_async_remote_copy(src_ref=v_ref, dst_ref=gath.at[me], send_sem=send_sems.at[msk - 1],
                                              recv_sem=recv_sems.at[msk - 1], device_id=peer, device_id_type=MESH)
            cp.start()
            sends.append(cp)
        for msk in range(1, N_DEV):
            pltpu.make_async_remote_copy(src_ref=v_ref, dst_ref=gath.at[me ^ msk], send_sem=send_sems.at[msk - 1],
                                         recv_sem=recv_sems.at[msk - 1], device_id=(x, y, c), device_id_type=MESH).wait_recv()
        for cp in sends:
            cp.wait_send()
        tot = gath[0]
        for q in range(1, N_DEV):
            tot = tot + gath[q]
        o_ref[...] = tot

    vm = pl.BlockSpec(memory_space=pltpu.VMEM)
    return pl.pallas_call(
        body, name=name, in_specs=[vm], out_specs=vm, out_shape=jax.ShapeDtypeStruct(v.shape, F32),
        scratch_shapes=[pltpu.VMEM((N_DEV, rows, LANES), F32), pltpu.SemaphoreType.DMA((N_DEV - 1,)),
                        pltpu.SemaphoreType.DMA((N_DEV - 1,))],
    )(v)


def _ffn_fwd(xin, gain, w_up, w_conv, w_down, bl, tag, rides=(None, None, None)):
    got = {}

    def carried(k, result, delivered):
        if rides[k] is not None:
            got.update(zip(rides[k][1], delivered))
        return result

    n = rmsnorm_fwd(xin, gain, name=f"{tag}_ffn_norm")
    if rides[0] is None:
        u = matmul(n, w_up, name=f"{tag}_ffn_up")
    else:
        u = carried(0, *matmul(n, w_up, ride=rides[0][0], name=f"{tag}_ffn_up"))
    *hc, delivered = conv_silu_fwd(u, w_conv, bl, name=f"{tag}_ffn_conv", ride=rides[1][0] if rides[1] else None)
    h, ca, cg = carried(1, hc, delivered)
    if rides[2] is None:
        xout = matmul(h, w_down, out_dtype=F32, add=xin, name=f"{tag}_ffn_down")
    else:
        xout = carried(2, *matmul(h, w_down, out_dtype=F32, add=xin, ride=rides[2][0], name=f"{tag}_ffn_down"))
    return xout, (n, u, h, ca, cg), got


def _ffn_bwd(dxout, dxout_b, xin, saved, gain, w_up, w_conv, w_down, bl, tag, shard):
    n, u, h, ca, cg = saved
    dh = matmul(dxout_b, w_down, tb=True, name=f"{tag}_ffn_down_dx")
    g_down = matmul(h, dxout_b, ta=True, name=f"{tag}_ffn_down_dw")
    dua, dug, g_conv = conv_silu_bwd(u, ca, cg, w_conv, dh, bl, name=f"{tag}_ffn_conv_bwd")
    dn = matmul(dua, w_up, a2=dug, tb=True, out_dtype=F32, name=f"{tag}_ffn_up_dx")
    g_up = matmul(n, dua, b2=dug, ta=True, shard_cols=shard, name=f"{tag}_ffn_up_dw")
    dxin, dxin_b, g_norm = rmsnorm_bwd(xin, gain, dn, dxout, name=f"{tag}_ffn_norm_bwd")
    return dxin, dxin_b, g_norm, g_up, g_conv, g_down


def _mem_kv(mem2, gain, w_kv, tag):
    mn = rmsnorm_fwd(mem2, gain, name=f"{tag}_mem_norm")
    return mn, matmul(mn, w_kv, name=f"{tag}_mem_kv")


def _mem_kv_bwd(mem2, mn, gain, w_kv, dkvm, tag):
    g_kv = matmul(mn, dkvm, ta=True, name=f"{tag}_mem_kv_dw")
    dmn = matmul(dkvm, w_kv, tb=True, out_dtype=F32, name=f"{tag}_mem_kv_dx")
    _, _, g_norm = rmsnorm_bwd(mem2, gain, dmn, None, name=f"{tag}_mem_norm_bwd")
    return g_norm, g_kv


def _shard_major(name, grad):
    if name in COL_SHARDED:
        return grad
    return grad.reshape(N_CHIPS, grad.shape[0] // N_CHIPS, grad.shape[1])


def local_step(x, mem, target, w, shard_of, late_gather=None, early_scatter=(), scatter_last=(), reduce_early=None):
    bl, s, d = x.shape
    t = bl * s
    x0 = x.reshape(t, d)
    mem2 = mem.reshape(bl * MEM_LEN, d)
    tgt = target.reshape(t, d)
    g = {}

    n1 = rmsnorm_fwd(x0, w["a_norm_attn"], name="a_attn_norm")
    proj_a = matmul(n1, w["a_w_in"], name="a_w_in")
    mn_a, kvm_a = _mem_kv(mem2, w["a_norm_mem"], w["a_w_mem_kv"], "a")
    late_gather = late_gather or {}
    behind_sb = late_gather.get("sb")
    o_sb, o_sb_f32, delivered = sb_fwd(proj_a, bl, name="a_sb_fwd", ride=behind_sb[0] if behind_sb else None)
    if behind_sb:
        w = {**w, **dict(zip(behind_sb[1], delivered))}
    o_mem_a = mem_fwd(proj_a, kvm_a, bl, name="a_mem_fwd")
    cat_a = jnp.concatenate([o_sb, o_mem_a], axis=1)
    x1 = matmul(cat_a, w["a_w_out"], out_dtype=F32, add=x0, name="a_w_out")
    x2, ffn_a, got = _ffn_fwd(x1, w["a_norm_ffn"], w["a_ffn_up"], w["a_ffn_conv"], w["a_ffn_down"], bl, "a",
                              rides=tuple(late_gather.get(host) for host in ("up", "conv", "down")))
    w = {**w, **got}

    nkv, n3 = rmsnorm_fwd2(x2, w["kv_norm"], w["b_norm_attn"], name="kv_b_attn_norm")
    kv = matmul(nkv, w["w_kv_shared"], out_dtype=F32, name="w_kv")
    proj_b = matmul(n3, w["b_w_in"], out_dtype=F32, name="b_w_in")
    mn_b, kvm_b = _mem_kv(mem2, w["b_norm_mem"], w["b_w_mem_kv"], "b")
    dil = [dil_fwd(proj_b, kv, gi, bl, name=f"b_dil_fwd{gi}") for gi in range(3)]
    os_, lses = [o for o, _ in dil], [l for _, l in dil]
    o_dil = dil_combine_fwd(os_, lses, name="b_dil_combine")
    o_mem_b = mem_fwd(proj_b, kvm_b, bl, name="b_mem_fwd")
    cat_b = jnp.concatenate([o_dil, o_mem_b], axis=1)
    x3 = matmul(cat_b, w["b_w_out"], out_dtype=F32, add=x2, name="b_w_out")
    x4, ffn_b, _ = _ffn_fwd(x3, w["b_norm_ffn"], w["b_ffn_up"], w["b_ffn_conv"], w["b_ffn_down"], bl, "b")

    dx4, dx4b, g["final_norm"], lossvec = final_loss(x4, w["final_norm"], tgt, name="final_loss")

    dx3, dx3b, g["b_norm_ffn"], g["b_ffn_up"], g["b_ffn_conv"], g["b_ffn_down"] = _ffn_bwd(
        dx4, dx4b, x3, ffn_b, w["b_norm_ffn"], w["b_ffn_up"], w["b_ffn_conv"], w["b_ffn_down"], bl, "b",
        shard_of["b_ffn_up"])
    dcat_b = matmul(dx3b, w["b_w_out"], tb=True, name="b_w_out_dx")
    g["b_w_out"] = matmul(cat_b, dx3b, ta=True, name="b_w_out_dw")
    dog, dshift = dil_combine_bwd(os_, lses, dcat_b, name="b_dil_combine_bwd")
    dqs, dks, dvs = [], [], []
    for gi in range(3):
        dq_g, dk_g, dv_g = dil_bwd(proj_b, kv, lses[gi], dog, dshift, gi, bl, name=f"b_dil_bwd{gi}")
        dqs.append(dq_g), dks.append(dk_g), dvs.append(dv_g)
    dq_mem_b, dkvm_b = mem_bwd(proj_b, kvm_b, dcat_b, bl, name="b_mem_bwd")
    g["b_norm_mem"], g["b_w_mem_kv"] = _mem_kv_bwd(mem2, mn_b, w["b_norm_mem"], w["b_w_mem_kv"], dkvm_b, "b")
    dproj_b = jnp.concatenate([dq_g.astype(BF) for dq_g in dqs] + [dq_mem_b], axis=1)
    dn3 = matmul(dproj_b, w["b_w_in"], tb=True, out_dtype=F32, name="b_w_in_dx")
    g["b_w_in"] = matmul(n3, dproj_b, ta=True, name="b_w_in_dw")
    dkv = jnp.concatenate(dks + dvs, axis=1).astype(BF)
    dnkv = matmul(dkv, w["w_kv_shared"], tb=True, out_dtype=F32, name="w_kv_dx")
    g["w_kv_shared"] = matmul(nkv, dkv, ta=True, shard_cols=shard_of["w_kv_shared"], name="w_kv_dw")
    dx2, dx2b, g["kv_norm"], g["b_norm_attn"] = rmsnorm_bwd2(x2, w["kv_norm"], w["b_norm_attn"], dnkv, dn3, dx3,
                                                             name="kv_b_attn_norm_bwd")

    dx1, dx1b, g["a_norm_ffn"], g["a_ffn_up"], g["a_ffn_conv"], g["a_ffn_down"] = _ffn_bwd(
        dx2, dx2b, x1, ffn_a, w["a_norm_ffn"], w["a_ffn_up"], w["a_ffn_conv"], w["a_ffn_down"], bl, "a",
        shard_of["a_ffn_up"])
    dcat_a = matmul(dx1b, w["a_w_out"], tb=True, name="a_w_out_dx")
    g["a_w_out"] = matmul(cat_a, dx1b, ta=True, name="a_w_out_dw")
    leaving = scatter_ride([_shard_major(n, g[n]) for n in early_scatter]) if early_scatter else None
    dq_sb, dk_sb, dv_sb, landed = sb_bwd(proj_a, o_sb_f32, dcat_a, bl, name="a_sb_bwd", ride=leaving)
    landed = dict(zip(early_scatter, landed))
    dq_mem_a, dkvm_a = mem_bwd(proj_a, kvm_a, dcat_a, bl, name="a_mem_bwd")
    g["a_norm_mem"], g["a_w_mem_kv"] = _mem_kv_bwd(mem2, mn_a, w["a_norm_mem"], w["a_w_mem_kv"], dkvm_a, "a")
    dproj_a = jnp.concatenate([dq_sb, dk_sb.astype(BF), dv_sb.astype(BF), dq_mem_a], axis=1)
    if reduce_early is None:
        dn1 = matmul(dproj_a, w["a_w_in"], tb=True, out_dtype=F32, name="a_w_in_dx")
        g["a_w_in"] = matmul(n1, dproj_a, ta=True, shard_cols=shard_of["a_w_in"], name="a_w_in_dw")
        dx0, _, g["a_norm_attn"] = rmsnorm_bwd(x0, w["a_norm_attn"], dn1, dx1, name="a_attn_norm_bwd")
        return lossvec, dx0, g, landed
    sums = reduce_early(g, landed)
    dn1, theirs = matmul(dproj_a, w["a_w_in"], tb=True, out_dtype=F32, name="a_w_in_dx",
                         ride=swap_ride([sums[n] for n in early_scatter]))
    g["a_w_in"] = matmul(n1, dproj_a, ta=True, shard_cols=shard_of["a_w_in"], name="a_w_in_dw")
    dx0, _, g["a_norm_attn"], landed_last = rmsnorm_bwd(
        x0, w["a_norm_attn"], dn1, dx1, name="a_attn_norm_bwd",
        ride=scatter_ride([_shard_major(n, g[n]) for n in scatter_last]))
    return lossvec, dx0, g, (sums, dict(zip(early_scatter, theirs)), dict(zip(scatter_last, landed_last)))


MATRICES = ("a_w_in", "a_w_out", "a_w_mem_kv", "a_ffn_up", "a_ffn_down", "w_kv_shared", "b_w_in", "b_w_out",
            "b_w_mem_kv", "b_ffn_up", "b_ffn_down")
COL_SHARDED = ("a_w_in", "a_ffn_up", "w_kv_shared", "b_ffn_up")
FIRST_NEEDED = ("a_w_in", "a_w_mem_kv")
GATHER_BEHIND = {"sb": ("a_w_out", "a_ffn_up", "a_ffn_down", "b_ffn_up"), "up": ("b_ffn_down", "w_kv_shared"),
                 "conv": ("b_w_in", "b_w_out", "b_w_mem_kv")}
SMALL_SHARDED = ("a_norm_attn", "a_norm_mem", "a_norm_ffn", "a_ffn_conv", "b_ffn_conv")
SMALL_REPLICATED = ("kv_norm", "b_norm_attn", "b_norm_mem", "b_norm_ffn", "final_norm")
WEIGHTS = ("a_norm_attn", "a_w_in", "a_w_out", "a_norm_mem", "a_w_mem_kv", "a_norm_ffn", "a_ffn_up", "a_ffn_conv",
           "a_ffn_down", "kv_norm", "w_kv_shared", "b_norm_attn", "b_w_in", "b_w_out", "b_norm_mem", "b_w_mem_kv",
           "b_norm_ffn", "b_ffn_up", "b_ffn_conv", "b_ffn_down", "final_norm")


def _two_d(a):
    if a.ndim == 1:
        return a.reshape(1, -1)
    return a.reshape(a.shape[-2], a.shape[-1])


def kernel(x, mem, a_norm_attn, a_w_in, a_w_out, a_norm_mem, a_w_mem_kv, a_norm_ffn, a_ffn_up, a_ffn_conv, a_ffn_down, kv_norm, w_kv_shared, b_norm_attn, b_w_in, b_w_out, b_norm_mem, b_w_mem_kv, b_norm_ffn, b_ffn_up, b_ffn_conv, b_ffn_down, final_norm, loss_target, m_a_norm_attn, m_a_w_in, m_a_w_out, m_a_norm_mem, m_a_w_mem_kv, m_a_norm_ffn, m_a_ffn_up, m_a_ffn_conv, m_a_ffn_down, m_kv_norm, m_w_kv_shared, m_b_norm_attn, m_b_w_in, m_b_w_out, m_b_norm_mem, m_b_w_mem_kv, m_b_norm_ffn, m_b_ffn_up, m_b_ffn_conv, m_b_ffn_down, m_final_norm, v_a_norm_attn, v_a_w_in, v_a_w_out, v_a_norm_mem, v_a_w_mem_kv, v_a_norm_ffn, v_a_ffn_up, v_a_ffn_conv, v_a_ffn_down, v_kv_norm, v_w_kv_shared, v_b_norm_attn, v_b_w_in, v_b_w_out, v_b_norm_mem, v_b_w_mem_kv, v_b_norm_ffn, v_b_ffn_up, v_b_ffn_conv, v_b_ffn_down, v_final_norm):
    given = dict(locals())
    wl = {n: _two_d(given[n]) for n in WEIGHTS}
    ml = {n: _two_d(given["m_" + n]) for n in WEIGHTS}
    vl = {n: _two_d(given["v_" + n]) for n in WEIGHTS}
    chip = 2 * lax.axis_index("x") + lax.axis_index("y")

    packed = jnp.concatenate([wl[n].reshape(-1, LANES) for n in SMALL_SHARDED], axis=0)
    axis_of = lambda n: 1 if n in COL_SHARDED else 0
    late = tuple(n for n in MATRICES if n not in FIRST_NEEDED)
    full = gather_ride([wl[n].astype(BF) for n in FIRST_NEEDED] + [packed],
                       [axis_of(n) for n in FIRST_NEEDED] + [0]).run("gather_first")
    w = dict(zip(FIRST_NEEDED, full[:-1]))
    late_gather = {host: (gather_ride([wl[n].astype(BF) for n in names], [axis_of(n) for n in names]), names)
                   for host, names in GATHER_BEHIND.items()}
    rows = packed.shape[0]
    per_chip = full[-1].reshape(N_CHIPS, rows, LANES)
    r0 = 0
    for n in SMALL_SHARDED:
        nr = wl[n].size // LANES
        piece = per_chip[:, r0:r0 + nr].reshape(N_CHIPS, wl[n].shape[0], wl[n].shape[1])
        w[n] = jnp.concatenate([piece[q] for q in range(N_CHIPS)], axis=1)
        r0 += nr
    for n in SMALL_REPLICATED:
        w[n] = wl[n]

    shard_of = {n: wl[n].shape[1] for n in COL_SHARDED}
    def core_sums(g, landed):
        return {n: sum4(lax.dynamic_index_in_dim(_shard_major(n, g[n]), chip, 0, keepdims=False), landed[n],
                        name=f"sum4_{n}") for n in landed}

    lossvec, dx0, g, (sums, theirs, landed_last) = local_step(
        x, mem, loss_target, w, shard_of, late_gather=late_gather, early_scatter=late, scatter_last=FIRST_NEEDED,
        reduce_early=core_sums)
    loss = lax.psum(0.5 * jnp.sum(lossvec) / x.shape[-1], ("x", "y", "c"))
    sums.update(core_sums(g, landed_last))
    theirs.update(zip(FIRST_NEEDED, swap_ride([sums[n] for n in FIRST_NEEDED]).run("swap_last")))
    out = {}
    for n in MATRICES:
        out[n] = adamw(wl[n], sums[n], theirs[n], ml[n], vl[n], name=f"adamw_{n}")

    small = SMALL_SHARDED + SMALL_REPLICATED
    flat = jnp.concatenate([g[n].reshape(-1, LANES) for n in small], axis=0)
    tot = all_reduce_small(flat, name="all_reduce_small")
    r0 = 0
    for n in small:
        nr = g[n].size // LANES
        gn = tot[r0:r0 + nr].reshape(g[n].shape)
        r0 += nr
        if n in SMALL_SHARDED:
            gn = lax.dynamic_slice_in_dim(gn, chip * wl[n].shape[1], wl[n].shape[1], axis=1)
        out[n] = adamw(wl[n], gn, None, ml[n], vl[n], name=f"adamw_{n}")

    res = [loss, dx0.reshape(x.shape)]
    for slot in range(4):
        res += [out[n][slot].reshape(given[n].shape) for n in WEIGHTS]
    return tuple(res)
```

```python
import functools
import math

import jax
import jax.numpy as jnp
from jax import lax
from jax.experimental import pallas as pl
from jax.experimental.pallas import tpu as pltpu

F32 = jnp.float32
BF = jnp.bfloat16
MESH = pl.DeviceIdType.MESH

HEAD_DIM = 64
LANES = 128
SB_WIDTH = 12 * HEAD_DIM
MEM_WIDTH = 4 * HEAD_DIM
DIL_WIDTH = 12 * HEAD_DIM
MEM_LEN = 256
DIL_GROUPS = ((128, 1), (512, 4), (2048, 16))
QB = 128
EPS = 1e-6
SCALE = HEAD_DIM ** -0.5
NEG = -1e30
ALIBI = tuple(2.0 ** (-8.0 * i / 12) for i in range(1, 13))
N_CHIPS = 4
N_DEV = 8

ADAM_LR, ADAM_B1, ADAM_B2, ADAM_EPS, ADAM_WD, ADAM_STEP = 0.001, 0.9, 0.999, 1e-08, 0.01, 10

VMEM_LIMIT = 52 * 1024 * 1024
MATMUL_VMEM_BUDGET = 44 * 1024 * 1024
MATMUL_MAX_TN = 2816
HBM_BYTES_PER_S = 3.0e12
VMEM_BYTES_PER_S = 6.0e12
STEP_OVERHEAD_S = 0.4e-6


def _mo(v, m):
    return v if isinstance(v, int) else pl.multiple_of(v, m)


def _pick(n, prefs):
    for t in prefs:
        if n % t == 0:
            return t
    return n


def _params(*sem):
    return pltpu.CompilerParams(dimension_semantics=sem, vmem_limit_bytes=VMEM_LIMIT)


def matmul(a, b, *, ta=False, tb=False, out_dtype=BF, add=None, shard_cols=0, a2=None, b2=None, ride=None, name):
    m, k = (a.shape[1], a.shape[0]) if ta else a.shape
    n = b.shape[0] if tb else b.shape[1]
    if a2 is not None:
        assert not ta and a2.shape == a.shape
        k *= 2
    if b2 is not None:
        assert not tb and b2.shape == b.shape
        n *= 2
    k_piece = a.shape[1] if a2 is not None else k
    n_piece = b.shape[1] if b2 is not None else n
    out_bytes = jnp.dtype(out_dtype).itemsize
    pieces_a, pieces_b = (2 if a2 is not None else 1), (2 if b2 is not None else 1)
    if shard_cols:
        tn_choices = [c * shard_cols for c in (4, 2, 1) if c * shard_cols <= MATMUL_MAX_TN and c <= 4 // pieces_b]
    else:
        tn_choices = [c for c in (2816, 2560, 1536, 1408, 1024, 512, 256, 128) if n_piece % c == 0]

    def vmem_bytes(tm_, tn_, tk_):
        blocks = (tm_ * tk_ * a.dtype.itemsize * pieces_a + tk_ * tn_ * b.dtype.itemsize * pieces_b
                  + tm_ * tn_ * out_bytes + (tm_ * tn_ * 4 if add is not None else 0))
        return 2 * blocks + (tm_ * tn_ * 4 if k // tk_ > 1 else 0) + tm_ * tn_ * 4

    def cost(c):
        tm_, tn_, tk_ = c
        steps = (m // tm_) * (n // tn_) * (k // tk_)
        hbm = (m * k * a.dtype.itemsize * (n // tn_) + k * n * b.dtype.itemsize * (m // tm_)
               + m * n * (out_bytes + (4 if add is not None else 0)))
        return hbm / HBM_BYTES_PER_S + steps * STEP_OVERHEAD_S + m * n * 8 * (k // tk_ - 1) / VMEM_BYTES_PER_S

    tm, tn, tk = min(((tm_, tn_, tk_) for tm_ in (1024, 512, 1408, 256, 128) if m % tm_ == 0 for tn_ in tn_choices
                      for tk_ in (2816, 2560, 2048, 1536, 1024, 512, 256, 128) if k_piece % tk_ == 0
                      if vmem_bytes(tm_, tn_, tk_) <= MATMUL_VMEM_BUDGET), key=cost)
    ns = tn // shard_cols if shard_cols else 1
    nk = k // tk
    nk1 = nk // 2
    nj1 = (n // tn) // 2
    dims = (((0,) if ta else (1,), (1,) if tb else (0,)), ((), ()))
    n_in = 2 + (a2 is not None) + (b2 is not None) + (add is not None)
    grid = (m // tm, n // tn, nk)

    def body(*refs):
        own_in, (o_ref,), scr, ride_start, ride_wait = _ride_hooks(ride, refs, n_in, 1, int(nk > 1), grid)
        ride_start()
        ins = list(own_in)
        a_ref, b_ref = ins.pop(0), ins.pop(0)
        a2_ref = ins.pop(0) if a2 is not None else None
        b2_ref = ins.pop(0) if b2 is not None else None
        add_ref = ins.pop(0) if add is not None else None

        a_tile, b_tile = a_ref[...], b_ref[...]
        if a2 is not None:
            a_tile = jnp.where(pl.program_id(2) < nk1, a_tile, a2_ref[...])
        if b2 is not None:
            b_tile = jnp.where(pl.program_id(1) < nj1, b_tile, b2_ref[...])
        part = lax.dot_general(a_tile.astype(BF), b_tile.astype(BF), dims, preferred_element_type=F32)

        def finish(r):
            if add is not None:
                r = r + add_ref[...]
            if shard_cols:
                for c in range(ns):
                    o_ref[c] = r[:, c * shard_cols:(c + 1) * shard_cols].astype(o_ref.dtype)
            else:
                o_ref[...] = r.astype(o_ref.dtype)

        if nk == 1:
            finish(part)
        else:
            acc_ref = scr[0]
            kk = pl.program_id(2)

            @pl.when(kk == 0)
            def _():
                acc_ref[...] = part

            @pl.when(kk > 0)
            def _():
                acc_ref[...] += part

            @pl.when(kk == nk - 1)
            def _():
                finish(acc_ref[...])
        ride_wait()

    a_spec = pl.BlockSpec((tk, tm), lambda i, j, q: (q, i)) if ta else pl.BlockSpec((tm, tk), lambda i, j, q: (i, q))
    b_spec = pl.BlockSpec((tn, tk), lambda i, j, q: (j, q)) if tb else pl.BlockSpec((tk, tn), lambda i, j, q: (q, j))
    if a2 is not None:
        a_spec = pl.BlockSpec((tm, tk), lambda i, j, q: (i, jnp.minimum(q, nk1 - 1)))
    if b2 is not None:
        b_spec = pl.BlockSpec((tk, tn), lambda i, j, q: (jnp.where(j < nj1, q, 0), jnp.minimum(j, nj1 - 1)))
    in_specs = [a_spec, b_spec]
    args = [a, b]
    if a2 is not None:
        in_specs.append(pl.BlockSpec((tm, tk), lambda i, j, q: (i, jnp.maximum(q - nk1, 0))))
        args.append(a2)
    if b2 is not None:
        in_specs.append(pl.BlockSpec((tk, tn), lambda i, j, q: (jnp.where(j < nj1, 0, q), jnp.maximum(j - nj1, 0))))
        args.append(b2)
    if add is not None:
        in_specs.append(pl.BlockSpec((tm, tn), lambda i, j, q: (i, j)))
        args.append(add)
    if shard_cols:
        out_shape = jax.ShapeDtypeStruct((N_CHIPS, m, shard_cols), out_dtype)
        out_spec = pl.BlockSpec((ns, tm, shard_cols), lambda i, j, q: (j, i, 0))
    else:
        out_shape = jax.ShapeDtypeStruct((m, n), out_dtype)
        out_spec = pl.BlockSpec((tm, tn), lambda i, j, q: (i, j))
    own_scratch = [pltpu.VMEM((tm, tn), F32)] if nk > 1 else []
    if ride is None:
        return pl.pallas_call(
            body, name=name, grid=grid, in_specs=in_specs, out_specs=out_spec, out_shape=out_shape,
            scratch_shapes=own_scratch, compiler_params=_params("parallel", "parallel", "arbitrary"),
        )(*args)
    out, *delivered = pl.pallas_call(
        body, name=name, grid=grid, in_specs=in_specs + [ANY] * len(ride.arrays),
        out_specs=[out_spec] + [ANY] * len(ride.out_shape), out_shape=[out_shape] + ride.out_shape,
        scratch_shapes=own_scratch + ride.scratch, compiler_params=_params("arbitrary", "arbitrary", "arbitrary"),
    )(*args, *ride.arrays)
    return out, delivered


def rmsnorm_fwd(x, g, *, name):
    t, d = x.shape
    tr = _pick(t, (512, 256))

    def body(x_ref, g_ref, o_ref):
        xv = x_ref[...]
        r = lax.rsqrt(jnp.mean(xv * xv, axis=-1, keepdims=True) + EPS)
        o_ref[...] = ((xv * r) * g_ref[...]).astype(o_ref.dtype)

    return pl.pallas_call(
        body, name=name, grid=(t // tr,),
        in_specs=[pl.BlockSpec((tr, d), lambda i: (i, 0)), pl.BlockSpec((1, d), lambda i: (0, 0))],
        out_specs=pl.BlockSpec((tr, d), lambda i: (i, 0)), out_shape=jax.ShapeDtypeStruct((t, d), BF),
        compiler_params=_params("parallel"),
    )(x, g)


def rmsnorm_bwd(x, g, dn, dres, *, name, ride=None):
    t, d = x.shape
    tr = _pick(t, (512, 256))
    want_dx = dres is not None

    def body(*refs):
        if want_dx:
            (x_ref, g_ref, dn_ref, dres_ref), (dx_ref, dxb_ref, dg_ref), _, ride_start, ride_wait = _ride_hooks(
                ride, refs, 4, 3, 0, (t // tr,))
            ride_start()
        else:
            x_ref, g_ref, dn_ref, dg_ref = refs

        @pl.when(pl.program_id(0) == 0)
        def _():
            dg_ref[...] = jnp.zeros_like(dg_ref)

        xv = x_ref[...]
        r = lax.rsqrt(jnp.mean(xv * xv, axis=-1, keepdims=True) + EPS)
        xn = xv * r
        dnv = dn_ref[...].astype(F32)
        dg_ref[...] += jnp.sum(dnv * xn, axis=0, keepdims=True)
        if want_dx:
            dyg = dnv * g_ref[...]
            cm = jnp.mean(dyg * xn, axis=-1, keepdims=True)
            dx = dres_ref[...] + r * (dyg - xn * cm)
            dx_ref[...] = dx
            dxb_ref[...] = dx.astype(BF)
            ride_wait()

    row = pl.BlockSpec((tr, d), lambda i: (i, 0))
    vec = pl.BlockSpec((1, d), lambda i: (0, 0))
    if want_dx:
        extra = ride.arrays if ride else []
        dx, dxb, dg, *delivered = pl.pallas_call(
            body, name=name, grid=(t // tr,), in_specs=[row, vec, row, row] + [ANY] * len(extra),
            out_specs=[row, row, vec] + [ANY] * (len(ride.out_shape) if ride else 0),
            out_shape=[jax.ShapeDtypeStruct((t, d), F32), jax.ShapeDtypeStruct((t, d), BF), jax.ShapeDtypeStruct((1, d), F32)]
            + (ride.out_shape if ride else []),
            scratch_shapes=ride.scratch if ride else [], compiler_params=_params("arbitrary"),
        )(x, g, dn, dres, *extra)
        return (dx, dxb, dg, delivered) if ride else (dx, dxb, dg)
    return None, None, pl.pallas_call(
        body, name=name, grid=(t // tr,), in_specs=[row, vec, row], out_specs=vec,
        out_shape=jax.ShapeDtypeStruct((1, d), F32), compiler_params=_params("arbitrary"),
    )(x, g, dn)


def rmsnorm_fwd2(x, g1, g2, *, name):
    t, d = x.shape
    tr = _pick(t, (512, 256))

    def body(x_ref, g1_ref, g2_ref, o1_ref, o2_ref):
        xv = x_ref[...]
        xn = xv * lax.rsqrt(jnp.mean(xv * xv, axis=-1, keepdims=True) + EPS)
        o1_ref[...] = (xn * g1_ref[...]).astype(o1_ref.dtype)
        o2_ref[...] = (xn * g2_ref[...]).astype(o2_ref.dtype)

    row = pl.BlockSpec((tr, d), lambda i: (i, 0))
    vec = pl.BlockSpec((1, d), lambda i: (0, 0))
    return pl.pallas_call(
        body, name=name, grid=(t // tr,), in_specs=[row, vec, vec], out_specs=[row, row],
        out_shape=[jax.ShapeDtypeStruct((t, d), BF)] * 2, compiler_params=_params("parallel"),
    )(x, g1, g2)


def rmsnorm_bwd2(x, g1, g2, dn1, dn2, dres, *, name):
    t, d = x.shape
    tr = _pick(t, (512, 256))

    def body(x_ref, g1_ref, g2_ref, dn1_ref, dn2_ref, dres_ref, dx_ref, dxb_ref, dg1_ref, dg2_ref):
        @pl.when(pl.program_id(0) == 0)
        def _():
            dg1_ref[...] = jnp.zeros_like(dg1_ref)
            dg2_ref[...] = jnp.zeros_like(dg2_ref)

        xv = x_ref[...]
        r = lax.rsqrt(jnp.mean(xv * xv, axis=-1, keepdims=True) + EPS)
        xn = xv * r
        d1, d2 = dn1_ref[...].astype(F32), dn2_ref[...].astype(F32)
        dg1_ref[...] += jnp.sum(d1 * xn, axis=0, keepdims=True)
        dg2_ref[...] += jnp.sum(d2 * xn, axis=0, keepdims=True)
        dyg = d1 * g1_ref[...] + d2 * g2_ref[...]
        cm = jnp.mean(dyg * xn, axis=-1, keepdims=True)
        dx = dres_ref[...] + r * (dyg - xn * cm)
        dx_ref[...] = dx
        dxb_ref[...] = dx.astype(BF)

    row = pl.BlockSpec((tr, d), lambda i: (i, 0))
    vec = pl.BlockSpec((1, d), lambda i: (0, 0))
    return pl.pallas_call(
        body, name=name, grid=(t // tr,), in_specs=[row, vec, vec, row, row, row], out_specs=[row, row, vec, vec],
        out_shape=[jax.ShapeDtypeStruct((t, d), F32), jax.ShapeDtypeStruct((t, d), BF), jax.ShapeDtypeStruct((1, d), F32),
                   jax.ShapeDtypeStruct((1, d), F32)],
        compiler_params=_params("arbitrary"),
    )(x, g1, g2, dn1, dn2, dres)


def final_loss(x, g, target, *, name):
    t, d = x.shape
    tr = _pick(t, (512, 256))

    def body(x_ref, g_ref, t_ref, dx_ref, dxb_ref, dg_ref, lv_ref):
        @pl.when(pl.program_id(0) == 0)
        def _():
            dg_ref[...] = jnp.zeros_like(dg_ref)
            lv_ref[...] = jnp.zeros_like(lv_ref)

        xv = x_ref[...]
        r = lax.rsqrt(jnp.mean(xv * xv, axis=-1, keepdims=True) + EPS)
        xn = xv * r
        err = xn * g_ref[...] - t_ref[...]
        lv_ref[...] += jnp.sum(err * err, axis=0, keepdims=True)
        dy = err * (1.0 / d)
        dg_ref[...] += jnp.sum(dy * xn, axis=0, keepdims=True)
        dyg = dy * g_ref[...]
        cm = jnp.mean(dyg * xn, axis=-1, keepdims=True)
        dx = r * (dyg - xn * cm)
        dx_ref[...] = dx
        dxb_ref[...] = dx.astype(BF)

    row = pl.BlockSpec((tr, d), lambda i: (i, 0))
    vec = pl.BlockSpec((1, d), lambda i: (0, 0))
    return pl.pallas_call(
        body, name=name, grid=(t // tr,), in_specs=[row, vec, row], out_specs=[row, row, vec, vec],
        out_shape=[jax.ShapeDtypeStruct((t, d), F32), jax.ShapeDtypeStruct((t, d), BF), jax.ShapeDtypeStruct((1, d), F32),
                   jax.ShapeDtypeStruct((1, d), F32)],
        compiler_params=_params("arbitrary"),
    )(x, g, target)


CONV_ROWS = 256
HALO = 16


def _conv_taps(ext, w, rows):
    s0 = ext[HALO:HALO + rows]
    s1 = pltpu.roll(ext, 1, 0)[HALO:HALO + rows]
    s2 = pltpu.roll(ext, 2, 0)[HALO:HALO + rows]
    return (w[0:1] * s2 + w[1:2] * s1) + w[2:3] * s0, s0, s1, s2


def conv_silu_fwd(u, w, bl, *, name, ride=None):
    t, f2 = u.shape
    f = f2 // 2
    s = t // bl
    tc = _pick(f, (256, 128))
    nf = f // tc
    nch = s // CONV_ROWS

    def body(*refs):
        (ua_ref, ug_ref, wa_ref, wg_ref), (h_ref, ca_ref, cg_ref), _, ride_start, ride_wait = _ride_hooks(
            ride, refs, 4, 3, 0, (nf, bl))
        ride_start()
        wa, wg = wa_ref[...], wg_ref[...]

        def chunk(ci, carry):
            r0 = pl.multiple_of(ci * CONV_ROWS, CONV_ROWS)
            ps = pl.multiple_of(jnp.maximum(r0 - HALO, 0), HALO)
            keep = (ci > 0).astype(F32)
            rows = pl.ds(r0, CONV_ROWS)

            def conv(ref, wv):
                ext = jnp.concatenate([ref[0, pl.ds(ps, HALO), :].astype(F32) * keep, ref[0, rows, :].astype(F32)], axis=0)
                return _conv_taps(ext, wv, CONV_ROWS)[0]

            ca, cg = conv(ua_ref, wa), conv(ug_ref, wg)
            sg = pl.reciprocal(1.0 + jnp.exp(-cg), approx=True)
            h_ref[0, rows, :] = ((cg * sg) * ca).astype(h_ref.dtype)
            ca_ref[0, rows, :] = ca.astype(ca_ref.dtype)
            cg_ref[0, rows, :] = cg.astype(cg_ref.dtype)
            return carry

        lax.fori_loop(0, nch, chunk, 0)
        ride_wait()

    u3 = u.reshape(bl, s, f2)
    blk = lambda off: pl.BlockSpec((1, s, tc), lambda j, b: (b, 0, j + off))
    wblk = lambda off: pl.BlockSpec((3, tc), lambda j, b: (0, j + off))
    extra = ride.arrays if ride else []
    h, ca, cg, *delivered = pl.pallas_call(
        body, name=name, grid=(nf, bl), in_specs=[blk(0), blk(nf), wblk(0), wblk(nf)] + [ANY] * len(extra),
        out_specs=[blk(0)] * 3 + [ANY] * (len(ride.out_shape) if ride else 0),
        out_shape=[jax.ShapeDtypeStruct((bl, s, f), BF)] * 3 + (ride.out_shape if ride else []),
        scratch_shapes=ride.scratch if ride else [],
        compiler_params=_params(*(("arbitrary",) * 2 if ride else ("parallel",) * 2)),
    )(u3, u3, w, w, *extra)
    return h.reshape(t, f), ca.reshape(t, f), cg.reshape(t, f), delivered


def conv_silu_bwd(u, ca, cg, w, dh, bl, *, name):
    t, f2 = u.shape
    f = f2 // 2
    s = t // bl
    tc = _pick(f, (256, 128))
    nf = f // tc
    nch = s // CONV_ROWS
    ext_rows = CONV_ROWS + HALO

    def body(ua_ref, ug_ref, ca_ref, cg_ref, wa_ref, wg_ref, dh_ref, dua_ref, dug_ref, dwa_ref, dwg_ref):
        wa, wg = wa_ref[...], wg_ref[...]

        @pl.when(pl.program_id(1) == 0)
        def _():
            dwa_ref[...] = jnp.zeros_like(dwa_ref)
            dwg_ref[...] = jnp.zeros_like(dwg_ref)

        def chunk(ci, carry):
            r0 = pl.multiple_of(ci * CONV_ROWS, CONV_ROWS)
            ns = pl.multiple_of(jnp.minimum(r0 + CONV_ROWS, s - HALO), HALO)
            keep_n = (ci < nch - 1).astype(F32)
            rows = pl.ds(r0, CONV_ROWS)

            def ext_of(ref):
                return jnp.concatenate([ref[0, rows, :].astype(F32), ref[0, pl.ds(ns, HALO), :].astype(F32)], axis=0)

            dhe = jnp.concatenate([dh_ref[0, rows, :].astype(F32), dh_ref[0, pl.ds(ns, HALO), :].astype(F32) * keep_n],
                                  axis=0)
            cae, cge = ext_of(ca_ref), ext_of(cg_ref)
            sg = pl.reciprocal(1.0 + jnp.exp(-cge), approx=True)
            dca = dhe * (cge * sg)
            dcg = dhe * cae * (sg * (1.0 + cge * (1.0 - sg)))

            def back(dc, wv, u_ref, du_ref, dw_ref):
                d0 = dc[:CONV_ROWS]
                n1 = pltpu.roll(dc, ext_rows - 1, 0)[:CONV_ROWS]
                n2 = pltpu.roll(dc, ext_rows - 2, 0)[:CONV_ROWS]
                du_ref[0, rows, :] = ((wv[2:3] * d0 + wv[1:2] * n1) + wv[0:1] * n2).astype(du_ref.dtype)
                uc = u_ref[0, rows, :].astype(F32)
                for k, shifted in enumerate((n2, n1, d0)):
                    dw_ref[k:k + 1, :] += jnp.sum(shifted * uc, axis=0, keepdims=True)

            back(dca, wa, ua_ref, dua_ref, dwa_ref)
            back(dcg, wg, ug_ref, dug_ref, dwg_ref)
            return carry

        lax.fori_loop(0, nch, chunk, 0)

    u3 = u.reshape(bl, s, f2)
    blk = lambda off: pl.BlockSpec((1, s, tc), lambda j, b: (b, 0, j + off))
    wblk = lambda off: pl.BlockSpec((3, tc), lambda j, b: (0, j + off))
    dua, dug, dwa, dwg = pl.pallas_call(
        body, name=name, grid=(nf, bl), in_specs=[blk(0), blk(nf), blk(0), blk(0), wblk(0), wblk(nf), blk(0)],
        out_specs=[blk(0), blk(0), wblk(0), wblk(0)],
        out_shape=[jax.ShapeDtypeStruct((bl, s, f), BF), jax.ShapeDtypeStruct((bl, s, f), BF),
                   jax.ShapeDtypeStruct((3, f), F32), jax.ShapeDtypeStruct((3, f), F32)],
        compiler_params=_params("parallel", "arbitrary"),
    )(u3, u3, ca.reshape(bl, s, f), cg.reshape(bl, s, f), w, w, dh.reshape(bl, s, f))
    return dua.reshape(t, f), dug.reshape(t, f), jnp.concatenate([dwa, dwg], axis=1)


def _lane_masks(rows):
    lane = lax.broadcasted_iota(jnp.int32, (rows, LANES), 1)
    return lane < HEAD_DIM, lane >= HEAD_DIM


def _nt(a, b):
    return lax.dot_general(a, b, (((1,), (1,)), ((), ())), preferred_element_type=F32)


def _tn(a, b):
    return lax.dot_general(a, b, (((0,), (0,)), ((), ())), preferred_element_type=F32)


def _nn(a, b):
    return jnp.dot(a, b, preferred_element_type=F32)


def _suffix_ones():
    j = lax.broadcasted_iota(jnp.int32, (2 * QB, QB), 0) % QB
    s = lax.broadcasted_iota(jnp.int32, (2 * QB, QB), 1)
    return (j >= s).astype(BF)


def _softplus(z):
    return jnp.maximum(z, 0.0) + jnp.log(1.0 + jnp.exp(-jnp.abs(z)))


SB_BLOCK = 256
SB_DEAD = -100.0


SB_STRIP = 32
SB_SUB = SB_BLOCK // QB


def _sb_scratch(backward):
    blk = (2, SB_BLOCK, SB_BLOCK)
    per_head = pltpu.VMEM((2, SB_BLOCK, LANES), F32)
    scr = [pltpu.VMEM((2 * QB, QB), BF),
           pltpu.VMEM(blk, F32),
           pltpu.VMEM((2, SB_SUB) + blk[1:], BF),
           pltpu.VMEM(blk, F32),
           pltpu.VMEM(blk, BF),
           per_head]
    if not backward:
        return scr + [per_head]
    return scr + [pltpu.VMEM(blk, F32),
                  pltpu.VMEM(blk, F32),
                  pltpu.VMEM(blk, F32),
                  pltpu.VMEM(blk, BF),
                  per_head,
                  per_head,
                  per_head]


def _sb_strips(fn):
    for r in range(SB_BLOCK // SB_STRIP):
        fn(pl.ds(r * SB_STRIP, SB_STRIP), r)


def _sb_keep(diag, r, v):
    if not diag:
        return v
    rel = (lax.broadcasted_iota(jnp.int32, (SB_STRIP, SB_BLOCK), 1)
           - lax.broadcasted_iota(jnp.int32, (SB_STRIP, SB_BLOCK), 0))
    return jnp.where(rel < r * SB_STRIP, v, 0.0)


def _sb_split_store(sp_scr, rows, v):
    hi = v.astype(BF)
    lo = (v - hi.astype(F32)).astype(BF)
    for u in range(SB_SUB):
        sp_scr[u, rows, 0:QB] = hi[:, u * QB:(u + 1) * QB]
        sp_scr[u, rows, QB:2 * QB] = lo[:, u * QB:(u + 1) * QB]


def _sb_suffix_sums(uu_scr, sp_scr, out_scr):
    for u in range(SB_SUB):
        out_scr[:, u * QB:(u + 1) * QB] = _nn(sp_scr[u], uu_scr[...])


def _sb_fold(sums, off):
    offs = [None] * SB_SUB
    for u in reversed(range(SB_SUB)):
        offs[u] = off
        off = off + jnp.broadcast_to(sums[:, u * QB:u * QB + 1], (SB_STRIP, LANES))
    return offs, off


def _sb_first(diag, z_scr, sp_scr, ls_scr=None):
    def strip(rows, r):
        ls = _sb_keep(diag, r, -_softplus(z_scr[rows, :]))
        if ls_scr is not None:
            ls_scr[rows, :] = ls
        _sb_split_store(sp_scr, rows, ls)

    _sb_strips(strip)


def _sb_second(diag, z_scr, cin_scr, w_scr, carry_scr):
    def strip(rows, r):
        z, cin = z_scr[rows, :], cin_scr[rows, :]
        offs, carry_scr[rows, :] = _sb_fold(cin, carry_scr[rows, :])
        e = jnp.concatenate([z[:, u * QB:(u + 1) * QB] + cin[:, u * QB:(u + 1) * QB] + offs[u] for u in range(SB_SUB)],
                            axis=1)
        w_scr[rows, :] = _sb_keep(diag, r, jnp.exp(e)).astype(BF)

    _sb_strips(strip)


def _sb_third(w_scr, dw_scr, da_scr, sp_scr):
    def strip(rows, r):
        da = w_scr[rows, :].astype(F32) * dw_scr[rows, :]
        da_scr[rows, :] = da
        _sb_split_store(sp_scr, rows, da)

    _sb_strips(strip)


def _sb_fourth(diag, z_scr, ls_scr, da_scr, sin_scr, dz_scr, carry_s_scr, dsum_scr):
    def strip(rows, r):
        da, sin = da_scr[rows, :], sin_scr[rows, :]
        offs, carry_s_scr[rows, :] = _sb_fold(sin, carry_s_scr[rows, :])
        dsum = dsum_scr[rows, :]
        pre = jnp.concatenate([dsum - (sin[:, u * QB:(u + 1) * QB] - da[:, u * QB:(u + 1) * QB] + offs[u])
                               for u in range(SB_SUB)], axis=1)
        sig = jnp.exp(z_scr[rows, :] + ls_scr[rows, :])
        dz_scr[rows, :] = _sb_keep(diag, r, da - sig * pre).astype(BF)

    _sb_strips(strip)


def _sb_alive(carry_scr):
    return (jnp.max(carry_scr[...]) > SB_DEAD).astype(jnp.int32)


def _ride_hooks(ride, refs, n_in, n_out, n_scratch, grid):
    if ride is None:
        return refs[:n_in], refs[n_in:n_in + n_out], refs[n_in + n_out:], lambda: None, lambda: None
    ni, no = len(ride.arrays), len(ride.out_shape)
    own_in, rin = refs[:n_in], refs[n_in:n_in + ni]
    own_out, rout = refs[n_in + ni:n_in + ni + n_out], refs[n_in + ni + n_out:n_in + ni + n_out + no]
    rest = refs[n_in + ni + n_out + no:]
    own_scr, sems = rest[:n_scratch], rest[n_scratch:]
    ids = [pl.program_id(a) for a in range(len(grid))]

    def start():
        first = functools.reduce(lambda u, v: u & v, [i == 0 for i in ids])
        pl.when(first)(lambda: ride.start(rin, rout, sems))

    def wait():
        last = functools.reduce(lambda u, v: u & v, [i == n - 1 for i, n in zip(ids, grid)])
        pl.when(last)(lambda: ride.wait(rin, rout, sems))

    return own_in, own_out, own_scr, start, wait


def sb_fwd(proj, bl, *, name, ride=None):
    t, width = proj.shape
    s = t // bl
    npair = SB_WIDTH // LANES
    nq = s // SB_BLOCK
    grid = (bl, npair, nq)

    own_scratch = _sb_scratch(backward=False)

    def body(*refs):
        (q_ref, k_ref, v_ref), (o_ref, of_ref), scr, ride_start, ride_wait = _ride_hooks(
            ride, refs, 3, 2, len(own_scratch), grid)
        uu_scr, z_scr, sp_scr, cin_scr, w_scr, carry_scr, acc_scr = scr
        ride_start()
        i = pl.program_id(2)
        uu_scr[...] = _suffix_ones()
        carry_scr[...] = jnp.zeros_like(carry_scr)
        acc_scr[...] = jnp.zeros_like(acc_scr)
        qs = (q_ref[0].astype(F32) * SCALE).astype(BF)
        masks = _lane_masks(SB_BLOCK)

        z, sp, cin, w, car, acc = ([r.at[h] for h in range(2)] for r in (z_scr, sp_scr, cin_scr, w_scr, carry_scr, acc_scr))

        def live(c, diag):
            start = pl.multiple_of(c * SB_BLOCK, SB_BLOCK)
            kb, vb = k_ref[0, pl.ds(start, SB_BLOCK), :], v_ref[0, pl.ds(start, SB_BLOCK), :]
            for h in range(2):
                z[h][...] = _nt(jnp.where(masks[h], qs, jnp.zeros_like(qs)), kb)
            _sb_first(diag, z[0], sp[0])
            _sb_suffix_sums(uu_scr, sp[0], cin[0])
            _sb_first(diag, z[1], sp[1])
            _sb_suffix_sums(uu_scr, sp[1], cin[1])
            _sb_second(diag, z[0], cin[0], w[0], car[0])
            acc[0][...] += _nn(w[0][...], vb)
            _sb_second(diag, z[1], cin[1], w[1], car[1])
            acc[1][...] += _nn(w[1][...], vb)
            return _sb_alive(carry_scr)

        def step(n, alive):
            return lax.cond(alive > 0, lambda: live(i - n, False), lambda: alive)

        lax.fori_loop(1, i + 1, step, live(i, True))
        both = jnp.where(masks[0], acc_scr[0], acc_scr[1])
        o_ref[0] = both.astype(o_ref.dtype)
        of_ref[0] = both
        ride_wait()

    p3 = proj.reshape(bl, s, width)
    qblk = pl.BlockSpec((1, SB_BLOCK, LANES), lambda b, p, i: (b, i, p))
    extra = ride.arrays if ride else []
    o, of, *delivered = pl.pallas_call(
        body, name=name, grid=grid,
        in_specs=[qblk, pl.BlockSpec((1, s, LANES), lambda b, p, i: (b, 0, npair + p)),
                  pl.BlockSpec((1, s, LANES), lambda b, p, i: (b, 0, 2 * npair + p))] + [ANY] * len(extra),
        out_specs=[qblk, qblk] + [ANY] * (len(ride.out_shape) if ride else 0),
        out_shape=[jax.ShapeDtypeStruct((bl, s, SB_WIDTH), BF), jax.ShapeDtypeStruct((bl, s, SB_WIDTH), F32)]
        + (ride.out_shape if ride else []),
        scratch_shapes=own_scratch + (ride.scratch if ride else []),
        compiler_params=_params("arbitrary", "arbitrary", "arbitrary"),
    )(p3, p3, p3, *extra)
    return o.reshape(t, SB_WIDTH), of.reshape(t, SB_WIDTH), delivered


def sb_bwd(proj, o, dcat, bl, *, name, ride=None):
    t, width = proj.shape
    s = t // bl
    npair = SB_WIDTH // LANES
    nq = s // SB_BLOCK
    grid = (bl, npair, nq)
    own_scratch = _sb_scratch(backward=True)

    def body(*refs):
        (q_ref, k_ref, v_ref, o_ref, do_ref), (dq_ref, dk_ref, dv_ref), scr, ride_start, ride_wait = _ride_hooks(
            ride, refs, 5, 3, len(own_scratch), grid)
        (uu_scr, z_scr, sp_scr, cin_scr, w_scr, carry_scr,
         ls_scr, dw_scr, da_scr, dz_scr, carry_s_scr, dsum_scr, dq_scr) = scr
        ride_start()
        i = pl.program_id(2)

        @pl.when(i == 0)
        def _():
            dk_ref[...] = jnp.zeros_like(dk_ref)
            dv_ref[...] = jnp.zeros_like(dv_ref)

        uu_scr[...] = _suffix_ones()
        for ref in (carry_scr, carry_s_scr, dq_scr):
            ref[...] = jnp.zeros_like(ref)
        masks = _lane_masks(SB_BLOCK)
        qs = (q_ref[0].astype(F32) * SCALE).astype(BF)
        do = do_ref[0]
        prod = do.astype(F32) * o_ref[0]
        for h in range(2):
            dsum_scr[h] = jnp.broadcast_to(jnp.sum(jnp.where(masks[h], prod, 0.0), axis=1, keepdims=True),
                                           (SB_BLOCK, LANES))

        z, sp, cin, w, car, ls, dw, da, dz, cars, dsum, dq = (
            [r.at[h] for h in range(2)] for r in (z_scr, sp_scr, cin_scr, w_scr, carry_scr, ls_scr, dw_scr, da_scr,
                                                  dz_scr, carry_s_scr, dsum_scr, dq_scr))

        def live(c, diag):
            start = pl.multiple_of(c * SB_BLOCK, SB_BLOCK)
            keys = pl.ds(start, SB_BLOCK)
            kb, vb = k_ref[0, keys, :], v_ref[0, keys, :]
            qh = [jnp.where(masks[h], qs, jnp.zeros_like(qs)) for h in range(2)]
            doh = [jnp.where(masks[h], do, jnp.zeros_like(do)) for h in range(2)]
            for h in range(2):
                z[h][...] = _nt(qh[h], kb)
                dw[h][...] = _nt(doh[h], vb)

            def weights(h):
                _sb_second(diag, z[h], cin[h], w[h], car[h])
                _sb_third(w[h], dw[h], da[h], sp[h])
                _sb_suffix_sums(uu_scr, sp[h], cin[h])

            def grads(h):
                _sb_fourth(diag, z[h], ls[h], da[h], cin[h], dz[h], cars[h], dsum[h])
                dk_ref[0, keys, :] += _tn(dz[h][...], qh[h])
                dv_ref[0, keys, :] += _tn(w[h][...], doh[h])
                dq[h][...] += _nn(dz[h][...], kb)

            _sb_first(diag, z[0], sp[0], ls[0])
            _sb_suffix_sums(uu_scr, sp[0], cin[0])
            _sb_first(diag, z[1], sp[1], ls[1])
            _sb_suffix_sums(uu_scr, sp[1], cin[1])
            weights(0)
            weights(1)
            grads(0)
            grads(1)
            return _sb_alive(carry_scr)

        def step(n, alive):
            return lax.cond(alive > 0, lambda: live(i - n, False), lambda: alive)

        lax.fori_loop(1, i + 1, step, live(i, True))
        dq_ref[0] = (jnp.where(masks[0], dq_scr[0], dq_scr[1]) * SCALE).astype(dq_ref.dtype)
        ride_wait()

    p3 = proj.reshape(bl, s, width)
    o3 = o.reshape(bl, s, SB_WIDTH)
    d3 = dcat.reshape(bl, s, dcat.shape[1])
    qblk = pl.BlockSpec((1, SB_BLOCK, LANES), lambda b, p, i: (b, i, p))
    full = pl.BlockSpec((1, s, LANES), lambda b, p, i: (b, 0, p))
    extra = ride.arrays if ride else []
    dq, dk, dv, *delivered = pl.pallas_call(
        body, name=name, grid=grid,
        in_specs=[qblk, pl.BlockSpec((1, s, LANES), lambda b, p, i: (b, 0, npair + p)),
                  pl.BlockSpec((1, s, LANES), lambda b, p, i: (b, 0, 2 * npair + p)), qblk, qblk] + [ANY] * len(extra),
        out_specs=[qblk, full, full] + [ANY] * (len(ride.out_shape) if ride else 0),
        out_shape=[jax.ShapeDtypeStruct((bl, s, SB_WIDTH), BF), jax.ShapeDtypeStruct((bl, s, SB_WIDTH), F32),
                   jax.ShapeDtypeStruct((bl, s, SB_WIDTH), F32)] + (ride.out_shape if ride else []),
        scratch_shapes=own_scratch + (ride.scratch if ride else []),
        compiler_params=_params("arbitrary", "arbitrary", "arbitrary"),
    )(p3, p3, p3, o3, d3, *extra)
    return dq.reshape(t, SB_WIDTH), dk.reshape(t, SB_WIDTH), dv.reshape(t, SB_WIDTH), delivered


MEM_TQ = 512


def mem_fwd(proj, kvm, bl, *, name):
    t, width = proj.shape
    s = t // bl
    qoff = (width - MEM_WIDTH) // LANES
    npair = MEM_WIDTH // LANES

    def body(q_ref, k_ref, v_ref, o_ref):
        ma, mb = _lane_masks(MEM_TQ)
        q = q_ref[...].astype(BF)
        outs = []
        for m in (ma, mb):
            qh = jnp.where(m, q, jnp.zeros_like(q))
            sc = _nt(qh, k_ref[...]) * SCALE
            p = jnp.exp(sc - jnp.max(sc, axis=-1, keepdims=True))
            p = p * (1.0 / jnp.sum(p, axis=-1, keepdims=True))
            outs.append(_nn(p.astype(BF), v_ref[...]))
        o_ref[...] = jnp.where(ma, outs[0], outs[1]).astype(o_ref.dtype)

    nt = s // MEM_TQ
    return pl.pallas_call(
        body, name=name, grid=(bl, npair, nt),
        in_specs=[pl.BlockSpec((MEM_TQ, LANES), lambda b, p, i: (b * nt + i, qoff + p)),
                  pl.BlockSpec((MEM_LEN, LANES), lambda b, p, i: (b, p)),
                  pl.BlockSpec((MEM_LEN, LANES), lambda b, p, i: (b, npair + p))],
        out_specs=pl.BlockSpec((MEM_TQ, LANES), lambda b, p, i: (b * nt + i, p)),
        out_shape=jax.ShapeDtypeStruct((t, MEM_WIDTH), BF),
        compiler_params=_params("parallel", "parallel", "parallel"),
    )(proj, kvm, kvm)


def mem_bwd(proj, kvm, dcat, bl, *, name):
    t, width = proj.shape
    s = t // bl
    qoff = (width - MEM_WIDTH) // LANES
    doff = (dcat.shape[1] - MEM_WIDTH) // LANES
    npair = MEM_WIDTH // LANES
    nt = s // MEM_TQ

    def body(q_ref, k_ref, v_ref, do_ref, dq_ref, dk_ref, dv_ref):
        @pl.when(pl.program_id(2) == 0)
        def _():
            dk_ref[...] = jnp.zeros_like(dk_ref)
            dv_ref[...] = jnp.zeros_like(dv_ref)

        ma, mb = _lane_masks(MEM_TQ)
        q = q_ref[...].astype(BF)
        do = do_ref[...]
        kb, vb = k_ref[...], v_ref[...]
        dqs = []
        for m in (ma, mb):
            qh = jnp.where(m, q, jnp.zeros_like(q))
            doh = jnp.where(m, do, jnp.zeros_like(do))
            sc = _nt(qh, kb) * SCALE
            p = jnp.exp(sc - jnp.max(sc, axis=-1, keepdims=True))
            p = p * (1.0 / jnp.sum(p, axis=-1, keepdims=True))
            dp = _nt(doh, vb)
            ds = p * (dp - jnp.sum(p * dp, axis=-1, keepdims=True))
            dss = (ds * SCALE).astype(BF)
            dk_ref[...] += _tn(dss, qh)
            dv_ref[...] += _tn(p.astype(BF), doh)
            dqs.append(_nn(dss, kb))
        dq_ref[...] = jnp.where(ma, dqs[0], dqs[1]).astype(dq_ref.dtype)

    kblk = pl.BlockSpec((MEM_LEN, LANES), lambda b, p, i: (b, p))
    dq, dk, dv = pl.pallas_call(
        body, name=name, grid=(bl, npair, nt),
        in_specs=[pl.BlockSpec((MEM_TQ, LANES), lambda b, p, i: (b * nt + i, qoff + p)), kblk,
                  pl.BlockSpec((MEM_LEN, LANES), lambda b, p, i: (b, npair + p)),
                  pl.BlockSpec((MEM_TQ, LANES), lambda b, p, i: (b * nt + i, doff + p))],
        out_specs=[pl.BlockSpec((MEM_TQ, LANES), lambda b, p, i: (b * nt + i, p)), kblk, kblk],
        out_shape=[jax.ShapeDtypeStruct((t, MEM_WIDTH), BF), jax.ShapeDtypeStruct((bl * MEM_LEN, MEM_WIDTH), F32),
                   jax.ShapeDtypeStruct((bl * MEM_LEN, MEM_WIDTH), F32)],
        compiler_params=_params("parallel", "parallel", "arbitrary"),
    )(proj, kvm, kvm, dcat)
    return dq, jnp.concatenate([dk, dv], axis=1).astype(BF)


def _dil_rows(r, u0, size, dil):
    if dil == 1:
        return pl.ds(_mo(u0, QB), size)
    return pl.ds(u0 * dil + r, size, stride=dil)


def _dil_walk(block, dil, nb):
    def residue(r, c):
        block(r, 0, QB)
        if nb > 1:
            def step(n, c2):
                block(r, n, 2 * QB)
                return c2
            lax.fori_loop(1, nb, step, 0, unroll=3)
        return c

    if dil == 1:
        residue(0, 0)
    else:
        lax.fori_loop(0, dil, residue, 0, unroll=4 if nb == 1 else 1)


def _dil_scores(qh, kb, slope_d, n_keys):
    i = lax.broadcasted_iota(jnp.int32, (QB, n_keys), 0)
    j = lax.broadcasted_iota(jnp.int32, (QB, n_keys), 1)
    delta = i + (n_keys - QB) - j
    valid = (delta >= 0) & (delta <= QB)
    sc = _nt(qh, kb) * SCALE - slope_d * delta.astype(F32)
    return jnp.where(valid, sc, NEG)


def _dil_slopes(g, dil):
    p = pl.program_id(1)
    sa = jnp.where(p == 0, ALIBI[4 * g] * dil, ALIBI[4 * g + 2] * dil).astype(F32)
    sb = jnp.where(p == 0, ALIBI[4 * g + 1] * dil, ALIBI[4 * g + 3] * dil).astype(F32)
    return sa, sb


def dil_fwd(projb, kv, g, bl, *, name):
    _, dil = DIL_GROUPS[g]
    t, wq = projb.shape
    wk = kv.shape[1]
    s = t // bl
    ln = s // dil
    nb = ln // QB
    gw = 4 * HEAD_DIM
    ck, co = wk // LANES, gw // LANES

    def body(q_ref, k_ref, v_ref, o_ref, lse_ref):
        sa, sb = _dil_slopes(g, dil)
        ma, mb = _lane_masks(QB)

        def block(r, n, n_keys):
            q0 = n * QB
            k0 = q0 - (n_keys - QB)
            q = q_ref[0, _dil_rows(r, q0, QB, dil), :].astype(BF)
            kb = k_ref[0, _dil_rows(r, k0, n_keys, dil), :].astype(BF)
            vb = v_ref[0, _dil_rows(r, k0, n_keys, dil), :].astype(BF)
            outs, lses = [], []
            for m, sl in ((ma, sa), (mb, sb)):
                qh = jnp.where(m, q, jnp.zeros_like(q))
                sc = _dil_scores(qh, kb, sl, n_keys)
                mx = jnp.max(sc, axis=-1, keepdims=True)
                p = jnp.exp(sc - mx)
                den = jnp.sum(p, axis=-1, keepdims=True)
                outs.append(_nn(p.astype(BF), vb) * (1.0 / den))
                lses.append(mx + jnp.log(den))
            o_ref[0, _dil_rows(r, q0, QB, dil), :] = jnp.where(ma, outs[0], outs[1])
            lse_ref[0, _dil_rows(r, q0, QB, dil), :] = jnp.where(ma, lses[0], lses[1])

        _dil_walk(block, dil, nb)

    colblk = lambda off: pl.BlockSpec((1, s, LANES), lambda b, p: (b, 0, off + p))
    o, lse = pl.pallas_call(
        body, name=name, grid=(bl, co),
        in_specs=[colblk(g * co), colblk(g * co), colblk(ck // 2 + g * co)],
        out_specs=[colblk(0), colblk(0)],
        out_shape=[jax.ShapeDtypeStruct((bl, s, gw), F32), jax.ShapeDtypeStruct((bl, s, gw), F32)],
        compiler_params=_params("parallel", "parallel"),
    )(projb.reshape(bl, s, wq), kv.reshape(bl, s, wk), kv.reshape(bl, s, wk))
    return o.reshape(t, gw), lse.reshape(t, gw)


def dil_bwd(projb, kv, lse, dog, dshift, g, bl, *, name):
    _, dil = DIL_GROUPS[g]
    t, wq = projb.shape
    wk = kv.shape[1]
    s = t // bl
    ln = s // dil
    nb = ln // QB
    gw = 4 * HEAD_DIM
    ck, co = wk // LANES, gw // LANES

    def body(q_ref, k_ref, v_ref, lse_ref, do_ref, sh_ref, dq_ref, dk_ref, dv_ref):
        sa, sb = _dil_slopes(g, dil)
        ma, mb = _lane_masks(QB)
        dk_ref[...] = jnp.zeros_like(dk_ref)
        dv_ref[...] = jnp.zeros_like(dv_ref)

        def block(r, n, n_keys):
            q0 = n * QB
            k0 = q0 - (n_keys - QB)
            qrows, krows = _dil_rows(r, q0, QB, dil), _dil_rows(r, k0, n_keys, dil)
            q = q_ref[0, qrows, :].astype(BF)
            do = do_ref[0, qrows, :].astype(BF)
            lse_b = lse_ref[0, qrows, :]
            sh_b = sh_ref[0, qrows, :]
            kb = k_ref[0, krows, :].astype(BF)
            vb = v_ref[0, krows, :].astype(BF)
            dqs = []
            for m, sl, c0 in ((ma, sa, 0), (mb, sb, HEAD_DIM)):
                qh = jnp.where(m, q, jnp.zeros_like(q))
                doh = jnp.where(m, do, jnp.zeros_like(do))
                sc = _dil_scores(qh, kb, sl, n_keys)
                p = jnp.exp(sc - lse_b[:, c0:c0 + 1])
                ds = p * (_nt(doh, vb) - sh_b[:, c0:c0 + 1])
                dss = (ds * SCALE).astype(BF)
                dk_ref[0, krows, :] += _tn(dss, qh)
                dv_ref[0, krows, :] += _tn(p.astype(BF), doh)
                dqs.append(_nn(dss, kb))
            dq_ref[0, qrows, :] = jnp.where(ma, dqs[0], dqs[1])

        _dil_walk(block, dil, nb)

    colblk = lambda off: pl.BlockSpec((1, s, LANES), lambda b, p: (b, 0, off + p))
    dq, dk, dv = pl.pallas_call(
        body, name=name, grid=(bl, co),
        in_specs=[colblk(g * co), colblk(g * co), colblk(ck // 2 + g * co), colblk(0), colblk(g * co), colblk(g * co)],
        out_specs=[colblk(0)] * 3,
        out_shape=[jax.ShapeDtypeStruct((bl, s, gw), F32)] * 3,
        compiler_params=_params("parallel", "parallel"),
    )(projb.reshape(bl, s, wq), kv.reshape(bl, s, wk), kv.reshape(bl, s, wk), lse.reshape(bl, s, gw),
      dog.reshape(bl, s, DIL_WIDTH), dshift.reshape(bl, s, DIL_WIDTH))
    return dq.reshape(t, gw), dk.reshape(t, gw), dv.reshape(t, gw)


def _group_weights(lses):
    mx = jnp.maximum(jnp.maximum(lses[0], lses[1]), lses[2])
    es = [jnp.exp(l - mx) for l in lses]
    inv = 1.0 / (es[0] + es[1] + es[2])
    return [e * inv for e in es]


def dil_combine_fwd(os_, lses, *, name):
    t, gw = os_[0].shape
    tr = _pick(t, (512, 256))

    def body(o0, o1, o2, l0, l1, l2, out_ref):
        al = _group_weights([l0[...], l1[...], l2[...]])
        for g, o_ref in enumerate((o0, o1, o2)):
            out_ref[:, g * gw:(g + 1) * gw] = (o_ref[...] * al[g]).astype(out_ref.dtype)

    blk = pl.BlockSpec((tr, gw), lambda i: (i, 0))
    return pl.pallas_call(
        body, name=name, grid=(t // tr,), in_specs=[blk] * 6,
        out_specs=pl.BlockSpec((tr, 3 * gw), lambda i: (i, 0)), out_shape=jax.ShapeDtypeStruct((t, 3 * gw), BF),
        compiler_params=_params("parallel"),
    )(*os_, *lses)


def dil_combine_bwd(os_, lses, dcat, *, name):
    t, gw = os_[0].shape
    tr = _pick(t, (512, 256))

    def head_sum(v):
        parts = []
        for c in range(gw // LANES):
            blk = v[:, c * LANES:(c + 1) * LANES]
            ma, _ = _lane_masks(tr)
            sa = jnp.sum(jnp.where(ma, blk, 0.0), axis=1, keepdims=True)
            sb = jnp.sum(blk, axis=1, keepdims=True) - sa
            parts.append(jnp.where(ma, sa, sb))
        return jnp.concatenate(parts, axis=1)

    def body(o0, o1, o2, l0, l1, l2, d_ref, dog_ref, sh_ref):
        al = _group_weights([l0[...], l1[...], l2[...]])
        dos = [d_ref[:, g * gw:(g + 1) * gw].astype(F32) for g in range(3)]
        dal = [head_sum(dos[g] * o_ref[...]) for g, o_ref in enumerate((o0, o1, o2))]
        mix = al[0] * dal[0] + al[1] * dal[1] + al[2] * dal[2]
        for g in range(3):
            dog_ref[:, g * gw:(g + 1) * gw] = (al[g] * dos[g]).astype(dog_ref.dtype)
            sh_ref[:, g * gw:(g + 1) * gw] = al[g] * mix

    blk = pl.BlockSpec((tr, gw), lambda i: (i, 0))
    wide = pl.BlockSpec((tr, 3 * gw), lambda i: (i, 0))
    return pl.pallas_call(
        body, name=name, grid=(t // tr,), in_specs=[blk] * 6 + [wide], out_specs=[wide, wide],
        out_shape=[jax.ShapeDtypeStruct((t, 3 * gw), F32), jax.ShapeDtypeStruct((t, 3 * gw), F32)],
        compiler_params=_params("parallel"),
    )(*os_, *lses, dcat)


def adamw(w, g1, g2, m, v, *, name):
    r, c = w.shape
    tr = r
    for cand in (256, 128, 64, 32, 16, 8):
        if r % cand == 0 and cand * c * 4 <= (1 << 20):
            tr = cand
            break
    two = g2 is not None

    def body(*refs):
        if two:
            w_ref, g1_ref, g2_ref, m_ref, v_ref, g_ref, d_ref, nm_ref, nv_ref = refs
            g = g1_ref[...] + g2_ref[...]
        else:
            w_ref, g1_ref, m_ref, v_ref, g_ref, d_ref, nm_ref, nv_ref = refs
            g = g1_ref[...]
        nm = ADAM_B1 * m_ref[...] + (1.0 - ADAM_B1) * g
        nv = ADAM_B2 * v_ref[...] + (1.0 - ADAM_B2) * (g * g)
        m_hat = nm / (1.0 - ADAM_B1 ** ADAM_STEP)
        v_hat = nv / (1.0 - ADAM_B2 ** ADAM_STEP)
        g_ref[...] = g
        d_ref[...] = -ADAM_LR * (m_hat / (jnp.sqrt(v_hat) + ADAM_EPS) + ADAM_WD * w_ref[...])
        nm_ref[...] = nm
        nv_ref[...] = nv

    blk = pl.BlockSpec((tr, c), lambda i: (i, 0))
    args = [w, g1] + ([g2] if two else []) + [m, v]
    return pl.pallas_call(
        body, name=name, grid=(r // tr,), in_specs=[blk] * len(args), out_specs=[blk] * 4,
        out_shape=[jax.ShapeDtypeStruct((r, c), F32)] * 4, compiler_params=_params("parallel"),
    )(*args)


def sum4(own, land, *, name):
    r, c = own.shape
    tr = _pick(r, (256, 128, 64))

    def body(o_ref, l_ref, s_ref):
        s_ref[...] = ((o_ref[...].astype(F32) + l_ref[0].astype(F32)) + l_ref[1].astype(F32)) + l_ref[2].astype(F32)

    return pl.pallas_call(
        body, name=name, grid=(r // tr,),
        in_specs=[pl.BlockSpec((tr, c), lambda i: (i, 0)), pl.BlockSpec((3, tr, c), lambda i: (0, i, 0))],
        out_specs=pl.BlockSpec((tr, c), lambda i: (i, 0)), out_shape=jax.ShapeDtypeStruct((r, c), F32),
        compiler_params=_params("parallel"),
    )(own, land)


ANY = pl.BlockSpec(memory_space=pl.ANY)


def _place():
    x, y, c = lax.axis_index("x"), lax.axis_index("y"), lax.axis_index("c")
    chips = [(1 - x, y), (x, 1 - y), (1 - x, 1 - y)]
    return x, y, c, chips


class Ride:
    def __init__(self, arrays, out_shape, copies):
        self.arrays, self.out_shape, self.copies = list(arrays), list(out_shape), copies
        n = 3 * len(self.arrays)
        self.scratch = [pltpu.SemaphoreType.DMA((n,)), pltpu.SemaphoreType.DMA((n,)),
                        pltpu.SemaphoreType.DMA((len(self.arrays),))]

    def split(self, refs):
        n, m = len(self.arrays), len(self.out_shape)
        return refs[:n], refs[n:n + m], refs[n + m:]

    def start(self, ins, outs, sems):
        local, sends, _ = self.copies(ins, outs, *sems)
        for cp in local + sends:
            cp.start()

    def wait(self, ins, outs, sems):
        local, sends, arrivals = self.copies(ins, outs, *sems)
        for cp in arrivals:
            cp.wait_recv()
        for cp in sends:
            cp.wait_send()
        for cp in local:
            cp.wait()

    def run(self, name):
        def body(*refs):
            ins, outs, sems = self.split(refs)
            self.start(ins, outs, sems)
            self.wait(ins, outs, sems)

        return pl.pallas_call(body, name=name, in_specs=[ANY] * len(self.arrays), out_specs=[ANY] * len(self.out_shape),
                              out_shape=self.out_shape, scratch_shapes=self.scratch)(*self.arrays)


def gather_ride(shards, axes):
    def copies(ins, outs, send_sems, recv_sems, local_sems):
        x, y, c, chips = _place()

        def slot(a, q):
            size = shards[a].shape[axes[a]]
            start = pl.multiple_of(q * size, size)
            return outs[a].at[pl.ds(start, size), :] if axes[a] == 0 else outs[a].at[:, pl.ds(start, size)]

        def remote(a, k, q):
            px, py = chips[k]
            return pltpu.make_async_remote_copy(src_ref=ins[a], dst_ref=slot(a, q), send_sem=send_sems.at[3 * a + k],
                                                recv_sem=recv_sems.at[3 * a + k], device_id=(px, py, c), device_id_type=MESH)

        me = 2 * x + y
        n = len(shards)
        local = [pltpu.make_async_copy(ins[a], slot(a, me), local_sems.at[a]) for a in range(n)]
        sends = [remote(a, k, me) for a in range(n) for k in range(3)]
        arrivals = [remote(a, k, 2 * chips[k][0] + chips[k][1]) for a in range(n) for k in range(3)]
        return local, sends, arrivals

    out_shape = []
    for a, sh in enumerate(shards):
        full = list(sh.shape)
        full[axes[a]] *= N_CHIPS
        out_shape.append(jax.ShapeDtypeStruct(tuple(full), sh.dtype))
    return Ride(shards, out_shape, copies)


def scatter_ride(grads):
    def copies(ins, outs, send_sems, recv_sems, local_sems, base=0):
        x, y, c, chips = _place()
        sends = [pltpu.make_async_remote_copy(src_ref=ins[a].at[2 * px + py], dst_ref=outs[a].at[k],
                                              send_sem=send_sems.at[3 * (base + a) + k],
                                              recv_sem=recv_sems.at[3 * (base + a) + k],
                                              device_id=(px, py, c), device_id_type=MESH)
                 for a in range(len(grads)) for k, (px, py) in enumerate(chips)]
        return [], sends, sends

    return Ride(grads, [jax.ShapeDtypeStruct((3,) + g.shape[1:], g.dtype) for g in grads], copies)


def swap_ride(arrs):
    def copies(ins, outs, send_sems, recv_sems, local_sems, base=0):
        x, y, c, _ = _place()
        sends = [pltpu.make_async_remote_copy(src_ref=ins[a], dst_ref=outs[a], send_sem=send_sems.at[3 * (base + a)],
                                              recv_sem=recv_sems.at[3 * (base + a)], device_id=(x, y, 1 - c),
                                              device_id_type=MESH)
                 for a in range(len(arrs))]
        return [], sends, sends

    return Ride(arrs, [jax.ShapeDtypeStruct(a.shape, a.dtype) for a in arrs], copies)


def merge_rides(first, second):
    n1, m1 = len(first.arrays), len(first.out_shape)

    def copies(ins, outs, send_sems, recv_sems, local_sems, base=0):
        parts = [first.copies(ins[:n1], outs[:m1], send_sems, recv_sems, local_sems, base),
                 second.copies(ins[n1:], outs[m1:], send_sems, recv_sems, local_sems, base + n1)]
        return tuple(parts[0][k] + parts[1][k] for k in range(3))

    return Ride(first.arrays + second.arrays, first.out_shape + second.out_shape, copies)


def all_reduce_small(v, *, name):
    rows = v.shape[0]

    def body(v_ref, o_ref, gath, send_sems, recv_sems):
        x, y, c, _ = _place()
        me = 4 * x + 2 * y + c
        gath[me] = v_ref[...]
        sends = []
        for msk in range(1, N_DEV):
            peer = (x ^ (msk >> 2), y ^ ((msk >> 1) & 1), c ^ (msk & 1))
            cp = pltpu.make_async_remote_copy(src_ref=v_ref, dst_ref=gath.at[me], send_sem=send_sems.at[msk - 1],
                                              recv_sem=recv_sems.at[msk - 1], device_id=peer, device_id_type=MESH)
            cp.start()
            sends.append(cp)
        for msk in range(1, N_DEV):
            pltpu.make_async_remote_copy(src_ref=v_ref, dst_ref=gath.at[me ^ msk], send_sem=send_sems.at[msk - 1],
                                         recv_sem=recv_sems.at[msk - 1], device_id=(x, y, c), device_id_type=MESH).wait_recv()
        for cp in sends:
            cp.wait_send()
        tot = gath[0]
        for q in range(1, N_DEV):
            tot = tot + gath[q]
        o_ref[...] = tot

    vm = pl.BlockSpec(memory_space=pltpu.VMEM)
    return pl.pallas_call(
        body, name=name, in_specs=[vm], out_specs=vm, out_shape=jax.ShapeDtypeStruct(v.shape, F32),
        scratch_shapes=[pltpu.VMEM((N_DEV, rows, LANES), F32), pltpu.SemaphoreType.DMA((N_DEV - 1,)),
                        pltpu.SemaphoreType.DMA((N_DEV - 1,))],
    )(v)


def _ffn_fwd(xin, gain, w_up, w_conv, w_down, bl, tag, rides=(None, None, None)):
    got = {}

    def carried(k, result, delivered):
        if rides[k] is not None:
            got.update(zip(rides[k][1], delivered))
        return result

    n = rmsnorm_fwd(xin, gain, name=f"{tag}_ffn_norm")
    if rides[0] is None:
        u = matmul(n, w_up, name=f"{tag}_ffn_up")
    else:
        u = carried(0, *matmul(n, w_up, ride=rides[0][0], name=f"{tag}_ffn_up"))
    *hc, delivered = conv_silu_fwd(u, w_conv, bl, name=f"{tag}_ffn_conv", ride=rides[1][0] if rides[1] else None)
    h, ca, cg = carried(1, hc, delivered)
    if rides[2] is None:
        xout = matmul(h, w_down, out_dtype=F32, add=xin, name=f"{tag}_ffn_down")
    else:
        xout = carried(2, *matmul(h, w_down, out_dtype=F32, add=xin, ride=rides[2][0], name=f"{tag}_ffn_down"))
    return xout, (n, u, h, ca, cg), got


def _ffn_bwd(dxout, dxout_b, xin, saved, gain, w_up, w_conv, w_down, bl, tag, shard):
    n, u, h, ca, cg = saved
    dh = matmul(dxout_b, w_down, tb=True, name=f"{tag}_ffn_down_dx")
    g_down = matmul(h, dxout_b, ta=True, name=f"{tag}_ffn_down_dw")
    dua, dug, g_conv = conv_silu_bwd(u, ca, cg, w_conv, dh, bl, name=f"{tag}_ffn_conv_bwd")
    dn = matmul(dua, w_up, a2=dug, tb=True, out_dtype=F32, name=f"{tag}_ffn_up_dx")
    g_up = matmul(n, dua, b2=dug, ta=True, shard_cols=shard, name=f"{tag}_ffn_up_dw")
    dxin, dxin_b, g_norm = rmsnorm_bwd(xin, gain, dn, dxout, name=f"{tag}_ffn_norm_bwd")
    return dxin, dxin_b, g_norm, g_up, g_conv, g_down


def _mem_kv(mem2, gain, w_kv, tag):
    mn = rmsnorm_fwd(mem2, gain, name=f"{tag}_mem_norm")
    return mn, matmul(mn, w_kv, name=f"{tag}_mem_kv")


def _mem_kv_bwd(mem2, mn, gain, w_kv, dkvm, tag):
    g_kv = matmul(mn, dkvm, ta=True, name=f"{tag}_mem_kv_dw")
    dmn = matmul(dkvm, w_kv, tb=True, out_dtype=F32, name=f"{tag}_mem_kv_dx")
    _, _, g_norm = rmsnorm_bwd(mem2, gain, dmn, None, name=f"{tag}_mem_norm_bwd")
    return g_norm, g_kv


def _shard_major(name, grad):
    if name in COL_SHARDED:
        return grad
    return grad.reshape(N_CHIPS, grad.shape[0] // N_CHIPS, grad.shape[1])


def local_step(x, mem, target, w, shard_of, late_gather=None, early_scatter=(), scatter_last=(), reduce_early=None):
    bl, s, d = x.shape
    t = bl * s
    x0 = x.reshape(t, d)
    mem2 = mem.reshape(bl * MEM_LEN, d)
    tgt = target.reshape(t, d)
    g = {}

    n1 = rmsnorm_fwd(x0, w["a_norm_attn"], name="a_attn_norm")
    proj_a = matmul(n1, w["a_w_in"], name="a_w_in")
    mn_a, kvm_a = _mem_kv(mem2, w["a_norm_mem"], w["a_w_mem_kv"], "a")
    late_gather = late_gather or {}
    behind_sb = late_gather.get("sb")
    o_sb, o_sb_f32, delivered = sb_fwd(proj_a, bl, name="a_sb_fwd", ride=behind_sb[0] if behind_sb else None)
    if behind_sb:
        w = {**w, **dict(zip(behind_sb[1], delivered))}
    o_mem_a = mem_fwd(proj_a, kvm_a, bl, name="a_mem_fwd")
    cat_a = jnp.concatenate([o_sb, o_mem_a], axis=1)
    x1 = matmul(cat_a, w["a_w_out"], out_dtype=F32, add=x0, name="a_w_out")
    x2, ffn_a, got = _ffn_fwd(x1, w["a_norm_ffn"], w["a_ffn_up"], w["a_ffn_conv"], w["a_ffn_down"], bl, "a",
                              rides=tuple(late_gather.get(host) for host in ("up", "conv", "down")))
    w = {**w, **got}

    nkv, n3 = rmsnorm_fwd2(x2, w["kv_norm"], w["b_norm_attn"], name="kv_b_attn_norm")
    kv = matmul(nkv, w["w_kv_shared"], out_dtype=F32, name="w_kv")
    proj_b = matmul(n3, w["b_w_in"], out_dtype=F32, name="b_w_in")
    mn_b, kvm_b = _mem_kv(mem2, w["b_norm_mem"], w["b_w_mem_kv"], "b")
    dil = [dil_fwd(proj_b, kv, gi, bl, name=f"b_dil_fwd{gi}") for gi in range(3)]
    os_, lses = [o for o, _ in dil], [l for _, l in dil]
    o_dil = dil_combine_fwd(os_, lses, name="b_dil_combine")
    o_mem_b = mem_fwd(proj_b, kvm_b, bl, name="b_mem_fwd")
    cat_b = jnp.concatenate([o_dil, o_mem_b], axis=1)
    x3 = matmul(cat_b, w["b_w_out"], out_dtype=F32, add=x2, name="b_w_out")
    x4, ffn_b, _ = _ffn_fwd(x3, w["b_norm_ffn"], w["b_ffn_up"], w["b_ffn_conv"], w["b_ffn_down"], bl, "b")

    dx4, dx4b, g["final_norm"], lossvec = final_loss(x4, w["final_norm"], tgt, name="final_loss")

    dx3, dx3b, g["b_norm_ffn"], g["b_ffn_up"], g["b_ffn_conv"], g["b_ffn_down"] = _ffn_bwd(
        dx4, dx4b, x3, ffn_b, w["b_norm_ffn"], w["b_ffn_up"], w["b_ffn_conv"], w["b_ffn_down"], bl, "b",
        shard_of["b_ffn_up"])
    dcat_b = matmul(dx3b, w["b_w_out"], tb=True, name="b_w_out_dx")
    g["b_w_out"] = matmul(cat_b, dx3b, ta=True, name="b_w_out_dw")
    dog, dshift = dil_combine_bwd(os_, lses, dcat_b, name="b_dil_combine_bwd")
    dqs, dks, dvs = [], [], []
    for gi in range(3):
        dq_g, dk_g, dv_g = dil_bwd(proj_b, kv, lses[gi], dog, dshift, gi, bl, name=f"b_dil_bwd{gi}")
        dqs.append(dq_g), dks.append(dk_g), dvs.append(dv_g)
    dq_mem_b, dkvm_b = mem_bwd(proj_b, kvm_b, dcat_b, bl, name="b_mem_bwd")
    g["b_norm_mem"], g["b_w_mem_kv"] = _mem_kv_bwd(mem2, mn_b, w["b_norm_mem"], w["b_w_mem_kv"], dkvm_b, "b")
    dproj_b = jnp.concatenate([dq_g.astype(BF) for dq_g in dqs] + [dq_mem_b], axis=1)
    dn3 = matmul(dproj_b, w["b_w_in"], tb=True, out_dtype=F32, name="b_w_in_dx")
    g["b_w_in"] = matmul(n3, dproj_b, ta=True, name="b_w_in_dw")
    dkv = jnp.concatenate(dks + dvs, axis=1).astype(BF)
    dnkv = matmul(dkv, w["w_kv_shared"], tb=True, out_dtype=F32, name="w_kv_dx")
    g["w_kv_shared"] = matmul(nkv, dkv, ta=True, shard_cols=shard_of["w_kv_shared"], name="w_kv_dw")
    dx2, dx2b, g["kv_norm"], g["b_norm_attn"] = rmsnorm_bwd2(x2, w["kv_norm"], w["b_norm_attn"], dnkv, dn3, dx3,
                                                             name="kv_b_attn_norm_bwd")

    dx1, dx1b, g["a_norm_ffn"], g["a_ffn_up"], g["a_ffn_conv"], g["a_ffn_down"] = _ffn_bwd(
        dx2, dx2b, x1, ffn_a, w["a_norm_ffn"], w["a_ffn_up"], w["a_ffn_conv"], w["a_ffn_down"], bl, "a",
        shard_of["a_ffn_up"])
    dcat_a = matmul(dx1b, w["a_w_out"], tb=True, name="a_w_out_dx")
    g["a_w_out"] = matmul(cat_a, dx1b, ta=True, name="a_w_out_dw")
    leaving = scatter_ride([_shard_major(n, g[n]) for n in early_scatter]) if early_scatter else None
    dq_sb, dk_sb, dv_sb, landed = sb_bwd(proj_a, o_sb_f32, dcat_a, bl, name="a_sb_bwd", ride=leaving)
    landed = dict(zip(early_scatter, landed))
    dq_mem_a, dkvm_a = mem_bwd(proj_a, kvm_a, dcat_a, bl, name="a_mem_bwd")
    g["a_norm_mem"], g["a_w_mem_kv"] = _mem_kv_bwd(mem2, mn_a, w["a_norm_mem"], w["a_w_mem_kv"], dkvm_a, "a")
    dproj_a = jnp.concatenate([dq_sb, dk_sb.astype(BF), dv_sb.astype(BF), dq_mem_a], axis=1)
    if reduce_early is None:
        dn1 = matmul(dproj_a, w["a_w_in"], tb=True, out_dtype=F32, name="a_w_in_dx")
        g["a_w_in"] = matmul(n1, dproj_a, ta=True, shard_cols=shard_of["a_w_in"], name="a_w_in_dw")
        dx0, _, g["a_norm_attn"] = rmsnorm_bwd(x0, w["a_norm_attn"], dn1, dx1, name="a_attn_norm_bwd")
        return lossvec, dx0, g, landed
    sums = reduce_early(g, landed)
    g["a_w_in"] = matmul(n1, dproj_a, ta=True, shard_cols=shard_of["a_w_in"], name="a_w_in_dw")
    dn1, delivered = matmul(dproj_a, w["a_w_in"], tb=True, out_dtype=F32, name="a_w_in_dx",
                            ride=merge_rides(swap_ride([sums[n] for n in early_scatter]),
                                             scatter_ride([_shard_major(n, g[n]) for n in scatter_last])))
    theirs, landed_last = delivered[:len(early_scatter)], delivered[len(early_scatter):]
    dx0, _, g["a_norm_attn"] = rmsnorm_bwd(x0, w["a_norm_attn"], dn1, dx1, name="a_attn_norm_bwd")
    return lossvec, dx0, g, (sums, dict(zip(early_scatter, theirs)), dict(zip(scatter_last, landed_last)))


MATRICES = ("a_w_in", "a_w_out", "a_w_mem_kv", "a_ffn_up", "a_ffn_down", "w_kv_shared", "b_w_in", "b_w_out",
            "b_w_mem_kv", "b_ffn_up", "b_ffn_down")
COL_SHARDED = ("a_w_in", "a_ffn_up", "w_kv_shared", "b_ffn_up")
FIRST_NEEDED = ("a_w_in", "a_w_mem_kv")
GATHER_BEHIND = {"sb": ("a_w_out", "a_ffn_up", "a_ffn_down", "b_ffn_up"), "up": ("b_ffn_down", "w_kv_shared"),
                 "conv": ("b_w_in", "b_w_out", "b_w_mem_kv")}
SMALL_SHARDED = ("a_norm_attn", "a_norm_mem", "a_norm_ffn", "a_ffn_conv", "b_ffn_conv")
SMALL_REPLICATED = ("kv_norm", "b_norm_attn", "b_norm_mem", "b_norm_ffn", "final_norm")
WEIGHTS = ("a_norm_attn", "a_w_in", "a_w_out", "a_norm_mem", "a_w_mem_kv", "a_norm_ffn", "a_ffn_up", "a_ffn_conv",
           "a_ffn_down", "kv_norm", "w_kv_shared", "b_norm_attn", "b_w_in", "b_w_out", "b_norm_mem", "b_w_mem_kv",
           "b_norm_ffn", "b_ffn_up", "b_ffn_conv", "b_ffn_down", "final_norm")


def _two_d(a):
    if a.ndim == 1:
        return a.reshape(1, -1)
    return a.reshape(a.shape[-2], a.shape[-1])


def kernel(x, mem, a_norm_attn, a_w_in, a_w_out, a_norm_mem, a_w_mem_kv, a_norm_ffn, a_ffn_up, a_ffn_conv, a_ffn_down, kv_norm, w_kv_shared, b_norm_attn, b_w_in, b_w_out, b_norm_mem, b_w_mem_kv, b_norm_ffn, b_ffn_up, b_ffn_conv, b_ffn_down, final_norm, loss_target, m_a_norm_attn, m_a_w_in, m_a_w_out, m_a_norm_mem, m_a_w_mem_kv, m_a_norm_ffn, m_a_ffn_up, m_a_ffn_conv, m_a_ffn_down, m_kv_norm, m_w_kv_shared, m_b_norm_attn, m_b_w_in, m_b_w_out, m_b_norm_mem, m_b_w_mem_kv, m_b_norm_ffn, m_b_ffn_up, m_b_ffn_conv, m_b_ffn_down, m_final_norm, v_a_norm_attn, v_a_w_in, v_a_w_out, v_a_norm_mem, v_a_w_mem_kv, v_a_norm_ffn, v_a_ffn_up, v_a_ffn_conv, v_a_ffn_down, v_kv_norm, v_w_kv_shared, v_b_norm_attn, v_b_w_in, v_b_w_out, v_b_norm_mem, v_b_w_mem_kv, v_b_norm_ffn, v_b_ffn_up, v_b_ffn_conv, v_b_ffn_down, v_final_norm):
    given = dict(locals())
    wl = {n: _two_d(given[n]) for n in WEIGHTS}
    ml = {n: _two_d(given["m_" + n]) for n in WEIGHTS}
    vl = {n: _two_d(given["v_" + n]) for n in WEIGHTS}
    chip = 2 * lax.axis_index("x") + lax.axis_index("y")

    packed = jnp.concatenate([wl[n].reshape(-1, LANES) for n in SMALL_SHARDED], axis=0)
    axis_of = lambda n: 1 if n in COL_SHARDED else 0
    late = tuple(n for n in MATRICES if n not in FIRST_NEEDED)
    full = gather_ride([wl[n].astype(BF) for n in FIRST_NEEDED] + [packed],
                       [axis_of(n) for n in FIRST_NEEDED] + [0]).run("gather_first")
    w = dict(zip(FIRST_NEEDED, full[:-1]))
    late_gather = {host: (gather_ride([wl[n].astype(BF) for n in names], [axis_of(n) for n in names]), names)
                   for host, names in GATHER_BEHIND.items()}
    rows = packed.shape[0]
    per_chip = full[-1].reshape(N_CHIPS, rows, LANES)
    r0 = 0
    for n in SMALL_SHARDED:
        nr = wl[n].size // LANES
        piece = per_chip[:, r0:r0 + nr].reshape(N_CHIPS, wl[n].shape[0], wl[n].shape[1])
        w[n] = jnp.concatenate([piece[q] for q in range(N_CHIPS)], axis=1)
        r0 += nr
    for n in SMALL_REPLICATED:
        w[n] = wl[n]

    shard_of = {n: wl[n].shape[1] for n in COL_SHARDED}
    def core_sums(g, landed):
        return {n: sum4(lax.dynamic_index_in_dim(_shard_major(n, g[n]), chip, 0, keepdims=False), landed[n],
                        name=f"sum4_{n}") for n in landed}

    lossvec, dx0, g, (sums, theirs, landed_last) = local_step(
        x, mem, loss_target, w, shard_of, late_gather=late_gather, early_scatter=late, scatter_last=FIRST_NEEDED,
        reduce_early=core_sums)
    loss = lax.psum(0.5 * jnp.sum(lossvec) / x.shape[-1], ("x", "y", "c"))
    sums.update(core_sums(g, landed_last))
    theirs.update(zip(FIRST_NEEDED, swap_ride([sums[n] for n in FIRST_NEEDED]).run("swap_last")))
    out = {}
    for n in MATRICES:
        out[n] = adamw(wl[n], sums[n], theirs[n], ml[n], vl[n], name=f"adamw_{n}")

    small = SMALL_SHARDED + SMALL_REPLICATED
    flat = jnp.concatenate([g[n].reshape(-1, LANES) for n in small], axis=0)
    tot = all_reduce_small(flat, name="all_reduce_small")
    r0 = 0
    for n in small:
        nr = g[n].size // LANES
        gn = tot[r0:r0 + nr].reshape(g[n].shape)
        r0 += nr
        if n in SMALL_SHARDED:
            gn = lax.dynamic_slice_in_dim(gn, chip * wl[n].shape[1], wl[n].shape[1], axis=1)
        out[n] = adamw(wl[n], gn, None, ml[n], vl[n], name=f"adamw_{n}")

    res = [loss, dx0.reshape(x.shape)]
    for slot in range(4):
        res += [out[n][slot].reshape(given[n].shape) for n in WEIGHTS]
    return tuple(res)
```

```python
import functools
import math

import jax
import jax.numpy as jnp
from jax import lax
from jax.experimental import pallas as pl
from jax.experimental.pallas import tpu as pltpu

F32 = jnp.float32
BF = jnp.bfloat16
MESH = pl.DeviceIdType.MESH

HEAD_DIM = 64
LANES = 128
SB_WIDTH = 12 * HEAD_DIM
MEM_WIDTH = 4 * HEAD_DIM
DIL_WIDTH = 12 * HEAD_DIM
MEM_LEN = 256
DIL_GROUPS = ((128, 1), (512, 4), (2048, 16))
QB = 128
EPS = 1e-6
SCALE = HEAD_DIM ** -0.5
NEG = -1e30
ALIBI = tuple(2.0 ** (-8.0 * i / 12) for i in range(1, 13))
N_CHIPS = 4
N_DEV = 8

ADAM_LR, ADAM_B1, ADAM_B2, ADAM_EPS, ADAM_WD, ADAM_STEP = 0.001, 0.9, 0.999, 1e-08, 0.01, 10

VMEM_LIMIT = 52 * 1024 * 1024
MATMUL_VMEM_BUDGET = 44 * 1024 * 1024
MATMUL_MAX_TN = 2816
HBM_BYTES_PER_S = 3.0e12
VMEM_BYTES_PER_S = 6.0e12
STEP_OVERHEAD_S = 0.4e-6


def _mo(v, m):
    return v if isinstance(v, int) else pl.multiple_of(v, m)


def _pick(n, prefs):
    for t in prefs:
        if n % t == 0:
            return t
    return n


def _params(*sem):
    return pltpu.CompilerParams(dimension_semantics=sem, vmem_limit_bytes=VMEM_LIMIT)


def matmul(a, b, *, ta=False, tb=False, out_dtype=BF, add=None, shard_cols=0, a2=None, b2=None, ride=None, name):
    m, k = (a.shape[1], a.shape[0]) if ta else a.shape
    n = b.shape[0] if tb else b.shape[1]
    if a2 is not None:
        assert not ta and a2.shape == a.shape
        k *= 2
    if b2 is not None:
        assert not tb and b2.shape == b.shape
        n *= 2
    k_piece = a.shape[1] if a2 is not None else k
    n_piece = b.shape[1] if b2 is not None else n
    out_bytes = jnp.dtype(out_dtype).itemsize
    pieces_a, pieces_b = (2 if a2 is not None else 1), (2 if b2 is not None else 1)
    if shard_cols:
        tn_choices = [c * shard_cols for c in (4, 2, 1) if c * shard_cols <= MATMUL_MAX_TN and c <= 4 // pieces_b]
    else:
        tn_choices = [c for c in (2816, 2560, 1536, 1408, 1024, 512, 256, 128) if n_piece % c == 0]

    def vmem_bytes(tm_, tn_, tk_):
        blocks = (tm_ * tk_ * a.dtype.itemsize * pieces_a + tk_ * tn_ * b.dtype.itemsize * pieces_b
                  + tm_ * tn_ * out_bytes + (tm_ * tn_ * 4 if add is not None else 0))
        return 2 * blocks + (tm_ * tn_ * 4 if k // tk_ > 1 else 0) + tm_ * tn_ * 4

    def cost(c):
        tm_, tn_, tk_ = c
        steps = (m // tm_) * (n // tn_) * (k // tk_)
        hbm = (m * k * a.dtype.itemsize * (n // tn_) + k * n * b.dtype.itemsize * (m // tm_)
               + m * n * (out_bytes + (4 if add is not None else 0)))
        return hbm / HBM_BYTES_PER_S + steps * STEP_OVERHEAD_S + m * n * 8 * (k // tk_ - 1) / VMEM_BYTES_PER_S

    tm, tn, tk = min(((tm_, tn_, tk_) for tm_ in (1024, 512, 1408, 256, 128) if m % tm_ == 0 for tn_ in tn_choices
                      for tk_ in (2816, 2560, 2048, 1536, 1024, 512, 256, 128) if k_piece % tk_ == 0
                      if vmem_bytes(tm_, tn_, tk_) <= MATMUL_VMEM_BUDGET), key=cost)
    ns = tn // shard_cols if shard_cols else 1
    nk = k // tk
    nk1 = nk // 2
    nj1 = (n // tn) // 2
    dims = (((0,) if ta else (1,), (1,) if tb else (0,)), ((), ()))
    n_in = 2 + (a2 is not None) + (b2 is not None) + (add is not None)
    grid = (m // tm, n // tn, nk)

    def body(*refs):
        own_in, (o_ref,), scr, ride_start, ride_wait = _ride_hooks(ride, refs, n_in, 1, int(nk > 1), grid)
        ride_start()
        ins = list(own_in)
        a_ref, b_ref = ins.pop(0), ins.pop(0)
        a2_ref = ins.pop(0) if a2 is not None else None
        b2_ref = ins.pop(0) if b2 is not None else None
        add_ref = ins.pop(0) if add is not None else None

        a_tile, b_tile = a_ref[...], b_ref[...]
        if a2 is not None:
            a_tile = jnp.where(pl.program_id(2) < nk1, a_tile, a2_ref[...])
        if b2 is not None:
            b_tile = jnp.where(pl.program_id(1) < nj1, b_tile, b2_ref[...])
        part = lax.dot_general(a_tile.astype(BF), b_tile.astype(BF), dims, preferred_element_type=F32)

        def finish(r):
            if add is not None:
                r = r + add_ref[...]
            if shard_cols:
                for c in range(ns):
                    o_ref[c] = r[:, c * shard_cols:(c + 1) * shard_cols].astype(o_ref.dtype)
            else:
                o_ref[...] = r.astype(o_ref.dtype)

        if nk == 1:
            finish(part)
        else:
            acc_ref = scr[0]
            kk = pl.program_id(2)

            @pl.when(kk == 0)
            def _():
                acc_ref[...] = part

            @pl.when(kk > 0)
            def _():
                acc_ref[...] += part

            @pl.when(kk == nk - 1)
            def _():
                finish(acc_ref[...])
        ride_wait()

    a_spec = pl.BlockSpec((tk, tm), lambda i, j, q: (q, i)) if ta else pl.BlockSpec((tm, tk), lambda i, j, q: (i, q))
    b_spec = pl.BlockSpec((tn, tk), lambda i, j, q: (j, q)) if tb else pl.BlockSpec((tk, tn), lambda i, j, q: (q, j))
    if a2 is not None:
        a_spec = pl.BlockSpec((tm, tk), lambda i, j, q: (i, jnp.minimum(q, nk1 - 1)))
    if b2 is not None:
        b_spec = pl.BlockSpec((tk, tn), lambda i, j, q: (jnp.where(j < nj1, q, 0), jnp.minimum(j, nj1 - 1)))
    in_specs = [a_spec, b_spec]
    args = [a, b]
    if a2 is not None:
        in_specs.append(pl.BlockSpec((tm, tk), lambda i, j, q: (i, jnp.maximum(q - nk1, 0))))
        args.append(a2)
    if b2 is not None:
        in_specs.append(pl.BlockSpec((tk, tn), lambda i, j, q: (jnp.where(j < nj1, 0, q), jnp.maximum(j - nj1, 0))))
        args.append(b2)
    if add is not None:
        in_specs.append(pl.BlockSpec((tm, tn), lambda i, j, q: (i, j)))
        args.append(add)
    if shard_cols:
        out_shape = jax.ShapeDtypeStruct((N_CHIPS, m, shard_cols), out_dtype)
        out_spec = pl.BlockSpec((ns, tm, shard_cols), lambda i, j, q: (j, i, 0))
    else:
        out_shape = jax.ShapeDtypeStruct((m, n), out_dtype)
        out_spec = pl.BlockSpec((tm, tn), lambda i, j, q: (i, j))
    own_scratch = [pltpu.VMEM((tm, tn), F32)] if nk > 1 else []
    if ride is None:
        return pl.pallas_call(
            body, name=name, grid=grid, in_specs=in_specs, out_specs=out_spec, out_shape=out_shape,
            scratch_shapes=own_scratch, compiler_params=_params("parallel", "parallel", "arbitrary"),
        )(*args)
    out, *delivered = pl.pallas_call(
        body, name=name, grid=grid, in_specs=in_specs + [ANY] * len(ride.arrays),
        out_specs=[out_spec] + [ANY] * len(ride.out_shape), out_shape=[out_shape] + ride.out_shape,
        scratch_shapes=own_scratch + ride.scratch, compiler_params=_params("arbitrary", "arbitrary", "arbitrary"),
    )(*args, *ride.arrays)
    return out, delivered


def rmsnorm_fwd(x, g, *, name):
    t, d = x.shape
    tr = _pick(t, (512, 256))

    def body(x_ref, g_ref, o_ref):
        xv = x_ref[...]
        r = lax.rsqrt(jnp.mean(xv * xv, axis=-1, keepdims=True) + EPS)
        o_ref[...] = ((xv * r) * g_ref[...]).astype(o_ref.dtype)

    return pl.pallas_call(
        body, name=name, grid=(t // tr,),
        in_specs=[pl.BlockSpec((tr, d), lambda i: (i, 0)), pl.BlockSpec((1, d), lambda i: (0, 0))],
        out_specs=pl.BlockSpec((tr, d), lambda i: (i, 0)), out_shape=jax.ShapeDtypeStruct((t, d), BF),
        compiler_params=_params("parallel"),
    )(x, g)


def rmsnorm_bwd(x, g, dn, dres, *, name, ride=None):
    t, d = x.shape
    tr = _pick(t, (512, 256))
    want_dx = dres is not None

    def body(*refs):
        if want_dx:
            (x_ref, g_ref, dn_ref, dres_ref), (dx_ref, dxb_ref, dg_ref), _, ride_start, ride_wait = _ride_hooks(
                ride, refs, 4, 3, 0, (t // tr,))
            ride_start()
        else:
            x_ref, g_ref, dn_ref, dg_ref = refs

        @pl.when(pl.program_id(0) == 0)
        def _():
            dg_ref[...] = jnp.zeros_like(dg_ref)

        xv = x_ref[...]
        r = lax.rsqrt(jnp.mean(xv * xv, axis=-1, keepdims=True) + EPS)
        xn = xv * r
        dnv = dn_ref[...].astype(F32)
        dg_ref[...] += jnp.sum(dnv * xn, axis=0, keepdims=True)
        if want_dx:
            dyg = dnv * g_ref[...]
            cm = jnp.mean(dyg * xn, axis=-1, keepdims=True)
            dx = dres_ref[...] + r * (dyg - xn * cm)
            dx_ref[...] = dx
            dxb_ref[...] = dx.astype(BF)
            ride_wait()

    row = pl.BlockSpec((tr, d), lambda i: (i, 0))
    vec = pl.BlockSpec((1, d), lambda i: (0, 0))
    if want_dx:
        extra = ride.arrays if ride else []
        dx, dxb, dg, *delivered = pl.pallas_call(
            body, name=name, grid=(t // tr,), in_specs=[row, vec, row, row] + [ANY] * len(extra),
            out_specs=[row, row, vec] + [ANY] * (len(ride.out_shape) if ride else 0),
            out_shape=[jax.ShapeDtypeStruct((t, d), F32), jax.ShapeDtypeStruct((t, d), BF), jax.ShapeDtypeStruct((1, d), F32)]
            + (ride.out_shape if ride else []),
            scratch_shapes=ride.scratch if ride else [], compiler_params=_params("arbitrary"),
        )(x, g, dn, dres, *extra)
        return (dx, dxb, dg, delivered) if ride else (dx, dxb, dg)
    return None, None, pl.pallas_call(
        body, name=name, grid=(t // tr,), in_specs=[row, vec, row], out_specs=vec,
        out_shape=jax.ShapeDtypeStruct((1, d), F32), compiler_params=_params("arbitrary"),
    )(x, g, dn)


def rmsnorm_fwd2(x, g1, g2, *, name):
    t, d = x.shape
    tr = _pick(t, (512, 256))

    def body(x_ref, g1_ref, g2_ref, o1_ref, o2_ref):
        xv = x_ref[...]
        xn = xv * lax.rsqrt(jnp.mean(xv * xv, axis=-1, keepdims=True) + EPS)
        o1_ref[...] = (xn * g1_ref[...]).astype(o1_ref.dtype)
        o2_ref[...] = (xn * g2_ref[...]).astype(o2_ref.dtype)

    row = pl.BlockSpec((tr, d), lambda i: (i, 0))
    vec = pl.BlockSpec((1, d), lambda i: (0, 0))
    return pl.pallas_call(
        body, name=name, grid=(t // tr,), in_specs=[row, vec, vec], out_specs=[row, row],
        out_shape=[jax.ShapeDtypeStruct((t, d), BF)] * 2, compiler_params=_params("parallel"),
    )(x, g1, g2)


def rmsnorm_bwd2(x, g1, g2, dn1, dn2, dres, *, name):
    t, d = x.shape
    tr = _pick(t, (512, 256))

    def body(x_ref, g1_ref, g2_ref, dn1_ref, dn2_ref, dres_ref, dx_ref, dxb_ref, dg1_ref, dg2_ref):
        @pl.when(pl.program_id(0) == 0)
        def _():
            dg1_ref[...] = jnp.zeros_like(dg1_ref)
            dg2_ref[...] = jnp.zeros_like(dg2_ref)

        xv = x_ref[...]
        r = lax.rsqrt(jnp.mean(xv * xv, axis=-1, keepdims=True) + EPS)
        xn = xv * r
        d1, d2 = dn1_ref[...].astype(F32), dn2_ref[...].astype(F32)
        dg1_ref[...] += jnp.sum(d1 * xn, axis=0, keepdims=True)
        dg2_ref[...] += jnp.sum(d2 * xn, axis=0, keepdims=True)
        dyg = d1 * g1_ref[...] + d2 * g2_ref[...]
        cm = jnp.mean(dyg * xn, axis=-1, keepdims=True)
        dx = dres_ref[...] + r * (dyg - xn * cm)
        dx_ref[...] = dx
        dxb_ref[...] = dx.astype(BF)

    row = pl.BlockSpec((tr, d), lambda i: (i, 0))
    vec = pl.BlockSpec((1, d), lambda i: (0, 0))
    return pl.pallas_call(
        body, name=name, grid=(t // tr,), in_specs=[row, vec, vec, row, row, row], out_specs=[row, row, vec, vec],
        out_shape=[jax.ShapeDtypeStruct((t, d), F32), jax.ShapeDtypeStruct((t, d), BF), jax.ShapeDtypeStruct((1, d), F32),
                   jax.ShapeDtypeStruct((1, d), F32)],
        compiler_params=_params("arbitrary"),
    )(x, g1, g2, dn1, dn2, dres)


def final_loss(x, g, target, *, name):
    t, d = x.shape
    tr = _pick(t, (512, 256))

    def body(x_ref, g_ref, t_ref, dx_ref, dxb_ref, dg_ref, lv_ref):
        @pl.when(pl.program_id(0) == 0)
        def _():
            dg_ref[...] = jnp.zeros_like(dg_ref)
            lv_ref[...] = jnp.zeros_like(lv_ref)

        xv = x_ref[...]
        r = lax.rsqrt(jnp.mean(xv * xv, axis=-1, keepdims=True) + EPS)
        xn = xv * r
        err = xn * g_ref[...] - t_ref[...]
        lv_ref[...] += jnp.sum(err * err, axis=0, keepdims=True)
        dy = err * (1.0 / d)
        dg_ref[...] += jnp.sum(dy * xn, axis=0, keepdims=True)
        dyg = dy * g_ref[...]
        cm = jnp.mean(dyg * xn, axis=-1, keepdims=True)
        dx = r * (dyg - xn * cm)
        dx_ref[...] = dx
        dxb_ref[...] = dx.astype(BF)

    row = pl.BlockSpec((tr, d), lambda i: (i, 0))
    vec = pl.BlockSpec((1, d), lambda i: (0, 0))
    return pl.pallas_call(
        body, name=name, grid=(t // tr,), in_specs=[row, vec, row], out_specs=[row, row, vec, vec],
        out_shape=[jax.ShapeDtypeStruct((t, d), F32), jax.ShapeDtypeStruct((t, d), BF), jax.ShapeDtypeStruct((1, d), F32),
                   jax.ShapeDtypeStruct((1, d), F32)],
        compiler_params=_params("arbitrary"),
    )(x, g, target)


CONV_ROWS = 256
HALO = 16


def _conv_taps(ext, w, rows):
    s0 = ext[HALO:HALO + rows]
    s1 = pltpu.roll(ext, 1, 0)[HALO:HALO + rows]
    s2 = pltpu.roll(ext, 2, 0)[HALO:HALO + rows]
    return (w[0:1] * s2 + w[1:2] * s1) + w[2:3] * s0, s0, s1, s2


def conv_silu_fwd(u, w, bl, *, name, ride=None):
    t, f2 = u.shape
    f = f2 // 2
    s = t // bl
    tc = _pick(f, (256, 128))
    nf = f // tc
    nch = s // CONV_ROWS

    def body(*refs):
        (ua_ref, ug_ref, wa_ref, wg_ref), (h_ref, ca_ref, cg_ref), _, ride_start, ride_wait = _ride_hooks(
            ride, refs, 4, 3, 0, (nf, bl))
        ride_start()
        wa, wg = wa_ref[...], wg_ref[...]

        def chunk(ci, carry):
            r0 = pl.multiple_of(ci * CONV_ROWS, CONV_ROWS)
            ps = pl.multiple_of(jnp.maximum(r0 - HALO, 0), HALO)
            keep = (ci > 0).astype(F32)
            rows = pl.ds(r0, CONV_ROWS)

            def conv(ref, wv):
                ext = jnp.concatenate([ref[0, pl.ds(ps, HALO), :].astype(F32) * keep, ref[0, rows, :].astype(F32)], axis=0)
                return _conv_taps(ext, wv, CONV_ROWS)[0]

            ca, cg = conv(ua_ref, wa), conv(ug_ref, wg)
            sg = pl.reciprocal(1.0 + jnp.exp(-cg), approx=True)
            h_ref[0, rows, :] = ((cg * sg) * ca).astype(h_ref.dtype)
            ca_ref[0, rows, :] = ca.astype(ca_ref.dtype)
            cg_ref[0, rows, :] = cg.astype(cg_ref.dtype)
            return carry

        lax.fori_loop(0, nch, chunk, 0)
        ride_wait()

    u3 = u.reshape(bl, s, f2)
    blk = lambda off: pl.BlockSpec((1, s, tc), lambda j, b: (b, 0, j + off))
    wblk = lambda off: pl.BlockSpec((3, tc), lambda j, b: (0, j + off))
    extra = ride.arrays if ride else []
    h, ca, cg, *delivered = pl.pallas_call(
        body, name=name, grid=(nf, bl), in_specs=[blk(0), blk(nf), wblk(0), wblk(nf)] + [ANY] * len(extra),
        out_specs=[blk(0)] * 3 + [ANY] * (len(ride.out_shape) if ride else 0),
        out_shape=[jax.ShapeDtypeStruct((bl, s, f), BF)] * 3 + (ride.out_shape if ride else []),
        scratch_shapes=ride.scratch if ride else [],
        compiler_params=_params(*(("arbitrary",) * 2 if ride else ("parallel",) * 2)),
    )(u3, u3, w, w, *extra)
    return h.reshape(t, f), ca.reshape(t, f), cg.reshape(t, f), delivered


def conv_silu_bwd(u, ca, cg, w, dh, bl, *, name):
    t, f2 = u.shape
    f = f2 // 2
    s = t // bl
    tc = _pick(f, (256, 128))
    nf = f // tc
    nch = s // CONV_ROWS
    ext_rows = CONV_ROWS + HALO

    def body(ua_ref, ug_ref, ca_ref, cg_ref, wa_ref, wg_ref, dh_ref, dua_ref, dug_ref, dwa_ref, dwg_ref):
        wa, wg = wa_ref[...], wg_ref[...]

        @pl.when(pl.program_id(1) == 0)
        def _():
            dwa_ref[...] = jnp.zeros_like(dwa_ref)
            dwg_ref[...] = jnp.zeros_like(dwg_ref)

        def chunk(ci, carry):
            r0 = pl.multiple_of(ci * CONV_ROWS, CONV_ROWS)
            ns = pl.multiple_of(jnp.minimum(r0 + CONV_ROWS, s - HALO), HALO)
            keep_n = (ci < nch - 1).astype(F32)
            rows = pl.ds(r0, CONV_ROWS)

            def ext_of(ref):
                return jnp.concatenate([ref[0, rows, :].astype(F32), ref[0, pl.ds(ns, HALO), :].astype(F32)], axis=0)

            dhe = jnp.concatenate([dh_ref[0, rows, :].astype(F32), dh_ref[0, pl.ds(ns, HALO), :].astype(F32) * keep_n],
                                  axis=0)
            cae, cge = ext_of(ca_ref), ext_of(cg_ref)
            sg = pl.reciprocal(1.0 + jnp.exp(-cge), approx=True)
            dca = dhe * (cge * sg)
            dcg = dhe * cae * (sg * (1.0 + cge * (1.0 - sg)))

            def back(dc, wv, u_ref, du_ref, dw_ref):
                d0 = dc[:CONV_ROWS]
                n1 = pltpu.roll(dc, ext_rows - 1, 0)[:CONV_ROWS]
                n2 = pltpu.roll(dc, ext_rows - 2, 0)[:CONV_ROWS]
                du_ref[0, rows, :] = ((wv[2:3] * d0 + wv[1:2] * n1) + wv[0:1] * n2).astype(du_ref.dtype)
                uc = u_ref[0, rows, :].astype(F32)
                for k, shifted in enumerate((n2, n1, d0)):
                    dw_ref[k:k + 1, :] += jnp.sum(shifted * uc, axis=0, keepdims=True)

            back(dca, wa, ua_ref, dua_ref, dwa_ref)
            back(dcg, wg, ug_ref, dug_ref, dwg_ref)
            return carry

        lax.fori_loop(0, nch, chunk, 0)

    u3 = u.reshape(bl, s, f2)
    blk = lambda off: pl.BlockSpec((1, s, tc), lambda j, b: (b, 0, j + off))
    wblk = lambda off: pl.BlockSpec((3, tc), lambda j, b: (0, j + off))
    dua, dug, dwa, dwg = pl.pallas_call(
        body, name=name, grid=(nf, bl), in_specs=[blk(0), blk(nf), blk(0), blk(0), wblk(0), wblk(nf), blk(0)],
        out_specs=[blk(0), blk(0), wblk(0), wblk(0)],
        out_shape=[jax.ShapeDtypeStruct((bl, s, f), BF), jax.ShapeDtypeStruct((bl, s, f), BF),
                   jax.ShapeDtypeStruct((3, f), F32), jax.ShapeDtypeStruct((3, f), F32)],
        compiler_params=_params("parallel", "arbitrary"),
    )(u3, u3, ca.reshape(bl, s, f), cg.reshape(bl, s, f), w, w, dh.reshape(bl, s, f))
    return dua.reshape(t, f), dug.reshape(t, f), jnp.concatenate([dwa, dwg], axis=1)


def _lane_masks(rows):
    lane = lax.broadcasted_iota(jnp.int32, (rows, LANES), 1)
    return lane < HEAD_DIM, lane >= HEAD_DIM


def _nt(a, b):
    return lax.dot_general(a, b, (((1,), (1,)), ((), ())), preferred_element_type=F32)


def _tn(a, b):
    return lax.dot_general(a, b, (((0,), (0,)), ((), ())), preferred_element_type=F32)


def _nn(a, b):
    return jnp.dot(a, b, preferred_element_type=F32)


def _suffix_ones():
    j = lax.broadcasted_iota(jnp.int32, (2 * QB, QB), 0) % QB
    s = lax.broadcasted_iota(jnp.int32, (2 * QB, QB), 1)
    return (j >= s).astype(BF)


def _softplus(z):
    return jnp.maximum(z, 0.0) + jnp.log(1.0 + jnp.exp(-jnp.abs(z)))


SB_BLOCK = 256
SB_DEAD = -100.0


SB_STRIP = 32
SB_SUB = SB_BLOCK // QB


def _sb_scratch(backward):
    blk = (2, SB_BLOCK, SB_BLOCK)
    per_head = pltpu.VMEM((2, SB_BLOCK, LANES), F32)
    scr = [pltpu.VMEM((2 * QB, QB), BF),
           pltpu.VMEM(blk, F32),
           pltpu.VMEM((2, SB_SUB) + blk[1:], BF),
           pltpu.VMEM(blk, F32),
           pltpu.VMEM(blk, BF),
           per_head]
    if not backward:
        return scr + [per_head]
    return scr + [pltpu.VMEM(blk, F32),
                  pltpu.VMEM(blk, F32),
                  pltpu.VMEM(blk, F32),
                  pltpu.VMEM(blk, BF),
                  per_head,
                  per_head,
                  per_head]


def _sb_strips(fn, diag):
    for r in range(SB_BLOCK // SB_STRIP):
        fn(pl.ds(r * SB_STRIP, SB_STRIP), r, QB if diag and (r + 1) * SB_STRIP <= QB else SB_BLOCK)


def _sb_keep(diag, r, v):
    if not diag:
        return v
    rel = (lax.broadcasted_iota(jnp.int32, v.shape, 1) - lax.broadcasted_iota(jnp.int32, v.shape, 0))
    return jnp.where(rel < r * SB_STRIP, v, 0.0)


def _sb_split_store(sp_scr, rows, v):
    hi = v.astype(BF)
    lo = (v - hi.astype(F32)).astype(BF)
    for u in range(SB_SUB):
        if (u + 1) * QB <= v.shape[1]:
            sp_scr[u, rows, 0:QB] = hi[:, u * QB:(u + 1) * QB]
            sp_scr[u, rows, QB:2 * QB] = lo[:, u * QB:(u + 1) * QB]
        else:
            sp_scr[u, rows, :] = jnp.zeros((SB_STRIP, 2 * QB), BF)


def _sb_suffix_sums(uu_scr, sp_scr, out_scr):
    for u in range(SB_SUB):
        out_scr[:, u * QB:(u + 1) * QB] = _nn(sp_scr[u], uu_scr[...])


def _sb_fold(sums, off):
    nsub = sums.shape[1] // QB
    offs = [None] * nsub
    for u in reversed(range(nsub)):
        offs[u] = off
        off = off + jnp.broadcast_to(sums[:, u * QB:u * QB + 1], (SB_STRIP, LANES))
    return offs, off


def _sb_store_live(ref, rows, v):
    live = v.shape[1]
    ref[rows, 0:live] = v
    if live < SB_BLOCK:
        ref[rows, live:] = jnp.zeros((SB_STRIP, SB_BLOCK - live), v.dtype)


def _sb_first(diag, z_scr, sp_scr, ls_scr=None):
    def strip(rows, r, live):
        ls = _sb_keep(diag, r, -_softplus(z_scr[rows, 0:live]))
        if ls_scr is not None:
            ls_scr[rows, 0:live] = ls
        _sb_split_store(sp_scr, rows, ls)

    _sb_strips(strip, diag)


def _sb_second(diag, z_scr, cin_scr, w_scr, carry_scr):
    def strip(rows, r, live):
        z, cin = z_scr[rows, 0:live], cin_scr[rows, 0:live]
        offs, carry_scr[rows, :] = _sb_fold(cin, carry_scr[rows, :])
        e = jnp.concatenate([z[:, u * QB:(u + 1) * QB] + cin[:, u * QB:(u + 1) * QB] + offs[u]
                             for u in range(live // QB)], axis=1)
        _sb_store_live(w_scr, rows, _sb_keep(diag, r, jnp.exp(e)).astype(BF))

    _sb_strips(strip, diag)


def _sb_third(diag, w_scr, dw_scr, da_scr, sp_scr):
    def strip(rows, r, live):
        da = w_scr[rows, 0:live].astype(F32) * dw_scr[rows, 0:live]
        da_scr[rows, 0:live] = da
        _sb_split_store(sp_scr, rows, da)

    _sb_strips(strip, diag)


def _sb_fourth(diag, z_scr, ls_scr, da_scr, sin_scr, dz_scr, carry_s_scr, dsum_scr):
    def strip(rows, r, live):
        da, sin = da_scr[rows, 0:live], sin_scr[rows, 0:live]
        offs, carry_s_scr[rows, :] = _sb_fold(sin, carry_s_scr[rows, :])
        dsum = dsum_scr[rows, :]
        pre = jnp.concatenate([dsum - (sin[:, u * QB:(u + 1) * QB] - da[:, u * QB:(u + 1) * QB] + offs[u])
                               for u in range(live // QB)], axis=1)
        sig = jnp.exp(z_scr[rows, 0:live] + ls_scr[rows, 0:live])
        _sb_store_live(dz_scr, rows, _sb_keep(diag, r, da - sig * pre).astype(BF))

    _sb_strips(strip, diag)


def _sb_alive(carry_scr):
    return (jnp.max(carry_scr[...]) > SB_DEAD).astype(jnp.int32)


def _ride_hooks(ride, refs, n_in, n_out, n_scratch, grid):
    if ride is None:
        return refs[:n_in], refs[n_in:n_in + n_out], refs[n_in + n_out:], lambda: None, lambda: None
    ni, no = len(ride.arrays), len(ride.out_shape)
    own_in, rin = refs[:n_in], refs[n_in:n_in + ni]
    own_out, rout = refs[n_in + ni:n_in + ni + n_out], refs[n_in + ni + n_out:n_in + ni + n_out + no]
    rest = refs[n_in + ni + n_out + no:]
    own_scr, sems = rest[:n_scratch], rest[n_scratch:]
    ids = [pl.program_id(a) for a in range(len(grid))]

    def start():
        first = functools.reduce(lambda u, v: u & v, [i == 0 for i in ids])
        pl.when(first)(lambda: ride.start(rin, rout, sems))

    def wait():
        last = functools.reduce(lambda u, v: u & v, [i == n - 1 for i, n in zip(ids, grid)])
        pl.when(last)(lambda: ride.wait(rin, rout, sems))

    return own_in, own_out, own_scr, start, wait


def sb_fwd(proj, bl, *, name, ride=None):
    t, width = proj.shape
    s = t // bl
    npair = SB_WIDTH // LANES
    nq = s // SB_BLOCK
    grid = (bl, npair, nq)

    own_scratch = _sb_scratch(backward=False)

    def body(*refs):
        (q_ref, k_ref, v_ref), (o_ref, of_ref), scr, ride_start, ride_wait = _ride_hooks(
            ride, refs, 3, 2, len(own_scratch), grid)
        uu_scr, z_scr, sp_scr, cin_scr, w_scr, carry_scr, acc_scr = scr
        ride_start()
        i = pl.program_id(2)
        uu_scr[...] = _suffix_ones()
        carry_scr[...] = jnp.zeros_like(carry_scr)
        acc_scr[...] = jnp.zeros_like(acc_scr)
        qs = (q_ref[0].astype(F32) * SCALE).astype(BF)
        masks = _lane_masks(SB_BLOCK)

        z, sp, cin, w, car, acc = ([r.at[h] for h in range(2)] for r in (z_scr, sp_scr, cin_scr, w_scr, carry_scr, acc_scr))

        def live(c, diag):
            start = pl.multiple_of(c * SB_BLOCK, SB_BLOCK)
            kb, vb = k_ref[0, pl.ds(start, SB_BLOCK), :], v_ref[0, pl.ds(start, SB_BLOCK), :]
            for h in range(2):
                z[h][...] = _nt(jnp.where(masks[h], qs, jnp.zeros_like(qs)), kb)
            _sb_first(diag, z[0], sp[0])
            _sb_suffix_sums(uu_scr, sp[0], cin[0])
            _sb_first(diag, z[1], sp[1])
            _sb_suffix_sums(uu_scr, sp[1], cin[1])
            _sb_second(diag, z[0], cin[0], w[0], car[0])
            acc[0][...] += _nn(w[0][...], vb)
            _sb_second(diag, z[1], cin[1], w[1], car[1])
            acc[1][...] += _nn(w[1][...], vb)
            return _sb_alive(carry_scr)

        def step(n, alive):
            return lax.cond(alive > 0, lambda: live(i - n, False), lambda: alive)

        lax.fori_loop(1, i + 1, step, live(i, True))
        both = jnp.where(masks[0], acc_scr[0], acc_scr[1])
        o_ref[0] = both.astype(o_ref.dtype)
        of_ref[0] = both
        ride_wait()

    p3 = proj.reshape(bl, s, width)
    qblk = pl.BlockSpec((1, SB_BLOCK, LANES), lambda b, p, i: (b, i, p))
    extra = ride.arrays if ride else []
    o, of, *delivered = pl.pallas_call(
        body, name=name, grid=grid,
        in_specs=[qblk, pl.BlockSpec((1, s, LANES), lambda b, p, i: (b, 0, npair + p)),
                  pl.BlockSpec((1, s, LANES), lambda b, p, i: (b, 0, 2 * npair + p))] + [ANY] * len(extra),
        out_specs=[qblk, qblk] + [ANY] * (len(ride.out_shape) if ride else 0),
        out_shape=[jax.ShapeDtypeStruct((bl, s, SB_WIDTH), BF), jax.ShapeDtypeStruct((bl, s, SB_WIDTH), F32)]
        + (ride.out_shape if ride else []),
        scratch_shapes=own_scratch + (ride.scratch if ride else []),
        compiler_params=_params("arbitrary", "arbitrary", "arbitrary"),
    )(p3, p3, p3, *extra)
    return o.reshape(t, SB_WIDTH), of.reshape(t, SB_WIDTH), delivered


def sb_bwd(proj, o, dcat, bl, *, name, ride=None):
    t, width = proj.shape
    s = t // bl
    npair = SB_WIDTH // LANES
    nq = s // SB_BLOCK
    grid = (bl, npair, nq)
    own_scratch = _sb_scratch(backward=True)

    def body(*refs):
        (q_ref, k_ref, v_ref, o_ref, do_ref), (dq_ref, dk_ref, dv_ref), scr, ride_start, ride_wait = _ride_hooks(
            ride, refs, 5, 3, len(own_scratch), grid)
        (uu_scr, z_scr, sp_scr, cin_scr, w_scr, carry_scr,
         ls_scr, dw_scr, da_scr, dz_scr, carry_s_scr, dsum_scr, dq_scr) = scr
        ride_start()
        i = pl.program_id(2)

        @pl.when(i == 0)
        def _():
            dk_ref[...] = jnp.zeros_like(dk_ref)
            dv_ref[...] = jnp.zeros_like(dv_ref)

        uu_scr[...] = _suffix_ones()
        for ref in (carry_scr, carry_s_scr, dq_scr):
            ref[...] = jnp.zeros_like(ref)
        masks = _lane_masks(SB_BLOCK)
        qs = (q_ref[0].astype(F32) * SCALE).astype(BF)
        do = do_ref[0]
        prod = do.astype(F32) * o_ref[0]
        for h in range(2):
            dsum_scr[h] = jnp.broadcast_to(jnp.sum(jnp.where(masks[h], prod, 0.0), axis=1, keepdims=True),
                                           (SB_BLOCK, LANES))

        z, sp, cin, w, car, ls, dw, da, dz, cars, dsum, dq = (
            [r.at[h] for h in range(2)] for r in (z_scr, sp_scr, cin_scr, w_scr, carry_scr, ls_scr, dw_scr, da_scr,
                                                  dz_scr, carry_s_scr, dsum_scr, dq_scr))

        def live(c, diag):
            start = pl.multiple_of(c * SB_BLOCK, SB_BLOCK)
            keys = pl.ds(start, SB_BLOCK)
            kb, vb = k_ref[0, keys, :], v_ref[0, keys, :]
            qh = [jnp.where(masks[h], qs, jnp.zeros_like(qs)) for h in range(2)]
            doh = [jnp.where(masks[h], do, jnp.zeros_like(do)) for h in range(2)]
            for h in range(2):
                z[h][...] = _nt(qh[h], kb)
                dw[h][...] = _nt(doh[h], vb)

            def weights(h):
                _sb_second(diag, z[h], cin[h], w[h], car[h])
                _sb_third(diag, w[h], dw[h], da[h], sp[h])
                _sb_suffix_sums(uu_scr, sp[h], cin[h])

            def grads(h):
                _sb_fourth(diag, z[h], ls[h], da[h], cin[h], dz[h], cars[h], dsum[h])
                dk_ref[0, keys, :] += _tn(dz[h][...], qh[h])
                dv_ref[0, keys, :] += _tn(w[h][...], doh[h])
                dq[h][...] += _nn(dz[h][...], kb)

            _sb_first(diag, z[0], sp[0], ls[0])
            _sb_suffix_sums(uu_scr, sp[0], cin[0])
            _sb_first(diag, z[1], sp[1], ls[1])
            _sb_suffix_sums(uu_scr, sp[1], cin[1])
            weights(0)
            weights(1)
            grads(0)
            grads(1)
            return _sb_alive(carry_scr)

        def step(n, alive):
            return lax.cond(alive > 0, lambda: live(i - n, False), lambda: alive)

        lax.fori_loop(1, i + 1, step, live(i, True))
        dq_ref[0] = (jnp.where(masks[0], dq_scr[0], dq_scr[1]) * SCALE).astype(dq_ref.dtype)
        ride_wait()

    p3 = proj.reshape(bl, s, width)
    o3 = o.reshape(bl, s, SB_WIDTH)
    d3 = dcat.reshape(bl, s, dcat.shape[1])
    qblk = pl.BlockSpec((1, SB_BLOCK, LANES), lambda b, p, i: (b, i, p))
    full = pl.BlockSpec((1, s, LANES), lambda b, p, i: (b, 0, p))
    extra = ride.arrays if ride else []
    dq, dk, dv, *delivered = pl.pallas_call(
        body, name=name, grid=grid,
        in_specs=[qblk, pl.BlockSpec((1, s, LANES), lambda b, p, i: (b, 0, npair + p)),
                  pl.BlockSpec((1, s, LANES), lambda b, p, i: (b, 0, 2 * npair + p)), qblk, qblk] + [ANY] * len(extra),
        out_specs=[qblk, full, full] + [ANY] * (len(ride.out_shape) if ride else 0),
        out_shape=[jax.ShapeDtypeStruct((bl, s, SB_WIDTH), BF), jax.ShapeDtypeStruct((bl, s, SB_WIDTH), F32),
                   jax.ShapeDtypeStruct((bl, s, SB_WIDTH), F32)] + (ride.out_shape if ride else []),
        scratch_shapes=own_scratch + (ride.scratch if ride else []),
        compiler_params=_params("arbitrary", "arbitrary", "arbitrary"),
    )(p3, p3, p3, o3, d3, *extra)
    return dq.reshape(t, SB_WIDTH), dk.reshape(t, SB_WIDTH), dv.reshape(t, SB_WIDTH), delivered


MEM_TQ = 512


def mem_fwd(proj, kvm, bl, *, name):
    t, width = proj.shape
    s = t // bl
    qoff = (width - MEM_WIDTH) // LANES
    npair = MEM_WIDTH // LANES

    def body(q_ref, k_ref, v_ref, o_ref):
        ma, mb = _lane_masks(MEM_TQ)
        q = q_ref[...].astype(BF)
        outs = []
        for m in (ma, mb):
            qh = jnp.where(m, q, jnp.zeros_like(q))
            sc = _nt(qh, k_ref[...]) * SCALE
            p = jnp.exp(sc - jnp.max(sc, axis=-1, keepdims=True))
            p = p * (1.0 / jnp.sum(p, axis=-1, keepdims=True))
            outs.append(_nn(p.astype(BF), v_ref[...]))
        o_ref[...] = jnp.where(ma, outs[0], outs[1]).astype(o_ref.dtype)

    nt = s // MEM_TQ
    return pl.pallas_call(
        body, name=name, grid=(bl, npair, nt),
        in_specs=[pl.BlockSpec((MEM_TQ, LANES), lambda b, p, i: (b * nt + i, qoff + p)),
                  pl.BlockSpec((MEM_LEN, LANES), lambda b, p, i: (b, p)),
                  pl.BlockSpec((MEM_LEN, LANES), lambda b, p, i: (b, npair + p))],
        out_specs=pl.BlockSpec((MEM_TQ, LANES), lambda b, p, i: (b * nt + i, p)),
        out_shape=jax.ShapeDtypeStruct((t, MEM_WIDTH), BF),
        compiler_params=_params("parallel", "parallel", "parallel"),
    )(proj, kvm, kvm)


def mem_bwd(proj, kvm, dcat, bl, *, name):
    t, width = proj.shape
    s = t // bl
    qoff = (width - MEM_WIDTH) // LANES
    doff = (dcat.shape[1] - MEM_WIDTH) // LANES
    npair = MEM_WIDTH // LANES
    nt = s // MEM_TQ

    def body(q_ref, k_ref, v_ref, do_ref, dq_ref, dk_ref, dv_ref):
        @pl.when(pl.program_id(2) == 0)
        def _():
            dk_ref[...] = jnp.zeros_like(dk_ref)
            dv_ref[...] = jnp.zeros_like(dv_ref)

        ma, mb = _lane_masks(MEM_TQ)
        q = q_ref[...].astype(BF)
        do = do_ref[...]
        kb, vb = k_ref[...], v_ref[...]
        dqs = []
        for m in (ma, mb):
            qh = jnp.where(m, q, jnp.zeros_like(q))
            doh = jnp.where(m, do, jnp.zeros_like(do))
            sc = _nt(qh, kb) * SCALE
            p = jnp.exp(sc - jnp.max(sc, axis=-1, keepdims=True))
            p = p * (1.0 / jnp.sum(p, axis=-1, keepdims=True))
            dp = _nt(doh, vb)
            ds = p * (dp - jnp.sum(p * dp, axis=-1, keepdims=True))
            dss = (ds * SCALE).astype(BF)
            dk_ref[...] += _tn(dss, qh)
            dv_ref[...] += _tn(p.astype(BF), doh)
            dqs.append(_nn(dss, kb))
        dq_ref[...] = jnp.where(ma, dqs[0], dqs[1]).astype(dq_ref.dtype)

    kblk = pl.BlockSpec((MEM_LEN, LANES), lambda b, p, i: (b, p))
    dq, dk, dv = pl.pallas_call(
        body, name=name, grid=(bl, npair, nt),
        in_specs=[pl.BlockSpec((MEM_TQ, LANES), lambda b, p, i: (b * nt + i, qoff + p)), kblk,
                  pl.BlockSpec((MEM_LEN, LANES), lambda b, p, i: (b, npair + p)),
                  pl.BlockSpec((MEM_TQ, LANES), lambda b, p, i: (b * nt + i, doff + p))],
        out_specs=[pl.BlockSpec((MEM_TQ, LANES), lambda b, p, i: (b * nt + i, p)), kblk, kblk],
        out_shape=[jax.ShapeDtypeStruct((t, MEM_WIDTH), BF), jax.ShapeDtypeStruct((bl * MEM_LEN, MEM_WIDTH), F32),
                   jax.ShapeDtypeStruct((bl * MEM_LEN, MEM_WIDTH), F32)],
        compiler_params=_params("parallel", "parallel", "arbitrary"),
    )(proj, kvm, kvm, dcat)
    return dq, jnp.concatenate([dk, dv], axis=1).astype(BF)


def _dil_rows(r, u0, size, dil):
    if dil == 1:
        return pl.ds(_mo(u0, QB), size)
    return pl.ds(u0 * dil + r, size, stride=dil)


def _dil_walk(block, dil, nb):
    def residue(r, c):
        block(r, 0, QB)
        if nb > 1:
            def step(n, c2):
                block(r, n, 2 * QB)
                return c2
            lax.fori_loop(1, nb, step, 0, unroll=3)
        return c

    if dil == 1:
        residue(0, 0)
    else:
        lax.fori_loop(0, dil, residue, 0, unroll=4 if nb == 1 else 1)


def _dil_scores(qh, kb, slope_d, n_keys):
    i = lax.broadcasted_iota(jnp.int32, (QB, n_keys), 0)
    j = lax.broadcasted_iota(jnp.int32, (QB, n_keys), 1)
    delta = i + (n_keys - QB) - j
    valid = (delta >= 0) & (delta <= QB)
    sc = _nt(qh, kb) * SCALE - slope_d * delta.astype(F32)
    return jnp.where(valid, sc, NEG)


def _dil_slopes(g, dil):
    p = pl.program_id(1)
    sa = jnp.where(p == 0, ALIBI[4 * g] * dil, ALIBI[4 * g + 2] * dil).astype(F32)
    sb = jnp.where(p == 0, ALIBI[4 * g + 1] * dil, ALIBI[4 * g + 3] * dil).astype(F32)
    return sa, sb


def dil_fwd(projb, kv, g, bl, *, name):
    _, dil = DIL_GROUPS[g]
    t, wq = projb.shape
    wk = kv.shape[1]
    s = t // bl
    ln = s // dil
    nb = ln // QB
    gw = 4 * HEAD_DIM
    ck, co = wk // LANES, gw // LANES

    def body(q_ref, k_ref, v_ref, o_ref, lse_ref):
        sa, sb = _dil_slopes(g, dil)
        ma, mb = _lane_masks(QB)

        def block(r, n, n_keys):
            q0 = n * QB
            k0 = q0 - (n_keys - QB)
            q = q_ref[0, _dil_rows(r, q0, QB, dil), :].astype(BF)
            kb = k_ref[0, _dil_rows(r, k0, n_keys, dil), :].astype(BF)
            vb = v_ref[0, _dil_rows(r, k0, n_keys, dil), :].astype(BF)
            outs, lses = [], []
            for m, sl in ((ma, sa), (mb, sb)):
                qh = jnp.where(m, q, jnp.zeros_like(q))
                sc = _dil_scores(qh, kb, sl, n_keys)
                mx = jnp.max(sc, axis=-1, keepdims=True)
                p = jnp.exp(sc - mx)
                den = jnp.sum(p, axis=-1, keepdims=True)
                outs.append(_nn(p.astype(BF), vb) * (1.0 / den))
                lses.append(mx + jnp.log(den))
            o_ref[0, _dil_rows(r, q0, QB, dil), :] = jnp.where(ma, outs[0], outs[1])
            lse_ref[0, _dil_rows(r, q0, QB, dil), :] = jnp.where(ma, lses[0], lses[1])

        _dil_walk(block, dil, nb)

    colblk = lambda off: pl.BlockSpec((1, s, LANES), lambda b, p: (b, 0, off + p))
    o, lse = pl.pallas_call(
        body, name=name, grid=(bl, co),
        in_specs=[colblk(g * co), colblk(g * co), colblk(ck // 2 + g * co)],
        out_specs=[colblk(0), colblk(0)],
        out_shape=[jax.ShapeDtypeStruct((bl, s, gw), F32), jax.ShapeDtypeStruct((bl, s, gw), F32)],
        compiler_params=_params("parallel", "parallel"),
    )(projb.reshape(bl, s, wq), kv.reshape(bl, s, wk), kv.reshape(bl, s, wk))
    return o.reshape(t, gw), lse.reshape(t, gw)


def dil_bwd(projb, kv, lse, dog, dshift, g, bl, *, name):
    _, dil = DIL_GROUPS[g]
    t, wq = projb.shape
    wk = kv.shape[1]
    s = t // bl
    ln = s // dil
    nb = ln // QB
    gw = 4 * HEAD_DIM
    ck, co = wk // LANES, gw // LANES

    def body(q_ref, k_ref, v_ref, lse_ref, do_ref, sh_ref, dq_ref, dk_ref, dv_ref):
        sa, sb = _dil_slopes(g, dil)
        ma, mb = _lane_masks(QB)
        dk_ref[...] = jnp.zeros_like(dk_ref)
        dv_ref[...] = jnp.zeros_like(dv_ref)

        def block(r, n, n_keys):
            q0 = n * QB
            k0 = q0 - (n_keys - QB)
            qrows, krows = _dil_rows(r, q0, QB, dil), _dil_rows(r, k0, n_keys, dil)
            q = q_ref[0, qrows, :].astype(BF)
            do = do_ref[0, qrows, :].astype(BF)
            lse_b = lse_ref[0, qrows, :]
            sh_b = sh_ref[0, qrows, :]
            kb = k_ref[0, krows, :].astype(BF)
            vb = v_ref[0, krows, :].astype(BF)
            dqs = []
            for m, sl, c0 in ((ma, sa, 0), (mb, sb, HEAD_DIM)):
                qh = jnp.where(m, q, jnp.zeros_like(q))
                doh = jnp.where(m, do, jnp.zeros_like(do))
                sc = _dil_scores(qh, kb, sl, n_keys)
                p = jnp.exp(sc - lse_b[:, c0:c0 + 1])
                ds = p * (_nt(doh, vb) - sh_b[:, c0:c0 + 1])
                dss = (ds * SCALE).astype(BF)
                dk_ref[0, krows, :] += _tn(dss, qh)
                dv_ref[0, krows, :] += _tn(p.astype(BF), doh)
                dqs.append(_nn(dss, kb))
            dq_ref[0, qrows, :] = jnp.where(ma, dqs[0], dqs[1])

        _dil_walk(block, dil, nb)

    colblk = lambda off: pl.BlockSpec((1, s, LANES), lambda b, p: (b, 0, off + p))
    dq, dk, dv = pl.pallas_call(
        body, name=name, grid=(bl, co),
        in_specs=[colblk(g * co), colblk(g * co), colblk(ck // 2 + g * co), colblk(0), colblk(g * co), colblk(g * co)],
        out_specs=[colblk(0)] * 3,
        out_shape=[jax.ShapeDtypeStruct((bl, s, gw), F32)] * 3,
        compiler_params=_params("parallel", "parallel"),
    )(projb.reshape(bl, s, wq), kv.reshape(bl, s, wk), kv.reshape(bl, s, wk), lse.reshape(bl, s, gw),
      dog.reshape(bl, s, DIL_WIDTH), dshift.reshape(bl, s, DIL_WIDTH))
    return dq.reshape(t, gw), dk.reshape(t, gw), dv.reshape(t, gw)


def _group_weights(lses):
    mx = jnp.maximum(jnp.maximum(lses[0], lses[1]), lses[2])
    es = [jnp.exp(l - mx) for l in lses]
    inv = 1.0 / (es[0] + es[1] + es[2])
    return [e * inv for e in es]


def dil_combine_fwd(os_, lses, *, name):
    t, gw = os_[0].shape
    tr = _pick(t, (512, 256))

    def body(o0, o1, o2, l0, l1, l2, out_ref):
        al = _group_weights([l0[...], l1[...], l2[...]])
        for g, o_ref in enumerate((o0, o1, o2)):
            out_ref[:, g * gw:(g + 1) * gw] = (o_ref[...] * al[g]).astype(out_ref.dtype)

    blk = pl.BlockSpec((tr, gw), lambda i: (i, 0))
    return pl.pallas_call(
        body, name=name, grid=(t // tr,), in_specs=[blk] * 6,
        out_specs=pl.BlockSpec((tr, 3 * gw), lambda i: (i, 0)), out_shape=jax.ShapeDtypeStruct((t, 3 * gw), BF),
        compiler_params=_params("parallel"),
    )(*os_, *lses)


def dil_combine_bwd(os_, lses, dcat, *, name):
    t, gw = os_[0].shape
    tr = _pick(t, (512, 256))

    def head_sum(v):
        parts = []
        for c in range(gw // LANES):
            blk = v[:, c * LANES:(c + 1) * LANES]
            ma, _ = _lane_masks(tr)
            sa = jnp.sum(jnp.where(ma, blk, 0.0), axis=1, keepdims=True)
            sb = jnp.sum(blk, axis=1, keepdims=True) - sa
            parts.append(jnp.where(ma, sa, sb))
        return jnp.concatenate(parts, axis=1)

    def body(o0, o1, o2, l0, l1, l2, d_ref, dog_ref, sh_ref):
        al = _group_weights([l0[...], l1[...], l2[...]])
        dos = [d_ref[:, g * gw:(g + 1) * gw].astype(F32) for g in range(3)]
        dal = [head_sum(dos[g] * o_ref[...]) for g, o_ref in enumerate((o0, o1, o2))]
        mix = al[0] * dal[0] + al[1] * dal[1] + al[2] * dal[2]
        for g in range(3):
            dog_ref[:, g * gw:(g + 1) * gw] = (al[g] * dos[g]).astype(dog_ref.dtype)
            sh_ref[:, g * gw:(g + 1) * gw] = al[g] * mix

    blk = pl.BlockSpec((tr, gw), lambda i: (i, 0))
    wide = pl.BlockSpec((tr, 3 * gw), lambda i: (i, 0))
    return pl.pallas_call(
        body, name=name, grid=(t // tr,), in_specs=[blk] * 6 + [wide], out_specs=[wide, wide],
        out_shape=[jax.ShapeDtypeStruct((t, 3 * gw), F32), jax.ShapeDtypeStruct((t, 3 * gw), F32)],
        compiler_params=_params("parallel"),
    )(*os_, *lses, dcat)


def adamw(w, g1, g2, m, v, *, name):
    r, c = w.shape
    tr = r
    for cand in (256, 128, 64, 32, 16, 8):
        if r % cand == 0 and cand * c * 4 <= (1 << 20):
            tr = cand
            break
    two = g2 is not None

    def body(*refs):
        if two:
            w_ref, g1_ref, g2_ref, m_ref, v_ref, g_ref, d_ref, nm_ref, nv_ref = refs
            g = g1_ref[...] + g2_ref[...]
        else:
            w_ref, g1_ref, m_ref, v_ref, g_ref, d_ref, nm_ref, nv_ref = refs
            g = g1_ref[...]
        nm = ADAM_B1 * m_ref[...] + (1.0 - ADAM_B1) * g
        nv = ADAM_B2 * v_ref[...] + (1.0 - ADAM_B2) * (g * g)
        m_hat = nm / (1.0 - ADAM_B1 ** ADAM_STEP)
        v_hat = nv / (1.0 - ADAM_B2 ** ADAM_STEP)
        g_ref[...] = g
        d_ref[...] = -ADAM_LR * (m_hat / (jnp.sqrt(v_hat) + ADAM_EPS) + ADAM_WD * w_ref[...])
        nm_ref[...] = nm
        nv_ref[...] = nv

    blk = pl.BlockSpec((tr, c), lambda i: (i, 0))
    args = [w, g1] + ([g2] if two else []) + [m, v]
    return pl.pallas_call(
        body, name=name, grid=(r // tr,), in_specs=[blk] * len(args), out_specs=[blk] * 4,
        out_shape=[jax.ShapeDtypeStruct((r, c), F32)] * 4, compiler_params=_params("parallel"),
    )(*args)


def sum4(own, land, *, name):
    r, c = own.shape
    tr = _pick(r, (256, 128, 64))

    def body(o_ref, l_ref, s_ref):
        s_ref[...] = ((o_ref[...].astype(F32) + l_ref[0].astype(F32)) + l_ref[1].astype(F32)) + l_ref[2].astype(F32)

    return pl.pallas_call(
        body, name=name, grid=(r // tr,),
        in_specs=[pl.BlockSpec((tr, c), lambda i: (i, 0)), pl.BlockSpec((3, tr, c), lambda i: (0, i, 0))],
        out_specs=pl.BlockSpec((tr, c), lambda i: (i, 0)), out_shape=jax.ShapeDtypeStruct((r, c), F32),
        compiler_params=_params("parallel"),
    )(own, land)


ANY = pl.BlockSpec(memory_space=pl.ANY)


def _place():
    x, y, c = lax.axis_index("x"), lax.axis_index("y"), lax.axis_index("c")
    chips = [(1 - x, y), (x, 1 - y), (1 - x, 1 - y)]
    return x, y, c, chips


class Ride:
    def __init__(self, arrays, out_shape, copies):
        self.arrays, self.out_shape, self.copies = list(arrays), list(out_shape), copies
        n = 3 * len(self.arrays)
        self.scratch = [pltpu.SemaphoreType.DMA((n,)), pltpu.SemaphoreType.DMA((n,)),
                        pltpu.SemaphoreType.DMA((len(self.arrays),))]

    def split(self, refs):
        n, m = len(self.arrays), len(self.out_shape)
        return refs[:n], refs[n:n + m], refs[n + m:]

    def start(self, ins, outs, sems):
        local, sends, _ = self.copies(ins, outs, *sems)
        for cp in local + sends:
            cp.start()

    def wait(self, ins, outs, sems):
        local, sends, arrivals = self.copies(ins, outs, *sems)
        for cp in arrivals:
            cp.wait_recv()
        for cp in sends:
            cp.wait_send()
        for cp in local:
            cp.wait()

    def run(self, name):
        def body(*refs):
            ins, outs, sems = self.split(refs)
            self.start(ins, outs, sems)
            self.wait(ins, outs, sems)

        return pl.pallas_call(body, name=name, in_specs=[ANY] * len(self.arrays), out_specs=[ANY] * len(self.out_shape),
                              out_shape=self.out_shape, scratch_shapes=self.scratch)(*self.arrays)


def gather_ride(shards, axes):
    def copies(ins, outs, send_sems, recv_sems, local_sems):
        x, y, c, chips = _place()

        def slot(a, q):
            size = shards[a].shape[axes[a]]
            start = pl.multiple_of(q * size, size)
            return outs[a].at[pl.ds(start, size), :] if axes[a] == 0 else outs[a].at[:, pl.ds(start, size)]

        def remote(a, k, q):
            px, py = chips[k]
            return pltpu.make_async_remote_copy(src_ref=ins[a], dst_ref=slot(a, q), send_sem=send_sems.at[3 * a + k],
                                                recv_sem=recv_sems.at[3 * a + k], device_id=(px, py, c), device_id_type=MESH)

        me = 2 * x + y
        n = len(shards)
        local = [pltpu.make_async_copy(ins[a], slot(a, me), local_sems.at[a]) for a in range(n)]
        sends = [remote(a, k, me) for a in range(n) for k in range(3)]
        arrivals = [remote(a, k, 2 * chips[k][0] + chips[k][1]) for a in range(n) for k in range(3)]
        return local, sends, arrivals

    out_shape = []
    for a, sh in enumerate(shards):
        full = list(sh.shape)
        full[axes[a]] *= N_CHIPS
        out_shape.append(jax.ShapeDtypeStruct(tuple(full), sh.dtype))
    return Ride(shards, out_shape, copies)


def scatter_ride(grads):
    def copies(ins, outs, send_sems, recv_sems, local_sems, base=0):
        x, y, c, chips = _place()
        sends = [pltpu.make_async_remote_copy(src_ref=ins[a].at[2 * px + py], dst_ref=outs[a].at[k],
                                              send_sem=send_sems.at[3 * (base + a) + k],
                                              recv_sem=recv_sems.at[3 * (base + a) + k],
                                              device_id=(px, py, c), device_id_type=MESH)
                 for a in range(len(grads)) for k, (px, py) in enumerate(chips)]
        return [], sends, sends

    return Ride(grads, [jax.ShapeDtypeStruct((3,) + g.shape[1:], g.dtype) for g in grads], copies)


def swap_ride(arrs):
    def copies(ins, outs, send_sems, recv_sems, local_sems, base=0):
        x, y, c, _ = _place()
        sends = [pltpu.make_async_remote_copy(src_ref=ins[a], dst_ref=outs[a], send_sem=send_sems.at[3 * (base + a)],
                                              recv_sem=recv_sems.at[3 * (base + a)], device_id=(x, y, 1 - c),
                                              device_id_type=MESH)
                 for a in range(len(arrs))]
        return [], sends, sends

    return Ride(arrs, [jax.ShapeDtypeStruct(a.shape, a.dtype) for a in arrs], copies)


def merge_rides(first, second):
    n1, m1 = len(first.arrays), len(first.out_shape)

    def copies(ins, outs, send_sems, recv_sems, local_sems, base=0):
        parts = [first.copies(ins[:n1], outs[:m1], send_sems, recv_sems, local_sems, base),
                 second.copies(ins[n1:], outs[m1:], send_sems, recv_sems, local_sems, base + n1)]
        return tuple(parts[0][k] + parts[1][k] for k in range(3))

    return Ride(first.arrays + second.arrays, first.out_shape + second.out_shape, copies)


def all_reduce_small(v, *, name):
    rows = v.shape[0]

    def body(v_ref, o_ref, gath, send_sems, recv_sems):
        x, y, c, _ = _place()
        me = 4 * x + 2 * y + c
        gath[me] = v_ref[...]
        sends = []
        for msk in range(1, N_DEV):
            peer = (x ^ (msk >> 2), y ^ ((msk >> 1) & 1), c ^ (msk & 1))
            cp = pltpu.make_async_remote_copy(src_ref=v_ref, dst_ref=gath.at[me], send_sem=send_sems.at[msk - 1],
                                              recv_sem=recv_sems.at[msk - 1], device_id=peer, device_id_type=MESH)
            cp.start()
            sends.append(cp)
        for msk in range(1, N_DEV):
            pltpu.make_async_remote_copy(src_ref=v_ref, dst_ref=gath.at[me ^ msk], send_sem=send_sems.at[msk - 1],
                                         recv_sem=recv_sems.at[msk - 1], device_id=(x, y, c), device_id_type=MESH).wait_recv()
        for cp in sends:
            cp.wait_send()
        tot = gath[0]
        for q in range(1, N_DEV):
            tot = tot + gath[q]
        o_ref[...] = tot

    vm = pl.BlockSpec(memory_space=pltpu.VMEM)
    return pl.pallas_call(
        body, name=name, in_specs=[vm], out_specs=vm, out_shape=jax.ShapeDtypeStruct(v.shape, F32),
        scratch_shapes=[pltpu.VMEM((N_DEV, rows, LANES), F32), pltpu.SemaphoreType.DMA((N_DEV - 1,)),
                        pltpu.SemaphoreType.DMA((N_DEV - 1,))],
    )(v)


def _ffn_fwd(xin, gain, w_up, w_conv, w_down, bl, tag, rides=(None, None, None)):
    got = {}

    def carried(k, result, delivered):
        if rides[k] is not None:
            got.update(zip(rides[k][1], delivered))
        return result

    n = rmsnorm_fwd(xin, gain, name=f"{tag}_ffn_norm")
    if rides[0] is None:
        u = matmul(n, w_up, name=f"{tag}_ffn_up")
    else:
        u = carried(0, *matmul(n, w_up, ride=rides[0][0], name=f"{tag}_ffn_up"))
    *hc, delivered = conv_silu_fwd(u, w_conv, bl, name=f"{tag}_ffn_conv", ride=rides[1][0] if rides[1] else None)
    h, ca, cg = carried(1, hc, delivered)
    if rides[2] is None:
        xout = matmul(h, w_down, out_dtype=F32, add=xin, name=f"{tag}_ffn_down")
    else:
        xout = carried(2, *matmul(h, w_down, out_dtype=F32, add=xin, ride=rides[2][0], name=f"{tag}_ffn_down"))
    return xout, (n, u, h, ca, cg), got


def _ffn_bwd(dxout, dxout_b, xin, saved, gain, w_up, w_conv, w_down, bl, tag, shard):
    n, u, h, ca, cg = saved
    dh = matmul(dxout_b, w_down, tb=True, name=f"{tag}_ffn_down_dx")
    g_down = matmul(h, dxout_b, ta=True, name=f"{tag}_ffn_down_dw")
    dua, dug, g_conv = conv_silu_bwd(u, ca, cg, w_conv, dh, bl, name=f"{tag}_ffn_conv_bwd")
    dn = matmul(dua, w_up, a2=dug, tb=True, out_dtype=F32, name=f"{tag}_ffn_up_dx")
    g_up = matmul(n, dua, b2=dug, ta=True, shard_cols=shard, name=f"{tag}_ffn_up_dw")
    dxin, dxin_b, g_norm = rmsnorm_bwd(xin, gain, dn, dxout, name=f"{tag}_ffn_norm_bwd")
    return dxin, dxin_b, g_norm, g_up, g_conv, g_down


def _mem_kv(mem2, gain, w_kv, tag):
    mn = rmsnorm_fwd(mem2, gain, name=f"{tag}_mem_norm")
    return mn, matmul(mn, w_kv, name=f"{tag}_mem_kv")


def _mem_kv_bwd(mem2, mn, gain, w_kv, dkvm, tag):
    g_kv = matmul(mn, dkvm, ta=True, name=f"{tag}_mem_kv_dw")
    dmn = matmul(dkvm, w_kv, tb=True, out_dtype=F32, name=f"{tag}_mem_kv_dx")
    _, _, g_norm = rmsnorm_bwd(mem2, gain, dmn, None, name=f"{tag}_mem_norm_bwd")
    return g_norm, g_kv


def _shard_major(name, grad):
    if name in COL_SHARDED:
        return grad
    return grad.reshape(N_CHIPS, grad.shape[0] // N_CHIPS, grad.shape[1])


def local_step(x, mem, target, w, shard_of, late_gather=None, early_scatter=(), scatter_last=(), reduce_early=None):
    bl, s, d = x.shape
    t = bl * s
    x0 = x.reshape(t, d)
    mem2 = mem.reshape(bl * MEM_LEN, d)
    tgt = target.reshape(t, d)
    g = {}

    n1 = rmsnorm_fwd(x0, w["a_norm_attn"], name="a_attn_norm")
    proj_a = matmul(n1, w["a_w_in"], name="a_w_in")
    mn_a, kvm_a = _mem_kv(mem2, w["a_norm_mem"], w["a_w_mem_kv"], "a")
    late_gather = late_gather or {}
    behind_sb = late_gather.get("sb")
    o_sb, o_sb_f32, delivered = sb_fwd(proj_a, bl, name="a_sb_fwd", ride=behind_sb[0] if behind_sb else None)
    if behind_sb:
        w = {**w, **dict(zip(behind_sb[1], delivered))}
    o_mem_a = mem_fwd(proj_a, kvm_a, bl, name="a_mem_fwd")
    cat_a = jnp.concatenate([o_sb, o_mem_a], axis=1)
    x1 = matmul(cat_a, w["a_w_out"], out_dtype=F32, add=x0, name="a_w_out")
    x2, ffn_a, got = _ffn_fwd(x1, w["a_norm_ffn"], w["a_ffn_up"], w["a_ffn_conv"], w["a_ffn_down"], bl, "a",
                              rides=tuple(late_gather.get(host) for host in ("up", "conv", "down")))
    w = {**w, **got}

    nkv, n3 = rmsnorm_fwd2(x2, w["kv_norm"], w["b_norm_attn"], name="kv_b_attn_norm")
    kv = matmul(nkv, w["w_kv_shared"], out_dtype=F32, name="w_kv")
    proj_b = matmul(n3, w["b_w_in"], out_dtype=F32, name="b_w_in")
    mn_b, kvm_b = _mem_kv(mem2, w["b_norm_mem"], w["b_w_mem_kv"], "b")
    dil = [dil_fwd(proj_b, kv, gi, bl, name=f"b_dil_fwd{gi}") for gi in range(3)]
    os_, lses = [o for o, _ in dil], [l for _, l in dil]
    o_dil = dil_combine_fwd(os_, lses, name="b_dil_combine")
    o_mem_b = mem_fwd(proj_b, kvm_b, bl, name="b_mem_fwd")
    cat_b = jnp.concatenate([o_dil, o_mem_b], axis=1)
    x3 = matmul(cat_b, w["b_w_out"], out_dtype=F32, add=x2, name="b_w_out")
    x4, ffn_b, _ = _ffn_fwd(x3, w["b_norm_ffn"], w["b_ffn_up"], w["b_ffn_conv"], w["b_ffn_down"], bl, "b")

    dx4, dx4b, g["final_norm"], lossvec = final_loss(x4, w["final_norm"], tgt, name="final_loss")

    dx3, dx3b, g["b_norm_ffn"], g["b_ffn_up"], g["b_ffn_conv"], g["b_ffn_down"] = _ffn_bwd(
        dx4, dx4b, x3, ffn_b, w["b_norm_ffn"], w["b_ffn_up"], w["b_ffn_conv"], w["b_ffn_down"], bl, "b",
        shard_of["b_ffn_up"])
    dcat_b = matmul(dx3b, w["b_w_out"], tb=True, name="b_w_out_dx")
    g["b_w_out"] = matmul(cat_b, dx3b, ta=True, name="b_w_out_dw")
    dog, dshift = dil_combine_bwd(os_, lses, dcat_b, name="b_dil_combine_bwd")
    dqs, dks, dvs = [], [], []
    for gi in range(3):
        dq_g, dk_g, dv_g = dil_bwd(proj_b, kv, lses[gi], dog, dshift, gi, bl, name=f"b_dil_bwd{gi}")
        dqs.append(dq_g), dks.append(dk_g), dvs.append(dv_g)
    dq_mem_b, dkvm_b = mem_bwd(proj_b, kvm_b, dcat_b, bl, name="b_mem_bwd")
    g["b_norm_mem"], g["b_w_mem_kv"] = _mem_kv_bwd(mem2, mn_b, w["b_norm_mem"], w["b_w_mem_kv"], dkvm_b, "b")
    dproj_b = jnp.concatenate([dq_g.astype(BF) for dq_g in dqs] + [dq_mem_b], axis=1)
    dn3 = matmul(dproj_b, w["b_w_in"], tb=True, out_dtype=F32, name="b_w_in_dx")
    g["b_w_in"] = matmul(n3, dproj_b, ta=True, name="b_w_in_dw")
    dkv = jnp.concatenate(dks + dvs, axis=1).astype(BF)
    dnkv = matmul(dkv, w["w_kv_shared"], tb=True, out_dtype=F32, name="w_kv_dx")
    g["w_kv_shared"] = matmul(nkv, dkv, ta=True, shard_cols=shard_of["w_kv_shared"], name="w_kv_dw")
    dx2, dx2b, g["kv_norm"], g["b_norm_attn"] = rmsnorm_bwd2(x2, w["kv_norm"], w["b_norm_attn"], dnkv, dn3, dx3,
                                                             name="kv_b_attn_norm_bwd")

    dx1, dx1b, g["a_norm_ffn"], g["a_ffn_up"], g["a_ffn_conv"], g["a_ffn_down"] = _ffn_bwd(
        dx2, dx2b, x1, ffn_a, w["a_norm_ffn"], w["a_ffn_up"], w["a_ffn_conv"], w["a_ffn_down"], bl, "a",
        shard_of["a_ffn_up"])
    dcat_a = matmul(dx1b, w["a_w_out"], tb=True, name="a_w_out_dx")
    g["a_w_out"] = matmul(cat_a, dx1b, ta=True, name="a_w_out_dw")
    leaving = scatter_ride([_shard_major(n, g[n]) for n in early_scatter]) if early_scatter else None
    dq_sb, dk_sb, dv_sb, landed = sb_bwd(proj_a, o_sb_f32, dcat_a, bl, name="a_sb_bwd", ride=leaving)
    landed = dict(zip(early_scatter, landed))
    dq_mem_a, dkvm_a = mem_bwd(proj_a, kvm_a, dcat_a, bl, name="a_mem_bwd")
    g["a_norm_mem"], g["a_w_mem_kv"] = _mem_kv_bwd(mem2, mn_a, w["a_norm_mem"], w["a_w_mem_kv"], dkvm_a, "a")
    dproj_a = jnp.concatenate([dq_sb, dk_sb.astype(BF), dv_sb.astype(BF), dq_mem_a], axis=1)
    if reduce_early is None:
        dn1 = matmul(dproj_a, w["a_w_in"], tb=True, out_dtype=F32, name="a_w_in_dx")
        g["a_w_in"] = matmul(n1, dproj_a, ta=True, shard_cols=shard_of["a_w_in"], name="a_w_in_dw")
        dx0, _, g["a_norm_attn"] = rmsnorm_bwd(x0, w["a_norm_attn"], dn1, dx1, name="a_attn_norm_bwd")
        return lossvec, dx0, g, landed
    sums = reduce_early(g, landed)
    g["a_w_in"] = matmul(n1, dproj_a, ta=True, shard_cols=shard_of["a_w_in"], name="a_w_in_dw")
    dn1, delivered = matmul(dproj_a, w["a_w_in"], tb=True, out_dtype=F32, name="a_w_in_dx",
                            ride=merge_rides(swap_ride([sums[n] for n in early_scatter]),
                                             scatter_ride([_shard_major(n, g[n]) for n in scatter_last])))
    theirs, landed_last = delivered[:len(early_scatter)], delivered[len(early_scatter):]
    dx0, _, g["a_norm_attn"] = rmsnorm_bwd(x0, w["a_norm_attn"], dn1, dx1, name="a_attn_norm_bwd")
    return lossvec, dx0, g, (sums, dict(zip(early_scatter, theirs)), dict(zip(scatter_last, landed_last)))


MATRICES = ("a_w_in", "a_w_out", "a_w_mem_kv", "a_ffn_up", "a_ffn_down", "w_kv_shared", "b_w_in", "b_w_out",
            "b_w_mem_kv", "b_ffn_up", "b_ffn_down")
COL_SHARDED = ("a_w_in", "a_ffn_up", "w_kv_shared", "b_ffn_up")
FIRST_NEEDED = ("a_w_in", "a_w_mem_kv")
GATHER_BEHIND = {"sb": ("a_w_out", "a_ffn_up", "a_ffn_down", "b_ffn_up"), "up": ("b_ffn_down", "w_kv_shared"),
                 "conv": ("b_w_in", "b_w_out", "b_w_mem_kv")}
SMALL_SHARDED = ("a_norm_attn", "a_norm_mem", "a_norm_ffn", "a_ffn_conv", "b_ffn_conv")
SMALL_REPLICATED = ("kv_norm", "b_norm_attn", "b_norm_mem", "b_norm_ffn", "final_norm")
WEIGHTS = ("a_norm_attn", "a_w_in", "a_w_out", "a_norm_mem", "a_w_mem_kv", "a_norm_ffn", "a_ffn_up", "a_ffn_conv",
           "a_ffn_down", "kv_norm", "w_kv_shared", "b_norm_attn", "b_w_in", "b_w_out", "b_norm_mem", "b_w_mem_kv",
           "b_norm_ffn", "b_ffn_up", "b_ffn_conv", "b_ffn_down", "final_norm")


def _two_d(a):
    if a.ndim == 1:
        return a.reshape(1, -1)
    return a.reshape(a.shape[-2], a.shape[-1])


def kernel(x, mem, a_norm_attn, a_w_in, a_w_out, a_norm_mem, a_w_mem_kv, a_norm_ffn, a_ffn_up, a_ffn_conv, a_ffn_down, kv_norm, w_kv_shared, b_norm_attn, b_w_in, b_w_out, b_norm_mem, b_w_mem_kv, b_norm_ffn, b_ffn_up, b_ffn_conv, b_ffn_down, final_norm, loss_target, m_a_norm_attn, m_a_w_in, m_a_w_out, m_a_norm_mem, m_a_w_mem_kv, m_a_norm_ffn, m_a_ffn_up, m_a_ffn_conv, m_a_ffn_down, m_kv_norm, m_w_kv_shared, m_b_norm_attn, m_b_w_in, m_b_w_out, m_b_norm_mem, m_b_w_mem_kv, m_b_norm_ffn, m_b_ffn_up, m_b_ffn_conv, m_b_ffn_down, m_final_norm, v_a_norm_attn, v_a_w_in, v_a_w_out, v_a_norm_mem, v_a_w_mem_kv, v_a_norm_ffn, v_a_ffn_up, v_a_ffn_conv, v_a_ffn_down, v_kv_norm, v_w_kv_shared, v_b_norm_attn, v_b_w_in, v_b_w_out, v_b_norm_mem, v_b_w_mem_kv, v_b_norm_ffn, v_b_ffn_up, v_b_ffn_conv, v_b_ffn_down, v_final_norm):
    given = dict(locals())
    wl = {n: _two_d(given[n]) for n in WEIGHTS}
    ml = {n: _two_d(given["m_" + n]) for n in WEIGHTS}
    vl = {n: _two_d(given["v_" + n]) for n in WEIGHTS}
    chip = 2 * lax.axis_index("x") + lax.axis_index("y")

    packed = jnp.concatenate([wl[n].reshape(-1, LANES) for n in SMALL_SHARDED], axis=0)
    axis_of = lambda n: 1 if n in COL_SHARDED else 0
    late = tuple(n for n in MATRICES if n not in FIRST_NEEDED)
    full = gather_ride([wl[n].astype(BF) for n in FIRST_NEEDED] + [packed],
                       [axis_of(n) for n in FIRST_NEEDED] + [0]).run("gather_first")
    w = dict(zip(FIRST_NEEDED, full[:-1]))
    late_gather = {host: (gather_ride([wl[n].astype(BF) for n in names], [axis_of(n) for n in names]), names)
                   for host, names in GATHER_BEHIND.items()}
    rows = packed.shape[0]
    per_chip = full[-1].reshape(N_CHIPS, rows, LANES)
    r0 = 0
    for n in SMALL_SHARDED:
        nr = wl[n].size // LANES
        piece = per_chip[:, r0:r0 + nr].reshape(N_CHIPS, wl[n].shape[0], wl[n].shape[1])
        w[n] = jnp.concatenate([piece[q] for q in range(N_CHIPS)], axis=1)
        r0 += nr
    for n in SMALL_REPLICATED:
        w[n] = wl[n]

    shard_of = {n: wl[n].shape[1] for n in COL_SHARDED}
    def core_sums(g, landed):
        return {n: sum4(lax.dynamic_index_in_dim(_shard_major(n, g[n]), chip, 0, keepdims=False), landed[n],
                        name=f"sum4_{n}") for n in landed}

    lossvec, dx0, g, (sums, theirs, landed_last) = local_step(
        x, mem, loss_target, w, shard_of, late_gather=late_gather, early_scatter=late, scatter_last=FIRST_NEEDED,
        reduce_early=core_sums)
    loss = lax.psum(0.5 * jnp.sum(lossvec) / x.shape[-1], ("x", "y", "c"))
    sums.update(core_sums(g, landed_last))
    theirs.update(zip(FIRST_NEEDED, swap_ride([sums[n] for n in FIRST_NEEDED]).run("swap_last")))
    out = {}
    for n in MATRICES:
        out[n] = adamw(wl[n], sums[n], theirs[n], ml[n], vl[n], name=f"adamw_{n}")

    small = SMALL_SHARDED + SMALL_REPLICATED
    flat = jnp.concatenate([g[n].reshape(-1, LANES) for n in small], axis=0)
    tot = all_reduce_small(flat, name="all_reduce_small")
    r0 = 0
    for n in small:
        nr = g[n].size // LANES
        gn = tot[r0:r0 + nr].reshape(g[n].shape)
        r0 += nr
        if n in SMALL_SHARDED:
            gn = lax.dynamic_slice_in_dim(gn, chip * wl[n].shape[1], wl[n].shape[1], axis=1)
        out[n] = adamw(wl[n], gn, None, ml[n], vl[n], name=f"adamw_{n}")

    res = [loss, dx0.reshape(x.shape)]
    for slot in range(4):
        res += [out[n][slot].reshape(given[n].shape) for n in WEIGHTS]
    return tuple(res)
```

```python
import functools
import math

import jax
import jax.numpy as jnp
from jax import lax
from jax.experimental import pallas as pl
from jax.experimental.pallas import tpu as pltpu

F32 = jnp.float32
BF = jnp.bfloat16
MESH = pl.DeviceIdType.MESH

HEAD_DIM = 64
LANES = 128
SB_WIDTH = 12 * HEAD_DIM
MEM_WIDTH = 4 * HEAD_DIM
DIL_WIDTH = 12 * HEAD_DIM
MEM_LEN = 256
DIL_GROUPS = ((128, 1), (512, 4), (2048, 16))
QB = 128
EPS = 1e-6
SCALE = HEAD_DIM ** -0.5
NEG = -1e30
ALIBI = tuple(2.0 ** (-8.0 * i / 12) for i in range(1, 13))
N_CHIPS = 4
N_DEV = 8

ADAM_LR, ADAM_B1, ADAM_B2, ADAM_EPS, ADAM_WD, ADAM_STEP = 0.001, 0.9, 0.999, 1e-08, 0.01, 10

VMEM_LIMIT = 52 * 1024 * 1024
MATMUL_VMEM_BUDGET = 44 * 1024 * 1024
MATMUL_MAX_TN = 2816
HBM_BYTES_PER_S = 3.0e12
VMEM_BYTES_PER_S = 6.0e12
STEP_OVERHEAD_S = 0.4e-6


def _mo(v, m):
    return v if isinstance(v, int) else pl.multiple_of(v, m)


def _pick(n, prefs):
    for t in prefs:
        if n % t == 0:
            return t
    return n


def _params(*sem):
    return pltpu.CompilerParams(dimension_semantics=sem, vmem_limit_bytes=VMEM_LIMIT)


def matmul(a, b, *, ta=False, tb=False, out_dtype=BF, add=None, shard_cols=0, a2=None, b2=None, ride=None, name):
    m, k = (a.shape[1], a.shape[0]) if ta else a.shape
    n = b.shape[0] if tb else b.shape[1]
    if a2 is not None:
        assert not ta and a2.shape == a.shape
        k *= 2
    if b2 is not None:
        assert not tb and b2.shape == b.shape
        n *= 2
    k_piece = a.shape[1] if a2 is not None else k
    n_piece = b.shape[1] if b2 is not None else n
    out_bytes = jnp.dtype(out_dtype).itemsize
    pieces_a, pieces_b = (2 if a2 is not None else 1), (2 if b2 is not None else 1)
    if shard_cols:
        tn_choices = [c * shard_cols for c in (4, 2, 1) if c * shard_cols <= MATMUL_MAX_TN and c <= 4 // pieces_b]
    else:
        tn_choices = [c for c in (2816, 2560, 1536, 1408, 1024, 512, 256, 128) if n_piece % c == 0]

    def vmem_bytes(tm_, tn_, tk_):
        blocks = (tm_ * tk_ * a.dtype.itemsize * pieces_a + tk_ * tn_ * b.dtype.itemsize * pieces_b
                  + tm_ * tn_ * out_bytes + (tm_ * tn_ * 4 if add is not None else 0))
        return 2 * blocks + (tm_ * tn_ * 4 if k // tk_ > 1 else 0) + tm_ * tn_ * 4

    def cost(c):
        tm_, tn_, tk_ = c
        steps = (m // tm_) * (n // tn_) * (k // tk_)
        hbm = (m * k * a.dtype.itemsize * (n // tn_) + k * n * b.dtype.itemsize * (m // tm_)
               + m * n * (out_bytes + (4 if add is not None else 0)))
        return hbm / HBM_BYTES_PER_S + steps * STEP_OVERHEAD_S + m * n * 8 * (k // tk_ - 1) / VMEM_BYTES_PER_S

    tm, tn, tk = min(((tm_, tn_, tk_) for tm_ in (1024, 512, 1408, 256, 128) if m % tm_ == 0 for tn_ in tn_choices
                      for tk_ in (2816, 2560, 2048, 1536, 1024, 512, 256, 128) if k_piece % tk_ == 0
                      if vmem_bytes(tm_, tn_, tk_) <= MATMUL_VMEM_BUDGET), key=cost)
    ns = tn // shard_cols if shard_cols else 1
    nk = k // tk
    nk1 = nk // 2
    nj1 = (n // tn) // 2
    dims = (((0,) if ta else (1,), (1,) if tb else (0,)), ((), ()))
    n_in = 2 + (a2 is not None) + (b2 is not None) + (add is not None)
    grid = (m // tm, n // tn, nk)

    def body(*refs):
        own_in, (o_ref,), scr, ride_start, ride_wait = _ride_hooks(ride, refs, n_in, 1, int(nk > 1), grid)
        ride_start()
        ins = list(own_in)
        a_ref, b_ref = ins.pop(0), ins.pop(0)
        a2_ref = ins.pop(0) if a2 is not None else None
        b2_ref = ins.pop(0) if b2 is not None else None
        add_ref = ins.pop(0) if add is not None else None

        a_tile, b_tile = a_ref[...], b_ref[...]
        if a2 is not None:
            a_tile = jnp.where(pl.program_id(2) < nk1, a_tile, a2_ref[...])
        if b2 is not None:
            b_tile = jnp.where(pl.program_id(1) < nj1, b_tile, b2_ref[...])
        part = lax.dot_general(a_tile.astype(BF), b_tile.astype(BF), dims, preferred_element_type=F32)

        def finish(r):
            if add is not None:
                r = r + add_ref[...]
            if shard_cols:
                for c in range(ns):
                    o_ref[c] = r[:, c * shard_cols:(c + 1) * shard_cols].astype(o_ref.dtype)
            else:
                o_ref[...] = r.astype(o_ref.dtype)

        if nk == 1:
            finish(part)
        else:
            acc_ref = scr[0]
            kk = pl.program_id(2)

            @pl.when(kk == 0)
            def _():
                acc_ref[...] = part

            @pl.when(kk > 0)
            def _():
                acc_ref[...] += part

            @pl.when(kk == nk - 1)
            def _():
                finish(acc_ref[...])
        ride_wait()

    a_spec = pl.BlockSpec((tk, tm), lambda i, j, q: (q, i)) if ta else pl.BlockSpec((tm, tk), lambda i, j, q: (i, q))
    b_spec = pl.BlockSpec((tn, tk), lambda i, j, q: (j, q)) if tb else pl.BlockSpec((tk, tn), lambda i, j, q: (q, j))
    if a2 is not None:
        a_spec = pl.BlockSpec((tm, tk), lambda i, j, q: (i, jnp.minimum(q, nk1 - 1)))
    if b2 is not None:
        b_spec = pl.BlockSpec((tk, tn), lambda i, j, q: (jnp.where(j < nj1, q, 0), jnp.minimum(j, nj1 - 1)))
    in_specs = [a_spec, b_spec]
    args = [a, b]
    if a2 is not None:
        in_specs.append(pl.BlockSpec((tm, tk), lambda i, j, q: (i, jnp.maximum(q - nk1, 0))))
        args.append(a2)
    if b2 is not None:
        in_specs.append(pl.BlockSpec((tk, tn), lambda i, j, q: (jnp.where(j < nj1, 0, q), jnp.maximum(j - nj1, 0))))
        args.append(b2)
    if add is not None:
        in_specs.append(pl.BlockSpec((tm, tn), lambda i, j, q: (i, j)))
        args.append(add)
    if shard_cols:
        out_shape = jax.ShapeDtypeStruct((N_CHIPS, m, shard_cols), out_dtype)
        out_spec = pl.BlockSpec((ns, tm, shard_cols), lambda i, j, q: (j, i, 0))
    else:
        out_shape = jax.ShapeDtypeStruct((m, n), out_dtype)
        out_spec = pl.BlockSpec((tm, tn), lambda i, j, q: (i, j))
    own_scratch = [pltpu.VMEM((tm, tn), F32)] if nk > 1 else []
    if ride is None:
        return pl.pallas_call(
            body, name=name, grid=grid, in_specs=in_specs, out_specs=out_spec, out_shape=out_shape,
            scratch_shapes=own_scratch, compiler_params=_params("parallel", "parallel", "arbitrary"),
        )(*args)
    out, *delivered = pl.pallas_call(
        body, name=name, grid=grid, in_specs=in_specs + [ANY] * len(ride.arrays),
        out_specs=[out_spec] + [ANY] * len(ride.out_shape), out_shape=[out_shape] + ride.out_shape,
        scratch_shapes=own_scratch + ride.scratch, compiler_params=_params("arbitrary", "arbitrary", "arbitrary"),
    )(*args, *ride.arrays)
    return out, delivered


def rmsnorm_fwd(x, g, *, name):
    t, d = x.shape
    tr = _pick(t, (512, 256))

    def body(x_ref, g_ref, o_ref):
        xv = x_ref[...]
        r = lax.rsqrt(jnp.mean(xv * xv, axis=-1, keepdims=True) + EPS)
        o_ref[...] = ((xv * r) * g_ref[...]).astype(o_ref.dtype)

    return pl.pallas_call(
        body, name=name, grid=(t // tr,),
        in_specs=[pl.BlockSpec((tr, d), lambda i: (i, 0)), pl.BlockSpec((1, d), lambda i: (0, 0))],
        out_specs=pl.BlockSpec((tr, d), lambda i: (i, 0)), out_shape=jax.ShapeDtypeStruct((t, d), BF),
        compiler_params=_params("parallel"),
    )(x, g)


def rmsnorm_bwd(x, g, dn, dres, *, name, ride=None):
    t, d = x.shape
    tr = _pick(t, (512, 256))
    want_dx = dres is not None

    def body(*refs):
        if want_dx:
            (x_ref, g_ref, dn_ref, dres_ref), (dx_ref, dxb_ref, dg_ref), _, ride_start, ride_wait = _ride_hooks(
                ride, refs, 4, 3, 0, (t // tr,))
            ride_start()
        else:
            x_ref, g_ref, dn_ref, dg_ref = refs

        @pl.when(pl.program_id(0) == 0)
        def _():
            dg_ref[...] = jnp.zeros_like(dg_ref)

        xv = x_ref[...]
        r = lax.rsqrt(jnp.mean(xv * xv, axis=-1, keepdims=True) + EPS)
        xn = xv * r
        dnv = dn_ref[...].astype(F32)
        dg_ref[...] += jnp.sum(dnv * xn, axis=0, keepdims=True)
        if want_dx:
            dyg = dnv * g_ref[...]
            cm = jnp.mean(dyg * xn, axis=-1, keepdims=True)
            dx = dres_ref[...] + r * (dyg - xn * cm)
            dx_ref[...] = dx
            dxb_ref[...] = dx.astype(BF)
            ride_wait()

    row = pl.BlockSpec((tr, d), lambda i: (i, 0))
    vec = pl.BlockSpec((1, d), lambda i: (0, 0))
    if want_dx:
        extra = ride.arrays if ride else []
        dx, dxb, dg, *delivered = pl.pallas_call(
            body, name=name, grid=(t // tr,), in_specs=[row, vec, row, row] + [ANY] * len(extra),
            out_specs=[row, row, vec] + [ANY] * (len(ride.out_shape) if ride else 0),
            out_shape=[jax.ShapeDtypeStruct((t, d), F32), jax.ShapeDtypeStruct((t, d), BF), jax.ShapeDtypeStruct((1, d), F32)]
            + (ride.out_shape if ride else []),
            scratch_shapes=ride.scratch if ride else [], compiler_params=_params("arbitrary"),
        )(x, g, dn, dres, *extra)
        return (dx, dxb, dg, delivered) if ride else (dx, dxb, dg)
    return None, None, pl.pallas_call(
        body, name=name, grid=(t // tr,), in_specs=[row, vec, row], out_specs=vec,
        out_shape=jax.ShapeDtypeStruct((1, d), F32), compiler_params=_params("arbitrary"),
    )(x, g, dn)


def rmsnorm_fwd2(x, g1, g2, *, name):
    t, d = x.shape
    tr = _pick(t, (512, 256))

    def body(x_ref, g1_ref, g2_ref, o1_ref, o2_ref):
        xv = x_ref[...]
        xn = xv * lax.rsqrt(jnp.mean(xv * xv, axis=-1, keepdims=True) + EPS)
        o1_ref[...] = (xn * g1_ref[...]).astype(o1_ref.dtype)
        o2_ref[...] = (xn * g2_ref[...]).astype(o2_ref.dtype)

    row = pl.BlockSpec((tr, d), lambda i: (i, 0))
    vec = pl.BlockSpec((1, d), lambda i: (0, 0))
    return pl.pallas_call(
        body, name=name, grid=(t // tr,), in_specs=[row, vec, vec], out_specs=[row, row],
        out_shape=[jax.ShapeDtypeStruct((t, d), BF)] * 2, compiler_params=_params("parallel"),
    )(x, g1, g2)


def rmsnorm_bwd2(x, g1, g2, dn1, dn2, dres, *, name):
    t, d = x.shape
    tr = _pick(t, (512, 256))

    def body(x_ref, g1_ref, g2_ref, dn1_ref, dn2_ref, dres_ref, dx_ref, dxb_ref, dg1_ref, dg2_ref):
        @pl.when(pl.program_id(0) == 0)
        def _():
            dg1_ref[...] = jnp.zeros_like(dg1_ref)
            dg2_ref[...] = jnp.zeros_like(dg2_ref)

        xv = x_ref[...]
        r = lax.rsqrt(jnp.mean(xv * xv, axis=-1, keepdims=True) + EPS)
        xn = xv * r
        d1, d2 = dn1_ref[...].astype(F32), dn2_ref[...].astype(F32)
        dg1_ref[...] += jnp.sum(d1 * xn, axis=0, keepdims=True)
        dg2_ref[...] += jnp.sum(d2 * xn, axis=0, keepdims=True)
        dyg = d1 * g1_ref[...] + d2 * g2_ref[...]
        cm = jnp.mean(dyg * xn, axis=-1, keepdims=True)
        dx = dres_ref[...] + r * (dyg - xn * cm)
        dx_ref[...] = dx
        dxb_ref[...] = dx.astype(BF)

    row = pl.BlockSpec((tr, d), lambda i: (i, 0))
    vec = pl.BlockSpec((1, d), lambda i: (0, 0))
    return pl.pallas_call(
        body, name=name, grid=(t // tr,), in_specs=[row, vec, vec, row, row, row], out_specs=[row, row, vec, vec],
        out_shape=[jax.ShapeDtypeStruct((t, d), F32), jax.ShapeDtypeStruct((t, d), BF), jax.ShapeDtypeStruct((1, d), F32),
                   jax.ShapeDtypeStruct((1, d), F32)],
        compiler_params=_params("arbitrary"),
    )(x, g1, g2, dn1, dn2, dres)


def final_loss(x, g, target, *, name):
    t, d = x.shape
    tr = _pick(t, (512, 256))

    def body(x_ref, g_ref, t_ref, dx_ref, dxb_ref, dg_ref, lv_ref):
        @pl.when(pl.program_id(0) == 0)
        def _():
            dg_ref[...] = jnp.zeros_like(dg_ref)
            lv_ref[...] = jnp.zeros_like(lv_ref)

        xv = x_ref[...]
        r = lax.rsqrt(jnp.mean(xv * xv, axis=-1, keepdims=True) + EPS)
        xn = xv * r
        err = xn * g_ref[...] - t_ref[...]
        lv_ref[...] += jnp.sum(err * err, axis=0, keepdims=True)
        dy = err * (1.0 / d)
        dg_ref[...] += jnp.sum(dy * xn, axis=0, keepdims=True)
        dyg = dy * g_ref[...]
        cm = jnp.mean(dyg * xn, axis=-1, keepdims=True)
        dx = r * (dyg - xn * cm)
        dx_ref[...] = dx
        dxb_ref[...] = dx.astype(BF)

    row = pl.BlockSpec((tr, d), lambda i: (i, 0))
    vec = pl.BlockSpec((1, d), lambda i: (0, 0))
    return pl.pallas_call(
        body, name=name, grid=(t // tr,), in_specs=[row, vec, row], out_specs=[row, row, vec, vec],
        out_shape=[jax.ShapeDtypeStruct((t, d), F32), jax.ShapeDtypeStruct((t, d), BF), jax.ShapeDtypeStruct((1, d), F32),
                   jax.ShapeDtypeStruct((1, d), F32)],
        compiler_params=_params("arbitrary"),
    )(x, g, target)


CONV_ROWS = 256
HALO = 16


def _conv_taps(ext, w, rows):
    s0 = ext[HALO:HALO + rows]
    s1 = pltpu.roll(ext, 1, 0)[HALO:HALO + rows]
    s2 = pltpu.roll(ext, 2, 0)[HALO:HALO + rows]
    return (w[0:1] * s2 + w[1:2] * s1) + w[2:3] * s0, s0, s1, s2


def conv_silu_fwd(u, w, bl, *, name, ride=None):
    t, f2 = u.shape
    f = f2 // 2
    s = t // bl
    tc = _pick(f, (256, 128))
    nf = f // tc
    nch = s // CONV_ROWS

    def body(*refs):
        (ua_ref, ug_ref, wa_ref, wg_ref), (h_ref, ca_ref, cg_ref), _, ride_start, ride_wait = _ride_hooks(
            ride, refs, 4, 3, 0, (nf, bl))
        ride_start()
        wa, wg = wa_ref[...], wg_ref[...]

        def chunk(ci, carry):
            r0 = pl.multiple_of(ci * CONV_ROWS, CONV_ROWS)
            ps = pl.multiple_of(jnp.maximum(r0 - HALO, 0), HALO)
            keep = (ci > 0).astype(F32)
            rows = pl.ds(r0, CONV_ROWS)

            def conv(ref, wv):
                ext = jnp.concatenate([ref[0, pl.ds(ps, HALO), :].astype(F32) * keep, ref[0, rows, :].astype(F32)], axis=0)
                return _conv_taps(ext, wv, CONV_ROWS)[0]

            ca, cg = conv(ua_ref, wa), conv(ug_ref, wg)
            sg = pl.reciprocal(1.0 + jnp.exp(-cg), approx=True)
            h_ref[0, rows, :] = ((cg * sg) * ca).astype(h_ref.dtype)
            ca_ref[0, rows, :] = ca.astype(ca_ref.dtype)
            cg_ref[0, rows, :] = cg.astype(cg_ref.dtype)
            return carry

        lax.fori_loop(0, nch, chunk, 0)
        ride_wait()

    u3 = u.reshape(bl, s, f2)
    blk = lambda off: pl.BlockSpec((1, s, tc), lambda j, b: (b, 0, j + off))
    wblk = lambda off: pl.BlockSpec((3, tc), lambda j, b: (0, j + off))
    extra = ride.arrays if ride else []
    h, ca, cg, *delivered = pl.pallas_call(
        body, name=name, grid=(nf, bl), in_specs=[blk(0), blk(nf), wblk(0), wblk(nf)] + [ANY] * len(extra),
        out_specs=[blk(0)] * 3 + [ANY] * (len(ride.out_shape) if ride else 0),
        out_shape=[jax.ShapeDtypeStruct((bl, s, f), BF)] * 3 + (ride.out_shape if ride else []),
        scratch_shapes=ride.scratch if ride else [],
        compiler_params=_params(*(("arbitrary",) * 2 if ride else ("parallel",) * 2)),
    )(u3, u3, w, w, *extra)
    return h.reshape(t, f), ca.reshape(t, f), cg.reshape(t, f), delivered


def conv_silu_bwd(u, ca, cg, w, dh, bl, *, name):
    t, f2 = u.shape
    f = f2 // 2
    s = t // bl
    tc = _pick(f, (256, 128))
    nf = f // tc
    nch = s // CONV_ROWS
    ext_rows = CONV_ROWS + HALO

    def body(ua_ref, ug_ref, ca_ref, cg_ref, wa_ref, wg_ref, dh_ref, dua_ref, dug_ref, dwa_ref, dwg_ref):
        wa, wg = wa_ref[...], wg_ref[...]

        @pl.when(pl.program_id(1) == 0)
        def _():
            dwa_ref[...] = jnp.zeros_like(dwa_ref)
            dwg_ref[...] = jnp.zeros_like(dwg_ref)

        def chunk(ci, carry):
            r0 = pl.multiple_of(ci * CONV_ROWS, CONV_ROWS)
            ns = pl.multiple_of(jnp.minimum(r0 + CONV_ROWS, s - HALO), HALO)
            keep_n = (ci < nch - 1).astype(F32)
            rows = pl.ds(r0, CONV_ROWS)

            def ext_of(ref):
                return jnp.concatenate([ref[0, rows, :].astype(F32), ref[0, pl.ds(ns, HALO), :].astype(F32)], axis=0)

            dhe = jnp.concatenate([dh_ref[0, rows, :].astype(F32), dh_ref[0, pl.ds(ns, HALO), :].astype(F32) * keep_n],
                                  axis=0)
            cae, cge = ext_of(ca_ref), ext_of(cg_ref)
            sg = pl.reciprocal(1.0 + jnp.exp(-cge), approx=True)
            dca = dhe * (cge * sg)
            dcg = dhe * cae * (sg * (1.0 + cge * (1.0 - sg)))

            def back(dc, wv, u_ref, du_ref, dw_ref):
                d0 = dc[:CONV_ROWS]
                n1 = pltpu.roll(dc, ext_rows - 1, 0)[:CONV_ROWS]
                n2 = pltpu.roll(dc, ext_rows - 2, 0)[:CONV_ROWS]
                du_ref[0, rows, :] = ((wv[2:3] * d0 + wv[1:2] * n1) + wv[0:1] * n2).astype(du_ref.dtype)
                uc = u_ref[0, rows, :].astype(F32)
                for k, shifted in enumerate((n2, n1, d0)):
                    dw_ref[k:k + 1, :] += jnp.sum(shifted * uc, axis=0, keepdims=True)

            back(dca, wa, ua_ref, dua_ref, dwa_ref)
            back(dcg, wg, ug_ref, dug_ref, dwg_ref)
            return carry

        lax.fori_loop(0, nch, chunk, 0)

    u3 = u.reshape(bl, s, f2)
    blk = lambda off: pl.BlockSpec((1, s, tc), lambda j, b: (b, 0, j + off))
    wblk = lambda off: pl.BlockSpec((3, tc), lambda j, b: (0, j + off))
    dua, dug, dwa, dwg = pl.pallas_call(
        body, name=name, grid=(nf, bl), in_specs=[blk(0), blk(nf), blk(0), blk(0), wblk(0), wblk(nf), blk(0)],
        out_specs=[blk(0), blk(0), wblk(0), wblk(0)],
        out_shape=[jax.ShapeDtypeStruct((bl, s, f), BF), jax.ShapeDtypeStruct((bl, s, f), BF),
                   jax.ShapeDtypeStruct((3, f), F32), jax.ShapeDtypeStruct((3, f), F32)],
        compiler_params=_params("parallel", "arbitrary"),
    )(u3, u3, ca.reshape(bl, s, f), cg.reshape(bl, s, f), w, w, dh.reshape(bl, s, f))
    return dua.reshape(t, f), dug.reshape(t, f), jnp.concatenate([dwa, dwg], axis=1)


def _lane_masks(rows):
    lane = lax.broadcasted_iota(jnp.int32, (rows, LANES), 1)
    return lane < HEAD_DIM, lane >= HEAD_DIM


def _nt(a, b):
    return lax.dot_general(a, b, (((1,), (1,)), ((), ())), preferred_element_type=F32)


def _tn(a, b):
    return lax.dot_general(a, b, (((0,), (0,)), ((), ())), preferred_element_type=F32)


def _nn(a, b):
    return jnp.dot(a, b, preferred_element_type=F32)


def _suffix_ones():
    j = lax.broadcasted_iota(jnp.int32, (2 * QB, QB), 0) % QB
    s = lax.broadcasted_iota(jnp.int32, (2 * QB, QB), 1)
    return (j >= s).astype(BF)


def _softplus(z):
    return jnp.maximum(z, 0.0) + jnp.log(1.0 + jnp.exp(-jnp.abs(z)))


SB_BLOCK = 256
SB_DEAD = -100.0


SB_STRIP = 32
SB_SUB = SB_BLOCK // QB


def _sb_scratch(backward):
    blk = (2, SB_BLOCK, SB_BLOCK)
    per_head = pltpu.VMEM((2, SB_BLOCK, LANES), F32)
    scr = [pltpu.VMEM((2 * QB, QB), BF),
           pltpu.VMEM(blk, F32),
           pltpu.VMEM((2, SB_SUB) + blk[1:], BF),
           pltpu.VMEM(blk, F32),
           pltpu.VMEM(blk, BF),
           per_head]
    if not backward:
        return scr + [per_head]
    return scr + [pltpu.VMEM(blk, F32),
                  pltpu.VMEM(blk, F32),
                  pltpu.VMEM(blk, F32),
                  pltpu.VMEM(blk, BF),
                  per_head,
                  per_head,
                  per_head]


def _sb_strips(fn, diag):
    for r in range(SB_BLOCK // SB_STRIP):
        fn(pl.ds(r * SB_STRIP, SB_STRIP), r, QB if diag and (r + 1) * SB_STRIP <= QB else SB_BLOCK)


def _sb_keep(diag, r, v):
    if not diag:
        return v
    rel = (lax.broadcasted_iota(jnp.int32, v.shape, 1) - lax.broadcasted_iota(jnp.int32, v.shape, 0))
    return jnp.where(rel < r * SB_STRIP, v, 0.0)


def _sb_split_store(sp_scr, rows, v):
    hi = v.astype(BF)
    lo = (v - hi.astype(F32)).astype(BF)
    for u in range(SB_SUB):
        if (u + 1) * QB <= v.shape[1]:
            sp_scr[u, rows, 0:QB] = hi[:, u * QB:(u + 1) * QB]
            sp_scr[u, rows, QB:2 * QB] = lo[:, u * QB:(u + 1) * QB]
        else:
            sp_scr[u, rows, :] = jnp.zeros((SB_STRIP, 2 * QB), BF)


def _sb_suffix_sums(uu_scr, sp_scr, out_scr):
    for u in range(SB_SUB):
        out_scr[:, u * QB:(u + 1) * QB] = _nn(sp_scr[u], uu_scr[...])


def _sb_fold(sums, off):
    nsub = sums.shape[1] // QB
    offs = [None] * nsub
    for u in reversed(range(nsub)):
        offs[u] = off
        off = off + jnp.broadcast_to(sums[:, u * QB:u * QB + 1], (SB_STRIP, LANES))
    return offs, off


def _sb_store_live(ref, rows, v):
    live = v.shape[1]
    ref[rows, 0:live] = v
    if live < SB_BLOCK:
        ref[rows, live:] = jnp.zeros((SB_STRIP, SB_BLOCK - live), v.dtype)


def _sb_first(diag, z_scr, sp_scr, ls_scr=None):
    def strip(rows, r, live):
        ls = _sb_keep(diag, r, -_softplus(z_scr[rows, 0:live]))
        if ls_scr is not None:
            ls_scr[rows, 0:live] = ls
        _sb_split_store(sp_scr, rows, ls)

    _sb_strips(strip, diag)


def _sb_second(diag, z_scr, cin_scr, w_scr, carry_scr):
    def strip(rows, r, live):
        z, cin = z_scr[rows, 0:live], cin_scr[rows, 0:live]
        offs, carry_scr[rows, :] = _sb_fold(cin, carry_scr[rows, :])
        e = jnp.concatenate([z[:, u * QB:(u + 1) * QB] + cin[:, u * QB:(u + 1) * QB] + offs[u]
                             for u in range(live // QB)], axis=1)
        _sb_store_live(w_scr, rows, _sb_keep(diag, r, jnp.exp(e)).astype(BF))

    _sb_strips(strip, diag)


def _sb_third(diag, w_scr, dw_scr, da_scr, sp_scr):
    def strip(rows, r, live):
        da = w_scr[rows, 0:live].astype(F32) * dw_scr[rows, 0:live]
        da_scr[rows, 0:live] = da
        _sb_split_store(sp_scr, rows, da)

    _sb_strips(strip, diag)


def _sb_fourth(diag, z_scr, ls_scr, da_scr, sin_scr, dz_scr, carry_s_scr, dsum_scr):
    def strip(rows, r, live):
        da, sin = da_scr[rows, 0:live], sin_scr[rows, 0:live]
        offs, carry_s_scr[rows, :] = _sb_fold(sin, carry_s_scr[rows, :])
        dsum = dsum_scr[rows, :]
        pre = jnp.concatenate([dsum - (sin[:, u * QB:(u + 1) * QB] - da[:, u * QB:(u + 1) * QB] + offs[u])
                               for u in range(live // QB)], axis=1)
        sig = jnp.exp(z_scr[rows, 0:live] + ls_scr[rows, 0:live])
        _sb_store_live(dz_scr, rows, _sb_keep(diag, r, da - sig * pre).astype(BF))

    _sb_strips(strip, diag)


def _sb_alive(carry_scr):
    return (jnp.max(carry_scr[...]) > SB_DEAD).astype(jnp.int32)


def _ride_hooks(ride, refs, n_in, n_out, n_scratch, grid):
    if ride is None:
        return refs[:n_in], refs[n_in:n_in + n_out], refs[n_in + n_out:], lambda: None, lambda: None
    ni, no = len(ride.arrays), len(ride.out_shape)
    own_in, rin = refs[:n_in], refs[n_in:n_in + ni]
    own_out, rout = refs[n_in + ni:n_in + ni + n_out], refs[n_in + ni + n_out:n_in + ni + n_out + no]
    rest = refs[n_in + ni + n_out + no:]
    own_scr, sems = rest[:n_scratch], rest[n_scratch:]
    ids = [pl.program_id(a) for a in range(len(grid))]

    def start():
        first = functools.reduce(lambda u, v: u & v, [i == 0 for i in ids])
        pl.when(first)(lambda: ride.start(rin, rout, sems))

    def wait():
        last = functools.reduce(lambda u, v: u & v, [i == n - 1 for i, n in zip(ids, grid)])
        pl.when(last)(lambda: ride.wait(rin, rout, sems))

    return own_in, own_out, own_scr, start, wait


def sb_fwd(proj, bl, *, name, ride=None):
    t, width = proj.shape
    s = t // bl
    npair = SB_WIDTH // LANES
    nq = s // SB_BLOCK
    grid = (bl, npair, nq)

    own_scratch = _sb_scratch(backward=False)

    def body(*refs):
        (q_ref, k_ref, v_ref), (o_ref, of_ref), scr, ride_start, ride_wait = _ride_hooks(
            ride, refs, 3, 2, len(own_scratch), grid)
        uu_scr, z_scr, sp_scr, cin_scr, w_scr, carry_scr, acc_scr = scr
        ride_start()
        i = pl.program_id(2)
        uu_scr[...] = _suffix_ones()
        carry_scr[...] = jnp.zeros_like(carry_scr)
        acc_scr[...] = jnp.zeros_like(acc_scr)
        qs = (q_ref[0].astype(F32) * SCALE).astype(BF)
        masks = _lane_masks(SB_BLOCK)

        z, sp, cin, w, car, acc = ([r.at[h] for h in range(2)] for r in (z_scr, sp_scr, cin_scr, w_scr, carry_scr, acc_scr))

        def live(c, diag):
            start = pl.multiple_of(c * SB_BLOCK, SB_BLOCK)
            kb, vb = k_ref[0, pl.ds(start, SB_BLOCK), :], v_ref[0, pl.ds(start, SB_BLOCK), :]
            for h in range(2):
                z[h][...] = _nt(jnp.where(masks[h], qs, jnp.zeros_like(qs)), kb)
            _sb_first(diag, z[0], sp[0])
            _sb_suffix_sums(uu_scr, sp[0], cin[0])
            _sb_first(diag, z[1], sp[1])
            _sb_suffix_sums(uu_scr, sp[1], cin[1])
            _sb_second(diag, z[0], cin[0], w[0], car[0])
            acc[0][...] += _nn(w[0][...], vb)
            _sb_second(diag, z[1], cin[1], w[1], car[1])
            acc[1][...] += _nn(w[1][...], vb)
            return _sb_alive(carry_scr)

        def step(n, alive):
            return lax.cond(alive > 0, lambda: live(i - n, False), lambda: alive)

        lax.fori_loop(1, i + 1, step, live(i, True))
        both = jnp.where(masks[0], acc_scr[0], acc_scr[1])
        o_ref[0] = both.astype(o_ref.dtype)
        of_ref[0] = both
        ride_wait()

    p3 = proj.reshape(bl, s, width)
    qblk = pl.BlockSpec((1, SB_BLOCK, LANES), lambda b, p, i: (b, i, p))
    extra = ride.arrays if ride else []
    o, of, *delivered = pl.pallas_call(
        body, name=name, grid=grid,
        in_specs=[qblk, pl.BlockSpec((1, s, LANES), lambda b, p, i: (b, 0, npair + p)),
                  pl.BlockSpec((1, s, LANES), lambda b, p, i: (b, 0, 2 * npair + p))] + [ANY] * len(extra),
        out_specs=[qblk, qblk] + [ANY] * (len(ride.out_shape) if ride else 0),
        out_shape=[jax.ShapeDtypeStruct((bl, s, SB_WIDTH), BF), jax.ShapeDtypeStruct((bl, s, SB_WIDTH), F32)]
        + (ride.out_shape if ride else []),
        scratch_shapes=own_scratch + (ride.scratch if ride else []),
        compiler_params=_params("arbitrary", "arbitrary", "arbitrary"),
    )(p3, p3, p3, *extra)
    return o.reshape(t, SB_WIDTH), of.reshape(t, SB_WIDTH), delivered


def sb_bwd(proj, o, dcat, bl, *, name, ride=None):
    t, width = proj.shape
    s = t // bl
    npair = SB_WIDTH // LANES
    nq = s // SB_BLOCK
    grid = (bl, npair, nq)
    own_scratch = _sb_scratch(backward=True)

    def body(*refs):
        (q_ref, k_ref, v_ref, o_ref, do_ref), (dq_ref, dk_ref, dv_ref), scr, ride_start, ride_wait = _ride_hooks(
            ride, refs, 5, 3, len(own_scratch), grid)
        (uu_scr, z_scr, sp_scr, cin_scr, w_scr, carry_scr,
         ls_scr, dw_scr, da_scr, dz_scr, carry_s_scr, dsum_scr, dq_scr) = scr
        ride_start()
        i = pl.program_id(2)

        @pl.when(i == 0)
        def _():
            dk_ref[...] = jnp.zeros_like(dk_ref)
            dv_ref[...] = jnp.zeros_like(dv_ref)

        uu_scr[...] = _suffix_ones()
        for ref in (carry_scr, carry_s_scr, dq_scr):
            ref[...] = jnp.zeros_like(ref)
        masks = _lane_masks(SB_BLOCK)
        qs = (q_ref[0].astype(F32) * SCALE).astype(BF)
        do = do_ref[0]
        prod = do.astype(F32) * o_ref[0]
        for h in range(2):
            dsum_scr[h] = jnp.broadcast_to(jnp.sum(jnp.where(masks[h], prod, 0.0), axis=1, keepdims=True),
                                           (SB_BLOCK, LANES))

        z, sp, cin, w, car, ls, dw, da, dz, cars, dsum, dq = (
            [r.at[h] for h in range(2)] for r in (z_scr, sp_scr, cin_scr, w_scr, carry_scr, ls_scr, dw_scr, da_scr,
                                                  dz_scr, carry_s_scr, dsum_scr, dq_scr))

        def live(c, diag):
            start = pl.multiple_of(c * SB_BLOCK, SB_BLOCK)
            keys = pl.ds(start, SB_BLOCK)
            kb, vb = k_ref[0, keys, :], v_ref[0, keys, :]
            qh = [jnp.where(masks[h], qs, jnp.zeros_like(qs)) for h in range(2)]
            doh = [jnp.where(masks[h], do, jnp.zeros_like(do)) for h in range(2)]
            for h in range(2):
                z[h][...] = _nt(qh[h], kb)
                dw[h][...] = _nt(doh[h], vb)

            def weights(h):
                _sb_second(diag, z[h], cin[h], w[h], car[h])
                _sb_third(diag, w[h], dw[h], da[h], sp[h])
                _sb_suffix_sums(uu_scr, sp[h], cin[h])

            def grads(h):
                _sb_fourth(diag, z[h], ls[h], da[h], cin[h], dz[h], cars[h], dsum[h])
                dk_ref[0, keys, :] += _tn(dz[h][...], qh[h])
                dv_ref[0, keys, :] += _tn(w[h][...], doh[h])
                dq[h][...] += _nn(dz[h][...], kb)

            _sb_first(diag, z[0], sp[0], ls[0])
            _sb_suffix_sums(uu_scr, sp[0], cin[0])
            _sb_first(diag, z[1], sp[1], ls[1])
            _sb_suffix_sums(uu_scr, sp[1], cin[1])
            weights(0)
            weights(1)
            grads(0)
            grads(1)
            return _sb_alive(carry_scr)

        def step(n, alive):
            return lax.cond(alive > 0, lambda: live(i - n, False), lambda: alive)

        lax.fori_loop(1, i + 1, step, live(i, True))
        dq_ref[0] = (jnp.where(masks[0], dq_scr[0], dq_scr[1]) * SCALE).astype(dq_ref.dtype)
        ride_wait()

    p3 = proj.reshape(bl, s, width)
    o3 = o.reshape(bl, s, SB_WIDTH)
    d3 = dcat.reshape(bl, s, dcat.shape[1])
    qblk = pl.BlockSpec((1, SB_BLOCK, LANES), lambda b, p, i: (b, i, p))
    full = pl.BlockSpec((1, s, LANES), lambda b, p, i: (b, 0, p))
    extra = ride.arrays if ride else []
    dq, dk, dv, *delivered = pl.pallas_call(
        body, name=name, grid=grid,
        in_specs=[qblk, pl.BlockSpec((1, s, LANES), lambda b, p, i: (b, 0, npair + p)),
                  pl.BlockSpec((1, s, LANES), lambda b, p, i: (b, 0, 2 * npair + p)), qblk, qblk] + [ANY] * len(extra),
        out_specs=[qblk, full, full] + [ANY] * (len(ride.out_shape) if ride else 0),
        out_shape=[jax.ShapeDtypeStruct((bl, s, SB_WIDTH), BF), jax.ShapeDtypeStruct((bl, s, SB_WIDTH), F32),
                   jax.ShapeDtypeStruct((bl, s, SB_WIDTH), F32)] + (ride.out_shape if ride else []),
        scratch_shapes=own_scratch + (ride.scratch if ride else []),
        compiler_params=_params("arbitrary", "arbitrary", "arbitrary"),
    )(p3, p3, p3, o3, d3, *extra)
    return dq.reshape(t, SB_WIDTH), dk.reshape(t, SB_WIDTH), dv.reshape(t, SB_WIDTH), delivered


MEM_TQ = 512


def mem_fwd(proj, kvm, bl, *, name):
    t, width = proj.shape
    s = t // bl
    qoff = (width - MEM_WIDTH) // LANES
    npair = MEM_WIDTH // LANES

    def body(q_ref, k_ref, v_ref, o_ref):
        ma, mb = _lane_masks(MEM_TQ)
        q = q_ref[...].astype(BF)
        outs = []
        for m in (ma, mb):
            qh = jnp.where(m, q, jnp.zeros_like(q))
            sc = _nt(qh, k_ref[...]) * SCALE
            p = jnp.exp(sc - jnp.max(sc, axis=-1, keepdims=True))
            p = p * (1.0 / jnp.sum(p, axis=-1, keepdims=True))
            outs.append(_nn(p.astype(BF), v_ref[...]))
        o_ref[...] = jnp.where(ma, outs[0], outs[1]).astype(o_ref.dtype)

    nt = s // MEM_TQ
    return pl.pallas_call(
        body, name=name, grid=(bl, npair, nt),
        in_specs=[pl.BlockSpec((MEM_TQ, LANES), lambda b, p, i: (b * nt + i, qoff + p)),
                  pl.BlockSpec((MEM_LEN, LANES), lambda b, p, i: (b, p)),
                  pl.BlockSpec((MEM_LEN, LANES), lambda b, p, i: (b, npair + p))],
        out_specs=pl.BlockSpec((MEM_TQ, LANES), lambda b, p, i: (b * nt + i, p)),
        out_shape=jax.ShapeDtypeStruct((t, MEM_WIDTH), BF),
        compiler_params=_params("parallel", "parallel", "parallel"),
    )(proj, kvm, kvm)


def mem_bwd(proj, kvm, dcat, bl, *, name):
    t, width = proj.shape
    s = t // bl
    qoff = (width - MEM_WIDTH) // LANES
    doff = (dcat.shape[1] - MEM_WIDTH) // LANES
    npair = MEM_WIDTH // LANES
    nt = s // MEM_TQ

    def body(q_ref, k_ref, v_ref, do_ref, dq_ref, dk_ref, dv_ref):
        @pl.when(pl.program_id(2) == 0)
        def _():
            dk_ref[...] = jnp.zeros_like(dk_ref)
            dv_ref[...] = jnp.zeros_like(dv_ref)

        ma, mb = _lane_masks(MEM_TQ)
        q = q_ref[...].astype(BF)
        do = do_ref[...]
        kb, vb = k_ref[...], v_ref[...]
        dqs = []
        for m in (ma, mb):
            qh = jnp.where(m, q, jnp.zeros_like(q))
            doh = jnp.where(m, do, jnp.zeros_like(do))
            sc = _nt(qh, kb) * SCALE
            p = jnp.exp(sc - jnp.max(sc, axis=-1, keepdims=True))
            p = p * (1.0 / jnp.sum(p, axis=-1, keepdims=True))
            dp = _nt(doh, vb)
            ds = p * (dp - jnp.sum(p * dp, axis=-1, keepdims=True))
            dss = (ds * SCALE).astype(BF)
            dk_ref[...] += _tn(dss, qh)
            dv_ref[...] += _tn(p.astype(BF), doh)
            dqs.append(_nn(dss, kb))
        dq_ref[...] = jnp.where(ma, dqs[0], dqs[1]).astype(dq_ref.dtype)

    kblk = pl.BlockSpec((MEM_LEN, LANES), lambda b, p, i: (b, p))
    dq, dk, dv = pl.pallas_call(
        body, name=name, grid=(bl, npair, nt),
        in_specs=[pl.BlockSpec((MEM_TQ, LANES), lambda b, p, i: (b * nt + i, qoff + p)), kblk,
                  pl.BlockSpec((MEM_LEN, LANES), lambda b, p, i: (b, npair + p)),
                  pl.BlockSpec((MEM_TQ, LANES), lambda b, p, i: (b * nt + i, doff + p))],
        out_specs=[pl.BlockSpec((MEM_TQ, LANES), lambda b, p, i: (b * nt + i, p)), kblk, kblk],
        out_shape=[jax.ShapeDtypeStruct((t, MEM_WIDTH), BF), jax.ShapeDtypeStruct((bl * MEM_LEN, MEM_WIDTH), F32),
                   jax.ShapeDtypeStruct((bl * MEM_LEN, MEM_WIDTH), F32)],
        compiler_params=_params("parallel", "parallel", "arbitrary"),
    )(proj, kvm, kvm, dcat)
    return dq, jnp.concatenate([dk, dv], axis=1).astype(BF)


def _dil_rows(r, u0, size, dil):
    if dil == 1:
        return pl.ds(_mo(u0, QB), size)
    return pl.ds(u0 * dil + r, size, stride=dil)


def _dil_walk(block, dil, nb):
    def residue(r, c):
        block(r, 0, QB)
        if nb > 1:
            def step(n, c2):
                block(r, n, 2 * QB)
                return c2
            lax.fori_loop(1, nb, step, 0, unroll=3)
        return c

    if dil == 1:
        residue(0, 0)
    else:
        lax.fori_loop(0, dil, residue, 0, unroll=4 if nb == 1 else 1)


def _dil_scores(qh, kb, slope_d, n_keys):
    i = lax.broadcasted_iota(jnp.int32, (QB, n_keys), 0)
    j = lax.broadcasted_iota(jnp.int32, (QB, n_keys), 1)
    delta = i + (n_keys - QB) - j
    valid = (delta >= 0) & (delta <= QB)
    sc = _nt(qh, kb) * SCALE - slope_d * delta.astype(F32)
    return jnp.where(valid, sc, NEG)


def _dil_slopes(g, dil):
    p = pl.program_id(1)
    sa = jnp.where(p == 0, ALIBI[4 * g] * dil, ALIBI[4 * g + 2] * dil).astype(F32)
    sb = jnp.where(p == 0, ALIBI[4 * g + 1] * dil, ALIBI[4 * g + 3] * dil).astype(F32)
    return sa, sb


def dil_fwd(projb, kv, g, bl, *, name):
    _, dil = DIL_GROUPS[g]
    t, wq = projb.shape
    wk = kv.shape[1]
    s = t // bl
    ln = s // dil
    nb = ln // QB
    gw = 4 * HEAD_DIM
    ck, co = wk // LANES, gw // LANES

    def body(q_ref, k_ref, v_ref, o_ref, lse_ref):
        sa, sb = _dil_slopes(g, dil)
        ma, mb = _lane_masks(QB)

        def block(r, n, n_keys):
            q0 = n * QB
            k0 = q0 - (n_keys - QB)
            q = q_ref[0, _dil_rows(r, q0, QB, dil), :].astype(BF)
            kb = k_ref[0, _dil_rows(r, k0, n_keys, dil), :].astype(BF)
            vb = v_ref[0, _dil_rows(r, k0, n_keys, dil), :].astype(BF)
            outs, lses = [], []
            for m, sl in ((ma, sa), (mb, sb)):
                qh = jnp.where(m, q, jnp.zeros_like(q))
                sc = _dil_scores(qh, kb, sl, n_keys)
                mx = jnp.max(sc, axis=-1, keepdims=True)
                p = jnp.exp(sc - mx)
                den = jnp.sum(p, axis=-1, keepdims=True)
                outs.append(_nn(p.astype(BF), vb) * (1.0 / den))
                lses.append(mx + jnp.log(den))
            o_ref[0, _dil_rows(r, q0, QB, dil), :] = jnp.where(ma, outs[0], outs[1])
            lse_ref[0, _dil_rows(r, q0, QB, dil), :] = jnp.where(ma, lses[0], lses[1])

        _dil_walk(block, dil, nb)

    colblk = lambda off: pl.BlockSpec((1, s, LANES), lambda b, p: (b, 0, off + p))
    o, lse = pl.pallas_call(
        body, name=name, grid=(bl, co),
        in_specs=[colblk(g * co), colblk(g * co), colblk(ck // 2 + g * co)],
        out_specs=[colblk(0), colblk(0)],
        out_shape=[jax.ShapeDtypeStruct((bl, s, gw), F32), jax.ShapeDtypeStruct((bl, s, gw), F32)],
        compiler_params=_params("parallel", "parallel"),
    )(projb.reshape(bl, s, wq), kv.reshape(bl, s, wk), kv.reshape(bl, s, wk))
    return o.reshape(t, gw), lse.reshape(t, gw)


def dil_bwd(projb, kv, lse, dog, dshift, g, bl, *, name):
    _, dil = DIL_GROUPS[g]
    t, wq = projb.shape
    wk = kv.shape[1]
    s = t // bl
    ln = s // dil
    nb = ln // QB
    gw = 4 * HEAD_DIM
    ck, co = wk // LANES, gw // LANES

    def body(q_ref, k_ref, v_ref, lse_ref, do_ref, sh_ref, dq_ref, dk_ref, dv_ref):
        sa, sb = _dil_slopes(g, dil)
        ma, mb = _lane_masks(QB)
        dk_ref[...] = jnp.zeros_like(dk_ref)
        dv_ref[...] = jnp.zeros_like(dv_ref)

        def block(r, n, n_keys):
            q0 = n * QB
            k0 = q0 - (n_keys - QB)
            qrows, krows = _dil_rows(r, q0, QB, dil), _dil_rows(r, k0, n_keys, dil)
            q = q_ref[0, qrows, :].astype(BF)
            do = do_ref[0, qrows, :].astype(BF)
            lse_b = lse_ref[0, qrows, :]
            sh_b = sh_ref[0, qrows, :]
            kb = k_ref[0, krows, :].astype(BF)
            vb = v_ref[0, krows, :].astype(BF)
            dqs = []
            for m, sl, c0 in ((ma, sa, 0), (mb, sb, HEAD_DIM)):
                qh = jnp.where(m, q, jnp.zeros_like(q))
                doh = jnp.where(m, do, jnp.zeros_like(do))
                sc = _dil_scores(qh, kb, sl, n_keys)
                p = jnp.exp(sc - lse_b[:, c0:c0 + 1])
                ds = p * (_nt(doh, vb) - sh_b[:, c0:c0 + 1])
                dss = (ds * SCALE).astype(BF)
                dk_ref[0, krows, :] += _tn(dss, qh)
                dv_ref[0, krows, :] += _tn(p.astype(BF), doh)
                dqs.append(_nn(dss, kb))
            dq_ref[0, qrows, :] = jnp.where(ma, dqs[0], dqs[1])

        _dil_walk(block, dil, nb)

    colblk = lambda off: pl.BlockSpec((1, s, LANES), lambda b, p: (b, 0, off + p))
    dq, dk, dv = pl.pallas_call(
        body, name=name, grid=(bl, co),
        in_specs=[colblk(g * co), colblk(g * co), colblk(ck // 2 + g * co), colblk(0), colblk(g * co), colblk(g * co)],
        out_specs=[colblk(0)] * 3,
        out_shape=[jax.ShapeDtypeStruct((bl, s, gw), F32)] * 3,
        compiler_params=_params("parallel", "parallel"),
    )(projb.reshape(bl, s, wq), kv.reshape(bl, s, wk), kv.reshape(bl, s, wk), lse.reshape(bl, s, gw),
      dog.reshape(bl, s, DIL_WIDTH), dshift.reshape(bl, s, DIL_WIDTH))
    return dq.reshape(t, gw), dk.reshape(t, gw), dv.reshape(t, gw)


def _group_weights(lses):
    mx = jnp.maximum(jnp.maximum(lses[0], lses[1]), lses[2])
    es = [jnp.exp(l - mx) for l in lses]
    inv = 1.0 / (es[0] + es[1] + es[2])
    return [e * inv for e in es]


def dil_combine_fwd(os_, lses, *, name):
    t, gw = os_[0].shape
    tr = _pick(t, (512, 256))

    def body(o0, o1, o2, l0, l1, l2, out_ref):
        al = _group_weights([l0[...], l1[...], l2[...]])
        for g, o_ref in enumerate((o0, o1, o2)):
            out_ref[:, g * gw:(g + 1) * gw] = (o_ref[...] * al[g]).astype(out_ref.dtype)

    blk = pl.BlockSpec((tr, gw), lambda i: (i, 0))
    return pl.pallas_call(
        body, name=name, grid=(t // tr,), in_specs=[blk] * 6,
        out_specs=pl.BlockSpec((tr, 3 * gw), lambda i: (i, 0)), out_shape=jax.ShapeDtypeStruct((t, 3 * gw), BF),
        compiler_params=_params("parallel"),
    )(*os_, *lses)


def dil_combine_bwd(os_, lses, dcat, *, name):
    t, gw = os_[0].shape
    tr = _pick(t, (512, 256))

    def head_sum(v):
        parts = []
        for c in range(gw // LANES):
            blk = v[:, c * LANES:(c + 1) * LANES]
            ma, _ = _lane_masks(tr)
            sa = jnp.sum(jnp.where(ma, blk, 0.0), axis=1, keepdims=True)
            sb = jnp.sum(blk, axis=1, keepdims=True) - sa
            parts.append(jnp.where(ma, sa, sb))
        return jnp.concatenate(parts, axis=1)

    def body(o0, o1, o2, l0, l1, l2, d_ref, dog_ref, sh_ref):
        al = _group_weights([l0[...], l1[...], l2[...]])
        dos = [d_ref[:, g * gw:(g + 1) * gw].astype(F32) for g in range(3)]
        dal = [head_sum(dos[g] * o_ref[...]) for g, o_ref in enumerate((o0, o1, o2))]
        mix = al[0] * dal[0] + al[1] * dal[1] + al[2] * dal[2]
        for g in range(3):
            dog_ref[:, g * gw:(g + 1) * gw] = (al[g] * dos[g]).astype(dog_ref.dtype)
            sh_ref[:, g * gw:(g + 1) * gw] = al[g] * mix

    blk = pl.BlockSpec((tr, gw), lambda i: (i, 0))
    wide = pl.BlockSpec((tr, 3 * gw), lambda i: (i, 0))
    return pl.pallas_call(
        body, name=name, grid=(t // tr,), in_specs=[blk] * 6 + [wide], out_specs=[wide, wide],
        out_shape=[jax.ShapeDtypeStruct((t, 3 * gw), F32), jax.ShapeDtypeStruct((t, 3 * gw), F32)],
        compiler_params=_params("parallel"),
    )(*os_, *lses, dcat)


def adamw(w, g1, g2, m, v, *, name):
    r, c = w.shape
    tr = r
    for cand in (256, 128, 64, 32, 16, 8):
        if r % cand == 0 and cand * c * 4 <= (1 << 20):
            tr = cand
            break
    two = g2 is not None

    def body(*refs):
        if two:
            w_ref, g1_ref, g2_ref, m_ref, v_ref, g_ref, d_ref, nm_ref, nv_ref = refs
            g = g1_ref[...] + g2_ref[...]
        else:
            w_ref, g1_ref, m_ref, v_ref, g_ref, d_ref, nm_ref, nv_ref = refs
            g = g1_ref[...]
        nm = ADAM_B1 * m_ref[...] + (1.0 - ADAM_B1) * g
        nv = ADAM_B2 * v_ref[...] + (1.0 - ADAM_B2) * (g * g)
        m_hat = nm / (1.0 - ADAM_B1 ** ADAM_STEP)
        v_hat = nv / (1.0 - ADAM_B2 ** ADAM_STEP)
        g_ref[...] = g
        d_ref[...] = -ADAM_LR * (m_hat / (jnp.sqrt(v_hat) + ADAM_EPS) + ADAM_WD * w_ref[...])
        nm_ref[...] = nm
        nv_ref[...] = nv

    blk = pl.BlockSpec((tr, c), lambda i: (i, 0))
    args = [w, g1] + ([g2] if two else []) + [m, v]
    return pl.pallas_call(
        body, name=name, grid=(r // tr,), in_specs=[blk] * len(args), out_specs=[blk] * 4,
        out_shape=[jax.ShapeDtypeStruct((r, c), F32)] * 4, compiler_params=_params("parallel"),
    )(*args)


def sum4(own, land, *, name):
    r, c = own.shape
    tr = _pick(r, (256, 128, 64))

    def body(o_ref, l_ref, s_ref):
        s_ref[...] = ((o_ref[...].astype(F32) + l_ref[0].astype(F32)) + l_ref[1].astype(F32)) + l_ref[2].astype(F32)

    return pl.pallas_call(
        body, name=name, grid=(r // tr,),
        in_specs=[pl.BlockSpec((tr, c), lambda i: (i, 0)), pl.BlockSpec((3, tr, c), lambda i: (0, i, 0))],
        out_specs=pl.BlockSpec((tr, c), lambda i: (i, 0)), out_shape=jax.ShapeDtypeStruct((r, c), F32),
        compiler_params=_params("parallel"),
    )(own, land)


ANY = pl.BlockSpec(memory_space=pl.ANY)


def _place():
    x, y, c = lax.axis_index("x"), lax.axis_index("y"), lax.axis_index("c")
    chips = [(1 - x, y), (x, 1 - y), (1 - x, 1 - y)]
    return x, y, c, chips


class Ride:
    def __init__(self, arrays, out_shape, copies):
        self.arrays, self.out_shape, self.copies = list(arrays), list(out_shape), copies
        n = 3 * len(self.arrays)
        self.scratch = [pltpu.SemaphoreType.DMA((n,)), pltpu.SemaphoreType.DMA((n,)),
                        pltpu.SemaphoreType.DMA((len(self.arrays),))]

    def split(self, refs):
        n, m = len(self.arrays), len(self.out_shape)
        return refs[:n], refs[n:n + m], refs[n + m:]

    def start(self, ins, outs, sems):
        local, sends, _ = self.copies(ins, outs, *sems)
        for cp in local + sends:
            cp.start()

    def wait(self, ins, outs, sems):
        local, sends, arrivals = self.copies(ins, outs, *sems)
        for cp in arrivals:
            cp.wait_recv()
        for cp in sends:
            cp.wait_send()
        for cp in local:
            cp.wait()

    def run(self, name):
        def body(*refs):
            ins, outs, sems = self.split(refs)
            self.start(ins, outs, sems)
            self.wait(ins, outs, sems)

        return pl.pallas_call(body, name=name, in_specs=[ANY] * len(self.arrays), out_specs=[ANY] * len(self.out_shape),
                              out_shape=self.out_shape, scratch_shapes=self.scratch)(*self.arrays)


def gather_ride(shards, axes):
    def copies(ins, outs, send_sems, recv_sems, local_sems):
        x, y, c, chips = _place()

        def slot(a, q):
            size = shards[a].shape[axes[a]]
            start = pl.multiple_of(q * size, size)
            return outs[a].at[pl.ds(start, size), :] if axes[a] == 0 else outs[a].at[:, pl.ds(start, size)]

        def remote(a, k, q):
            px, py = chips[k]
            return pltpu.make_async_remote_copy(src_ref=ins[a], dst_ref=slot(a, q), send_sem=send_sems.at[3 * a + k],
                                                recv_sem=recv_sems.at[3 * a + k], device_id=(px, py, c), device_id_type=MESH)

        me = 2 * x + y
        n = len(shards)
        local = [pltpu.make_async_copy(ins[a], slot(a, me), local_sems.at[a]) for a in range(n)]
        sends = [remote(a, k, me) for a in range(n) for k in range(3)]
        arrivals = [remote(a, k, 2 * chips[k][0] + chips[k][1]) for a in range(n) for k in range(3)]
        return local, sends, arrivals

    out_shape = []
    for a, sh in enumerate(shards):
        full = list(sh.shape)
        full[axes[a]] *= N_CHIPS
        out_shape.append(jax.ShapeDtypeStruct(tuple(full), sh.dtype))
    return Ride(shards, out_shape, copies)


def scatter_ride(grads):
    def copies(ins, outs, send_sems, recv_sems, local_sems):
        x, y, c, chips = _place()
        sends = [pltpu.make_async_remote_copy(src_ref=ins[a].at[2 * px + py], dst_ref=outs[a].at[k],
                                              send_sem=send_sems.at[3 * a + k], recv_sem=recv_sems.at[3 * a + k],
                                              device_id=(px, py, c), device_id_type=MESH)
                 for a in range(len(grads)) for k, (px, py) in enumerate(chips)]
        return [], sends, sends

    return Ride(grads, [jax.ShapeDtypeStruct((3,) + g.shape[1:], g.dtype) for g in grads], copies)


def swap_ride(arrs):
    def copies(ins, outs, send_sems, recv_sems, local_sems):
        x, y, c, _ = _place()
        sends = [pltpu.make_async_remote_copy(src_ref=ins[a], dst_ref=outs[a], send_sem=send_sems.at[a],
                                              recv_sem=recv_sems.at[a], device_id=(x, y, 1 - c), device_id_type=MESH)
                 for a in range(len(arrs))]
        return [], sends, sends

    return Ride(arrs, [jax.ShapeDtypeStruct(a.shape, a.dtype) for a in arrs], copies)


def all_reduce_small(v, *, name):
    rows = v.shape[0]

    def body(v_ref, o_ref, gath, send_sems, recv_sems):
        x, y, c, _ = _place()
        me = 4 * x + 2 * y + c
        gath[me] = v_ref[...]
        sends = []
        for msk in range(1, N_DEV):
            peer = (x ^ (msk >> 2), y ^ ((msk >> 1) & 1), c ^ (msk & 1))
            cp = pltpu.make_async_remote_copy(src_ref=v_ref, dst_ref=gath.at[me], send_sem=send_sems.at[msk - 1],
                                              recv_sem=recv_sems.at[msk - 1], device_id=peer, device_id_type=MESH)
            cp.start()
            sends.append(cp)
        for msk in range(1, N_DEV):
            pltpu.make_async_remote_copy(src_ref=v_ref, dst_ref=gath.at[me ^ msk], send_sem=send_sems.at[msk - 1],
                                         recv_sem=recv_sems.at[msk - 1], device_id=(x, y, c), device_id_type=MESH).wait_recv()
        for cp in sends:
            cp.wait_send()
        tot = gath[0]
        for q in range(1, N_DEV):
            tot = tot + gath[q]
        o_ref[...] = tot

    vm = pl.BlockSpec(memory_space=pltpu.VMEM)
    return pl.pallas_call(
        body, name=name, in_specs=[vm], out_specs=vm, out_shape=jax.ShapeDtypeStruct(v.shape, F32),
        scratch_shapes=[pltpu.VMEM((N_DEV, rows, LANES), F32), pltpu.SemaphoreType.DMA((N_DEV - 1,)),
                        pltpu.SemaphoreType.DMA((N_DEV - 1,))],
    )(v)


def _ffn_fwd(xin, gain, w_up, w_conv, w_down, bl, tag, rides=(None, None, None)):
    got = {}

    def carried(k, result, delivered):
        if rides[k] is not None:
            got.update(zip(rides[k][1], delivered))
        return result

    n = rmsnorm_fwd(xin, gain, name=f"{tag}_ffn_norm")
    if rides[0] is None:
        u = matmul(n, w_up, name=f"{tag}_ffn_up")
    else:
        u = carried(0, *matmul(n, w_up, ride=rides[0][0], name=f"{tag}_ffn_up"))
    *hc, delivered = conv_silu_fwd(u, w_conv, bl, name=f"{tag}_ffn_conv", ride=rides[1][0] if rides[1] else None)
    h, ca, cg = carried(1, hc, delivered)
    if rides[2] is None:
        xout = matmul(h, w_down, out_dtype=F32, add=xin, name=f"{tag}_ffn_down")
    else:
        xout = carried(2, *matmul(h, w_down, out_dtype=F32, add=xin, ride=rides[2][0], name=f"{tag}_ffn_down"))
    return xout, (n, u, h, ca, cg), got


def _ffn_bwd(dxout, dxout_b, xin, saved, gain, w_up, w_conv, w_down, bl, tag, shard):
    n, u, h, ca, cg = saved
    dh = matmul(dxout_b, w_down, tb=True, name=f"{tag}_ffn_down_dx")
    g_down = matmul(h, dxout_b, ta=True, name=f"{tag}_ffn_down_dw")
    dua, dug, g_conv = conv_silu_bwd(u, ca, cg, w_conv, dh, bl, name=f"{tag}_ffn_conv_bwd")
    dn = matmul(dua, w_up, a2=dug, tb=True, out_dtype=F32, name=f"{tag}_ffn_up_dx")
    g_up = matmul(n, dua, b2=dug, ta=True, shard_cols=shard, name=f"{tag}_ffn_up_dw")
    dxin, dxin_b, g_norm = rmsnorm_bwd(xin, gain, dn, dxout, name=f"{tag}_ffn_norm_bwd")
    return dxin, dxin_b, g_norm, g_up, g_conv, g_down


def _mem_kv(mem2, gain, w_kv, tag):
    mn = rmsnorm_fwd(mem2, gain, name=f"{tag}_mem_norm")
    return mn, matmul(mn, w_kv, name=f"{tag}_mem_kv")


def _mem_kv_bwd(mem2, mn, gain, w_kv, dkvm, tag):
    g_kv = matmul(mn, dkvm, ta=True, name=f"{tag}_mem_kv_dw")
    dmn = matmul(dkvm, w_kv, tb=True, out_dtype=F32, name=f"{tag}_mem_kv_dx")
    _, _, g_norm = rmsnorm_bwd(mem2, gain, dmn, None, name=f"{tag}_mem_norm_bwd")
    return g_norm, g_kv


def _shard_major(name, grad):
    if name in COL_SHARDED:
        return grad
    return grad.reshape(N_CHIPS, grad.shape[0] // N_CHIPS, grad.shape[1])


def local_step(x, mem, target, w, shard_of, late_gather=None, early_scatter=(), scatter_last=(), reduce_early=None):
    bl, s, d = x.shape
    t = bl * s
    x0 = x.reshape(t, d)
    mem2 = mem.reshape(bl * MEM_LEN, d)
    tgt = target.reshape(t, d)
    g = {}

    n1 = rmsnorm_fwd(x0, w["a_norm_attn"], name="a_attn_norm")
    proj_a = matmul(n1, w["a_w_in"], name="a_w_in")
    mn_a, kvm_a = _mem_kv(mem2, w["a_norm_mem"], w["a_w_mem_kv"], "a")
    late_gather = late_gather or {}
    behind_sb = late_gather.get("sb")
    o_sb, o_sb_f32, delivered = sb_fwd(proj_a, bl, name="a_sb_fwd", ride=behind_sb[0] if behind_sb else None)
    if behind_sb:
        w = {**w, **dict(zip(behind_sb[1], delivered))}
    o_mem_a = mem_fwd(proj_a, kvm_a, bl, name="a_mem_fwd")
    cat_a = jnp.concatenate([o_sb, o_mem_a], axis=1)
    x1 = matmul(cat_a, w["a_w_out"], out_dtype=F32, add=x0, name="a_w_out")
    x2, ffn_a, got = _ffn_fwd(x1, w["a_norm_ffn"], w["a_ffn_up"], w["a_ffn_conv"], w["a_ffn_down"], bl, "a",
                              rides=tuple(late_gather.get(host) for host in ("up", "conv", "down")))
    w = {**w, **got}

    nkv, n3 = rmsnorm_fwd2(x2, w["kv_norm"], w["b_norm_attn"], name="kv_b_attn_norm")
    kv = matmul(nkv, w["w_kv_shared"], out_dtype=F32, name="w_kv")
    proj_b = matmul(n3, w["b_w_in"], out_dtype=F32, name="b_w_in")
    mn_b, kvm_b = _mem_kv(mem2, w["b_norm_mem"], w["b_w_mem_kv"], "b")
    dil = [dil_fwd(proj_b, kv, gi, bl, name=f"b_dil_fwd{gi}") for gi in range(3)]
    os_, lses = [o for o, _ in dil], [l for _, l in dil]
    o_dil = dil_combine_fwd(os_, lses, name="b_dil_combine")
    o_mem_b = mem_fwd(proj_b, kvm_b, bl, name="b_mem_fwd")
    cat_b = jnp.concatenate([o_dil, o_mem_b], axis=1)
    x3 = matmul(cat_b, w["b_w_out"], out_dtype=F32, add=x2, name="b_w_out")
    x4, ffn_b, _ = _ffn_fwd(x3, w["b_norm_ffn"], w["b_ffn_up"], w["b_ffn_conv"], w["b_ffn_down"], bl, "b")

    dx4, dx4b, g["final_norm"], lossvec = final_loss(x4, w["final_norm"], tgt, name="final_loss")

    dx3, dx3b, g["b_norm_ffn"], g["b_ffn_up"], g["b_ffn_conv"], g["b_ffn_down"] = _ffn_bwd(
        dx4, dx4b, x3, ffn_b, w["b_norm_ffn"], w["b_ffn_up"], w["b_ffn_conv"], w["b_ffn_down"], bl, "b",
        shard_of["b_ffn_up"])
    dcat_b = matmul(dx3b, w["b_w_out"], tb=True, name="b_w_out_dx")
    g["b_w_out"] = matmul(cat_b, dx3b, ta=True, name="b_w_out_dw")
    dog, dshift = dil_combine_bwd(os_, lses, dcat_b, name="b_dil_combine_bwd")
    dqs, dks, dvs = [], [], []
    for gi in range(3):
        dq_g, dk_g, dv_g = dil_bwd(proj_b, kv, lses[gi], dog, dshift, gi, bl, name=f"b_dil_bwd{gi}")
        dqs.append(dq_g), dks.append(dk_g), dvs.append(dv_g)
    dq_mem_b, dkvm_b = mem_bwd(proj_b, kvm_b, dcat_b, bl, name="b_mem_bwd")
    g["b_norm_mem"], g["b_w_mem_kv"] = _mem_kv_bwd(mem2, mn_b, w["b_norm_mem"], w["b_w_mem_kv"], dkvm_b, "b")
    dproj_b = jnp.concatenate([dq_g.astype(BF) for dq_g in dqs] + [dq_mem_b], axis=1)
    dn3 = matmul(dproj_b, w["b_w_in"], tb=True, out_dtype=F32, name="b_w_in_dx")
    g["b_w_in"] = matmul(n3, dproj_b, ta=True, name="b_w_in_dw")
    dkv = jnp.concatenate(dks + dvs, axis=1).astype(BF)
    dnkv = matmul(dkv, w["w_kv_shared"], tb=True, out_dtype=F32, name="w_kv_dx")
    g["w_kv_shared"] = matmul(nkv, dkv, ta=True, shard_cols=shard_of["w_kv_shared"], name="w_kv_dw")
    dx2, dx2b, g["kv_norm"], g["b_norm_attn"] = rmsnorm_bwd2(x2, w["kv_norm"], w["b_norm_attn"], dnkv, dn3, dx3,
                                                             name="kv_b_attn_norm_bwd")

    dx1, dx1b, g["a_norm_ffn"], g["a_ffn_up"], g["a_ffn_conv"], g["a_ffn_down"] = _ffn_bwd(
        dx2, dx2b, x1, ffn_a, w["a_norm_ffn"], w["a_ffn_up"], w["a_ffn_conv"], w["a_ffn_down"], bl, "a",
        shard_of["a_ffn_up"])
    dcat_a = matmul(dx1b, w["a_w_out"], tb=True, name="a_w_out_dx")
    g["a_w_out"] = matmul(cat_a, dx1b, ta=True, name="a_w_out_dw")
    leaving = scatter_ride([_shard_major(n, g[n]) for n in early_scatter]) if early_scatter else None
    dq_sb, dk_sb, dv_sb, landed = sb_bwd(proj_a, o_sb_f32, dcat_a, bl, name="a_sb_bwd", ride=leaving)
    landed = dict(zip(early_scatter, landed))
    dq_mem_a, dkvm_a = mem_bwd(proj_a, kvm_a, dcat_a, bl, name="a_mem_bwd")
    g["a_norm_mem"], g["a_w_mem_kv"] = _mem_kv_bwd(mem2, mn_a, w["a_norm_mem"], w["a_w_mem_kv"], dkvm_a, "a")
    dproj_a = jnp.concatenate([dq_sb, dk_sb.astype(BF), dv_sb.astype(BF), dq_mem_a], axis=1)
    if reduce_early is None:
        dn1 = matmul(dproj_a, w["a_w_in"], tb=True, out_dtype=F32, name="a_w_in_dx")
        g["a_w_in"] = matmul(n1, dproj_a, ta=True, shard_cols=shard_of["a_w_in"], name="a_w_in_dw")
        dx0, _, g["a_norm_attn"] = rmsnorm_bwd(x0, w["a_norm_attn"], dn1, dx1, name="a_attn_norm_bwd")
        return lossvec, dx0, g, landed
    sums = reduce_early(g, landed)
    dn1, theirs = matmul(dproj_a, w["a_w_in"], tb=True, out_dtype=F32, name="a_w_in_dx",
                         ride=swap_ride([sums[n] for n in early_scatter]))
    g["a_w_in"] = matmul(n1, dproj_a, ta=True, shard_cols=shard_of["a_w_in"], name="a_w_in_dw")
    dx0, _, g["a_norm_attn"], landed_last = rmsnorm_bwd(
        x0, w["a_norm_attn"], dn1, dx1, name="a_attn_norm_bwd",
        ride=scatter_ride([_shard_major(n, g[n]) for n in scatter_last]))
    return lossvec, dx0, g, (sums, dict(zip(early_scatter, theirs)), dict(zip(scatter_last, landed_last)))


MATRICES = ("a_w_in", "a_w_out", "a_w_mem_kv", "a_ffn_up", "a_ffn_down", "w_kv_shared", "b_w_in", "b_w_out",
            "b_w_mem_kv", "b_ffn_up", "b_ffn_down")
COL_SHARDED = ("a_w_in", "a_ffn_up", "w_kv_shared", "b_ffn_up")
FIRST_NEEDED = ("a_w_in", "a_w_mem_kv")
GATHER_BEHIND = {"sb": ("a_w_out", "a_ffn_up", "a_ffn_down", "b_ffn_up"), "up": ("b_ffn_down", "w_kv_shared"),
                 "conv": ("b_w_in", "b_w_out", "b_w_mem_kv")}
SMALL_SHARDED = ("a_norm_attn", "a_norm_mem", "a_norm_ffn", "a_ffn_conv", "b_ffn_conv")
SMALL_REPLICATED = ("kv_norm", "b_norm_attn", "b_norm_mem", "b_norm_ffn", "final_norm")
WEIGHTS = ("a_norm_attn", "a_w_in", "a_w_out", "a_norm_mem", "a_w_mem_kv", "a_norm_ffn", "a_ffn_up", "a_ffn_conv",
           "a_ffn_down", "kv_norm", "w_kv_shared", "b_norm_attn", "b_w_in", "b_w_out", "b_norm_mem", "b_w_mem_kv",
           "b_norm_ffn", "b_ffn_up", "b_ffn_conv", "b_ffn_down", "final_norm")


def _two_d(a):
    if a.ndim == 1:
        return a.reshape(1, -1)
    return a.reshape(a.shape[-2], a.shape[-1])


def kernel(x, mem, a_norm_attn, a_w_in, a_w_out, a_norm_mem, a_w_mem_kv, a_norm_ffn, a_ffn_up, a_ffn_conv, a_ffn_down, kv_norm, w_kv_shared, b_norm_attn, b_w_in, b_w_out, b_norm_mem, b_w_mem_kv, b_norm_ffn, b_ffn_up, b_ffn_conv, b_ffn_down, final_norm, loss_target, m_a_norm_attn, m_a_w_in, m_a_w_out, m_a_norm_mem, m_a_w_mem_kv, m_a_norm_ffn, m_a_ffn_up, m_a_ffn_conv, m_a_ffn_down, m_kv_norm, m_w_kv_shared, m_b_norm_attn, m_b_w_in, m_b_w_out, m_b_norm_mem, m_b_w_mem_kv, m_b_norm_ffn, m_b_ffn_up, m_b_ffn_conv, m_b_ffn_down, m_final_norm, v_a_norm_attn, v_a_w_in, v_a_w_out, v_a_norm_mem, v_a_w_mem_kv, v_a_norm_ffn, v_a_ffn_up, v_a_ffn_conv, v_a_ffn_down, v_kv_norm, v_w_kv_shared, v_b_norm_attn, v_b_w_in, v_b_w_out, v_b_norm_mem, v_b_w_mem_kv, v_b_norm_ffn, v_b_ffn_up, v_b_ffn_conv, v_b_ffn_down, v_final_norm):
    given = dict(locals())
    wl = {n: _two_d(given[n]) for n in WEIGHTS}
    ml = {n: _two_d(given["m_" + n]) for n in WEIGHTS}
    vl = {n: _two_d(given["v_" + n]) for n in WEIGHTS}
    chip = 2 * lax.axis_index("x") + lax.axis_index("y")

    packed = jnp.concatenate([wl[n].reshape(-1, LANES) for n in SMALL_SHARDED], axis=0)
    axis_of = lambda n: 1 if n in COL_SHARDED else 0
    late = tuple(n for n in MATRICES if n not in FIRST_NEEDED)
    full = gather_ride([wl[n].astype(BF) for n in FIRST_NEEDED] + [packed],
                       [axis_of(n) for n in FIRST_NEEDED] + [0]).run("gather_first")
    w = dict(zip(FIRST_NEEDED, full[:-1]))
    late_gather = {host: (gather_ride([wl[n].astype(BF) for n in names], [axis_of(n) for n in names]), names)
                   for host, names in GATHER_BEHIND.items()}
    rows = packed.shape[0]
    per_chip = full[-1].reshape(N_CHIPS, rows, LANES)
    r0 = 0
    for n in SMALL_SHARDED:
        nr = wl[n].size // LANES
        piece = per_chip[:, r0:r0 + nr].reshape(N_CHIPS, wl[n].shape[0], wl[n].shape[1])
        w[n] = jnp.concatenate([piece[q] for q in range(N_CHIPS)], axis=1)
        r0 += nr
    for n in SMALL_REPLICATED:
        w[n] = wl[n]

    shard_of = {n: wl[n].shape[1] for n in COL_SHARDED}
    def core_sums(g, landed):
        return {n: sum4(lax.dynamic_index_in_dim(_shard_major(n, g[n]), chip, 0, keepdims=False), landed[n],
                        name=f"sum4_{n}") for n in landed}

    lossvec, dx0, g, (sums, theirs, landed_last) = local_step(
        x, mem, loss_target, w, shard_of, late_gather=late_gather, early_scatter=late, scatter_last=FIRST_NEEDED,
        reduce_early=core_sums)
    loss = lax.psum(0.5 * jnp.sum(lossvec) / x.shape[-1], ("x", "y", "c"))
    sums.update(core_sums(g, landed_last))
    theirs.update(zip(FIRST_NEEDED, swap_ride([sums[n] for n in FIRST_NEEDED]).run("swap_last")))
    out = {}
    for n in MATRICES:
        out[n] = adamw(wl[n], sums[n], theirs[n], ml[n], vl[n], name=f"adamw_{n}")

    small = SMALL_SHARDED + SMALL_REPLICATED
    flat = jnp.concatenate([g[n].reshape(-1, LANES) for n in small], axis=0)
    tot = all_reduce_small(flat, name="all_reduce_small")
    r0 = 0
    for n in small:
        nr = g[n].size // LANES
        gn = tot[r0:r0 + nr].reshape(g[n].shape)
        r0 += nr
        if n in SMALL_SHARDED:
            gn = lax.dynamic_slice_in_dim(gn, chip * wl[n].shape[1], wl[n].shape[1], axis=1)
        out[n] = adamw(wl[n], gn, None, ml[n], vl[n], name=f"adamw_{n}")

    res = [loss, dx0.reshape(x.shape)]
    for slot in range(4):
        res += [out[n][slot].reshape(given[n].shape) for n in WEIGHTS]
    return tuple(res)
```

```python
import functools
import math

import jax
import jax.numpy as jnp
from jax import lax
from jax.experimental import pallas as pl
from jax.experimental.pallas import tpu as pltpu

F32 = jnp.float32
BF = jnp.bfloat16
MESH = pl.DeviceIdType.MESH

HEAD_DIM = 64
LANES = 128
SB_WIDTH = 12 * HEAD_DIM
MEM_WIDTH = 4 * HEAD_DIM
DIL_WIDTH = 12 * HEAD_DIM
MEM_LEN = 256
DIL_GROUPS = ((128, 1), (512, 4), (2048, 16))
QB = 128
EPS = 1e-6
SCALE = HEAD_DIM ** -0.5
NEG = -1e30
ALIBI = tuple(2.0 ** (-8.0 * i / 12) for i in range(1, 13))
N_CHIPS = 4
N_DEV = 8

ADAM_LR, ADAM_B1, ADAM_B2, ADAM_EPS, ADAM_WD, ADAM_STEP = 0.001, 0.9, 0.999, 1e-08, 0.01, 10

VMEM_LIMIT = 52 * 1024 * 1024
MATMUL_VMEM_BUDGET = 44 * 1024 * 1024
MATMUL_MAX_TN = 2816
HBM_BYTES_PER_S = 3.0e12
VMEM_BYTES_PER_S = 6.0e12
STEP_OVERHEAD_S = 0.4e-6


def _mo(v, m):
    return v if isinstance(v, int) else pl.multiple_of(v, m)


def _pick(n, prefs):
    for t in prefs:
        if n % t == 0:
            return t
    return n


def _params(*sem):
    return pltpu.CompilerParams(dimension_semantics=sem, vmem_limit_bytes=VMEM_LIMIT)


def matmul(a, b, *, ta=False, tb=False, out_dtype=BF, add=None, shard_cols=0, a2=None, b2=None, ride=None, name):
    m, k = (a.shape[1], a.shape[0]) if ta else a.shape
    n = b.shape[0] if tb else b.shape[1]
    if a2 is not None:
        assert not ta and a2.shape == a.shape
        k *= 2
    if b2 is not None:
        assert not tb and b2.shape == b.shape
        n *= 2
    k_piece = a.shape[1] if a2 is not None else k
    n_piece = b.shape[1] if b2 is not None else n
    out_bytes = jnp.dtype(out_dtype).itemsize
    pieces_a, pieces_b = (2 if a2 is not None else 1), (2 if b2 is not None else 1)
    if shard_cols:
        tn_choices = [c * shard_cols for c in (4, 2, 1) if c * shard_cols <= MATMUL_MAX_TN and c <= 4 // pieces_b]
    else:
        tn_choices = [c for c in (2816, 2560, 1536, 1408, 1024, 512, 256, 128) if n_piece % c == 0]

    def vmem_bytes(tm_, tn_, tk_):
        blocks = (tm_ * tk_ * a.dtype.itemsize * pieces_a + tk_ * tn_ * b.dtype.itemsize * pieces_b
                  + tm_ * tn_ * out_bytes + (tm_ * tn_ * 4 if add is not None else 0))
        return 2 * blocks + (tm_ * tn_ * 4 if k // tk_ > 1 else 0) + tm_ * tn_ * 4

    def cost(c):
        tm_, tn_, tk_ = c
        steps = (m // tm_) * (n // tn_) * (k // tk_)
        hbm = (m * k * a.dtype.itemsize * (n // tn_) + k * n * b.dtype.itemsize * (m // tm_)
               + m * n * (out_bytes + (4 if add is not None else 0)))
        return hbm / HBM_BYTES_PER_S + steps * STEP_OVERHEAD_S + m * n * 8 * (k // tk_ - 1) / VMEM_BYTES_PER_S

    tm, tn, tk = min(((tm_, tn_, tk_) for tm_ in (1024, 512, 1408, 256, 128) if m % tm_ == 0 for tn_ in tn_choices
                      for tk_ in (2816, 2560, 2048, 1536, 1024, 512, 256, 128) if k_piece % tk_ == 0
                      if vmem_bytes(tm_, tn_, tk_) <= MATMUL_VMEM_BUDGET), key=cost)
    ns = tn // shard_cols if shard_cols else 1
    nk = k // tk
    nk1 = nk // 2
    nj1 = (n // tn) // 2
    dims = (((0,) if ta else (1,), (1,) if tb else (0,)), ((), ()))
    n_in = 2 + (a2 is not None) + (b2 is not None) + (add is not None)
    grid = (m // tm, n // tn, nk)

    def body(*refs):
        own_in, (o_ref,), scr, ride_start, ride_wait = _ride_hooks(ride, refs, n_in, 1, int(nk > 1), grid)
        ride_start()
        ins = list(own_in)
        a_ref, b_ref = ins.pop(0), ins.pop(0)
        a2_ref = ins.pop(0) if a2 is not None else None
        b2_ref = ins.pop(0) if b2 is not None else None
        add_ref = ins.pop(0) if add is not None else None

        a_tile, b_tile = a_ref[...], b_ref[...]
        if a2 is not None:
            a_tile = jnp.where(pl.program_id(2) < nk1, a_tile, a2_ref[...])
        if b2 is not None:
            b_tile = jnp.where(pl.program_id(1) < nj1, b_tile, b2_ref[...])
        part = lax.dot_general(a_tile.astype(BF), b_tile.astype(BF), dims, preferred_element_type=F32)

        def finish(r):
            if add is not None:
                r = r + add_ref[...]
            if shard_cols:
                for c in range(ns):
                    o_ref[c] = r[:, c * shard_cols:(c + 1) * shard_cols].astype(o_ref.dtype)
            else:
                o_ref[...] = r.astype(o_ref.dtype)

        if nk == 1:
            finish(part)
        else:
            acc_ref = scr[0]
            kk = pl.program_id(2)

            @pl.when(kk == 0)
            def _():
                acc_ref[...] = part

            @pl.when(kk > 0)
            def _():
                acc_ref[...] += part

            @pl.when(kk == nk - 1)
            def _():
                finish(acc_ref[...])
        ride_wait()

    a_spec = pl.BlockSpec((tk, tm), lambda i, j, q: (q, i)) if ta else pl.BlockSpec((tm, tk), lambda i, j, q: (i, q))
    b_spec = pl.BlockSpec((tn, tk), lambda i, j, q: (j, q)) if tb else pl.BlockSpec((tk, tn), lambda i, j, q: (q, j))
    if a2 is not None:
        a_spec = pl.BlockSpec((tm, tk), lambda i, j, q: (i, jnp.minimum(q, nk1 - 1)))
    if b2 is not None:
        b_spec = pl.BlockSpec((tk, tn), lambda i, j, q: (jnp.where(j < nj1, q, 0), jnp.minimum(j, nj1 - 1)))
    in_specs = [a_spec, b_spec]
    args = [a, b]
    if a2 is not None:
        in_specs.append(pl.BlockSpec((tm, tk), lambda i, j, q: (i, jnp.maximum(q - nk1, 0))))
        args.append(a2)
    if b2 is not None:
        in_specs.append(pl.BlockSpec((tk, tn), lambda i, j, q: (jnp.where(j < nj1, 0, q), jnp.maximum(j - nj1, 0))))
        args.append(b2)
    if add is not None:
        in_specs.append(pl.BlockSpec((tm, tn), lambda i, j, q: (i, j)))
        args.append(add)
    if shard_cols:
        out_shape = jax.ShapeDtypeStruct((N_CHIPS, m, shard_cols), out_dtype)
        out_spec = pl.BlockSpec((ns, tm, shard_cols), lambda i, j, q: (j, i, 0))
    else:
        out_shape = jax.ShapeDtypeStruct((m, n), out_dtype)
        out_spec = pl.BlockSpec((tm, tn), lambda i, j, q: (i, j))
    own_scratch = [pltpu.VMEM((tm, tn), F32)] if nk > 1 else []
    if ride is None:
        return pl.pallas_call(
            body, name=name, grid=grid, in_specs=in_specs, out_specs=out_spec, out_shape=out_shape,
            scratch_shapes=own_scratch, compiler_params=_params("parallel", "parallel", "arbitrary"),
        )(*args)
    out, *delivered = pl.pallas_call(
        body, name=name, grid=grid, in_specs=in_specs + [ANY] * len(ride.arrays),
        out_specs=[out_spec] + [ANY] * len(ride.out_shape), out_shape=[out_shape] + ride.out_shape,
        scratch_shapes=own_scratch + ride.scratch, compiler_params=_params("arbitrary", "arbitrary", "arbitrary"),
    )(*args, *ride.arrays)
    return out, delivered


def rmsnorm_fwd(x, g, *, name):
    t, d = x.shape
    tr = _pick(t, (512, 256))

    def body(x_ref, g_ref, o_ref):
        xv = x_ref[...]
        r = lax.rsqrt(jnp.mean(xv * xv, axis=-1, keepdims=True) + EPS)
        o_ref[...] = ((xv * r) * g_ref[...]).astype(o_ref.dtype)

    return pl.pallas_call(
        body, name=name, grid=(t // tr,),
        in_specs=[pl.BlockSpec((tr, d), lambda i: (i, 0)), pl.BlockSpec((1, d), lambda i: (0, 0))],
        out_specs=pl.BlockSpec((tr, d), lambda i: (i, 0)), out_shape=jax.ShapeDtypeStruct((t, d), BF),
        compiler_params=_params("parallel"),
    )(x, g)


def rmsnorm_bwd(x, g, dn, dres, *, name, ride=None):
    t, d = x.shape
    tr = _pick(t, (512, 256))
    want_dx = dres is not None

    def body(*refs):
        if want_dx:
            (x_ref, g_ref, dn_ref, dres_ref), (dx_ref, dxb_ref, dg_ref), _, ride_start, ride_wait = _ride_hooks(
                ride, refs, 4, 3, 0, (t // tr,))
            ride_start()
        else:
            x_ref, g_ref, dn_ref, dg_ref = refs

        @pl.when(pl.program_id(0) == 0)
        def _():
            dg_ref[...] = jnp.zeros_like(dg_ref)

        xv = x_ref[...]
        r = lax.rsqrt(jnp.mean(xv * xv, axis=-1, keepdims=True) + EPS)
        xn = xv * r
        dnv = dn_ref[...].astype(F32)
        dg_ref[...] += jnp.sum(dnv * xn, axis=0, keepdims=True)
        if want_dx:
            dyg = dnv * g_ref[...]
            cm = jnp.mean(dyg * xn, axis=-1, keepdims=True)
            dx = dres_ref[...] + r * (dyg - xn * cm)
            dx_ref[...] = dx
            dxb_ref[...] = dx.astype(BF)
            ride_wait()

    row = pl.BlockSpec((tr, d), lambda i: (i, 0))
    vec = pl.BlockSpec((1, d), lambda i: (0, 0))
    if want_dx:
        extra = ride.arrays if ride else []
        dx, dxb, dg, *delivered = pl.pallas_call(
            body, name=name, grid=(t // tr,), in_specs=[row, vec, row, row] + [ANY] * len(extra),
            out_specs=[row, row, vec] + [ANY] * (len(ride.out_shape) if ride else 0),
            out_shape=[jax.ShapeDtypeStruct((t, d), F32), jax.ShapeDtypeStruct((t, d), BF), jax.ShapeDtypeStruct((1, d), F32)]
            + (ride.out_shape if ride else []),
            scratch_shapes=ride.scratch if ride else [], compiler_params=_params("arbitrary"),
        )(x, g, dn, dres, *extra)
        return (dx, dxb, dg, delivered) if ride else (dx, dxb, dg)
    return None, None, pl.pallas_call(
        body, name=name, grid=(t // tr,), in_specs=[row, vec, row], out_specs=vec,
        out_shape=jax.ShapeDtypeStruct((1, d), F32), compiler_params=_params("arbitrary"),
    )(x, g, dn)


def rmsnorm_fwd2(x, g1, g2, *, name):
    t, d = x.shape
    tr = _pick(t, (512, 256))

    def body(x_ref, g1_ref, g2_ref, o1_ref, o2_ref):
        xv = x_ref[...]
        xn = xv * lax.rsqrt(jnp.mean(xv * xv, axis=-1, keepdims=True) + EPS)
        o1_ref[...] = (xn * g1_ref[...]).astype(o1_ref.dtype)
        o2_ref[...] = (xn * g2_ref[...]).astype(o2_ref.dtype)

    row = pl.BlockSpec((tr, d), lambda i: (i, 0))
    vec = pl.BlockSpec((1, d), lambda i: (0, 0))
    return pl.pallas_call(
        body, name=name, grid=(t // tr,), in_specs=[row, vec, vec], out_specs=[row, row],
        out_shape=[jax.ShapeDtypeStruct((t, d), BF)] * 2, compiler_params=_params("parallel"),
    )(x, g1, g2)


def rmsnorm_bwd2(x, g1, g2, dn1, dn2, dres, *, name):
    t, d = x.shape
    tr = _pick(t, (512, 256))

    def body(x_ref, g1_ref, g2_ref, dn1_ref, dn2_ref, dres_ref, dx_ref, dxb_ref, dg1_ref, dg2_ref):
        @pl.when(pl.program_id(0) == 0)
        def _():
            dg1_ref[...] = jnp.zeros_like(dg1_ref)
            dg2_ref[...] = jnp.zeros_like(dg2_ref)

        xv = x_ref[...]
        r = lax.rsqrt(jnp.mean(xv * xv, axis=-1, keepdims=True) + EPS)
        xn = xv * r
        d1, d2 = dn1_ref[...].astype(F32), dn2_ref[...].astype(F32)
        dg1_ref[...] += jnp.sum(d1 * xn, axis=0, keepdims=True)
        dg2_ref[...] += jnp.sum(d2 * xn, axis=0, keepdims=True)
        dyg = d1 * g1_ref[...] + d2 * g2_ref[...]
        cm = jnp.mean(dyg * xn, axis=-1, keepdims=True)
        dx = dres_ref[...] + r * (dyg - xn * cm)
        dx_ref[...] = dx
        dxb_ref[...] = dx.astype(BF)

    row = pl.BlockSpec((tr, d), lambda i: (i, 0))
    vec = pl.BlockSpec((1, d), lambda i: (0, 0))
    return pl.pallas_call(
        body, name=name, grid=(t // tr,), in_specs=[row, vec, vec, row, row, row], out_specs=[row, row, vec, vec],
        out_shape=[jax.ShapeDtypeStruct((t, d), F32), jax.ShapeDtypeStruct((t, d), BF), jax.ShapeDtypeStruct((1, d), F32),
                   jax.ShapeDtypeStruct((1, d), F32)],
        compiler_params=_params("arbitrary"),
    )(x, g1, g2, dn1, dn2, dres)


def final_loss(x, g, target, *, name):
    t, d = x.shape
    tr = _pick(t, (512, 256))

    def body(x_ref, g_ref, t_ref, dx_ref, dxb_ref, dg_ref, lv_ref):
        @pl.when(pl.program_id(0) == 0)
        def _():
            dg_ref[...] = jnp.zeros_like(dg_ref)
            lv_ref[...] = jnp.zeros_like(lv_ref)

        xv = x_ref[...]
        r = lax.rsqrt(jnp.mean(xv * xv, axis=-1, keepdims=True) + EPS)
        xn = xv * r
        err = xn * g_ref[...] - t_ref[...]
        lv_ref[...] += jnp.sum(err * err, axis=0, keepdims=True)
        dy = err * (1.0 / d)
        dg_ref[...] += jnp.sum(dy * xn, axis=0, keepdims=True)
        dyg = dy * g_ref[...]
        cm = jnp.mean(dyg * xn, axis=-1, keepdims=True)
        dx = r * (dyg - xn * cm)
        dx_ref[...] = dx
        dxb_ref[...] = dx.astype(BF)

    row = pl.BlockSpec((tr, d), lambda i: (i, 0))
    vec = pl.BlockSpec((1, d), lambda i: (0, 0))
    return pl.pallas_call(
        body, name=name, grid=(t // tr,), in_specs=[row, vec, row], out_specs=[row, row, vec, vec],
        out_shape=[jax.ShapeDtypeStruct((t, d), F32), jax.ShapeDtypeStruct((t, d), BF), jax.ShapeDtypeStruct((1, d), F32),
                   jax.ShapeDtypeStruct((1, d), F32)],
        compiler_params=_params("arbitrary"),
    )(x, g, target)


CONV_ROWS = 256
HALO = 16


def _conv_taps(ext, w, rows):
    s0 = ext[HALO:HALO + rows]
    s1 = pltpu.roll(ext, 1, 0)[HALO:HALO + rows]
    s2 = pltpu.roll(ext, 2, 0)[HALO:HALO + rows]
    return (w[0:1] * s2 + w[1:2] * s1) + w[2:3] * s0, s0, s1, s2


def conv_silu_fwd(u, w, bl, *, name, ride=None):
    t, f2 = u.shape
    f = f2 // 2
    s = t // bl
    tc = _pick(f, (256, 128))
    nf = f // tc
    nch = s // CONV_ROWS

    def body(*refs):
        (ua_ref, ug_ref, wa_ref, wg_ref), (h_ref, ca_ref, cg_ref), _, ride_start, ride_wait = _ride_hooks(
            ride, refs, 4, 3, 0, (nf, bl))
        ride_start()
        wa, wg = wa_ref[...], wg_ref[...]

        def chunk(ci, carry):
            r0 = pl.multiple_of(ci * CONV_ROWS, CONV_ROWS)
            ps = pl.multiple_of(jnp.maximum(r0 - HALO, 0), HALO)
            keep = (ci > 0).astype(F32)
            rows = pl.ds(r0, CONV_ROWS)

            def conv(ref, wv):
                ext = jnp.concatenate([ref[0, pl.ds(ps, HALO), :].astype(F32) * keep, ref[0, rows, :].astype(F32)], axis=0)
                return _conv_taps(ext, wv, CONV_ROWS)[0]

            ca, cg = conv(ua_ref, wa), conv(ug_ref, wg)
            sg = pl.reciprocal(1.0 + jnp.exp(-cg), approx=True)
            h_ref[0, rows, :] = ((cg * sg) * ca).astype(h_ref.dtype)
            ca_ref[0, rows, :] = ca.astype(ca_ref.dtype)
            cg_ref[0, rows, :] = cg.astype(cg_ref.dtype)
            return carry

        lax.fori_loop(0, nch, chunk, 0)
        ride_wait()

    u3 = u.reshape(bl, s, f2)
    blk = lambda off: pl.BlockSpec((1, s, tc), lambda j, b: (b, 0, j + off))
    wblk = lambda off: pl.BlockSpec((3, tc), lambda j, b: (0, j + off))
    extra = ride.arrays if ride else []
    h, ca, cg, *delivered = pl.pallas_call(
        body, name=name, grid=(nf, bl), in_specs=[blk(0), blk(nf), wblk(0), wblk(nf)] + [ANY] * len(extra),
        out_specs=[blk(0)] * 3 + [ANY] * (len(ride.out_shape) if ride else 0),
        out_shape=[jax.ShapeDtypeStruct((bl, s, f), BF)] * 3 + (ride.out_shape if ride else []),
        scratch_shapes=ride.scratch if ride else [],
        compiler_params=_params(*(("arbitrary",) * 2 if ride else ("parallel",) * 2)),
    )(u3, u3, w, w, *extra)
    return h.reshape(t, f), ca.reshape(t, f), cg.reshape(t, f), delivered


def conv_silu_bwd(u, ca, cg, w, dh, bl, *, name):
    t, f2 = u.shape
    f = f2 // 2
    s = t // bl
    tc = _pick(f, (256, 128))
    nf = f // tc
    nch = s // CONV_ROWS
    ext_rows = CONV_ROWS + HALO

    def body(ua_ref, ug_ref, ca_ref, cg_ref, wa_ref, wg_ref, dh_ref, dua_ref, dug_ref, dwa_ref, dwg_ref):
        wa, wg = wa_ref[...], wg_ref[...]

        @pl.when(pl.program_id(1) == 0)
        def _():
            dwa_ref[...] = jnp.zeros_like(dwa_ref)
            dwg_ref[...] = jnp.zeros_like(dwg_ref)

        def chunk(ci, carry):
            r0 = pl.multiple_of(ci * CONV_ROWS, CONV_ROWS)
            ns = pl.multiple_of(jnp.minimum(r0 + CONV_ROWS, s - HALO), HALO)
            keep_n = (ci < nch - 1).astype(F32)
            rows = pl.ds(r0, CONV_ROWS)

            def ext_of(ref):
                return jnp.concatenate([ref[0, rows, :].astype(F32), ref[0, pl.ds(ns, HALO), :].astype(F32)], axis=0)

            dhe = jnp.concatenate([dh_ref[0, rows, :].astype(F32), dh_ref[0, pl.ds(ns, HALO), :].astype(F32) * keep_n],
                                  axis=0)
            cae, cge = ext_of(ca_ref), ext_of(cg_ref)
            sg = pl.reciprocal(1.0 + jnp.exp(-cge), approx=True)
            dca = dhe * (cge * sg)
            dcg = dhe * cae * (sg * (1.0 + cge * (1.0 - sg)))

            def back(dc, wv, u_ref, du_ref, dw_ref):
                d0 = dc[:CONV_ROWS]
                n1 = pltpu.roll(dc, ext_rows - 1, 0)[:CONV_ROWS]
                n2 = pltpu.roll(dc, ext_rows - 2, 0)[:CONV_ROWS]
                du_ref[0, rows, :] = ((wv[2:3] * d0 + wv[1:2] * n1) + wv[0:1] * n2).astype(du_ref.dtype)
                uc = u_ref[0, rows, :].astype(F32)
                for k, shifted in enumerate((n2, n1, d0)):
                    dw_ref[k:k + 1, :] += jnp.sum(shifted * uc, axis=0, keepdims=True)

            back(dca, wa, ua_ref, dua_ref, dwa_ref)
            back(dcg, wg, ug_ref, dug_ref, dwg_ref)
            return carry

        lax.fori_loop(0, nch, chunk, 0)

    u3 = u.reshape(bl, s, f2)
    blk = lambda off: pl.BlockSpec((1, s, tc), lambda j, b: (b, 0, j + off))
    wblk = lambda off: pl.BlockSpec((3, tc), lambda j, b: (0, j + off))
    dua, dug, dwa, dwg = pl.pallas_call(
        body, name=name, grid=(nf, bl), in_specs=[blk(0), blk(nf), blk(0), blk(0), wblk(0), wblk(nf), blk(0)],
        out_specs=[blk(0), blk(0), wblk(0), wblk(0)],
        out_shape=[jax.ShapeDtypeStruct((bl, s, f), BF), jax.ShapeDtypeStruct((bl, s, f), BF),
                   jax.ShapeDtypeStruct((3, f), F32), jax.ShapeDtypeStruct((3, f), F32)],
        compiler_params=_params("parallel", "arbitrary"),
    )(u3, u3, ca.reshape(bl, s, f), cg.reshape(bl, s, f), w, w, dh.reshape(bl, s, f))
    return dua.reshape(t, f), dug.reshape(t, f), jnp.concatenate([dwa, dwg], axis=1)


def _lane_masks(rows):
    lane = lax.broadcasted_iota(jnp.int32, (rows, LANES), 1)
    return lane < HEAD_DIM, lane >= HEAD_DIM


def _nt(a, b):
    return lax.dot_general(a, b, (((1,), (1,)), ((), ())), preferred_element_type=F32)


def _tn(a, b):
    return lax.dot_general(a, b, (((0,), (0,)), ((), ())), preferred_element_type=F32)


def _nn(a, b):
    return jnp.dot(a, b, preferred_element_type=F32)


def _suffix_ones():
    j = lax.broadcasted_iota(jnp.int32, (2 * QB, QB), 0) % QB
    s = lax.broadcasted_iota(jnp.int32, (2 * QB, QB), 1)
    return (j >= s).astype(BF)


def _softplus(z):
    return jnp.maximum(z, 0.0) + jnp.log(1.0 + jnp.exp(-jnp.abs(z)))


SB_BLOCK = 256
SB_DEAD = -100.0


SB_STRIP = 32
SB_SUB = SB_BLOCK // QB


def _sb_scratch(backward):
    blk = (2, SB_BLOCK, SB_BLOCK)
    per_head = pltpu.VMEM((2, SB_BLOCK, LANES), F32)
    scr = [pltpu.VMEM((2 * QB, QB), BF),
           pltpu.VMEM(blk, F32),
           pltpu.VMEM((2, SB_SUB) + blk[1:], BF),
           pltpu.VMEM(blk, F32),
           pltpu.VMEM(blk, BF),
           per_head]
    if not backward:
        return scr + [per_head]
    return scr + [pltpu.VMEM(blk, F32),
                  pltpu.VMEM(blk, F32),
                  pltpu.VMEM(blk, F32),
                  pltpu.VMEM(blk, BF),
                  per_head,
                  per_head,
                  per_head]


def _sb_strips(fn, diag):
    for r in range(SB_BLOCK // SB_STRIP):
        fn(pl.ds(r * SB_STRIP, SB_STRIP), r, QB if diag and (r + 1) * SB_STRIP <= QB else SB_BLOCK)


def _sb_keep(diag, r, v):
    if not diag:
        return v
    rel = (lax.broadcasted_iota(jnp.int32, v.shape, 1) - lax.broadcasted_iota(jnp.int32, v.shape, 0))
    return jnp.where(rel < r * SB_STRIP, v, 0.0)


def _sb_split_store(sp_scr, rows, v):
    hi = v.astype(BF)
    lo = (v - hi.astype(F32)).astype(BF)
    for u in range(SB_SUB):
        if (u + 1) * QB <= v.shape[1]:
            sp_scr[u, rows, 0:QB] = hi[:, u * QB:(u + 1) * QB]
            sp_scr[u, rows, QB:2 * QB] = lo[:, u * QB:(u + 1) * QB]
        else:
            sp_scr[u, rows, :] = jnp.zeros((SB_STRIP, 2 * QB), BF)


def _sb_suffix_sums(uu_scr, sp_scr, out_scr):
    for u in range(SB_SUB):
        out_scr[:, u * QB:(u + 1) * QB] = _nn(sp_scr[u], uu_scr[...])


def _sb_fold(sums, off):
    nsub = sums.shape[1] // QB
    offs = [None] * nsub
    for u in reversed(range(nsub)):
        offs[u] = off
        off = off + jnp.broadcast_to(sums[:, u * QB:u * QB + 1], (SB_STRIP, LANES))
    return offs, off


def _sb_store_live(ref, rows, v):
    live = v.shape[1]
    ref[rows, 0:live] = v
    if live < SB_BLOCK:
        ref[rows, live:] = jnp.zeros((SB_STRIP, SB_BLOCK - live), v.dtype)


def _sb_first(diag, z_scr, sp_scr, ls_scr=None):
    def strip(rows, r, live):
        ls = _sb_keep(diag, r, -_softplus(z_scr[rows, 0:live]))
        if ls_scr is not None:
            ls_scr[rows, 0:live] = ls
        _sb_split_store(sp_scr, rows, ls)

    _sb_strips(strip, diag)


def _sb_second(diag, z_scr, cin_scr, w_scr, carry_scr):
    def strip(rows, r, live):
        z, cin = z_scr[rows, 0:live], cin_scr[rows, 0:live]
        offs, carry_scr[rows, :] = _sb_fold(cin, carry_scr[rows, :])
        e = jnp.concatenate([z[:, u * QB:(u + 1) * QB] + cin[:, u * QB:(u + 1) * QB] + offs[u]
                             for u in range(live // QB)], axis=1)
        _sb_store_live(w_scr, rows, _sb_keep(diag, r, jnp.exp(e)).astype(BF))

    _sb_strips(strip, diag)


def _sb_third(diag, w_scr, dw_scr, da_scr, sp_scr):
    def strip(rows, r, live):
        da = w_scr[rows, 0:live].astype(F32) * dw_scr[rows, 0:live]
        da_scr[rows, 0:live] = da
        _sb_split_store(sp_scr, rows, da)

    _sb_strips(strip, diag)


def _sb_fourth(diag, z_scr, ls_scr, da_scr, sin_scr, dz_scr, carry_s_scr, dsum_scr):
    def strip(rows, r, live):
        da, sin = da_scr[rows, 0:live], sin_scr[rows, 0:live]
        offs, carry_s_scr[rows, :] = _sb_fold(sin, carry_s_scr[rows, :])
        dsum = dsum_scr[rows, :]
        pre = jnp.concatenate([dsum - (sin[:, u * QB:(u + 1) * QB] - da[:, u * QB:(u + 1) * QB] + offs[u])
                               for u in range(live // QB)], axis=1)
        sig = jnp.exp(z_scr[rows, 0:live] + ls_scr[rows, 0:live])
        _sb_store_live(dz_scr, rows, _sb_keep(diag, r, da - sig * pre).astype(BF))

    _sb_strips(strip, diag)


def _sb_alive(carry_scr):
    return (jnp.max(carry_scr[...]) > SB_DEAD).astype(jnp.int32)


def _ride_hooks(ride, refs, n_in, n_out, n_scratch, grid):
    if ride is None:
        return refs[:n_in], refs[n_in:n_in + n_out], refs[n_in + n_out:], lambda: None, lambda: None
    ni, no = len(ride.arrays), len(ride.out_shape)
    own_in, rin = refs[:n_in], refs[n_in:n_in + ni]
    own_out, rout = refs[n_in + ni:n_in + ni + n_out], refs[n_in + ni + n_out:n_in + ni + n_out + no]
    rest = refs[n_in + ni + n_out + no:]
    own_scr, sems = rest[:n_scratch], rest[n_scratch:]
    ids = [pl.program_id(a) for a in range(len(grid))]

    def start():
        first = functools.reduce(lambda u, v: u & v, [i == 0 for i in ids])
        pl.when(first)(lambda: ride.start(rin, rout, sems))

    def wait():
        last = functools.reduce(lambda u, v: u & v, [i == n - 1 for i, n in zip(ids, grid)])
        pl.when(last)(lambda: ride.wait(rin, rout, sems))

    return own_in, own_out, own_scr, start, wait


def sb_fwd(proj, bl, *, name, ride=None):
    t, width = proj.shape
    s = t // bl
    npair = SB_WIDTH // LANES
    nq = s // SB_BLOCK
    grid = (bl, npair, nq)

    own_scratch = _sb_scratch(backward=False)

    def body(*refs):
        (q_ref, k_ref, v_ref), (o_ref, of_ref), scr, ride_start, ride_wait = _ride_hooks(
            ride, refs, 3, 2, len(own_scratch), grid)
        uu_scr, z_scr, sp_scr, cin_scr, w_scr, carry_scr, acc_scr = scr
        ride_start()
        i = pl.program_id(2)
        uu_scr[...] = _suffix_ones()
        carry_scr[...] = jnp.zeros_like(carry_scr)
        acc_scr[...] = jnp.zeros_like(acc_scr)
        qs = (q_ref[0].astype(F32) * SCALE).astype(BF)
        masks = _lane_masks(SB_BLOCK)

        z, sp, cin, w, car, acc = ([r.at[h] for h in range(2)] for r in (z_scr, sp_scr, cin_scr, w_scr, carry_scr, acc_scr))

        def live(c, diag):
            start = pl.multiple_of(c * SB_BLOCK, SB_BLOCK)
            kb, vb = k_ref[0, pl.ds(start, SB_BLOCK), :], v_ref[0, pl.ds(start, SB_BLOCK), :]
            for h in range(2):
                z[h][...] = _nt(jnp.where(masks[h], qs, jnp.zeros_like(qs)), kb)
            _sb_first(diag, z[0], sp[0])
            _sb_suffix_sums(uu_scr, sp[0], cin[0])
            _sb_first(diag, z[1], sp[1])
            _sb_suffix_sums(uu_scr, sp[1], cin[1])
            _sb_second(diag, z[0], cin[0], w[0], car[0])
            acc[0][...] += _nn(w[0][...], vb)
            _sb_second(diag, z[1], cin[1], w[1], car[1])
            acc[1][...] += _nn(w[1][...], vb)
            return _sb_alive(carry_scr)

        def step(n, alive):
            return lax.cond(alive > 0, lambda: live(i - n, False), lambda: alive)

        lax.fori_loop(1, i + 1, step, live(i, True))
        both = jnp.where(masks[0], acc_scr[0], acc_scr[1])
        o_ref[0] = both.astype(o_ref.dtype)
        of_ref[0] = both
        ride_wait()

    p3 = proj.reshape(bl, s, width)
    qblk = pl.BlockSpec((1, SB_BLOCK, LANES), lambda b, p, i: (b, i, p))
    extra = ride.arrays if ride else []
    o, of, *delivered = pl.pallas_call(
        body, name=name, grid=grid,
        in_specs=[qblk, pl.BlockSpec((1, s, LANES), lambda b, p, i: (b, 0, npair + p)),
                  pl.BlockSpec((1, s, LANES), lambda b, p, i: (b, 0, 2 * npair + p))] + [ANY] * len(extra),
        out_specs=[qblk, qblk] + [ANY] * (len(ride.out_shape) if ride else 0),
        out_shape=[jax.ShapeDtypeStruct((bl, s, SB_WIDTH), BF), jax.ShapeDtypeStruct((bl, s, SB_WIDTH), F32)]
        + (ride.out_shape if ride else []),
        scratch_shapes=own_scratch + (ride.scratch if ride else []),
        compiler_params=_params("arbitrary", "arbitrary", "arbitrary"),
    )(p3, p3, p3, *extra)
    return o.reshape(t, SB_WIDTH), of.reshape(t, SB_WIDTH), delivered


def sb_bwd(proj, o, dcat, bl, *, name, ride=None):
    t, width = proj.shape
    s = t // bl
    npair = SB_WIDTH // LANES
    nq = s // SB_BLOCK
    grid = (bl, npair, nq)
    own_scratch = _sb_scratch(backward=True)

    def body(*refs):
        (q_ref, k_ref, v_ref, o_ref, do_ref), (dq_ref, dk_ref, dv_ref), scr, ride_start, ride_wait = _ride_hooks(
            ride, refs, 5, 3, len(own_scratch), grid)
        (uu_scr, z_scr, sp_scr, cin_scr, w_scr, carry_scr,
         ls_scr, dw_scr, da_scr, dz_scr, carry_s_scr, dsum_scr, dq_scr) = scr
        ride_start()
        i = pl.program_id(2)

        @pl.when(i == 0)
        def _():
            dk_ref[...] = jnp.zeros_like(dk_ref)
            dv_ref[...] = jnp.zeros_like(dv_ref)

        uu_scr[...] = _suffix_ones()
        for ref in (carry_scr, carry_s_scr, dq_scr):
            ref[...] = jnp.zeros_like(ref)
        masks = _lane_masks(SB_BLOCK)
        qs = (q_ref[0].astype(F32) * SCALE).astype(BF)
        do = do_ref[0]
        prod = do.astype(F32) * o_ref[0]
        for h in range(2):
            dsum_scr[h] = jnp.broadcast_to(jnp.sum(jnp.where(masks[h], prod, 0.0), axis=1, keepdims=True),
                                           (SB_BLOCK, LANES))

        z, sp, cin, w, car, ls, dw, da, dz, cars, dsum, dq = (
            [r.at[h] for h in range(2)] for r in (z_scr, sp_scr, cin_scr, w_scr, carry_scr, ls_scr, dw_scr, da_scr,
                                                  dz_scr, carry_s_scr, dsum_scr, dq_scr))

        def live(c, diag):
            start = pl.multiple_of(c * SB_BLOCK, SB_BLOCK)
            keys = pl.ds(start, SB_BLOCK)
            kb, vb = k_ref[0, keys, :], v_ref[0, keys, :]
            qh = [jnp.where(masks[h], qs, jnp.zeros_like(qs)) for h in range(2)]
            doh = [jnp.where(masks[h], do, jnp.zeros_like(do)) for h in range(2)]
            for h in range(2):
                z[h][...] = _nt(qh[h], kb)
                dw[h][...] = _nt(doh[h], vb)

            def weights(h):
                _sb_second(diag, z[h], cin[h], w[h], car[h])
                _sb_third(diag, w[h], dw[h], da[h], sp[h])
                _sb_suffix_sums(uu_scr, sp[h], cin[h])

            def grads(h):
                _sb_fourth(diag, z[h], ls[h], da[h], cin[h], dz[h], cars[h], dsum[h])
                dk_ref[0, keys, :] += _tn(dz[h][...], qh[h])
                dv_ref[0, keys, :] += _tn(w[h][...], doh[h])
                dq[h][...] += _nn(dz[h][...], kb)

            _sb_first(diag, z[0], sp[0], ls[0])
            _sb_suffix_sums(uu_scr, sp[0], cin[0])
            _sb_first(diag, z[1], sp[1], ls[1])
            _sb_suffix_sums(uu_scr, sp[1], cin[1])
            weights(0)
            weights(1)
            grads(0)
            grads(1)
            return _sb_alive(carry_scr)

        def step(n, alive):
            return lax.cond(alive > 0, lambda: live(i - n, False), lambda: alive)

        lax.fori_loop(1, i + 1, step, live(i, True))
        dq_ref[0] = (jnp.where(masks[0], dq_scr[0], dq_scr[1]) * SCALE).astype(dq_ref.dtype)
        ride_wait()

    p3 = proj.reshape(bl, s, width)
    o3 = o.reshape(bl, s, SB_WIDTH)
    d3 = dcat.reshape(bl, s, dcat.shape[1])
    qblk = pl.BlockSpec((1, SB_BLOCK, LANES), lambda b, p, i: (b, i, p))
    full = pl.BlockSpec((1, s, LANES), lambda b, p, i: (b, 0, p))
    extra = ride.arrays if ride else []
    dq, dk, dv, *delivered = pl.pallas_call(
        body, name=name, grid=grid,
        in_specs=[qblk, pl.BlockSpec((1, s, LANES), lambda b, p, i: (b, 0, npair + p)),
                  pl.BlockSpec((1, s, LANES), lambda b, p, i: (b, 0, 2 * npair + p)), qblk, qblk] + [ANY] * len(extra),
        out_specs=[qblk, full, full] + [ANY] * (len(ride.out_shape) if ride else 0),
        out_shape=[jax.ShapeDtypeStruct((bl, s, SB_WIDTH), BF), jax.ShapeDtypeStruct((bl, s, SB_WIDTH), F32),
                   jax.ShapeDtypeStruct((bl, s, SB_WIDTH), F32)] + (ride.out_shape if ride else []),
        scratch_shapes=own_scratch + (ride.scratch if ride else []),
        compiler_params=_params("arbitrary", "arbitrary", "arbitrary"),
    )(p3, p3, p3, o3, d3, *extra)
    return dq.reshape(t, SB_WIDTH), dk.reshape(t, SB_WIDTH), dv.reshape(t, SB_WIDTH), delivered


MEM_TQ = 512


def mem_fwd(proj, kvm, bl, *, name):
    t, width = proj.shape
    s = t // bl
    qoff = (width - MEM_WIDTH) // LANES
    npair = MEM_WIDTH // LANES

    def body(q_ref, k_ref, v_ref, o_ref):
        ma, mb = _lane_masks(MEM_TQ)
        q = q_ref[...].astype(BF)
        outs = []
        for m in (ma, mb):
            qh = jnp.where(m, q, jnp.zeros_like(q))
            sc = _nt(qh, k_ref[...]) * SCALE
            p = jnp.exp(sc - jnp.max(sc, axis=-1, keepdims=True))
            p = p * (1.0 / jnp.sum(p, axis=-1, keepdims=True))
            outs.append(_nn(p.astype(BF), v_ref[...]))
        o_ref[...] = jnp.where(ma, outs[0], outs[1]).astype(o_ref.dtype)

    nt = s // MEM_TQ
    return pl.pallas_call(
        body, name=name, grid=(bl, npair, nt),
        in_specs=[pl.BlockSpec((MEM_TQ, LANES), lambda b, p, i: (b * nt + i, qoff + p)),
                  pl.BlockSpec((MEM_LEN, LANES), lambda b, p, i: (b, p)),
                  pl.BlockSpec((MEM_LEN, LANES), lambda b, p, i: (b, npair + p))],
        out_specs=pl.BlockSpec((MEM_TQ, LANES), lambda b, p, i: (b * nt + i, p)),
        out_shape=jax.ShapeDtypeStruct((t, MEM_WIDTH), BF),
        compiler_params=_params("parallel", "parallel", "parallel"),
    )(proj, kvm, kvm)


def mem_bwd(proj, kvm, dcat, bl, *, name):
    t, width = proj.shape
    s = t // bl
    qoff = (width - MEM_WIDTH) // LANES
    doff = (dcat.shape[1] - MEM_WIDTH) // LANES
    npair = MEM_WIDTH // LANES
    nt = s // MEM_TQ

    def body(q_ref, k_ref, v_ref, do_ref, dq_ref, dk_ref, dv_ref):
        @pl.when(pl.program_id(2) == 0)
        def _():
            dk_ref[...] = jnp.zeros_like(dk_ref)
            dv_ref[...] = jnp.zeros_like(dv_ref)

        ma, mb = _lane_masks(MEM_TQ)
        q = q_ref[...].astype(BF)
        do = do_ref[...]
        kb, vb = k_ref[...], v_ref[...]
        dqs = []
        for m in (ma, mb):
            qh = jnp.where(m, q, jnp.zeros_like(q))
            doh = jnp.where(m, do, jnp.zeros_like(do))
            sc = _nt(qh, kb) * SCALE
            p = jnp.exp(sc - jnp.max(sc, axis=-1, keepdims=True))
            p = p * (1.0 / jnp.sum(p, axis=-1, keepdims=True))
            dp = _nt(doh, vb)
            ds = p * (dp - jnp.sum(p * dp, axis=-1, keepdims=True))
            dss = (ds * SCALE).astype(BF)
            dk_ref[...] += _tn(dss, qh)
            dv_ref[...] += _tn(p.astype(BF), doh)
            dqs.append(_nn(dss, kb))
        dq_ref[...] = jnp.where(ma, dqs[0], dqs[1]).astype(dq_ref.dtype)

    kblk = pl.BlockSpec((MEM_LEN, LANES), lambda b, p, i: (b, p))
    dq, dk, dv = pl.pallas_call(
        body, name=name, grid=(bl, npair, nt),
        in_specs=[pl.BlockSpec((MEM_TQ, LANES), lambda b, p, i: (b * nt + i, qoff + p)), kblk,
                  pl.BlockSpec((MEM_LEN, LANES), lambda b, p, i: (b, npair + p)),
                  pl.BlockSpec((MEM_TQ, LANES), lambda b, p, i: (b * nt + i, doff + p))],
        out_specs=[pl.BlockSpec((MEM_TQ, LANES), lambda b, p, i: (b * nt + i, p)), kblk, kblk],
        out_shape=[jax.ShapeDtypeStruct((t, MEM_WIDTH), BF), jax.ShapeDtypeStruct((bl * MEM_LEN, MEM_WIDTH), F32),
                   jax.ShapeDtypeStruct((bl * MEM_LEN, MEM_WIDTH), F32)],
        compiler_params=_params("parallel", "parallel", "arbitrary"),
    )(proj, kvm, kvm, dcat)
    return dq, jnp.concatenate([dk, dv], axis=1).astype(BF)


def _dil_rows(r, u0, size, dil):
    if dil == 1:
        return pl.ds(_mo(u0, QB), size)
    return pl.ds(u0 * dil + r, size, stride=dil)


def _dil_walk(block, dil, nb):
    def residue(r, c):
        block(r, 0, QB)
        if nb > 1:
            def step(n, c2):
                block(r, n, 2 * QB)
                return c2
            lax.fori_loop(1, nb, step, 0, unroll=3)
        return c

    if dil == 1:
        residue(0, 0)
    else:
        for r in range(dil):
            for n in range(nb):
                block(r, n, QB if n == 0 else 2 * QB)


def _dil_scores(qh, kb, slope_d, n_keys):
    i = lax.broadcasted_iota(jnp.int32, (QB, n_keys), 0)
    j = lax.broadcasted_iota(jnp.int32, (QB, n_keys), 1)
    delta = i + (n_keys - QB) - j
    valid = (delta >= 0) & (delta <= QB)
    sc = _nt(qh, kb) * SCALE - slope_d * delta.astype(F32)
    return jnp.where(valid, sc, NEG)


def _dil_slopes(g, dil):
    p = pl.program_id(1)
    sa = jnp.where(p == 0, ALIBI[4 * g] * dil, ALIBI[4 * g + 2] * dil).astype(F32)
    sb = jnp.where(p == 0, ALIBI[4 * g + 1] * dil, ALIBI[4 * g + 3] * dil).astype(F32)
    return sa, sb


def dil_fwd(projb, kv, g, bl, *, name):
    _, dil = DIL_GROUPS[g]
    t, wq = projb.shape
    wk = kv.shape[1]
    s = t // bl
    ln = s // dil
    nb = ln // QB
    gw = 4 * HEAD_DIM
    ck, co = wk // LANES, gw // LANES

    def body(q_ref, k_ref, v_ref, o_ref, lse_ref):
        sa, sb = _dil_slopes(g, dil)
        ma, mb = _lane_masks(QB)

        def block(r, n, n_keys):
            q0 = n * QB
            k0 = q0 - (n_keys - QB)
            q = q_ref[0, _dil_rows(r, q0, QB, dil), :].astype(BF)
            kb = k_ref[0, _dil_rows(r, k0, n_keys, dil), :].astype(BF)
            vb = v_ref[0, _dil_rows(r, k0, n_keys, dil), :].astype(BF)
            outs, lses = [], []
            for m, sl in ((ma, sa), (mb, sb)):
                qh = jnp.where(m, q, jnp.zeros_like(q))
                sc = _dil_scores(qh, kb, sl, n_keys)
                mx = jnp.max(sc, axis=-1, keepdims=True)
                p = jnp.exp(sc - mx)
                den = jnp.sum(p, axis=-1, keepdims=True)
                outs.append(_nn(p.astype(BF), vb) * (1.0 / den))
                lses.append(mx + jnp.log(den))
            o_ref[0, _dil_rows(r, q0, QB, dil), :] = jnp.where(ma, outs[0], outs[1])
            lse_ref[0, _dil_rows(r, q0, QB, dil), :] = jnp.where(ma, lses[0], lses[1])

        _dil_walk(block, dil, nb)

    colblk = lambda off: pl.BlockSpec((1, s, LANES), lambda b, p: (b, 0, off + p))
    o, lse = pl.pallas_call(
        body, name=name, grid=(bl, co),
        in_specs=[colblk(g * co), colblk(g * co), colblk(ck // 2 + g * co)],
        out_specs=[colblk(0), colblk(0)],
        out_shape=[jax.ShapeDtypeStruct((bl, s, gw), F32), jax.ShapeDtypeStruct((bl, s, gw), F32)],
        compiler_params=_params("parallel", "parallel"),
    )(projb.reshape(bl, s, wq), kv.reshape(bl, s, wk), kv.reshape(bl, s, wk))
    return o.reshape(t, gw), lse.reshape(t, gw)


def dil_bwd(projb, kv, lse, dog, dshift, g, bl, *, name):
    _, dil = DIL_GROUPS[g]
    t, wq = projb.shape
    wk = kv.shape[1]
    s = t // bl
    ln = s // dil
    nb = ln // QB
    gw = 4 * HEAD_DIM
    ck, co = wk // LANES, gw // LANES

    def body(q_ref, k_ref, v_ref, lse_ref, do_ref, sh_ref, dq_ref, dk_ref, dv_ref):
        sa, sb = _dil_slopes(g, dil)
        ma, mb = _lane_masks(QB)
        dk_ref[...] = jnp.zeros_like(dk_ref)
        dv_ref[...] = jnp.zeros_like(dv_ref)

        def block(r, n, n_keys):
            q0 = n * QB
            k0 = q0 - (n_keys - QB)
            qrows, krows = _dil_rows(r, q0, QB, dil), _dil_rows(r, k0, n_keys, dil)
            q = q_ref[0, qrows, :].astype(BF)
            do = do_ref[0, qrows, :].astype(BF)
            lse_b = lse_ref[0, qrows, :]
            sh_b = sh_ref[0, qrows, :]
            kb = k_ref[0, krows, :].astype(BF)
            vb = v_ref[0, krows, :].astype(BF)
            dqs = []
            for m, sl, c0 in ((ma, sa, 0), (mb, sb, HEAD_DIM)):
                qh = jnp.where(m, q, jnp.zeros_like(q))
                doh = jnp.where(m, do, jnp.zeros_like(do))
                sc = _dil_scores(qh, kb, sl, n_keys)
                p = jnp.exp(sc - lse_b[:, c0:c0 + 1])
                ds = p * (_nt(doh, vb) - sh_b[:, c0:c0 + 1])
                dss = (ds * SCALE).astype(BF)
                dk_ref[0, krows, :] += _tn(dss, qh)
                dv_ref[0, krows, :] += _tn(p.astype(BF), doh)
                dqs.append(_nn(dss, kb))
            dq_ref[0, qrows, :] = jnp.where(ma, dqs[0], dqs[1])

        _dil_walk(block, dil, nb)

    colblk = lambda off: pl.BlockSpec((1, s, LANES), lambda b, p: (b, 0, off + p))
    dq, dk, dv = pl.pallas_call(
        body, name=name, grid=(bl, co),
        in_specs=[colblk(g * co), colblk(g * co), colblk(ck // 2 + g * co), colblk(0), colblk(g * co), colblk(g * co)],
        out_specs=[colblk(0)] * 3,
        out_shape=[jax.ShapeDtypeStruct((bl, s, gw), F32)] * 3,
        compiler_params=_params("parallel", "parallel"),
    )(projb.reshape(bl, s, wq), kv.reshape(bl, s, wk), kv.reshape(bl, s, wk), lse.reshape(bl, s, gw),
      dog.reshape(bl, s, DIL_WIDTH), dshift.reshape(bl, s, DIL_WIDTH))
    return dq.reshape(t, gw), dk.reshape(t, gw), dv.reshape(t, gw)


def _group_weights(lses):
    mx = jnp.maximum(jnp.maximum(lses[0], lses[1]), lses[2])
    es = [jnp.exp(l - mx) for l in lses]
    inv = 1.0 / (es[0] + es[1] + es[2])
    return [e * inv for e in es]


def dil_combine_fwd(os_, lses, *, name):
    t, gw = os_[0].shape
    tr = _pick(t, (512, 256))

    def body(o0, o1, o2, l0, l1, l2, out_ref):
        al = _group_weights([l0[...], l1[...], l2[...]])
        for g, o_ref in enumerate((o0, o1, o2)):
            out_ref[:, g * gw:(g + 1) * gw] = (o_ref[...] * al[g]).astype(out_ref.dtype)

    blk = pl.BlockSpec((tr, gw), lambda i: (i, 0))
    return pl.pallas_call(
        body, name=name, grid=(t // tr,), in_specs=[blk] * 6,
        out_specs=pl.BlockSpec((tr, 3 * gw), lambda i: (i, 0)), out_shape=jax.ShapeDtypeStruct((t, 3 * gw), BF),
        compiler_params=_params("parallel"),
    )(*os_, *lses)


def dil_combine_bwd(os_, lses, dcat, *, name):
    t, gw = os_[0].shape
    tr = _pick(t, (512, 256))

    def head_sum(v):
        parts = []
        for c in range(gw // LANES):
            blk = v[:, c * LANES:(c + 1) * LANES]
            ma, _ = _lane_masks(tr)
            sa = jnp.sum(jnp.where(ma, blk, 0.0), axis=1, keepdims=True)
            sb = jnp.sum(blk, axis=1, keepdims=True) - sa
            parts.append(jnp.where(ma, sa, sb))
        return jnp.concatenate(parts, axis=1)

    def body(o0, o1, o2, l0, l1, l2, d_ref, dog_ref, sh_ref):
        al = _group_weights([l0[...], l1[...], l2[...]])
        dos = [d_ref[:, g * gw:(g + 1) * gw].astype(F32) for g in range(3)]
        dal = [head_sum(dos[g] * o_ref[...]) for g, o_ref in enumerate((o0, o1, o2))]
        mix = al[0] * dal[0] + al[1] * dal[1] + al[2] * dal[2]
        for g in range(3):
            dog_ref[:, g * gw:(g + 1) * gw] = (al[g] * dos[g]).astype(dog_ref.dtype)
            sh_ref[:, g * gw:(g + 1) * gw] = al[g] * mix

    blk = pl.BlockSpec((tr, gw), lambda i: (i, 0))
    wide = pl.BlockSpec((tr, 3 * gw), lambda i: (i, 0))
    return pl.pallas_call(
        body, name=name, grid=(t // tr,), in_specs=[blk] * 6 + [wide], out_specs=[wide, wide],
        out_shape=[jax.ShapeDtypeStruct((t, 3 * gw), F32), jax.ShapeDtypeStruct((t, 3 * gw), F32)],
        compiler_params=_params("parallel"),
    )(*os_, *lses, dcat)


def adamw(w, g1, g2, m, v, *, name):
    r, c = w.shape
    tr = r
    for cand in (256, 128, 64, 32, 16, 8):
        if r % cand == 0 and cand * c * 4 <= (1 << 20):
            tr = cand
            break
    two = g2 is not None

    def body(*refs):
        if two:
            w_ref, g1_ref, g2_ref, m_ref, v_ref, g_ref, d_ref, nm_ref, nv_ref = refs
            g = g1_ref[...] + g2_ref[...]
        else:
            w_ref, g1_ref, m_ref, v_ref, g_ref, d_ref, nm_ref, nv_ref = refs
            g = g1_ref[...]
        nm = ADAM_B1 * m_ref[...] + (1.0 - ADAM_B1) * g
        nv = ADAM_B2 * v_ref[...] + (1.0 - ADAM_B2) * (g * g)
        m_hat = nm / (1.0 - ADAM_B1 ** ADAM_STEP)
        v_hat = nv / (1.0 - ADAM_B2 ** ADAM_STEP)
        g_ref[...] = g
        d_ref[...] = -ADAM_LR * (m_hat / (jnp.sqrt(v_hat) + ADAM_EPS) + ADAM_WD * w_ref[...])
        nm_ref[...] = nm
        nv_ref[...] = nv

    blk = pl.BlockSpec((tr, c), lambda i: (i, 0))
    args = [w, g1] + ([g2] if two else []) + [m, v]
    return pl.pallas_call(
        body, name=name, grid=(r // tr,), in_specs=[blk] * len(args), out_specs=[blk] * 4,
        out_shape=[jax.ShapeDtypeStruct((r, c), F32)] * 4, compiler_params=_params("parallel"),
    )(*args)


def sum4(own, land, *, name):
    r, c = own.shape
    tr = _pick(r, (256, 128, 64))

    def body(o_ref, l_ref, s_ref):
        s_ref[...] = ((o_ref[...].astype(F32) + l_ref[0].astype(F32)) + l_ref[1].astype(F32)) + l_ref[2].astype(F32)

    return pl.pallas_call(
        body, name=name, grid=(r // tr,),
        in_specs=[pl.BlockSpec((tr, c), lambda i: (i, 0)), pl.BlockSpec((3, tr, c), lambda i: (0, i, 0))],
        out_specs=pl.BlockSpec((tr, c), lambda i: (i, 0)), out_shape=jax.ShapeDtypeStruct((r, c), F32),
        compiler_params=_params("parallel"),
    )(own, land)


ANY = pl.BlockSpec(memory_space=pl.ANY)


def _place():
    x, y, c = lax.axis_index("x"), lax.axis_index("y"), lax.axis_index("c")
    chips = [(1 - x, y), (x, 1 - y), (1 - x, 1 - y)]
    return x, y, c, chips


class Ride:
    def __init__(self, arrays, out_shape, copies):
        self.arrays, self.out_shape, self.copies = list(arrays), list(out_shape), copies
        n = 3 * len(self.arrays)
        self.scratch = [pltpu.SemaphoreType.DMA((n,)), pltpu.SemaphoreType.DMA((n,)),
                        pltpu.SemaphoreType.DMA((len(self.arrays),))]

    def split(self, refs):
        n, m = len(self.arrays), len(self.out_shape)
        return refs[:n], refs[n:n + m], refs[n + m:]

    def start(self, ins, outs, sems):
        local, sends, _ = self.copies(ins, outs, *sems)
        for cp in local + sends:
            cp.start()

    def wait(self, ins, outs, sems):
        local, sends, arrivals = self.copies(ins, outs, *sems)
        for cp in arrivals:
            cp.wait_recv()
        for cp in sends:
            cp.wait_send()
        for cp in local:
            cp.wait()

    def run(self, name):
        def body(*refs):
            ins, outs, sems = self.split(refs)
            self.start(ins, outs, sems)
            self.wait(ins, outs, sems)

        return pl.pallas_call(body, name=name, in_specs=[ANY] * len(self.arrays), out_specs=[ANY] * len(self.out_shape),
                              out_shape=self.out_shape, scratch_shapes=self.scratch)(*self.arrays)


def gather_ride(shards, axes):
    def copies(ins, outs, send_sems, recv_sems, local_sems):
        x, y, c, chips = _place()

        def slot(a, q):
            size = shards[a].shape[axes[a]]
            start = pl.multiple_of(q * size, size)
            return outs[a].at[pl.ds(start, size), :] if axes[a] == 0 else outs[a].at[:, pl.ds(start, size)]

        def remote(a, k, q):
            px, py = chips[k]
            return pltpu.make_async_remote_copy(src_ref=ins[a], dst_ref=slot(a, q), send_sem=send_sems.at[3 * a + k],
                                                recv_sem=recv_sems.at[3 * a + k], device_id=(px, py, c), device_id_type=MESH)

        me = 2 * x + y
        n = len(shards)
        local = [pltpu.make_async_copy(ins[a], slot(a, me), local_sems.at[a]) for a in range(n)]
        sends = [remote(a, k, me) for a in range(n) for k in range(3)]
        arrivals = [remote(a, k, 2 * chips[k][0] + chips[k][1]) for a in range(n) for k in range(3)]
        return local, sends, arrivals

    out_shape = []
    for a, sh in enumerate(shards):
        full = list(sh.shape)
        full[axes[a]] *= N_CHIPS
        out_shape.append(jax.ShapeDtypeStruct(tuple(full), sh.dtype))
    return Ride(shards, out_shape, copies)


def scatter_ride(grads):
    def copies(ins, outs, send_sems, recv_sems, local_sems):
        x, y, c, chips = _place()
        sends = [pltpu.make_async_remote_copy(src_ref=ins[a].at[2 * px + py], dst_ref=outs[a].at[k],
                                              send_sem=send_sems.at[3 * a + k], recv_sem=recv_sems.at[3 * a + k],
                                              device_id=(px, py, c), device_id_type=MESH)
                 for a in range(len(grads)) for k, (px, py) in enumerate(chips)]
        return [], sends, sends

    return Ride(grads, [jax.ShapeDtypeStruct((3,) + g.shape[1:], g.dtype) for g in grads], copies)


def swap_ride(arrs):
    def copies(ins, outs, send_sems, recv_sems, local_sems):
        x, y, c, _ = _place()
        sends = [pltpu.make_async_remote_copy(src_ref=ins[a], dst_ref=outs[a], send_sem=send_sems.at[a],
                                              recv_sem=recv_sems.at[a], device_id=(x, y, 1 - c), device_id_type=MESH)
                 for a in range(len(arrs))]
        return [], sends, sends

    return Ride(arrs, [jax.ShapeDtypeStruct(a.shape, a.dtype) for a in arrs], copies)


def all_reduce_small(v, *, name):
    rows = v.shape[0]

    def body(v_ref, o_ref, gath, send_sems, recv_sems):
        x, y, c, _ = _place()
        me = 4 * x + 2 * y + c
        gath[me] = v_ref[...]
        sends = []
        for msk in range(1, N_DEV):
            peer = (x ^ (msk >> 2), y ^ ((msk >> 1) & 1), c ^ (msk & 1))
            cp = pltpu.make_async_remote_copy(src_ref=v_ref, dst_ref=gath.at[me], send_sem=send_sems.at[msk - 1],
                                              recv_sem=recv_sems.at[msk - 1], device_id=peer, device_id_type=MESH)
            cp.start()
            sends.append(cp)
        for msk in range(1, N_DEV):
            pltpu.make_async_remote_copy(src_ref=v_ref, dst_ref=gath.at[me ^ msk], send_sem=send_sems.at[msk - 1],
                                         recv_sem=recv_sems.at[msk - 1], device_id=(x, y, c), device_id_type=MESH).wait_recv()
        for cp in sends:
            cp.wait_send()
        tot = gath[0]
        for q in range(1, N_DEV):
            tot = tot + gath[q]
        o_ref[...] = tot

    vm = pl.BlockSpec(memory_space=pltpu.VMEM)
    return pl.pallas_call(
        body, name=name, in_specs=[vm], out_specs=vm, out_shape=jax.ShapeDtypeStruct(v.shape, F32),
        scratch_shapes=[pltpu.VMEM((N_DEV, rows, LANES), F32), pltpu.SemaphoreType.DMA((N_DEV - 1,)),
                        pltpu.SemaphoreType.DMA((N_DEV - 1,))],
    )(v)


def _ffn_fwd(xin, gain, w_up, w_conv, w_down, bl, tag, rides=(None, None, None)):
    got = {}

    def carried(k, result, delivered):
        if rides[k] is not None:
            got.update(zip(rides[k][1], delivered))
        return result

    n = rmsnorm_fwd(xin, gain, name=f"{tag}_ffn_norm")
    if rides[0] is None:
        u = matmul(n, w_up, name=f"{tag}_ffn_up")
    else:
        u = carried(0, *matmul(n, w_up, ride=rides[0][0], name=f"{tag}_ffn_up"))
    *hc, delivered = conv_silu_fwd(u, w_conv, bl, name=f"{tag}_ffn_conv", ride=rides[1][0] if rides[1] else None)
    h, ca, cg = carried(1, hc, delivered)
    if rides[2] is None:
        xout = matmul(h, w_down, out_dtype=F32, add=xin, name=f"{tag}_ffn_down")
    else:
        xout = carried(2, *matmul(h, w_down, out_dtype=F32, add=xin, ride=rides[2][0], name=f"{tag}_ffn_down"))
    return xout, (n, u, h, ca, cg), got


def _ffn_bwd(dxout, dxout_b, xin, saved, gain, w_up, w_conv, w_down, bl, tag, shard):
    n, u, h, ca, cg = saved
    dh = matmul(dxout_b, w_down, tb=True, name=f"{tag}_ffn_down_dx")
    g_down = matmul(h, dxout_b, ta=True, name=f"{tag}_ffn_down_dw")
    dua, dug, g_conv = conv_silu_bwd(u, ca, cg, w_conv, dh, bl, name=f"{tag}_ffn_conv_bwd")
    dn = matmul(dua, w_up, a2=dug, tb=True, out_dtype=F32, name=f"{tag}_ffn_up_dx")
    g_up = matmul(n, dua, b2=dug, ta=True, shard_cols=shard, name=f"{tag}_ffn_up_dw")
    dxin, dxin_b, g_norm = rmsnorm_bwd(xin, gain, dn, dxout, name=f"{tag}_ffn_norm_bwd")
    return dxin, dxin_b, g_norm, g_up, g_conv, g_down


def _mem_kv(mem2, gain, w_kv, tag):
    mn = rmsnorm_fwd(mem2, gain, name=f"{tag}_mem_norm")
    return mn, matmul(mn, w_kv, name=f"{tag}_mem_kv")


def _mem_kv_bwd(mem2, mn, gain, w_kv, dkvm, tag):
    g_kv = matmul(mn, dkvm, ta=True, name=f"{tag}_mem_kv_dw")
    dmn = matmul(dkvm, w_kv, tb=True, out_dtype=F32, name=f"{tag}_mem_kv_dx")
    _, _, g_norm = rmsnorm_bwd(mem2, gain, dmn, None, name=f"{tag}_mem_norm_bwd")
    return g_norm, g_kv


def _shard_major(name, grad):
    if name in COL_SHARDED:
        return grad
    return grad.reshape(N_CHIPS, grad.shape[0] // N_CHIPS, grad.shape[1])


def local_step(x, mem, target, w, shard_of, late_gather=None, early_scatter=(), scatter_last=(), reduce_early=None):
    bl, s, d = x.shape
    t = bl * s
    x0 = x.reshape(t, d)
    mem2 = mem.reshape(bl * MEM_LEN, d)
    tgt = target.reshape(t, d)
    g = {}

    n1 = rmsnorm_fwd(x0, w["a_norm_attn"], name="a_attn_norm")
    proj_a = matmul(n1, w["a_w_in"], name="a_w_in")
    mn_a, kvm_a = _mem_kv(mem2, w["a_norm_mem"], w["a_w_mem_kv"], "a")
    late_gather = late_gather or {}
    behind_sb = late_gather.get("sb")
    o_sb, o_sb_f32, delivered = sb_fwd(proj_a, bl, name="a_sb_fwd", ride=behind_sb[0] if behind_sb else None)
    if behind_sb:
        w = {**w, **dict(zip(behind_sb[1], delivered))}
    o_mem_a = mem_fwd(proj_a, kvm_a, bl, name="a_mem_fwd")
    cat_a = jnp.concatenate([o_sb, o_mem_a], axis=1)
    x1 = matmul(cat_a, w["a_w_out"], out_dtype=F32, add=x0, name="a_w_out")
    x2, ffn_a, got = _ffn_fwd(x1, w["a_norm_ffn"], w["a_ffn_up"], w["a_ffn_conv"], w["a_ffn_down"], bl, "a",
                              rides=tuple(late_gather.get(host) for host in ("up", "conv", "down")))
    w = {**w, **got}

    nkv, n3 = rmsnorm_fwd2(x2, w["kv_norm"], w["b_norm_attn"], name="kv_b_attn_norm")
    kv = matmul(nkv, w["w_kv_shared"], out_dtype=F32, name="w_kv")
    proj_b = matmul(n3, w["b_w_in"], out_dtype=F32, name="b_w_in")
    mn_b, kvm_b = _mem_kv(mem2, w["b_norm_mem"], w["b_w_mem_kv"], "b")
    dil = [dil_fwd(proj_b, kv, gi, bl, name=f"b_dil_fwd{gi}") for gi in range(3)]
    os_, lses = [o for o, _ in dil], [l for _, l in dil]
    o_dil = dil_combine_fwd(os_, lses, name="b_dil_combine")
    o_mem_b = mem_fwd(proj_b, kvm_b, bl, name="b_mem_fwd")
    cat_b = jnp.concatenate([o_dil, o_mem_b], axis=1)
    x3 = matmul(cat_b, w["b_w_out"], out_dtype=F32, add=x2, name="b_w_out")
    x4, ffn_b, _ = _ffn_fwd(x3, w["b_norm_ffn"], w["b_ffn_up"], w["b_ffn_conv"], w["b_ffn_down"], bl, "b")

    dx4, dx4b, g["final_norm"], lossvec = final_loss(x4, w["final_norm"], tgt, name="final_loss")

    dx3, dx3b, g["b_norm_ffn"], g["b_ffn_up"], g["b_ffn_conv"], g["b_ffn_down"] = _ffn_bwd(
        dx4, dx4b, x3, ffn_b, w["b_norm_ffn"], w["b_ffn_up"], w["b_ffn_conv"], w["b_ffn_down"], bl, "b",
        shard_of["b_ffn_up"])
    dcat_b = matmul(dx3b, w["b_w_out"], tb=True, name="b_w_out_dx")
    g["b_w_out"] = matmul(cat_b, dx3b, ta=True, name="b_w_out_dw")
    dog, dshift = dil_combine_bwd(os_, lses, dcat_b, name="b_dil_combine_bwd")
    dqs, dks, dvs = [], [], []
    for gi in range(3):
        dq_g, dk_g, dv_g = dil_bwd(proj_b, kv, lses[gi], dog, dshift, gi, bl, name=f"b_dil_bwd{gi}")
        dqs.append(dq_g), dks.append(dk_g), dvs.append(dv_g)
    dq_mem_b, dkvm_b = mem_bwd(proj_b, kvm_b, dcat_b, bl, name="b_mem_bwd")
    g["b_norm_mem"], g["b_w_mem_kv"] = _mem_kv_bwd(mem2, mn_b, w["b_norm_mem"], w["b_w_mem_kv"], dkvm_b, "b")
    dproj_b = jnp.concatenate([dq_g.astype(BF) for dq_g in dqs] + [dq_mem_b], axis=1)
    dn3 = matmul(dproj_b, w["b_w_in"], tb=True, out_dtype=F32, name="b_w_in_dx")
    g["b_w_in"] = matmul(n3, dproj_b, ta=True, name="b_w_in_dw")
    dkv = jnp.concatenate(dks + dvs, axis=1).astype(BF)
    dnkv = matmul(dkv, w["w_kv_shared"], tb=True, out_dtype=F32, name="w_kv_dx")
    g["w_kv_shared"] = matmul(nkv, dkv, ta=True, shard_cols=shard_of["w_kv_shared"], name="w_kv_dw")
    dx2, dx2b, g["kv_norm"], g["b_norm_attn"] = rmsnorm_bwd2(x2, w["kv_norm"], w["b_norm_attn"], dnkv, dn3, dx3,
                                                             name="kv_b_attn_norm_bwd")

    dx1, dx1b, g["a_norm_ffn"], g["a_ffn_up"], g["a_ffn_conv"], g["a_ffn_down"] = _ffn_bwd(
        dx2, dx2b, x1, ffn_a, w["a_norm_ffn"], w["a_ffn_up"], w["a_ffn_conv"], w["a_ffn_down"], bl, "a",
        shard_of["a_ffn_up"])
    dcat_a = matmul(dx1b, w["a_w_out"], tb=True, name="a_w_out_dx")
    g["a_w_out"] = matmul(cat_a, dx1b, ta=True, name="a_w_out_dw")
    leaving = scatter_ride([_shard_major(n, g[n]) for n in early_scatter]) if early_scatter else None
    dq_sb, dk_sb, dv_sb, landed = sb_bwd(proj_a, o_sb_f32, dcat_a, bl, name="a_sb_bwd", ride=leaving)
    landed = dict(zip(early_scatter, landed))
    dq_mem_a, dkvm_a = mem_bwd(proj_a, kvm_a, dcat_a, bl, name="a_mem_bwd")
    g["a_norm_mem"], g["a_w_mem_kv"] = _mem_kv_bwd(mem2, mn_a, w["a_norm_mem"], w["a_w_mem_kv"], dkvm_a, "a")
    dproj_a = jnp.concatenate([dq_sb, dk_sb.astype(BF), dv_sb.astype(BF), dq_mem_a], axis=1)
    if reduce_early is None:
        dn1 = matmul(dproj_a, w["a_w_in"], tb=True, out_dtype=F32, name="a_w_in_dx")
        g["a_w_in"] = matmul(n1, dproj_a, ta=True, shard_cols=shard_of["a_w_in"], name="a_w_in_dw")
        dx0, _, g["a_norm_attn"] = rmsnorm_bwd(x0, w["a_norm_attn"], dn1, dx1, name="a_attn_norm_bwd")
        return lossvec, dx0, g, landed
    sums = reduce_early(g, landed)
    dn1, theirs = matmul(dproj_a, w["a_w_in"], tb=True, out_dtype=F32, name="a_w_in_dx",
                         ride=swap_ride([sums[n] for n in early_scatter]))
    g["a_w_in"] = matmul(n1, dproj_a, ta=True, shard_cols=shard_of["a_w_in"], name="a_w_in_dw")
    dx0, _, g["a_norm_attn"], landed_last = rmsnorm_bwd(
        x0, w["a_norm_attn"], dn1, dx1, name="a_attn_norm_bwd",
        ride=scatter_ride([_shard_major(n, g[n]) for n in scatter_last]))
    return lossvec, dx0, g, (sums, dict(zip(early_scatter, theirs)), dict(zip(scatter_last, landed_last)))


MATRICES = ("a_w_in", "a_w_out", "a_w_mem_kv", "a_ffn_up", "a_ffn_down", "w_kv_shared", "b_w_in", "b_w_out",
            "b_w_mem_kv", "b_ffn_up", "b_ffn_down")
COL_SHARDED = ("a_w_in", "a_ffn_up", "w_kv_shared", "b_ffn_up")
FIRST_NEEDED = ("a_w_in", "a_w_mem_kv")
GATHER_BEHIND = {"sb": ("a_w_out", "a_ffn_up", "a_ffn_down", "b_ffn_up"), "up": ("b_ffn_down", "w_kv_shared"),
                 "conv": ("b_w_in", "b_w_out", "b_w_mem_kv")}
SMALL_SHARDED = ("a_norm_attn", "a_norm_mem", "a_norm_ffn", "a_ffn_conv", "b_ffn_conv")
SMALL_REPLICATED = ("kv_norm", "b_norm_attn", "b_norm_mem", "b_norm_ffn", "final_norm")
WEIGHTS = ("a_norm_attn", "a_w_in", "a_w_out", "a_norm_mem", "a_w_mem_kv", "a_norm_ffn", "a_ffn_up", "a_ffn_conv",
           "a_ffn_down", "kv_norm", "w_kv_shared", "b_norm_attn", "b_w_in", "b_w_out", "b_norm_mem", "b_w_mem_kv",
           "b_norm_ffn", "b_ffn_up", "b_ffn_conv", "b_ffn_down", "final_norm")


def _two_d(a):
    if a.ndim == 1:
        return a.reshape(1, -1)
    return a.reshape(a.shape[-2], a.shape[-1])


def kernel(x, mem, a_norm_attn, a_w_in, a_w_out, a_norm_mem, a_w_mem_kv, a_norm_ffn, a_ffn_up, a_ffn_conv, a_ffn_down, kv_norm, w_kv_shared, b_norm_attn, b_w_in, b_w_out, b_norm_mem, b_w_mem_kv, b_norm_ffn, b_ffn_up, b_ffn_conv, b_ffn_down, final_norm, loss_target, m_a_norm_attn, m_a_w_in, m_a_w_out, m_a_norm_mem, m_a_w_mem_kv, m_a_norm_ffn, m_a_ffn_up, m_a_ffn_conv, m_a_ffn_down, m_kv_norm, m_w_kv_shared, m_b_norm_attn, m_b_w_in, m_b_w_out, m_b_norm_mem, m_b_w_mem_kv, m_b_norm_ffn, m_b_ffn_up, m_b_ffn_conv, m_b_ffn_down, m_final_norm, v_a_norm_attn, v_a_w_in, v_a_w_out, v_a_norm_mem, v_a_w_mem_kv, v_a_norm_ffn, v_a_ffn_up, v_a_ffn_conv, v_a_ffn_down, v_kv_norm, v_w_kv_shared, v_b_norm_attn, v_b_w_in, v_b_w_out, v_b_norm_mem, v_b_w_mem_kv, v_b_norm_ffn, v_b_ffn_up, v_b_ffn_conv, v_b_ffn_down, v_final_norm):
    given = dict(locals())
    wl = {n: _two_d(given[n]) for n in WEIGHTS}
    ml = {n: _two_d(given["m_" + n]) for n in WEIGHTS}
    vl = {n: _two_d(given["v_" + n]) for n in WEIGHTS}
    chip = 2 * lax.axis_index("x") + lax.axis_index("y")

    packed = jnp.concatenate([wl[n].reshape(-1, LANES) for n in SMALL_SHARDED], axis=0)
    axis_of = lambda n: 1 if n in COL_SHARDED else 0
    late = tuple(n for n in MATRICES if n not in FIRST_NEEDED)
    full = gather_ride([wl[n].astype(BF) for n in FIRST_NEEDED] + [packed],
                       [axis_of(n) for n in FIRST_NEEDED] + [0]).run("gather_first")
    w = dict(zip(FIRST_NEEDED, full[:-1]))
    late_gather = {host: (gather_ride([wl[n].astype(BF) for n in names], [axis_of(n) for n in names]), names)
                   for host, names in GATHER_BEHIND.items()}
    rows = packed.shape[0]
    per_chip = full[-1].reshape(N_CHIPS, rows, LANES)
    r0 = 0
    for n in SMALL_SHARDED:
        nr = wl[n].size // LANES
        piece = per_chip[:, r0:r0 + nr].reshape(N_CHIPS, wl[n].shape[0], wl[n].shape[1])
        w[n] = jnp.concatenate([piece[q] for q in range(N_CHIPS)], axis=1)
        r0 += nr
    for n in SMALL_REPLICATED:
        w[n] = wl[n]

    shard_of = {n: wl[n].shape[1] for n in COL_SHARDED}
    def core_sums(g, landed):
        return {n: sum4(lax.dynamic_index_in_dim(_shard_major(n, g[n]), chip, 0, keepdims=False), landed[n],
                        name=f"sum4_{n}") for n in landed}

    lossvec, dx0, g, (sums, theirs, landed_last) = local_step(
        x, mem, loss_target, w, shard_of, late_gather=late_gather, early_scatter=late, scatter_last=FIRST_NEEDED,
        reduce_early=core_sums)
    loss = lax.psum(0.5 * jnp.sum(lossvec) / x.shape[-1], ("x", "y", "c"))
    sums.update(core_sums(g, landed_last))
    theirs.update(zip(FIRST_NEEDED, swap_ride([sums[n] for n in FIRST_NEEDED]).run("swap_last")))
    out = {}
    for n in MATRICES:
        out[n] = adamw(wl[n], sums[n], theirs[n], ml[n], vl[n], name=f"adamw_{n}")

    small = SMALL_SHARDED + SMALL_REPLICATED
    flat = jnp.concatenate([g[n].reshape(-1, LANES) for n in small], axis=0)
    tot = all_reduce_small(flat, name="all_reduce_small")
    r0 = 0
    for n in small:
        nr = g[n].size // LANES
        gn = tot[r0:r0 + nr].reshape(g[n].shape)
        r0 += nr
        if n in SMALL_SHARDED:
            gn = lax.dynamic_slice_in_dim(gn, chip * wl[n].shape[1], wl[n].shape[1], axis=1)
        out[n] = adamw(wl[n], gn, None, ml[n], vl[n], name=f"adamw_{n}")

    res = [loss, dx0.reshape(x.shape)]
    for slot in range(4):
        res += [out[n][slot].reshape(given[n].shape) for n in WEIGHTS]
    return tuple(res)
```
